```python
import jax, jax.numpy as jnp
from jax import lax
import numpy as np

D_MODEL = 1024
BATCH = 8
SEQ = 8192
DEPTH = 1

N_META = 16
BLOCK_Q = 128
META_PAD = BLOCK_Q - N_META
ATTN_HEADS = 8
HEAD_DIM = 64
ATTN_WIDTH = ATTN_HEADS * HEAD_DIM
CONV_GROUPS = 8
CONV_WIDTH = 512
CONV_K = 3
D_FF = 2816
NORM_EPS = 1e-6
IN_SPLITS = (ATTN_WIDTH, ATTN_WIDTH, ATTN_WIDTH, ATTN_HEADS,
             CONV_WIDTH, CONV_WIDTH, CONV_WIDTH, D_MODEL, D_MODEL)
IN_COLS = ATTN_WIDTH * 3 + ATTN_HEADS + CONV_WIDTH * 3 + D_MODEL * 2

kernel_name = "hybrid_fox_shortconv_macaron_layer"


def rms_norm(x, g):
    xf = x.astype(jnp.float32)
    y = xf * lax.rsqrt(jnp.mean(xf * xf, axis=-1, keepdims=True) + NORM_EPS)
    return (y * g.astype(jnp.float32)).astype(x.dtype)


def swiglu(x, w_in, w_out):
    a, b = jnp.split(x @ w_in, 2, axis=-1)
    return (jax.nn.silu(a) * b) @ w_out


def split_cols(z):
    idx, acc = [], 0
    for s in IN_SPLITS[:-1]:
        acc += s
        idx.append(acc)
    return jnp.split(z, idx, axis=-1)


def forgetting_attention(q, k, v, log_f):
    B, L, H, Dh = q.shape
    F = jnp.cumsum(log_f, axis=1)
    n_blocks = (L + META_PAD) // BLOCK_Q
    q_blocks = jnp.pad(q, ((0, 0), (META_PAD, 0), (0, 0), (0, 0)))
    q_blocks = q_blocks.reshape(B, n_blocks, BLOCK_Q, H, Dh).transpose(1, 0, 2, 3, 4)
    fq_blocks = jnp.pad(F, ((0, 0), (META_PAD, 0), (0, 0)))
    fq_blocks = fq_blocks.reshape(B, n_blocks, BLOCK_Q, H).transpose(1, 0, 3, 2)
    f_k = F.transpose(0, 2, 1)
    k_pos = jnp.arange(L)
    scale = HEAD_DIM ** -0.5

    def one_block(args):
        blk, qb, fqb = args
        q_pos = blk * BLOCK_Q + jnp.arange(BLOCK_Q) - META_PAD
        s = jnp.einsum('bqhd,bkhd->bhqk', qb, k).astype(jnp.float32) * scale
        s = s + fqb[..., None] - f_k[:, :, None, :]
        mask = k_pos[None, :] <= jnp.maximum(q_pos, 0)[:, None]
        s = jnp.where(mask[None, None], s, -jnp.inf)
        p = jax.nn.softmax(s, axis=-1).astype(v.dtype)
        return jnp.einsum('bhqk,bkhd->bqhd', p, v)

    out = lax.map(one_block, (jnp.arange(n_blocks), q_blocks, fq_blocks))
    out = out.transpose(1, 0, 2, 3, 4).reshape(B, n_blocks * BLOCK_Q, H, Dh)
    return out[:, META_PAD:]


def short_conv(u, w):
    L = u.shape[1]
    up = jnp.pad(u, ((0, 0), (CONV_K - 1, 0), (0, 0)))
    y = up[:, 0:L] * w[0]
    for j in range(1, CONV_K):
        y = y + up[:, j:j + L] * w[j]
    return y


def hybrid_layer(h, w_in, b_forget, conv_w, w_attn_branch, w_conv_branch, w_out,
                 g_ffn1_pre, g_ffn1_post, w_ffn1_in, w_ffn1_out,
                 g_mix_pre, g_mix_post, g_ffn2_pre, g_ffn2_post, w_ffn2_in, w_ffn2_out):
    B, L, _ = h.shape
    h = h + 0.5 * rms_norm(swiglu(rms_norm(h, g_ffn1_pre), w_ffn1_in, w_ffn1_out), g_ffn1_post)
    u = rms_norm(h, g_mix_pre)
    q, k, v, f_logit, c_b, c_c, c_in, gate_a, gate_c = split_cols(u @ w_in)
    q = q.reshape(B, L, ATTN_HEADS, HEAD_DIM)
    k = k.reshape(B, L, ATTN_HEADS, HEAD_DIM)
    v = v.reshape(B, L, ATTN_HEADS, HEAD_DIM)
    log_f = jax.nn.log_sigmoid((f_logit + b_forget).astype(jnp.float32))
    y_attn = forgetting_attention(q, k, v, log_f).reshape(B, L, ATTN_WIDTH) @ w_attn_branch
    y_conv = (c_b * short_conv(c_c * c_in, conv_w)) @ w_conv_branch
    mixed = (jax.nn.sigmoid(gate_a) * y_attn + jax.nn.sigmoid(gate_c) * y_conv) @ w_out
    h = h + rms_norm(mixed, g_mix_post)
    h = h + 0.5 * rms_norm(swiglu(rms_norm(h, g_ffn2_pre), w_ffn2_in, w_ffn2_out), g_ffn2_post)
    return h


def _fwd_setup_inputs(seed: int = 0) -> dict:
    key = jax.random.key(seed)
    ks = jax.random.split(key, 24)
    nrm = lambda k, shape, scale: jax.random.normal(k, shape, jnp.float32) * scale
    gain = lambda k: 1.0 + 0.05 * jax.random.normal(k, (DEPTH, D_MODEL), jnp.float32)
    return {
        "x": nrm(ks[0], (BATCH, SEQ, D_MODEL), 1.0),
        "meta_tokens": nrm(ks[1], (N_META, D_MODEL), 1.0),
        "w_in": nrm(ks[2], (DEPTH, D_MODEL, IN_COLS), D_MODEL ** -0.5),
        "b_forget": nrm(ks[3], (DEPTH, ATTN_HEADS), 0.1),
        "conv_w": nrm(ks[4], (DEPTH, CONV_K, CONV_WIDTH), CONV_K ** -0.5),
        "w_attn_branch": nrm(ks[5], (DEPTH, ATTN_WIDTH, D_MODEL), ATTN_WIDTH ** -0.5),
        "w_conv_branch": nrm(ks[6], (DEPTH, CONV_WIDTH, D_MODEL), CONV_WIDTH ** -0.5),
        "w_out": nrm(ks[7], (DEPTH, D_MODEL, D_MODEL), D_MODEL ** -0.5),
        "g_ffn1_pre": gain(ks[8]),
        "g_ffn1_post": gain(ks[9]),
        "w_ffn1_in": nrm(ks[10], (DEPTH, D_MODEL, 2 * D_FF), D_MODEL ** -0.5),
        "w_ffn1_out": nrm(ks[11], (DEPTH, D_FF, D_MODEL), D_FF ** -0.5),
        "g_mix_pre": gain(ks[12]),
        "g_mix_post": gain(ks[13]),
        "g_ffn2_pre": gain(ks[14]),
        "g_ffn2_post": gain(ks[15]),
        "w_ffn2_in": nrm(ks[16], (DEPTH, D_MODEL, 2 * D_FF), D_MODEL ** -0.5),
        "w_ffn2_out": nrm(ks[17], (DEPTH, D_FF, D_MODEL), D_FF ** -0.5),
    }


def _fwd_reference(x, meta_tokens, w_in, b_forget, conv_w, w_attn_branch, w_conv_branch, w_out,
              g_ffn1_pre, g_ffn1_post, w_ffn1_in, w_ffn1_out,
              g_mix_pre, g_mix_post, g_ffn2_pre, g_ffn2_post, w_ffn2_in, w_ffn2_out):
    B = x.shape[0]
    meta = jnp.broadcast_to(meta_tokens.astype(x.dtype)[None], (B, N_META, x.shape[-1]))
    h = jnp.concatenate([meta, x], axis=1)
    for l in range(DEPTH):
        h = hybrid_layer(h, w_in[l], b_forget[l], conv_w[l], w_attn_branch[l], w_conv_branch[l], w_out[l],
                         g_ffn1_pre[l], g_ffn1_post[l], w_ffn1_in[l], w_ffn1_out[l],
                         g_mix_pre[l], g_mix_post[l], g_ffn2_pre[l], g_ffn2_post[l],
                         w_ffn2_in[l], w_ffn2_out[l])
    return h[:, N_META:]


import jax as _jax
import jax.numpy as _jnp

TWIN_FORMAT = 'train_step'
FWD_PARAMS = ['x', 'meta_tokens', 'w_in', 'b_forget', 'conv_w', 'w_attn_branch', 'w_conv_branch', 'w_out', 'g_ffn1_pre', 'g_ffn1_post', 'w_ffn1_in', 'w_ffn1_out', 'g_mix_pre', 'g_mix_post', 'g_ffn2_pre', 'g_ffn2_post', 'w_ffn2_in', 'w_ffn2_out']
TWIN_WEIGHTS = ['meta_tokens', 'w_in', 'b_forget', 'conv_w', 'w_attn_branch', 'w_conv_branch', 'w_out', 'g_ffn1_pre', 'g_ffn1_post', 'w_ffn1_in', 'w_ffn1_out', 'g_mix_pre', 'g_mix_post', 'g_ffn2_pre', 'g_ffn2_post', 'w_ffn2_in', 'w_ffn2_out']
TWIN_DIFF_INPUT = 'x'
TWIN_INPUTS = ['x', 'meta_tokens', 'w_in', 'b_forget', 'conv_w', 'w_attn_branch', 'w_conv_branch', 'w_out', 'g_ffn1_pre', 'g_ffn1_post', 'w_ffn1_in', 'w_ffn1_out', 'g_mix_pre', 'g_mix_post', 'g_ffn2_pre', 'g_ffn2_post', 'w_ffn2_in', 'w_ffn2_out', 'loss_target', 'm_meta_tokens', 'm_w_in', 'm_b_forget', 'm_conv_w', 'm_w_attn_branch', 'm_w_conv_branch', 'm_w_out', 'm_g_ffn1_pre', 'm_g_ffn1_post', 'm_w_ffn1_in', 'm_w_ffn1_out', 'm_g_mix_pre', 'm_g_mix_post', 'm_g_ffn2_pre', 'm_g_ffn2_post', 'm_w_ffn2_in', 'm_w_ffn2_out', 'v_meta_tokens', 'v_w_in', 'v_b_forget', 'v_conv_w', 'v_w_attn_branch', 'v_w_conv_branch', 'v_w_out', 'v_g_ffn1_pre', 'v_g_ffn1_post', 'v_w_ffn1_in', 'v_w_ffn1_out', 'v_g_mix_pre', 'v_g_mix_post', 'v_g_ffn2_pre', 'v_g_ffn2_post', 'v_w_ffn2_in', 'v_w_ffn2_out']
TWIN_OUTPUTS = ['loss', 'grad_x', 'grad_meta_tokens', 'grad_w_in', 'grad_b_forget', 'grad_conv_w', 'grad_w_attn_branch', 'grad_w_conv_branch', 'grad_w_out', 'grad_g_ffn1_pre', 'grad_g_ffn1_post', 'grad_w_ffn1_in', 'grad_w_ffn1_out', 'grad_g_mix_pre', 'grad_g_mix_post', 'grad_g_ffn2_pre', 'grad_g_ffn2_post', 'grad_w_ffn2_in', 'grad_w_ffn2_out', 'delta_meta_tokens', 'delta_w_in', 'delta_b_forget', 'delta_conv_w', 'delta_w_attn_branch', 'delta_w_conv_branch', 'delta_w_out', 'delta_g_ffn1_pre', 'delta_g_ffn1_post', 'delta_w_ffn1_in', 'delta_w_ffn1_out', 'delta_g_mix_pre', 'delta_g_mix_post', 'delta_g_ffn2_pre', 'delta_g_ffn2_post', 'delta_w_ffn2_in', 'delta_w_ffn2_out', 'new_m_meta_tokens', 'new_m_w_in', 'new_m_b_forget', 'new_m_conv_w', 'new_m_w_attn_branch', 'new_m_w_conv_branch', 'new_m_w_out', 'new_m_g_ffn1_pre', 'new_m_g_ffn1_post', 'new_m_w_ffn1_in', 'new_m_w_ffn1_out', 'new_m_g_mix_pre', 'new_m_g_mix_post', 'new_m_g_ffn2_pre', 'new_m_g_ffn2_post', 'new_m_w_ffn2_in', 'new_m_w_ffn2_out', 'new_v_meta_tokens', 'new_v_w_in', 'new_v_b_forget', 'new_v_conv_w', 'new_v_w_attn_branch', 'new_v_w_conv_branch', 'new_v_w_out', 'new_v_g_ffn1_pre', 'new_v_g_ffn1_post', 'new_v_w_ffn1_in', 'new_v_w_ffn1_out', 'new_v_g_mix_pre', 'new_v_g_mix_post', 'new_v_g_ffn2_pre', 'new_v_g_ffn2_post', 'new_v_w_ffn2_in', 'new_v_w_ffn2_out']
TWIN_LEAF_KINDS = {'loss': 'loss', 'grad_x': 'grad_x', 'grad_meta_tokens': 'grad_w', 'grad_w_in': 'grad_w', 'grad_b_forget': 'grad_w', 'grad_conv_w': 'grad_w', 'grad_w_attn_branch': 'grad_w', 'grad_w_conv_branch': 'grad_w', 'grad_w_out': 'grad_w', 'grad_g_ffn1_pre': 'grad_w', 'grad_g_ffn1_post': 'grad_w', 'grad_w_ffn1_in': 'grad_w', 'grad_w_ffn1_out': 'grad_w', 'grad_g_mix_pre': 'grad_w', 'grad_g_mix_post': 'grad_w', 'grad_g_ffn2_pre': 'grad_w', 'grad_g_ffn2_post': 'grad_w', 'grad_w_ffn2_in': 'grad_w', 'grad_w_ffn2_out': 'grad_w', 'delta_meta_tokens': 'delta_w', 'delta_w_in': 'delta_w', 'delta_b_forget': 'delta_w', 'delta_conv_w': 'delta_w', 'delta_w_attn_branch': 'delta_w', 'delta_w_conv_branch': 'delta_w', 'delta_w_out': 'delta_w', 'delta_g_ffn1_pre': 'delta_w', 'delta_g_ffn1_post': 'delta_w', 'delta_w_ffn1_in': 'delta_w', 'delta_w_ffn1_out': 'delta_w', 'delta_g_mix_pre': 'delta_w', 'delta_g_mix_post': 'delta_w', 'delta_g_ffn2_pre': 'delta_w', 'delta_g_ffn2_post': 'delta_w', 'delta_w_ffn2_in': 'delta_w', 'delta_w_ffn2_out': 'delta_w', 'new_m_meta_tokens': 'new_m', 'new_m_w_in': 'new_m', 'new_m_b_forget': 'new_m', 'new_m_conv_w': 'new_m', 'new_m_w_attn_branch': 'new_m', 'new_m_w_conv_branch': 'new_m', 'new_m_w_out': 'new_m', 'new_m_g_ffn1_pre': 'new_m', 'new_m_g_ffn1_post': 'new_m', 'new_m_w_ffn1_in': 'new_m', 'new_m_w_ffn1_out': 'new_m', 'new_m_g_mix_pre': 'new_m', 'new_m_g_mix_post': 'new_m', 'new_m_g_ffn2_pre': 'new_m', 'new_m_g_ffn2_post': 'new_m', 'new_m_w_ffn2_in': 'new_m', 'new_m_w_ffn2_out': 'new_m', 'new_v_meta_tokens': 'new_v', 'new_v_w_in': 'new_v', 'new_v_b_forget': 'new_v', 'new_v_conv_w': 'new_v', 'new_v_w_attn_branch': 'new_v', 'new_v_w_conv_branch': 'new_v', 'new_v_w_out': 'new_v', 'new_v_g_ffn1_pre': 'new_v', 'new_v_g_ffn1_post': 'new_v', 'new_v_w_ffn1_in': 'new_v', 'new_v_w_ffn1_out': 'new_v', 'new_v_g_mix_pre': 'new_v', 'new_v_g_mix_post': 'new_v', 'new_v_g_ffn2_pre': 'new_v', 'new_v_g_ffn2_post': 'new_v', 'new_v_w_ffn2_in': 'new_v', 'new_v_w_ffn2_out': 'new_v'}


def _forward(args):
    return _fwd_reference(*[args[k] for k in FWD_PARAMS])


def _output_shape():
    def fwd():
        inp = _fwd_setup_inputs(0)
        return _fwd_reference(*[inp[k] for k in FWD_PARAMS])
    out = _jax.eval_shape(fwd)
    return out.shape, out.dtype

N_MICROBATCH = 1
ADAM_LR = 0.001
ADAM_B1 = 0.9
ADAM_B2 = 0.999
ADAM_EPS = 1e-08
ADAM_WD = 0.01
ADAM_STEP = 10
PER_EXAMPLE_BATCH_AXIS = {'x': 0, 'loss_target': 0}
SHARED_INPUTS = []
_WEIGHT_DTYPES = {'meta_tokens': _jnp.float32, 'w_in': _jnp.float32, 'b_forget': _jnp.float32, 'conv_w': _jnp.float32, 'w_attn_branch': _jnp.float32, 'w_conv_branch': _jnp.float32, 'w_out': _jnp.float32, 'g_ffn1_pre': _jnp.float32, 'g_ffn1_post': _jnp.float32, 'w_ffn1_in': _jnp.float32, 'w_ffn1_out': _jnp.float32, 'g_mix_pre': _jnp.float32, 'g_mix_post': _jnp.float32, 'g_ffn2_pre': _jnp.float32, 'g_ffn2_post': _jnp.float32, 'w_ffn2_in': _jnp.float32, 'w_ffn2_out': _jnp.float32}
MOMENT_SCALE = {'meta_tokens': 9.752215e-03, 'w_in': 3.953286e-01, 'b_forget': 5.001321e+00, 'conv_w': 5.654013e-01, 'w_attn_branch': 6.203651e-01, 'w_conv_branch': 4.290248e-01, 'w_out': 7.530248e-01, 'g_ffn1_pre': 6.511802e-01, 'g_ffn1_post': 1.596426e+01, 'w_ffn1_in': 2.715144e-01, 'w_ffn1_out': 4.751856e-01, 'g_mix_pre': 9.306900e-01, 'g_mix_post': 6.432219e+01, 'g_ffn2_pre': 5.050397e-01, 'g_ffn2_post': 1.595449e+01, 'w_ffn2_in': 2.119294e-01, 'w_ffn2_out': 4.622750e-01}


def _to_microbatches(a, axis):
    t = _jnp.moveaxis(a, axis, 0)
    t = t.reshape((N_MICROBATCH, t.shape[0] // N_MICROBATCH) + t.shape[1:])
    return _jnp.moveaxis(t, 1, axis + 1)


def setup_inputs(seed: int = 0) -> dict:
    inp = _fwd_setup_inputs(seed)
    key = _jax.random.fold_in(_jax.random.key(seed), 7919)
    shape, _ = _output_shape()
    out = dict(inp)
    out["loss_target"] = _jax.random.normal(_jax.random.fold_in(key, 0), shape, _jnp.float32)
    for i, name in enumerate(TWIN_WEIGHTS):
        w = inp[name].astype(_jnp.float32)
        if MOMENT_SCALE is None:
            s = _jnp.sqrt(_jnp.mean(_jnp.square(w)) + 1e-30)
        else:
            s = MOMENT_SCALE[name]
        km, kv = _jax.random.split(_jax.random.fold_in(key, i + 1))
        out[name] = w
        out["m_" + name] = s * _jax.random.normal(km, w.shape, _jnp.float32)
        out["v_" + name] = (s * s) * _jax.random.uniform(kv, w.shape, _jnp.float32, 0.5, 1.5)
    if N_MICROBATCH > 1:
        for name, axis in PER_EXAMPLE_BATCH_AXIS.items():
            out[name] = _to_microbatches(out[name], axis)
    return {'x': out['x'], 'meta_tokens': out['meta_tokens'], 'w_in': out['w_in'], 'b_forget': out['b_forget'], 'conv_w': out['conv_w'], 'w_attn_branch': out['w_attn_branch'], 'w_conv_branch': out['w_conv_branch'], 'w_out': out['w_out'], 'g_ffn1_pre': out['g_ffn1_pre'], 'g_ffn1_post': out['g_ffn1_post'], 'w_ffn1_in': out['w_ffn1_in'], 'w_ffn1_out': out['w_ffn1_out'], 'g_mix_pre': out['g_mix_pre'], 'g_mix_post': out['g_mix_post'], 'g_ffn2_pre': out['g_ffn2_pre'], 'g_ffn2_post': out['g_ffn2_post'], 'w_ffn2_in': out['w_ffn2_in'], 'w_ffn2_out': out['w_ffn2_out'], 'loss_target': out['loss_target'], 'm_meta_tokens': out['m_meta_tokens'], 'm_w_in': out['m_w_in'], 'm_b_forget': out['m_b_forget'], 'm_conv_w': out['m_conv_w'], 'm_w_attn_branch': out['m_w_attn_branch'], 'm_w_conv_branch': out['m_w_conv_branch'], 'm_w_out': out['m_w_out'], 'm_g_ffn1_pre': out['m_g_ffn1_pre'], 'm_g_ffn1_post': out['m_g_ffn1_post'], 'm_w_ffn1_in': out['m_w_ffn1_in'], 'm_w_ffn1_out': out['m_w_ffn1_out'], 'm_g_mix_pre': out['m_g_mix_pre'], 'm_g_mix_post': out['m_g_mix_post'], 'm_g_ffn2_pre': out['m_g_ffn2_pre'], 'm_g_ffn2_post': out['m_g_ffn2_post'], 'm_w_ffn2_in': out['m_w_ffn2_in'], 'm_w_ffn2_out': out['m_w_ffn2_out'], 'v_meta_tokens': out['v_meta_tokens'], 'v_w_in': out['v_w_in'], 'v_b_forget': out['v_b_forget'], 'v_conv_w': out['v_conv_w'], 'v_w_attn_branch': out['v_w_attn_branch'], 'v_w_conv_branch': out['v_w_conv_branch'], 'v_w_out': out['v_w_out'], 'v_g_ffn1_pre': out['v_g_ffn1_pre'], 'v_g_ffn1_post': out['v_g_ffn1_post'], 'v_w_ffn1_in': out['v_w_ffn1_in'], 'v_w_ffn1_out': out['v_w_ffn1_out'], 'v_g_mix_pre': out['v_g_mix_pre'], 'v_g_mix_post': out['v_g_mix_post'], 'v_g_ffn2_pre': out['v_g_ffn2_pre'], 'v_g_ffn2_post': out['v_g_ffn2_post'], 'v_w_ffn2_in': out['v_w_ffn2_in'], 'v_w_ffn2_out': out['v_w_ffn2_out']}


def _loss(weights, diff, rest, loss_target):
    with _jax.named_scope("forward"):
        args = {**rest, TWIN_DIFF_INPUT: diff, **{k: w.astype(_WEIGHT_DTYPES[k]) for k, w in weights.items()}}
        y = _forward(args)
    with _jax.named_scope("loss_head"):
        err = _jnp.square(y.astype(_jnp.float32) - loss_target)
        return 0.5 * _jnp.sum(_jnp.mean(err, axis=-1)) if err.ndim else 0.5 * err


def _adamw(w, g, m, v):
    m = ADAM_B1 * m + (1.0 - ADAM_B1) * g
    v = ADAM_B2 * v + (1.0 - ADAM_B2) * _jnp.square(g)
    m_hat = m / (1.0 - ADAM_B1 ** ADAM_STEP)
    v_hat = v / (1.0 - ADAM_B2 ** ADAM_STEP)
    delta = -ADAM_LR * (m_hat / (_jnp.sqrt(v_hat) + ADAM_EPS) + ADAM_WD * w)
    return delta, m, v


def reference(x, meta_tokens, w_in, b_forget, conv_w, w_attn_branch, w_conv_branch, w_out, g_ffn1_pre, g_ffn1_post, w_ffn1_in, w_ffn1_out, g_mix_pre, g_mix_post, g_ffn2_pre, g_ffn2_post, w_ffn2_in, w_ffn2_out, loss_target, m_meta_tokens, m_w_in, m_b_forget, m_conv_w, m_w_attn_branch, m_w_conv_branch, m_w_out, m_g_ffn1_pre, m_g_ffn1_post, m_w_ffn1_in, m_w_ffn1_out, m_g_mix_pre, m_g_mix_post, m_g_ffn2_pre, m_g_ffn2_post, m_w_ffn2_in, m_w_ffn2_out, v_meta_tokens, v_w_in, v_b_forget, v_conv_w, v_w_attn_branch, v_w_conv_branch, v_w_out, v_g_ffn1_pre, v_g_ffn1_post, v_w_ffn1_in, v_w_ffn1_out, v_g_mix_pre, v_g_mix_post, v_g_ffn2_pre, v_g_ffn2_post, v_w_ffn2_in, v_w_ffn2_out):
    given = dict(x=x, meta_tokens=meta_tokens, w_in=w_in, b_forget=b_forget, conv_w=conv_w, w_attn_branch=w_attn_branch, w_conv_branch=w_conv_branch, w_out=w_out, g_ffn1_pre=g_ffn1_pre, g_ffn1_post=g_ffn1_post, w_ffn1_in=w_ffn1_in, w_ffn1_out=w_ffn1_out, g_mix_pre=g_mix_pre, g_mix_post=g_mix_post, g_ffn2_pre=g_ffn2_pre, g_ffn2_post=g_ffn2_post, w_ffn2_in=w_ffn2_in, w_ffn2_out=w_ffn2_out, loss_target=loss_target, m_meta_tokens=m_meta_tokens, m_w_in=m_w_in, m_b_forget=m_b_forget, m_conv_w=m_conv_w, m_w_attn_branch=m_w_attn_branch, m_w_conv_branch=m_w_conv_branch, m_w_out=m_w_out, m_g_ffn1_pre=m_g_ffn1_pre, m_g_ffn1_post=m_g_ffn1_post, m_w_ffn1_in=m_w_ffn1_in, m_w_ffn1_out=m_w_ffn1_out, m_g_mix_pre=m_g_mix_pre, m_g_mix_post=m_g_mix_post, m_g_ffn2_pre=m_g_ffn2_pre, m_g_ffn2_post=m_g_ffn2_post, m_w_ffn2_in=m_w_ffn2_in, m_w_ffn2_out=m_w_ffn2_out, v_meta_tokens=v_meta_tokens, v_w_in=v_w_in, v_b_forget=v_b_forget, v_conv_w=v_conv_w, v_w_attn_branch=v_w_attn_branch, v_w_conv_branch=v_w_conv_branch, v_w_out=v_w_out, v_g_ffn1_pre=v_g_ffn1_pre, v_g_ffn1_post=v_g_ffn1_post, v_w_ffn1_in=v_w_ffn1_in, v_w_ffn1_out=v_w_ffn1_out, v_g_mix_pre=v_g_mix_pre, v_g_mix_post=v_g_mix_post, v_g_ffn2_pre=v_g_ffn2_pre, v_g_ffn2_post=v_g_ffn2_post, v_w_ffn2_in=v_w_ffn2_in, v_w_ffn2_out=v_w_ffn2_out)
    weights = {n: given[n] for n in TWIN_WEIGHTS}
    shared = {n: given[n] for n in SHARED_INPUTS}
    per_example = {n: given[n] for n in ['x']}
    grad_fn = _jax.value_and_grad(_loss, argnums=(0, 1))

    def one_microbatch(ex, loss_target):
        ex = dict(ex)
        diff = ex.pop(TWIN_DIFF_INPUT)
        return grad_fn(weights, diff, {**shared, **ex}, loss_target)

    if N_MICROBATCH == 1:
        loss, (grad_w, grad_x) = one_microbatch(per_example, given["loss_target"])
    else:
        def body(carry, xs):
            loss_sum, grad_sum = carry
            l_k, (gw_k, gx_k) = one_microbatch(xs[0], xs[1])
            with _jax.named_scope("update"):
                return (loss_sum + l_k, _jax.tree.map(_jnp.add, grad_sum, gw_k)), gx_k

        init = (_jnp.zeros((), _jnp.float32), _jax.tree.map(_jnp.zeros_like, weights))
        (loss, grad_w), grad_x = _jax.lax.scan(body, init, (per_example, given["loss_target"]))
    with _jax.named_scope("update"):
        delta_w, new_m, new_v = {}, {}, {}
        for n in TWIN_WEIGHTS:
            delta_w[n], new_m[n], new_v[n] = _adamw(weights[n], grad_w[n], given["m_" + n], given["v_" + n])
    return (loss, grad_x, *[grad_w[n] for n in TWIN_WEIGHTS], *[delta_w[n] for n in TWIN_WEIGHTS],
            *[new_m[n] for n in TWIN_WEIGHTS], *[new_v[n] for n in TWIN_WEIGHTS])
```

```python
import functools

import jax
import jax.numpy as jnp
from jax import lax
from jax.experimental import pallas as pl
from jax.experimental.pallas import tpu as pltpu

N_META = 16
ROW_PAD = 112
N_FRONT = ROW_PAD + N_META
HEADS = 8
HEAD_DIM = 64
ATTN_W = HEADS * HEAD_DIM
CONV_W = 512
NORM_EPS = 1e-6
ROW_TILE = 640
F_PAD = 512
NEG = -1e30
ADAM_LR = 0.001
ADAM_B1 = 0.9
ADAM_B2 = 0.999
ADAM_EPS = 1e-08
ADAM_WD = 0.01
ADAM_STEP = 10
VMEM_BIG = 56 * 1024 * 1024
MESH = pl.DeviceIdType.MESH
ANY = pl.BlockSpec(memory_space=pl.ANY)
F32 = jnp.float32
BF16 = jnp.bfloat16


def _params(sem, vmem=None):
    return pltpu.CompilerParams(dimension_semantics=sem, vmem_limit_bytes=vmem)


def _sigmoid(x):
    return 1.0 / (1.0 + jnp.exp(-x))


def _rstd(x):
    return lax.rsqrt(jnp.mean(x * x, axis=-1, keepdims=True) + NORM_EPS)


def _rms_bwd(x, g, dy):
    r = _rstd(x)
    xr = x * r
    gdy = g * dy
    dx = r * (gdy - xr * jnp.mean(xr * gdy, axis=-1, keepdims=True))
    return dx, jnp.sum(dy * xr, axis=0, keepdims=True)


def _dot(a, b):
    return jnp.dot(a, b, preferred_element_type=F32)


def _dot_nt(a, b):
    return lax.dot_general(a, b, (((1,), (1,)), ((), ())), preferred_element_type=F32)


def _k_tile(t):
    return 1664 if t % 1664 == 0 else ROW_TILE


def _place():
    x, y, c = lax.axis_index("x"), lax.axis_index("y"), lax.axis_index("c")
    chips = [(1 - x, y), (x, 1 - y), (1 - x, 1 - y)]
    return x, y, c, chips


def _all_gather(shards):
    n = len(shards)

    def body(*refs):
        ins, outs = refs[:n], refs[n:2 * n]
        send_sems, recv_sems, local_sems = refs[2 * n:]
        x, y, c, chips = _place()
        me = 2 * x + y
        sibling = (x, y, 1 - c)

        def remote(i, k, slot, rows, to, src=None):
            dst = outs[i].at[slot, rows]
            return pltpu.make_async_remote_copy(
                src_ref=dst if src is None else src, dst_ref=dst,
                send_sem=send_sems.at[i, k], recv_sem=recv_sems.at[i, k],
                device_id=to, device_id_type=MESH)

        started = []
        local = []
        for i in range(n):
            half = ins[i].shape[0] // 2
            mine = pl.ds(c * half, half)
            cp = pltpu.make_async_copy(ins[i], outs[i].at[me], local_sems.at[i])
            cp.start()
            local.append(cp)
            for k, (cx, cy) in enumerate(chips):
                cp = remote(i, k, me, mine, (cx, cy, c), src=ins[i].at[mine])
                cp.start()
                started.append(cp)
        for i in range(n):
            half = ins[i].shape[0] // 2
            mine = pl.ds(c * half, half)
            for k, (cx, cy) in enumerate(chips):
                remote(i, k, 2 * cx + cy, mine, (x, y, c)).wait_recv()
                cp = remote(i, 3 + k, 2 * cx + cy, mine, sibling)
                cp.start()
                started.append(cp)
        for i in range(n):
            half = ins[i].shape[0] // 2
            theirs = pl.ds((1 - c) * half, half)
            for k, (cx, cy) in enumerate(chips):
                remote(i, 3 + k, 2 * cx + cy, theirs, (x, y, c)).wait_recv()
        for cp in started:
            cp.wait_send()
        for cp in local:
            cp.wait()

    return pl.pallas_call(
        body, name="all_gather_weights",
        out_shape=[jax.ShapeDtypeStruct((4,) + s.shape, s.dtype) for s in shards],
        in_specs=[ANY] * n, out_specs=[ANY] * n,
        scratch_shapes=[pltpu.SemaphoreType.DMA((n, 6)), pltpu.SemaphoreType.DMA((n, 6)),
                        pltpu.SemaphoreType.DMA((n,))],
    )(*shards)


def _pair_send_halves(grads):
    n = len(grads)

    def body(*refs):
        ins, outs = refs[:n], refs[n:2 * n]
        send_sems, recv_sems = refs[2 * n:]
        x, y, c, _ = _place()
        cps = []
        for i in range(n):
            half = ins[i].shape[1] // 2
            cp = pltpu.make_async_remote_copy(
                src_ref=ins[i].at[:, pl.ds((1 - c) * half, half)], dst_ref=outs[i],
                send_sem=send_sems.at[i], recv_sem=recv_sems.at[i],
                device_id=(x, y, 1 - c), device_id_type=MESH)
            cp.start()
            cps.append(cp)
        for cp in cps:
            cp.wait()

    return pl.pallas_call(
        body, name="grad_pair_exchange",
        out_shape=[jax.ShapeDtypeStruct((4, g.shape[1] // 2, g.shape[2]), g.dtype) for g in grads],
        in_specs=[ANY] * n, out_specs=[ANY] * n,
        scratch_shapes=[pltpu.SemaphoreType.DMA((n,)), pltpu.SemaphoreType.DMA((n,))],
    )(*grads)


def _chip_scatter(parts):
    n = len(parts)

    def body(*refs):
        ins, outs = refs[:n], refs[n:2 * n]
        send_sems, recv_sems, local_sems = refs[2 * n:]
        x, y, c, chips = _place()
        me = 2 * x + y
        sends, local = [], []
        for i in range(n):
            cp = pltpu.make_async_copy(ins[i].at[me], outs[i].at[me], local_sems.at[i])
            cp.start()
            local.append(cp)
            for k, (cx, cy) in enumerate(chips):
                cp = pltpu.make_async_remote_copy(
                    src_ref=ins[i].at[2 * cx + cy], dst_ref=outs[i].at[me],
                    send_sem=send_sems.at[i, k], recv_sem=recv_sems.at[i, k],
                    device_id=(cx, cy, c), device_id_type=MESH)
                cp.start()
                sends.append(cp)
        for i in range(n):
            for k, (cx, cy) in enumerate(chips):
                got = outs[i].at[2 * cx + cy]
                pltpu.make_async_remote_copy(
                    src_ref=got, dst_ref=got, send_sem=send_sems.at[i, k], recv_sem=recv_sems.at[i, k],
                    device_id=(x, y, c), device_id_type=MESH).wait_recv()
        for cp in sends:
            cp.wait_send()
        for cp in local:
            cp.wait()

    return pl.pallas_call(
        body, name="grad_chip_scatter",
        out_shape=[jax.ShapeDtypeStruct(p.shape, p.dtype) for p in parts],
        in_specs=[ANY] * n, out_specs=[ANY] * n,
        scratch_shapes=[pltpu.SemaphoreType.DMA((n, 3)), pltpu.SemaphoreType.DMA((n, 3)),
                        pltpu.SemaphoreType.DMA((n,))],
    )(*parts)


def _pair_join_halves(halves):
    n = len(halves)

    def body(*refs):
        ins, outs = refs[:n], refs[n:2 * n]
        send_sems, recv_sems, local_sems = refs[2 * n:]
        x, y, c, _ = _place()
        cps, local = [], []
        for i in range(n):
            half = ins[i].shape[0]
            mine = pl.ds(c * half, half)
            cp = pltpu.make_async_copy(ins[i], outs[i].at[mine], local_sems.at[i])
            cp.start()
            local.append(cp)
            cp = pltpu.make_async_remote_copy(
                src_ref=ins[i], dst_ref=outs[i].at[mine],
                send_sem=send_sems.at[i], recv_sem=recv_sems.at[i],
                device_id=(x, y, 1 - c), device_id_type=MESH)
            cp.start()
            cps.append(cp)
        for i in range(n):
            half = ins[i].shape[0]
            theirs = outs[i].at[pl.ds((1 - c) * half, half)]
            pltpu.make_async_remote_copy(
                src_ref=theirs, dst_ref=theirs, send_sem=send_sems.at[i], recv_sem=recv_sems.at[i],
                device_id=(x, y, c), device_id_type=MESH).wait_recv()
        for cp in cps:
            cp.wait_send()
        for cp in local:
            cp.wait()

    return pl.pallas_call(
        body, name="grad_pair_join",
        out_shape=[jax.ShapeDtypeStruct((2 * h.shape[0], h.shape[1]), h.dtype) for h in halves],
        in_specs=[ANY] * n, out_specs=[ANY] * n,
        scratch_shapes=[pltpu.SemaphoreType.DMA((n,)), pltpu.SemaphoreType.DMA((n,)),
                        pltpu.SemaphoreType.DMA((n,))],
    )(*halves)


def _row_block(rows, cols, n_bufs, budget=20 * 1024 * 1024):
    best = 8
    for b in range(8, rows + 1, 8):
        if rows % b == 0 and 2 * n_bufs * b * cols * 4 <= budget:
            best = b
    return best


def _pair_add(tag, grad, got, c_arr):
    _, rows, cols = grad.shape
    half = rows // 2
    bh = _row_block(half, cols, 3)
    nb = half // bh

    def body(c_ref, g_ref, a_ref, o_ref):
        o_ref[...] = g_ref[...] + a_ref[...]

    return pl.pallas_call(
        body, name=f"pair_add_{tag}",
        out_shape=jax.ShapeDtypeStruct((4, half, cols), F32),
        grid_spec=pltpu.PrefetchScalarGridSpec(
            num_scalar_prefetch=1, grid=(4, nb),
            in_specs=[pl.BlockSpec((None, bh, cols), lambda j, r, c: (j, c[0] * nb + r, 0)),
                      pl.BlockSpec((None, bh, cols), lambda j, r, c: (j, r, 0))],
            out_specs=pl.BlockSpec((None, bh, cols), lambda j, r, c: (j, r, 0))),
        compiler_params=_params(("parallel", "parallel")),
    )(c_arr, grad, got)


def _chip_add(tag, parts):
    _, half, cols = parts.shape
    bh = _row_block(half, cols, 5)

    def body(p_ref, o_ref):
        o_ref[...] = ((p_ref[0] + p_ref[1]) + p_ref[2]) + p_ref[3]

    return pl.pallas_call(
        body, name=f"chip_add_{tag}",
        out_shape=jax.ShapeDtypeStruct((half, cols), F32),
        grid=(half // bh,),
        in_specs=[pl.BlockSpec((4, bh, cols), lambda r: (0, r, 0))],
        out_specs=pl.BlockSpec((bh, cols), lambda r: (r, 0)),
        compiler_params=_params(("parallel",)),
    )(parts)


def _adamw(tag, w, g, m, v):
    rows, cols = w.shape
    br = _row_block(rows, cols, 7)

    def body(w_ref, g_ref, m_ref, v_ref, d_ref, mo_ref, vo_ref):
        g = g_ref[...]
        m_new = ADAM_B1 * m_ref[...] + (1.0 - ADAM_B1) * g
        v_new = ADAM_B2 * v_ref[...] + (1.0 - ADAM_B2) * (g * g)
        m_hat = m_new / (1.0 - ADAM_B1 ** ADAM_STEP)
        v_hat = v_new / (1.0 - ADAM_B2 ** ADAM_STEP)
        d_ref[...] = -ADAM_LR * (m_hat / (jnp.sqrt(v_hat) + ADAM_EPS) + ADAM_WD * w_ref[...])
        mo_ref[...] = m_new
        vo_ref[...] = v_new

    spec = pl.BlockSpec((br, cols), lambda r: (r, 0))
    return pl.pallas_call(
        body, name=f"adamw_{tag}",
        out_shape=[jax.ShapeDtypeStruct((rows, cols), F32)] * 3,
        grid=(rows // br,), in_specs=[spec] * 4, out_specs=[spec] * 3,
        compiler_params=_params(("parallel",)),
    )(w, g, m, v)


def _matmul(name, x, w, out_shape, grid, x_spec, w_spec, o_spec, *, nt=False, vmem=None):
    nk = grid[2]
    acc_shape = tuple(d for d in o_spec.block_shape if d is not None)

    def body(x_ref, w_ref, o_ref, acc_ref):
        k = pl.program_id(2)
        part = _dot_nt(x_ref[...], w_ref[...]) if nt else _dot(x_ref[...], w_ref[...])
        if nk == 1:
            o_ref[...] = part.astype(o_ref.dtype)
        else:
            @pl.when(k == 0)
            def _():
                acc_ref[...] = part

            @pl.when(k > 0)
            def _():
                acc_ref[...] += part

            @pl.when(k == nk - 1)
            def _():
                o_ref[...] = acc_ref[...].astype(o_ref.dtype)

    return pl.pallas_call(
        body, name=name, out_shape=out_shape, grid=grid,
        in_specs=[x_spec, w_spec], out_specs=o_spec,
        scratch_shapes=[pltpu.VMEM(acc_shape if nk > 1 else (8, 128), F32)],
        compiler_params=_params(("parallel", "parallel", "arbitrary"), vmem),
    )(x, w)


def _weight_grad(name, xt, dy, bn, out_rows=None):
    m, t = xt.shape
    n = dy.shape[1]
    bm = m if out_rows is None else out_rows
    bk = _k_tile(t)
    return _matmul(
        name, xt, dy, jax.ShapeDtypeStruct((m, n), F32), (m // bm, n // bn, t // bk),
        pl.BlockSpec((bm, bk), lambda a, b, k: (a, k)),
        pl.BlockSpec((bk, bn), lambda a, b, k: (k, b)),
        pl.BlockSpec((bm, bn), lambda a, b, k: (a, b)), vmem=VMEM_BIG)


def _norm_fwd(name, h, g):
    t, d = h.shape
    tm = ROW_TILE

    def body(h_ref, g_ref, n_ref, nt_ref):
        x = h_ref[...]
        y = x * _rstd(x) * g_ref[...]
        n_ref[...] = y.astype(BF16)
        nt_ref[...] = y.T.astype(BF16)

    return pl.pallas_call(
        body, name=name,
        out_shape=[jax.ShapeDtypeStruct((t, d), BF16), jax.ShapeDtypeStruct((d, t), BF16)],
        grid=(t // tm,),
        in_specs=[pl.BlockSpec((tm, d), lambda i: (i, 0)), pl.BlockSpec((1, d), lambda i: (0, 0))],
        out_specs=[pl.BlockSpec((tm, d), lambda i: (i, 0)), pl.BlockSpec((d, tm), lambda i: (0, i))],
        compiler_params=_params(("parallel",)),
    )(h, g)


def _slot_of(kk):
    return (kk % 2) * 2 + kk // 2


def _ffn_in(name, n, w4):
    t, d = n.shape
    cw = w4.shape[2]
    tm = ROW_TILE

    def body(x_ref, wg_ref, wu_ref, ab_ref, s_ref, st_ref):
        x = x_ref[...]
        a = _dot(x, wg_ref[...])
        b = _dot(x, wu_ref[...])
        ab_ref[:, :cw] = a
        ab_ref[:, cw:] = b
        s = a * _sigmoid(a) * b
        s_ref[...] = s.astype(BF16)
        st_ref[...] = s.T.astype(BF16)

    return pl.pallas_call(
        body, name=name,
        out_shape=[jax.ShapeDtypeStruct((t, 4 * cw), F32), jax.ShapeDtypeStruct((t, 2 * cw), BF16),
                   jax.ShapeDtypeStruct((2 * cw, t), BF16)],
        grid=(2, t // tm),
        in_specs=[pl.BlockSpec((tm, d), lambda j, i: (i, 0)),
                  pl.BlockSpec((None, d, cw), lambda j, i: (j, 0, 0)),
                  pl.BlockSpec((None, d, cw), lambda j, i: (2 + j, 0, 0))],
        out_specs=[pl.BlockSpec((tm, 2 * cw), lambda j, i: (i, j)),
                   pl.BlockSpec((tm, cw), lambda j, i: (i, j)),
                   pl.BlockSpec((cw, tm), lambda j, i: (j, i))],
        compiler_params=_params(("parallel", "parallel"), VMEM_BIG),
    )(n, w4, w4)


def _mm_resid_norm(name, x, w, h, g_post, alpha, g_next):
    t, kdim = x.shape
    d = w.shape[1]
    tm = ROW_TILE
    bk = kdim if kdim <= 1024 else kdim // 2
    nk = kdim // bk
    with_next = g_next is not None

    def body(x_ref, w_ref, h_ref, gp_ref, gn_ref, f_ref, hn_ref, *rest):
        acc_ref = rest[-1]
        k = pl.program_id(1)
        part = _dot(x_ref[...], w_ref[...])

        @pl.when(k == 0)
        def _():
            acc_ref[...] = part

        @pl.when(k > 0)
        def _():
            acc_ref[...] += part

        @pl.when(k == nk - 1)
        def _():
            f = acc_ref[...]
            f_ref[...] = f
            hn = h_ref[...] + alpha * (f * _rstd(f) * gp_ref[...])
            hn_ref[...] = hn
            if with_next:
                y = hn * _rstd(hn) * gn_ref[...]
                rest[0][...] = y.astype(BF16)
                rest[1][...] = y.T.astype(BF16)

    row = lambda i, k: (i, 0)
    vec = pl.BlockSpec((1, d), lambda i, k: (0, 0))
    out_shape = [jax.ShapeDtypeStruct((t, d), F32), jax.ShapeDtypeStruct((t, d), F32)]
    out_specs = [pl.BlockSpec((tm, d), row), pl.BlockSpec((tm, d), row)]
    if with_next:
        out_shape += [jax.ShapeDtypeStruct((t, d), BF16), jax.ShapeDtypeStruct((d, t), BF16)]
        out_specs += [pl.BlockSpec((tm, d), row), pl.BlockSpec((d, tm), lambda i, k: (0, i))]
    return pl.pallas_call(
        body, name=name, out_shape=out_shape, grid=(t // tm, nk),
        in_specs=[pl.BlockSpec((tm, bk), lambda i, k: (i, k)), pl.BlockSpec((bk, d), lambda i, k: (k, 0)),
                  pl.BlockSpec((tm, d), row), vec, vec],
        out_specs=out_specs,
        scratch_shapes=[pltpu.VMEM((tm, d), F32)],
        compiler_params=_params(("parallel", "arbitrary"), VMEM_BIG),
    )(x, w, h, g_post, g_post if g_next is None else g_next)


def _gate_prep(z, b_pad, f_col):
    t = z.shape[0]
    tm = ROW_TILE

    def body(z_ref, b_ref, f_ref, carry_ref):
        i = pl.program_id(0)

        @pl.when(i == 0)
        def _():
            carry_ref[...] = jnp.zeros_like(carry_ref)

        xs = z_ref[...] + b_ref[...]
        logf = jnp.minimum(xs, 0.0) - jnp.log(1.0 + jnp.exp(-jnp.abs(xs)))
        row = i * tm + lax.broadcasted_iota(jnp.int32, (tm, 1), 0)
        logf = jnp.where(row >= ROW_PAD, logf, 0.0)
        tri = (lax.broadcasted_iota(jnp.int32, (tm, tm), 0) >= lax.broadcasted_iota(jnp.int32, (tm, tm), 1))
        f = jnp.dot(tri.astype(F32), logf, preferred_element_type=F32, precision=lax.Precision.HIGHEST)
        f = f + carry_ref[0:1, :]
        f_ref[...] = f
        carry_ref[...] = jnp.broadcast_to(f[tm - 1:tm, :], carry_ref.shape)

    return pl.pallas_call(
        body, name="forget_gate_cumsum", out_shape=jax.ShapeDtypeStruct((t, 128), F32),
        grid=(t // tm,),
        in_specs=[pl.BlockSpec((tm, 128), lambda i: (i, f_col // 128)), pl.BlockSpec((1, 128), lambda i: (0, 0))],
        out_specs=pl.BlockSpec((tm, 128), lambda i: (i, 0)),
        scratch_shapes=[pltpu.VMEM((8, 128), F32)],
        compiler_params=_params(("arbitrary",)),
    )(z, b_pad)


def _lane_halves():
    lane = lax.broadcasted_iota(jnp.int32, (1, 128), 1)
    return lane < HEAD_DIM


def _attn_mask(qi, ki, tq, tk):
    row = qi * tq + lax.broadcasted_iota(jnp.int32, (tq, 1), 0)
    col = ki * tk + lax.broadcasted_iota(jnp.int32, (1, tk), 1)
    return (col <= row) & (col >= ROW_PAD)


def _attn_fwd(z, f_pair, f_row):
    t = z.shape[0]
    tq = tk = ROW_TILE
    nq = t // tq

    def body(q_ref, k_ref, v_ref, fq_ref, fk_ref, o_ref, lse_ref, m_ref, l_ref, acc_ref):
        qi, ki = pl.program_id(1), pl.program_id(2)

        @pl.when(ki == 0)
        def _():
            m_ref[...] = jnp.full_like(m_ref, NEG)
            l_ref[...] = jnp.zeros_like(l_ref)
            acc_ref[...] = jnp.zeros_like(acc_ref)

        @pl.when(ki <= qi)
        def _():
            first = _lane_halves()
            q = q_ref[...] * (HEAD_DIM ** -0.5)
            k = k_ref[...].astype(BF16)
            v = v_ref[...]
            mask = _attn_mask(qi, ki, tq, tk)
            new = []
            for hh in range(2):
                lanes = first if hh == 0 else jnp.logical_not(first)
                c0 = hh * HEAD_DIM
                s = _dot_nt(jnp.where(lanes, q, 0.0).astype(BF16), k)
                s = s + fq_ref[:, c0:c0 + 1] - fk_ref[hh:hh + 1, :]
                s = jnp.where(mask, s, NEG)
                m_prev = m_ref[:, c0:c0 + 1]
                m_new = jnp.maximum(m_prev, jnp.max(s, axis=1, keepdims=True))
                alpha = jnp.exp(m_prev - m_new)
                p = jnp.exp(s - m_new)
                l_new = alpha * l_ref[:, c0:c0 + 1] + jnp.sum(p, axis=1, keepdims=True)
                pv = _dot(p.astype(BF16), jnp.where(lanes, v, 0.0).astype(BF16))
                new.append((m_new, l_new, alpha, pv))
            (m0, l0, a0, pv0), (m1, l1, a1, pv1) = new
            acc_ref[...] = acc_ref[...] * jnp.where(first, a0, a1) + pv0 + pv1
            m_ref[...] = jnp.where(first, m0, m1)
            l_ref[...] = jnp.where(first, l0, l1)

        @pl.when(ki == qi)
        def _():
            o_ref[...] = acc_ref[...] / l_ref[...]
            lse_ref[...] = m_ref[...] + jnp.log(l_ref[...])

    kv = lambda p, qi, ki: jnp.minimum(ki, qi)
    return pl.pallas_call(
        body, name="attention_fwd",
        out_shape=[jax.ShapeDtypeStruct((t, ATTN_W), F32), jax.ShapeDtypeStruct((t, ATTN_W), F32)],
        grid=(4, nq, nq),
        in_specs=[pl.BlockSpec((tq, 128), lambda p, qi, ki: (qi, p)),
                  pl.BlockSpec((tk, 128), lambda p, qi, ki: (kv(p, qi, ki), 4 + p)),
                  pl.BlockSpec((tk, 128), lambda p, qi, ki: (kv(p, qi, ki), 8 + p)),
                  pl.BlockSpec((None, tq, 128), lambda p, qi, ki: (p, qi, 0)),
                  pl.BlockSpec((None, 8, tk), lambda p, qi, ki: (p, 0, kv(p, qi, ki)))],
        out_specs=[pl.BlockSpec((tq, 128), lambda p, qi, ki: (qi, p)),
                   pl.BlockSpec((tq, 128), lambda p, qi, ki: (qi, p))],
        scratch_shapes=[pltpu.VMEM((tq, 128), F32)] * 3,
        compiler_params=_params(("parallel", "parallel", "arbitrary")),
    )(z, z, z, f_pair, f_row)


def _attn_bwd(z, f_pair, f_row, o, lse, do):
    t = z.shape[0]
    tq = tk = ROW_TILE
    nq = t // tq

    def body(q_ref, k_ref, v_ref, fq_ref, fk_ref, o_ref, lse_ref, do_ref, dq_ref, dk_ref, dv_ref, dfk_ref, dfq_ref):
        ki, qi = pl.program_id(1), pl.program_id(2)

        @pl.when(qi == 0)
        def _():
            dk_ref[...] = jnp.zeros_like(dk_ref)
            dv_ref[...] = jnp.zeros_like(dv_ref)
            dfk_ref[...] = jnp.zeros_like(dfk_ref)

        rows = pl.ds(pl.multiple_of(qi * tq, tq), tq)

        @pl.when((ki == 0) & (qi >= ki))
        def _():
            dq_ref[rows, :] = jnp.zeros((tq, 128), F32)
            dfq_ref[rows, :] = jnp.zeros((tq, 128), F32)

        @pl.when(qi >= ki)
        def _():
            first = _lane_halves()
            scale = HEAD_DIM ** -0.5
            q = q_ref[...] * scale
            k = k_ref[...]
            v = v_ref[...].astype(BF16)
            do_ = do_ref[...]
            od = o_ref[...] * do_
            mask = _attn_mask(qi, ki, tq, tk)
            dq_acc = jnp.zeros((tq, 128), F32)
            dk_acc = jnp.zeros((tk, 128), F32)
            dv_acc = jnp.zeros((tk, 128), F32)
            col_sums, row_sums = [], []
            for hh in range(2):
                lanes = first if hh == 0 else jnp.logical_not(first)
                c0 = hh * HEAD_DIM
                qh = jnp.where(lanes, q, 0.0).astype(BF16)
                kh = jnp.where(lanes, k, 0.0).astype(BF16)
                doh = jnp.where(lanes, do_, 0.0).astype(BF16)
                s = _dot_nt(qh, kh)
                s = s + fq_ref[:, c0:c0 + 1] - fk_ref[hh:hh + 1, :]
                p = jnp.where(mask, jnp.exp(s - lse_ref[:, c0:c0 + 1]), 0.0)
                dp = _dot_nt(doh, v)
                delta = jnp.sum(jnp.where(lanes, od, 0.0), axis=1, keepdims=True)
                ds = p * (dp - delta)
                col_sums.append(jnp.sum(ds, axis=0, keepdims=True))
                row_sums.append(jnp.sum(ds, axis=1, keepdims=True))
                dv_acc += _dot(p.T.astype(BF16), doh)
                dk_acc += _dot(ds.T.astype(BF16), qh)
                dq_acc += _dot(ds.astype(BF16), kh)
            dq_ref[rows, :] += dq_acc * scale
            dfq_ref[rows, :] += jnp.where(first, row_sums[0], row_sums[1])
            dk_ref[...] += dk_acc
            dv_ref[...] += dv_acc
            dfk_ref[0:1, :] += col_sums[0]
            dfk_ref[1:2, :] += col_sums[1]

    qb = lambda p, ki, qi: jnp.maximum(qi, ki)
    return pl.pallas_call(
        body, name="attention_bwd",
        out_shape=[jax.ShapeDtypeStruct((t, ATTN_W), F32)] * 3 + [jax.ShapeDtypeStruct((4, 8, t), F32),
                                                                jax.ShapeDtypeStruct((t, ATTN_W), F32)],
        grid=(4, nq, nq),
        in_specs=[pl.BlockSpec((tq, 128), lambda p, ki, qi: (qb(p, ki, qi), p)),
                  pl.BlockSpec((tk, 128), lambda p, ki, qi: (ki, 4 + p)),
                  pl.BlockSpec((tk, 128), lambda p, ki, qi: (ki, 8 + p)),
                  pl.BlockSpec((None, tq, 128), lambda p, ki, qi: (p, qb(p, ki, qi), 0)),
                  pl.BlockSpec((None, 8, tk), lambda p, ki, qi: (p, 0, ki)),
                  pl.BlockSpec((tq, 128), lambda p, ki, qi: (qb(p, ki, qi), p)),
                  pl.BlockSpec((tq, 128), lambda p, ki, qi: (qb(p, ki, qi), p)),
                  pl.BlockSpec((tq, 128), lambda p, ki, qi: (qb(p, ki, qi), p))],
        out_specs=[pl.BlockSpec((t, 128), lambda p, ki, qi: (0, p)),
                   pl.BlockSpec((tk, 128), lambda p, ki, qi: (ki, p)),
                   pl.BlockSpec((tk, 128), lambda p, ki, qi: (ki, p)),
                   pl.BlockSpec((None, 8, tk), lambda p, ki, qi: (p, 0, ki)),
                   pl.BlockSpec((t, 128), lambda p, ki, qi: (0, p))],
        compiler_params=_params(("parallel", "arbitrary", "arbitrary"), VMEM_BIG),
    )(z, z, z, f_pair, f_row, o, lse, do)


def _shifted(prev_rows, x, shift):
    tm = x.shape[0]
    return pltpu.roll(jnp.concatenate([prev_rows, x], axis=0), shift, 0)[8:8 + tm]


def _ahead(x, next_rows, shift):
    tm = x.shape[0]
    return pltpu.roll(jnp.concatenate([x, next_rows], axis=0), tm + 8 - shift, 0)[0:tm]


def _conv_specs(tm, nt, cols):
    tiles = [pl.BlockSpec((tm, CONV_W), functools.partial(lambda i, c: (i, c), c=c)) for c in cols]
    halos = [pl.BlockSpec((8, CONV_W), functools.partial(lambda i, c: (jnp.maximum(i * (tm // 8) - 1, 0), c), c=c))
             for c in cols]
    return tiles, halos


def _conv_gate(z, conv_w):
    t = z.shape[0]
    tm = ROW_TILE
    nt = t // tm

    def body(cb_ref, cc_ref, ci_ref, hc_ref, hi_ref, w_ref, g_ref, gt_ref):
        i = pl.program_id(0)
        cc = cc_ref[...] * ci_ref[...]
        prev = jnp.where(i > 0, hc_ref[...] * hi_ref[...], 0.0)
        conv = w_ref[0:1, :] * _shifted(prev, cc, 2) + w_ref[1:2, :] * _shifted(prev, cc, 1) + w_ref[2:3, :] * cc
        g = cb_ref[...] * conv
        g_ref[...] = g.astype(BF16)
        gt_ref[...] = g.T.astype(BF16)

    (cb, cc, ci), (_, hc, hi) = _conv_specs(tm, nt, (3, 4, 5))
    return pl.pallas_call(
        body, name="conv_gate_fwd",
        out_shape=[jax.ShapeDtypeStruct((t, CONV_W), BF16), jax.ShapeDtypeStruct((CONV_W, t), BF16)],
        grid=(nt,),
        in_specs=[cb, cc, ci, hc, hi, pl.BlockSpec((8, CONV_W), lambda i: (0, 0))],
        out_specs=[pl.BlockSpec((tm, CONV_W), lambda i: (i, 0)), pl.BlockSpec((CONV_W, tm), lambda i: (0, i))],
        compiler_params=_params(("parallel",)),
    )(z, z, z, z, z, conv_w)


def _conv_bwd(z, dg, conv_w):
    t = z.shape[0]
    tm = ROW_TILE
    nt = t // tm

    def body(cb_ref, cc_ref, ci_ref, hc_ref, hi_ref, dg_ref, ncb_ref, ndg_ref, w_ref, dz_ref, dw_ref):
        i = pl.program_id(0)

        @pl.when(i == 0)
        def _():
            dw_ref[...] = jnp.zeros_like(dw_ref)

        cb, c_c, c_in = cb_ref[...], cc_ref[...], ci_ref[...]
        cc = c_c * c_in
        prev = jnp.where(i > 0, hc_ref[...] * hi_ref[...], 0.0)
        cc1, cc2 = _shifted(prev, cc, 1), _shifted(prev, cc, 2)
        w0, w1, w2 = w_ref[0:1, :], w_ref[1:2, :], w_ref[2:3, :]
        conv = w0 * cc2 + w1 * cc1 + w2 * cc
        dgv = dg_ref[...]
        dconv = dgv * cb
        nxt = jnp.where(i < nt - 1, ndg_ref[...] * ncb_ref[...], 0.0)
        dcc = w2 * dconv + w1 * _ahead(dconv, nxt, 1) + w0 * _ahead(dconv, nxt, 2)
        dz_ref[:, 0:CONV_W] = (dgv * conv).astype(BF16)
        dz_ref[:, CONV_W:2 * CONV_W] = (dcc * c_in).astype(BF16)
        dz_ref[:, 2 * CONV_W:] = (dcc * c_c).astype(BF16)
        dw_ref[0:1, :] += jnp.sum(dconv * cc2, axis=0, keepdims=True)
        dw_ref[1:2, :] += jnp.sum(dconv * cc1, axis=0, keepdims=True)
        dw_ref[2:3, :] += jnp.sum(dconv * cc, axis=0, keepdims=True)

    (cb, cc, ci), (_, hc, hi) = _conv_specs(tm, nt, (3, 4, 5))
    nxt = lambda i, c: (jnp.minimum((i + 1) * (tm // 8), t // 8 - 1), c)
    return pl.pallas_call(
        body, name="conv_gate_bwd",
        out_shape=[jax.ShapeDtypeStruct((t, 3 * CONV_W), BF16), jax.ShapeDtypeStruct((8, CONV_W), F32)],
        grid=(nt,),
        in_specs=[cb, cc, ci, hc, hi, pl.BlockSpec((tm, CONV_W), lambda i: (i, 0)),
                  pl.BlockSpec((8, CONV_W), lambda i: nxt(i, 3)), pl.BlockSpec((8, CONV_W), lambda i: nxt(i, 0)),
                  pl.BlockSpec((8, CONV_W), lambda i: (0, 0))],
        out_specs=[pl.BlockSpec((tm, 3 * CONV_W), lambda i: (i, 0)), pl.BlockSpec((8, CONV_W), lambda i: (0, 0))],
        compiler_params=_params(("arbitrary",)),
    )(z, z, z, z, z, dg, z, dg, conv_w)


def _branch_mix(z, o, g, w_ab, w_cb, d):
    t = z.shape[0]
    tm = ROW_TILE
    ga_col = (3 * ATTN_W + 3 * CONV_W) // d

    def body(o_ref, g_ref, ga_ref, gc_ref, wa_ref, wc_ref, mp_ref, mpt_ref, ot_ref):
        o_ = o_ref[...]
        ya = _dot(o_.astype(BF16), wa_ref[...])
        yc = _dot(g_ref[...], wc_ref[...])
        mp = _sigmoid(ga_ref[...]) * ya + _sigmoid(gc_ref[...]) * yc
        mp_ref[...] = mp.astype(BF16)
        mpt_ref[...] = mp.T.astype(BF16)
        ot_ref[...] = o_.T.astype(BF16)

    return pl.pallas_call(
        body, name="branch_mix_fwd",
        out_shape=[jax.ShapeDtypeStruct((t, d), BF16), jax.ShapeDtypeStruct((d, t), BF16),
                   jax.ShapeDtypeStruct((ATTN_W, t), BF16)],
        grid=(t // tm,),
        in_specs=[pl.BlockSpec((tm, ATTN_W), lambda i: (i, 0)), pl.BlockSpec((tm, CONV_W), lambda i: (i, 0)),
                  pl.BlockSpec((tm, d), lambda i: (i, ga_col)), pl.BlockSpec((tm, d), lambda i: (i, ga_col + 1)),
                  pl.BlockSpec((ATTN_W, d), lambda i: (0, 0)), pl.BlockSpec((CONV_W, d), lambda i: (0, 0))],
        out_specs=[pl.BlockSpec((tm, d), lambda i: (i, 0)), pl.BlockSpec((d, tm), lambda i: (0, i)),
                   pl.BlockSpec((ATTN_W, tm), lambda i: (0, i))],
        compiler_params=_params(("parallel",), VMEM_BIG),
    )(o, g, z, z, w_ab, w_cb)


def _branch_bwd(z, o, g, dmixed, w_out, w_ab, w_cb, d):
    t = z.shape[0]
    tm = ROW_TILE // 2
    ga_col = (3 * ATTN_W + 3 * CONV_W) // d

    def body(dm_ref, o_ref, g_ref, ga_ref, gc_ref, wo_ref, wa_ref, wc_ref, dya_ref, dyc_ref, dgt_ref, do_ref, dg_ref):
        dmp = _dot_nt(dm_ref[...], wo_ref[...])
        ya = _dot(o_ref[...].astype(BF16), wa_ref[...])
        yc = _dot(g_ref[...], wc_ref[...])
        sa, sc = _sigmoid(ga_ref[...]), _sigmoid(gc_ref[...])
        dya = (dmp * sa).astype(BF16)
        dyc = (dmp * sc).astype(BF16)
        dya_ref[...] = dya
        dyc_ref[...] = dyc
        dgt_ref[:, :d] = (dmp * ya * sa * (1.0 - sa)).astype(BF16)
        dgt_ref[:, d:] = (dmp * yc * sc * (1.0 - sc)).astype(BF16)
        do_ref[...] = _dot_nt(dya, wa_ref[...])
        dg_ref[...] = _dot_nt(dyc, wc_ref[...])

    row = lambda i: (i, 0)
    fixed = lambda i: (0, 0)
    return pl.pallas_call(
        body, name="branch_mix_bwd",
        out_shape=[jax.ShapeDtypeStruct((t, d), BF16), jax.ShapeDtypeStruct((t, d), BF16),
                   jax.ShapeDtypeStruct((t, 2 * d), BF16), jax.ShapeDtypeStruct((t, ATTN_W), F32),
                   jax.ShapeDtypeStruct((t, CONV_W), F32)],
        grid=(t // tm,),
        in_specs=[pl.BlockSpec((tm, d), row), pl.BlockSpec((tm, ATTN_W), row), pl.BlockSpec((tm, CONV_W), row),
                  pl.BlockSpec((tm, d), lambda i: (i, ga_col)), pl.BlockSpec((tm, d), lambda i: (i, ga_col + 1)),
                  pl.BlockSpec((d, d), fixed), pl.BlockSpec((ATTN_W, d), fixed), pl.BlockSpec((CONV_W, d), fixed)],
        out_specs=[pl.BlockSpec((tm, d), row), pl.BlockSpec((tm, d), row), pl.BlockSpec((tm, 2 * d), row),
                   pl.BlockSpec((tm, ATTN_W), row), pl.BlockSpec((tm, CONV_W), row)],
        compiler_params=_params(("parallel",), VMEM_BIG),
    )(dmixed, o, g, z, z, w_out, w_ab, w_cb)


def _loss_grad(h, target_pad):
    t, d = h.shape
    tm = ROW_TILE

    def body(h_ref, t_ref, dy_ref, loss_ref):
        i = pl.program_id(0)

        @pl.when(i == 0)
        def _():
            loss_ref[...] = jnp.zeros_like(loss_ref)

        row = i * tm + lax.broadcasted_iota(jnp.int32, (tm, 1), 0)
        err = jnp.where(row >= N_FRONT, h_ref[...] - t_ref[...], 0.0)
        dy_ref[...] = err * (1.0 / d)
        per_row = jnp.sum(err * err, axis=1, keepdims=True) * (1.0 / d)
        loss_ref[...] += 0.5 * jnp.sum(per_row, axis=0, keepdims=True)

    return pl.pallas_call(
        body, name="loss_and_grad",
        out_shape=[jax.ShapeDtypeStruct((t, d), F32), jax.ShapeDtypeStruct((1, 128), F32)],
        grid=(t // tm,),
        in_specs=[pl.BlockSpec((tm, d), lambda i: (i, 0))] * 2,
        out_specs=[pl.BlockSpec((tm, d), lambda i: (i, 0)), pl.BlockSpec((1, 128), lambda i: (0, 0))],
        compiler_params=_params(("arbitrary",)),
    )(h, target_pad)


def _norm_bwd(name, x, g, dy, alpha):
    t, d = x.shape
    tm = ROW_TILE

    def body(x_ref, g_ref, dy_ref, dx_ref, dg_ref):
        @pl.when(pl.program_id(0) == 0)
        def _():
            dg_ref[...] = jnp.zeros_like(dg_ref)

        dx, dg = _rms_bwd(x_ref[...], g_ref[...], dy_ref[...])
        dx_ref[...] = (alpha * dx).astype(BF16)
        dg_ref[...] += alpha * dg

    row = pl.BlockSpec((tm, d), lambda i: (i, 0))
    vec = pl.BlockSpec((1, d), lambda i: (0, 0))
    return pl.pallas_call(
        body, name=name,
        out_shape=[jax.ShapeDtypeStruct((t, d), BF16), jax.ShapeDtypeStruct((1, d), F32)],
        grid=(t // tm,), in_specs=[row, vec, row], out_specs=[row, vec],
        compiler_params=_params(("arbitrary",)),
    )(x, g, dy)


def _ffn_bwd_mid(name, df, w_out, ab):
    t, d = df.shape
    cw = ab.shape[1] // 4
    tm = ROW_TILE

    def body(df_ref, w_ref, ab_ref, o_ref):
        ds = _dot_nt(df_ref[...], w_ref[...])
        a = ab_ref[:, :cw]
        b = ab_ref[:, cw:]
        sg = _sigmoid(a)
        o_ref[:, :cw] = (ds * b * (sg * (1.0 + a * (1.0 - sg)))).astype(BF16)
        o_ref[:, cw:] = (ds * (a * sg)).astype(BF16)

    return pl.pallas_call(
        body, name=name, out_shape=jax.ShapeDtypeStruct((t, 4 * cw), BF16),
        grid=(2, t // tm),
        in_specs=[pl.BlockSpec((tm, d), lambda j, i: (i, 0)), pl.BlockSpec((cw, d), lambda j, i: (j, 0)),
                  pl.BlockSpec((tm, 2 * cw), lambda j, i: (i, j))],
        out_specs=pl.BlockSpec((tm, 2 * cw), lambda j, i: (i, j)),
        compiler_params=_params(("parallel", "parallel"), VMEM_BIG),
    )(df, w_out, ab)


def _mm_nt_norm_bwd(name, dy, w, w_spec, bk, h, g, dh_in):
    t, kdim = dy.shape
    d = h.shape[1]
    tm = ROW_TILE
    nk = kdim // bk

    def body(dy_ref, w_ref, h_ref, g_ref, dhi_ref, dh_ref, dg_ref, acc_ref):
        i, k = pl.program_id(0), pl.program_id(1)

        @pl.when((i == 0) & (k == 0))
        def _():
            dg_ref[...] = jnp.zeros_like(dg_ref)

        part = _dot_nt(dy_ref[...], w_ref[...])

        @pl.when(k == 0)
        def _():
            acc_ref[...] = part

        @pl.when(k > 0)
        def _():
            acc_ref[...] += part

        @pl.when(k == nk - 1)
        def _():
            dx, dg = _rms_bwd(h_ref[...], g_ref[...], acc_ref[...])
            dh_ref[...] = dhi_ref[...] + dx
            dg_ref[...] += dg

    row = pl.BlockSpec((tm, d), lambda i, k: (i, 0))
    vec = pl.BlockSpec((1, d), lambda i, k: (0, 0))
    return pl.pallas_call(
        body, name=name,
        out_shape=[jax.ShapeDtypeStruct((t, d), F32), jax.ShapeDtypeStruct((1, d), F32)],
        grid=(t // tm, nk),
        in_specs=[pl.BlockSpec((tm, bk), lambda i, k: (i, k)), w_spec, row, vec, row],
        out_specs=[row, vec],
        scratch_shapes=[pltpu.VMEM((tm, d), F32)],
        compiler_params=_params(("arbitrary", "arbitrary"), VMEM_BIG),
    )(dy, w, h, g, dh_in)


def _gate_bwd(df_pad, z, b_pad, f_col):
    t = z.shape[0]
    tm = ROW_TILE
    nt = t // tm

    def body(d_ref, z_ref, b_ref, dz_ref, db_ref, carry_ref):
        i = pl.program_id(0)

        @pl.when(i == 0)
        def _():
            carry_ref[...] = jnp.zeros_like(carry_ref)
            db_ref[...] = jnp.zeros_like(db_ref)

        tri = (lax.broadcasted_iota(jnp.int32, (tm, tm), 0) <= lax.broadcasted_iota(jnp.int32, (tm, tm), 1))
        tail = jnp.dot(tri.astype(F32), d_ref[...], preferred_element_type=F32, precision=lax.Precision.HIGHEST)
        tail = tail + carry_ref[0:1, :]
        carry_ref[...] = jnp.broadcast_to(tail[0:1, :], carry_ref.shape)
        row = (nt - 1 - i) * tm + lax.broadcasted_iota(jnp.int32, (tm, 1), 0)
        dlogit = jnp.where(row >= ROW_PAD, tail * _sigmoid(-(z_ref[...] + b_ref[...])), 0.0)
        dz_ref[...] = jnp.zeros_like(dz_ref)
        dz_ref[:, 0:128] = dlogit.astype(BF16)
        db_ref[...] += jnp.sum(dlogit, axis=0, keepdims=True)

    rev = lambda i: (nt - 1 - i, 0)
    return pl.pallas_call(
        body, name="forget_gate_bwd",
        out_shape=[jax.ShapeDtypeStruct((t, F_PAD), BF16), jax.ShapeDtypeStruct((1, 128), F32)],
        grid=(nt,),
        in_specs=[pl.BlockSpec((tm, 128), rev), pl.BlockSpec((tm, 128), lambda i: (nt - 1 - i, f_col // 128)),
                  pl.BlockSpec((1, 128), lambda i: (0, 0))],
        out_specs=[pl.BlockSpec((tm, F_PAD), rev), pl.BlockSpec((1, 128), lambda i: (0, 0))],
        scratch_shapes=[pltpu.VMEM((8, 128), F32)],
        compiler_params=_params(("arbitrary",)),
    )(df_pad, z, b_pad)


def _ffn_fwd(tag, n, w_in4, w_out, h, g_post, g_next):
    ab, s, s_t = _ffn_in(f"{tag}_in_fwd", n, w_in4)
    outs = _mm_resid_norm(f"{tag}_out_fwd", s, w_out, h, g_post, 0.5, g_next)
    return ab, s_t, outs


def _ffn_bwd(tag, dh, f, g_post, ab, s_t, n_t, w_in4, w_out, h_in, g_pre):
    d, cw = w_in4.shape[1], w_in4.shape[2]
    t = dh.shape[0]
    df, dg_post = _norm_bwd(f"{tag}_post_norm_bwd", f, g_post, dh, 0.5)
    dw_out = _weight_grad(f"{tag}_dw_out", s_t, df, d, out_rows=cw // 2)
    dab = _ffn_bwd_mid(f"{tag}_mid_bwd", df, w_out, ab)
    dh_in, dg_pre = _mm_nt_norm_bwd(
        f"{tag}_in_bwd", dab, w_in4, pl.BlockSpec((None, d, cw), lambda i, k: (_slot_of(k), 0, 0)), cw, h_in, g_pre, dh)
    bk = _k_tile(t)
    dw_in = _matmul(
        f"{tag}_dw_in", n_t, dab, jax.ShapeDtypeStruct((4, d, cw), F32), (1, 4, t // bk),
        pl.BlockSpec((d, bk), lambda a, b, k: (0, k)), pl.BlockSpec((bk, cw), lambda a, b, k: (k, b)),
        pl.BlockSpec((None, d, cw), lambda a, b, k: (_slot_of(b), 0, 0)), vmem=VMEM_BIG)
    return dh_in, dg_post, dg_pre, dw_in, dw_out


def _pack_small(meta, conv, gains, b_forget):
    d = gains[0].shape[1]
    rows = [meta.reshape(4, d), jnp.pad(conv.reshape(1, 3 * 128), ((0, 0), (0, d - 3 * 128)))]
    rows += list(gains) + [jnp.pad(b_forget, ((0, 0), (0, d - HEADS)))]
    return jnp.concatenate(rows + [jnp.zeros((4, d), F32)], axis=0)


def _unpack_small(block):
    d = block.shape[1]
    meta = block[0:4].reshape(N_META, d // 4)
    conv = block[4, :3 * 128].reshape(1, 3, 128)
    gains = [block[5 + i:6 + i] for i in range(6)]
    return meta, conv, gains, block[11:12, :HEADS]


def kernel(x, meta_tokens, w_in, b_forget, conv_w, w_attn_branch, w_conv_branch, w_out, g_ffn1_pre, g_ffn1_post, w_ffn1_in, w_ffn1_out, g_mix_pre, g_mix_post, g_ffn2_pre, g_ffn2_post, w_ffn2_in, w_ffn2_out, loss_target, m_meta_tokens, m_w_in, m_b_forget, m_conv_w, m_w_attn_branch, m_w_conv_branch, m_w_out, m_g_ffn1_pre, m_g_ffn1_post, m_w_ffn1_in, m_w_ffn1_out, m_g_mix_pre, m_g_mix_post, m_g_ffn2_pre, m_g_ffn2_post, m_w_ffn2_in, m_w_ffn2_out, v_meta_tokens, v_w_in, v_b_forget, v_conv_w, v_w_attn_branch, v_w_conv_branch, v_w_out, v_g_ffn1_pre, v_g_ffn1_post, v_w_ffn1_in, v_w_ffn1_out, v_g_mix_pre, v_g_mix_post, v_g_ffn2_pre, v_g_ffn2_post, v_w_ffn2_in, v_w_ffn2_out):
    seq, d = x.shape[1], x.shape[2]
    t = seq + N_FRONT
    n_main = 3 * ATTN_W + 3 * CONV_W + 2 * d
    nz = n_main + F_PAD
    f_lo = 3 * ATTN_W
    c_arr = lax.axis_index("c").astype(jnp.int32).reshape(1)
    chip = 2 * lax.axis_index("x") + lax.axis_index("y")

    big = [w_in[0], w_attn_branch[0], w_conv_branch[0], w_out[0], w_ffn1_in[0], w_ffn1_out[0], w_ffn2_in[0], w_ffn2_out[0]]
    small_gather = jnp.concatenate(
        [meta_tokens.reshape(4, d), jnp.pad(conv_w.reshape(1, 3 * 128), ((0, 0), (0, d - 3 * 128))),
         jnp.zeros((11, d), F32)], axis=0)
    gathered = _all_gather([w.astype(BF16) for w in big] + [small_gather])
    w_in4, w_ab4, w_cb4, w_out4, w_f1_in4, w_f1_out4, w_f2_in4, w_f2_out4, small4 = gathered

    w_in_full = jnp.transpose(w_in4, (1, 0, 2)).reshape(d, 4 * w_in4.shape[2])
    w_in_pad = jnp.concatenate(
        [w_in_full[:, :f_lo], w_in_full[:, f_lo + HEADS:], w_in_full[:, f_lo:f_lo + HEADS],
         jnp.zeros((d, F_PAD - HEADS), BF16)], axis=1)
    w_ab = jnp.transpose(w_ab4, (1, 0, 2)).reshape(ATTN_W, d)
    w_cb = jnp.transpose(w_cb4, (1, 0, 2)).reshape(CONV_W, d)
    w_out_full = w_out4.reshape(d, d)
    w_f1_out = w_f1_out4.reshape(-1, d)
    w_f2_out = w_f2_out4.reshape(-1, d)
    meta_full = jnp.transpose(small4[:, 0:4].reshape(4, N_META, d // 4), (1, 0, 2)).reshape(N_META, d)
    conv_full = jnp.transpose(small4[:, 4, :3 * 128].reshape(4, 3, 128), (1, 0, 2)).reshape(3, CONV_W)
    conv_pad = jnp.pad(conv_full, ((0, 5), (0, 0)))
    b_pad = jnp.pad(b_forget, ((0, 0), (0, 128 - HEADS)))

    h0 = jnp.concatenate([jnp.zeros((ROW_PAD, d), F32), meta_full, x[0]], axis=0)
    target_pad = jnp.concatenate([jnp.zeros((N_FRONT, d), F32), loss_target[0]], axis=0)
    n1, n1_t = _norm_fwd("ffn1_pre_norm", h0, g_ffn1_pre)
    ab1, s1_t, (f1, h1, u, u_t) = _ffn_fwd("ffn1", n1, w_f1_in4, w_f1_out, h0, g_ffn1_post, g_mix_pre)
    z = _matmul(
        "mix_in_proj", u, w_in_pad, jax.ShapeDtypeStruct((t, nz), F32), (nz // 512, t // ROW_TILE, 1),
        pl.BlockSpec((ROW_TILE, d), lambda j, i, k: (i, 0)), pl.BlockSpec((d, 512), lambda j, i, k: (0, j)),
        pl.BlockSpec((ROW_TILE, 512), lambda j, i, k: (i, j)))
    f_cum = _gate_prep(z, b_pad, n_main)
    f_heads = f_cum[:, :HEADS]
    f_pair = jnp.transpose(jnp.repeat(f_heads.reshape(t, 4, 2), HEAD_DIM, axis=2), (1, 0, 2))
    f_row = jnp.pad(jnp.transpose(f_heads).reshape(4, 2, t), ((0, 0), (0, 6), (0, 0)))
    o, lse = _attn_fwd(z, f_pair, f_row)
    g, g_t = _conv_gate(z, conv_pad)
    mp, mp_t, o_t = _branch_mix(z, o, g, w_ab, w_cb, d)
    mixed, h2, n2, n2_t = _mm_resid_norm("mix_out_fwd", mp, w_out_full, h1, g_mix_post, 1.0, g_ffn2_pre)
    ab2, s2_t, (f2, h3) = _ffn_fwd("ffn2", n2, w_f2_in4, w_f2_out, h2, g_ffn2_post, None)
    dh3, loss_part = _loss_grad(h3, target_pad)
    loss = lax.psum(loss_part[0, 0], ("x", "y", "c"))

    dh2, dg_f2_post, dg_f2_pre, dw_f2_in, dw_f2_out = _ffn_bwd(
        "ffn2", dh3, f2, g_ffn2_post, ab2, s2_t, n2_t, w_f2_in4, w_f2_out, h2, g_ffn2_pre)
    dmixed, dg_mix_post = _norm_bwd("mix_post_norm_bwd", mixed, g_mix_post, dh2, 1.0)
    dw_out = _weight_grad("mix_dw_out", mp_t, dmixed, d)
    dya, dyc, dgates, do, dgconv = _branch_bwd(z, o, g, dmixed, w_out_full, w_ab, w_cb, d)
    dw_ab = _weight_grad("mix_dw_attn_branch", o_t, dya, d)
    dw_cb = _weight_grad("mix_dw_conv_branch", g_t, dyc, d)
    dz_conv, dconv_w = _conv_bwd(z, dgconv, conv_pad)
    dq, dk, dv, dfk, dfq = _attn_bwd(z, f_pair, f_row, o, lse, do)
    df_cum = dfq[:, ::HEAD_DIM] - jnp.transpose(dfk[:, :2].reshape(HEADS, t))
    df_pad = jnp.pad(df_cum, ((0, 0), (0, 128 - HEADS)))
    dz_f, db_forget = _gate_bwd(df_pad, z, b_pad, n_main)
    dz = jnp.concatenate([dq.astype(BF16), dk.astype(BF16), dv.astype(BF16), dz_conv, dgates, dz_f], axis=1)
    dh1, dg_mix_pre = _mm_nt_norm_bwd(
        "mix_in_bwd", dz, w_in_pad, pl.BlockSpec((d, 512), lambda i, k: (0, k)), 512, h1, g_mix_pre, dh2)
    dw_in_pad = _weight_grad("mix_dw_in", u_t, dz, 512)
    dh0, dg_f1_post, dg_f1_pre, dw_f1_in, dw_f1_out = _ffn_bwd(
        "ffn1", dh1, f1, g_ffn1_post, ab1, s1_t, n1_t, w_f1_in4, w_f1_out, h0, g_ffn1_pre)
    grad_x = dh0[N_FRONT:][None]
    dmeta = dh0[ROW_PAD:N_FRONT]

    cs = w_in4.shape[2]
    dw_in_full = jnp.concatenate(
        [dw_in_pad[:, :f_lo], dw_in_pad[:, n_main:n_main + HEADS], dw_in_pad[:, f_lo:n_main]], axis=1)
    small_grad = jnp.stack([
        _pack_small(dmeta[:, j * (d // 4):(j + 1) * (d // 4)], dconv_w[:3, j * 128:(j + 1) * 128],
                    [dg_f1_pre, dg_f1_post, dg_mix_pre, dg_mix_post, dg_f2_pre, dg_f2_post], db_forget[:, :HEADS])
        for j in range(4)])
    slots = [
        jnp.transpose(dw_in_full.reshape(d, 4, cs), (1, 0, 2)),
        jnp.transpose(dw_ab.reshape(ATTN_W, 4, d // 4), (1, 0, 2)),
        jnp.transpose(dw_cb.reshape(CONV_W, 4, d // 4), (1, 0, 2)),
        dw_out.reshape(4, d // 4, d),
        dw_f1_in, dw_f1_out.reshape(4, -1, d), dw_f2_in, dw_f2_out.reshape(4, -1, d),
        small_grad,
    ]
    tags = ["w_in", "w_attn_branch", "w_conv_branch", "w_out", "w_ffn1_in", "w_ffn1_out", "w_ffn2_in", "w_ffn2_out", "small"]

    got = _pair_send_halves(slots)
    pair_sums = [_pair_add(tag, s, a, c_arr) for tag, s, a in zip(tags, slots, got)]
    arrived = _chip_scatter(pair_sums)
    halves = [_chip_add(tag, a) for tag, a in zip(tags, arrived)]
    grads = _pair_join_halves(halves)

    small = [g_ffn1_pre, g_ffn1_post, g_mix_pre, g_mix_post, g_ffn2_pre, g_ffn2_post]
    small_m = [m_g_ffn1_pre, m_g_ffn1_post, m_g_mix_pre, m_g_mix_post, m_g_ffn2_pre, m_g_ffn2_post]
    small_v = [v_g_ffn1_pre, v_g_ffn1_post, v_g_mix_pre, v_g_mix_post, v_g_ffn2_pre, v_g_ffn2_post]
    ws = big + [_pack_small(meta_tokens, conv_w[0], small, b_forget)]
    ms = [m_w_in[0], m_w_attn_branch[0], m_w_conv_branch[0], m_w_out[0], m_w_ffn1_in[0], m_w_ffn1_out[0],
          m_w_ffn2_in[0], m_w_ffn2_out[0], _pack_small(m_meta_tokens, m_conv_w[0], small_m, m_b_forget)]
    vs = [v_w_in[0], v_w_attn_branch[0], v_w_conv_branch[0], v_w_out[0], v_w_ffn1_in[0], v_w_ffn1_out[0],
          v_w_ffn2_in[0], v_w_ffn2_out[0], _pack_small(v_meta_tokens, v_conv_w[0], small_v, v_b_forget)]
    updates = [_adamw(tag, w, g_, m, v) for tag, w, g_, m, v in zip(tags, ws, grads, ms, vs)]

    def leaves(big_vals, small_block):
        meta, conv, gains, bf = _unpack_small(small_block)
        w_in_, w_ab_, w_cb_, w_out_, f1_in, f1_out, f2_in, f2_out = [b[None] for b in big_vals]
        return [meta, w_in_, bf, conv, w_ab_, w_cb_, w_out_, gains[0], gains[1], f1_in, f1_out,
                gains[2], gains[3], gains[4], gains[5], f2_in, f2_out]

    out_g = leaves(grads[:8], grads[8])
    out_d = leaves([u_[0] for u_ in updates[:8]], updates[8][0])
    out_m = leaves([u_[1] for u_ in updates[:8]], updates[8][1])
    out_v = leaves([u_[2] for u_ in updates[:8]], updates[8][2])
    del chip
    return (loss, grad_x, *out_g, *out_d, *out_m, *out_v)
```

```python
import functools

import jax
import jax.numpy as jnp
from jax import lax
from jax.experimental import pallas as pl
from jax.experimental.pallas import tpu as pltpu

N_META = 16
ROW_PAD = 112
N_FRONT = ROW_PAD + N_META
HEADS = 8
HEAD_DIM = 64
ATTN_W = HEADS * HEAD_DIM
CONV_W = 512
NORM_EPS = 1e-6
ROW_TILE = 640
F_PAD = 512
ATTN_ROW_PARTS = 1
NEG = -1e30
ADAM_LR = 0.001
ADAM_B1 = 0.9
ADAM_B2 = 0.999
ADAM_EPS = 1e-08
ADAM_WD = 0.01
ADAM_STEP = 10
VMEM_BIG = 56 * 1024 * 1024
MESH = pl.DeviceIdType.MESH
ANY = pl.BlockSpec(memory_space=pl.ANY)
F32 = jnp.float32
BF16 = jnp.bfloat16


def _params(sem, vmem=None):
    return pltpu.CompilerParams(dimension_semantics=sem, vmem_limit_bytes=vmem)


def _sigmoid(x):
    return 1.0 / (1.0 + jnp.exp(-x))


def _rstd(x):
    return lax.rsqrt(jnp.mean(x * x, axis=-1, keepdims=True) + NORM_EPS)


def _rms_bwd(x, g, dy):
    r = _rstd(x)
    xr = x * r
    gdy = g * dy
    dx = r * (gdy - xr * jnp.mean(xr * gdy, axis=-1, keepdims=True))
    return dx, jnp.sum(dy * xr, axis=0, keepdims=True)


def _dot(a, b):
    return jnp.dot(a, b, preferred_element_type=F32)


def _dot_nt(a, b):
    return lax.dot_general(a, b, (((1,), (1,)), ((), ())), preferred_element_type=F32)


def _k_tile(t):
    return 1664 if t % 1664 == 0 else ROW_TILE


def _place():
    x, y, c = lax.axis_index("x"), lax.axis_index("y"), lax.axis_index("c")
    chips = [(1 - x, y), (x, 1 - y), (1 - x, 1 - y)]
    return x, y, c, chips


def _all_gather(shards):
    n = len(shards)

    def body(*refs):
        ins, outs = refs[:n], refs[n:2 * n]
        send_sems, recv_sems, local_sems = refs[2 * n:]
        x, y, c, chips = _place()
        me = 2 * x + y
        sibling = (x, y, 1 - c)

        def remote(i, k, slot, rows, to, src=None):
            dst = outs[i].at[slot, rows]
            return pltpu.make_async_remote_copy(
                src_ref=dst if src is None else src, dst_ref=dst,
                send_sem=send_sems.at[i, k], recv_sem=recv_sems.at[i, k],
                device_id=to, device_id_type=MESH)

        started = []
        local = []
        for i in range(n):
            half = ins[i].shape[0] // 2
            mine = pl.ds(c * half, half)
            cp = pltpu.make_async_copy(ins[i], outs[i].at[me], local_sems.at[i])
            cp.start()
            local.append(cp)
            for k, (cx, cy) in enumerate(chips):
                cp = remote(i, k, me, mine, (cx, cy, c), src=ins[i].at[mine])
                cp.start()
                started.append(cp)
        for i in range(n):
            half = ins[i].shape[0] // 2
            mine = pl.ds(c * half, half)
            for k, (cx, cy) in enumerate(chips):
                remote(i, k, 2 * cx + cy, mine, (x, y, c)).wait_recv()
                cp = remote(i, 3 + k, 2 * cx + cy, mine, sibling)
                cp.start()
                started.append(cp)
        for i in range(n):
            half = ins[i].shape[0] // 2
            theirs = pl.ds((1 - c) * half, half)
            for k, (cx, cy) in enumerate(chips):
                remote(i, 3 + k, 2 * cx + cy, theirs, (x, y, c)).wait_recv()
        for cp in started:
            cp.wait_send()
        for cp in local:
            cp.wait()

    return pl.pallas_call(
        body, name="all_gather_weights",
        out_shape=[jax.ShapeDtypeStruct((4,) + s.shape, s.dtype) for s in shards],
        in_specs=[ANY] * n, out_specs=[ANY] * n,
        scratch_shapes=[pltpu.SemaphoreType.DMA((n, 6)), pltpu.SemaphoreType.DMA((n, 6)),
                        pltpu.SemaphoreType.DMA((n,))],
    )(*shards)


def _pair_send_halves(grads):
    n = len(grads)

    def body(*refs):
        ins, outs = refs[:n], refs[n:2 * n]
        send_sems, recv_sems = refs[2 * n:]
        x, y, c, _ = _place()
        cps = []
        for i in range(n):
            half = ins[i].shape[1] // 2
            cp = pltpu.make_async_remote_copy(
                src_ref=ins[i].at[:, pl.ds((1 - c) * half, half)], dst_ref=outs[i],
                send_sem=send_sems.at[i], recv_sem=recv_sems.at[i],
                device_id=(x, y, 1 - c), device_id_type=MESH)
            cp.start()
            cps.append(cp)
        for cp in cps:
            cp.wait()

    return pl.pallas_call(
        body, name="grad_pair_exchange",
        out_shape=[jax.ShapeDtypeStruct((4, g.shape[1] // 2, g.shape[2]), g.dtype) for g in grads],
        in_specs=[ANY] * n, out_specs=[ANY] * n,
        scratch_shapes=[pltpu.SemaphoreType.DMA((n,)), pltpu.SemaphoreType.DMA((n,))],
    )(*grads)


def _chip_scatter(parts):
    n = len(parts)

    def body(*refs):
        ins, outs = refs[:n], refs[n:2 * n]
        send_sems, recv_sems, local_sems = refs[2 * n:]
        x, y, c, chips = _place()
        me = 2 * x + y
        sends, local = [], []
        for i in range(n):
            cp = pltpu.make_async_copy(ins[i].at[me], outs[i].at[me], local_sems.at[i])
            cp.start()
            local.append(cp)
            for k, (cx, cy) in enumerate(chips):
                cp = pltpu.make_async_remote_copy(
                    src_ref=ins[i].at[2 * cx + cy], dst_ref=outs[i].at[me],
                    send_sem=send_sems.at[i, k], recv_sem=recv_sems.at[i, k],
                    device_id=(cx, cy, c), device_id_type=MESH)
                cp.start()
                sends.append(cp)
        for i in range(n):
            for k, (cx, cy) in enumerate(chips):
                got = outs[i].at[2 * cx + cy]
                pltpu.make_async_remote_copy(
                    src_ref=got, dst_ref=got, send_sem=send_sems.at[i, k], recv_sem=recv_sems.at[i, k],
                    device_id=(x, y, c), device_id_type=MESH).wait_recv()
        for cp in sends:
            cp.wait_send()
        for cp in local:
            cp.wait()

    return pl.pallas_call(
        body, name="grad_chip_scatter",
        out_shape=[jax.ShapeDtypeStruct(p.shape, p.dtype) for p in parts],
        in_specs=[ANY] * n, out_specs=[ANY] * n,
        scratch_shapes=[pltpu.SemaphoreType.DMA((n, 3)), pltpu.SemaphoreType.DMA((n, 3)),
                        pltpu.SemaphoreType.DMA((n,))],
    )(*parts)


def _pair_join_halves(halves):
    n = len(halves)

    def body(*refs):
        ins, outs = refs[:n], refs[n:2 * n]
        send_sems, recv_sems, local_sems = refs[2 * n:]
        x, y, c, _ = _place()
        cps, local = [], []
        for i in range(n):
            half = ins[i].shape[0]
            mine = pl.ds(c * half, half)
            cp = pltpu.make_async_copy(ins[i], outs[i].at[mine], local_sems.at[i])
            cp.start()
            local.append(cp)
            cp = pltpu.make_async_remote_copy(
                src_ref=ins[i], dst_ref=outs[i].at[mine],
                send_sem=send_sems.at[i], recv_sem=recv_sems.at[i],
                device_id=(x, y, 1 - c), device_id_type=MESH)
            cp.start()
            cps.append(cp)
        for i in range(n):
            half = ins[i].shape[0]
            theirs = outs[i].at[pl.ds((1 - c) * half, half)]
            pltpu.make_async_remote_copy(
                src_ref=theirs, dst_ref=theirs, send_sem=send_sems.at[i], recv_sem=recv_sems.at[i],
                device_id=(x, y, c), device_id_type=MESH).wait_recv()
        for cp in cps:
            cp.wait_send()
        for cp in local:
            cp.wait()

    return pl.pallas_call(
        body, name="grad_pair_join",
        out_shape=[jax.ShapeDtypeStruct((2 * h.shape[0], h.shape[1]), h.dtype) for h in halves],
        in_specs=[ANY] * n, out_specs=[ANY] * n,
        scratch_shapes=[pltpu.SemaphoreType.DMA((n,)), pltpu.SemaphoreType.DMA((n,)),
                        pltpu.SemaphoreType.DMA((n,))],
    )(*halves)


def _row_block(rows, cols, n_bufs, budget=20 * 1024 * 1024):
    best = min(rows, 16)
    for b in range(16, rows + 1, 16):
        if rows % b == 0 and 2 * n_bufs * b * cols * 4 <= budget:
            best = b
    return best


def _pair_add(tag, grad, got, c_arr, out_dtype):
    _, rows, cols = grad.shape
    half = rows // 2
    bh = _row_block(half, cols, 3)
    nb = half // bh

    def body(c_ref, g_ref, a_ref, o_ref):
        o_ref[...] = (g_ref[...] + a_ref[...]).astype(out_dtype)

    return pl.pallas_call(
        body, name=f"pair_add_{tag}",
        out_shape=jax.ShapeDtypeStruct((4, half, cols), out_dtype),
        grid_spec=pltpu.PrefetchScalarGridSpec(
            num_scalar_prefetch=1, grid=(4, nb),
            in_specs=[pl.BlockSpec((None, bh, cols), lambda j, r, c: (j, c[0] * nb + r, 0)),
                      pl.BlockSpec((None, bh, cols), lambda j, r, c: (j, r, 0))],
            out_specs=pl.BlockSpec((None, bh, cols), lambda j, r, c: (j, r, 0))),
        compiler_params=_params(("parallel", "parallel")),
    )(c_arr, grad, got)


def _chip_add(tag, parts):
    _, half, cols = parts.shape
    bh = _row_block(half, cols, 5)

    def body(p_ref, o_ref):
        a, b, c, d = [p_ref[j].astype(F32) for j in range(4)]
        o_ref[...] = ((a + b) + c) + d

    return pl.pallas_call(
        body, name=f"chip_add_{tag}",
        out_shape=jax.ShapeDtypeStruct((half, cols), F32),
        grid=(half // bh,),
        in_specs=[pl.BlockSpec((4, bh, cols), lambda r: (0, r, 0))],
        out_specs=pl.BlockSpec((bh, cols), lambda r: (r, 0)),
        compiler_params=_params(("parallel",)),
    )(parts)


def _adamw(tag, w, g, m, v):
    rows, cols = w.shape
    br = _row_block(rows, cols, 7)

    def body(w_ref, g_ref, m_ref, v_ref, d_ref, mo_ref, vo_ref):
        g = g_ref[...]
        m_new = ADAM_B1 * m_ref[...] + (1.0 - ADAM_B1) * g
        v_new = ADAM_B2 * v_ref[...] + (1.0 - ADAM_B2) * (g * g)
        m_hat = m_new / (1.0 - ADAM_B1 ** ADAM_STEP)
        v_hat = v_new / (1.0 - ADAM_B2 ** ADAM_STEP)
        d_ref[...] = -ADAM_LR * (m_hat / (jnp.sqrt(v_hat) + ADAM_EPS) + ADAM_WD * w_ref[...])
        mo_ref[...] = m_new
        vo_ref[...] = v_new

    spec = pl.BlockSpec((br, cols), lambda r: (r, 0))
    return pl.pallas_call(
        body, name=f"adamw_{tag}",
        out_shape=[jax.ShapeDtypeStruct((rows, cols), F32)] * 3,
        grid=(rows // br,), in_specs=[spec] * 4, out_specs=[spec] * 3,
        compiler_params=_params(("parallel",)),
    )(w, g, m, v)


def _matmul(name, x, w, out_shape, grid, x_spec, w_spec, o_spec, *, nt=False, vmem=None):
    nk = grid[2]
    acc_shape = tuple(d for d in o_spec.block_shape if d is not None)

    def body(x_ref, w_ref, o_ref, acc_ref):
        k = pl.program_id(2)
        part = _dot_nt(x_ref[...], w_ref[...]) if nt else _dot(x_ref[...], w_ref[...])
        if nk == 1:
            o_ref[...] = part.astype(o_ref.dtype)
        else:
            @pl.when(k == 0)
            def _():
                acc_ref[...] = part

            @pl.when(k > 0)
            def _():
                acc_ref[...] += part

            @pl.when(k == nk - 1)
            def _():
                o_ref[...] = acc_ref[...].astype(o_ref.dtype)

    return pl.pallas_call(
        body, name=name, out_shape=out_shape, grid=grid,
        in_specs=[x_spec, w_spec], out_specs=o_spec,
        scratch_shapes=[pltpu.VMEM(acc_shape if nk > 1 else (8, 128), F32)],
        compiler_params=_params(("parallel", "parallel", "arbitrary"), vmem),
    )(x, w)


def _weight_grad(name, xt, dy, bn, out_rows=None):
    m, t = xt.shape
    n = dy.shape[1]
    bm = m if out_rows is None else out_rows
    bk = _k_tile(t)
    return _matmul(
        name, xt, dy, jax.ShapeDtypeStruct((m, n), F32), (m // bm, n // bn, t // bk),
        pl.BlockSpec((bm, bk), lambda a, b, k: (a, k)),
        pl.BlockSpec((bk, bn), lambda a, b, k: (k, b)),
        pl.BlockSpec((bm, bn), lambda a, b, k: (a, b)), vmem=VMEM_BIG)


def _norm_fwd(name, h, g):
    t, d = h.shape
    tm = ROW_TILE

    def body(h_ref, g_ref, n_ref, nt_ref):
        x = h_ref[...]
        y = x * _rstd(x) * g_ref[...]
        n_ref[...] = y.astype(BF16)
        nt_ref[...] = y.T.astype(BF16)

    return pl.pallas_call(
        body, name=name,
        out_shape=[jax.ShapeDtypeStruct((t, d), BF16), jax.ShapeDtypeStruct((d, t), BF16)],
        grid=(t // tm,),
        in_specs=[pl.BlockSpec((tm, d), lambda i: (i, 0)), pl.BlockSpec((1, d), lambda i: (0, 0))],
        out_specs=[pl.BlockSpec((tm, d), lambda i: (i, 0)), pl.BlockSpec((d, tm), lambda i: (0, i))],
        compiler_params=_params(("parallel",)),
    )(h, g)


def _slot_of(kk):
    return (kk % 2) * 2 + kk // 2


def _ffn_in(name, n, w4):
    t, d = n.shape
    cw = w4.shape[2]
    tm = ROW_TILE

    def body(x_ref, wg_ref, wu_ref, ab_ref, s_ref, st_ref):
        x = x_ref[...]
        a = _dot(x, wg_ref[...])
        b = _dot(x, wu_ref[...])
        ab_ref[:, :cw] = a
        ab_ref[:, cw:] = b
        s = a * _sigmoid(a) * b
        s_ref[...] = s.astype(BF16)
        st_ref[...] = s.T.astype(BF16)

    return pl.pallas_call(
        body, name=name,
        out_shape=[jax.ShapeDtypeStruct((t, 4 * cw), F32), jax.ShapeDtypeStruct((t, 2 * cw), BF16),
                   jax.ShapeDtypeStruct((2 * cw, t), BF16)],
        grid=(2, t // tm),
        in_specs=[pl.BlockSpec((tm, d), lambda j, i: (i, 0)),
                  pl.BlockSpec((None, d, cw), lambda j, i: (j, 0, 0)),
                  pl.BlockSpec((None, d, cw), lambda j, i: (2 + j, 0, 0))],
        out_specs=[pl.BlockSpec((tm, 2 * cw), lambda j, i: (i, j)),
                   pl.BlockSpec((tm, cw), lambda j, i: (i, j)),
                   pl.BlockSpec((cw, tm), lambda j, i: (j, i))],
        compiler_params=_params(("parallel", "parallel"), VMEM_BIG),
    )(n, w4, w4)


def _mm_resid_norm(name, x, w, h, g_post, alpha, g_next):
    t, kdim = x.shape
    d = w.shape[1]
    tm = ROW_TILE
    bk = kdim if kdim <= 1024 else kdim // 2
    nk = kdim // bk
    with_next = g_next is not None

    def body(x_ref, w_ref, h_ref, gp_ref, gn_ref, f_ref, hn_ref, *rest):
        acc_ref = rest[-1]
        k = pl.program_id(1)
        part = _dot(x_ref[...], w_ref[...])

        @pl.when(k == 0)
        def _():
            acc_ref[...] = part

        @pl.when(k > 0)
        def _():
            acc_ref[...] += part

        @pl.when(k == nk - 1)
        def _():
            f = acc_ref[...]
            f_ref[...] = f
            hn = h_ref[...] + alpha * (f * _rstd(f) * gp_ref[...])
            hn_ref[...] = hn
            if with_next:
                y = hn * _rstd(hn) * gn_ref[...]
                rest[0][...] = y.astype(BF16)
                rest[1][...] = y.T.astype(BF16)

    row = lambda i, k: (i, 0)
    vec = pl.BlockSpec((1, d), lambda i, k: (0, 0))
    out_shape = [jax.ShapeDtypeStruct((t, d), F32), jax.ShapeDtypeStruct((t, d), F32)]
    out_specs = [pl.BlockSpec((tm, d), row), pl.BlockSpec((tm, d), row)]
    if with_next:
        out_shape += [jax.ShapeDtypeStruct((t, d), BF16), jax.ShapeDtypeStruct((d, t), BF16)]
        out_specs += [pl.BlockSpec((tm, d), row), pl.BlockSpec((d, tm), lambda i, k: (0, i))]
    return pl.pallas_call(
        body, name=name, out_shape=out_shape, grid=(t // tm, nk),
        in_specs=[pl.BlockSpec((tm, bk), lambda i, k: (i, k)), pl.BlockSpec((bk, d), lambda i, k: (k, 0)),
                  pl.BlockSpec((tm, d), row), vec, vec],
        out_specs=out_specs,
        scratch_shapes=[pltpu.VMEM((tm, d), F32)],
        compiler_params=_params(("parallel", "arbitrary"), VMEM_BIG),
    )(x, w, h, g_post, g_post if g_next is None else g_next)


def _gate_prep(z, b_pad, f_col):
    t = z.shape[0]
    tm = ROW_TILE

    def body(z_ref, b_ref, f_ref, carry_ref):
        i = pl.program_id(0)

        @pl.when(i == 0)
        def _():
            carry_ref[...] = jnp.zeros_like(carry_ref)

        xs = z_ref[...] + b_ref[...]
        logf = jnp.minimum(xs, 0.0) - jnp.log(1.0 + jnp.exp(-jnp.abs(xs)))
        row = i * tm + lax.broadcasted_iota(jnp.int32, (tm, 1), 0)
        logf = jnp.where(row >= ROW_PAD, logf, 0.0)
        tri = (lax.broadcasted_iota(jnp.int32, (tm, tm), 0) >= lax.broadcasted_iota(jnp.int32, (tm, tm), 1))
        f = jnp.dot(tri.astype(F32), logf, preferred_element_type=F32, precision=lax.Precision.HIGHEST)
        f = f + carry_ref[0:1, :]
        f_ref[...] = f
        carry_ref[...] = jnp.broadcast_to(f[tm - 1:tm, :], carry_ref.shape)

    return pl.pallas_call(
        body, name="forget_gate_cumsum", out_shape=jax.ShapeDtypeStruct((t, 128), F32),
        grid=(t // tm,),
        in_specs=[pl.BlockSpec((tm, 128), lambda i: (i, f_col // 128)), pl.BlockSpec((1, 128), lambda i: (0, 0))],
        out_specs=pl.BlockSpec((tm, 128), lambda i: (i, 0)),
        scratch_shapes=[pltpu.VMEM((8, 128), F32)],
        compiler_params=_params(("arbitrary",)),
    )(z, b_pad)


def _lane_halves():
    lane = lax.broadcasted_iota(jnp.int32, (1, 128), 1)
    return lane < HEAD_DIM


def _causal_mask(tq, tk, row0=0):
    row = row0 + lax.broadcasted_iota(jnp.int32, (tq, 1), 0)
    col = lax.broadcasted_iota(jnp.int32, (1, tk), 1)
    return col <= row


def _lane_one(lane):
    return (lax.broadcasted_iota(jnp.int32, (1, 128), 1) == lane).astype(BF16)


def _split3(x):
    hi = x.astype(BF16)
    rest = x - hi.astype(F32)
    mid = rest.astype(BF16)
    return hi, mid, (rest - mid.astype(F32)).astype(BF16)


def _split3_glue(x):
    hi = lax.reduce_precision(x, 8, 7)
    mid = lax.reduce_precision(x - hi, 8, 7)
    lo = lax.reduce_precision((x - hi) - mid, 8, 7)
    return hi.astype(BF16), mid.astype(BF16), lo.astype(BF16)


def _aug_pairs(cols):
    t = cols[0].shape[0]
    a = jnp.pad(jnp.stack(cols, axis=2), ((0, 0), (0, 0), (0, HEAD_DIM - len(cols))))
    a = a.reshape(t, 4, 2, HEAD_DIM)[:, :, ::-1, :]
    return jnp.transpose(a.reshape(t, 4, 128), (1, 0, 2))


def _attn_bias_operands(f_heads, lse_heads=None):
    t = f_heads.shape[0]
    one = jnp.ones((t, HEADS), BF16)
    row = lax.broadcasted_iota(jnp.int32, (t, 1), 0)
    fq = _split3_glue(f_heads)
    fk = _split3_glue(jnp.where(row < ROW_PAD, 1e9, f_heads))
    q_cols = list(fq) + [one] * 3
    k_cols = [one] * 3 + [-c for c in fk]
    if lse_heads is not None:
        q_cols += [-c for c in _split3_glue(lse_heads)]
        k_cols += [one] * 3
    return _aug_pairs(q_cols), _aug_pairs(k_cols)


def _attn_steps(nq, by_key):
    if by_key:
        pairs = [(qi, ki) for ki in range(nq) for qi in range(ki, nq)]
    else:
        pairs = [(qi, ki) for qi in range(nq) for ki in range(qi + 1)]
    return (jnp.array([p[0] for p in pairs], jnp.int32), jnp.array([p[1] for p in pairs], jnp.int32))


def _attn_fwd(z, aug_q, aug_k):
    t = z.shape[0]
    tq = tk = ROW_TILE
    nq = t // tq
    q_tab, k_tab = _attn_steps(nq, by_key=False)

    def body(qt_ref, kt_ref, q_ref, k_ref, v_ref, aq_ref, ak_ref, o_ref, lse_ref, m_ref, l_ref, acc_ref):
        step = pl.program_id(1)
        qi, ki = qt_ref[step], kt_ref[step]

        @pl.when(ki == 0)
        def _():
            m_ref[...] = jnp.full_like(m_ref, NEG)
            l_ref[...] = jnp.zeros_like(l_ref)
            acc_ref[...] = jnp.zeros_like(acc_ref)

        def sweep(diagonal):
            first = _lane_halves()
            q = (q_ref[...] * (HEAD_DIM ** -0.5)).astype(BF16)
            k = k_ref[...].astype(BF16)
            v = v_ref[...].astype(BF16)
            aq, ak = aq_ref[...], ak_ref[...]
            halves = (first, jnp.logical_not(first))
            qa = [jnp.where(lanes, q, aq) for lanes in halves]
            ka = [jnp.where(lanes, k, ak) for lanes in halves]
            va = [jnp.where(lanes, v, _lane_one(a0)) for lanes, a0 in zip(halves, (HEAD_DIM, 0))]
            chains = [(hh, r) for r in range(ATTN_ROW_PARTS) for hh in range(2)]
            rp = tq // ATTN_ROW_PARTS
            rows = [slice(r * rp, (r + 1) * rp) for _, r in chains]
            s = [_dot_nt(qa[hh][rw], ka[hh]) for (hh, _), rw in zip(chains, rows)]
            if diagonal:
                s = [jnp.where(_causal_mask(rp, tk, rw.start), s_c, NEG) for s_c, rw in zip(s, rows)]
            m_prev = [m_ref[rw, hh * HEAD_DIM:hh * HEAD_DIM + 1] for (hh, _), rw in zip(chains, rows)]
            m_new = [jnp.maximum(mp, jnp.max(s_c, axis=1, keepdims=True)) for mp, s_c in zip(m_prev, s)]
            p = [jnp.exp(s_c - m_c).astype(BF16) for s_c, m_c in zip(s, m_new)]
            pv = [_dot(p_c, va[hh]) for p_c, (hh, _) in zip(p, chains)]
            alpha = [jnp.exp(mp - m_c) for mp, m_c in zip(m_prev, m_new)]
            for r in range(ATTN_ROW_PARTS):
                (m0, m1), (al0, al1), (pv0, pv1) = [x[2 * r:2 * r + 2] for x in (m_new, alpha, pv)]
                rw = rows[2 * r]
                l0 = al0 * l_ref[rw, 0:1] + pv0[:, HEAD_DIM:HEAD_DIM + 1]
                l1 = al1 * l_ref[rw, HEAD_DIM:HEAD_DIM + 1] + pv1[:, 0:1]
                acc_ref[rw, :] = acc_ref[rw, :] * jnp.where(first, al0, al1) + jnp.where(first, pv0, pv1)
                m_ref[rw, :] = jnp.where(first, m0, m1)
                l_ref[rw, :] = jnp.where(first, l0, l1)

        @pl.when(ki < qi)
        def _():
            sweep(False)

        @pl.when(ki == qi)
        def _():
            sweep(True)
            o_ref[...] = acc_ref[...] / l_ref[...]
            lse_ref[...] = m_ref[...] + jnp.log(l_ref[...])

    return pl.pallas_call(
        body, name="attention_fwd",
        out_shape=[jax.ShapeDtypeStruct((t, ATTN_W), F32), jax.ShapeDtypeStruct((t, ATTN_W), F32)],
        grid_spec=pltpu.PrefetchScalarGridSpec(
            num_scalar_prefetch=2, grid=(4, int(q_tab.shape[0])),
            in_specs=[pl.BlockSpec((tq, 128), lambda p, s, qt, kt: (qt[s], p)),
                      pl.BlockSpec((tk, 128), lambda p, s, qt, kt: (kt[s], 4 + p)),
                      pl.BlockSpec((tk, 128), lambda p, s, qt, kt: (kt[s], 8 + p)),
                      pl.BlockSpec((None, tq, 128), lambda p, s, qt, kt: (p, qt[s], 0)),
                      pl.BlockSpec((None, tk, 128), lambda p, s, qt, kt: (p, kt[s], 0))],
            out_specs=[pl.BlockSpec((tq, 128), lambda p, s, qt, kt: (qt[s], p)),
                       pl.BlockSpec((tq, 128), lambda p, s, qt, kt: (qt[s], p))],
            scratch_shapes=[pltpu.VMEM((tq, 128), F32)] * 3),
        compiler_params=_params(("parallel", "arbitrary")),
    )(q_tab, k_tab, z, z, z, aug_q, aug_k)


def _attn_bwd(z, aug_q, aug_k, o, do):
    t = z.shape[0]
    tq = tk = ROW_TILE
    nq = t // tq
    q_tab, k_tab = _attn_steps(nq, by_key=True)
    tn = (((0,), (0,)), ((), ()))

    def body(qt_ref, kt_ref, q_ref, k_ref, v_ref, aq_ref, ak_ref, o_ref, do_ref,
             dq_ref, dk_ref, dv_ref, dfk_ref, dfq_ref):
        step = pl.program_id(1)
        qi, ki = qt_ref[step], kt_ref[step]
        rows = pl.ds(pl.multiple_of(qi * tq, tq), tq)

        @pl.when(ki == 0)
        def _():
            dq_ref[rows, :] = jnp.zeros((tq, 128), F32)
            dfq_ref[rows, :] = jnp.zeros((tq, 128), F32)

        @pl.when(qi == ki)
        def _():
            dk_ref[...] = jnp.zeros_like(dk_ref)
            dv_ref[...] = jnp.zeros_like(dv_ref)
            dfk_ref[...] = jnp.zeros_like(dfk_ref)

        def sweep(diagonal):
            first = _lane_halves()
            lane = lax.broadcasted_iota(jnp.int32, (1, 128), 1)
            scale = HEAD_DIM ** -0.5
            q = (q_ref[...] * scale).astype(BF16)
            k = k_ref[...].astype(BF16)
            v = v_ref[...].astype(BF16)
            do_ = do_ref[...]
            do16 = do_.astype(BF16)
            od = o_ref[...] * do_
            aq, ak = aq_ref[...], ak_ref[...]
            parts = []
            for hh in range(2):
                lanes = first if hh == 0 else jnp.logical_not(first)
                a0 = HEAD_DIM - hh * HEAD_DIM
                one = _lane_one(a0)
                d_hi, d_mid, d_lo = _split3(jnp.sum(jnp.where(lanes, od, 0.0), axis=1, keepdims=True))
                minus_delta = jnp.where(lane == a0, -d_hi, jnp.where(lane == a0 + 1, -d_mid,
                                        jnp.where(lane == a0 + 2, -d_lo, jnp.zeros((), BF16))))
                ones3 = ((lane >= a0) & (lane < a0 + 3)).astype(BF16)
                s = _dot_nt(jnp.where(lanes, q, aq), jnp.where(lanes, k, ak))
                p = jnp.exp(s)
                if diagonal:
                    p = jnp.where(_causal_mask(tq, tk), p, 0.0)
                dp = _dot_nt(jnp.where(lanes, do16, minus_delta), jnp.where(lanes, v, ones3))
                ds16 = (p * dp).astype(BF16)
                dv_h = lax.dot_general(p.astype(BF16), jnp.where(lanes, do16, jnp.zeros((), BF16)), tn,
                                       preferred_element_type=F32)
                dk_h = lax.dot_general(ds16, jnp.where(lanes, q, one), tn, preferred_element_type=F32)
                dq_h = _dot(ds16, jnp.where(lanes, k, one))
                parts.append((dq_h, dk_h, dv_h, a0))
            (dq0, dk0, dv0, a0), (dq1, dk1, dv1, a1) = parts
            dq_ref[rows, :] += jnp.where(first, dq0, dq1) * scale
            dfq_ref[rows, :] += jnp.where(first, dq0[:, a0:a0 + 1], dq1[:, a1:a1 + 1])
            dk_ref[...] += jnp.where(first, dk0, dk1)
            dfk_ref[...] += jnp.where(first, dk0[:, a0:a0 + 1], dk1[:, a1:a1 + 1])
            dv_ref[...] += dv0 + dv1

        @pl.when(qi > ki)
        def _():
            sweep(False)

        @pl.when(qi == ki)
        def _():
            sweep(True)

    qrow = lambda p, s, qt, kt: (qt[s], p)
    krow = lambda p, s, qt, kt: (kt[s], p)
    return pl.pallas_call(
        body, name="attention_bwd",
        out_shape=[jax.ShapeDtypeStruct((t, ATTN_W), F32)] * 5,
        grid_spec=pltpu.PrefetchScalarGridSpec(
            num_scalar_prefetch=2, grid=(4, int(q_tab.shape[0])),
            in_specs=[pl.BlockSpec((tq, 128), qrow),
                      pl.BlockSpec((tk, 128), lambda p, s, qt, kt: (kt[s], 4 + p)),
                      pl.BlockSpec((tk, 128), lambda p, s, qt, kt: (kt[s], 8 + p)),
                      pl.BlockSpec((None, tq, 128), lambda p, s, qt, kt: (p, qt[s], 0)),
                      pl.BlockSpec((None, tk, 128), lambda p, s, qt, kt: (p, kt[s], 0)),
                      pl.BlockSpec((tq, 128), qrow), pl.BlockSpec((tq, 128), qrow)],
            out_specs=[pl.BlockSpec((t, 128), lambda p, s, qt, kt: (0, p)),
                       pl.BlockSpec((tk, 128), krow), pl.BlockSpec((tk, 128), krow), pl.BlockSpec((tk, 128), krow),
                       pl.BlockSpec((t, 128), lambda p, s, qt, kt: (0, p))]),
        compiler_params=_params(("parallel", "arbitrary"), VMEM_BIG),
    )(q_tab, k_tab, z, z, z, aug_q, aug_k, o, do)


def _shifted(prev_rows, x, shift):
    tm = x.shape[0]
    return pltpu.roll(jnp.concatenate([prev_rows, x], axis=0), shift, 0)[8:8 + tm]


def _ahead(x, next_rows, shift):
    tm = x.shape[0]
    return pltpu.roll(jnp.concatenate([x, next_rows], axis=0), tm + 8 - shift, 0)[0:tm]


def _conv_specs(tm, nt, cols):
    tiles = [pl.BlockSpec((tm, CONV_W), functools.partial(lambda i, c: (i, c), c=c)) for c in cols]
    halos = [pl.BlockSpec((8, CONV_W), functools.partial(lambda i, c: (jnp.maximum(i * (tm // 8) - 1, 0), c), c=c))
             for c in cols]
    return tiles, halos


def _conv_gate(z, conv_w):
    t = z.shape[0]
    tm = ROW_TILE
    nt = t // tm

    def body(cb_ref, cc_ref, ci_ref, hc_ref, hi_ref, w_ref, g_ref, gt_ref):
        i = pl.program_id(0)
        cc = cc_ref[...] * ci_ref[...]
        prev = jnp.where(i > 0, hc_ref[...] * hi_ref[...], 0.0)
        conv = w_ref[0:1, :] * _shifted(prev, cc, 2) + w_ref[1:2, :] * _shifted(prev, cc, 1) + w_ref[2:3, :] * cc
        g = cb_ref[...] * conv
        g_ref[...] = g.astype(BF16)
        gt_ref[...] = g.T.astype(BF16)

    (cb, cc, ci), (_, hc, hi) = _conv_specs(tm, nt, (3, 4, 5))
    return pl.pallas_call(
        body, name="conv_gate_fwd",
        out_shape=[jax.ShapeDtypeStruct((t, CONV_W), BF16), jax.ShapeDtypeStruct((CONV_W, t), BF16)],
        grid=(nt,),
        in_specs=[cb, cc, ci, hc, hi, pl.BlockSpec((8, CONV_W), lambda i: (0, 0))],
        out_specs=[pl.BlockSpec((tm, CONV_W), lambda i: (i, 0)), pl.BlockSpec((CONV_W, tm), lambda i: (0, i))],
        compiler_params=_params(("parallel",)),
    )(z, z, z, z, z, conv_w)


def _conv_bwd(z, dg, conv_w):
    t = z.shape[0]
    tm = ROW_TILE
    nt = t // tm

    def body(cb_ref, cc_ref, ci_ref, hc_ref, hi_ref, dg_ref, ncb_ref, ndg_ref, w_ref, dz_ref, dw_ref):
        i = pl.program_id(0)

        @pl.when(i == 0)
        def _():
            dw_ref[...] = jnp.zeros_like(dw_ref)

        cb, c_c, c_in = cb_ref[...], cc_ref[...], ci_ref[...]
        cc = c_c * c_in
        prev = jnp.where(i > 0, hc_ref[...] * hi_ref[...], 0.0)
        cc1, cc2 = _shifted(prev, cc, 1), _shifted(prev, cc, 2)
        w0, w1, w2 = w_ref[0:1, :], w_ref[1:2, :], w_ref[2:3, :]
        conv = w0 * cc2 + w1 * cc1 + w2 * cc
        dgv = dg_ref[...]
        dconv = dgv * cb
        nxt = jnp.where(i < nt - 1, ndg_ref[...] * ncb_ref[...], 0.0)
        dcc = w2 * dconv + w1 * _ahead(dconv, nxt, 1) + w0 * _ahead(dconv, nxt, 2)
        dz_ref[:, 0:CONV_W] = (dgv * conv).astype(BF16)
        dz_ref[:, CONV_W:2 * CONV_W] = (dcc * c_in).astype(BF16)
        dz_ref[:, 2 * CONV_W:] = (dcc * c_c).astype(BF16)
        dw_ref[0:1, :] += jnp.sum(dconv * cc2, axis=0, keepdims=True)
        dw_ref[1:2, :] += jnp.sum(dconv * cc1, axis=0, keepdims=True)
        dw_ref[2:3, :] += jnp.sum(dconv * cc, axis=0, keepdims=True)

    (cb, cc, ci), (_, hc, hi) = _conv_specs(tm, nt, (3, 4, 5))
    nxt = lambda i, c: (jnp.minimum((i + 1) * (tm // 8), t // 8 - 1), c)
    return pl.pallas_call(
        body, name="conv_gate_bwd",
        out_shape=[jax.ShapeDtypeStruct((t, 3 * CONV_W), BF16), jax.ShapeDtypeStruct((8, CONV_W), F32)],
        grid=(nt,),
        in_specs=[cb, cc, ci, hc, hi, pl.BlockSpec((tm, CONV_W), lambda i: (i, 0)),
                  pl.BlockSpec((8, CONV_W), lambda i: nxt(i, 3)), pl.BlockSpec((8, CONV_W), lambda i: nxt(i, 0)),
                  pl.BlockSpec((8, CONV_W), lambda i: (0, 0))],
        out_specs=[pl.BlockSpec((tm, 3 * CONV_W), lambda i: (i, 0)), pl.BlockSpec((8, CONV_W), lambda i: (0, 0))],
        compiler_params=_params(("arbitrary",)),
    )(z, z, z, z, z, dg, z, dg, conv_w)


def _branch_mix(z, o, g, w_ab, w_cb, d):
    t = z.shape[0]
    tm = ROW_TILE
    ga_col = (3 * ATTN_W + 3 * CONV_W) // d

    def body(o_ref, g_ref, ga_ref, gc_ref, wa_ref, wc_ref, mp_ref, mpt_ref, ot_ref):
        o_ = o_ref[...]
        ya = _dot(o_.astype(BF16), wa_ref[...])
        yc = _dot(g_ref[...], wc_ref[...])
        mp = _sigmoid(ga_ref[...]) * ya + _sigmoid(gc_ref[...]) * yc
        mp_ref[...] = mp.astype(BF16)
        mpt_ref[...] = mp.T.astype(BF16)
        ot_ref[...] = o_.T.astype(BF16)

    return pl.pallas_call(
        body, name="branch_mix_fwd",
        out_shape=[jax.ShapeDtypeStruct((t, d), BF16), jax.ShapeDtypeStruct((d, t), BF16),
                   jax.ShapeDtypeStruct((ATTN_W, t), BF16)],
        grid=(t // tm,),
        in_specs=[pl.BlockSpec((tm, ATTN_W), lambda i: (i, 0)), pl.BlockSpec((tm, CONV_W), lambda i: (i, 0)),
                  pl.BlockSpec((tm, d), lambda i: (i, ga_col)), pl.BlockSpec((tm, d), lambda i: (i, ga_col + 1)),
                  pl.BlockSpec((ATTN_W, d), lambda i: (0, 0)), pl.BlockSpec((CONV_W, d), lambda i: (0, 0))],
        out_specs=[pl.BlockSpec((tm, d), lambda i: (i, 0)), pl.BlockSpec((d, tm), lambda i: (0, i)),
                   pl.BlockSpec((ATTN_W, tm), lambda i: (0, i))],
        compiler_params=_params(("parallel",), VMEM_BIG),
    )(o, g, z, z, w_ab, w_cb)


def _branch_bwd(z, o, g, dmixed, w_out, w_ab, w_cb, d):
    t = z.shape[0]
    tm = ROW_TILE // 2
    ga_col = (3 * ATTN_W + 3 * CONV_W) // d

    def body(dm_ref, o_ref, g_ref, ga_ref, gc_ref, wo_ref, wa_ref, wc_ref, dya_ref, dyc_ref, dgt_ref, do_ref, dg_ref):
        dmp = _dot_nt(dm_ref[...], wo_ref[...])
        ya = _dot(o_ref[...].astype(BF16), wa_ref[...])
        yc = _dot(g_ref[...], wc_ref[...])
        sa, sc = _sigmoid(ga_ref[...]), _sigmoid(gc_ref[...])
        dya = (dmp * sa).astype(BF16)
        dyc = (dmp * sc).astype(BF16)
        dya_ref[...] = dya
        dyc_ref[...] = dyc
        dgt_ref[:, :d] = (dmp * ya * sa * (1.0 - sa)).astype(BF16)
        dgt_ref[:, d:] = (dmp * yc * sc * (1.0 - sc)).astype(BF16)
        do_ref[...] = _dot_nt(dya, wa_ref[...])
        dg_ref[...] = _dot_nt(dyc, wc_ref[...])

    row = lambda i: (i, 0)
    fixed = lambda i: (0, 0)
    return pl.pallas_call(
        body, name="branch_mix_bwd",
        out_shape=[jax.ShapeDtypeStruct((t, d), BF16), jax.ShapeDtypeStruct((t, d), BF16),
                   jax.ShapeDtypeStruct((t, 2 * d), BF16), jax.ShapeDtypeStruct((t, ATTN_W), F32),
                   jax.ShapeDtypeStruct((t, CONV_W), F32)],
        grid=(t // tm,),
        in_specs=[pl.BlockSpec((tm, d), row), pl.BlockSpec((tm, ATTN_W), row), pl.BlockSpec((tm, CONV_W), row),
                  pl.BlockSpec((tm, d), lambda i: (i, ga_col)), pl.BlockSpec((tm, d), lambda i: (i, ga_col + 1)),
                  pl.BlockSpec((d, d), fixed), pl.BlockSpec((ATTN_W, d), fixed), pl.BlockSpec((CONV_W, d), fixed)],
        out_specs=[pl.BlockSpec((tm, d), row), pl.BlockSpec((tm, d), row), pl.BlockSpec((tm, 2 * d), row),
                   pl.BlockSpec((tm, ATTN_W), row), pl.BlockSpec((tm, CONV_W), row)],
        compiler_params=_params(("parallel",), VMEM_BIG),
    )(dmixed, o, g, z, z, w_out, w_ab, w_cb)


def _loss_grad(h, target_pad):
    t, d = h.shape
    tm = ROW_TILE

    def body(h_ref, t_ref, dy_ref, loss_ref):
        i = pl.program_id(0)

        @pl.when(i == 0)
        def _():
            loss_ref[...] = jnp.zeros_like(loss_ref)

        row = i * tm + lax.broadcasted_iota(jnp.int32, (tm, 1), 0)
        err = jnp.where(row >= N_FRONT, h_ref[...] - t_ref[...], 0.0)
        dy_ref[...] = err * (1.0 / d)
        per_row = jnp.sum(err * err, axis=1, keepdims=True) * (1.0 / d)
        loss_ref[...] += 0.5 * jnp.sum(per_row, axis=0, keepdims=True)

    return pl.pallas_call(
        body, name="loss_and_grad",
        out_shape=[jax.ShapeDtypeStruct((t, d), F32), jax.ShapeDtypeStruct((1, 128), F32)],
        grid=(t // tm,),
        in_specs=[pl.BlockSpec((tm, d), lambda i: (i, 0))] * 2,
        out_specs=[pl.BlockSpec((tm, d), lambda i: (i, 0)), pl.BlockSpec((1, 128), lambda i: (0, 0))],
        compiler_params=_params(("arbitrary",)),
    )(h, target_pad)


def _norm_bwd(name, x, g, dy, alpha):
    t, d = x.shape
    tm = ROW_TILE

    def body(x_ref, g_ref, dy_ref, dx_ref, dg_ref):
        @pl.when(pl.program_id(0) == 0)
        def _():
            dg_ref[...] = jnp.zeros_like(dg_ref)

        dx, dg = _rms_bwd(x_ref[...], g_ref[...], dy_ref[...])
        dx_ref[...] = (alpha * dx).astype(BF16)
        dg_ref[...] += alpha * dg

    row = pl.BlockSpec((tm, d), lambda i: (i, 0))
    vec = pl.BlockSpec((1, d), lambda i: (0, 0))
    return pl.pallas_call(
        body, name=name,
        out_shape=[jax.ShapeDtypeStruct((t, d), BF16), jax.ShapeDtypeStruct((1, d), F32)],
        grid=(t // tm,), in_specs=[row, vec, row], out_specs=[row, vec],
        compiler_params=_params(("arbitrary",)),
    )(x, g, dy)


def _ffn_bwd_mid(name, df, w_out, ab):
    t, d = df.shape
    cw = ab.shape[1] // 4
    tm = ROW_TILE

    def body(df_ref, w_ref, ab_ref, o_ref):
        ds = _dot_nt(df_ref[...], w_ref[...])
        a = ab_ref[:, :cw]
        b = ab_ref[:, cw:]
        sg = _sigmoid(a)
        o_ref[:, :cw] = (ds * b * (sg * (1.0 + a * (1.0 - sg)))).astype(BF16)
        o_ref[:, cw:] = (ds * (a * sg)).astype(BF16)

    return pl.pallas_call(
        body, name=name, out_shape=jax.ShapeDtypeStruct((t, 4 * cw), BF16),
        grid=(2, t // tm),
        in_specs=[pl.BlockSpec((tm, d), lambda j, i: (i, 0)), pl.BlockSpec((cw, d), lambda j, i: (j, 0)),
                  pl.BlockSpec((tm, 2 * cw), lambda j, i: (i, j))],
        out_specs=pl.BlockSpec((tm, 2 * cw), lambda j, i: (i, j)),
        compiler_params=_params(("parallel", "parallel"), VMEM_BIG),
    )(df, w_out, ab)


def _mm_nt_norm_bwd(name, dy, w, w_spec, bk, h, g, dh_in):
    t, kdim = dy.shape
    d = h.shape[1]
    tm = ROW_TILE
    nk = kdim // bk

    def body(dy_ref, w_ref, h_ref, g_ref, dhi_ref, dh_ref, dg_ref, acc_ref):
        i, k = pl.program_id(0), pl.program_id(1)

        @pl.when((i == 0) & (k == 0))
        def _():
            dg_ref[...] = jnp.zeros_like(dg_ref)

        part = _dot_nt(dy_ref[...], w_ref[...])

        @pl.when(k == 0)
        def _():
            acc_ref[...] = part

        @pl.when(k > 0)
        def _():
            acc_ref[...] += part

        @pl.when(k == nk - 1)
        def _():
            dx, dg = _rms_bwd(h_ref[...], g_ref[...], acc_ref[...])
            dh_ref[...] = dhi_ref[...] + dx
            dg_ref[...] += dg

    row = pl.BlockSpec((tm, d), lambda i, k: (i, 0))
    vec = pl.BlockSpec((1, d), lambda i, k: (0, 0))
    return pl.pallas_call(
        body, name=name,
        out_shape=[jax.ShapeDtypeStruct((t, d), F32), jax.ShapeDtypeStruct((1, d), F32)],
        grid=(t // tm, nk),
        in_specs=[pl.BlockSpec((tm, bk), lambda i, k: (i, k)), w_spec, row, vec, row],
        out_specs=[row, vec],
        scratch_shapes=[pltpu.VMEM((tm, d), F32)],
        compiler_params=_params(("arbitrary", "arbitrary"), VMEM_BIG),
    )(dy, w, h, g, dh_in)


def _gate_bwd(df_pad, z, b_pad, f_col):
    t = z.shape[0]
    tm = ROW_TILE
    nt = t // tm

    def body(d_ref, z_ref, b_ref, dz_ref, db_ref, carry_ref):
        i = pl.program_id(0)

        @pl.when(i == 0)
        def _():
            carry_ref[...] = jnp.zeros_like(carry_ref)
            db_ref[...] = jnp.zeros_like(db_ref)

        tri = (lax.broadcasted_iota(jnp.int32, (tm, tm), 0) <= lax.broadcasted_iota(jnp.int32, (tm, tm), 1))
        tail = jnp.dot(tri.astype(F32), d_ref[...], preferred_element_type=F32, precision=lax.Precision.HIGHEST)
        tail = tail + carry_ref[0:1, :]
        carry_ref[...] = jnp.broadcast_to(tail[0:1, :], carry_ref.shape)
        row = (nt - 1 - i) * tm + lax.broadcasted_iota(jnp.int32, (tm, 1), 0)
        dlogit = jnp.where(row >= ROW_PAD, tail * _sigmoid(-(z_ref[...] + b_ref[...])), 0.0)
        dz_ref[...] = jnp.zeros_like(dz_ref)
        dz_ref[:, 0:128] = dlogit.astype(BF16)
        db_ref[...] += jnp.sum(dlogit, axis=0, keepdims=True)

    rev = lambda i: (nt - 1 - i, 0)
    return pl.pallas_call(
        body, name="forget_gate_bwd",
        out_shape=[jax.ShapeDtypeStruct((t, F_PAD), BF16), jax.ShapeDtypeStruct((1, 128), F32)],
        grid=(nt,),
        in_specs=[pl.BlockSpec((tm, 128), rev), pl.BlockSpec((tm, 128), lambda i: (nt - 1 - i, f_col // 128)),
                  pl.BlockSpec((1, 128), lambda i: (0, 0))],
        out_specs=[pl.BlockSpec((tm, F_PAD), rev), pl.BlockSpec((1, 128), lambda i: (0, 0))],
        scratch_shapes=[pltpu.VMEM((8, 128), F32)],
        compiler_params=_params(("arbitrary",)),
    )(df_pad, z, b_pad)


def _ffn_fwd(tag, n, w_in4, w_out, h, g_post, g_next):
    ab, s, s_t = _ffn_in(f"{tag}_in_fwd", n, w_in4)
    outs = _mm_resid_norm(f"{tag}_out_fwd", s, w_out, h, g_post, 0.5, g_next)
    return ab, s_t, outs


def _ffn_bwd(tag, dh, f, g_post, ab, s_t, n_t, w_in4, w_out, h_in, g_pre):
    d, cw = w_in4.shape[1], w_in4.shape[2]
    t = dh.shape[0]
    df, dg_post = _norm_bwd(f"{tag}_post_norm_bwd", f, g_post, dh, 0.5)
    dw_out = _weight_grad(f"{tag}_dw_out", s_t, df, d, out_rows=cw // 2)
    dab = _ffn_bwd_mid(f"{tag}_mid_bwd", df, w_out, ab)
    dh_in, dg_pre = _mm_nt_norm_bwd(
        f"{tag}_in_bwd", dab, w_in4, pl.BlockSpec((None, d, cw), lambda i, k: (_slot_of(k), 0, 0)), cw, h_in, g_pre, dh)
    bk = _k_tile(t)
    dw_in = _matmul(
        f"{tag}_dw_in", n_t, dab, jax.ShapeDtypeStruct((4, d, cw), F32), (1, 4, t // bk),
        pl.BlockSpec((d, bk), lambda a, b, k: (0, k)), pl.BlockSpec((bk, cw), lambda a, b, k: (k, b)),
        pl.BlockSpec((None, d, cw), lambda a, b, k: (_slot_of(b), 0, 0)), vmem=VMEM_BIG)
    return dh_in, dg_post, dg_pre, dw_in, dw_out


def _pack_small(meta, conv, gains, b_forget):
    d = gains[0].shape[1]
    rows = [meta.reshape(4, d), jnp.pad(conv.reshape(1, 3 * 128), ((0, 0), (0, d - 3 * 128)))]
    rows += list(gains) + [jnp.pad(b_forget, ((0, 0), (0, d - HEADS)))]
    return jnp.concatenate(rows + [jnp.zeros((4, d), F32)], axis=0)


def _unpack_small(block):
    d = block.shape[1]
    meta = block[0:4].reshape(N_META, d // 4)
    conv = block[4, :3 * 128].reshape(1, 3, 128)
    gains = [block[5 + i:6 + i] for i in range(6)]
    return meta, conv, gains, block[11:12, :HEADS]


def kernel(x, meta_tokens, w_in, b_forget, conv_w, w_attn_branch, w_conv_branch, w_out, g_ffn1_pre, g_ffn1_post, w_ffn1_in, w_ffn1_out, g_mix_pre, g_mix_post, g_ffn2_pre, g_ffn2_post, w_ffn2_in, w_ffn2_out, loss_target, m_meta_tokens, m_w_in, m_b_forget, m_conv_w, m_w_attn_branch, m_w_conv_branch, m_w_out, m_g_ffn1_pre, m_g_ffn1_post, m_w_ffn1_in, m_w_ffn1_out, m_g_mix_pre, m_g_mix_post, m_g_ffn2_pre, m_g_ffn2_post, m_w_ffn2_in, m_w_ffn2_out, v_meta_tokens, v_w_in, v_b_forget, v_conv_w, v_w_attn_branch, v_w_conv_branch, v_w_out, v_g_ffn1_pre, v_g_ffn1_post, v_w_ffn1_in, v_w_ffn1_out, v_g_mix_pre, v_g_mix_post, v_g_ffn2_pre, v_g_ffn2_post, v_w_ffn2_in, v_w_ffn2_out):
    seq, d = x.shape[1], x.shape[2]
    t = seq + N_FRONT
    n_main = 3 * ATTN_W + 3 * CONV_W + 2 * d
    nz = n_main + F_PAD
    f_lo = 3 * ATTN_W
    c_arr = lax.axis_index("c").astype(jnp.int32).reshape(1)

    big = [w_in[0], w_attn_branch[0], w_conv_branch[0], w_out[0], w_ffn1_in[0], w_ffn1_out[0], w_ffn2_in[0], w_ffn2_out[0]]
    small_gather = jnp.concatenate(
        [meta_tokens.reshape(4, d), jnp.pad(conv_w.reshape(1, 3 * 128), ((0, 0), (0, d - 3 * 128))),
         jnp.zeros((11, d), F32)], axis=0)
    gathered = _all_gather([w.astype(BF16) for w in big] + [small_gather])
    w_in4, w_ab4, w_cb4, w_out4, w_f1_in4, w_f1_out4, w_f2_in4, w_f2_out4, small4 = gathered

    w_in_full = jnp.transpose(w_in4, (1, 0, 2)).reshape(d, 4 * w_in4.shape[2])
    w_in_pad = jnp.concatenate(
        [w_in_full[:, :f_lo], w_in_full[:, f_lo + HEADS:], w_in_full[:, f_lo:f_lo + HEADS],
         jnp.zeros((d, F_PAD - HEADS), BF16)], axis=1)
    w_ab = jnp.transpose(w_ab4, (1, 0, 2)).reshape(ATTN_W, d)
    w_cb = jnp.transpose(w_cb4, (1, 0, 2)).reshape(CONV_W, d)
    w_out_full = w_out4.reshape(d, d)
    w_f1_out = w_f1_out4.reshape(-1, d)
    w_f2_out = w_f2_out4.reshape(-1, d)
    meta_full = jnp.transpose(small4[:, 0:4].reshape(4, N_META, d // 4), (1, 0, 2)).reshape(N_META, d)
    conv_full = jnp.transpose(small4[:, 4, :3 * 128].reshape(4, 3, 128), (1, 0, 2)).reshape(3, CONV_W)
    conv_pad = jnp.pad(conv_full, ((0, 5), (0, 0)))
    b_pad = jnp.pad(b_forget, ((0, 0), (0, 128 - HEADS)))

    h0 = jnp.concatenate([jnp.zeros((ROW_PAD, d), F32), meta_full, x[0]], axis=0)
    target_pad = jnp.concatenate([jnp.zeros((N_FRONT, d), F32), loss_target[0]], axis=0)
    n1, n1_t = _norm_fwd("ffn1_pre_norm", h0, g_ffn1_pre)
    ab1, s1_t, (f1, h1, u, u_t) = _ffn_fwd("ffn1", n1, w_f1_in4, w_f1_out, h0, g_ffn1_post, g_mix_pre)
    z = _matmul(
        "mix_in_proj", u, w_in_pad, jax.ShapeDtypeStruct((t, nz), F32), (nz // 512, t // ROW_TILE, 1),
        pl.BlockSpec((ROW_TILE, d), lambda j, i, k: (i, 0)), pl.BlockSpec((d, 512), lambda j, i, k: (0, j)),
        pl.BlockSpec((ROW_TILE, 512), lambda j, i, k: (i, j)))
    f_cum = _gate_prep(z, b_pad, n_main)
    f_heads = f_cum[:, :HEADS]
    o, lse = _attn_fwd(z, *_attn_bias_operands(f_heads))
    g, g_t = _conv_gate(z, conv_pad)
    mp, mp_t, o_t = _branch_mix(z, o, g, w_ab, w_cb, d)
    mixed, h2, n2, n2_t = _mm_resid_norm("mix_out_fwd", mp, w_out_full, h1, g_mix_post, 1.0, g_ffn2_pre)
    ab2, s2_t, (f2, h3) = _ffn_fwd("ffn2", n2, w_f2_in4, w_f2_out, h2, g_ffn2_post, None)
    dh3, loss_part = _loss_grad(h3, target_pad)
    loss = lax.psum(loss_part[0, 0], ("x", "y", "c"))

    dh2, dg_f2_post, dg_f2_pre, dw_f2_in, dw_f2_out = _ffn_bwd(
        "ffn2", dh3, f2, g_ffn2_post, ab2, s2_t, n2_t, w_f2_in4, w_f2_out, h2, g_ffn2_pre)
    dmixed, dg_mix_post = _norm_bwd("mix_post_norm_bwd", mixed, g_mix_post, dh2, 1.0)
    dw_out = _weight_grad("mix_dw_out", mp_t, dmixed, d)
    dya, dyc, dgates, do, dgconv = _branch_bwd(z, o, g, dmixed, w_out_full, w_ab, w_cb, d)
    dw_ab = _weight_grad("mix_dw_attn_branch", o_t, dya, d)
    dw_cb = _weight_grad("mix_dw_conv_branch", g_t, dyc, d)
    dz_conv, dconv_w = _conv_bwd(z, dgconv, conv_pad)
    front = lax.broadcasted_iota(jnp.int32, (t, 1), 0) < ROW_PAD
    lse_heads = jnp.where(front, 1e9, lse[:, ::HEAD_DIM])
    dq, dk, dv, dfk, dfq = _attn_bwd(z, *_attn_bias_operands(f_heads, lse_heads), o, do)
    df_pad = jnp.pad(dfq[:, ::HEAD_DIM] - dfk[:, ::HEAD_DIM], ((0, 0), (0, 128 - HEADS)))
    dz_f, db_forget = _gate_bwd(df_pad, z, b_pad, n_main)
    dz = jnp.concatenate([dq.astype(BF16), dk.astype(BF16), dv.astype(BF16), dz_conv, dgates, dz_f], axis=1)
    dh1, dg_mix_pre = _mm_nt_norm_bwd(
        "mix_in_bwd", dz, w_in_pad, pl.BlockSpec((d, 512), lambda i, k: (0, k)), 512, h1, g_mix_pre, dh2)
    dw_in_pad = _weight_grad("mix_dw_in", u_t, dz, 512)
    dh0, dg_f1_post, dg_f1_pre, dw_f1_in, dw_f1_out = _ffn_bwd(
        "ffn1", dh1, f1, g_ffn1_post, ab1, s1_t, n1_t, w_f1_in4, w_f1_out, h0, g_ffn1_pre)
    grad_x = dh0[N_FRONT:][None]
    dmeta = dh0[ROW_PAD:N_FRONT]

    cs = w_in4.shape[2]
    dw_in_full = jnp.concatenate(
        [dw_in_pad[:, :f_lo], dw_in_pad[:, n_main:n_main + HEADS], dw_in_pad[:, f_lo:n_main]], axis=1)
    small_grad = jnp.stack([
        _pack_small(dmeta[:, j * (d // 4):(j + 1) * (d // 4)], dconv_w[:3, j * 128:(j + 1) * 128],
                    [dg_f1_pre, dg_f1_post, dg_mix_pre, dg_mix_post, dg_f2_pre, dg_f2_post], db_forget[:, :HEADS])
        for j in range(4)])
    slots = [
        jnp.transpose(dw_in_full.reshape(d, 4, cs), (1, 0, 2)),
        jnp.transpose(dw_ab.reshape(ATTN_W, 4, d // 4), (1, 0, 2)),
        jnp.transpose(dw_cb.reshape(CONV_W, 4, d // 4), (1, 0, 2)),
        dw_out.reshape(4, d // 4, d),
        dw_f1_in, dw_f1_out.reshape(4, -1, d), dw_f2_in, dw_f2_out.reshape(4, -1, d),
        small_grad,
    ]
    tags = ["w_in", "w_attn_branch", "w_conv_branch", "w_out", "w_ffn1_in", "w_ffn1_out", "w_ffn2_in", "w_ffn2_out", "small"]

    got = _pair_send_halves(slots)
    pair_sums = [_pair_add(tag, s, a, c_arr, F32 if tag == "small" else BF16) for tag, s, a in zip(tags, slots, got)]
    arrived = _chip_scatter(pair_sums)
    halves = [_chip_add(tag, a) for tag, a in zip(tags, arrived)]
    grads = _pair_join_halves(halves)

    small = [g_ffn1_pre, g_ffn1_post, g_mix_pre, g_mix_post, g_ffn2_pre, g_ffn2_post]
    small_m = [m_g_ffn1_pre, m_g_ffn1_post, m_g_mix_pre, m_g_mix_post, m_g_ffn2_pre, m_g_ffn2_post]
    small_v = [v_g_ffn1_pre, v_g_ffn1_post, v_g_mix_pre, v_g_mix_post, v_g_ffn2_pre, v_g_ffn2_post]
    ws = big + [_pack_small(meta_tokens, conv_w[0], small, b_forget)]
    ms = [m_w_in[0], m_w_attn_branch[0], m_w_conv_branch[0], m_w_out[0], m_w_ffn1_in[0], m_w_ffn1_out[0],
          m_w_ffn2_in[0], m_w_ffn2_out[0], _pack_small(m_meta_tokens, m_conv_w[0], small_m, m_b_forget)]
    vs = [v_w_in[0], v_w_attn_branch[0], v_w_conv_branch[0], v_w_out[0], v_w_ffn1_in[0], v_w_ffn1_out[0],
          v_w_ffn2_in[0], v_w_ffn2_out[0], _pack_small(v_meta_tokens, v_conv_w[0], small_v, v_b_forget)]
    updates = [_adamw(tag, w, g_, m, v) for tag, w, g_, m, v in zip(tags, ws, grads, ms, vs)]

    def leaves(big_vals, small_block):
        meta, conv, gains, bf = _unpack_small(small_block)
        w_in_, w_ab_, w_cb_, w_out_, f1_in, f1_out, f2_in, f2_out = [b[None] for b in big_vals]
        return [meta, w_in_, bf, conv, w_ab_, w_cb_, w_out_, gains[0], gains[1], f1_in, f1_out,
                gains[2], gains[3], gains[4], gains[5], f2_in, f2_out]

    out_g = leaves(grads[:8], grads[8])
    out_d = leaves([u_[0] for u_ in updates[:8]], updates[8][0])
    out_m = leaves([u_[1] for u_ in updates[:8]], updates[8][1])
    out_v = leaves([u_[2] for u_ in updates[:8]], updates[8][2])
    return (loss, grad_x, *out_g, *out_d, *out_m, *out_v)
```

```python
import functools

import jax
import jax.numpy as jnp
from jax import lax
from jax.experimental import pallas as pl
from jax.experimental.pallas import tpu as pltpu

N_META = 16
ROW_PAD = 112
N_FRONT = ROW_PAD + N_META
HEADS = 8
HEAD_DIM = 64
ATTN_W = HEADS * HEAD_DIM
CONV_W = 512
NORM_EPS = 1e-6
ROW_TILE = 640
F_PAD = 512
ATTN_ROW_PARTS = 1
NEG = -1e30
ADAM_LR = 0.001
ADAM_B1 = 0.9
ADAM_B2 = 0.999
ADAM_EPS = 1e-08
ADAM_WD = 0.01
ADAM_STEP = 10
VMEM_BIG = 56 * 1024 * 1024
MESH = pl.DeviceIdType.MESH
ANY = pl.BlockSpec(memory_space=pl.ANY)
F32 = jnp.float32
BF16 = jnp.bfloat16


def _params(sem, vmem=None):
    return pltpu.CompilerParams(dimension_semantics=sem, vmem_limit_bytes=vmem)


def _sigmoid(x):
    return 1.0 / (1.0 + jnp.exp(-x))


def _rstd(x):
    return lax.rsqrt(jnp.mean(x * x, axis=-1, keepdims=True) + NORM_EPS)


def _rms_bwd(x, g, dy):
    r = _rstd(x)
    xr = x * r
    gdy = g * dy
    dx = r * (gdy - xr * jnp.mean(xr * gdy, axis=-1, keepdims=True))
    return dx, jnp.sum(dy * xr, axis=0, keepdims=True)


def _dot(a, b):
    return jnp.dot(a, b, preferred_element_type=F32)


def _dot_nt(a, b):
    return lax.dot_general(a, b, (((1,), (1,)), ((), ())), preferred_element_type=F32)


def _k_tile(t):
    return 1664 if t % 1664 == 0 else ROW_TILE


def _place():
    x, y, c = lax.axis_index("x"), lax.axis_index("y"), lax.axis_index("c")
    chips = [(1 - x, y), (x, 1 - y), (1 - x, 1 - y)]
    return x, y, c, chips


def _all_gather(shards):
    n = len(shards)
    split = [s.reshape(2, s.shape[0] // 2, s.shape[1]) for s in shards]

    def body(*refs):
        ins, outs = refs[:n], refs[n:2 * n]
        send_sems, recv_sems, local_sems = refs[2 * n:]
        x, y, c, chips = _place()
        me = 2 * x + y
        sibling = (x, y, 1 - c)

        def remote(i, k, slot, part, to, src=None):
            dst = outs[i].at[slot, part]
            return pltpu.make_async_remote_copy(
                src_ref=dst if src is None else src, dst_ref=dst,
                send_sem=send_sems.at[i, k], recv_sem=recv_sems.at[i, k],
                device_id=to, device_id_type=MESH)

        started = []
        local = []
        for i in range(n):
            cp = pltpu.make_async_copy(ins[i], outs[i].at[me], local_sems.at[i])
            cp.start()
            local.append(cp)
            for k, (cx, cy) in enumerate(chips):
                cp = remote(i, k, me, c, (cx, cy, c), src=ins[i].at[c])
                cp.start()
                started.append(cp)
        for i in range(n):
            for k, (cx, cy) in enumerate(chips):
                remote(i, k, 2 * cx + cy, c, (x, y, c)).wait_recv()
                cp = remote(i, 3 + k, 2 * cx + cy, c, sibling)
                cp.start()
                started.append(cp)
        for i in range(n):
            for k, (cx, cy) in enumerate(chips):
                remote(i, 3 + k, 2 * cx + cy, 1 - c, (x, y, c)).wait_recv()
        for cp in started:
            cp.wait_send()
        for cp in local:
            cp.wait()

    outs = pl.pallas_call(
        body, name="all_gather_weights",
        out_shape=[jax.ShapeDtypeStruct((4,) + s.shape, s.dtype) for s in split],
        in_specs=[ANY] * n, out_specs=[ANY] * n,
        scratch_shapes=[pltpu.SemaphoreType.DMA((n, 6)), pltpu.SemaphoreType.DMA((n, 6)),
                        pltpu.SemaphoreType.DMA((n,))],
    )(*split)
    return [o.reshape((4,) + s.shape) for o, s in zip(outs, shards)]


def _pair_send_halves(grads):
    n = len(grads)

    def body(*refs):
        ins, outs = refs[:n], refs[n:2 * n]
        send_sems, recv_sems = refs[2 * n:]
        x, y, c, _ = _place()
        cps = []
        for i in range(n):
            half = ins[i].shape[1] // 2
            cp = pltpu.make_async_remote_copy(
                src_ref=ins[i].at[:, pl.ds((1 - c) * half, half)], dst_ref=outs[i],
                send_sem=send_sems.at[i], recv_sem=recv_sems.at[i],
                device_id=(x, y, 1 - c), device_id_type=MESH)
            cp.start()
            cps.append(cp)
        for cp in cps:
            cp.wait()

    return pl.pallas_call(
        body, name="grad_pair_exchange",
        out_shape=[jax.ShapeDtypeStruct((4, g.shape[1] // 2, g.shape[2]), g.dtype) for g in grads],
        in_specs=[ANY] * n, out_specs=[ANY] * n,
        scratch_shapes=[pltpu.SemaphoreType.DMA((n,)), pltpu.SemaphoreType.DMA((n,))],
    )(*grads)


def _chip_scatter(parts):
    n = len(parts)

    def body(*refs):
        ins, outs = refs[:n], refs[n:2 * n]
        send_sems, recv_sems, local_sems = refs[2 * n:]
        x, y, c, chips = _place()
        me = 2 * x + y
        sends, local = [], []
        for i in range(n):
            cp = pltpu.make_async_copy(ins[i].at[me], outs[i].at[me], local_sems.at[i])
            cp.start()
            local.append(cp)
            for k, (cx, cy) in enumerate(chips):
                cp = pltpu.make_async_remote_copy(
                    src_ref=ins[i].at[2 * cx + cy], dst_ref=outs[i].at[me],
                    send_sem=send_sems.at[i, k], recv_sem=recv_sems.at[i, k],
                    device_id=(cx, cy, c), device_id_type=MESH)
                cp.start()
                sends.append(cp)
        for i in range(n):
            for k, (cx, cy) in enumerate(chips):
                got = outs[i].at[2 * cx + cy]
                pltpu.make_async_remote_copy(
                    src_ref=got, dst_ref=got, send_sem=send_sems.at[i, k], recv_sem=recv_sems.at[i, k],
                    device_id=(x, y, c), device_id_type=MESH).wait_recv()
        for cp in sends:
            cp.wait_send()
        for cp in local:
            cp.wait()

    return pl.pallas_call(
        body, name="grad_chip_scatter",
        out_shape=[jax.ShapeDtypeStruct(p.shape, p.dtype) for p in parts],
        in_specs=[ANY] * n, out_specs=[ANY] * n,
        scratch_shapes=[pltpu.SemaphoreType.DMA((n, 3)), pltpu.SemaphoreType.DMA((n, 3)),
                        pltpu.SemaphoreType.DMA((n,))],
    )(*parts)


def _pair_join_halves(halves):
    n = len(halves)

    def body(*refs):
        ins, outs = refs[:n], refs[n:2 * n]
        send_sems, recv_sems, local_sems = refs[2 * n:]
        x, y, c, _ = _place()
        cps, local = [], []
        for i in range(n):
            cp = pltpu.make_async_copy(ins[i], outs[i].at[c], local_sems.at[i])
            cp.start()
            local.append(cp)
            cp = pltpu.make_async_remote_copy(
                src_ref=ins[i], dst_ref=outs[i].at[c],
                send_sem=send_sems.at[i], recv_sem=recv_sems.at[i],
                device_id=(x, y, 1 - c), device_id_type=MESH)
            cp.start()
            cps.append(cp)
        for i in range(n):
            theirs = outs[i].at[1 - c]
            pltpu.make_async_remote_copy(
                src_ref=theirs, dst_ref=theirs, send_sem=send_sems.at[i], recv_sem=recv_sems.at[i],
                device_id=(x, y, c), device_id_type=MESH).wait_recv()
        for cp in cps:
            cp.wait_send()
        for cp in local:
            cp.wait()

    outs = pl.pallas_call(
        body, name="grad_pair_join",
        out_shape=[jax.ShapeDtypeStruct((2,) + h.shape, h.dtype) for h in halves],
        in_specs=[ANY] * n, out_specs=[ANY] * n,
        scratch_shapes=[pltpu.SemaphoreType.DMA((n,)), pltpu.SemaphoreType.DMA((n,)),
                        pltpu.SemaphoreType.DMA((n,))],
    )(*halves)
    return [o.reshape(2 * h.shape[0], h.shape[1]) for o, h in zip(outs, halves)]


def _row_block(rows, cols, n_bufs, budget=20 * 1024 * 1024):
    best = min(rows, 16)
    for b in range(16, rows + 1, 16):
        if rows % b == 0 and 2 * n_bufs * b * cols * 4 <= budget:
            best = b
    return best


def _pair_add(tag, grad, got, c_arr, out_dtype):
    _, rows, cols = grad.shape
    half = rows // 2
    bh = _row_block(half, cols, 3)
    nb = half // bh

    def body(c_ref, g_ref, a_ref, o_ref):
        o_ref[...] = (g_ref[...] + a_ref[...]).astype(out_dtype)

    return pl.pallas_call(
        body, name=f"pair_add_{tag}",
        out_shape=jax.ShapeDtypeStruct((4, half, cols), out_dtype),
        grid_spec=pltpu.PrefetchScalarGridSpec(
            num_scalar_prefetch=1, grid=(4, nb),
            in_specs=[pl.BlockSpec((None, bh, cols), lambda j, r, c: (j, c[0] * nb + r, 0)),
                      pl.BlockSpec((None, bh, cols), lambda j, r, c: (j, r, 0))],
            out_specs=pl.BlockSpec((None, bh, cols), lambda j, r, c: (j, r, 0))),
        compiler_params=_params(("parallel", "parallel")),
    )(c_arr, grad, got)


def _chip_add(tag, parts):
    _, half, cols = parts.shape
    bh = _row_block(half, cols, 5)

    def body(p_ref, o_ref):
        a, b, c, d = [p_ref[j].astype(F32) for j in range(4)]
        o_ref[...] = ((a + b) + c) + d

    return pl.pallas_call(
        body, name=f"chip_add_{tag}",
        out_shape=jax.ShapeDtypeStruct((half, cols), F32),
        grid=(half // bh,),
        in_specs=[pl.BlockSpec((4, bh, cols), lambda r: (0, r, 0))],
        out_specs=pl.BlockSpec((bh, cols), lambda r: (r, 0)),
        compiler_params=_params(("parallel",)),
    )(parts)


def _adamw(tag, w, g, m, v):
    rows, cols = w.shape
    br = _row_block(rows, cols, 7)

    def body(w_ref, g_ref, m_ref, v_ref, d_ref, mo_ref, vo_ref):
        g = g_ref[...]
        m_new = ADAM_B1 * m_ref[...] + (1.0 - ADAM_B1) * g
        v_new = ADAM_B2 * v_ref[...] + (1.0 - ADAM_B2) * (g * g)
        m_hat = m_new / (1.0 - ADAM_B1 ** ADAM_STEP)
        v_hat = v_new / (1.0 - ADAM_B2 ** ADAM_STEP)
        d_ref[...] = -ADAM_LR * (m_hat / (jnp.sqrt(v_hat) + ADAM_EPS) + ADAM_WD * w_ref[...])
        mo_ref[...] = m_new
        vo_ref[...] = v_new

    spec = pl.BlockSpec((br, cols), lambda r: (r, 0))
    return pl.pallas_call(
        body, name=f"adamw_{tag}",
        out_shape=[jax.ShapeDtypeStruct((rows, cols), F32)] * 3,
        grid=(rows // br,), in_specs=[spec] * 4, out_specs=[spec] * 3,
        compiler_params=_params(("parallel",)),
    )(w, g, m, v)


def _matmul(name, x, w, out_shape, grid, x_spec, w_spec, o_spec, *, nt=False, vmem=None):
    nk = grid[2]
    acc_shape = tuple(d for d in o_spec.block_shape if d is not None)

    def body(x_ref, w_ref, o_ref, acc_ref):
        k = pl.program_id(2)
        part = _dot_nt(x_ref[...], w_ref[...]) if nt else _dot(x_ref[...], w_ref[...])
        if nk == 1:
            o_ref[...] = part.astype(o_ref.dtype)
        else:
            @pl.when(k == 0)
            def _():
                acc_ref[...] = part

            @pl.when(k > 0)
            def _():
                acc_ref[...] += part

            @pl.when(k == nk - 1)
            def _():
                o_ref[...] = acc_ref[...].astype(o_ref.dtype)

    return pl.pallas_call(
        body, name=name, out_shape=out_shape, grid=grid,
        in_specs=[x_spec, w_spec], out_specs=o_spec,
        scratch_shapes=[pltpu.VMEM(acc_shape if nk > 1 else (8, 128), F32)],
        compiler_params=_params(("parallel", "parallel", "arbitrary"), vmem),
    )(x, w)


def _weight_grad(name, xt, dy, bn, out_rows=None):
    m, t = xt.shape
    n = dy.shape[1]
    bm = m if out_rows is None else out_rows
    bk = _k_tile(t)
    return _matmul(
        name, xt, dy, jax.ShapeDtypeStruct((m, n), F32), (m // bm, n // bn, t // bk),
        pl.BlockSpec((bm, bk), lambda a, b, k: (a, k)),
        pl.BlockSpec((bk, bn), lambda a, b, k: (k, b)),
        pl.BlockSpec((bm, bn), lambda a, b, k: (a, b)), vmem=VMEM_BIG)


def _norm_fwd(name, h, g):
    t, d = h.shape
    tm = ROW_TILE

    def body(h_ref, g_ref, n_ref, nt_ref):
        x = h_ref[...]
        y = x * _rstd(x) * g_ref[...]
        n_ref[...] = y.astype(BF16)
        nt_ref[...] = y.T.astype(BF16)

    return pl.pallas_call(
        body, name=name,
        out_shape=[jax.ShapeDtypeStruct((t, d), BF16), jax.ShapeDtypeStruct((d, t), BF16)],
        grid=(t // tm,),
        in_specs=[pl.BlockSpec((tm, d), lambda i: (i, 0)), pl.BlockSpec((1, d), lambda i: (0, 0))],
        out_specs=[pl.BlockSpec((tm, d), lambda i: (i, 0)), pl.BlockSpec((d, tm), lambda i: (0, i))],
        compiler_params=_params(("parallel",)),
    )(h, g)


def _slot_of(kk):
    return (kk % 2) * 2 + kk // 2


def _ffn_in(name, n, w4):
    t, d = n.shape
    cw = w4.shape[2]
    tm = ROW_TILE

    def body(x_ref, wg_ref, wu_ref, ab_ref, s_ref, st_ref):
        x = x_ref[...]
        a = _dot(x, wg_ref[...])
        b = _dot(x, wu_ref[...])
        ab_ref[:, :cw] = a
        ab_ref[:, cw:] = b
        s = a * _sigmoid(a) * b
        s_ref[...] = s.astype(BF16)
        st_ref[...] = s.T.astype(BF16)

    return pl.pallas_call(
        body, name=name,
        out_shape=[jax.ShapeDtypeStruct((t, 4 * cw), F32), jax.ShapeDtypeStruct((t, 2 * cw), BF16),
                   jax.ShapeDtypeStruct((2 * cw, t), BF16)],
        grid=(2, t // tm),
        in_specs=[pl.BlockSpec((tm, d), lambda j, i: (i, 0)),
                  pl.BlockSpec((None, d, cw), lambda j, i: (j, 0, 0)),
                  pl.BlockSpec((None, d, cw), lambda j, i: (2 + j, 0, 0))],
        out_specs=[pl.BlockSpec((tm, 2 * cw), lambda j, i: (i, j)),
                   pl.BlockSpec((tm, cw), lambda j, i: (i, j)),
                   pl.BlockSpec((cw, tm), lambda j, i: (j, i))],
        compiler_params=_params(("parallel", "parallel"), VMEM_BIG),
    )(n, w4, w4)


def _mm_resid_norm(name, x, w, h, g_post, alpha, g_next):
    t, kdim = x.shape
    d = w.shape[1]
    tm = ROW_TILE
    with_next = g_next is not None

    def body(x_ref, w_ref, h_ref, gp_ref, gn_ref, f_ref, hn_ref, *rest):
        f = _dot(x_ref[...], w_ref[...])
        f_ref[...] = f
        hn = h_ref[...] + alpha * (f * _rstd(f) * gp_ref[...])
        hn_ref[...] = hn
        if with_next:
            y = hn * _rstd(hn) * gn_ref[...]
            rest[0][...] = y.astype(BF16)
            rest[1][...] = y.T.astype(BF16)

    row = lambda i: (i, 0)
    vec = pl.BlockSpec((1, d), lambda i: (0, 0))
    out_shape = [jax.ShapeDtypeStruct((t, d), F32), jax.ShapeDtypeStruct((t, d), F32)]
    out_specs = [pl.BlockSpec((tm, d), row), pl.BlockSpec((tm, d), row)]
    if with_next:
        out_shape += [jax.ShapeDtypeStruct((t, d), BF16), jax.ShapeDtypeStruct((d, t), BF16)]
        out_specs += [pl.BlockSpec((tm, d), row), pl.BlockSpec((d, tm), lambda i: (0, i))]
    return pl.pallas_call(
        body, name=name, out_shape=out_shape, grid=(t // tm,),
        in_specs=[pl.BlockSpec((tm, kdim), row), pl.BlockSpec((kdim, d), lambda i: (0, 0)),
                  pl.BlockSpec((tm, d), row), vec, vec],
        out_specs=out_specs,
        compiler_params=_params(("parallel",), VMEM_BIG),
    )(x, w, h, g_post, g_post if g_next is None else g_next)


def _in_proj(u, w):
    t, d = u.shape
    nz = w.shape[1]
    nq = 3 * ATTN_W
    tm = ROW_TILE // 2

    def body(u_ref, w_ref, qkv_ref, z_ref):
        qkv_ref[...] = _dot(u_ref[...], w_ref[:, 0:nq]).astype(BF16)
        z_ref[...] = _dot(u_ref[...], w_ref[:, nq:])

    return pl.pallas_call(
        body, name="mix_in_proj",
        out_shape=[jax.ShapeDtypeStruct((t, nq), BF16), jax.ShapeDtypeStruct((t, nz - nq), F32)],
        grid=(t // tm,),
        in_specs=[pl.BlockSpec((tm, d), lambda i: (i, 0)), pl.BlockSpec((d, nz), lambda i: (0, 0))],
        out_specs=[pl.BlockSpec((tm, nq), lambda i: (i, 0)), pl.BlockSpec((tm, nz - nq), lambda i: (i, 0))],
        compiler_params=_params(("parallel",), VMEM_BIG),
    )(u, w)


def _gate_prep(z, b_pad, f_col):
    t = z.shape[0]
    tm = ROW_TILE

    def body(z_ref, b_ref, f_ref, carry_ref):
        i = pl.program_id(0)

        @pl.when(i == 0)
        def _():
            carry_ref[...] = jnp.zeros_like(carry_ref)

        xs = z_ref[...] + b_ref[...]
        logf = jnp.minimum(xs, 0.0) - jnp.log(1.0 + jnp.exp(-jnp.abs(xs)))
        row = i * tm + lax.broadcasted_iota(jnp.int32, (tm, 1), 0)
        logf = jnp.where(row >= ROW_PAD, logf, 0.0)
        tri = (lax.broadcasted_iota(jnp.int32, (tm, tm), 0) >= lax.broadcasted_iota(jnp.int32, (tm, tm), 1))
        f = jnp.dot(tri.astype(F32), logf, preferred_element_type=F32, precision=lax.Precision.HIGHEST)
        f = f + carry_ref[0:1, :]
        f_ref[...] = f
        carry_ref[...] = jnp.broadcast_to(f[tm - 1:tm, :], carry_ref.shape)

    return pl.pallas_call(
        body, name="forget_gate_cumsum", out_shape=jax.ShapeDtypeStruct((t, 128), F32),
        grid=(t // tm,),
        in_specs=[pl.BlockSpec((tm, 128), lambda i: (i, f_col // 128)), pl.BlockSpec((1, 128), lambda i: (0, 0))],
        out_specs=pl.BlockSpec((tm, 128), lambda i: (i, 0)),
        scratch_shapes=[pltpu.VMEM((8, 128), F32)],
        compiler_params=_params(("arbitrary",)),
    )(z, b_pad)


def _lane_halves():
    lane = lax.broadcasted_iota(jnp.int32, (1, 128), 1)
    return lane < HEAD_DIM


def _causal_mask(tq, tk, row0=0):
    row = row0 + lax.broadcasted_iota(jnp.int32, (tq, 1), 0)
    col = lax.broadcasted_iota(jnp.int32, (1, tk), 1)
    return col <= row


def _lane_one(lane):
    return (lax.broadcasted_iota(jnp.int32, (1, 128), 1) == lane).astype(BF16)


def _split3(x):
    hi = x.astype(BF16)
    rest = x - hi.astype(F32)
    mid = rest.astype(BF16)
    return hi, mid, (rest - mid.astype(F32)).astype(BF16)


def _split3_glue(x):
    hi = lax.reduce_precision(x, 8, 7)
    mid = lax.reduce_precision(x - hi, 8, 7)
    lo = lax.reduce_precision((x - hi) - mid, 8, 7)
    return hi.astype(BF16), mid.astype(BF16), lo.astype(BF16)


def _aug_pairs(cols):
    t = cols[0].shape[0]
    a = jnp.pad(jnp.stack(cols, axis=2), ((0, 0), (0, 0), (0, HEAD_DIM - len(cols))))
    a = a.reshape(t, 4, 2, HEAD_DIM)[:, :, ::-1, :]
    return jnp.transpose(a.reshape(t, 4, 128), (1, 0, 2))


def _attn_bias_operands(f_heads, lse_heads=None):
    t = f_heads.shape[0]
    one = jnp.ones((t, HEADS), BF16)
    row = lax.broadcasted_iota(jnp.int32, (t, 1), 0)
    fq = _split3_glue(f_heads)
    fk = _split3_glue(jnp.where(row < ROW_PAD, 1e9, f_heads))
    q_cols = list(fq) + [one] * 3
    k_cols = [one] * 3 + [-c for c in fk]
    if lse_heads is not None:
        q_cols += [-c for c in _split3_glue(lse_heads)]
        k_cols += [one] * 3
    return _aug_pairs(q_cols), _aug_pairs(k_cols)


def _attn_steps(nq, by_key):
    if by_key:
        pairs = [(qi, ki) for ki in range(nq) for qi in range(ki, nq)]
    else:
        pairs = [(qi, ki) for qi in range(nq) for ki in range(qi + 1)]
    return (jnp.array([p[0] for p in pairs], jnp.int32), jnp.array([p[1] for p in pairs], jnp.int32))


def _attn_fwd(z, aug_q, aug_k):
    t = z.shape[0]
    tq = tk = ROW_TILE
    nq = t // tq
    q_tab, k_tab = _attn_steps(nq, by_key=False)

    def body(qt_ref, kt_ref, q_ref, k_ref, v_ref, aq_ref, ak_ref, o_ref, lse_ref, m_ref, l_ref, acc_ref):
        step = pl.program_id(1)
        qi, ki = qt_ref[step], kt_ref[step]

        @pl.when(ki == 0)
        def _():
            m_ref[...] = jnp.full_like(m_ref, NEG)
            l_ref[...] = jnp.zeros_like(l_ref)
            acc_ref[...] = jnp.zeros_like(acc_ref)

        def sweep(diagonal):
            first = _lane_halves()
            q = (q_ref[...] * (HEAD_DIM ** -0.5)).astype(BF16)
            k = k_ref[...].astype(BF16)
            v = v_ref[...].astype(BF16)
            aq, ak = aq_ref[...], ak_ref[...]
            halves = (first, jnp.logical_not(first))
            qa = [jnp.where(lanes, q, aq) for lanes in halves]
            ka = [jnp.where(lanes, k, ak) for lanes in halves]
            va = [jnp.where(lanes, v, _lane_one(a0)) for lanes, a0 in zip(halves, (HEAD_DIM, 0))]
            chains = [(hh, r) for r in range(ATTN_ROW_PARTS) for hh in range(2)]
            rp = tq // ATTN_ROW_PARTS
            rows = [slice(r * rp, (r + 1) * rp) for _, r in chains]
            s = [_dot_nt(qa[hh][rw], ka[hh]) for (hh, _), rw in zip(chains, rows)]
            if diagonal:
                s = [jnp.where(_causal_mask(rp, tk, rw.start), s_c, NEG) for s_c, rw in zip(s, rows)]
            m_prev = [m_ref[rw, hh * HEAD_DIM:hh * HEAD_DIM + 1] for (hh, _), rw in zip(chains, rows)]
            m_new = [jnp.maximum(mp, jnp.max(s_c, axis=1, keepdims=True)) for mp, s_c in zip(m_prev, s)]
            p = [jnp.exp(s_c - m_c).astype(BF16) for s_c, m_c in zip(s, m_new)]
            pv = [_dot(p_c, va[hh]) for p_c, (hh, _) in zip(p, chains)]
            alpha = [jnp.exp(mp - m_c) for mp, m_c in zip(m_prev, m_new)]
            for r in range(ATTN_ROW_PARTS):
                (m0, m1), (al0, al1), (pv0, pv1) = [x[2 * r:2 * r + 2] for x in (m_new, alpha, pv)]
                rw = rows[2 * r]
                l0 = al0 * l_ref[rw, 0:1] + pv0[:, HEAD_DIM:HEAD_DIM + 1]
                l1 = al1 * l_ref[rw, HEAD_DIM:HEAD_DIM + 1] + pv1[:, 0:1]
                acc_ref[rw, :] = acc_ref[rw, :] * jnp.where(first, al0, al1) + jnp.where(first, pv0, pv1)
                m_ref[rw, :] = jnp.where(first, m0, m1)
                l_ref[rw, :] = jnp.where(first, l0, l1)

        @pl.when(ki < qi)
        def _():
            sweep(False)

        @pl.when(ki == qi)
        def _():
            sweep(True)
            o_ref[...] = acc_ref[...] / l_ref[...]
            lse_ref[...] = m_ref[...] + jnp.log(l_ref[...])

    return pl.pallas_call(
        body, name="attention_fwd",
        out_shape=[jax.ShapeDtypeStruct((t, ATTN_W), F32), jax.ShapeDtypeStruct((t, ATTN_W), F32)],
        grid_spec=pltpu.PrefetchScalarGridSpec(
            num_scalar_prefetch=2, grid=(4, int(q_tab.shape[0])),
            in_specs=[pl.BlockSpec((tq, 128), lambda p, s, qt, kt: (qt[s], p)),
                      pl.BlockSpec((tk, 128), lambda p, s, qt, kt: (kt[s], 4 + p)),
                      pl.BlockSpec((tk, 128), lambda p, s, qt, kt: (kt[s], 8 + p)),
                      pl.BlockSpec((None, tq, 128), lambda p, s, qt, kt: (p, qt[s], 0)),
                      pl.BlockSpec((None, tk, 128), lambda p, s, qt, kt: (p, kt[s], 0))],
            out_specs=[pl.BlockSpec((tq, 128), lambda p, s, qt, kt: (qt[s], p)),
                       pl.BlockSpec((tq, 128), lambda p, s, qt, kt: (qt[s], p))],
            scratch_shapes=[pltpu.VMEM((tq, 128), F32)] * 3),
        compiler_params=_params(("parallel", "arbitrary")),
    )(q_tab, k_tab, z, z, z, aug_q, aug_k)


def _attn_bwd(z, aug_q, aug_k, o, do):
    t = z.shape[0]
    tq = tk = ROW_TILE
    nq = t // tq
    q_tab, k_tab = _attn_steps(nq, by_key=True)
    tn = (((0,), (0,)), ((), ()))

    def body(qt_ref, kt_ref, q_ref, k_ref, v_ref, aq_ref, ak_ref, o_ref, do_ref,
             dq_ref, dk_ref, dv_ref, dfk_ref, dfq_ref):
        step = pl.program_id(1)
        qi, ki = qt_ref[step], kt_ref[step]
        rows = pl.ds(pl.multiple_of(qi * tq, tq), tq)

        @pl.when(ki == 0)
        def _():
            dq_ref[rows, :] = jnp.zeros((tq, 128), F32)
            dfq_ref[rows, :] = jnp.zeros((tq, 128), F32)

        @pl.when(qi == ki)
        def _():
            dk_ref[...] = jnp.zeros_like(dk_ref)
            dv_ref[...] = jnp.zeros_like(dv_ref)
            dfk_ref[...] = jnp.zeros_like(dfk_ref)

        def sweep(diagonal):
            first = _lane_halves()
            lane = lax.broadcasted_iota(jnp.int32, (1, 128), 1)
            scale = HEAD_DIM ** -0.5
            q = (q_ref[...] * scale).astype(BF16)
            k = k_ref[...].astype(BF16)
            v = v_ref[...].astype(BF16)
            do_ = do_ref[...]
            do16 = do_.astype(BF16)
            od = o_ref[...] * do_
            aq, ak = aq_ref[...], ak_ref[...]
            parts = []
            for hh in range(2):
                lanes = first if hh == 0 else jnp.logical_not(first)
                a0 = HEAD_DIM - hh * HEAD_DIM
                one = _lane_one(a0)
                d_hi, d_mid, d_lo = _split3(jnp.sum(jnp.where(lanes, od, 0.0), axis=1, keepdims=True))
                minus_delta = jnp.where(lane == a0, -d_hi, jnp.where(lane == a0 + 1, -d_mid,
                                        jnp.where(lane == a0 + 2, -d_lo, jnp.zeros((), BF16))))
                ones3 = ((lane >= a0) & (lane < a0 + 3)).astype(BF16)
                s = _dot_nt(jnp.where(lanes, q, aq), jnp.where(lanes, k, ak))
                p = jnp.exp(s)
                if diagonal:
                    p = jnp.where(_causal_mask(tq, tk), p, 0.0)
                dp = _dot_nt(jnp.where(lanes, do16, minus_delta), jnp.where(lanes, v, ones3))
                ds16 = (p * dp).astype(BF16)
                dv_h = lax.dot_general(p.astype(BF16), jnp.where(lanes, do16, jnp.zeros((), BF16)), tn,
                                       preferred_element_type=F32)
                dk_h = lax.dot_general(ds16, jnp.where(lanes, q, one), tn, preferred_element_type=F32)
                dq_h = _dot(ds16, jnp.where(lanes, k, one))
                parts.append((dq_h, dk_h, dv_h, a0))
            (dq0, dk0, dv0, a0), (dq1, dk1, dv1, a1) = parts
            dq_ref[rows, :] += jnp.where(first, dq0, dq1) * scale
            dfq_ref[rows, :] += jnp.where(first, dq0[:, a0:a0 + 1], dq1[:, a1:a1 + 1])
            dk_ref[...] += jnp.where(first, dk0, dk1)
            dfk_ref[...] += jnp.where(first, dk0[:, a0:a0 + 1], dk1[:, a1:a1 + 1])
            dv_ref[...] += dv0 + dv1

        @pl.when(qi > ki)
        def _():
            sweep(False)

        @pl.when(qi == ki)
        def _():
            sweep(True)

    qrow = lambda p, s, qt, kt: (qt[s], p)
    krow = lambda p, s, qt, kt: (kt[s], p)
    return pl.pallas_call(
        body, name="attention_bwd",
        out_shape=[jax.ShapeDtypeStruct((t, ATTN_W), F32)] * 5,
        grid_spec=pltpu.PrefetchScalarGridSpec(
            num_scalar_prefetch=2, grid=(4, int(q_tab.shape[0])),
            in_specs=[pl.BlockSpec((tq, 128), qrow),
                      pl.BlockSpec((tk, 128), lambda p, s, qt, kt: (kt[s], 4 + p)),
                      pl.BlockSpec((tk, 128), lambda p, s, qt, kt: (kt[s], 8 + p)),
                      pl.BlockSpec((None, tq, 128), lambda p, s, qt, kt: (p, qt[s], 0)),
                      pl.BlockSpec((None, tk, 128), lambda p, s, qt, kt: (p, kt[s], 0)),
                      pl.BlockSpec((tq, 128), qrow), pl.BlockSpec((tq, 128), qrow)],
            out_specs=[pl.BlockSpec((t, 128), lambda p, s, qt, kt: (0, p)),
                       pl.BlockSpec((tk, 128), krow), pl.BlockSpec((tk, 128), krow), pl.BlockSpec((tk, 128), krow),
                       pl.BlockSpec((t, 128), lambda p, s, qt, kt: (0, p))]),
        compiler_params=_params(("parallel", "arbitrary"), VMEM_BIG),
    )(q_tab, k_tab, z, z, z, aug_q, aug_k, o, do)


def _shifted(prev_rows, x, shift):
    tm = x.shape[0]
    return pltpu.roll(jnp.concatenate([prev_rows, x], axis=0), shift, 0)[8:8 + tm]


def _ahead(x, next_rows, shift):
    tm = x.shape[0]
    return pltpu.roll(jnp.concatenate([x, next_rows], axis=0), tm + 8 - shift, 0)[0:tm]


def _conv_col0(z):
    return (z.shape[1] - F_PAD - 3 * CONV_W) // CONV_W


def _conv_specs(tm, c0):
    cols = (c0, c0 + 1, c0 + 2)
    tiles = [pl.BlockSpec((tm, CONV_W), functools.partial(lambda i, c: (i, c), c=c)) for c in cols]
    halos = [pl.BlockSpec((8, CONV_W), functools.partial(lambda i, c: (jnp.maximum(i * (tm // 8) - 1, 0), c), c=c))
             for c in cols]
    return tiles, halos


def _conv_gate(z, conv_w):
    t = z.shape[0]
    tm = ROW_TILE
    nt = t // tm

    def body(cb_ref, cc_ref, ci_ref, hc_ref, hi_ref, w_ref, g_ref, gt_ref):
        i = pl.program_id(0)
        cc = cc_ref[...] * ci_ref[...]
        prev = jnp.where(i > 0, hc_ref[...] * hi_ref[...], 0.0)
        conv = w_ref[0:1, :] * _shifted(prev, cc, 2) + w_ref[1:2, :] * _shifted(prev, cc, 1) + w_ref[2:3, :] * cc
        g = cb_ref[...] * conv
        g_ref[...] = g.astype(BF16)
        gt_ref[...] = g.T.astype(BF16)

    (cb, cc, ci), (_, hc, hi) = _conv_specs(tm, _conv_col0(z))
    return pl.pallas_call(
        body, name="conv_gate_fwd",
        out_shape=[jax.ShapeDtypeStruct((t, CONV_W), BF16), jax.ShapeDtypeStruct((CONV_W, t), BF16)],
        grid=(nt,),
        in_specs=[cb, cc, ci, hc, hi, pl.BlockSpec((8, CONV_W), lambda i: (0, 0))],
        out_specs=[pl.BlockSpec((tm, CONV_W), lambda i: (i, 0)), pl.BlockSpec((CONV_W, tm), lambda i: (0, i))],
        compiler_params=_params(("parallel",)),
    )(z, z, z, z, z, conv_w)


def _conv_bwd(z, dg, conv_w):
    t = z.shape[0]
    tm = ROW_TILE
    nt = t // tm

    def body(cb_ref, cc_ref, ci_ref, hc_ref, hi_ref, dg_ref, ncb_ref, ndg_ref, w_ref, dz_ref, dw_ref):
        i = pl.program_id(0)

        @pl.when(i == 0)
        def _():
            dw_ref[...] = jnp.zeros_like(dw_ref)

        cb, c_c, c_in = cb_ref[...], cc_ref[...], ci_ref[...]
        cc = c_c * c_in
        prev = jnp.where(i > 0, hc_ref[...] * hi_ref[...], 0.0)
        cc1, cc2 = _shifted(prev, cc, 1), _shifted(prev, cc, 2)
        w0, w1, w2 = w_ref[0:1, :], w_ref[1:2, :], w_ref[2:3, :]
        conv = w0 * cc2 + w1 * cc1 + w2 * cc
        dgv = dg_ref[...]
        dconv = dgv * cb
        nxt = jnp.where(i < nt - 1, ndg_ref[...] * ncb_ref[...], 0.0)
        dcc = w2 * dconv + w1 * _ahead(dconv, nxt, 1) + w0 * _ahead(dconv, nxt, 2)
        dz_ref[:, 0:CONV_W] = (dgv * conv).astype(BF16)
        dz_ref[:, CONV_W:2 * CONV_W] = (dcc * c_in).astype(BF16)
        dz_ref[:, 2 * CONV_W:] = (dcc * c_c).astype(BF16)
        dw_ref[0:1, :] += jnp.sum(dconv * cc2, axis=0, keepdims=True)
        dw_ref[1:2, :] += jnp.sum(dconv * cc1, axis=0, keepdims=True)
        dw_ref[2:3, :] += jnp.sum(dconv * cc, axis=0, keepdims=True)

    c0 = _conv_col0(z)
    (cb, cc, ci), (_, hc, hi) = _conv_specs(tm, c0)
    nxt = lambda i, c: (jnp.minimum((i + 1) * (tm // 8), t // 8 - 1), c)
    return pl.pallas_call(
        body, name="conv_gate_bwd",
        out_shape=[jax.ShapeDtypeStruct((t, 3 * CONV_W), BF16), jax.ShapeDtypeStruct((8, CONV_W), F32)],
        grid=(nt,),
        in_specs=[cb, cc, ci, hc, hi, pl.BlockSpec((tm, CONV_W), lambda i: (i, 0)),
                  pl.BlockSpec((8, CONV_W), lambda i: nxt(i, c0)), pl.BlockSpec((8, CONV_W), lambda i: nxt(i, 0)),
                  pl.BlockSpec((8, CONV_W), lambda i: (0, 0))],
        out_specs=[pl.BlockSpec((tm, 3 * CONV_W), lambda i: (i, 0)), pl.BlockSpec((8, CONV_W), lambda i: (0, 0))],
        compiler_params=_params(("arbitrary",)),
    )(z, z, z, z, z, dg, z, dg, conv_w)


def _branch_mix(z, o, g, w_ab, w_cb, d):
    t = z.shape[0]
    tm = ROW_TILE
    ga_col = 0

    def body(o_ref, g_ref, ga_ref, gc_ref, wa_ref, wc_ref, mp_ref, mpt_ref, ot_ref):
        o_ = o_ref[...]
        ya = _dot(o_.astype(BF16), wa_ref[...])
        yc = _dot(g_ref[...], wc_ref[...])
        mp = _sigmoid(ga_ref[...]) * ya + _sigmoid(gc_ref[...]) * yc
        mp_ref[...] = mp.astype(BF16)
        mpt_ref[...] = mp.T.astype(BF16)
        ot_ref[...] = o_.T.astype(BF16)

    return pl.pallas_call(
        body, name="branch_mix_fwd",
        out_shape=[jax.ShapeDtypeStruct((t, d), BF16), jax.ShapeDtypeStruct((d, t), BF16),
                   jax.ShapeDtypeStruct((ATTN_W, t), BF16)],
        grid=(t // tm,),
        in_specs=[pl.BlockSpec((tm, ATTN_W), lambda i: (i, 0)), pl.BlockSpec((tm, CONV_W), lambda i: (i, 0)),
                  pl.BlockSpec((tm, d), lambda i: (i, ga_col)), pl.BlockSpec((tm, d), lambda i: (i, ga_col + 1)),
                  pl.BlockSpec((ATTN_W, d), lambda i: (0, 0)), pl.BlockSpec((CONV_W, d), lambda i: (0, 0))],
        out_specs=[pl.BlockSpec((tm, d), lambda i: (i, 0)), pl.BlockSpec((d, tm), lambda i: (0, i)),
                   pl.BlockSpec((ATTN_W, tm), lambda i: (0, i))],
        compiler_params=_params(("parallel",), VMEM_BIG),
    )(o, g, z, z, w_ab, w_cb)


def _branch_bwd(z, o, g, dmixed, w_out, w_ab, w_cb, d):
    t = z.shape[0]
    tm = ROW_TILE // 2
    ga_col = 0

    def body(dm_ref, o_ref, g_ref, ga_ref, gc_ref, wo_ref, wa_ref, wc_ref, dya_ref, dyc_ref, dgt_ref, do_ref, dg_ref):
        dmp = _dot_nt(dm_ref[...], wo_ref[...])
        ya = _dot(o_ref[...].astype(BF16), wa_ref[...])
        yc = _dot(g_ref[...], wc_ref[...])
        sa, sc = _sigmoid(ga_ref[...]), _sigmoid(gc_ref[...])
        dya = (dmp * sa).astype(BF16)
        dyc = (dmp * sc).astype(BF16)
        dya_ref[...] = dya
        dyc_ref[...] = dyc
        dgt_ref[:, :d] = (dmp * ya * sa * (1.0 - sa)).astype(BF16)
        dgt_ref[:, d:] = (dmp * yc * sc * (1.0 - sc)).astype(BF16)
        do_ref[...] = _dot_nt(dya, wa_ref[...])
        dg_ref[...] = _dot_nt(dyc, wc_ref[...])

    row = lambda i: (i, 0)
    fixed = lambda i: (0, 0)
    return pl.pallas_call(
        body, name="branch_mix_bwd",
        out_shape=[jax.ShapeDtypeStruct((t, d), BF16), jax.ShapeDtypeStruct((t, d), BF16),
                   jax.ShapeDtypeStruct((t, 2 * d), BF16), jax.ShapeDtypeStruct((t, ATTN_W), F32),
                   jax.ShapeDtypeStruct((t, CONV_W), F32)],
        grid=(t // tm,),
        in_specs=[pl.BlockSpec((tm, d), row), pl.BlockSpec((tm, ATTN_W), row), pl.BlockSpec((tm, CONV_W), row),
                  pl.BlockSpec((tm, d), lambda i: (i, ga_col)), pl.BlockSpec((tm, d), lambda i: (i, ga_col + 1)),
                  pl.BlockSpec((d, d), fixed), pl.BlockSpec((ATTN_W, d), fixed), pl.BlockSpec((CONV_W, d), fixed)],
        out_specs=[pl.BlockSpec((tm, d), row), pl.BlockSpec((tm, d), row), pl.BlockSpec((tm, 2 * d), row),
                   pl.BlockSpec((tm, ATTN_W), row), pl.BlockSpec((tm, CONV_W), row)],
        compiler_params=_params(("parallel",), VMEM_BIG),
    )(dmixed, o, g, z, z, w_out, w_ab, w_cb)


def _loss_grad(h, target_pad):
    t, d = h.shape
    tm = ROW_TILE

    def body(h_ref, t_ref, dy_ref, loss_ref):
        i = pl.program_id(0)

        @pl.when(i == 0)
        def _():
            loss_ref[...] = jnp.zeros_like(loss_ref)

        row = i * tm + lax.broadcasted_iota(jnp.int32, (tm, 1), 0)
        err = jnp.where(row >= N_FRONT, h_ref[...] - t_ref[...], 0.0)
        dy_ref[...] = err * (1.0 / d)
        per_row = jnp.sum(err * err, axis=1, keepdims=True) * (1.0 / d)
        loss_ref[...] += 0.5 * jnp.sum(per_row, axis=0, keepdims=True)

    return pl.pallas_call(
        body, name="loss_and_grad",
        out_shape=[jax.ShapeDtypeStruct((t, d), F32), jax.ShapeDtypeStruct((1, 128), F32)],
        grid=(t // tm,),
        in_specs=[pl.BlockSpec((tm, d), lambda i: (i, 0))] * 2,
        out_specs=[pl.BlockSpec((tm, d), lambda i: (i, 0)), pl.BlockSpec((1, 128), lambda i: (0, 0))],
        compiler_params=_params(("arbitrary",)),
    )(h, target_pad)


def _norm_bwd(name, x, g, dy, alpha):
    t, d = x.shape
    tm = ROW_TILE

    def body(x_ref, g_ref, dy_ref, dx_ref, dg_ref):
        @pl.when(pl.program_id(0) == 0)
        def _():
            dg_ref[...] = jnp.zeros_like(dg_ref)

        dx, dg = _rms_bwd(x_ref[...], g_ref[...], dy_ref[...])
        dx_ref[...] = (alpha * dx).astype(BF16)
        dg_ref[...] += alpha * dg

    row = pl.BlockSpec((tm, d), lambda i: (i, 0))
    vec = pl.BlockSpec((1, d), lambda i: (0, 0))
    return pl.pallas_call(
        body, name=name,
        out_shape=[jax.ShapeDtypeStruct((t, d), BF16), jax.ShapeDtypeStruct((1, d), F32)],
        grid=(t // tm,), in_specs=[row, vec, row], out_specs=[row, vec],
        compiler_params=_params(("arbitrary",)),
    )(x, g, dy)


def _ffn_bwd_mid(name, df, w_out, ab):
    t, d = df.shape
    cw = ab.shape[1] // 4
    tm = ROW_TILE

    def body(df_ref, w_ref, ab_ref, o_ref):
        ds = _dot_nt(df_ref[...], w_ref[...])
        a = ab_ref[:, :cw]
        b = ab_ref[:, cw:]
        sg = _sigmoid(a)
        o_ref[:, :cw] = (ds * b * (sg * (1.0 + a * (1.0 - sg)))).astype(BF16)
        o_ref[:, cw:] = (ds * (a * sg)).astype(BF16)

    return pl.pallas_call(
        body, name=name, out_shape=jax.ShapeDtypeStruct((t, 4 * cw), BF16),
        grid=(2, t // tm),
        in_specs=[pl.BlockSpec((tm, d), lambda j, i: (i, 0)), pl.BlockSpec((cw, d), lambda j, i: (j, 0)),
                  pl.BlockSpec((tm, 2 * cw), lambda j, i: (i, j))],
        out_specs=pl.BlockSpec((tm, 2 * cw), lambda j, i: (i, j)),
        compiler_params=_params(("parallel", "parallel"), VMEM_BIG),
    )(df, w_out, ab)


def _mm_nt_norm_bwd(name, dy, w, h, g, dh_in):
    t, kdim = dy.shape
    d = h.shape[1]
    tm = ROW_TILE // 2
    slots = w.ndim == 3

    def body(dy_ref, w_ref, h_ref, g_ref, dhi_ref, dh_ref, dg_ref):
        @pl.when(pl.program_id(0) == 0)
        def _():
            dg_ref[...] = jnp.zeros_like(dg_ref)

        if slots:
            cw = w_ref.shape[2]
            dn = _dot_nt(dy_ref[:, 0:cw], w_ref[_slot_of(0)])
            for k in range(1, 4):
                dn += _dot_nt(dy_ref[:, k * cw:(k + 1) * cw], w_ref[_slot_of(k)])
        else:
            dn = _dot_nt(dy_ref[...], w_ref[...])
        dx, dg = _rms_bwd(h_ref[...], g_ref[...], dn)
        dh_ref[...] = dhi_ref[...] + dx
        dg_ref[...] += dg

    row = pl.BlockSpec((tm, d), lambda i: (i, 0))
    vec = pl.BlockSpec((1, d), lambda i: (0, 0))
    return pl.pallas_call(
        body, name=name,
        out_shape=[jax.ShapeDtypeStruct((t, d), F32), jax.ShapeDtypeStruct((1, d), F32)],
        grid=(t // tm,),
        in_specs=[pl.BlockSpec((tm, kdim), lambda i: (i, 0)), pl.BlockSpec(w.shape, lambda i: (0,) * w.ndim),
                  row, vec, row],
        out_specs=[row, vec],
        compiler_params=_params(("arbitrary",), VMEM_BIG),
    )(dy, w, h, g, dh_in)


def _gate_bwd(df_pad, z, b_pad, f_col):
    t = z.shape[0]
    tm = ROW_TILE
    nt = t // tm

    def body(d_ref, z_ref, b_ref, dz_ref, db_ref, carry_ref):
        i = pl.program_id(0)

        @pl.when(i == 0)
        def _():
            carry_ref[...] = jnp.zeros_like(carry_ref)
            db_ref[...] = jnp.zeros_like(db_ref)

        tri = (lax.broadcasted_iota(jnp.int32, (tm, tm), 0) <= lax.broadcasted_iota(jnp.int32, (tm, tm), 1))
        tail = jnp.dot(tri.astype(F32), d_ref[...], preferred_element_type=F32, precision=lax.Precision.HIGHEST)
        tail = tail + carry_ref[0:1, :]
        carry_ref[...] = jnp.broadcast_to(tail[0:1, :], carry_ref.shape)
        row = (nt - 1 - i) * tm + lax.broadcasted_iota(jnp.int32, (tm, 1), 0)
        dlogit = jnp.where(row >= ROW_PAD, tail * _sigmoid(-(z_ref[...] + b_ref[...])), 0.0)
        dz_ref[...] = jnp.zeros_like(dz_ref)
        dz_ref[:, 0:128] = dlogit.astype(BF16)
        db_ref[...] += jnp.sum(dlogit, axis=0, keepdims=True)

    rev = lambda i: (nt - 1 - i, 0)
    return pl.pallas_call(
        body, name="forget_gate_bwd",
        out_shape=[jax.ShapeDtypeStruct((t, F_PAD), BF16), jax.ShapeDtypeStruct((1, 128), F32)],
        grid=(nt,),
        in_specs=[pl.BlockSpec((tm, 128), rev), pl.BlockSpec((tm, 128), lambda i: (nt - 1 - i, f_col // 128)),
                  pl.BlockSpec((1, 128), lambda i: (0, 0))],
        out_specs=[pl.BlockSpec((tm, F_PAD), rev), pl.BlockSpec((1, 128), lambda i: (0, 0))],
        scratch_shapes=[pltpu.VMEM((8, 128), F32)],
        compiler_params=_params(("arbitrary",)),
    )(df_pad, z, b_pad)


def _ffn_fwd(tag, n, w_in4, w_out, h, g_post, g_next):
    ab, s, s_t = _ffn_in(f"{tag}_in_fwd", n, w_in4)
    outs = _mm_resid_norm(f"{tag}_out_fwd", s, w_out, h, g_post, 0.5, g_next)
    return ab, s_t, outs


def _ffn_bwd(tag, dh, f, g_post, ab, s_t, n_t, w_in4, w_out, h_in, g_pre):
    d, cw = w_in4.shape[1], w_in4.shape[2]
    t = dh.shape[0]
    df, dg_post = _norm_bwd(f"{tag}_post_norm_bwd", f, g_post, dh, 0.5)
    dw_out = _weight_grad(f"{tag}_dw_out", s_t, df, d, out_rows=cw // 2)
    dab = _ffn_bwd_mid(f"{tag}_mid_bwd", df, w_out, ab)
    dh_in, dg_pre = _mm_nt_norm_bwd(f"{tag}_in_bwd", dab, w_in4, h_in, g_pre, dh)
    bk = _k_tile(t)
    dw_in = _matmul(
        f"{tag}_dw_in", n_t, dab, jax.ShapeDtypeStruct((4, d, cw), F32), (1, 4, t // bk),
        pl.BlockSpec((d, bk), lambda a, b, k: (0, k)), pl.BlockSpec((bk, cw), lambda a, b, k: (k, b)),
        pl.BlockSpec((None, d, cw), lambda a, b, k: (_slot_of(b), 0, 0)), vmem=VMEM_BIG)
    return dh_in, dg_post, dg_pre, dw_in, dw_out


def _pack_small(meta, conv, gains, b_forget):
    d = gains[0].shape[1]
    rows = [meta.reshape(4, d), jnp.pad(conv.reshape(1, 3 * 128), ((0, 0), (0, d - 3 * 128)))]
    rows += list(gains) + [jnp.pad(b_forget, ((0, 0), (0, d - HEADS)))]
    return jnp.concatenate(rows + [jnp.zeros((4, d), F32)], axis=0)


def _unpack_small(block):
    d = block.shape[1]
    meta = block[0:4].reshape(N_META, d // 4)
    conv = block[4, :3 * 128].reshape(1, 3, 128)
    gains = [block[5 + i:6 + i] for i in range(6)]
    return meta, conv, gains, block[11:12, :HEADS]


def kernel(x, meta_tokens, w_in, b_forget, conv_w, w_attn_branch, w_conv_branch, w_out, g_ffn1_pre, g_ffn1_post, w_ffn1_in, w_ffn1_out, g_mix_pre, g_mix_post, g_ffn2_pre, g_ffn2_post, w_ffn2_in, w_ffn2_out, loss_target, m_meta_tokens, m_w_in, m_b_forget, m_conv_w, m_w_attn_branch, m_w_conv_branch, m_w_out, m_g_ffn1_pre, m_g_ffn1_post, m_w_ffn1_in, m_w_ffn1_out, m_g_mix_pre, m_g_mix_post, m_g_ffn2_pre, m_g_ffn2_post, m_w_ffn2_in, m_w_ffn2_out, v_meta_tokens, v_w_in, v_b_forget, v_conv_w, v_w_attn_branch, v_w_conv_branch, v_w_out, v_g_ffn1_pre, v_g_ffn1_post, v_w_ffn1_in, v_w_ffn1_out, v_g_mix_pre, v_g_mix_post, v_g_ffn2_pre, v_g_ffn2_post, v_w_ffn2_in, v_w_ffn2_out):
    seq, d = x.shape[1], x.shape[2]
    t = seq + N_FRONT
    n_main = 3 * ATTN_W + 3 * CONV_W + 2 * d
    nz = n_main + F_PAD
    f_lo = 3 * ATTN_W
    c_arr = lax.axis_index("c").astype(jnp.int32).reshape(1)

    big = [w_in[0], w_attn_branch[0], w_conv_branch[0], w_out[0], w_ffn1_in[0], w_ffn1_out[0], w_ffn2_in[0], w_ffn2_out[0]]
    small_gather = jnp.concatenate(
        [meta_tokens.reshape(4, d), jnp.pad(conv_w.reshape(1, 3 * 128), ((0, 0), (0, d - 3 * 128))),
         jnp.zeros((11, d), F32)], axis=0)
    gathered = _all_gather([w.astype(BF16) for w in big] + [small_gather])
    w_in4, w_ab4, w_cb4, w_out4, w_f1_in4, w_f1_out4, w_f2_in4, w_f2_out4, small4 = gathered

    w_in_full = jnp.transpose(w_in4, (1, 0, 2)).reshape(d, 4 * w_in4.shape[2])
    g_lo = f_lo + HEADS + 3 * CONV_W
    w_in_pad = jnp.concatenate(
        [w_in_full[:, :f_lo], w_in_full[:, g_lo:], w_in_full[:, f_lo + HEADS:g_lo], w_in_full[:, f_lo:f_lo + HEADS],
         jnp.zeros((d, F_PAD - HEADS), BF16)], axis=1)
    w_ab = jnp.transpose(w_ab4, (1, 0, 2)).reshape(ATTN_W, d)
    w_cb = jnp.transpose(w_cb4, (1, 0, 2)).reshape(CONV_W, d)
    w_out_full = w_out4.reshape(d, d)
    w_f1_out = w_f1_out4.reshape(-1, d)
    w_f2_out = w_f2_out4.reshape(-1, d)
    meta_full = jnp.transpose(small4[:, 0:4].reshape(4, N_META, d // 4), (1, 0, 2)).reshape(N_META, d)
    conv_full = jnp.transpose(small4[:, 4, :3 * 128].reshape(4, 3, 128), (1, 0, 2)).reshape(3, CONV_W)
    conv_pad = jnp.pad(conv_full, ((0, 5), (0, 0)))
    b_pad = jnp.pad(b_forget, ((0, 0), (0, 128 - HEADS)))

    h0 = jnp.concatenate([jnp.zeros((ROW_PAD, d), F32), meta_full, x[0]], axis=0)
    target_pad = jnp.concatenate([jnp.zeros((N_FRONT, d), F32), loss_target[0]], axis=0)
    n1, n1_t = _norm_fwd("ffn1_pre_norm", h0, g_ffn1_pre)
    ab1, s1_t, (f1, h1, u, u_t) = _ffn_fwd("ffn1", n1, w_f1_in4, w_f1_out, h0, g_ffn1_post, g_mix_pre)
    qkv, z = _in_proj(u, w_in_pad)
    f_col = z.shape[1] - F_PAD
    f_cum = _gate_prep(z, b_pad, f_col)
    f_heads = f_cum[:, :HEADS]
    o, lse = _attn_fwd(qkv, *_attn_bias_operands(f_heads))
    g, g_t = _conv_gate(z, conv_pad)
    mp, mp_t, o_t = _branch_mix(z, o, g, w_ab, w_cb, d)
    mixed, h2, n2, n2_t = _mm_resid_norm("mix_out_fwd", mp, w_out_full, h1, g_mix_post, 1.0, g_ffn2_pre)
    ab2, s2_t, (f2, h3) = _ffn_fwd("ffn2", n2, w_f2_in4, w_f2_out, h2, g_ffn2_post, None)
    dh3, loss_part = _loss_grad(h3, target_pad)
    loss = lax.psum(loss_part[0, 0], ("x", "y", "c"))

    dh2, dg_f2_post, dg_f2_pre, dw_f2_in, dw_f2_out = _ffn_bwd(
        "ffn2", dh3, f2, g_ffn2_post, ab2, s2_t, n2_t, w_f2_in4, w_f2_out, h2, g_ffn2_pre)
    dmixed, dg_mix_post = _norm_bwd("mix_post_norm_bwd", mixed, g_mix_post, dh2, 1.0)
    dw_out = _weight_grad("mix_dw_out", mp_t, dmixed, d)
    dya, dyc, dgates, do, dgconv = _branch_bwd(z, o, g, dmixed, w_out_full, w_ab, w_cb, d)
    dw_ab = _weight_grad("mix_dw_attn_branch", o_t, dya, d)
    dw_cb = _weight_grad("mix_dw_conv_branch", g_t, dyc, d)
    dz_conv, dconv_w = _conv_bwd(z, dgconv, conv_pad)
    front = lax.broadcasted_iota(jnp.int32, (t, 1), 0) < ROW_PAD
    lse_heads = jnp.where(front, 1e9, lse[:, ::HEAD_DIM])
    dq, dk, dv, dfk, dfq = _attn_bwd(qkv, *_attn_bias_operands(f_heads, lse_heads), o, do)
    df_pad = jnp.pad(dfq[:, ::HEAD_DIM] - dfk[:, ::HEAD_DIM], ((0, 0), (0, 128 - HEADS)))
    dz_f, db_forget = _gate_bwd(df_pad, z, b_pad, f_col)
    dz = jnp.concatenate([dq.astype(BF16), dk.astype(BF16), dv.astype(BF16), dgates, dz_conv, dz_f], axis=1)
    dh1, dg_mix_pre = _mm_nt_norm_bwd("mix_in_bwd", dz, w_in_pad, h1, g_mix_pre, dh2)
    dw_in_pad = _weight_grad("mix_dw_in", u_t, dz, 512)
    dh0, dg_f1_post, dg_f1_pre, dw_f1_in, dw_f1_out = _ffn_bwd(
        "ffn1", dh1, f1, g_ffn1_post, ab1, s1_t, n1_t, w_f1_in4, w_f1_out, h0, g_ffn1_pre)
    grad_x = dh0[N_FRONT:][None]
    dmeta = dh0[ROW_PAD:N_FRONT]

    cs = w_in4.shape[2]
    c_lo = f_lo + 2 * d
    dw_in_full = jnp.concatenate(
        [dw_in_pad[:, :f_lo], dw_in_pad[:, n_main:n_main + HEADS], dw_in_pad[:, c_lo:n_main], dw_in_pad[:, f_lo:c_lo]],
        axis=1)
    small_grad = jnp.stack([
        _pack_small(dmeta[:, j * (d // 4):(j + 1) * (d // 4)], dconv_w[:3, j * 128:(j + 1) * 128],
                    [dg_f1_pre, dg_f1_post, dg_mix_pre, dg_mix_post, dg_f2_pre, dg_f2_post], db_forget[:, :HEADS])
        for j in range(4)])
    slots = [
        jnp.transpose(dw_in_full.reshape(d, 4, cs), (1, 0, 2)),
        jnp.transpose(dw_ab.reshape(ATTN_W, 4, d // 4), (1, 0, 2)),
        jnp.transpose(dw_cb.reshape(CONV_W, 4, d // 4), (1, 0, 2)),
        dw_out.reshape(4, d // 4, d),
        dw_f1_in, dw_f1_out.reshape(4, -1, d), dw_f2_in, dw_f2_out.reshape(4, -1, d),
        small_grad,
    ]
    tags = ["w_in", "w_attn_branch", "w_conv_branch", "w_out", "w_ffn1_in", "w_ffn1_out", "w_ffn2_in", "w_ffn2_out", "small"]

    got = _pair_send_halves(slots)
    pair_sums = [_pair_add(tag, s, a, c_arr, F32 if tag == "small" else BF16) for tag, s, a in zip(tags, slots, got)]
    arrived = _chip_scatter(pair_sums)
    halves = [_chip_add(tag, a) for tag, a in zip(tags, arrived)]
    grads = _pair_join_halves(halves)

    small = [g_ffn1_pre, g_ffn1_post, g_mix_pre, g_mix_post, g_ffn2_pre, g_ffn2_post]
    small_m = [m_g_ffn1_pre, m_g_ffn1_post, m_g_mix_pre, m_g_mix_post, m_g_ffn2_pre, m_g_ffn2_post]
    small_v = [v_g_ffn1_pre, v_g_ffn1_post, v_g_mix_pre, v_g_mix_post, v_g_ffn2_pre, v_g_ffn2_post]
    ws = big + [_pack_small(meta_tokens, conv_w[0], small, b_forget)]
    ms = [m_w_in[0], m_w_attn_branch[0], m_w_conv_branch[0], m_w_out[0], m_w_ffn1_in[0], m_w_ffn1_out[0],
          m_w_ffn2_in[0], m_w_ffn2_out[0], _pack_small(m_meta_tokens, m_conv_w[0], small_m, m_b_forget)]
    vs = [v_w_in[0], v_w_attn_branch[0], v_w_conv_branch[0], v_w_out[0], v_w_ffn1_in[0], v_w_ffn1_out[0],
          v_w_ffn2_in[0], v_w_ffn2_out[0], _pack_small(v_meta_tokens, v_conv_w[0], small_v, v_b_forget)]
    updates = [_adamw(tag, w, g_, m, v) for tag, w, g_, m, v in zip(tags, ws, grads, ms, vs)]

    def leaves(big_vals, small_block):
        meta, conv, gains, bf = _unpack_small(small_block)
        w_in_, w_ab_, w_cb_, w_out_, f1_in, f1_out, f2_in, f2_out = [b[None] for b in big_vals]
        return [meta, w_in_, bf, conv, w_ab_, w_cb_, w_out_, gains[0], gains[1], f1_in, f1_out,
                gains[2], gains[3], gains[4], gains[5], f2_in, f2_out]

    out_g = leaves(grads[:8], grads[8])
    out_d = leaves([u_[0] for u_ in updates[:8]], updates[8][0])
    out_m = leaves([u_[1] for u_ in updates[:8]], updates[8][1])
    out_v = leaves([u_[2] for u_ in updates[:8]], updates[8][2])
    return (loss, grad_x, *out_g, *out_d, *out_m, *out_v)
```

```python
import functools

import jax
import jax.numpy as jnp
from jax import lax
from jax.experimental import pallas as pl
from jax.experimental.pallas import tpu as pltpu

N_META = 16
ROW_PAD = 112
N_FRONT = ROW_PAD + N_META
HEADS = 8
HEAD_DIM = 64
ATTN_W = HEADS * HEAD_DIM
CONV_W = 512
NORM_EPS = 1e-6
ROW_TILE = 640
F_PAD = 512
ATTN_ROW_PARTS = 1
NEG = -1e30
ADAM_LR = 0.001
ADAM_B1 = 0.9
ADAM_B2 = 0.999
ADAM_EPS = 1e-08
ADAM_WD = 0.01
ADAM_STEP = 10
VMEM_BIG = 56 * 1024 * 1024
MESH = pl.DeviceIdType.MESH
ANY = pl.BlockSpec(memory_space=pl.ANY)
F32 = jnp.float32
BF16 = jnp.bfloat16


def _params(sem, vmem=None):
    return pltpu.CompilerParams(dimension_semantics=sem, vmem_limit_bytes=vmem)


def _sigmoid(x):
    return 1.0 / (1.0 + jnp.exp(-x))


def _rstd(x):
    return lax.rsqrt(jnp.mean(x * x, axis=-1, keepdims=True) + NORM_EPS)


def _rms_bwd(x, g, dy):
    r = _rstd(x)
    xr = x * r
    gdy = g * dy
    dx = r * (gdy - xr * jnp.mean(xr * gdy, axis=-1, keepdims=True))
    return dx, jnp.sum(dy * xr, axis=0, keepdims=True)


def _dot(a, b):
    return jnp.dot(a, b, preferred_element_type=F32)


def _dot_nt(a, b):
    return lax.dot_general(a, b, (((1,), (1,)), ((), ())), preferred_element_type=F32)


def _k_tile(t):
    return 1664 if t % 1664 == 0 else ROW_TILE


def _place():
    x, y, c = lax.axis_index("x"), lax.axis_index("y"), lax.axis_index("c")
    chips = [(1 - x, y), (x, 1 - y), (1 - x, 1 - y)]
    return x, y, c, chips


def _all_gather(shards):
    n = len(shards)
    split = [s.reshape(2, s.shape[0] // 2, s.shape[1]) for s in shards]

    def body(*refs):
        ins, outs = refs[:n], refs[n:2 * n]
        send_sems, recv_sems = refs[2 * n:]
        x, y, c, chips = _place()
        me = 2 * x + y
        sibling = (x, y, 1 - c)

        def remote(i, k, slot, part, to, src=None):
            dst = outs[i].at[slot, part]
            return pltpu.make_async_remote_copy(
                src_ref=dst if src is None else src, dst_ref=dst,
                send_sem=send_sems.at[i, k], recv_sem=recv_sems.at[i, k],
                device_id=to, device_id_type=MESH)

        started = []
        for i in range(n):
            for k, (cx, cy) in enumerate(chips):
                cp = remote(i, k, me, c, (cx, cy, c), src=ins[i].at[c])
                cp.start()
                started.append(cp)
        for i in range(n):
            for k, (cx, cy) in enumerate(chips):
                remote(i, k, 2 * cx + cy, c, (x, y, c)).wait_recv()
                cp = remote(i, 3 + k, 2 * cx + cy, c, sibling)
                cp.start()
                started.append(cp)
        for i in range(n):
            for k, (cx, cy) in enumerate(chips):
                remote(i, 3 + k, 2 * cx + cy, 1 - c, (x, y, c)).wait_recv()
        for cp in started:
            cp.wait_send()

    outs = pl.pallas_call(
        body, name="all_gather_weights",
        out_shape=[jax.ShapeDtypeStruct((4,) + s.shape, s.dtype) for s in split],
        in_specs=[ANY] * n, out_specs=[ANY] * n,
        scratch_shapes=[pltpu.SemaphoreType.DMA((n, 6)), pltpu.SemaphoreType.DMA((n, 6))],
    )(*split)
    me =2 * lax.axis_index("x") + lax.axis_index("y")
    outs = [lax.dynamic_update_slice(o, s[None], (me, 0, 0, 0)) for o, s in zip(outs, split)]
    return [o.reshape((4,) + s.shape) for o, s in zip(outs, shards)]


def _pair_send_halves(grads):
    n = len(grads)

    def body(*refs):
        ins, outs = refs[:n], refs[n:2 * n]
        send_sems, recv_sems = refs[2 * n:]
        x, y, c, _ = _place()
        cps = []
        for i in range(n):
            half = ins[i].shape[1] // 2
            cp = pltpu.make_async_remote_copy(
                src_ref=ins[i].at[:, pl.ds((1 - c) * half, half)], dst_ref=outs[i],
                send_sem=send_sems.at[i], recv_sem=recv_sems.at[i],
                device_id=(x, y, 1 - c), device_id_type=MESH)
            cp.start()
            cps.append(cp)
        for cp in cps:
            cp.wait()

    return pl.pallas_call(
        body, name="grad_pair_exchange",
        out_shape=[jax.ShapeDtypeStruct((4, g.shape[1] // 2, g.shape[2]), g.dtype) for g in grads],
        in_specs=[ANY] * n, out_specs=[ANY] * n,
        scratch_shapes=[pltpu.SemaphoreType.DMA((n,)), pltpu.SemaphoreType.DMA((n,))],
    )(*grads)


def _chip_scatter(parts):
    n = len(parts)

    def body(*refs):
        ins, outs = refs[:n], refs[n:2 * n]
        send_sems, recv_sems, local_sems = refs[2 * n:]
        x, y, c, chips = _place()
        me = 2 * x + y
        sends, local = [], []
        for i in range(n):
            cp = pltpu.make_async_copy(ins[i].at[me], outs[i].at[me], local_sems.at[i])
            cp.start()
            local.append(cp)
            for k, (cx, cy) in enumerate(chips):
                cp = pltpu.make_async_remote_copy(
                    src_ref=ins[i].at[2 * cx + cy], dst_ref=outs[i].at[me],
                    send_sem=send_sems.at[i, k], recv_sem=recv_sems.at[i, k],
                    device_id=(cx, cy, c), device_id_type=MESH)
                cp.start()
                sends.append(cp)
        for i in range(n):
            for k, (cx, cy) in enumerate(chips):
                got = outs[i].at[2 * cx + cy]
                pltpu.make_async_remote_copy(
                    src_ref=got, dst_ref=got, send_sem=send_sems.at[i, k], recv_sem=recv_sems.at[i, k],
                    device_id=(x, y, c), device_id_type=MESH).wait_recv()
        for cp in sends:
            cp.wait_send()
        for cp in local:
            cp.wait()

    return pl.pallas_call(
        body, name="grad_chip_scatter",
        out_shape=[jax.ShapeDtypeStruct(p.shape, p.dtype) for p in parts],
        in_specs=[ANY] * n, out_specs=[ANY] * n,
        scratch_shapes=[pltpu.SemaphoreType.DMA((n, 3)), pltpu.SemaphoreType.DMA((n, 3)),
                        pltpu.SemaphoreType.DMA((n,))],
    )(*parts)


def _pair_swap(halves):
    n = len(halves)

    def body(*refs):
        ins, outs = refs[:n], refs[n:2 * n]
        send_sems, recv_sems = refs[2 * n:]
        x, y, c, _ = _place()
        cps = []
        for i in range(n):
            cp = pltpu.make_async_remote_copy(
                src_ref=ins[i], dst_ref=outs[i], send_sem=send_sems.at[i], recv_sem=recv_sems.at[i],
                device_id=(x, y, 1 - c), device_id_type=MESH)
            cp.start()
            cps.append(cp)
        for cp in cps:
            cp.wait()

    return pl.pallas_call(
        body, name="grad_pair_swap",
        out_shape=[jax.ShapeDtypeStruct(h.shape, h.dtype) for h in halves],
        in_specs=[ANY] * n, out_specs=[ANY] * n,
        scratch_shapes=[pltpu.SemaphoreType.DMA((n,)), pltpu.SemaphoreType.DMA((n,))],
    )(*halves)


def _row_block(rows, cols, n_bufs, budget=20 * 1024 * 1024):
    best = min(rows, 16)
    for b in range(16, rows + 1, 16):
        if rows % b == 0 and 2 * n_bufs * b * cols * 4 <= budget:
            best = b
    return best


def _pair_add(tag, grad, got, c_arr, out_dtype):
    _, rows, cols = grad.shape
    half = rows // 2
    bh = _row_block(half, cols, 3)
    nb = half // bh

    def body(c_ref, g_ref, a_ref, o_ref):
        o_ref[...] = (g_ref[...] + a_ref[...]).astype(out_dtype)

    return pl.pallas_call(
        body, name=f"pair_add_{tag}",
        out_shape=jax.ShapeDtypeStruct((4, half, cols), out_dtype),
        grid_spec=pltpu.PrefetchScalarGridSpec(
            num_scalar_prefetch=1, grid=(4, nb),
            in_specs=[pl.BlockSpec((None, bh, cols), lambda j, r, c: (j, c[0] * nb + r, 0)),
                      pl.BlockSpec((None, bh, cols), lambda j, r, c: (j, r, 0))],
            out_specs=pl.BlockSpec((None, bh, cols), lambda j, r, c: (j, r, 0))),
        compiler_params=_params(("parallel", "parallel")),
    )(c_arr, grad, got)


def _chip_add(tag, parts):
    _, half, cols = parts.shape
    bh = _row_block(half, cols, 5)

    def body(p_ref, o_ref):
        a, b, c, d = [p_ref[j].astype(F32) for j in range(4)]
        o_ref[...] = ((a + b) + c) + d

    return pl.pallas_call(
        body, name=f"chip_add_{tag}",
        out_shape=jax.ShapeDtypeStruct((half, cols), F32),
        grid=(half // bh,),
        in_specs=[pl.BlockSpec((4, bh, cols), lambda r: (0, r, 0))],
        out_specs=pl.BlockSpec((bh, cols), lambda r: (r, 0)),
        compiler_params=_params(("parallel",)),
    )(parts)


def _adamw(tag, w, mine, theirs, m, v, c_arr):
    rows, cols = w.shape
    half = rows // 2
    br = _row_block(half, cols, 9)
    nb = half // br

    def body(c_ref, w_ref, a_ref, b_ref, m_ref, v_ref, g_ref, d_ref, mo_ref, vo_ref):
        own = (pl.program_id(0) // nb) == c_ref[0]
        g = jnp.where(own, a_ref[...], b_ref[...])
        g_ref[...] = g
        m_new = ADAM_B1 * m_ref[...] + (1.0 - ADAM_B1) * g
        v_new = ADAM_B2 * v_ref[...] + (1.0 - ADAM_B2) * (g * g)
        m_hat = m_new / (1.0 - ADAM_B1 ** ADAM_STEP)
        v_hat = v_new / (1.0 - ADAM_B2 ** ADAM_STEP)
        d_ref[...] = -ADAM_LR * (m_hat / (jnp.sqrt(v_hat) + ADAM_EPS) + ADAM_WD * w_ref[...])
        mo_ref[...] = m_new
        vo_ref[...] = v_new

    spec = pl.BlockSpec((br, cols), lambda r, c: (r, 0))
    mine_spec = pl.BlockSpec((br, cols), lambda r, c: (jnp.clip(r - c[0] * nb, 0, nb - 1), 0))
    theirs_spec = pl.BlockSpec((br, cols), lambda r, c: (jnp.clip(r - (1 - c[0]) * nb, 0, nb - 1), 0))
    return pl.pallas_call(
        body, name=f"adamw_{tag}",
        out_shape=[jax.ShapeDtypeStruct((rows, cols), F32)] * 4,
        grid_spec=pltpu.PrefetchScalarGridSpec(
            num_scalar_prefetch=1, grid=(rows // br,),
            in_specs=[spec, mine_spec, theirs_spec, spec, spec], out_specs=[spec] * 4),
        compiler_params=_params(("arbitrary",)),
    )(c_arr, w, mine, theirs, m, v)


def _matmul(name, x, w, out_shape, grid, x_spec, w_spec, o_spec, *, nt=False, vmem=None):
    nk = grid[2]
    acc_shape = tuple(d for d in o_spec.block_shape if d is not None)

    def body(x_ref, w_ref, o_ref, acc_ref):
        k = pl.program_id(2)
        part = _dot_nt(x_ref[...], w_ref[...]) if nt else _dot(x_ref[...], w_ref[...])
        if nk == 1:
            o_ref[...] = part.astype(o_ref.dtype)
        else:
            @pl.when(k == 0)
            def _():
                acc_ref[...] = part

            @pl.when(k > 0)
            def _():
                acc_ref[...] += part

            @pl.when(k == nk - 1)
            def _():
                o_ref[...] = acc_ref[...].astype(o_ref.dtype)

    return pl.pallas_call(
        body, name=name, out_shape=out_shape, grid=grid,
        in_specs=[x_spec, w_spec], out_specs=o_spec,
        scratch_shapes=[pltpu.VMEM(acc_shape if nk > 1 else (8, 128), F32)],
        compiler_params=_params(("parallel", "parallel", "arbitrary"), vmem),
    )(x, w)


def _weight_grad(name, xt, dy, bn, out_rows=None):
    m, t = xt.shape
    n = dy.shape[1]
    bm = m if out_rows is None else out_rows
    bk = _k_tile(t)
    return _matmul(
        name, xt, dy, jax.ShapeDtypeStruct((m, n), F32), (m // bm, n // bn, t // bk),
        pl.BlockSpec((bm, bk), lambda a, b, k: (a, k)),
        pl.BlockSpec((bk, bn), lambda a, b, k: (k, b)),
        pl.BlockSpec((bm, bn), lambda a, b, k: (a, b)), vmem=VMEM_BIG)


def _norm_fwd(name, h, g):
    t, d = h.shape
    tm = ROW_TILE

    def body(h_ref, g_ref, n_ref, nt_ref):
        x = h_ref[...]
        y = x * _rstd(x) * g_ref[...]
        n_ref[...] = y.astype(BF16)
        nt_ref[...] = y.T.astype(BF16)

    return pl.pallas_call(
        body, name=name,
        out_shape=[jax.ShapeDtypeStruct((t, d), BF16), jax.ShapeDtypeStruct((d, t), BF16)],
        grid=(t // tm,),
        in_specs=[pl.BlockSpec((tm, d), lambda i: (i, 0)), pl.BlockSpec((1, d), lambda i: (0, 0))],
        out_specs=[pl.BlockSpec((tm, d), lambda i: (i, 0)), pl.BlockSpec((d, tm), lambda i: (0, i))],
        compiler_params=_params(("parallel",)),
    )(h, g)


def _slot_of(kk):
    return (kk % 2) * 2 + kk // 2


def _ffn_in(name, n, w4):
    t, d = n.shape
    cw = w4.shape[2]
    tm = ROW_TILE

    def body(x_ref, wg_ref, wu_ref, ab_ref, s_ref, st_ref):
        x = x_ref[...]
        a = _dot(x, wg_ref[...])
        b = _dot(x, wu_ref[...])
        ab_ref[:, :cw] = a
        ab_ref[:, cw:] = b
        s = a * _sigmoid(a) * b
        s_ref[...] = s.astype(BF16)
        st_ref[...] = s.T.astype(BF16)

    return pl.pallas_call(
        body, name=name,
        out_shape=[jax.ShapeDtypeStruct((t, 4 * cw), F32), jax.ShapeDtypeStruct((t, 2 * cw), BF16),
                   jax.ShapeDtypeStruct((2 * cw, t), BF16)],
        grid=(2, t // tm),
        in_specs=[pl.BlockSpec((tm, d), lambda j, i: (i, 0)),
                  pl.BlockSpec((None, d, cw), lambda j, i: (j, 0, 0)),
                  pl.BlockSpec((None, d, cw), lambda j, i: (2 + j, 0, 0))],
        out_specs=[pl.BlockSpec((tm, 2 * cw), lambda j, i: (i, j)),
                   pl.BlockSpec((tm, cw), lambda j, i: (i, j)),
                   pl.BlockSpec((cw, tm), lambda j, i: (j, i))],
        compiler_params=_params(("parallel", "parallel"), VMEM_BIG),
    )(n, w4, w4)


def _mm_resid_norm(name, x, w, h, g_post, alpha, g_next):
    t, kdim = x.shape
    d = w.shape[1]
    tm = ROW_TILE
    with_next = g_next is not None

    def body(x_ref, w_ref, h_ref, gp_ref, gn_ref, f_ref, hn_ref, *rest):
        f = _dot(x_ref[...], w_ref[...])
        f_ref[...] = f
        hn = h_ref[...] + alpha * (f * _rstd(f) * gp_ref[...])
        hn_ref[...] = hn
        if with_next:
            y = hn * _rstd(hn) * gn_ref[...]
            rest[0][...] = y.astype(BF16)
            rest[1][...] = y.T.astype(BF16)

    row = lambda i: (i, 0)
    vec = pl.BlockSpec((1, d), lambda i: (0, 0))
    out_shape = [jax.ShapeDtypeStruct((t, d), F32), jax.ShapeDtypeStruct((t, d), F32)]
    out_specs = [pl.BlockSpec((tm, d), row), pl.BlockSpec((tm, d), row)]
    if with_next:
        out_shape += [jax.ShapeDtypeStruct((t, d), BF16), jax.ShapeDtypeStruct((d, t), BF16)]
        out_specs += [pl.BlockSpec((tm, d), row), pl.BlockSpec((d, tm), lambda i: (0, i))]
    return pl.pallas_call(
        body, name=name, out_shape=out_shape, grid=(t // tm,),
        in_specs=[pl.BlockSpec((tm, kdim), row), pl.BlockSpec((kdim, d), lambda i: (0, 0)),
                  pl.BlockSpec((tm, d), row), vec, vec],
        out_specs=out_specs,
        compiler_params=_params(("parallel",), VMEM_BIG),
    )(x, w, h, g_post, g_post if g_next is None else g_next)


def _in_proj(u, w):
    t, d = u.shape
    nz = w.shape[1]
    nq = 3 * ATTN_W
    tm = ROW_TILE // 2

    def body(u_ref, w_ref, qkv_ref, z_ref):
        qkv_ref[...] = _dot(u_ref[...], w_ref[:, 0:nq]).astype(BF16)
        z_ref[...] = _dot(u_ref[...], w_ref[:, nq:])

    return pl.pallas_call(
        body, name="mix_in_proj",
        out_shape=[jax.ShapeDtypeStruct((t, nq), BF16), jax.ShapeDtypeStruct((t, nz - nq), F32)],
        grid=(t // tm,),
        in_specs=[pl.BlockSpec((tm, d), lambda i: (i, 0)), pl.BlockSpec((d, nz), lambda i: (0, 0))],
        out_specs=[pl.BlockSpec((tm, nq), lambda i: (i, 0)), pl.BlockSpec((tm, nz - nq), lambda i: (i, 0))],
        compiler_params=_params(("parallel",), VMEM_BIG),
    )(u, w)


def _gate_prep(z, b_pad, f_col):
    t = z.shape[0]
    tm = ROW_TILE

    def body(z_ref, b_ref, f_ref, carry_ref):
        i = pl.program_id(0)

        @pl.when(i == 0)
        def _():
            carry_ref[...] = jnp.zeros_like(carry_ref)

        xs = z_ref[...] + b_ref[...]
        logf = jnp.minimum(xs, 0.0) - jnp.log(1.0 + jnp.exp(-jnp.abs(xs)))
        row = i * tm + lax.broadcasted_iota(jnp.int32, (tm, 1), 0)
        logf = jnp.where(row >= ROW_PAD, logf, 0.0)
        tri = (lax.broadcasted_iota(jnp.int32, (tm, tm), 0) >= lax.broadcasted_iota(jnp.int32, (tm, tm), 1))
        f = jnp.dot(tri.astype(F32), logf, preferred_element_type=F32, precision=lax.Precision.HIGHEST)
        f = f + carry_ref[0:1, :]
        f_ref[...] = f
        carry_ref[...] = jnp.broadcast_to(f[tm - 1:tm, :], carry_ref.shape)

    return pl.pallas_call(
        body, name="forget_gate_cumsum", out_shape=jax.ShapeDtypeStruct((t, 128), F32),
        grid=(t // tm,),
        in_specs=[pl.BlockSpec((tm, 128), lambda i: (i, f_col // 128)), pl.BlockSpec((1, 128), lambda i: (0, 0))],
        out_specs=pl.BlockSpec((tm, 128), lambda i: (i, 0)),
        scratch_shapes=[pltpu.VMEM((8, 128), F32)],
        compiler_params=_params(("arbitrary",)),
    )(z, b_pad)


def _lane_halves():
    lane = lax.broadcasted_iota(jnp.int32, (1, 128), 1)
    return lane < HEAD_DIM


def _causal_mask(tq, tk, row0=0):
    row = row0 + lax.broadcasted_iota(jnp.int32, (tq, 1), 0)
    col = lax.broadcasted_iota(jnp.int32, (1, tk), 1)
    return col <= row


def _lane_one(lane):
    return (lax.broadcasted_iota(jnp.int32, (1, 128), 1) == lane).astype(BF16)


def _split3(x):
    hi = x.astype(BF16)
    rest = x - hi.astype(F32)
    mid = rest.astype(BF16)
    return hi, mid, (rest - mid.astype(F32)).astype(BF16)


def _split3_glue(x):
    hi = lax.reduce_precision(x, 8, 7)
    mid = lax.reduce_precision(x - hi, 8, 7)
    lo = lax.reduce_precision((x - hi) - mid, 8, 7)
    return hi.astype(BF16), mid.astype(BF16), lo.astype(BF16)


def _aug_pairs(cols):
    t = cols[0].shape[0]
    a = jnp.pad(jnp.stack(cols, axis=2), ((0, 0), (0, 0), (0, HEAD_DIM - len(cols))))
    a = a.reshape(t, 4, 2, HEAD_DIM)[:, :, ::-1, :]
    return jnp.transpose(a.reshape(t, 4, 128), (1, 0, 2))


def _attn_bias_operands(f_heads, lse_heads=None):
    t = f_heads.shape[0]
    one = jnp.ones((t, HEADS), BF16)
    row = lax.broadcasted_iota(jnp.int32, (t, 1), 0)
    fq = _split3_glue(f_heads)
    fk = _split3_glue(jnp.where(row < ROW_PAD, 1e9, f_heads))
    q_cols = list(fq) + [one] * 3
    k_cols = [one] * 3 + [-c for c in fk]
    if lse_heads is not None:
        q_cols += [-c for c in _split3_glue(lse_heads)]
        k_cols += [one] * 3
    return _aug_pairs(q_cols), _aug_pairs(k_cols)


def _attn_steps(nq, by_key):
    if by_key:
        pairs = [(qi, ki) for ki in range(nq) for qi in range(ki, nq)]
    else:
        pairs = [(qi, ki) for qi in range(nq) for ki in range(qi + 1)]
    return (jnp.array([p[0] for p in pairs], jnp.int32), jnp.array([p[1] for p in pairs], jnp.int32))


def _attn_fwd(z, aug_q, aug_k):
    t = z.shape[0]
    tq = tk = ROW_TILE
    nq = t // tq
    q_tab, k_tab = _attn_steps(nq, by_key=False)

    def body(qt_ref, kt_ref, q_ref, k_ref, v_ref, aq_ref, ak_ref, o_ref, lse_ref, m_ref, l_ref, acc_ref):
        step = pl.program_id(1)
        qi, ki = qt_ref[step], kt_ref[step]

        @pl.when(ki == 0)
        def _():
            m_ref[...] = jnp.full_like(m_ref, NEG)
            l_ref[...] = jnp.zeros_like(l_ref)
            acc_ref[...] = jnp.zeros_like(acc_ref)

        def sweep(diagonal):
            first = _lane_halves()
            q = (q_ref[...] * (HEAD_DIM ** -0.5)).astype(BF16)
            k = k_ref[...].astype(BF16)
            v = v_ref[...].astype(BF16)
            aq, ak = aq_ref[...], ak_ref[...]
            halves = (first, jnp.logical_not(first))
            qa = [jnp.where(lanes, q, aq) for lanes in halves]
            ka = [jnp.where(lanes, k, ak) for lanes in halves]
            va = [jnp.where(lanes, v, _lane_one(a0)) for lanes, a0 in zip(halves, (HEAD_DIM, 0))]
            chains = [(hh, r) for r in range(ATTN_ROW_PARTS) for hh in range(2)]
            rp = tq // ATTN_ROW_PARTS
            rows = [slice(r * rp, (r + 1) * rp) for _, r in chains]
            s = [_dot_nt(qa[hh][rw], ka[hh]) for (hh, _), rw in zip(chains, rows)]
            if diagonal:
                s = [jnp.where(_causal_mask(rp, tk, rw.start), s_c, NEG) for s_c, rw in zip(s, rows)]
            m_prev = [m_ref[rw, hh * HEAD_DIM:hh * HEAD_DIM + 1] for (hh, _), rw in zip(chains, rows)]
            m_new = [jnp.maximum(mp, jnp.max(s_c, axis=1, keepdims=True)) for mp, s_c in zip(m_prev, s)]
            p = [jnp.exp(s_c - m_c).astype(BF16) for s_c, m_c in zip(s, m_new)]
            pv = [_dot(p_c, va[hh]) for p_c, (hh, _) in zip(p, chains)]
            alpha = [jnp.exp(mp - m_c) for mp, m_c in zip(m_prev, m_new)]
            for r in range(ATTN_ROW_PARTS):
                (m0, m1), (al0, al1), (pv0, pv1) = [x[2 * r:2 * r + 2] for x in (m_new, alpha, pv)]
                rw = rows[2 * r]
                l0 = al0 * l_ref[rw, 0:1] + pv0[:, HEAD_DIM:HEAD_DIM + 1]
                l1 = al1 * l_ref[rw, HEAD_DIM:HEAD_DIM + 1] + pv1[:, 0:1]
                acc_ref[rw, :] = acc_ref[rw, :] * jnp.where(first, al0, al1) + jnp.where(first, pv0, pv1)
                m_ref[rw, :] = jnp.where(first, m0, m1)
                l_ref[rw, :] = jnp.where(first, l0, l1)

        @pl.when(ki < qi)
        def _():
            sweep(False)

        @pl.when(ki == qi)
        def _():
            sweep(True)
            o_ref[...] = acc_ref[...] / l_ref[...]
            lse_ref[...] = m_ref[...] + jnp.log(l_ref[...])

    return pl.pallas_call(
        body, name="attention_fwd",
        out_shape=[jax.ShapeDtypeStruct((t, ATTN_W), F32), jax.ShapeDtypeStruct((t, ATTN_W), F32)],
        grid_spec=pltpu.PrefetchScalarGridSpec(
            num_scalar_prefetch=2, grid=(4, int(q_tab.shape[0])),
            in_specs=[pl.BlockSpec((tq, 128), lambda p, s, qt, kt: (qt[s], p)),
                      pl.BlockSpec((tk, 128), lambda p, s, qt, kt: (kt[s], 4 + p)),
                      pl.BlockSpec((tk, 128), lambda p, s, qt, kt: (kt[s], 8 + p)),
                      pl.BlockSpec((None, tq, 128), lambda p, s, qt, kt: (p, qt[s], 0)),
                      pl.BlockSpec((None, tk, 128), lambda p, s, qt, kt: (p, kt[s], 0))],
            out_specs=[pl.BlockSpec((tq, 128), lambda p, s, qt, kt: (qt[s], p)),
                       pl.BlockSpec((tq, 128), lambda p, s, qt, kt: (qt[s], p))],
            scratch_shapes=[pltpu.VMEM((tq, 128), F32)] * 3),
        compiler_params=_params(("parallel", "arbitrary")),
    )(q_tab, k_tab, z, z, z, aug_q, aug_k)


def _attn_bwd(z, aug_q, aug_k, o, do):
    t = z.shape[0]
    tq = tk = ROW_TILE
    nq = t // tq
    q_tab, k_tab = _attn_steps(nq, by_key=True)
    tn = (((0,), (0,)), ((), ()))

    def body(qt_ref, kt_ref, q_ref, k_ref, v_ref, aq_ref, ak_ref, o_ref, do_ref,
             dq_ref, dk_ref, dv_ref, dfk_ref, dfq_ref):
        step = pl.program_id(1)
        qi, ki = qt_ref[step], kt_ref[step]
        rows = pl.ds(pl.multiple_of(qi * tq, tq), tq)

        @pl.when(ki == 0)
        def _():
            dq_ref[rows, :] = jnp.zeros((tq, 128), F32)
            dfq_ref[rows, :] = jnp.zeros((tq, 128), F32)

        @pl.when(qi == ki)
        def _():
            dk_ref[...] = jnp.zeros_like(dk_ref)
            dv_ref[...] = jnp.zeros_like(dv_ref)
            dfk_ref[...] = jnp.zeros_like(dfk_ref)

        def sweep(diagonal):
            first = _lane_halves()
            lane = lax.broadcasted_iota(jnp.int32, (1, 128), 1)
            scale = HEAD_DIM ** -0.5
            q = (q_ref[...] * scale).astype(BF16)
            k = k_ref[...].astype(BF16)
            v = v_ref[...].astype(BF16)
            do_ = do_ref[...]
            do16 = do_.astype(BF16)
            od = o_ref[...] * do_
            aq, ak = aq_ref[...], ak_ref[...]
            parts = []
            for hh in range(2):
                lanes = first if hh == 0 else jnp.logical_not(first)
                a0 = HEAD_DIM - hh * HEAD_DIM
                one = _lane_one(a0)
                d_hi, d_mid, d_lo = _split3(jnp.sum(jnp.where(lanes, od, 0.0), axis=1, keepdims=True))
                minus_delta = jnp.where(lane == a0, -d_hi, jnp.where(lane == a0 + 1, -d_mid,
                                        jnp.where(lane == a0 + 2, -d_lo, jnp.zeros((), BF16))))
                ones3 = ((lane >= a0) & (lane < a0 + 3)).astype(BF16)
                s = _dot_nt(jnp.where(lanes, q, aq), jnp.where(lanes, k, ak))
                p = jnp.exp(s)
                if diagonal:
                    p = jnp.where(_causal_mask(tq, tk), p, 0.0)
                dp = _dot_nt(jnp.where(lanes, do16, minus_delta), jnp.where(lanes, v, ones3))
                ds16 = (p * dp).astype(BF16)
                dv_h = lax.dot_general(p.astype(BF16), jnp.where(lanes, do16, jnp.zeros((), BF16)), tn,
                                       preferred_element_type=F32)
                dk_h = lax.dot_general(ds16, jnp.where(lanes, q, one), tn, preferred_element_type=F32)
                dq_h = _dot(ds16, jnp.where(lanes, k, one))
                parts.append((dq_h, dk_h, dv_h, a0))
            (dq0, dk0, dv0, a0), (dq1, dk1, dv1, a1) = parts
            dq_ref[rows, :] += jnp.where(first, dq0, dq1) * scale
            dfq_ref[rows, :] += jnp.where(first, dq0[:, a0:a0 + 1], dq1[:, a1:a1 + 1])
            dk_ref[...] += jnp.where(first, dk0, dk1)
            dfk_ref[...] += jnp.where(first, dk0[:, a0:a0 + 1], dk1[:, a1:a1 + 1])
            dv_ref[...] += dv0 + dv1

        @pl.when(qi > ki)
        def _():
            sweep(False)

        @pl.when(qi == ki)
        def _():
            sweep(True)

    qrow = lambda p, s, qt, kt: (qt[s], p)
    krow = lambda p, s, qt, kt: (kt[s], p)
    return pl.pallas_call(
        body, name="attention_bwd",
        out_shape=[jax.ShapeDtypeStruct((t, ATTN_W), F32)] * 5,
        grid_spec=pltpu.PrefetchScalarGridSpec(
            num_scalar_prefetch=2, grid=(4, int(q_tab.shape[0])),
            in_specs=[pl.BlockSpec((tq, 128), qrow),
                      pl.BlockSpec((tk, 128), lambda p, s, qt, kt: (kt[s], 4 + p)),
                      pl.BlockSpec((tk, 128), lambda p, s, qt, kt: (kt[s], 8 + p)),
                      pl.BlockSpec((None, tq, 128), lambda p, s, qt, kt: (p, qt[s], 0)),
                      pl.BlockSpec((None, tk, 128), lambda p, s, qt, kt: (p, kt[s], 0)),
                      pl.BlockSpec((tq, 128), qrow), pl.BlockSpec((tq, 128), qrow)],
            out_specs=[pl.BlockSpec((t, 128), lambda p, s, qt, kt: (0, p)),
                       pl.BlockSpec((tk, 128), krow), pl.BlockSpec((tk, 128), krow), pl.BlockSpec((tk, 128), krow),
                       pl.BlockSpec((t, 128), lambda p, s, qt, kt: (0, p))]),
        compiler_params=_params(("parallel", "arbitrary"), VMEM_BIG),
    )(q_tab, k_tab, z, z, z, aug_q, aug_k, o, do)


def _shifted(prev_rows, x, shift):
    tm = x.shape[0]
    return pltpu.roll(jnp.concatenate([prev_rows, x], axis=0), shift, 0)[8:8 + tm]


def _ahead(x, next_rows, shift):
    tm = x.shape[0]
    return pltpu.roll(jnp.concatenate([x, next_rows], axis=0), tm + 8 - shift, 0)[0:tm]


def _conv_col0(z):
    return (z.shape[1] - F_PAD - 3 * CONV_W) // CONV_W


def _conv_specs(tm, c0):
    cols = (c0, c0 + 1, c0 + 2)
    tiles = [pl.BlockSpec((tm, CONV_W), functools.partial(lambda i, c: (i, c), c=c)) for c in cols]
    halos = [pl.BlockSpec((8, CONV_W), functools.partial(lambda i, c: (jnp.maximum(i * (tm // 8) - 1, 0), c), c=c))
             for c in cols]
    return tiles, halos


def _conv_gate(z, conv_w):
    t = z.shape[0]
    tm = ROW_TILE
    nt = t // tm

    def body(cb_ref, cc_ref, ci_ref, hc_ref, hi_ref, w_ref, g_ref, gt_ref):
        i = pl.program_id(0)
        cc = cc_ref[...] * ci_ref[...]
        prev = jnp.where(i > 0, hc_ref[...] * hi_ref[...], 0.0)
        conv = w_ref[0:1, :] * _shifted(prev, cc, 2) + w_ref[1:2, :] * _shifted(prev, cc, 1) + w_ref[2:3, :] * cc
        g = cb_ref[...] * conv
        g_ref[...] = g.astype(BF16)
        gt_ref[...] = g.T.astype(BF16)

    (cb, cc, ci), (_, hc, hi) = _conv_specs(tm, _conv_col0(z))
    return pl.pallas_call(
        body, name="conv_gate_fwd",
        out_shape=[jax.ShapeDtypeStruct((t, CONV_W), BF16), jax.ShapeDtypeStruct((CONV_W, t), BF16)],
        grid=(nt,),
        in_specs=[cb, cc, ci, hc, hi, pl.BlockSpec((8, CONV_W), lambda i: (0, 0))],
        out_specs=[pl.BlockSpec((tm, CONV_W), lambda i: (i, 0)), pl.BlockSpec((CONV_W, tm), lambda i: (0, i))],
        compiler_params=_params(("parallel",)),
    )(z, z, z, z, z, conv_w)


def _conv_bwd(z, dg, conv_w):
    t = z.shape[0]
    tm = ROW_TILE
    nt = t // tm

    def body(cb_ref, cc_ref, ci_ref, hc_ref, hi_ref, dg_ref, ncb_ref, ndg_ref, w_ref, dz_ref, dw_ref):
        i = pl.program_id(0)

        @pl.when(i == 0)
        def _():
            dw_ref[...] = jnp.zeros_like(dw_ref)

        cb, c_c, c_in = cb_ref[...], cc_ref[...], ci_ref[...]
        cc = c_c * c_in
        prev = jnp.where(i > 0, hc_ref[...] * hi_ref[...], 0.0)
        cc1, cc2 = _shifted(prev, cc, 1), _shifted(prev, cc, 2)
        w0, w1, w2 = w_ref[0:1, :], w_ref[1:2, :], w_ref[2:3, :]
        conv = w0 * cc2 + w1 * cc1 + w2 * cc
        dgv = dg_ref[...]
        dconv = dgv * cb
        nxt = jnp.where(i < nt - 1, ndg_ref[...] * ncb_ref[...], 0.0)
        dcc = w2 * dconv + w1 * _ahead(dconv, nxt, 1) + w0 * _ahead(dconv, nxt, 2)
        dz_ref[:, 0:CONV_W] = (dgv * conv).astype(BF16)
        dz_ref[:, CONV_W:2 * CONV_W] = (dcc * c_in).astype(BF16)
        dz_ref[:, 2 * CONV_W:] = (dcc * c_c).astype(BF16)
        dw_ref[0:1, :] += jnp.sum(dconv * cc2, axis=0, keepdims=True)
        dw_ref[1:2, :] += jnp.sum(dconv * cc1, axis=0, keepdims=True)
        dw_ref[2:3, :] += jnp.sum(dconv * cc, axis=0, keepdims=True)

    c0 = _conv_col0(z)
    (cb, cc, ci), (_, hc, hi) = _conv_specs(tm, c0)
    nxt = lambda i, c: (jnp.minimum((i + 1) * (tm // 8), t // 8 - 1), c)
    return pl.pallas_call(
        body, name="conv_gate_bwd",
        out_shape=[jax.ShapeDtypeStruct((t, 3 * CONV_W), BF16), jax.ShapeDtypeStruct((8, CONV_W), F32)],
        grid=(nt,),
        in_specs=[cb, cc, ci, hc, hi, pl.BlockSpec((tm, CONV_W), lambda i: (i, 0)),
                  pl.BlockSpec((8, CONV_W), lambda i: nxt(i, c0)), pl.BlockSpec((8, CONV_W), lambda i: nxt(i, 0)),
                  pl.BlockSpec((8, CONV_W), lambda i: (0, 0))],
        out_specs=[pl.BlockSpec((tm, 3 * CONV_W), lambda i: (i, 0)), pl.BlockSpec((8, CONV_W), lambda i: (0, 0))],
        compiler_params=_params(("arbitrary",)),
    )(z, z, z, z, z, dg, z, dg, conv_w)


def _branch_mix(z, o, g, w_ab, w_cb, d):
    t = z.shape[0]
    tm = ROW_TILE
    ga_col = 0

    def body(o_ref, g_ref, ga_ref, gc_ref, wa_ref, wc_ref, mp_ref, mpt_ref, ot_ref):
        o_ = o_ref[...]
        ya = _dot(o_.astype(BF16), wa_ref[...])
        yc = _dot(g_ref[...], wc_ref[...])
        mp = _sigmoid(ga_ref[...]) * ya + _sigmoid(gc_ref[...]) * yc
        mp_ref[...] = mp.astype(BF16)
        mpt_ref[...] = mp.T.astype(BF16)
        ot_ref[...] = o_.T.astype(BF16)

    return pl.pallas_call(
        body, name="branch_mix_fwd",
        out_shape=[jax.ShapeDtypeStruct((t, d), BF16), jax.ShapeDtypeStruct((d, t), BF16),
                   jax.ShapeDtypeStruct((ATTN_W, t), BF16)],
        grid=(t // tm,),
        in_specs=[pl.BlockSpec((tm, ATTN_W), lambda i: (i, 0)), pl.BlockSpec((tm, CONV_W), lambda i: (i, 0)),
                  pl.BlockSpec((tm, d), lambda i: (i, ga_col)), pl.BlockSpec((tm, d), lambda i: (i, ga_col + 1)),
                  pl.BlockSpec((ATTN_W, d), lambda i: (0, 0)), pl.BlockSpec((CONV_W, d), lambda i: (0, 0))],
        out_specs=[pl.BlockSpec((tm, d), lambda i: (i, 0)), pl.BlockSpec((d, tm), lambda i: (0, i)),
                   pl.BlockSpec((ATTN_W, tm), lambda i: (0, i))],
        compiler_params=_params(("parallel",), VMEM_BIG),
    )(o, g, z, z, w_ab, w_cb)


def _branch_bwd(z, o, g, dmixed, w_out, w_ab, w_cb, d):
    t = z.shape[0]
    tm = ROW_TILE // 2
    ga_col = 0

    def body(dm_ref, o_ref, g_ref, ga_ref, gc_ref, wo_ref, wa_ref, wc_ref, dya_ref, dyc_ref, dgt_ref, do_ref, dg_ref):
        dmp = _dot_nt(dm_ref[...], wo_ref[...])
        ya = _dot(o_ref[...].astype(BF16), wa_ref[...])
        yc = _dot(g_ref[...], wc_ref[...])
        sa, sc = _sigmoid(ga_ref[...]), _sigmoid(gc_ref[...])
        dya = (dmp * sa).astype(BF16)
        dyc = (dmp * sc).astype(BF16)
        dya_ref[...] = dya
        dyc_ref[...] = dyc
        dgt_ref[:, :d] = (dmp * ya * sa * (1.0 - sa)).astype(BF16)
        dgt_ref[:, d:] = (dmp * yc * sc * (1.0 - sc)).astype(BF16)
        do_ref[...] = _dot_nt(dya, wa_ref[...])
        dg_ref[...] = _dot_nt(dyc, wc_ref[...])

    row = lambda i: (i, 0)
    fixed = lambda i: (0, 0)
    return pl.pallas_call(
        body, name="branch_mix_bwd",
        out_shape=[jax.ShapeDtypeStruct((t, d), BF16), jax.ShapeDtypeStruct((t, d), BF16),
                   jax.ShapeDtypeStruct((t, 2 * d), BF16), jax.ShapeDtypeStruct((t, ATTN_W), F32),
                   jax.ShapeDtypeStruct((t, CONV_W), F32)],
        grid=(t // tm,),
        in_specs=[pl.BlockSpec((tm, d), row), pl.BlockSpec((tm, ATTN_W), row), pl.BlockSpec((tm, CONV_W), row),
                  pl.BlockSpec((tm, d), lambda i: (i, ga_col)), pl.BlockSpec((tm, d), lambda i: (i, ga_col + 1)),
                  pl.BlockSpec((d, d), fixed), pl.BlockSpec((ATTN_W, d), fixed), pl.BlockSpec((CONV_W, d), fixed)],
        out_specs=[pl.BlockSpec((tm, d), row), pl.BlockSpec((tm, d), row), pl.BlockSpec((tm, 2 * d), row),
                   pl.BlockSpec((tm, ATTN_W), row), pl.BlockSpec((tm, CONV_W), row)],
        compiler_params=_params(("parallel",), VMEM_BIG),
    )(dmixed, o, g, z, z, w_out, w_ab, w_cb)


def _loss_grad(h, target_pad):
    t, d = h.shape
    tm = ROW_TILE

    def body(h_ref, t_ref, dy_ref, loss_ref):
        i = pl.program_id(0)

        @pl.when(i == 0)
        def _():
            loss_ref[...] = jnp.zeros_like(loss_ref)

        row = i * tm + lax.broadcasted_iota(jnp.int32, (tm, 1), 0)
        err = jnp.where(row >= N_FRONT, h_ref[...] - t_ref[...], 0.0)
        dy_ref[...] = err * (1.0 / d)
        per_row = jnp.sum(err * err, axis=1, keepdims=True) * (1.0 / d)
        loss_ref[...] += 0.5 * jnp.sum(per_row, axis=0, keepdims=True)

    return pl.pallas_call(
        body, name="loss_and_grad",
        out_shape=[jax.ShapeDtypeStruct((t, d), F32), jax.ShapeDtypeStruct((1, 128), F32)],
        grid=(t // tm,),
        in_specs=[pl.BlockSpec((tm, d), lambda i: (i, 0))] * 2,
        out_specs=[pl.BlockSpec((tm, d), lambda i: (i, 0)), pl.BlockSpec((1, 128), lambda i: (0, 0))],
        compiler_params=_params(("arbitrary",)),
    )(h, target_pad)


def _norm_bwd(name, x, g, dy, alpha):
    t, d = x.shape
    tm = ROW_TILE

    def body(x_ref, g_ref, dy_ref, dx_ref, dg_ref):
        @pl.when(pl.program_id(0) == 0)
        def _():
            dg_ref[...] = jnp.zeros_like(dg_ref)

        dx, dg = _rms_bwd(x_ref[...], g_ref[...], dy_ref[...])
        dx_ref[...] = (alpha * dx).astype(BF16)
        dg_ref[...] += alpha * dg

    row = pl.BlockSpec((tm, d), lambda i: (i, 0))
    vec = pl.BlockSpec((1, d), lambda i: (0, 0))
    return pl.pallas_call(
        body, name=name,
        out_shape=[jax.ShapeDtypeStruct((t, d), BF16), jax.ShapeDtypeStruct((1, d), F32)],
        grid=(t // tm,), in_specs=[row, vec, row], out_specs=[row, vec],
        compiler_params=_params(("arbitrary",)),
    )(x, g, dy)


def _ffn_bwd_mid(name, df, w_out, ab):
    t, d = df.shape
    cw = ab.shape[1] // 4
    tm = ROW_TILE

    def body(df_ref, w_ref, ab_ref, o_ref):
        ds = _dot_nt(df_ref[...], w_ref[...])
        a = ab_ref[:, :cw]
        b = ab_ref[:, cw:]
        sg = _sigmoid(a)
        o_ref[:, :cw] = (ds * b * (sg * (1.0 + a * (1.0 - sg)))).astype(BF16)
        o_ref[:, cw:] = (ds * (a * sg)).astype(BF16)

    return pl.pallas_call(
        body, name=name, out_shape=jax.ShapeDtypeStruct((t, 4 * cw), BF16),
        grid=(2, t // tm),
        in_specs=[pl.BlockSpec((tm, d), lambda j, i: (i, 0)), pl.BlockSpec((cw, d), lambda j, i: (j, 0)),
                  pl.BlockSpec((tm, 2 * cw), lambda j, i: (i, j))],
        out_specs=pl.BlockSpec((tm, 2 * cw), lambda j, i: (i, j)),
        compiler_params=_params(("parallel", "parallel"), VMEM_BIG),
    )(df, w_out, ab)


def _mm_nt_norm_bwd(name, dy, w, h, g, dh_in):
    t, kdim = dy.shape
    d = h.shape[1]
    tm = ROW_TILE // 2
    slots = w.ndim == 3

    def body(dy_ref, w_ref, h_ref, g_ref, dhi_ref, dh_ref, dg_ref):
        @pl.when(pl.program_id(0) == 0)
        def _():
            dg_ref[...] = jnp.zeros_like(dg_ref)

        if slots:
            cw = w_ref.shape[2]
            dn = _dot_nt(dy_ref[:, 0:cw], w_ref[_slot_of(0)])
            for k in range(1, 4):
                dn += _dot_nt(dy_ref[:, k * cw:(k + 1) * cw], w_ref[_slot_of(k)])
        else:
            dn = _dot_nt(dy_ref[...], w_ref[...])
        dx, dg = _rms_bwd(h_ref[...], g_ref[...], dn)
        dh_ref[...] = dhi_ref[...] + dx
        dg_ref[...] += dg

    row = pl.BlockSpec((tm, d), lambda i: (i, 0))
    vec = pl.BlockSpec((1, d), lambda i: (0, 0))
    return pl.pallas_call(
        body, name=name,
        out_shape=[jax.ShapeDtypeStruct((t, d), F32), jax.ShapeDtypeStruct((1, d), F32)],
        grid=(t // tm,),
        in_specs=[pl.BlockSpec((tm, kdim), lambda i: (i, 0)), pl.BlockSpec(w.shape, lambda i: (0,) * w.ndim),
                  row, vec, row],
        out_specs=[row, vec],
        compiler_params=_params(("arbitrary",), VMEM_BIG),
    )(dy, w, h, g, dh_in)


def _gate_bwd(df_pad, z, b_pad, f_col):
    t = z.shape[0]
    tm = ROW_TILE
    nt = t // tm

    def body(d_ref, z_ref, b_ref, dz_ref, db_ref, carry_ref):
        i = pl.program_id(0)

        @pl.when(i == 0)
        def _():
            carry_ref[...] = jnp.zeros_like(carry_ref)
            db_ref[...] = jnp.zeros_like(db_ref)

        tri = (lax.broadcasted_iota(jnp.int32, (tm, tm), 0) <= lax.broadcasted_iota(jnp.int32, (tm, tm), 1))
        tail = jnp.dot(tri.astype(F32), d_ref[...], preferred_element_type=F32, precision=lax.Precision.HIGHEST)
        tail = tail + carry_ref[0:1, :]
        carry_ref[...] = jnp.broadcast_to(tail[0:1, :], carry_ref.shape)
        row = (nt - 1 - i) * tm + lax.broadcasted_iota(jnp.int32, (tm, 1), 0)
        dlogit = jnp.where(row >= ROW_PAD, tail * _sigmoid(-(z_ref[...] + b_ref[...])), 0.0)
        dz_ref[...] = jnp.zeros_like(dz_ref)
        dz_ref[:, 0:128] = dlogit.astype(BF16)
        db_ref[...] += jnp.sum(dlogit, axis=0, keepdims=True)

    rev = lambda i: (nt - 1 - i, 0)
    return pl.pallas_call(
        body, name="forget_gate_bwd",
        out_shape=[jax.ShapeDtypeStruct((t, F_PAD), BF16), jax.ShapeDtypeStruct((1, 128), F32)],
        grid=(nt,),
        in_specs=[pl.BlockSpec((tm, 128), rev), pl.BlockSpec((tm, 128), lambda i: (nt - 1 - i, f_col // 128)),
                  pl.BlockSpec((1, 128), lambda i: (0, 0))],
        out_specs=[pl.BlockSpec((tm, F_PAD), rev), pl.BlockSpec((1, 128), lambda i: (0, 0))],
        scratch_shapes=[pltpu.VMEM((8, 128), F32)],
        compiler_params=_params(("arbitrary",)),
    )(df_pad, z, b_pad)


def _ffn_fwd(tag, n, w_in4, w_out, h, g_post, g_next):
    ab, s, s_t = _ffn_in(f"{tag}_in_fwd", n, w_in4)
    outs = _mm_resid_norm(f"{tag}_out_fwd", s, w_out, h, g_post, 0.5, g_next)
    return ab, s_t, outs


def _ffn_bwd(tag, dh, f, g_post, ab, s_t, n_t, w_in4, w_out, h_in, g_pre):
    d, cw = w_in4.shape[1], w_in4.shape[2]
    t = dh.shape[0]
    df, dg_post = _norm_bwd(f"{tag}_post_norm_bwd", f, g_post, dh, 0.5)
    dw_out = _weight_grad(f"{tag}_dw_out", s_t, df, d, out_rows=cw // 2)
    dab = _ffn_bwd_mid(f"{tag}_mid_bwd", df, w_out, ab)
    dh_in, dg_pre = _mm_nt_norm_bwd(f"{tag}_in_bwd", dab, w_in4, h_in, g_pre, dh)
    bk = _k_tile(t)
    dw_in = _matmul(
        f"{tag}_dw_in", n_t, dab, jax.ShapeDtypeStruct((4, d, cw), F32), (1, 4, t // bk),
        pl.BlockSpec((d, bk), lambda a, b, k: (0, k)), pl.BlockSpec((bk, cw), lambda a, b, k: (k, b)),
        pl.BlockSpec((None, d, cw), lambda a, b, k: (_slot_of(b), 0, 0)), vmem=VMEM_BIG)
    return dh_in, dg_post, dg_pre, dw_in, dw_out


def _pack_small(meta, conv, gains, b_forget):
    d = gains[0].shape[1]
    rows = [meta.reshape(4, d), jnp.pad(conv.reshape(1, 3 * 128), ((0, 0), (0, d - 3 * 128)))]
    rows += list(gains) + [jnp.pad(b_forget, ((0, 0), (0, d - HEADS)))]
    return jnp.concatenate(rows + [jnp.zeros((4, d), F32)], axis=0)


def _unpack_small(block):
    d = block.shape[1]
    meta = block[0:4].reshape(N_META, d // 4)
    conv = block[4, :3 * 128].reshape(1, 3, 128)
    gains = [block[5 + i:6 + i] for i in range(6)]
    return meta, conv, gains, block[11:12, :HEADS]


def kernel(x, meta_tokens, w_in, b_forget, conv_w, w_attn_branch, w_conv_branch, w_out, g_ffn1_pre, g_ffn1_post, w_ffn1_in, w_ffn1_out, g_mix_pre, g_mix_post, g_ffn2_pre, g_ffn2_post, w_ffn2_in, w_ffn2_out, loss_target, m_meta_tokens, m_w_in, m_b_forget, m_conv_w, m_w_attn_branch, m_w_conv_branch, m_w_out, m_g_ffn1_pre, m_g_ffn1_post, m_w_ffn1_in, m_w_ffn1_out, m_g_mix_pre, m_g_mix_post, m_g_ffn2_pre, m_g_ffn2_post, m_w_ffn2_in, m_w_ffn2_out, v_meta_tokens, v_w_in, v_b_forget, v_conv_w, v_w_attn_branch, v_w_conv_branch, v_w_out, v_g_ffn1_pre, v_g_ffn1_post, v_w_ffn1_in, v_w_ffn1_out, v_g_mix_pre, v_g_mix_post, v_g_ffn2_pre, v_g_ffn2_post, v_w_ffn2_in, v_w_ffn2_out):
    seq, d = x.shape[1], x.shape[2]
    t = seq + N_FRONT
    n_main = 3 * ATTN_W + 3 * CONV_W + 2 * d
    nz = n_main + F_PAD
    f_lo = 3 * ATTN_W
    c_arr = lax.axis_index("c").astype(jnp.int32).reshape(1)

    big = [w_in[0], w_attn_branch[0], w_conv_branch[0], w_out[0], w_ffn1_in[0], w_ffn1_out[0], w_ffn2_in[0], w_ffn2_out[0]]
    small_gather = jnp.concatenate(
        [meta_tokens.reshape(4, d), jnp.pad(conv_w.reshape(1, 3 * 128), ((0, 0), (0, d - 3 * 128))),
         jnp.zeros((11, d), F32)], axis=0)
    gathered = _all_gather([w.astype(BF16) for w in big] + [small_gather])
    w_in4, w_ab4, w_cb4, w_out4, w_f1_in4, w_f1_out4, w_f2_in4, w_f2_out4, small4 = gathered

    w_in_full = jnp.transpose(w_in4, (1, 0, 2)).reshape(d, 4 * w_in4.shape[2])
    g_lo = f_lo + HEADS + 3 * CONV_W
    w_in_pad = jnp.concatenate(
        [w_in_full[:, :f_lo], w_in_full[:, g_lo:], w_in_full[:, f_lo + HEADS:g_lo], w_in_full[:, f_lo:f_lo + HEADS],
         jnp.zeros((d, F_PAD - HEADS), BF16)], axis=1)
    w_ab = jnp.transpose(w_ab4, (1, 0, 2)).reshape(ATTN_W, d)
    w_cb = jnp.transpose(w_cb4, (1, 0, 2)).reshape(CONV_W, d)
    w_out_full = w_out4.reshape(d, d)
    w_f1_out = w_f1_out4.reshape(-1, d)
    w_f2_out = w_f2_out4.reshape(-1, d)
    meta_full = jnp.transpose(small4[:, 0:4].reshape(4, N_META, d // 4), (1, 0, 2)).reshape(N_META, d)
    conv_full = jnp.transpose(small4[:, 4, :3 * 128].reshape(4, 3, 128), (1, 0, 2)).reshape(3, CONV_W)
    conv_pad = jnp.pad(conv_full, ((0, 5), (0, 0)))
    b_pad = jnp.pad(b_forget, ((0, 0), (0, 128 - HEADS)))

    h0 = jnp.concatenate([jnp.zeros((ROW_PAD, d), F32), meta_full, x[0]], axis=0)
    target_pad = jnp.concatenate([jnp.zeros((N_FRONT, d), F32), loss_target[0]], axis=0)
    n1, n1_t = _norm_fwd("ffn1_pre_norm", h0, g_ffn1_pre)
    ab1, s1_t, (f1, h1, u, u_t) = _ffn_fwd("ffn1", n1, w_f1_in4, w_f1_out, h0, g_ffn1_post, g_mix_pre)
    qkv, z = _in_proj(u, w_in_pad)
    f_col = z.shape[1] - F_PAD
    f_cum = _gate_prep(z, b_pad, f_col)
    f_heads = f_cum[:, :HEADS]
    o, lse = _attn_fwd(qkv, *_attn_bias_operands(f_heads))
    g, g_t = _conv_gate(z, conv_pad)
    mp, mp_t, o_t = _branch_mix(z, o, g, w_ab, w_cb, d)
    mixed, h2, n2, n2_t = _mm_resid_norm("mix_out_fwd", mp, w_out_full, h1, g_mix_post, 1.0, g_ffn2_pre)
    ab2, s2_t, (f2, h3) = _ffn_fwd("ffn2", n2, w_f2_in4, w_f2_out, h2, g_ffn2_post, None)
    dh3, loss_part = _loss_grad(h3, target_pad)
    loss = lax.psum(loss_part[0, 0], ("x", "y", "c"))

    dh2, dg_f2_post, dg_f2_pre, dw_f2_in, dw_f2_out = _ffn_bwd(
        "ffn2", dh3, f2, g_ffn2_post, ab2, s2_t, n2_t, w_f2_in4, w_f2_out, h2, g_ffn2_pre)
    dmixed, dg_mix_post = _norm_bwd("mix_post_norm_bwd", mixed, g_mix_post, dh2, 1.0)
    dw_out = _weight_grad("mix_dw_out", mp_t, dmixed, d)
    dya, dyc, dgates, do, dgconv = _branch_bwd(z, o, g, dmixed, w_out_full, w_ab, w_cb, d)
    dw_ab = _weight_grad("mix_dw_attn_branch", o_t, dya, d)
    dw_cb = _weight_grad("mix_dw_conv_branch", g_t, dyc, d)
    dz_conv, dconv_w = _conv_bwd(z, dgconv, conv_pad)
    front = lax.broadcasted_iota(jnp.int32, (t, 1), 0) < ROW_PAD
    lse_heads = jnp.where(front, 1e9, lse[:, ::HEAD_DIM])
    dq, dk, dv, dfk, dfq = _attn_bwd(qkv, *_attn_bias_operands(f_heads, lse_heads), o, do)
    df_pad = jnp.pad((dfq - dfk)[:, ::HEAD_DIM], ((0, 0), (0, 128 - HEADS)))
    dz_f, db_forget = _gate_bwd(df_pad, z, b_pad, f_col)
    dz = jnp.concatenate([dq.astype(BF16), dk.astype(BF16), dv.astype(BF16), dgates, dz_conv, dz_f], axis=1)
    dh1, dg_mix_pre = _mm_nt_norm_bwd("mix_in_bwd", dz, w_in_pad, h1, g_mix_pre, dh2)
    dw_in_pad = _weight_grad("mix_dw_in", u_t, dz, 512)
    dh0, dg_f1_post, dg_f1_pre, dw_f1_in, dw_f1_out = _ffn_bwd(
        "ffn1", dh1, f1, g_ffn1_post, ab1, s1_t, n1_t, w_f1_in4, w_f1_out, h0, g_ffn1_pre)
    grad_x = dh0[N_FRONT:][None]
    dmeta = dh0[ROW_PAD:N_FRONT]

    cs = w_in4.shape[2]
    c_lo = f_lo + 2 * d
    dw_in_full = jnp.concatenate(
        [dw_in_pad[:, :f_lo], dw_in_pad[:, n_main:n_main + HEADS], dw_in_pad[:, c_lo:n_main], dw_in_pad[:, f_lo:c_lo]],
        axis=1)
    small_grad = jnp.stack([
        _pack_small(dmeta[:, j * (d // 4):(j + 1) * (d // 4)], dconv_w[:3, j * 128:(j + 1) * 128],
                    [dg_f1_pre, dg_f1_post, dg_mix_pre, dg_mix_post, dg_f2_pre, dg_f2_post], db_forget[:, :HEADS])
        for j in range(4)])
    slots = [
        jnp.transpose(dw_in_full.reshape(d, 4, cs), (1, 0, 2)),
        jnp.transpose(dw_ab.reshape(ATTN_W, 4, d // 4), (1, 0, 2)),
        jnp.transpose(dw_cb.reshape(CONV_W, 4, d // 4), (1, 0, 2)),
        dw_out.reshape(4, d // 4, d),
        dw_f1_in, dw_f1_out.reshape(4, -1, d), dw_f2_in, dw_f2_out.reshape(4, -1, d),
        small_grad,
    ]
    tags = ["w_in", "w_attn_branch", "w_conv_branch", "w_out", "w_ffn1_in", "w_ffn1_out", "w_ffn2_in", "w_ffn2_out", "small"]

    got = _pair_send_halves(slots)
    pair_sums = [_pair_add(tag, s, a, c_arr, F32 if tag == "small" else BF16) for tag, s, a in zip(tags, slots, got)]
    arrived = _chip_scatter(pair_sums)
    halves = [_chip_add(tag, a) for tag, a in zip(tags, arrived)]
    others = _pair_swap(halves)

    small = [g_ffn1_pre, g_ffn1_post, g_mix_pre, g_mix_post, g_ffn2_pre, g_ffn2_post]
    small_m = [m_g_ffn1_pre, m_g_ffn1_post, m_g_mix_pre, m_g_mix_post, m_g_ffn2_pre, m_g_ffn2_post]
    small_v = [v_g_ffn1_pre, v_g_ffn1_post, v_g_mix_pre, v_g_mix_post, v_g_ffn2_pre, v_g_ffn2_post]
    ws = big + [_pack_small(meta_tokens, conv_w[0], small, b_forget)]
    ms = [m_w_in[0], m_w_attn_branch[0], m_w_conv_branch[0], m_w_out[0], m_w_ffn1_in[0], m_w_ffn1_out[0],
          m_w_ffn2_in[0], m_w_ffn2_out[0], _pack_small(m_meta_tokens, m_conv_w[0], small_m, m_b_forget)]
    vs = [v_w_in[0], v_w_attn_branch[0], v_w_conv_branch[0], v_w_out[0], v_w_ffn1_in[0], v_w_ffn1_out[0],
          v_w_ffn2_in[0], v_w_ffn2_out[0], _pack_small(v_meta_tokens, v_conv_w[0], small_v, v_b_forget)]
    updates = [_adamw(tag, w, a, b, m, v, c_arr) for tag, w, a, b, m, v in zip(tags, ws, halves, others, ms, vs)]

    def leaves(big_vals, small_block):
        meta, conv, gains, bf = _unpack_small(small_block)
        w_in_, w_ab_, w_cb_, w_out_, f1_in, f1_out, f2_in, f2_out = [b[None] for b in big_vals]
        return [meta, w_in_, bf, conv, w_ab_, w_cb_, w_out_, gains[0], gains[1], f1_in, f1_out,
                gains[2], gains[3], gains[4], gains[5], f2_in, f2_out]

    out_g, out_d, out_m, out_v = [leaves([u_[k] for u_ in updates[:8]], updates[8][k]) for k in range(4)]
    return (loss, grad_x, *out_g, *out_d, *out_m, *out_v)
```

```python
import functools

import jax
import jax.numpy as jnp
from jax import lax
from jax.experimental import pallas as pl
from jax.experimental.pallas import tpu as pltpu
from jax.experimental.pallas import tpu_sc as plsc

N_META = 16
ROW_PAD = 112
N_FRONT = ROW_PAD + N_META
HEADS = 8
HEAD_DIM = 64
ATTN_W = HEADS * HEAD_DIM
CONV_W = 512
NORM_EPS = 1e-6
ROW_TILE = 640
F_PAD = 512
ATTN_ROW_PARTS = 1
NEG = -1e30
ADAM_LR = 0.001
ADAM_B1 = 0.9
ADAM_B2 = 0.999
ADAM_EPS = 1e-08
ADAM_WD = 0.01
ADAM_STEP = 10
VMEM_BIG = 56 * 1024 * 1024
MESH = pl.DeviceIdType.MESH
ANY = pl.BlockSpec(memory_space=pl.ANY)
F32 = jnp.float32
BF16 = jnp.bfloat16


def _params(sem, vmem=None):
    return pltpu.CompilerParams(dimension_semantics=sem, vmem_limit_bytes=vmem)


def _sigmoid(x):
    return 1.0 / (1.0 + jnp.exp(-x))


def _rstd(x):
    return lax.rsqrt(jnp.mean(x * x, axis=-1, keepdims=True) + NORM_EPS)


def _rms_bwd(x, g, dy):
    r = _rstd(x)
    xr = x * r
    gdy = g * dy
    dx = r * (gdy - xr * jnp.mean(xr * gdy, axis=-1, keepdims=True))
    return dx, jnp.sum(dy * xr, axis=0, keepdims=True)


def _dot(a, b):
    return jnp.dot(a, b, preferred_element_type=F32)


def _dot_nt(a, b):
    return lax.dot_general(a, b, (((1,), (1,)), ((), ())), preferred_element_type=F32)


def _k_tile(t):
    return 1664 if t % 1664 == 0 else ROW_TILE


def _place():
    x, y, c = lax.axis_index("x"), lax.axis_index("y"), lax.axis_index("c")
    chips = [(1 - x, y), (x, 1 - y), (1 - x, 1 - y)]
    return x, y, c, chips


def _all_gather(shards):
    n = len(shards)
    split = [s.reshape(2, s.shape[0] // 2, s.shape[1]) for s in shards]

    def body(*refs):
        ins, outs = refs[:n], refs[n:2 * n]
        send_sems, recv_sems = refs[2 * n:]
        x, y, c, chips = _place()
        me = 2 * x + y
        sibling = (x, y, 1 - c)

        def remote(i, k, slot, part, to, src=None):
            dst = outs[i].at[slot, part]
            return pltpu.make_async_remote_copy(
                src_ref=dst if src is None else src, dst_ref=dst,
                send_sem=send_sems.at[i, k], recv_sem=recv_sems.at[i, k],
                device_id=to, device_id_type=MESH)

        started = []
        for i in range(n):
            for k, (cx, cy) in enumerate(chips):
                cp = remote(i, k, me, c, (cx, cy, c), src=ins[i].at[c])
                cp.start()
                started.append(cp)
        for i in range(n):
            for k, (cx, cy) in enumerate(chips):
                remote(i, k, 2 * cx + cy, c, (x, y, c)).wait_recv()
                cp = remote(i, 3 + k, 2 * cx + cy, c, sibling)
                cp.start()
                started.append(cp)
        for i in range(n):
            for k, (cx, cy) in enumerate(chips):
                remote(i, 3 + k, 2 * cx + cy, 1 - c, (x, y, c)).wait_recv()
        for cp in started:
            cp.wait_send()

    outs = pl.pallas_call(
        body, name="all_gather_weights",
        out_shape=[jax.ShapeDtypeStruct((4,) + s.shape, s.dtype) for s in split],
        in_specs=[ANY] * n, out_specs=[ANY] * n,
        scratch_shapes=[pltpu.SemaphoreType.DMA((n, 6)), pltpu.SemaphoreType.DMA((n, 6))],
    )(*split)
    me =2 * lax.axis_index("x") + lax.axis_index("y")
    outs = [lax.dynamic_update_slice(o, s[None], (me, 0, 0, 0)) for o, s in zip(outs, split)]
    return [o.reshape((4,) + s.shape) for o, s in zip(outs, shards)]


def _all_gather_async(shards):
    n = len(shards)
    split = [s.reshape(2, s.shape[0] // 2, s.shape[1]) for s in shards]
    ins = [jax.new_ref(s, memory_space=pltpu.MemorySpace.HBM) for s in split]
    outs = [jax.empty_ref(jax.ShapeDtypeStruct((4,) + s.shape, s.dtype), memory_space=pltpu.MemorySpace.HBM)
            for s in split]

    @pl.kernel(mesh=plsc.ScalarSubcoreMesh(axis_name="sequencer", num_cores=1), name="all_gather_rest",
               scratch_types=(pltpu.SemaphoreType.DMA((n, 6)), pltpu.SemaphoreType.DMA((n, 6))),
               compiler_params=pltpu.CompilerParams(collective_id=1))
    def launch(send_sems, recv_sems):
        x, y, c, chips = _place()
        me = 2 * x + y
        sibling = (x, y, 1 - c)
        barrier = pltpu.get_barrier_semaphore()
        for peer in [(cx, cy, c) for cx, cy in chips] + [sibling]:
            pl.semaphore_signal(barrier, inc=1, device_id=peer, device_id_type=MESH)
        pl.semaphore_wait(barrier, 4)

        def remote(i, k, slot, part, to, src=None):
            dst = outs[i].at[slot, part]
            return pltpu.make_async_remote_copy(
                src_ref=dst if src is None else src, dst_ref=dst,
                send_sem=send_sems.at[i, k], recv_sem=recv_sems.at[i, k],
                device_id=to, device_id_type=MESH)

        started = []
        for i in range(n):
            for k, (cx, cy) in enumerate(chips):
                cp = remote(i, k, me, c, (cx, cy, c), src=ins[i].at[c])
                cp.start()
                started.append(cp)
        for i in range(n):
            for k, (cx, cy) in enumerate(chips):
                remote(i, k, 2 * cx + cy, c, (x, y, c)).wait_recv()
                cp = remote(i, 3 + k, 2 * cx + cy, c, sibling)
                cp.start()
                started.append(cp)
        for i in range(n):
            for k, (cx, cy) in enumerate(chips):
                remote(i, 3 + k, 2 * cx + cy, 1 - c, (x, y, c)).wait_recv()
        for cp in started:
            cp.wait_send()

    launch()
    me = 2 * lax.axis_index("x") + lax.axis_index("y")
    gathered = [lax.dynamic_update_slice(o[...], s[None], (me, 0, 0, 0)) for o, s in zip(outs, split)]
    return [g.reshape((4,) + s.shape) for g, s in zip(gathered, shards)]


def _pair_send_halves(grads):
    n = len(grads)

    def body(*refs):
        ins, outs = refs[:n], refs[n:2 * n]
        send_sems, recv_sems = refs[2 * n:]
        x, y, c, _ = _place()
        cps = []
        for i in range(n):
            half = ins[i].shape[1] // 2
            cp = pltpu.make_async_remote_copy(
                src_ref=ins[i].at[:, pl.ds((1 - c) * half, half)], dst_ref=outs[i],
                send_sem=send_sems.at[i], recv_sem=recv_sems.at[i],
                device_id=(x, y, 1 - c), device_id_type=MESH)
            cp.start()
            cps.append(cp)
        for cp in cps:
            cp.wait()

    return pl.pallas_call(
        body, name="grad_pair_exchange",
        out_shape=[jax.ShapeDtypeStruct((4, g.shape[1] // 2, g.shape[2]), g.dtype) for g in grads],
        in_specs=[ANY] * n, out_specs=[ANY] * n,
        scratch_shapes=[pltpu.SemaphoreType.DMA((n,)), pltpu.SemaphoreType.DMA((n,))],
    )(*grads)


def _chip_scatter(parts):
    n = len(parts)

    def body(*refs):
        ins, outs = refs[:n], refs[n:2 * n]
        send_sems, recv_sems, local_sems = refs[2 * n:]
        x, y, c, chips = _place()
        me = 2 * x + y
        sends, local = [], []
        for i in range(n):
            cp = pltpu.make_async_copy(ins[i].at[me], outs[i].at[me], local_sems.at[i])
            cp.start()
            local.append(cp)
            for k, (cx, cy) in enumerate(chips):
                cp = pltpu.make_async_remote_copy(
                    src_ref=ins[i].at[2 * cx + cy], dst_ref=outs[i].at[me],
                    send_sem=send_sems.at[i, k], recv_sem=recv_sems.at[i, k],
                    device_id=(cx, cy, c), device_id_type=MESH)
                cp.start()
                sends.append(cp)
        for i in range(n):
            for k, (cx, cy) in enumerate(chips):
                got = outs[i].at[2 * cx + cy]
                pltpu.make_async_remote_copy(
                    src_ref=got, dst_ref=got, send_sem=send_sems.at[i, k], recv_sem=recv_sems.at[i, k],
                    device_id=(x, y, c), device_id_type=MESH).wait_recv()
        for cp in sends:
            cp.wait_send()
        for cp in local:
            cp.wait()

    return pl.pallas_call(
        body, name="grad_chip_scatter",
        out_shape=[jax.ShapeDtypeStruct(p.shape, p.dtype) for p in parts],
        in_specs=[ANY] * n, out_specs=[ANY] * n,
        scratch_shapes=[pltpu.SemaphoreType.DMA((n, 3)), pltpu.SemaphoreType.DMA((n, 3)),
                        pltpu.SemaphoreType.DMA((n,))],
    )(*parts)


def _pair_swap(halves):
    n = len(halves)

    def body(*refs):
        ins, outs = refs[:n], refs[n:2 * n]
        send_sems, recv_sems = refs[2 * n:]
        x, y, c, _ = _place()
        cps = []
        for i in range(n):
            cp = pltpu.make_async_remote_copy(
                src_ref=ins[i], dst_ref=outs[i], send_sem=send_sems.at[i], recv_sem=recv_sems.at[i],
                device_id=(x, y, 1 - c), device_id_type=MESH)
            cp.start()
            cps.append(cp)
        for cp in cps:
            cp.wait()

    return pl.pallas_call(
        body, name="grad_pair_swap",
        out_shape=[jax.ShapeDtypeStruct(h.shape, h.dtype) for h in halves],
        in_specs=[ANY] * n, out_specs=[ANY] * n,
        scratch_shapes=[pltpu.SemaphoreType.DMA((n,)), pltpu.SemaphoreType.DMA((n,))],
    )(*halves)


def _row_block(rows, cols, n_bufs, budget=20 * 1024 * 1024):
    best = min(rows, 16)
    for b in range(16, rows + 1, 16):
        if rows % b == 0 and 2 * n_bufs * b * cols * 4 <= budget:
            best = b
    return best


def _pair_add(tag, grad, got, c_arr, out_dtype):
    _, rows, cols = grad.shape
    half = rows // 2
    bh = _row_block(half, cols, 3)
    nb = half // bh

    def body(c_ref, g_ref, a_ref, o_ref):
        o_ref[...] = (g_ref[...] + a_ref[...]).astype(out_dtype)

    return pl.pallas_call(
        body, name=f"pair_add_{tag}",
        out_shape=jax.ShapeDtypeStruct((4, half, cols), out_dtype),
        grid_spec=pltpu.PrefetchScalarGridSpec(
            num_scalar_prefetch=1, grid=(4, nb),
            in_specs=[pl.BlockSpec((None, bh, cols), lambda j, r, c: (j, c[0] * nb + r, 0)),
                      pl.BlockSpec((None, bh, cols), lambda j, r, c: (j, r, 0))],
            out_specs=pl.BlockSpec((None, bh, cols), lambda j, r, c: (j, r, 0))),
        compiler_params=_params(("parallel", "parallel")),
    )(c_arr, grad, got)


def _chip_add(tag, parts):
    _, half, cols = parts.shape
    bh = _row_block(half, cols, 5)

    def body(p_ref, o_ref):
        a, b, c, d = [p_ref[j].astype(F32) for j in range(4)]
        o_ref[...] = ((a + b) + c) + d

    return pl.pallas_call(
        body, name=f"chip_add_{tag}",
        out_shape=jax.ShapeDtypeStruct((half, cols), F32),
        grid=(half // bh,),
        in_specs=[pl.BlockSpec((4, bh, cols), lambda r: (0, r, 0))],
        out_specs=pl.BlockSpec((bh, cols), lambda r: (r, 0)),
        compiler_params=_params(("parallel",)),
    )(parts)


def _adamw(tag, w, mine, theirs, m, v, c_arr):
    rows, cols = w.shape
    half = rows // 2
    br = _row_block(half, cols, 9)
    nb = half // br

    def body(c_ref, w_ref, a_ref, b_ref, m_ref, v_ref, g_ref, d_ref, mo_ref, vo_ref):
        own = (pl.program_id(0) // nb) == c_ref[0]
        g = jnp.where(own, a_ref[...], b_ref[...])
        g_ref[...] = g
        m_new = ADAM_B1 * m_ref[...] + (1.0 - ADAM_B1) * g
        v_new = ADAM_B2 * v_ref[...] + (1.0 - ADAM_B2) * (g * g)
        m_hat = m_new / (1.0 - ADAM_B1 ** ADAM_STEP)
        v_hat = v_new / (1.0 - ADAM_B2 ** ADAM_STEP)
        d_ref[...] = -ADAM_LR * (m_hat / (jnp.sqrt(v_hat) + ADAM_EPS) + ADAM_WD * w_ref[...])
        mo_ref[...] = m_new
        vo_ref[...] = v_new

    spec = pl.BlockSpec((br, cols), lambda r, c: (r, 0))
    mine_spec = pl.BlockSpec((br, cols), lambda r, c: (jnp.clip(r - c[0] * nb, 0, nb - 1), 0))
    theirs_spec = pl.BlockSpec((br, cols), lambda r, c: (jnp.clip(r - (1 - c[0]) * nb, 0, nb - 1), 0))
    return pl.pallas_call(
        body, name=f"adamw_{tag}",
        out_shape=[jax.ShapeDtypeStruct((rows, cols), F32)] * 4,
        grid_spec=pltpu.PrefetchScalarGridSpec(
            num_scalar_prefetch=1, grid=(rows // br,),
            in_specs=[spec, mine_spec, theirs_spec, spec, spec], out_specs=[spec] * 4),
        compiler_params=_params(("arbitrary",)),
    )(c_arr, w, mine, theirs, m, v)


def _matmul(name, x, w, out_shape, grid, x_spec, w_spec, o_spec, *, nt=False, vmem=None):
    nk = grid[2]
    acc_shape = tuple(d for d in o_spec.block_shape if d is not None)

    def body(x_ref, w_ref, o_ref, acc_ref):
        k = pl.program_id(2)
        part = _dot_nt(x_ref[...], w_ref[...]) if nt else _dot(x_ref[...], w_ref[...])
        if nk == 1:
            o_ref[...] = part.astype(o_ref.dtype)
        else:
            @pl.when(k == 0)
            def _():
                acc_ref[...] = part

            @pl.when(k > 0)
            def _():
                acc_ref[...] += part

            @pl.when(k == nk - 1)
            def _():
                o_ref[...] = acc_ref[...].astype(o_ref.dtype)

    return pl.pallas_call(
        body, name=name, out_shape=out_shape, grid=grid,
        in_specs=[x_spec, w_spec], out_specs=o_spec,
        scratch_shapes=[pltpu.VMEM(acc_shape if nk > 1 else (8, 128), F32)],
        compiler_params=_params(("parallel", "parallel", "arbitrary"), vmem),
    )(x, w)


def _weight_grad(name, xt, dy, bn, out_rows=None):
    m, t = xt.shape
    n = dy.shape[1]
    bm = m if out_rows is None else out_rows
    bk = _k_tile(t)
    return _matmul(
        name, xt, dy, jax.ShapeDtypeStruct((m, n), F32), (m // bm, n // bn, t // bk),
        pl.BlockSpec((bm, bk), lambda a, b, k: (a, k)),
        pl.BlockSpec((bk, bn), lambda a, b, k: (k, b)),
        pl.BlockSpec((bm, bn), lambda a, b, k: (a, b)), vmem=VMEM_BIG)


def _norm_fwd(name, h, g):
    t, d = h.shape
    tm = ROW_TILE

    def body(h_ref, g_ref, n_ref, nt_ref):
        x = h_ref[...]
        y = x * _rstd(x) * g_ref[...]
        n_ref[...] = y.astype(BF16)
        nt_ref[...] = y.T.astype(BF16)

    return pl.pallas_call(
        body, name=name,
        out_shape=[jax.ShapeDtypeStruct((t, d), BF16), jax.ShapeDtypeStruct((d, t), BF16)],
        grid=(t // tm,),
        in_specs=[pl.BlockSpec((tm, d), lambda i: (i, 0)), pl.BlockSpec((1, d), lambda i: (0, 0))],
        out_specs=[pl.BlockSpec((tm, d), lambda i: (i, 0)), pl.BlockSpec((d, tm), lambda i: (0, i))],
        compiler_params=_params(("parallel",)),
    )(h, g)


def _slot_of(kk):
    return (kk % 2) * 2 + kk // 2


def _ffn_in(name, n, w4):
    t, d = n.shape
    cw = w4.shape[2]
    tm = ROW_TILE

    def body(x_ref, wg_ref, wu_ref, ab_ref, s_ref, st_ref):
        x = x_ref[...]
        a = _dot(x, wg_ref[...])
        b = _dot(x, wu_ref[...])
        ab_ref[:, :cw] = a
        ab_ref[:, cw:] = b
        s = a * _sigmoid(a) * b
        s_ref[...] = s.astype(BF16)
        st_ref[...] = s.T.astype(BF16)

    return pl.pallas_call(
        body, name=name,
        out_shape=[jax.ShapeDtypeStruct((t, 4 * cw), F32), jax.ShapeDtypeStruct((t, 2 * cw), BF16),
                   jax.ShapeDtypeStruct((2 * cw, t), BF16)],
        grid=(2, t // tm),
        in_specs=[pl.BlockSpec((tm, d), lambda j, i: (i, 0)),
                  pl.BlockSpec((None, d, cw), lambda j, i: (j, 0, 0)),
                  pl.BlockSpec((None, d, cw), lambda j, i: (2 + j, 0, 0))],
        out_specs=[pl.BlockSpec((tm, 2 * cw), lambda j, i: (i, j)),
                   pl.BlockSpec((tm, cw), lambda j, i: (i, j)),
                   pl.BlockSpec((cw, tm), lambda j, i: (j, i))],
        compiler_params=_params(("parallel", "parallel"), VMEM_BIG),
    )(n, w4, w4)


def _mm_resid_norm(name, x, w, h, g_post, alpha, g_next):
    t, kdim = x.shape
    d = w.shape[1]
    tm = ROW_TILE
    with_next = g_next is not None

    def body(x_ref, w_ref, h_ref, gp_ref, gn_ref, f_ref, hn_ref, *rest):
        f = _dot(x_ref[...], w_ref[...])
        f_ref[...] = f
        hn = h_ref[...] + alpha * (f * _rstd(f) * gp_ref[...])
        hn_ref[...] = hn
        if with_next:
            y = hn * _rstd(hn) * gn_ref[...]
            rest[0][...] = y.astype(BF16)
            rest[1][...] = y.T.astype(BF16)

    row = lambda i: (i, 0)
    vec = pl.BlockSpec((1, d), lambda i: (0, 0))
    out_shape = [jax.ShapeDtypeStruct((t, d), F32), jax.ShapeDtypeStruct((t, d), F32)]
    out_specs = [pl.BlockSpec((tm, d), row), pl.BlockSpec((tm, d), row)]
    if with_next:
        out_shape += [jax.ShapeDtypeStruct((t, d), BF16), jax.ShapeDtypeStruct((d, t), BF16)]
        out_specs += [pl.BlockSpec((tm, d), row), pl.BlockSpec((d, tm), lambda i: (0, i))]
    return pl.pallas_call(
        body, name=name, out_shape=out_shape, grid=(t // tm,),
        in_specs=[pl.BlockSpec((tm, kdim), row), pl.BlockSpec((kdim, d), lambda i: (0, 0)),
                  pl.BlockSpec((tm, d), row), vec, vec],
        out_specs=out_specs,
        compiler_params=_params(("parallel",), VMEM_BIG),
    )(x, w, h, g_post, g_post if g_next is None else g_next)


def _in_proj(u, w):
    t, d = u.shape
    nz = w.shape[1]
    nq = 3 * ATTN_W
    tm = ROW_TILE // 2

    def body(u_ref, w_ref, qkv_ref, z_ref):
        qkv_ref[...] = _dot(u_ref[...], w_ref[:, 0:nq]).astype(BF16)
        z_ref[...] = _dot(u_ref[...], w_ref[:, nq:])

    return pl.pallas_call(
        body, name="mix_in_proj",
        out_shape=[jax.ShapeDtypeStruct((t, nq), BF16), jax.ShapeDtypeStruct((t, nz - nq), F32)],
        grid=(t // tm,),
        in_specs=[pl.BlockSpec((tm, d), lambda i: (i, 0)), pl.BlockSpec((d, nz), lambda i: (0, 0))],
        out_specs=[pl.BlockSpec((tm, nq), lambda i: (i, 0)), pl.BlockSpec((tm, nz - nq), lambda i: (i, 0))],
        compiler_params=_params(("parallel",), VMEM_BIG),
    )(u, w)


def _gate_prep(z, b_pad, f_col):
    t = z.shape[0]
    tm = ROW_TILE

    def body(z_ref, b_ref, f_ref, carry_ref):
        i = pl.program_id(0)

        @pl.when(i == 0)
        def _():
            carry_ref[...] = jnp.zeros_like(carry_ref)

        xs = z_ref[...] + b_ref[...]
        logf = jnp.minimum(xs, 0.0) - jnp.log(1.0 + jnp.exp(-jnp.abs(xs)))
        row = i * tm + lax.broadcasted_iota(jnp.int32, (tm, 1), 0)
        logf = jnp.where(row >= ROW_PAD, logf, 0.0)
        tri = (lax.broadcasted_iota(jnp.int32, (tm, tm), 0) >= lax.broadcasted_iota(jnp.int32, (tm, tm), 1))
        f = jnp.dot(tri.astype(F32), logf, preferred_element_type=F32, precision=lax.Precision.HIGHEST)
        f = f + carry_ref[0:1, :]
        f_ref[...] = f
        carry_ref[...] = jnp.broadcast_to(f[tm - 1:tm, :], carry_ref.shape)

    return pl.pallas_call(
        body, name="forget_gate_cumsum", out_shape=jax.ShapeDtypeStruct((t, 128), F32),
        grid=(t // tm,),
        in_specs=[pl.BlockSpec((tm, 128), lambda i: (i, f_col // 128)), pl.BlockSpec((1, 128), lambda i: (0, 0))],
        out_specs=pl.BlockSpec((tm, 128), lambda i: (i, 0)),
        scratch_shapes=[pltpu.VMEM((8, 128), F32)],
        compiler_params=_params(("arbitrary",)),
    )(z, b_pad)


def _lane_halves():
    lane = lax.broadcasted_iota(jnp.int32, (1, 128), 1)
    return lane < HEAD_DIM


def _causal_mask(tq, tk, row0=0):
    row = row0 + lax.broadcasted_iota(jnp.int32, (tq, 1), 0)
    col = lax.broadcasted_iota(jnp.int32, (1, tk), 1)
    return col <= row


def _lane_one(lane):
    return (lax.broadcasted_iota(jnp.int32, (1, 128), 1) == lane).astype(BF16)


def _split3(x):
    hi = x.astype(BF16)
    rest = x - hi.astype(F32)
    mid = rest.astype(BF16)
    return hi, mid, (rest - mid.astype(F32)).astype(BF16)


def _split3_glue(x):
    hi = lax.reduce_precision(x, 8, 7)
    mid = lax.reduce_precision(x - hi, 8, 7)
    lo = lax.reduce_precision((x - hi) - mid, 8, 7)
    return hi.astype(BF16), mid.astype(BF16), lo.astype(BF16)


def _aug_pairs(cols):
    t = cols[0].shape[0]
    a = jnp.pad(jnp.stack(cols, axis=2), ((0, 0), (0, 0), (0, HEAD_DIM - len(cols))))
    a = a.reshape(t, 4, 2, HEAD_DIM)[:, :, ::-1, :]
    return jnp.transpose(a.reshape(t, 4, 128), (1, 0, 2))


def _attn_bias_operands(f_heads, lse_heads=None):
    t = f_heads.shape[0]
    one = jnp.ones((t, HEADS), BF16)
    row = lax.broadcasted_iota(jnp.int32, (t, 1), 0)
    fq = _split3_glue(f_heads)
    fk = _split3_glue(jnp.where(row < ROW_PAD, 1e9, f_heads))
    q_cols = list(fq) + [one] * 3
    k_cols = [one] * 3 + [-c for c in fk]
    if lse_heads is not None:
        q_cols += [-c for c in _split3_glue(lse_heads)]
        k_cols += [one] * 3
    return _aug_pairs(q_cols), _aug_pairs(k_cols)


def _attn_steps(nq, by_key):
    if by_key:
        pairs = [(qi, ki) for ki in range(nq) for qi in range(ki, nq)]
    else:
        pairs = [(qi, ki) for qi in range(nq) for ki in range(qi + 1)]
    return (jnp.array([p[0] for p in pairs], jnp.int32), jnp.array([p[1] for p in pairs], jnp.int32))


def _attn_fwd(z, aug_q, aug_k):
    t = z.shape[0]
    tq = tk = ROW_TILE
    nq = t // tq
    q_tab, k_tab = _attn_steps(nq, by_key=False)

    def body(qt_ref, kt_ref, q_ref, k_ref, v_ref, aq_ref, ak_ref, o_ref, lse_ref, m_ref, l_ref, acc_ref):
        step = pl.program_id(1)
        qi, ki = qt_ref[step], kt_ref[step]

        @pl.when(ki == 0)
        def _():
            m_ref[...] = jnp.full_like(m_ref, NEG)
            l_ref[...] = jnp.zeros_like(l_ref)
            acc_ref[...] = jnp.zeros_like(acc_ref)

        def sweep(diagonal):
            first = _lane_halves()
            q = (q_ref[...] * (HEAD_DIM ** -0.5)).astype(BF16)
            k = k_ref[...].astype(BF16)
            v = v_ref[...].astype(BF16)
            aq, ak = aq_ref[...], ak_ref[...]
            halves = (first, jnp.logical_not(first))
            qa = [jnp.where(lanes, q, aq) for lanes in halves]
            ka = [jnp.where(lanes, k, ak) for lanes in halves]
            va = [jnp.where(lanes, v, _lane_one(a0)) for lanes, a0 in zip(halves, (HEAD_DIM, 0))]
            chains = [(hh, r) for r in range(ATTN_ROW_PARTS) for hh in range(2)]
            rp = tq // ATTN_ROW_PARTS
            rows = [slice(r * rp, (r + 1) * rp) for _, r in chains]
            s = [_dot_nt(qa[hh][rw], ka[hh]) for (hh, _), rw in zip(chains, rows)]
            if diagonal:
                s = [jnp.where(_causal_mask(rp, tk, rw.start), s_c, NEG) for s_c, rw in zip(s, rows)]
            m_prev = [m_ref[rw, hh * HEAD_DIM:hh * HEAD_DIM + 1] for (hh, _), rw in zip(chains, rows)]
            m_new = [jnp.maximum(mp, jnp.max(s_c, axis=1, keepdims=True)) for mp, s_c in zip(m_prev, s)]
            p = [jnp.exp(s_c - m_c).astype(BF16) for s_c, m_c in zip(s, m_new)]
            pv = [_dot(p_c, va[hh]) for p_c, (hh, _) in zip(p, chains)]
            alpha = [jnp.exp(mp - m_c) for mp, m_c in zip(m_prev, m_new)]
            for r in range(ATTN_ROW_PARTS):
                (m0, m1), (al0, al1), (pv0, pv1) = [x[2 * r:2 * r + 2] for x in (m_new, alpha, pv)]
                rw = rows[2 * r]
                l0 = al0 * l_ref[rw, 0:1] + pv0[:, HEAD_DIM:HEAD_DIM + 1]
                l1 = al1 * l_ref[rw, HEAD_DIM:HEAD_DIM + 1] + pv1[:, 0:1]
                acc_ref[rw, :] = acc_ref[rw, :] * jnp.where(first, al0, al1) + jnp.where(first, pv0, pv1)
                m_ref[rw, :] = jnp.where(first, m0, m1)
                l_ref[rw, :] = jnp.where(first, l0, l1)

        @pl.when(ki < qi)
        def _():
            sweep(False)

        @pl.when(ki == qi)
        def _():
            sweep(True)
            o_ref[...] = acc_ref[...] / l_ref[...]
            lse_ref[...] = m_ref[...] + jnp.log(l_ref[...])

    return pl.pallas_call(
        body, name="attention_fwd",
        out_shape=[jax.ShapeDtypeStruct((t, ATTN_W), F32), jax.ShapeDtypeStruct((t, ATTN_W), F32)],
        grid_spec=pltpu.PrefetchScalarGridSpec(
            num_scalar_prefetch=2, grid=(4, int(q_tab.shape[0])),
            in_specs=[pl.BlockSpec((tq, 128), lambda p, s, qt, kt: (qt[s], p)),
                      pl.BlockSpec((tk, 128), lambda p, s, qt, kt: (kt[s], 4 + p)),
                      pl.BlockSpec((tk, 128), lambda p, s, qt, kt: (kt[s], 8 + p)),
                      pl.BlockSpec((None, tq, 128), lambda p, s, qt, kt: (p, qt[s], 0)),
                      pl.BlockSpec((None, tk, 128), lambda p, s, qt, kt: (p, kt[s], 0))],
            out_specs=[pl.BlockSpec((tq, 128), lambda p, s, qt, kt: (qt[s], p)),
                       pl.BlockSpec((tq, 128), lambda p, s, qt, kt: (qt[s], p))],
            scratch_shapes=[pltpu.VMEM((tq, 128), F32)] * 3),
        compiler_params=_params(("parallel", "arbitrary")),
    )(q_tab, k_tab, z, z, z, aug_q, aug_k)


def _attn_bwd(z, aug_q, aug_k, o, do):
    t = z.shape[0]
    tq = tk = ROW_TILE
    nq = t // tq
    q_tab, k_tab = _attn_steps(nq, by_key=True)
    tn = (((0,), (0,)), ((), ()))

    def body(qt_ref, kt_ref, q_ref, k_ref, v_ref, aq_ref, ak_ref, o_ref, do_ref,
             dq_ref, dk_ref, dv_ref, dfk_ref, dfq_ref):
        step = pl.program_id(1)
        qi, ki = qt_ref[step], kt_ref[step]
        rows = pl.ds(pl.multiple_of(qi * tq, tq), tq)

        @pl.when(ki == 0)
        def _():
            dq_ref[rows, :] = jnp.zeros((tq, 128), F32)
            dfq_ref[rows, :] = jnp.zeros((tq, 128), F32)

        @pl.when(qi == ki)
        def _():
            dk_ref[...] = jnp.zeros_like(dk_ref)
            dv_ref[...] = jnp.zeros_like(dv_ref)
            dfk_ref[...] = jnp.zeros_like(dfk_ref)

        def sweep(diagonal):
            first = _lane_halves()
            lane = lax.broadcasted_iota(jnp.int32, (1, 128), 1)
            scale = HEAD_DIM ** -0.5
            q = (q_ref[...] * scale).astype(BF16)
            k = k_ref[...].astype(BF16)
            v = v_ref[...].astype(BF16)
            do_ = do_ref[...]
            do16 = do_.astype(BF16)
            od = o_ref[...] * do_
            aq, ak = aq_ref[...], ak_ref[...]
            halves = (first, jnp.logical_not(first))
            a0, a1 = HEAD_DIM, 0
            dos, vs = [], []
            for lanes, a in zip(halves, (a0, a1)):
                d_hi, d_mid, d_lo = _split3(jnp.sum(jnp.where(lanes, od, 0.0), axis=1, keepdims=True))
                minus_delta = jnp.where(lane == a, -d_hi, jnp.where(lane == a + 1, -d_mid,
                                        jnp.where(lane == a + 2, -d_lo, jnp.zeros((), BF16))))
                dos.append(jnp.where(lanes, do16, minus_delta))
                vs.append(jnp.where(lanes, v, ((lane >= a) & (lane < a + 3)).astype(BF16)))
            s = [_dot_nt(jnp.where(lanes, q, aq), jnp.where(lanes, k, ak)) for lanes in halves]
            dp = [_dot_nt(do_h, v_h) for do_h, v_h in zip(dos, vs)]
            p = [jnp.exp(s_h) for s_h in s]
            if diagonal:
                p = [jnp.where(_causal_mask(tq, tk), p_h, 0.0) for p_h in p]
            ds16 = [(p_h * dp_h).astype(BF16) for p_h, dp_h in zip(p, dp)]
            dv0, dv1 = [lax.dot_general(p_h.astype(BF16), jnp.where(lanes, do16, jnp.zeros((), BF16)), tn,
                                        preferred_element_type=F32) for p_h, lanes in zip(p, halves)]
            dk0, dk1 = [lax.dot_general(ds_h, jnp.where(lanes, q, _lane_one(a)), tn, preferred_element_type=F32)
                        for ds_h, lanes, a in zip(ds16, halves, (a0, a1))]
            dq0, dq1 = [_dot(ds_h, jnp.where(lanes, k, _lane_one(a))) for ds_h, lanes, a in zip(ds16, halves, (a0, a1))]
            dq_ref[rows, :] += jnp.where(first, dq0, dq1) * scale
            dfq_ref[rows, :] += jnp.where(first, dq0[:, a0:a0 + 1], dq1[:, a1:a1 + 1])
            dk_ref[...] += jnp.where(first, dk0, dk1)
            dfk_ref[...] += jnp.where(first, dk0[:, a0:a0 + 1], dk1[:, a1:a1 + 1])
            dv_ref[...] += dv0 + dv1

        @pl.when(qi > ki)
        def _():
            sweep(False)

        @pl.when(qi == ki)
        def _():
            sweep(True)

    qrow = lambda p, s, qt, kt: (qt[s], p)
    krow = lambda p, s, qt, kt: (kt[s], p)
    return pl.pallas_call(
        body, name="attention_bwd",
        out_shape=[jax.ShapeDtypeStruct((t, ATTN_W), F32)] * 5,
        grid_spec=pltpu.PrefetchScalarGridSpec(
            num_scalar_prefetch=2, grid=(4, int(q_tab.shape[0])),
            in_specs=[pl.BlockSpec((tq, 128), qrow),
                      pl.BlockSpec((tk, 128), lambda p, s, qt, kt: (kt[s], 4 + p)),
                      pl.BlockSpec((tk, 128), lambda p, s, qt, kt: (kt[s], 8 + p)),
                      pl.BlockSpec((None, tq, 128), lambda p, s, qt, kt: (p, qt[s], 0)),
                      pl.BlockSpec((None, tk, 128), lambda p, s, qt, kt: (p, kt[s], 0)),
                      pl.BlockSpec((tq, 128), qrow), pl.BlockSpec((tq, 128), qrow)],
            out_specs=[pl.BlockSpec((t, 128), lambda p, s, qt, kt: (0, p)),
                       pl.BlockSpec((tk, 128), krow), pl.BlockSpec((tk, 128), krow), pl.BlockSpec((tk, 128), krow),
                       pl.BlockSpec((t, 128), lambda p, s, qt, kt: (0, p))]),
        compiler_params=_params(("parallel", "arbitrary"), VMEM_BIG),
    )(q_tab, k_tab, z, z, z, aug_q, aug_k, o, do)


def _shifted(prev_rows, x, shift):
    tm = x.shape[0]
    return pltpu.roll(jnp.concatenate([prev_rows, x], axis=0), shift, 0)[8:8 + tm]


def _ahead(x, next_rows, shift):
    tm = x.shape[0]
    return pltpu.roll(jnp.concatenate([x, next_rows], axis=0), tm + 8 - shift, 0)[0:tm]


def _conv_col0(z):
    return (z.shape[1] - F_PAD - 3 * CONV_W) // CONV_W


def _conv_specs(tm, c0):
    cols = (c0, c0 + 1, c0 + 2)
    tiles = [pl.BlockSpec((tm, CONV_W), functools.partial(lambda i, c: (i, c), c=c)) for c in cols]
    halos = [pl.BlockSpec((8, CONV_W), functools.partial(lambda i, c: (jnp.maximum(i * (tm // 8) - 1, 0), c), c=c))
             for c in cols]
    return tiles, halos


def _conv_gate(z, conv_w):
    t = z.shape[0]
    tm = ROW_TILE
    nt = t // tm

    def body(cb_ref, cc_ref, ci_ref, hc_ref, hi_ref, w_ref, g_ref, gt_ref):
        i = pl.program_id(0)
        cc = cc_ref[...] * ci_ref[...]
        prev = jnp.where(i > 0, hc_ref[...] * hi_ref[...], 0.0)
        conv = w_ref[0:1, :] * _shifted(prev, cc, 2) + w_ref[1:2, :] * _shifted(prev, cc, 1) + w_ref[2:3, :] * cc
        g = cb_ref[...] * conv
        g_ref[...] = g.astype(BF16)
        gt_ref[...] = g.T.astype(BF16)

    (cb, cc, ci), (_, hc, hi) = _conv_specs(tm, _conv_col0(z))
    return pl.pallas_call(
        body, name="conv_gate_fwd",
        out_shape=[jax.ShapeDtypeStruct((t, CONV_W), BF16), jax.ShapeDtypeStruct((CONV_W, t), BF16)],
        grid=(nt,),
        in_specs=[cb, cc, ci, hc, hi, pl.BlockSpec((8, CONV_W), lambda i: (0, 0))],
        out_specs=[pl.BlockSpec((tm, CONV_W), lambda i: (i, 0)), pl.BlockSpec((CONV_W, tm), lambda i: (0, i))],
        compiler_params=_params(("parallel",)),
    )(z, z, z, z, z, conv_w)


def _conv_bwd(z, dg, conv_w):
    t = z.shape[0]
    tm = ROW_TILE
    nt = t // tm

    def body(cb_ref, cc_ref, ci_ref, hc_ref, hi_ref, dg_ref, ncb_ref, ndg_ref, w_ref, dz_ref, dw_ref):
        i = pl.program_id(0)

        @pl.when(i == 0)
        def _():
            dw_ref[...] = jnp.zeros_like(dw_ref)

        cb, c_c, c_in = cb_ref[...], cc_ref[...], ci_ref[...]
        cc = c_c * c_in
        prev = jnp.where(i > 0, hc_ref[...] * hi_ref[...], 0.0)
        cc1, cc2 = _shifted(prev, cc, 1), _shifted(prev, cc, 2)
        w0, w1, w2 = w_ref[0:1, :], w_ref[1:2, :], w_ref[2:3, :]
        conv = w0 * cc2 + w1 * cc1 + w2 * cc
        dgv = dg_ref[...]
        dconv = dgv * cb
        nxt = jnp.where(i < nt - 1, ndg_ref[...] * ncb_ref[...], 0.0)
        dcc = w2 * dconv + w1 * _ahead(dconv, nxt, 1) + w0 * _ahead(dconv, nxt, 2)
        dz_ref[:, 0:CONV_W] = (dgv * conv).astype(BF16)
        dz_ref[:, CONV_W:2 * CONV_W] = (dcc * c_in).astype(BF16)
        dz_ref[:, 2 * CONV_W:] = (dcc * c_c).astype(BF16)
        dw_ref[0:1, :] += jnp.sum(dconv * cc2, axis=0, keepdims=True)
        dw_ref[1:2, :] += jnp.sum(dconv * cc1, axis=0, keepdims=True)
        dw_ref[2:3, :] += jnp.sum(dconv * cc, axis=0, keepdims=True)

    c0 = _conv_col0(z)
    (cb, cc, ci), (_, hc, hi) = _conv_specs(tm, c0)
    nxt = lambda i, c: (jnp.minimum((i + 1) * (tm // 8), t // 8 - 1), c)
    return pl.pallas_call(
        body, name="conv_gate_bwd",
        out_shape=[jax.ShapeDtypeStruct((t, 3 * CONV_W), BF16), jax.ShapeDtypeStruct((8, CONV_W), F32)],
        grid=(nt,),
        in_specs=[cb, cc, ci, hc, hi, pl.BlockSpec((tm, CONV_W), lambda i: (i, 0)),
                  pl.BlockSpec((8, CONV_W), lambda i: nxt(i, c0)), pl.BlockSpec((8, CONV_W), lambda i: nxt(i, 0)),
                  pl.BlockSpec((8, CONV_W), lambda i: (0, 0))],
        out_specs=[pl.BlockSpec((tm, 3 * CONV_W), lambda i: (i, 0)), pl.BlockSpec((8, CONV_W), lambda i: (0, 0))],
        compiler_params=_params(("arbitrary",)),
    )(z, z, z, z, z, dg, z, dg, conv_w)


def _branch_mix(z, o, g, w_ab, w_cb, d):
    t = z.shape[0]
    tm = ROW_TILE
    ga_col = 0

    def body(o_ref, g_ref, ga_ref, gc_ref, wa_ref, wc_ref, mp_ref, mpt_ref, ot_ref):
        o_ = o_ref[...]
        ya = _dot(o_.astype(BF16), wa_ref[...])
        yc = _dot(g_ref[...], wc_ref[...])
        mp = _sigmoid(ga_ref[...]) * ya + _sigmoid(gc_ref[...]) * yc
        mp_ref[...] = mp.astype(BF16)
        mpt_ref[...] = mp.T.astype(BF16)
        ot_ref[...] = o_.T.astype(BF16)

    return pl.pallas_call(
        body, name="branch_mix_fwd",
        out_shape=[jax.ShapeDtypeStruct((t, d), BF16), jax.ShapeDtypeStruct((d, t), BF16),
                   jax.ShapeDtypeStruct((ATTN_W, t), BF16)],
        grid=(t // tm,),
        in_specs=[pl.BlockSpec((tm, ATTN_W), lambda i: (i, 0)), pl.BlockSpec((tm, CONV_W), lambda i: (i, 0)),
                  pl.BlockSpec((tm, d), lambda i: (i, ga_col)), pl.BlockSpec((tm, d), lambda i: (i, ga_col + 1)),
                  pl.BlockSpec((ATTN_W, d), lambda i: (0, 0)), pl.BlockSpec((CONV_W, d), lambda i: (0, 0))],
        out_specs=[pl.BlockSpec((tm, d), lambda i: (i, 0)), pl.BlockSpec((d, tm), lambda i: (0, i)),
                   pl.BlockSpec((ATTN_W, tm), lambda i: (0, i))],
        compiler_params=_params(("parallel",), VMEM_BIG),
    )(o, g, z, z, w_ab, w_cb)


def _branch_bwd(z, o, g, dmixed, w_out, w_ab, w_cb, d):
    t = z.shape[0]
    tm = ROW_TILE // 2
    ga_col = 0

    def body(dm_ref, o_ref, g_ref, ga_ref, gc_ref, wo_ref, wa_ref, wc_ref, dya_ref, dyc_ref, dgt_ref, do_ref, dg_ref):
        dmp = _dot_nt(dm_ref[...], wo_ref[...])
        ya = _dot(o_ref[...].astype(BF16), wa_ref[...])
        yc = _dot(g_ref[...], wc_ref[...])
        sa, sc = _sigmoid(ga_ref[...]), _sigmoid(gc_ref[...])
        dya = (dmp * sa).astype(BF16)
        dyc = (dmp * sc).astype(BF16)
        dya_ref[...] = dya
        dyc_ref[...] = dyc
        dgt_ref[:, :d] = (dmp * ya * sa * (1.0 - sa)).astype(BF16)
        dgt_ref[:, d:] = (dmp * yc * sc * (1.0 - sc)).astype(BF16)
        do_ref[...] = _dot_nt(dya, wa_ref[...])
        dg_ref[...] = _dot_nt(dyc, wc_ref[...])

    row = lambda i: (i, 0)
    fixed = lambda i: (0, 0)
    return pl.pallas_call(
        body, name="branch_mix_bwd",
        out_shape=[jax.ShapeDtypeStruct((t, d), BF16), jax.ShapeDtypeStruct((t, d), BF16),
                   jax.ShapeDtypeStruct((t, 2 * d), BF16), jax.ShapeDtypeStruct((t, ATTN_W), F32),
                   jax.ShapeDtypeStruct((t, CONV_W), F32)],
        grid=(t // tm,),
        in_specs=[pl.BlockSpec((tm, d), row), pl.BlockSpec((tm, ATTN_W), row), pl.BlockSpec((tm, CONV_W), row),
                  pl.BlockSpec((tm, d), lambda i: (i, ga_col)), pl.BlockSpec((tm, d), lambda i: (i, ga_col + 1)),
                  pl.BlockSpec((d, d), fixed), pl.BlockSpec((ATTN_W, d), fixed), pl.BlockSpec((CONV_W, d), fixed)],
        out_specs=[pl.BlockSpec((tm, d), row), pl.BlockSpec((tm, d), row), pl.BlockSpec((tm, 2 * d), row),
                   pl.BlockSpec((tm, ATTN_W), row), pl.BlockSpec((tm, CONV_W), row)],
        compiler_params=_params(("parallel",), VMEM_BIG),
    )(dmixed, o, g, z, z, w_out, w_ab, w_cb)


def _loss_grad(h, target_pad):
    t, d = h.shape
    tm = ROW_TILE

    def body(h_ref, t_ref, dy_ref, loss_ref):
        i = pl.program_id(0)

        @pl.when(i == 0)
        def _():
            loss_ref[...] = jnp.zeros_like(loss_ref)

        row = i * tm + lax.broadcasted_iota(jnp.int32, (tm, 1), 0)
        err = jnp.where(row >= N_FRONT, h_ref[...] - t_ref[...], 0.0)
        dy_ref[...] = err * (1.0 / d)
        per_row = jnp.sum(err * err, axis=1, keepdims=True) * (1.0 / d)
        loss_ref[...] += 0.5 * jnp.sum(per_row, axis=0, keepdims=True)

    return pl.pallas_call(
        body, name="loss_and_grad",
        out_shape=[jax.ShapeDtypeStruct((t, d), F32), jax.ShapeDtypeStruct((1, 128), F32)],
        grid=(t // tm,),
        in_specs=[pl.BlockSpec((tm, d), lambda i: (i, 0))] * 2,
        out_specs=[pl.BlockSpec((tm, d), lambda i: (i, 0)), pl.BlockSpec((1, 128), lambda i: (0, 0))],
        compiler_params=_params(("arbitrary",)),
    )(h, target_pad)


def _norm_bwd(name, x, g, dy, alpha):
    t, d = x.shape
    tm = ROW_TILE

    def body(x_ref, g_ref, dy_ref, dx_ref, dg_ref):
        @pl.when(pl.program_id(0) == 0)
        def _():
            dg_ref[...] = jnp.zeros_like(dg_ref)

        dx, dg = _rms_bwd(x_ref[...], g_ref[...], dy_ref[...])
        dx_ref[...] = (alpha * dx).astype(BF16)
        dg_ref[...] += alpha * dg

    row = pl.BlockSpec((tm, d), lambda i: (i, 0))
    vec = pl.BlockSpec((1, d), lambda i: (0, 0))
    return pl.pallas_call(
        body, name=name,
        out_shape=[jax.ShapeDtypeStruct((t, d), BF16), jax.ShapeDtypeStruct((1, d), F32)],
        grid=(t // tm,), in_specs=[row, vec, row], out_specs=[row, vec],
        compiler_params=_params(("arbitrary",)),
    )(x, g, dy)


def _ffn_bwd_mid(name, df, w_out, ab):
    t, d = df.shape
    cw = ab.shape[1] // 4
    tm = ROW_TILE

    def body(df_ref, w_ref, ab_ref, o_ref):
        ds = _dot_nt(df_ref[...], w_ref[...])
        a = ab_ref[:, :cw]
        b = ab_ref[:, cw:]
        sg = _sigmoid(a)
        o_ref[:, :cw] = (ds * b * (sg * (1.0 + a * (1.0 - sg)))).astype(BF16)
        o_ref[:, cw:] = (ds * (a * sg)).astype(BF16)

    return pl.pallas_call(
        body, name=name, out_shape=jax.ShapeDtypeStruct((t, 4 * cw), BF16),
        grid=(2, t // tm),
        in_specs=[pl.BlockSpec((tm, d), lambda j, i: (i, 0)), pl.BlockSpec((cw, d), lambda j, i: (j, 0)),
                  pl.BlockSpec((tm, 2 * cw), lambda j, i: (i, j))],
        out_specs=pl.BlockSpec((tm, 2 * cw), lambda j, i: (i, j)),
        compiler_params=_params(("parallel", "parallel"), VMEM_BIG),
    )(df, w_out, ab)


def _mm_nt_norm_bwd(name, dy, w, h, g, dh_in):
    t, kdim = dy.shape
    d = h.shape[1]
    tm = ROW_TILE // 2
    slots = w.ndim == 3

    def body(dy_ref, w_ref, h_ref, g_ref, dhi_ref, dh_ref, dg_ref):
        @pl.when(pl.program_id(0) == 0)
        def _():
            dg_ref[...] = jnp.zeros_like(dg_ref)

        if slots:
            cw = w_ref.shape[2]
            dn = _dot_nt(dy_ref[:, 0:cw], w_ref[_slot_of(0)])
            for k in range(1, 4):
                dn += _dot_nt(dy_ref[:, k * cw:(k + 1) * cw], w_ref[_slot_of(k)])
        else:
            dn = _dot_nt(dy_ref[...], w_ref[...])
        dx, dg = _rms_bwd(h_ref[...], g_ref[...], dn)
        dh_ref[...] = dhi_ref[...] + dx
        dg_ref[...] += dg

    row = pl.BlockSpec((tm, d), lambda i: (i, 0))
    vec = pl.BlockSpec((1, d), lambda i: (0, 0))
    return pl.pallas_call(
        body, name=name,
        out_shape=[jax.ShapeDtypeStruct((t, d), F32), jax.ShapeDtypeStruct((1, d), F32)],
        grid=(t // tm,),
        in_specs=[pl.BlockSpec((tm, kdim), lambda i: (i, 0)), pl.BlockSpec(w.shape, lambda i: (0,) * w.ndim),
                  row, vec, row],
        out_specs=[row, vec],
        compiler_params=_params(("arbitrary",), VMEM_BIG),
    )(dy, w, h, g, dh_in)


def _gate_bwd(df_pad, z, b_pad, f_col):
    t = z.shape[0]
    tm = ROW_TILE
    nt = t // tm

    def body(d_ref, z_ref, b_ref, dz_ref, db_ref, carry_ref):
        i = pl.program_id(0)

        @pl.when(i == 0)
        def _():
            carry_ref[...] = jnp.zeros_like(carry_ref)
            db_ref[...] = jnp.zeros_like(db_ref)

        tri = (lax.broadcasted_iota(jnp.int32, (tm, tm), 0) <= lax.broadcasted_iota(jnp.int32, (tm, tm), 1))
        tail = jnp.dot(tri.astype(F32), d_ref[...], preferred_element_type=F32, precision=lax.Precision.HIGHEST)
        tail = tail + carry_ref[0:1, :]
        carry_ref[...] = jnp.broadcast_to(tail[0:1, :], carry_ref.shape)
        row = (nt - 1 - i) * tm + lax.broadcasted_iota(jnp.int32, (tm, 1), 0)
        dlogit = jnp.where(row >= ROW_PAD, tail * _sigmoid(-(z_ref[...] + b_ref[...])), 0.0)
        dz_ref[...] = jnp.zeros_like(dz_ref)
        dz_ref[:, 0:128] = dlogit.astype(BF16)
        db_ref[...] += jnp.sum(dlogit, axis=0, keepdims=True)

    rev = lambda i: (nt - 1 - i, 0)
    return pl.pallas_call(
        body, name="forget_gate_bwd",
        out_shape=[jax.ShapeDtypeStruct((t, F_PAD), BF16), jax.ShapeDtypeStruct((1, 128), F32)],
        grid=(nt,),
        in_specs=[pl.BlockSpec((tm, 128), rev), pl.BlockSpec((tm, 128), lambda i: (nt - 1 - i, f_col // 128)),
                  pl.BlockSpec((1, 128), lambda i: (0, 0))],
        out_specs=[pl.BlockSpec((tm, F_PAD), rev), pl.BlockSpec((1, 128), lambda i: (0, 0))],
        scratch_shapes=[pltpu.VMEM((8, 128), F32)],
        compiler_params=_params(("arbitrary",)),
    )(df_pad, z, b_pad)


def _ffn_fwd(tag, n, w_in4, w_out, h, g_post, g_next):
    ab, s, s_t = _ffn_in(f"{tag}_in_fwd", n, w_in4)
    outs = _mm_resid_norm(f"{tag}_out_fwd", s, w_out, h, g_post, 0.5, g_next)
    return ab, s_t, outs


def _ffn_bwd(tag, dh, f, g_post, ab, s_t, n_t, w_in4, w_out, h_in, g_pre):
    d, cw = w_in4.shape[1], w_in4.shape[2]
    t = dh.shape[0]
    df, dg_post = _norm_bwd(f"{tag}_post_norm_bwd", f, g_post, dh, 0.5)
    dw_out = _weight_grad(f"{tag}_dw_out", s_t, df, d, out_rows=cw // 2)
    dab = _ffn_bwd_mid(f"{tag}_mid_bwd", df, w_out, ab)
    dh_in, dg_pre = _mm_nt_norm_bwd(f"{tag}_in_bwd", dab, w_in4, h_in, g_pre, dh)
    bk = _k_tile(t)
    dw_in = _matmul(
        f"{tag}_dw_in", n_t, dab, jax.ShapeDtypeStruct((4, d, cw), F32), (1, 4, t // bk),
        pl.BlockSpec((d, bk), lambda a, b, k: (0, k)), pl.BlockSpec((bk, cw), lambda a, b, k: (k, b)),
        pl.BlockSpec((None, d, cw), lambda a, b, k: (_slot_of(b), 0, 0)), vmem=VMEM_BIG)
    return dh_in, dg_post, dg_pre, dw_in, dw_out


def _pack_small(meta, conv, gains, b_forget):
    d = gains[0].shape[1]
    rows = [meta.reshape(4, d), jnp.pad(conv.reshape(1, 3 * 128), ((0, 0), (0, d - 3 * 128)))]
    rows += list(gains) + [jnp.pad(b_forget, ((0, 0), (0, d - HEADS)))]
    return jnp.concatenate(rows + [jnp.zeros((4, d), F32)], axis=0)


def _unpack_small(block):
    d = block.shape[1]
    meta = block[0:4].reshape(N_META, d // 4)
    conv = block[4, :3 * 128].reshape(1, 3, 128)
    gains = [block[5 + i:6 + i] for i in range(6)]
    return meta, conv, gains, block[11:12, :HEADS]


def kernel(x, meta_tokens, w_in, b_forget, conv_w, w_attn_branch, w_conv_branch, w_out, g_ffn1_pre, g_ffn1_post, w_ffn1_in, w_ffn1_out, g_mix_pre, g_mix_post, g_ffn2_pre, g_ffn2_post, w_ffn2_in, w_ffn2_out, loss_target, m_meta_tokens, m_w_in, m_b_forget, m_conv_w, m_w_attn_branch, m_w_conv_branch, m_w_out, m_g_ffn1_pre, m_g_ffn1_post, m_w_ffn1_in, m_w_ffn1_out, m_g_mix_pre, m_g_mix_post, m_g_ffn2_pre, m_g_ffn2_post, m_w_ffn2_in, m_w_ffn2_out, v_meta_tokens, v_w_in, v_b_forget, v_conv_w, v_w_attn_branch, v_w_conv_branch, v_w_out, v_g_ffn1_pre, v_g_ffn1_post, v_w_ffn1_in, v_w_ffn1_out, v_g_mix_pre, v_g_mix_post, v_g_ffn2_pre, v_g_ffn2_post, v_w_ffn2_in, v_w_ffn2_out):
    seq, d = x.shape[1], x.shape[2]
    t = seq + N_FRONT
    n_main = 3 * ATTN_W + 3 * CONV_W + 2 * d
    nz = n_main + F_PAD
    f_lo = 3 * ATTN_W
    c_arr = lax.axis_index("c").astype(jnp.int32).reshape(1)

    big = [w_in[0], w_attn_branch[0], w_conv_branch[0], w_out[0], w_ffn1_in[0], w_ffn1_out[0], w_ffn2_in[0], w_ffn2_out[0]]
    small_gather = jnp.concatenate(
        [meta_tokens.reshape(4, d), jnp.pad(conv_w.reshape(1, 3 * 128), ((0, 0), (0, d - 3 * 128))),
         jnp.zeros((11, d), F32)], axis=0)
    w_f1_in4, w_f1_out4, small4 = _all_gather([big[4].astype(BF16), big[5].astype(BF16), small_gather])
    w_in4, w_ab4, w_cb4, w_out4, w_f2_in4, w_f2_out4 = _all_gather_async(
        [big[i].astype(BF16) for i in (0, 1, 2, 3, 6, 7)])

    w_in_full = jnp.transpose(w_in4, (1, 0, 2)).reshape(d, 4 * w_in4.shape[2])
    g_lo = f_lo + HEADS + 3 * CONV_W
    w_in_pad = jnp.concatenate(
        [w_in_full[:, :f_lo], w_in_full[:, g_lo:], w_in_full[:, f_lo + HEADS:g_lo], w_in_full[:, f_lo:f_lo + HEADS],
         jnp.zeros((d, F_PAD - HEADS), BF16)], axis=1)
    w_ab = jnp.transpose(w_ab4, (1, 0, 2)).reshape(ATTN_W, d)
    w_cb = jnp.transpose(w_cb4, (1, 0, 2)).reshape(CONV_W, d)
    w_out_full = w_out4.reshape(d, d)
    w_f1_out = w_f1_out4.reshape(-1, d)
    w_f2_out = w_f2_out4.reshape(-1, d)
    meta_full = jnp.transpose(small4[:, 0:4].reshape(4, N_META, d // 4), (1, 0, 2)).reshape(N_META, d)
    conv_full = jnp.transpose(small4[:, 4, :3 * 128].reshape(4, 3, 128), (1, 0, 2)).reshape(3, CONV_W)
    conv_pad = jnp.pad(conv_full, ((0, 5), (0, 0)))
    b_pad = jnp.pad(b_forget, ((0, 0), (0, 128 - HEADS)))

    h0 = jnp.concatenate([jnp.zeros((ROW_PAD, d), F32), meta_full, x[0]], axis=0)
    target_pad = jnp.concatenate([jnp.zeros((N_FRONT, d), F32), loss_target[0]], axis=0)
    n1, n1_t = _norm_fwd("ffn1_pre_norm", h0, g_ffn1_pre)
    ab1, s1_t, (f1, h1, u, u_t) = _ffn_fwd("ffn1", n1, w_f1_in4, w_f1_out, h0, g_ffn1_post, g_mix_pre)
    qkv, z = _in_proj(u, w_in_pad)
    f_col = z.shape[1] - F_PAD
    f_cum = _gate_prep(z, b_pad, f_col)
    f_heads = f_cum[:, :HEADS]
    o, lse = _attn_fwd(qkv, *_attn_bias_operands(f_heads))
    g, g_t = _conv_gate(z, conv_pad)
    mp, mp_t, o_t = _branch_mix(z, o, g, w_ab, w_cb, d)
    mixed, h2, n2, n2_t = _mm_resid_norm("mix_out_fwd", mp, w_out_full, h1, g_mix_post, 1.0, g_ffn2_pre)
    ab2, s2_t, (f2, h3) = _ffn_fwd("ffn2", n2, w_f2_in4, w_f2_out, h2, g_ffn2_post, None)
    dh3, loss_part = _loss_grad(h3, target_pad)
    loss = lax.psum(loss_part[0, 0], ("x", "y", "c"))

    dh2, dg_f2_post, dg_f2_pre, dw_f2_in, dw_f2_out = _ffn_bwd(
        "ffn2", dh3, f2, g_ffn2_post, ab2, s2_t, n2_t, w_f2_in4, w_f2_out, h2, g_ffn2_pre)
    dmixed, dg_mix_post = _norm_bwd("mix_post_norm_bwd", mixed, g_mix_post, dh2, 1.0)
    dw_out = _weight_grad("mix_dw_out", mp_t, dmixed, d)
    dya, dyc, dgates, do, dgconv = _branch_bwd(z, o, g, dmixed, w_out_full, w_ab, w_cb, d)
    dw_ab = _weight_grad("mix_dw_attn_branch", o_t, dya, d)
    dw_cb = _weight_grad("mix_dw_conv_branch", g_t, dyc, d)
    dz_conv, dconv_w = _conv_bwd(z, dgconv, conv_pad)
    front = lax.broadcasted_iota(jnp.int32, (t, 1), 0) < ROW_PAD
    lse_heads = jnp.where(front, 1e9, lse[:, ::HEAD_DIM])
    dq, dk, dv, dfk, dfq = _attn_bwd(qkv, *_attn_bias_operands(f_heads, lse_heads), o, do)
    df_pad = jnp.pad((dfq - dfk)[:, ::HEAD_DIM], ((0, 0), (0, 128 - HEADS)))
    dz_f, db_forget = _gate_bwd(df_pad, z, b_pad, f_col)
    dz = jnp.concatenate([dq.astype(BF16), dk.astype(BF16), dv.astype(BF16), dgates, dz_conv, dz_f], axis=1)
    dh1, dg_mix_pre = _mm_nt_norm_bwd("mix_in_bwd", dz, w_in_pad, h1, g_mix_pre, dh2)
    dw_in_pad = _weight_grad("mix_dw_in", u_t, dz, 512)
    dh0, dg_f1_post, dg_f1_pre, dw_f1_in, dw_f1_out = _ffn_bwd(
        "ffn1", dh1, f1, g_ffn1_post, ab1, s1_t, n1_t, w_f1_in4, w_f1_out, h0, g_ffn1_pre)
    grad_x = dh0[N_FRONT:][None]
    dmeta = dh0[ROW_PAD:N_FRONT]

    cs = w_in4.shape[2]
    c_lo = f_lo + 2 * d
    dw_in_full = jnp.concatenate(
        [dw_in_pad[:, :f_lo], dw_in_pad[:, n_main:n_main + HEADS], dw_in_pad[:, c_lo:n_main], dw_in_pad[:, f_lo:c_lo]],
        axis=1)
    small_grad = jnp.stack([
        _pack_small(dmeta[:, j * (d // 4):(j + 1) * (d // 4)], dconv_w[:3, j * 128:(j + 1) * 128],
                    [dg_f1_pre, dg_f1_post, dg_mix_pre, dg_mix_post, dg_f2_pre, dg_f2_post], db_forget[:, :HEADS])
        for j in range(4)])
    slots = [
        jnp.transpose(dw_in_full.reshape(d, 4, cs), (1, 0, 2)),
        jnp.transpose(dw_ab.reshape(ATTN_W, 4, d // 4), (1, 0, 2)),
        jnp.transpose(dw_cb.reshape(CONV_W, 4, d // 4), (1, 0, 2)),
        dw_out.reshape(4, d // 4, d),
        dw_f1_in, dw_f1_out.reshape(4, -1, d), dw_f2_in, dw_f2_out.reshape(4, -1, d),
        small_grad,
    ]
    tags = ["w_in", "w_attn_branch", "w_conv_branch", "w_out", "w_ffn1_in", "w_ffn1_out", "w_ffn2_in", "w_ffn2_out", "small"]

    got = _pair_send_halves(slots)
    pair_sums = [_pair_add(tag, s, a, c_arr, F32 if tag == "small" else BF16) for tag, s, a in zip(tags, slots, got)]
    arrived = _chip_scatter(pair_sums)
    halves = [_chip_add(tag, a) for tag, a in zip(tags, arrived)]
    others = _pair_swap(halves)

    small = [g_ffn1_pre, g_ffn1_post, g_mix_pre, g_mix_post, g_ffn2_pre, g_ffn2_post]
    small_m = [m_g_ffn1_pre, m_g_ffn1_post, m_g_mix_pre, m_g_mix_post, m_g_ffn2_pre, m_g_ffn2_post]
    small_v = [v_g_ffn1_pre, v_g_ffn1_post, v_g_mix_pre, v_g_mix_post, v_g_ffn2_pre, v_g_ffn2_post]
    ws = big + [_pack_small(meta_tokens, conv_w[0], small, b_forget)]
    ms = [m_w_in[0], m_w_attn_branch[0], m_w_conv_branch[0], m_w_out[0], m_w_ffn1_in[0], m_w_ffn1_out[0],
          m_w_ffn2_in[0], m_w_ffn2_out[0], _pack_small(m_meta_tokens, m_conv_w[0], small_m, m_b_forget)]
    vs = [v_w_in[0], v_w_attn_branch[0], v_w_conv_branch[0], v_w_out[0], v_w_ffn1_in[0], v_w_ffn1_out[0],
          v_w_ffn2_in[0], v_w_ffn2_out[0], _pack_small(v_meta_tokens, v_conv_w[0], small_v, v_b_forget)]
    updates = [_adamw(tag, w, a, b, m, v, c_arr) for tag, w, a, b, m, v in zip(tags, ws, halves, others, ms, vs)]

    def leaves(big_vals, small_block):
        meta, conv, gains, bf = _unpack_small(small_block)
        w_in_, w_ab_, w_cb_, w_out_, f1_in, f1_out, f2_in, f2_out = [b[None] for b in big_vals]
        return [meta, w_in_, bf, conv, w_ab_, w_cb_, w_out_, gains[0], gains[1], f1_in, f1_out,
                gains[2], gains[3], gains[4], gains[5], f2_in, f2_out]

    out_g, out_d, out_m, out_v = [leaves([u_[k] for u_ in updates[:8]], updates[8][k]) for k in range(4)]
    return (loss, grad_x, *out_g, *out_d, *out_m, *out_v)
```

```python
import functools

import jax
import jax.numpy as jnp
from jax import lax
from jax.experimental import pallas as pl
from jax.experimental.pallas import tpu as pltpu
from jax.experimental.pallas import tpu_sc as plsc

N_META = 16
ROW_PAD = 112
N_FRONT = ROW_PAD + N_META
HEADS = 8
HEAD_DIM = 64
ATTN_W = HEADS * HEAD_DIM
CONV_W = 512
NORM_EPS = 1e-6
ROW_TILE = 640
F_PAD = 512
ATTN_ROW_PARTS = 1
NEG = -1e30
ADAM_LR = 0.001
ADAM_B1 = 0.9
ADAM_B2 = 0.999
ADAM_EPS = 1e-08
ADAM_WD = 0.01
ADAM_STEP = 10
VMEM_BIG = 56 * 1024 * 1024
MESH = pl.DeviceIdType.MESH
ANY = pl.BlockSpec(memory_space=pl.ANY)
F32 = jnp.float32
BF16 = jnp.bfloat16


def _params(sem, vmem=None):
    return pltpu.CompilerParams(dimension_semantics=sem, vmem_limit_bytes=vmem)


def _sigmoid(x):
    return 1.0 / (1.0 + jnp.exp(-x))


def _rstd(x):
    return lax.rsqrt(jnp.mean(x * x, axis=-1, keepdims=True) + NORM_EPS)


def _rms_bwd(x, g, dy):
    r = _rstd(x)
    xr = x * r
    gdy = g * dy
    dx = r * (gdy - xr * jnp.mean(xr * gdy, axis=-1, keepdims=True))
    return dx, jnp.sum(dy * xr, axis=0, keepdims=True)


def _dot(a, b):
    return jnp.dot(a, b, preferred_element_type=F32)


def _dot_nt(a, b):
    return lax.dot_general(a, b, (((1,), (1,)), ((), ())), preferred_element_type=F32)


def _k_tile(t):
    return 1664 if t % 1664 == 0 else ROW_TILE


def _place():
    x, y, c = lax.axis_index("x"), lax.axis_index("y"), lax.axis_index("c")
    chips = [(1 - x, y), (x, 1 - y), (1 - x, 1 - y)]
    return x, y, c, chips


def _all_gather(shards):
    n = len(shards)
    split = [s.reshape(2, s.shape[0] // 2, s.shape[1]) for s in shards]

    def body(*refs):
        ins, outs = refs[:n], refs[n:2 * n]
        send_sems, recv_sems = refs[2 * n:]
        x, y, c, chips = _place()
        me = 2 * x + y
        sibling = (x, y, 1 - c)

        def remote(i, k, slot, part, to, src=None):
            dst = outs[i].at[slot, part]
            return pltpu.make_async_remote_copy(
                src_ref=dst if src is None else src, dst_ref=dst,
                send_sem=send_sems.at[i, k], recv_sem=recv_sems.at[i, k],
                device_id=to, device_id_type=MESH)

        started = []
        for i in range(n):
            for k, (cx, cy) in enumerate(chips):
                cp = remote(i, k, me, c, (cx, cy, c), src=ins[i].at[c])
                cp.start()
                started.append(cp)
        for i in range(n):
            for k, (cx, cy) in enumerate(chips):
                remote(i, k, 2 * cx + cy, c, (x, y, c)).wait_recv()
                cp = remote(i, 3 + k, 2 * cx + cy, c, sibling)
                cp.start()
                started.append(cp)
        for i in range(n):
            for k, (cx, cy) in enumerate(chips):
                remote(i, 3 + k, 2 * cx + cy, 1 - c, (x, y, c)).wait_recv()
        for cp in started:
            cp.wait_send()

    outs = pl.pallas_call(
        body, name="all_gather_weights",
        out_shape=[jax.ShapeDtypeStruct((4,) + s.shape, s.dtype) for s in split],
        in_specs=[ANY] * n, out_specs=[ANY] * n,
        scratch_shapes=[pltpu.SemaphoreType.DMA((n, 6)), pltpu.SemaphoreType.DMA((n, 6))],
    )(*split)
    me =2 * lax.axis_index("x") + lax.axis_index("y")
    outs = [lax.dynamic_update_slice(o, s[None], (me, 0, 0, 0)) for o, s in zip(outs, split)]
    return [o.reshape((4,) + s.shape) for o, s in zip(outs, shards)]


def _all_gather_async(shards):
    n = len(shards)
    split = [s.reshape(2, s.shape[0] // 2, s.shape[1]) for s in shards]
    ins = [jax.new_ref(s, memory_space=pltpu.MemorySpace.HBM) for s in split]
    outs = [jax.empty_ref(jax.ShapeDtypeStruct((4,) + s.shape, s.dtype), memory_space=pltpu.MemorySpace.HBM)
            for s in split]

    @pl.kernel(mesh=plsc.ScalarSubcoreMesh(axis_name="sequencer", num_cores=1), name="all_gather_rest",
               scratch_types=(pltpu.SemaphoreType.DMA((n, 6)), pltpu.SemaphoreType.DMA((n, 6))),
               compiler_params=pltpu.CompilerParams(collective_id=1))
    def launch(send_sems, recv_sems):
        x, y, c, chips = _place()
        me = 2 * x + y
        sibling = (x, y, 1 - c)
        barrier = pltpu.get_barrier_semaphore()
        for peer in [(cx, cy, c) for cx, cy in chips] + [sibling]:
            pl.semaphore_signal(barrier, inc=1, device_id=peer, device_id_type=MESH)
        pl.semaphore_wait(barrier, 4)

        def remote(i, k, slot, part, to, src=None):
            dst = outs[i].at[slot, part]
            return pltpu.make_async_remote_copy(
                src_ref=dst if src is None else src, dst_ref=dst,
                send_sem=send_sems.at[i, k], recv_sem=recv_sems.at[i, k],
                device_id=to, device_id_type=MESH)

        started = []
        for i in range(n):
            for k, (cx, cy) in enumerate(chips):
                cp = remote(i, k, me, c, (cx, cy, c), src=ins[i].at[c])
                cp.start()
                started.append(cp)
        for i in range(n):
            for k, (cx, cy) in enumerate(chips):
                remote(i, k, 2 * cx + cy, c, (x, y, c)).wait_recv()
                cp = remote(i, 3 + k, 2 * cx + cy, c, sibling)
                cp.start()
                started.append(cp)
        for i in range(n):
            for k, (cx, cy) in enumerate(chips):
                remote(i, 3 + k, 2 * cx + cy, 1 - c, (x, y, c)).wait_recv()
        for cp in started:
            cp.wait_send()

    launch()
    me = 2 * lax.axis_index("x") + lax.axis_index("y")
    gathered = [lax.dynamic_update_slice(o[...], s[None], (me, 0, 0, 0)) for o, s in zip(outs, split)]
    return [g.reshape((4,) + s.shape) for g, s in zip(gathered, shards)]


def _pair_send_halves(name, grads):
    n = len(grads)

    def body(*refs):
        ins, outs = refs[:n], refs[n:2 * n]
        send_sems, recv_sems = refs[2 * n:]
        x, y, c, _ = _place()
        cps = []
        for i in range(n):
            half = ins[i].shape[1] // 2
            cp = pltpu.make_async_remote_copy(
                src_ref=ins[i].at[:, pl.ds((1 - c) * half, half)], dst_ref=outs[i],
                send_sem=send_sems.at[i], recv_sem=recv_sems.at[i],
                device_id=(x, y, 1 - c), device_id_type=MESH)
            cp.start()
            cps.append(cp)
        for cp in cps:
            cp.wait()

    return pl.pallas_call(
        body, name=name,
        out_shape=[jax.ShapeDtypeStruct((4, g.shape[1] // 2, g.shape[2]), g.dtype) for g in grads],
        in_specs=[ANY] * n, out_specs=[ANY] * n,
        scratch_shapes=[pltpu.SemaphoreType.DMA((n,)), pltpu.SemaphoreType.DMA((n,))],
    )(*grads)


def _chip_scatter(name, parts):
    n = len(parts)

    def body(*refs):
        _scatter_copies(refs[:n], refs[n:2 * n], *refs[2 * n:])

    arrived = pl.pallas_call(
        body, name=name,
        out_shape=[jax.ShapeDtypeStruct(p.shape, p.dtype) for p in parts],
        in_specs=[ANY] * n, out_specs=[ANY] * n,
        scratch_shapes=[pltpu.SemaphoreType.DMA((n, 3)), pltpu.SemaphoreType.DMA((n, 3))],
    )(*parts)
    return _own_slots(parts, arrived)


def _scatter_copies(ins, outs, send_sems, recv_sems):
    x, y, c, chips = _place()
    me = 2 * x + y
    sends = []
    for i in range(len(ins)):
        for k, (cx, cy) in enumerate(chips):
            cp = pltpu.make_async_remote_copy(
                src_ref=ins[i].at[2 * cx + cy], dst_ref=outs[i].at[me],
                send_sem=send_sems.at[i, k], recv_sem=recv_sems.at[i, k],
                device_id=(cx, cy, c), device_id_type=MESH)
            cp.start()
            sends.append(cp)
    for i in range(len(ins)):
        for k, (cx, cy) in enumerate(chips):
            got = outs[i].at[2 * cx + cy]
            pltpu.make_async_remote_copy(
                src_ref=got, dst_ref=got, send_sem=send_sems.at[i, k], recv_sem=recv_sems.at[i, k],
                device_id=(x, y, c), device_id_type=MESH).wait_recv()
    for cp in sends:
        cp.wait_send()


def _own_slots(parts, arrived):
    me = 2 * lax.axis_index("x") + lax.axis_index("y")
    return [lax.dynamic_update_slice(a, lax.dynamic_slice_in_dim(p, me, 1, axis=0), (me, 0, 0))
            for p, a in zip(parts, arrived)]


def _chip_scatter_async(name, parts, collective_id):
    n = len(parts)
    ins = [jax.new_ref(p, memory_space=pltpu.MemorySpace.HBM) for p in parts]
    outs = [jax.empty_ref(jax.ShapeDtypeStruct(p.shape, p.dtype), memory_space=pltpu.MemorySpace.HBM) for p in parts]

    @pl.kernel(mesh=plsc.ScalarSubcoreMesh(axis_name="sequencer", num_cores=1), name=name,
               scratch_types=(pltpu.SemaphoreType.DMA((n, 3)), pltpu.SemaphoreType.DMA((n, 3))),
               compiler_params=pltpu.CompilerParams(collective_id=collective_id))
    def launch(send_sems, recv_sems):
        x, y, c, chips = _place()
        barrier = pltpu.get_barrier_semaphore()
        for cx, cy in chips:
            pl.semaphore_signal(barrier, inc=1, device_id=(cx, cy, c), device_id_type=MESH)
        pl.semaphore_wait(barrier, 3)
        _scatter_copies(ins, outs, send_sems, recv_sems)

    launch()
    return _own_slots(parts, [o[...] for o in outs])


def _pair_swap(name, halves):
    n = len(halves)

    def body(*refs):
        ins, outs = refs[:n], refs[n:2 * n]
        send_sems, recv_sems = refs[2 * n:]
        x, y, c, _ = _place()
        cps = []
        for i in range(n):
            cp = pltpu.make_async_remote_copy(
                src_ref=ins[i], dst_ref=outs[i], send_sem=send_sems.at[i], recv_sem=recv_sems.at[i],
                device_id=(x, y, 1 - c), device_id_type=MESH)
            cp.start()
            cps.append(cp)
        for cp in cps:
            cp.wait()

    return pl.pallas_call(
        body, name=name,
        out_shape=[jax.ShapeDtypeStruct(h.shape, h.dtype) for h in halves],
        in_specs=[ANY] * n, out_specs=[ANY] * n,
        scratch_shapes=[pltpu.SemaphoreType.DMA((n,)), pltpu.SemaphoreType.DMA((n,))],
    )(*halves)


def _row_block(rows, cols, n_bufs, budget=20 * 1024 * 1024):
    best = min(rows, 16)
    for b in range(16, rows + 1, 16):
        if rows % b == 0 and 2 * n_bufs * b * cols * 4 <= budget:
            best = b
    return best


def _pair_add(tag, grad, got, c_arr, out_dtype):
    _, rows, cols = grad.shape
    half = rows // 2
    bh = _row_block(half, cols, 3)
    nb = half // bh

    def body(c_ref, g_ref, a_ref, o_ref):
        o_ref[...] = (g_ref[...] + a_ref[...]).astype(out_dtype)

    return pl.pallas_call(
        body, name=f"pair_add_{tag}",
        out_shape=jax.ShapeDtypeStruct((4, half, cols), out_dtype),
        grid_spec=pltpu.PrefetchScalarGridSpec(
            num_scalar_prefetch=1, grid=(4, nb),
            in_specs=[pl.BlockSpec((None, bh, cols), lambda j, r, c: (j, c[0] * nb + r, 0)),
                      pl.BlockSpec((None, bh, cols), lambda j, r, c: (j, r, 0))],
            out_specs=pl.BlockSpec((None, bh, cols), lambda j, r, c: (j, r, 0))),
        compiler_params=_params(("parallel", "parallel")),
    )(c_arr, grad, got)


def _chip_add(tag, parts):
    _, half, cols = parts.shape
    bh = _row_block(half, cols, 5)

    def body(p_ref, o_ref):
        a, b, c, d = [p_ref[j].astype(F32) for j in range(4)]
        o_ref[...] = ((a + b) + c) + d

    return pl.pallas_call(
        body, name=f"chip_add_{tag}",
        out_shape=jax.ShapeDtypeStruct((half, cols), F32),
        grid=(half // bh,),
        in_specs=[pl.BlockSpec((4, bh, cols), lambda r: (0, r, 0))],
        out_specs=pl.BlockSpec((bh, cols), lambda r: (r, 0)),
        compiler_params=_params(("parallel",)),
    )(parts)


def _adamw(tag, w, mine, theirs, m, v, c_arr):
    rows, cols = w.shape
    half = rows // 2
    br = _row_block(half, cols, 9)
    nb = half // br

    def body(c_ref, w_ref, a_ref, b_ref, m_ref, v_ref, g_ref, d_ref, mo_ref, vo_ref):
        own = (pl.program_id(0) // nb) == c_ref[0]
        g = jnp.where(own, a_ref[...], b_ref[...])
        g_ref[...] = g
        m_new = ADAM_B1 * m_ref[...] + (1.0 - ADAM_B1) * g
        v_new = ADAM_B2 * v_ref[...] + (1.0 - ADAM_B2) * (g * g)
        m_hat = m_new / (1.0 - ADAM_B1 ** ADAM_STEP)
        v_hat = v_new / (1.0 - ADAM_B2 ** ADAM_STEP)
        d_ref[...] = -ADAM_LR * (m_hat / (jnp.sqrt(v_hat) + ADAM_EPS) + ADAM_WD * w_ref[...])
        mo_ref[...] = m_new
        vo_ref[...] = v_new

    spec = pl.BlockSpec((br, cols), lambda r, c: (r, 0))
    mine_spec = pl.BlockSpec((br, cols), lambda r, c: (jnp.clip(r - c[0] * nb, 0, nb - 1), 0))
    theirs_spec = pl.BlockSpec((br, cols), lambda r, c: (jnp.clip(r - (1 - c[0]) * nb, 0, nb - 1), 0))
    return pl.pallas_call(
        body, name=f"adamw_{tag}",
        out_shape=[jax.ShapeDtypeStruct((rows, cols), F32)] * 4,
        grid_spec=pltpu.PrefetchScalarGridSpec(
            num_scalar_prefetch=1, grid=(rows // br,),
            in_specs=[spec, mine_spec, theirs_spec, spec, spec], out_specs=[spec] * 4),
        compiler_params=_params(("arbitrary",)),
    )(c_arr, w, mine, theirs, m, v)


def _matmul(name, x, w, out_shape, grid, x_spec, w_spec, o_spec, *, nt=False, vmem=None):
    nk = grid[2]
    acc_shape = tuple(d for d in o_spec.block_shape if d is not None)

    def body(x_ref, w_ref, o_ref, acc_ref):
        k = pl.program_id(2)
        part = _dot_nt(x_ref[...], w_ref[...]) if nt else _dot(x_ref[...], w_ref[...])
        if nk == 1:
            o_ref[...] = part.astype(o_ref.dtype)
        else:
            @pl.when(k == 0)
            def _():
                acc_ref[...] = part

            @pl.when(k > 0)
            def _():
                acc_ref[...] += part

            @pl.when(k == nk - 1)
            def _():
                o_ref[...] = acc_ref[...].astype(o_ref.dtype)

    return pl.pallas_call(
        body, name=name, out_shape=out_shape, grid=grid,
        in_specs=[x_spec, w_spec], out_specs=o_spec,
        scratch_shapes=[pltpu.VMEM(acc_shape if nk > 1 else (8, 128), F32)],
        compiler_params=_params(("parallel", "parallel", "arbitrary"), vmem),
    )(x, w)


def _weight_grad(name, xt, dy, bn, out_rows=None):
    m, t = xt.shape
    n = dy.shape[1]
    bm = m if out_rows is None else out_rows
    bk = _k_tile(t)
    return _matmul(
        name, xt, dy, jax.ShapeDtypeStruct((m, n), F32), (m // bm, n // bn, t // bk),
        pl.BlockSpec((bm, bk), lambda a, b, k: (a, k)),
        pl.BlockSpec((bk, bn), lambda a, b, k: (k, b)),
        pl.BlockSpec((bm, bn), lambda a, b, k: (a, b)), vmem=VMEM_BIG)


def _norm_fwd(name, h, g):
    t, d = h.shape
    tm = ROW_TILE

    def body(h_ref, g_ref, n_ref, nt_ref):
        x = h_ref[...]
        y = x * _rstd(x) * g_ref[...]
        n_ref[...] = y.astype(BF16)
        nt_ref[...] = y.T.astype(BF16)

    return pl.pallas_call(
        body, name=name,
        out_shape=[jax.ShapeDtypeStruct((t, d), BF16), jax.ShapeDtypeStruct((d, t), BF16)],
        grid=(t // tm,),
        in_specs=[pl.BlockSpec((tm, d), lambda i: (i, 0)), pl.BlockSpec((1, d), lambda i: (0, 0))],
        out_specs=[pl.BlockSpec((tm, d), lambda i: (i, 0)), pl.BlockSpec((d, tm), lambda i: (0, i))],
        compiler_params=_params(("parallel",)),
    )(h, g)


def _slot_of(kk):
    return (kk % 2) * 2 + kk // 2


def _ffn_in(name, n, w4):
    t, d = n.shape
    cw = w4.shape[2]
    tm = ROW_TILE

    def body(x_ref, wg_ref, wu_ref, ab_ref, s_ref, st_ref):
        x = x_ref[...]
        a = _dot(x, wg_ref[...])
        b = _dot(x, wu_ref[...])
        ab_ref[:, :cw] = a
        ab_ref[:, cw:] = b
        s = a * _sigmoid(a) * b
        s_ref[...] = s.astype(BF16)
        st_ref[...] = s.T.astype(BF16)

    return pl.pallas_call(
        body, name=name,
        out_shape=[jax.ShapeDtypeStruct((t, 4 * cw), F32), jax.ShapeDtypeStruct((t, 2 * cw), BF16),
                   jax.ShapeDtypeStruct((2 * cw, t), BF16)],
        grid=(2, t // tm),
        in_specs=[pl.BlockSpec((tm, d), lambda j, i: (i, 0)),
                  pl.BlockSpec((None, d, cw), lambda j, i: (j, 0, 0)),
                  pl.BlockSpec((None, d, cw), lambda j, i: (2 + j, 0, 0))],
        out_specs=[pl.BlockSpec((tm, 2 * cw), lambda j, i: (i, j)),
                   pl.BlockSpec((tm, cw), lambda j, i: (i, j)),
                   pl.BlockSpec((cw, tm), lambda j, i: (j, i))],
        compiler_params=_params(("parallel", "parallel"), VMEM_BIG),
    )(n, w4, w4)


def _mm_resid_norm(name, x, w, h, g_post, alpha, g_next):
    t, kdim = x.shape
    d = w.shape[1]
    tm = ROW_TILE
    with_next = g_next is not None

    def body(x_ref, w_ref, h_ref, gp_ref, gn_ref, f_ref, hn_ref, *rest):
        f = _dot(x_ref[...], w_ref[...])
        f_ref[...] = f
        hn = h_ref[...] + alpha * (f * _rstd(f) * gp_ref[...])
        hn_ref[...] = hn
        if with_next:
            y = hn * _rstd(hn) * gn_ref[...]
            rest[0][...] = y.astype(BF16)
            rest[1][...] = y.T.astype(BF16)

    row = lambda i: (i, 0)
    vec = pl.BlockSpec((1, d), lambda i: (0, 0))
    out_shape = [jax.ShapeDtypeStruct((t, d), F32), jax.ShapeDtypeStruct((t, d), F32)]
    out_specs = [pl.BlockSpec((tm, d), row), pl.BlockSpec((tm, d), row)]
    if with_next:
        out_shape += [jax.ShapeDtypeStruct((t, d), BF16), jax.ShapeDtypeStruct((d, t), BF16)]
        out_specs += [pl.BlockSpec((tm, d), row), pl.BlockSpec((d, tm), lambda i: (0, i))]
    return pl.pallas_call(
        body, name=name, out_shape=out_shape, grid=(t // tm,),
        in_specs=[pl.BlockSpec((tm, kdim), row), pl.BlockSpec((kdim, d), lambda i: (0, 0)),
                  pl.BlockSpec((tm, d), row), vec, vec],
        out_specs=out_specs,
        compiler_params=_params(("parallel",), VMEM_BIG),
    )(x, w, h, g_post, g_post if g_next is None else g_next)


def _in_proj(u, w):
    t, d = u.shape
    nz = w.shape[1]
    nq = 3 * ATTN_W
    tm = ROW_TILE // 2

    def body(u_ref, w_ref, qkv_ref, z_ref):
        qkv_ref[...] = _dot(u_ref[...], w_ref[:, 0:nq]).astype(BF16)
        z_ref[...] = _dot(u_ref[...], w_ref[:, nq:])

    return pl.pallas_call(
        body, name="mix_in_proj",
        out_shape=[jax.ShapeDtypeStruct((t, nq), BF16), jax.ShapeDtypeStruct((t, nz - nq), F32)],
        grid=(t // tm,),
        in_specs=[pl.BlockSpec((tm, d), lambda i: (i, 0)), pl.BlockSpec((d, nz), lambda i: (0, 0))],
        out_specs=[pl.BlockSpec((tm, nq), lambda i: (i, 0)), pl.BlockSpec((tm, nz - nq), lambda i: (i, 0))],
        compiler_params=_params(("parallel",), VMEM_BIG),
    )(u, w)


def _gate_prep(z, b_pad, f_col):
    t = z.shape[0]
    tm = ROW_TILE

    def body(z_ref, b_ref, f_ref, carry_ref):
        i = pl.program_id(0)

        @pl.when(i == 0)
        def _():
            carry_ref[...] = jnp.zeros_like(carry_ref)

        xs = z_ref[...] + b_ref[...]
        logf = jnp.minimum(xs, 0.0) - jnp.log(1.0 + jnp.exp(-jnp.abs(xs)))
        row = i * tm + lax.broadcasted_iota(jnp.int32, (tm, 1), 0)
        logf = jnp.where(row >= ROW_PAD, logf, 0.0)
        tri = (lax.broadcasted_iota(jnp.int32, (tm, tm), 0) >= lax.broadcasted_iota(jnp.int32, (tm, tm), 1))
        f = jnp.dot(tri.astype(F32), logf, preferred_element_type=F32, precision=lax.Precision.HIGHEST)
        f = f + carry_ref[0:1, :]
        f_ref[...] = f
        carry_ref[...] = jnp.broadcast_to(f[tm - 1:tm, :], carry_ref.shape)

    return pl.pallas_call(
        body, name="forget_gate_cumsum", out_shape=jax.ShapeDtypeStruct((t, 128), F32),
        grid=(t // tm,),
        in_specs=[pl.BlockSpec((tm, 128), lambda i: (i, f_col // 128)), pl.BlockSpec((1, 128), lambda i: (0, 0))],
        out_specs=pl.BlockSpec((tm, 128), lambda i: (i, 0)),
        scratch_shapes=[pltpu.VMEM((8, 128), F32)],
        compiler_params=_params(("arbitrary",)),
    )(z, b_pad)


def _lane_halves():
    lane = lax.broadcasted_iota(jnp.int32, (1, 128), 1)
    return lane < HEAD_DIM


def _causal_mask(tq, tk, row0=0):
    row = row0 + lax.broadcasted_iota(jnp.int32, (tq, 1), 0)
    col = lax.broadcasted_iota(jnp.int32, (1, tk), 1)
    return col <= row


def _lane_one(lane):
    return (lax.broadcasted_iota(jnp.int32, (1, 128), 1) == lane).astype(BF16)


def _split3(x):
    hi = x.astype(BF16)
    rest = x - hi.astype(F32)
    mid = rest.astype(BF16)
    return hi, mid, (rest - mid.astype(F32)).astype(BF16)


def _split3_glue(x):
    hi = lax.reduce_precision(x, 8, 7)
    mid = lax.reduce_precision(x - hi, 8, 7)
    lo = lax.reduce_precision((x - hi) - mid, 8, 7)
    return hi.astype(BF16), mid.astype(BF16), lo.astype(BF16)


def _aug_pairs(cols):
    t = cols[0].shape[0]
    a = jnp.pad(jnp.stack(cols, axis=2), ((0, 0), (0, 0), (0, HEAD_DIM - len(cols))))
    a = a.reshape(t, 4, 2, HEAD_DIM)[:, :, ::-1, :]
    return jnp.transpose(a.reshape(t, 4, 128), (1, 0, 2))


def _attn_bias_operands(f_heads, lse_heads=None):
    t = f_heads.shape[0]
    one = jnp.ones((t, HEADS), BF16)
    row = lax.broadcasted_iota(jnp.int32, (t, 1), 0)
    fq = _split3_glue(f_heads)
    fk = _split3_glue(jnp.where(row < ROW_PAD, 1e9, f_heads))
    q_cols = list(fq) + [one] * 3
    k_cols = [one] * 3 + [-c for c in fk]
    if lse_heads is not None:
        q_cols += [-c for c in _split3_glue(lse_heads)]
        k_cols += [one] * 3
    return _aug_pairs(q_cols), _aug_pairs(k_cols)


def _attn_steps(nq, by_key):
    if by_key:
        pairs = [(qi, ki) for ki in range(nq) for qi in range(ki, nq)]
    else:
        pairs = [(qi, ki) for qi in range(nq) for ki in range(qi + 1)]
    return (jnp.array([p[0] for p in pairs], jnp.int32), jnp.array([p[1] for p in pairs], jnp.int32))


def _attn_fwd(z, aug_q, aug_k):
    t = z.shape[0]
    tq = tk = ROW_TILE
    nq = t // tq
    q_tab, k_tab = _attn_steps(nq, by_key=False)

    def body(qt_ref, kt_ref, q_ref, k_ref, v_ref, aq_ref, ak_ref, o_ref, lse_ref, m_ref, l_ref, acc_ref):
        step = pl.program_id(1)
        qi, ki = qt_ref[step], kt_ref[step]

        @pl.when(ki == 0)
        def _():
            m_ref[...] = jnp.full_like(m_ref, NEG)
            l_ref[...] = jnp.zeros_like(l_ref)
            acc_ref[...] = jnp.zeros_like(acc_ref)

        def sweep(diagonal):
            first = _lane_halves()
            q = (q_ref[...] * (HEAD_DIM ** -0.5)).astype(BF16)
            k = k_ref[...].astype(BF16)
            v = v_ref[...].astype(BF16)
            aq, ak = aq_ref[...], ak_ref[...]
            halves = (first, jnp.logical_not(first))
            qa = [jnp.where(lanes, q, aq) for lanes in halves]
            ka = [jnp.where(lanes, k, ak) for lanes in halves]
            va = [jnp.where(lanes, v, _lane_one(a0)) for lanes, a0 in zip(halves, (HEAD_DIM, 0))]
            chains = [(hh, r) for r in range(ATTN_ROW_PARTS) for hh in range(2)]
            rp = tq // ATTN_ROW_PARTS
            rows = [slice(r * rp, (r + 1) * rp) for _, r in chains]
            s = [_dot_nt(qa[hh][rw], ka[hh]) for (hh, _), rw in zip(chains, rows)]
            if diagonal:
                s = [jnp.where(_causal_mask(rp, tk, rw.start), s_c, NEG) for s_c, rw in zip(s, rows)]
            m_prev = [m_ref[rw, hh * HEAD_DIM:hh * HEAD_DIM + 1] for (hh, _), rw in zip(chains, rows)]
            m_new = [jnp.maximum(mp, jnp.max(s_c, axis=1, keepdims=True)) for mp, s_c in zip(m_prev, s)]
            p = [jnp.exp(s_c - m_c).astype(BF16) for s_c, m_c in zip(s, m_new)]
            pv = [_dot(p_c, va[hh]) for p_c, (hh, _) in zip(p, chains)]
            alpha = [jnp.exp(mp - m_c) for mp, m_c in zip(m_prev, m_new)]
            for r in range(ATTN_ROW_PARTS):
                (m0, m1), (al0, al1), (pv0, pv1) = [x[2 * r:2 * r + 2] for x in (m_new, alpha, pv)]
                rw = rows[2 * r]
                l0 = al0 * l_ref[rw, 0:1] + pv0[:, HEAD_DIM:HEAD_DIM + 1]
                l1 = al1 * l_ref[rw, HEAD_DIM:HEAD_DIM + 1] + pv1[:, 0:1]
                acc_ref[rw, :] = acc_ref[rw, :] * jnp.where(first, al0, al1) + jnp.where(first, pv0, pv1)
                m_ref[rw, :] = jnp.where(first, m0, m1)
                l_ref[rw, :] = jnp.where(first, l0, l1)

        @pl.when(ki < qi)
        def _():
            sweep(False)

        @pl.when(ki == qi)
        def _():
            sweep(True)
            o_ref[...] = acc_ref[...] / l_ref[...]
            lse_ref[...] = m_ref[...] + jnp.log(l_ref[...])

    return pl.pallas_call(
        body, name="attention_fwd",
        out_shape=[jax.ShapeDtypeStruct((t, ATTN_W), F32), jax.ShapeDtypeStruct((t, ATTN_W), F32)],
        grid_spec=pltpu.PrefetchScalarGridSpec(
            num_scalar_prefetch=2, grid=(4, int(q_tab.shape[0])),
            in_specs=[pl.BlockSpec((tq, 128), lambda p, s, qt, kt: (qt[s], p)),
                      pl.BlockSpec((tk, 128), lambda p, s, qt, kt: (kt[s], 4 + p)),
                      pl.BlockSpec((tk, 128), lambda p, s, qt, kt: (kt[s], 8 + p)),
                      pl.BlockSpec((None, tq, 128), lambda p, s, qt, kt: (p, qt[s], 0)),
                      pl.BlockSpec((None, tk, 128), lambda p, s, qt, kt: (p, kt[s], 0))],
            out_specs=[pl.BlockSpec((tq, 128), lambda p, s, qt, kt: (qt[s], p)),
                       pl.BlockSpec((tq, 128), lambda p, s, qt, kt: (qt[s], p))],
            scratch_shapes=[pltpu.VMEM((tq, 128), F32)] * 3),
        compiler_params=_params(("parallel", "arbitrary")),
    )(q_tab, k_tab, z, z, z, aug_q, aug_k)


def _attn_bwd(z, aug_q, aug_k, o, do):
    t = z.shape[0]
    tq = tk = ROW_TILE
    nq = t // tq
    q_tab, k_tab = _attn_steps(nq, by_key=True)
    tn = (((0,), (0,)), ((), ()))

    def body(qt_ref, kt_ref, q_ref, k_ref, v_ref, aq_ref, ak_ref, o_ref, do_ref,
             dq_ref, dk_ref, dv_ref, dfk_ref, dfq_ref):
        step = pl.program_id(1)
        qi, ki = qt_ref[step], kt_ref[step]
        rows = pl.ds(pl.multiple_of(qi * tq, tq), tq)

        @pl.when(ki == 0)
        def _():
            dq_ref[rows, :] = jnp.zeros((tq, 128), F32)
            dfq_ref[rows, :] = jnp.zeros((tq, 128), F32)

        @pl.when(qi == ki)
        def _():
            dk_ref[...] = jnp.zeros_like(dk_ref)
            dv_ref[...] = jnp.zeros_like(dv_ref)
            dfk_ref[...] = jnp.zeros_like(dfk_ref)

        def sweep(diagonal):
            first = _lane_halves()
            lane = lax.broadcasted_iota(jnp.int32, (1, 128), 1)
            scale = HEAD_DIM ** -0.5
            q = (q_ref[...] * scale).astype(BF16)
            k = k_ref[...].astype(BF16)
            v = v_ref[...].astype(BF16)
            do_ = do_ref[...]
            do16 = do_.astype(BF16)
            od = o_ref[...] * do_
            aq, ak = aq_ref[...], ak_ref[...]
            halves = (first, jnp.logical_not(first))
            a0, a1 = HEAD_DIM, 0
            dos, vs = [], []
            for lanes, a in zip(halves, (a0, a1)):
                d_hi, d_mid, d_lo = _split3(jnp.sum(jnp.where(lanes, od, 0.0), axis=1, keepdims=True))
                minus_delta = jnp.where(lane == a, -d_hi, jnp.where(lane == a + 1, -d_mid,
                                        jnp.where(lane == a + 2, -d_lo, jnp.zeros((), BF16))))
                dos.append(jnp.where(lanes, do16, minus_delta))
                vs.append(jnp.where(lanes, v, ((lane >= a) & (lane < a + 3)).astype(BF16)))
            s = [_dot_nt(jnp.where(lanes, q, aq), jnp.where(lanes, k, ak)) for lanes in halves]
            dp = [_dot_nt(do_h, v_h) for do_h, v_h in zip(dos, vs)]
            p = [jnp.exp(s_h) for s_h in s]
            if diagonal:
                p = [jnp.where(_causal_mask(tq, tk), p_h, 0.0) for p_h in p]
            ds16 = [(p_h * dp_h).astype(BF16) for p_h, dp_h in zip(p, dp)]
            dv0, dv1 = [lax.dot_general(p_h.astype(BF16), jnp.where(lanes, do16, jnp.zeros((), BF16)), tn,
                                        preferred_element_type=F32) for p_h, lanes in zip(p, halves)]
            dk0, dk1 = [lax.dot_general(ds_h, jnp.where(lanes, q, _lane_one(a)), tn, preferred_element_type=F32)
                        for ds_h, lanes, a in zip(ds16, halves, (a0, a1))]
            dq0, dq1 = [_dot(ds_h, jnp.where(lanes, k, _lane_one(a))) for ds_h, lanes, a in zip(ds16, halves, (a0, a1))]
            dq_ref[rows, :] += jnp.where(first, dq0, dq1) * scale
            dfq_ref[rows, :] += jnp.where(first, dq0[:, a0:a0 + 1], dq1[:, a1:a1 + 1])
            dk_ref[...] += jnp.where(first, dk0, dk1)
            dfk_ref[...] += jnp.where(first, dk0[:, a0:a0 + 1], dk1[:, a1:a1 + 1])
            dv_ref[...] += dv0 + dv1

        @pl.when(qi > ki)
        def _():
            sweep(False)

        @pl.when(qi == ki)
        def _():
            sweep(True)

    qrow = lambda p, s, qt, kt: (qt[s], p)
    krow = lambda p, s, qt, kt: (kt[s], p)
    return pl.pallas_call(
        body, name="attention_bwd",
        out_shape=[jax.ShapeDtypeStruct((t, ATTN_W), F32)] * 5,
        grid_spec=pltpu.PrefetchScalarGridSpec(
            num_scalar_prefetch=2, grid=(4, int(q_tab.shape[0])),
            in_specs=[pl.BlockSpec((tq, 128), qrow),
                      pl.BlockSpec((tk, 128), lambda p, s, qt, kt: (kt[s], 4 + p)),
                      pl.BlockSpec((tk, 128), lambda p, s, qt, kt: (kt[s], 8 + p)),
                      pl.BlockSpec((None, tq, 128), lambda p, s, qt, kt: (p, qt[s], 0)),
                      pl.BlockSpec((None, tk, 128), lambda p, s, qt, kt: (p, kt[s], 0)),
                      pl.BlockSpec((tq, 128), qrow), pl.BlockSpec((tq, 128), qrow)],
            out_specs=[pl.BlockSpec((t, 128), lambda p, s, qt, kt: (0, p)),
                       pl.BlockSpec((tk, 128), krow), pl.BlockSpec((tk, 128), krow), pl.BlockSpec((tk, 128), krow),
                       pl.BlockSpec((t, 128), lambda p, s, qt, kt: (0, p))]),
        compiler_params=_params(("parallel", "arbitrary"), VMEM_BIG),
    )(q_tab, k_tab, z, z, z, aug_q, aug_k, o, do)


def _shifted(prev_rows, x, shift):
    tm = x.shape[0]
    return pltpu.roll(jnp.concatenate([prev_rows, x], axis=0), shift, 0)[8:8 + tm]


def _ahead(x, next_rows, shift):
    tm = x.shape[0]
    return pltpu.roll(jnp.concatenate([x, next_rows], axis=0), tm + 8 - shift, 0)[0:tm]


def _conv_col0(z):
    return (z.shape[1] - F_PAD - 3 * CONV_W) // CONV_W


def _conv_specs(tm, c0):
    cols = (c0, c0 + 1, c0 + 2)
    tiles = [pl.BlockSpec((tm, CONV_W), functools.partial(lambda i, c: (i, c), c=c)) for c in cols]
    halos = [pl.BlockSpec((8, CONV_W), functools.partial(lambda i, c: (jnp.maximum(i * (tm // 8) - 1, 0), c), c=c))
             for c in cols]
    return tiles, halos


def _conv_gate(z, conv_w):
    t = z.shape[0]
    tm = ROW_TILE
    nt = t // tm

    def body(cb_ref, cc_ref, ci_ref, hc_ref, hi_ref, w_ref, g_ref, gt_ref):
        i = pl.program_id(0)
        cc = cc_ref[...] * ci_ref[...]
        prev = jnp.where(i > 0, hc_ref[...] * hi_ref[...], 0.0)
        conv = w_ref[0:1, :] * _shifted(prev, cc, 2) + w_ref[1:2, :] * _shifted(prev, cc, 1) + w_ref[2:3, :] * cc
        g = cb_ref[...] * conv
        g_ref[...] = g.astype(BF16)
        gt_ref[...] = g.T.astype(BF16)

    (cb, cc, ci), (_, hc, hi) = _conv_specs(tm, _conv_col0(z))
    return pl.pallas_call(
        body, name="conv_gate_fwd",
        out_shape=[jax.ShapeDtypeStruct((t, CONV_W), BF16), jax.ShapeDtypeStruct((CONV_W, t), BF16)],
        grid=(nt,),
        in_specs=[cb, cc, ci, hc, hi, pl.BlockSpec((8, CONV_W), lambda i: (0, 0))],
        out_specs=[pl.BlockSpec((tm, CONV_W), lambda i: (i, 0)), pl.BlockSpec((CONV_W, tm), lambda i: (0, i))],
        compiler_params=_params(("parallel",)),
    )(z, z, z, z, z, conv_w)


def _conv_bwd(z, dg, conv_w):
    t = z.shape[0]
    tm = ROW_TILE
    nt = t // tm

    def body(cb_ref, cc_ref, ci_ref, hc_ref, hi_ref, dg_ref, ncb_ref, ndg_ref, w_ref, dz_ref, dw_ref):
        i = pl.program_id(0)

        @pl.when(i == 0)
        def _():
            dw_ref[...] = jnp.zeros_like(dw_ref)

        cb, c_c, c_in = cb_ref[...], cc_ref[...], ci_ref[...]
        cc = c_c * c_in
        prev = jnp.where(i > 0, hc_ref[...] * hi_ref[...], 0.0)
        cc1, cc2 = _shifted(prev, cc, 1), _shifted(prev, cc, 2)
        w0, w1, w2 = w_ref[0:1, :], w_ref[1:2, :], w_ref[2:3, :]
        conv = w0 * cc2 + w1 * cc1 + w2 * cc
        dgv = dg_ref[...]
        dconv = dgv * cb
        nxt = jnp.where(i < nt - 1, ndg_ref[...] * ncb_ref[...], 0.0)
        dcc = w2 * dconv + w1 * _ahead(dconv, nxt, 1) + w0 * _ahead(dconv, nxt, 2)
        dz_ref[:, 0:CONV_W] = (dgv * conv).astype(BF16)
        dz_ref[:, CONV_W:2 * CONV_W] = (dcc * c_in).astype(BF16)
        dz_ref[:, 2 * CONV_W:] = (dcc * c_c).astype(BF16)
        dw_ref[0:1, :] += jnp.sum(dconv * cc2, axis=0, keepdims=True)
        dw_ref[1:2, :] += jnp.sum(dconv * cc1, axis=0, keepdims=True)
        dw_ref[2:3, :] += jnp.sum(dconv * cc, axis=0, keepdims=True)

    c0 = _conv_col0(z)
    (cb, cc, ci), (_, hc, hi) = _conv_specs(tm, c0)
    nxt = lambda i, c: (jnp.minimum((i + 1) * (tm // 8), t // 8 - 1), c)
    return pl.pallas_call(
        body, name="conv_gate_bwd",
        out_shape=[jax.ShapeDtypeStruct((t, 3 * CONV_W), BF16), jax.ShapeDtypeStruct((8, CONV_W), F32)],
        grid=(nt,),
        in_specs=[cb, cc, ci, hc, hi, pl.BlockSpec((tm, CONV_W), lambda i: (i, 0)),
                  pl.BlockSpec((8, CONV_W), lambda i: nxt(i, c0)), pl.BlockSpec((8, CONV_W), lambda i: nxt(i, 0)),
                  pl.BlockSpec((8, CONV_W), lambda i: (0, 0))],
        out_specs=[pl.BlockSpec((tm, 3 * CONV_W), lambda i: (i, 0)), pl.BlockSpec((8, CONV_W), lambda i: (0, 0))],
        compiler_params=_params(("arbitrary",)),
    )(z, z, z, z, z, dg, z, dg, conv_w)


def _branch_mix(z, o, g, w_ab, w_cb, d):
    t = z.shape[0]
    tm = ROW_TILE
    ga_col = 0

    def body(o_ref, g_ref, ga_ref, gc_ref, wa_ref, wc_ref, mp_ref, mpt_ref, ot_ref):
        o_ = o_ref[...]
        ya = _dot(o_.astype(BF16), wa_ref[...])
        yc = _dot(g_ref[...], wc_ref[...])
        mp = _sigmoid(ga_ref[...]) * ya + _sigmoid(gc_ref[...]) * yc
        mp_ref[...] = mp.astype(BF16)
        mpt_ref[...] = mp.T.astype(BF16)
        ot_ref[...] = o_.T.astype(BF16)

    return pl.pallas_call(
        body, name="branch_mix_fwd",
        out_shape=[jax.ShapeDtypeStruct((t, d), BF16), jax.ShapeDtypeStruct((d, t), BF16),
                   jax.ShapeDtypeStruct((ATTN_W, t), BF16)],
        grid=(t // tm,),
        in_specs=[pl.BlockSpec((tm, ATTN_W), lambda i: (i, 0)), pl.BlockSpec((tm, CONV_W), lambda i: (i, 0)),
                  pl.BlockSpec((tm, d), lambda i: (i, ga_col)), pl.BlockSpec((tm, d), lambda i: (i, ga_col + 1)),
                  pl.BlockSpec((ATTN_W, d), lambda i: (0, 0)), pl.BlockSpec((CONV_W, d), lambda i: (0, 0))],
        out_specs=[pl.BlockSpec((tm, d), lambda i: (i, 0)), pl.BlockSpec((d, tm), lambda i: (0, i)),
                   pl.BlockSpec((ATTN_W, tm), lambda i: (0, i))],
        compiler_params=_params(("parallel",), VMEM_BIG),
    )(o, g, z, z, w_ab, w_cb)


def _branch_bwd(z, o, g, dmixed, w_out, w_ab, w_cb, d):
    t = z.shape[0]
    tm = ROW_TILE // 2
    ga_col = 0

    def body(dm_ref, o_ref, g_ref, ga_ref, gc_ref, wo_ref, wa_ref, wc_ref, dya_ref, dyc_ref, dgt_ref, do_ref, dg_ref):
        dmp = _dot_nt(dm_ref[...], wo_ref[...])
        ya = _dot(o_ref[...].astype(BF16), wa_ref[...])
        yc = _dot(g_ref[...], wc_ref[...])
        sa, sc = _sigmoid(ga_ref[...]), _sigmoid(gc_ref[...])
        dya = (dmp * sa).astype(BF16)
        dyc = (dmp * sc).astype(BF16)
        dya_ref[...] = dya
        dyc_ref[...] = dyc
        dgt_ref[:, :d] = (dmp * ya * sa * (1.0 - sa)).astype(BF16)
        dgt_ref[:, d:] = (dmp * yc * sc * (1.0 - sc)).astype(BF16)
        do_ref[...] = _dot_nt(dya, wa_ref[...])
        dg_ref[...] = _dot_nt(dyc, wc_ref[...])

    row = lambda i: (i, 0)
    fixed = lambda i: (0, 0)
    return pl.pallas_call(
        body, name="branch_mix_bwd",
        out_shape=[jax.ShapeDtypeStruct((t, d), BF16), jax.ShapeDtypeStruct((t, d), BF16),
                   jax.ShapeDtypeStruct((t, 2 * d), BF16), jax.ShapeDtypeStruct((t, ATTN_W), F32),
                   jax.ShapeDtypeStruct((t, CONV_W), F32)],
        grid=(t // tm,),
        in_specs=[pl.BlockSpec((tm, d), row), pl.BlockSpec((tm, ATTN_W), row), pl.BlockSpec((tm, CONV_W), row),
                  pl.BlockSpec((tm, d), lambda i: (i, ga_col)), pl.BlockSpec((tm, d), lambda i: (i, ga_col + 1)),
                  pl.BlockSpec((d, d), fixed), pl.BlockSpec((ATTN_W, d), fixed), pl.BlockSpec((CONV_W, d), fixed)],
        out_specs=[pl.BlockSpec((tm, d), row), pl.BlockSpec((tm, d), row), pl.BlockSpec((tm, 2 * d), row),
                   pl.BlockSpec((tm, ATTN_W), row), pl.BlockSpec((tm, CONV_W), row)],
        compiler_params=_params(("parallel",), VMEM_BIG),
    )(dmixed, o, g, z, z, w_out, w_ab, w_cb)


def _loss_grad(h, target_pad):
    t, d = h.shape
    tm = ROW_TILE

    def body(h_ref, t_ref, dy_ref, loss_ref):
        i = pl.program_id(0)

        @pl.when(i == 0)
        def _():
            loss_ref[...] = jnp.zeros_like(loss_ref)

        row = i * tm + lax.broadcasted_iota(jnp.int32, (tm, 1), 0)
        err = jnp.where(row >= N_FRONT, h_ref[...] - t_ref[...], 0.0)
        dy_ref[...] = err * (1.0 / d)
        per_row = jnp.sum(err * err, axis=1, keepdims=True) * (1.0 / d)
        loss_ref[...] += 0.5 * jnp.sum(per_row, axis=0, keepdims=True)

    return pl.pallas_call(
        body, name="loss_and_grad",
        out_shape=[jax.ShapeDtypeStruct((t, d), F32), jax.ShapeDtypeStruct((1, 128), F32)],
        grid=(t // tm,),
        in_specs=[pl.BlockSpec((tm, d), lambda i: (i, 0))] * 2,
        out_specs=[pl.BlockSpec((tm, d), lambda i: (i, 0)), pl.BlockSpec((1, 128), lambda i: (0, 0))],
        compiler_params=_params(("arbitrary",)),
    )(h, target_pad)


def _norm_bwd(name, x, g, dy, alpha):
    t, d = x.shape
    tm = ROW_TILE

    def body(x_ref, g_ref, dy_ref, dx_ref, dg_ref):
        @pl.when(pl.program_id(0) == 0)
        def _():
            dg_ref[...] = jnp.zeros_like(dg_ref)

        dx, dg = _rms_bwd(x_ref[...], g_ref[...], dy_ref[...])
        dx_ref[...] = (alpha * dx).astype(BF16)
        dg_ref[...] += alpha * dg

    row = pl.BlockSpec((tm, d), lambda i: (i, 0))
    vec = pl.BlockSpec((1, d), lambda i: (0, 0))
    return pl.pallas_call(
        body, name=name,
        out_shape=[jax.ShapeDtypeStruct((t, d), BF16), jax.ShapeDtypeStruct((1, d), F32)],
        grid=(t // tm,), in_specs=[row, vec, row], out_specs=[row, vec],
        compiler_params=_params(("arbitrary",)),
    )(x, g, dy)


def _ffn_bwd_mid(name, df, w_out, ab):
    t, d = df.shape
    cw = ab.shape[1] // 4
    tm = ROW_TILE

    def body(df_ref, w_ref, ab_ref, o_ref):
        ds = _dot_nt(df_ref[...], w_ref[...])
        a = ab_ref[:, :cw]
        b = ab_ref[:, cw:]
        sg = _sigmoid(a)
        o_ref[:, :cw] = (ds * b * (sg * (1.0 + a * (1.0 - sg)))).astype(BF16)
        o_ref[:, cw:] = (ds * (a * sg)).astype(BF16)

    return pl.pallas_call(
        body, name=name, out_shape=jax.ShapeDtypeStruct((t, 4 * cw), BF16),
        grid=(2, t // tm),
        in_specs=[pl.BlockSpec((tm, d), lambda j, i: (i, 0)), pl.BlockSpec((cw, d), lambda j, i: (j, 0)),
                  pl.BlockSpec((tm, 2 * cw), lambda j, i: (i, j))],
        out_specs=pl.BlockSpec((tm, 2 * cw), lambda j, i: (i, j)),
        compiler_params=_params(("parallel", "parallel"), VMEM_BIG),
    )(df, w_out, ab)


def _mm_nt_norm_bwd(name, dy, w, h, g, dh_in):
    t, kdim = dy.shape
    d = h.shape[1]
    tm = ROW_TILE // 2
    slots = w.ndim == 3

    def body(dy_ref, w_ref, h_ref, g_ref, dhi_ref, dh_ref, dg_ref):
        @pl.when(pl.program_id(0) == 0)
        def _():
            dg_ref[...] = jnp.zeros_like(dg_ref)

        if slots:
            cw = w_ref.shape[2]
            dn = _dot_nt(dy_ref[:, 0:cw], w_ref[_slot_of(0)])
            for k in range(1, 4):
                dn += _dot_nt(dy_ref[:, k * cw:(k + 1) * cw], w_ref[_slot_of(k)])
        else:
            dn = _dot_nt(dy_ref[...], w_ref[...])
        dx, dg = _rms_bwd(h_ref[...], g_ref[...], dn)
        dh_ref[...] = dhi_ref[...] + dx
        dg_ref[...] += dg

    row = pl.BlockSpec((tm, d), lambda i: (i, 0))
    vec = pl.BlockSpec((1, d), lambda i: (0, 0))
    return pl.pallas_call(
        body, name=name,
        out_shape=[jax.ShapeDtypeStruct((t, d), F32), jax.ShapeDtypeStruct((1, d), F32)],
        grid=(t // tm,),
        in_specs=[pl.BlockSpec((tm, kdim), lambda i: (i, 0)), pl.BlockSpec(w.shape, lambda i: (0,) * w.ndim),
                  row, vec, row],
        out_specs=[row, vec],
        compiler_params=_params(("arbitrary",), VMEM_BIG),
    )(dy, w, h, g, dh_in)


def _gate_bwd(df_pad, z, b_pad, f_col):
    t = z.shape[0]
    tm = ROW_TILE
    nt = t // tm

    def body(d_ref, z_ref, b_ref, dz_ref, db_ref, carry_ref):
        i = pl.program_id(0)

        @pl.when(i == 0)
        def _():
            carry_ref[...] = jnp.zeros_like(carry_ref)
            db_ref[...] = jnp.zeros_like(db_ref)

        tri = (lax.broadcasted_iota(jnp.int32, (tm, tm), 0) <= lax.broadcasted_iota(jnp.int32, (tm, tm), 1))
        tail = jnp.dot(tri.astype(F32), d_ref[...], preferred_element_type=F32, precision=lax.Precision.HIGHEST)
        tail = tail + carry_ref[0:1, :]
        carry_ref[...] = jnp.broadcast_to(tail[0:1, :], carry_ref.shape)
        row = (nt - 1 - i) * tm + lax.broadcasted_iota(jnp.int32, (tm, 1), 0)
        dlogit = jnp.where(row >= ROW_PAD, tail * _sigmoid(-(z_ref[...] + b_ref[...])), 0.0)
        dz_ref[...] = jnp.zeros_like(dz_ref)
        dz_ref[:, 0:128] = dlogit.astype(BF16)
        db_ref[...] += jnp.sum(dlogit, axis=0, keepdims=True)

    rev = lambda i: (nt - 1 - i, 0)
    return pl.pallas_call(
        body, name="forget_gate_bwd",
        out_shape=[jax.ShapeDtypeStruct((t, F_PAD), BF16), jax.ShapeDtypeStruct((1, 128), F32)],
        grid=(nt,),
        in_specs=[pl.BlockSpec((tm, 128), rev), pl.BlockSpec((tm, 128), lambda i: (nt - 1 - i, f_col // 128)),
                  pl.BlockSpec((1, 128), lambda i: (0, 0))],
        out_specs=[pl.BlockSpec((tm, F_PAD), rev), pl.BlockSpec((1, 128), lambda i: (0, 0))],
        scratch_shapes=[pltpu.VMEM((8, 128), F32)],
        compiler_params=_params(("arbitrary",)),
    )(df_pad, z, b_pad)


def _ffn_fwd(tag, n, w_in4, w_out, h, g_post, g_next):
    ab, s, s_t = _ffn_in(f"{tag}_in_fwd", n, w_in4)
    outs = _mm_resid_norm(f"{tag}_out_fwd", s, w_out, h, g_post, 0.5, g_next)
    return ab, s_t, outs


def _ffn_bwd(tag, dh, f, g_post, ab, s_t, n_t, w_in4, w_out, h_in, g_pre):
    d, cw = w_in4.shape[1], w_in4.shape[2]
    t = dh.shape[0]
    df, dg_post = _norm_bwd(f"{tag}_post_norm_bwd", f, g_post, dh, 0.5)
    dw_out = _weight_grad(f"{tag}_dw_out", s_t, df, d, out_rows=cw // 2)
    dab = _ffn_bwd_mid(f"{tag}_mid_bwd", df, w_out, ab)
    dh_in, dg_pre = _mm_nt_norm_bwd(f"{tag}_in_bwd", dab, w_in4, h_in, g_pre, dh)
    bk = _k_tile(t)
    dw_in = _matmul(
        f"{tag}_dw_in", n_t, dab, jax.ShapeDtypeStruct((4, d, cw), F32), (1, 4, t // bk),
        pl.BlockSpec((d, bk), lambda a, b, k: (0, k)), pl.BlockSpec((bk, cw), lambda a, b, k: (k, b)),
        pl.BlockSpec((None, d, cw), lambda a, b, k: (_slot_of(b), 0, 0)), vmem=VMEM_BIG)
    return dh_in, dg_post, dg_pre, dw_in, dw_out


def _pack_small(meta, conv, gains, b_forget):
    d = gains[0].shape[1]
    rows = [meta.reshape(4, d), jnp.pad(conv.reshape(1, 3 * 128), ((0, 0), (0, d - 3 * 128)))]
    rows += list(gains) + [jnp.pad(b_forget, ((0, 0), (0, d - HEADS)))]
    return jnp.concatenate(rows + [jnp.zeros((4, d), F32)], axis=0)


def _unpack_small(block):
    d = block.shape[1]
    meta = block[0:4].reshape(N_META, d // 4)
    conv = block[4, :3 * 128].reshape(1, 3, 128)
    gains = [block[5 + i:6 + i] for i in range(6)]
    return meta, conv, gains, block[11:12, :HEADS]


def kernel(x, meta_tokens, w_in, b_forget, conv_w, w_attn_branch, w_conv_branch, w_out, g_ffn1_pre, g_ffn1_post, w_ffn1_in, w_ffn1_out, g_mix_pre, g_mix_post, g_ffn2_pre, g_ffn2_post, w_ffn2_in, w_ffn2_out, loss_target, m_meta_tokens, m_w_in, m_b_forget, m_conv_w, m_w_attn_branch, m_w_conv_branch, m_w_out, m_g_ffn1_pre, m_g_ffn1_post, m_w_ffn1_in, m_w_ffn1_out, m_g_mix_pre, m_g_mix_post, m_g_ffn2_pre, m_g_ffn2_post, m_w_ffn2_in, m_w_ffn2_out, v_meta_tokens, v_w_in, v_b_forget, v_conv_w, v_w_attn_branch, v_w_conv_branch, v_w_out, v_g_ffn1_pre, v_g_ffn1_post, v_w_ffn1_in, v_w_ffn1_out, v_g_mix_pre, v_g_mix_post, v_g_ffn2_pre, v_g_ffn2_post, v_w_ffn2_in, v_w_ffn2_out):
    seq, d = x.shape[1], x.shape[2]
    t = seq + N_FRONT
    n_main = 3 * ATTN_W + 3 * CONV_W + 2 * d
    nz = n_main + F_PAD
    f_lo = 3 * ATTN_W
    c_arr = lax.axis_index("c").astype(jnp.int32).reshape(1)

    big = [w_in[0], w_attn_branch[0], w_conv_branch[0], w_out[0], w_ffn1_in[0], w_ffn1_out[0], w_ffn2_in[0], w_ffn2_out[0]]
    small_gather = jnp.concatenate(
        [meta_tokens.reshape(4, d), jnp.pad(conv_w.reshape(1, 3 * 128), ((0, 0), (0, d - 3 * 128))),
         jnp.zeros((11, d), F32)], axis=0)
    w_f1_in4, w_f1_out4, small4 = _all_gather([big[4].astype(BF16), big[5].astype(BF16), small_gather])
    rest, small4 = lax.optimization_barrier(([big[i].astype(BF16) for i in (0, 1, 2, 3, 6, 7)], small4))
    w_in4, w_ab4, w_cb4, w_out4, w_f2_in4, w_f2_out4 = _all_gather_async(rest)

    w_in_full = jnp.transpose(w_in4, (1, 0, 2)).reshape(d, 4 * w_in4.shape[2])
    g_lo = f_lo + HEADS + 3 * CONV_W
    w_in_pad = jnp.concatenate(
        [w_in_full[:, :f_lo], w_in_full[:, g_lo:], w_in_full[:, f_lo + HEADS:g_lo], w_in_full[:, f_lo:f_lo + HEADS],
         jnp.zeros((d, F_PAD - HEADS), BF16)], axis=1)
    w_ab = jnp.transpose(w_ab4, (1, 0, 2)).reshape(ATTN_W, d)
    w_cb = jnp.transpose(w_cb4, (1, 0, 2)).reshape(CONV_W, d)
    w_out_full = w_out4.reshape(d, d)
    w_f1_out = w_f1_out4.reshape(-1, d)
    w_f2_out = w_f2_out4.reshape(-1, d)
    meta_full = jnp.transpose(small4[:, 0:4].reshape(4, N_META, d // 4), (1, 0, 2)).reshape(N_META, d)
    conv_full = jnp.transpose(small4[:, 4, :3 * 128].reshape(4, 3, 128), (1, 0, 2)).reshape(3, CONV_W)
    conv_pad = jnp.pad(conv_full, ((0, 5), (0, 0)))
    b_pad = jnp.pad(b_forget, ((0, 0), (0, 128 - HEADS)))

    h0 = jnp.concatenate([jnp.zeros((ROW_PAD, d), F32), meta_full, x[0]], axis=0)
    target_pad = jnp.concatenate([jnp.zeros((N_FRONT, d), F32), loss_target[0]], axis=0)
    n1, n1_t = _norm_fwd("ffn1_pre_norm", h0, g_ffn1_pre)
    ab1, s1_t, (f1, h1, u, u_t) = _ffn_fwd("ffn1", n1, w_f1_in4, w_f1_out, h0, g_ffn1_post, g_mix_pre)
    qkv, z = _in_proj(u, w_in_pad)
    f_col = z.shape[1] - F_PAD
    f_cum = _gate_prep(z, b_pad, f_col)
    f_heads = f_cum[:, :HEADS]
    o, lse = _attn_fwd(qkv, *_attn_bias_operands(f_heads))
    g, g_t = _conv_gate(z, conv_pad)
    mp, mp_t, o_t = _branch_mix(z, o, g, w_ab, w_cb, d)
    mixed, h2, n2, n2_t = _mm_resid_norm("mix_out_fwd", mp, w_out_full, h1, g_mix_post, 1.0, g_ffn2_pre)
    ab2, s2_t, (f2, h3) = _ffn_fwd("ffn2", n2, w_f2_in4, w_f2_out, h2, g_ffn2_post, None)
    dh3, loss_part = _loss_grad(h3, target_pad)
    loss = lax.psum(loss_part[0, 0], ("x", "y", "c"))

    def reduce_scatter(label, tags, slots, sequencer_id):
        got = _pair_send_halves(f"grad_pair_exchange_{label}", slots)
        sums = [_pair_add(tag, s, a, c_arr, F32 if tag == "small" else BF16) for tag, s, a in zip(tags, slots, got)]
        if sequencer_id is None:
            arrived = _chip_scatter(f"grad_chip_scatter_{label}", sums)
        else:
            arrived = _chip_scatter_async(f"grad_chip_scatter_{label}", sums, sequencer_id)
        mine = [_chip_add(tag, a) for tag, a in zip(tags, arrived)]
        return dict(zip(tags, zip(mine, _pair_swap(f"grad_pair_swap_{label}", mine))))

    dh2, dg_f2_post, dg_f2_pre, dw_f2_in, dw_f2_out = _ffn_bwd(
        "ffn2", dh3, f2, g_ffn2_post, ab2, s2_t, n2_t, w_f2_in4, w_f2_out, h2, g_ffn2_pre)
    reduced = reduce_scatter("ffn2", ["w_ffn2_in", "w_ffn2_out"], [dw_f2_in, dw_f2_out.reshape(4, -1, d)], 2)
    dmixed, dg_mix_post = _norm_bwd("mix_post_norm_bwd", mixed, g_mix_post, dh2, 1.0)
    dw_out = _weight_grad("mix_dw_out", mp_t, dmixed, d)
    dya, dyc, dgates, do, dgconv = _branch_bwd(z, o, g, dmixed, w_out_full, w_ab, w_cb, d)
    dw_ab = _weight_grad("mix_dw_attn_branch", o_t, dya, d)
    dw_cb = _weight_grad("mix_dw_conv_branch", g_t, dyc, d)
    dz_conv, dconv_w = _conv_bwd(z, dgconv, conv_pad)
    front = lax.broadcasted_iota(jnp.int32, (t, 1), 0) < ROW_PAD
    lse_heads = jnp.where(front, 1e9, lse[:, ::HEAD_DIM])
    dq, dk, dv, dfk, dfq = _attn_bwd(qkv, *_attn_bias_operands(f_heads, lse_heads), o, do)
    df_pad = jnp.pad((dfq - dfk)[:, ::HEAD_DIM], ((0, 0), (0, 128 - HEADS)))
    dz_f, db_forget = _gate_bwd(df_pad, z, b_pad, f_col)
    dz = jnp.concatenate([dq.astype(BF16), dk.astype(BF16), dv.astype(BF16), dgates, dz_conv, dz_f], axis=1)
    dh1, dg_mix_pre = _mm_nt_norm_bwd("mix_in_bwd", dz, w_in_pad, h1, g_mix_pre, dh2)
    dw_in_pad = _weight_grad("mix_dw_in", u_t, dz, 512)
    cs = w_in4.shape[2]
    c_lo = f_lo + 2 * d
    dw_in_full = jnp.concatenate(
        [dw_in_pad[:, :f_lo], dw_in_pad[:, n_main:n_main + HEADS], dw_in_pad[:, c_lo:n_main], dw_in_pad[:, f_lo:c_lo]],
        axis=1)
    reduced.update(reduce_scatter(
        "mix", ["w_in", "w_attn_branch", "w_conv_branch", "w_out"],
        [jnp.transpose(dw_in_full.reshape(d, 4, cs), (1, 0, 2)),
         jnp.transpose(dw_ab.reshape(ATTN_W, 4, d // 4), (1, 0, 2)),
         jnp.transpose(dw_cb.reshape(CONV_W, 4, d // 4), (1, 0, 2)),
         dw_out.reshape(4, d // 4, d)], 3))
    dh0, dg_f1_post, dg_f1_pre, dw_f1_in, dw_f1_out = _ffn_bwd(
        "ffn1", dh1, f1, g_ffn1_post, ab1, s1_t, n1_t, w_f1_in4, w_f1_out, h0, g_ffn1_pre)
    grad_x = dh0[N_FRONT:][None]
    dmeta = dh0[ROW_PAD:N_FRONT]
    small_grad = jnp.stack([
        _pack_small(dmeta[:, j * (d // 4):(j + 1) * (d // 4)], dconv_w[:3, j * 128:(j + 1) * 128],
                    [dg_f1_pre, dg_f1_post, dg_mix_pre, dg_mix_post, dg_f2_pre, dg_f2_post], db_forget[:, :HEADS])
        for j in range(4)])
    reduced.update(reduce_scatter(
        "ffn1", ["w_ffn1_in", "w_ffn1_out", "small"], [dw_f1_in, dw_f1_out.reshape(4, -1, d), small_grad], None))
    tags = ["w_in", "w_attn_branch", "w_conv_branch", "w_out", "w_ffn1_in", "w_ffn1_out", "w_ffn2_in", "w_ffn2_out", "small"]
    halves = [reduced[tag][0] for tag in tags]
    others = [reduced[tag][1] for tag in tags]

    small = [g_ffn1_pre, g_ffn1_post, g_mix_pre, g_mix_post, g_ffn2_pre, g_ffn2_post]
    small_m = [m_g_ffn1_pre, m_g_ffn1_post, m_g_mix_pre, m_g_mix_post, m_g_ffn2_pre, m_g_ffn2_post]
    small_v = [v_g_ffn1_pre, v_g_ffn1_post, v_g_mix_pre, v_g_mix_post, v_g_ffn2_pre, v_g_ffn2_post]
    ws = big + [_pack_small(meta_tokens, conv_w[0], small, b_forget)]
    ms = [m_w_in[0], m_w_attn_branch[0], m_w_conv_branch[0], m_w_out[0], m_w_ffn1_in[0], m_w_ffn1_out[0],
          m_w_ffn2_in[0], m_w_ffn2_out[0], _pack_small(m_meta_tokens, m_conv_w[0], small_m, m_b_forget)]
    vs = [v_w_in[0], v_w_attn_branch[0], v_w_conv_branch[0], v_w_out[0], v_w_ffn1_in[0], v_w_ffn1_out[0],
          v_w_ffn2_in[0], v_w_ffn2_out[0], _pack_small(v_meta_tokens, v_conv_w[0], small_v, v_b_forget)]
    updates = [_adamw(tag, w, a, b, m, v, c_arr) for tag, w, a, b, m, v in zip(tags, ws, halves, others, ms, vs)]

    def leaves(big_vals, small_block):
        meta, conv, gains, bf = _unpack_small(small_block)
        w_in_, w_ab_, w_cb_, w_out_, f1_in, f1_out, f2_in, f2_out = [b[None] for b in big_vals]
        return [meta, w_in_, bf, conv, w_ab_, w_cb_, w_out_, gains[0], gains[1], f1_in, f1_out,
                gains[2], gains[3], gains[4], gains[5], f2_in, f2_out]

    out_g, out_d, out_m, out_v = [leaves([u_[k] for u_ in updates[:8]], updates[8][k]) for k in range(4)]
    return (loss, grad_x, *out_g, *out_d, *out_m, *out_v)
```

```python
import functools

import jax
import jax.numpy as jnp
from jax import lax
from jax.experimental import pallas as pl
from jax.experimental.pallas import tpu as pltpu
from jax.experimental.pallas import tpu_sc as plsc

N_META = 16
ROW_PAD = 112
N_FRONT = ROW_PAD + N_META
HEADS = 8
HEAD_DIM = 64
ATTN_W = HEADS * HEAD_DIM
CONV_W = 512
NORM_EPS = 1e-6
ROW_TILE = 640
F_PAD = 512
ATTN_ROW_PARTS = 1
NEG = -1e30
ADAM_LR = 0.001
ADAM_B1 = 0.9
ADAM_B2 = 0.999
ADAM_EPS = 1e-08
ADAM_WD = 0.01
ADAM_STEP = 10
VMEM_BIG = 56 * 1024 * 1024
MESH = pl.DeviceIdType.MESH
ANY = pl.BlockSpec(memory_space=pl.ANY)
F32 = jnp.float32
BF16 = jnp.bfloat16


def _params(sem, vmem=None):
    return pltpu.CompilerParams(dimension_semantics=sem, vmem_limit_bytes=vmem)


def _sigmoid(x):
    return 1.0 / (1.0 + jnp.exp(-x))


def _rstd(x):
    return lax.rsqrt(jnp.mean(x * x, axis=-1, keepdims=True) + NORM_EPS)


def _rms_bwd(x, g, dy):
    r = _rstd(x)
    xr = x * r
    gdy = g * dy
    dx = r * (gdy - xr * jnp.mean(xr * gdy, axis=-1, keepdims=True))
    return dx, jnp.sum(dy * xr, axis=0, keepdims=True)


def _dot(a, b):
    return jnp.dot(a, b, preferred_element_type=F32)


def _dot_nt(a, b):
    return lax.dot_general(a, b, (((1,), (1,)), ((), ())), preferred_element_type=F32)


def _k_tile(t):
    return 1664 if t % 1664 == 0 else ROW_TILE


def _place():
    x, y, c = lax.axis_index("x"), lax.axis_index("y"), lax.axis_index("c")
    chips = [(1 - x, y), (x, 1 - y), (1 - x, 1 - y)]
    return x, y, c, chips


def _all_gather(shards):
    n = len(shards)
    split = [s.reshape(2, s.shape[0] // 2, s.shape[1]) for s in shards]

    def body(*refs):
        ins, outs = refs[:n], refs[n:2 * n]
        send_sems, recv_sems = refs[2 * n:]
        x, y, c, chips = _place()
        me = 2 * x + y
        sibling = (x, y, 1 - c)

        def remote(i, k, slot, part, to, src=None):
            dst = outs[i].at[slot, part]
            return pltpu.make_async_remote_copy(
                src_ref=dst if src is None else src, dst_ref=dst,
                send_sem=send_sems.at[i, k], recv_sem=recv_sems.at[i, k],
                device_id=to, device_id_type=MESH)

        started = []
        for i in range(n):
            for k, (cx, cy) in enumerate(chips):
                cp = remote(i, k, me, c, (cx, cy, c), src=ins[i].at[c])
                cp.start()
                started.append(cp)
        for i in range(n):
            for k, (cx, cy) in enumerate(chips):
                remote(i, k, 2 * cx + cy, c, (x, y, c)).wait_recv()
                cp = remote(i, 3 + k, 2 * cx + cy, c, sibling)
                cp.start()
                started.append(cp)
        for i in range(n):
            for k, (cx, cy) in enumerate(chips):
                remote(i, 3 + k, 2 * cx + cy, 1 - c, (x, y, c)).wait_recv()
        for cp in started:
            cp.wait_send()

    outs = pl.pallas_call(
        body, name="all_gather_weights",
        out_shape=[jax.ShapeDtypeStruct((4,) + s.shape, s.dtype) for s in split],
        in_specs=[ANY] * n, out_specs=[ANY] * n,
        scratch_shapes=[pltpu.SemaphoreType.DMA((n, 6)), pltpu.SemaphoreType.DMA((n, 6))],
    )(*split)
    me =2 * lax.axis_index("x") + lax.axis_index("y")
    outs = [lax.dynamic_update_slice(o, s[None], (me, 0, 0, 0)) for o, s in zip(outs, split)]
    return [o.reshape((4,) + s.shape) for o, s in zip(outs, shards)]


def _all_gather_async(shards):
    n = len(shards)
    split = [s.reshape(2, s.shape[0] // 2, s.shape[1]) for s in shards]
    ins = [jax.new_ref(s, memory_space=pltpu.MemorySpace.HBM) for s in split]
    outs = [jax.empty_ref(jax.ShapeDtypeStruct((4,) + s.shape, s.dtype), memory_space=pltpu.MemorySpace.HBM)
            for s in split]

    @pl.kernel(mesh=plsc.ScalarSubcoreMesh(axis_name="sequencer", num_cores=1), name="all_gather_rest",
               scratch_types=(pltpu.SemaphoreType.DMA((n, 6)), pltpu.SemaphoreType.DMA((n, 6))),
               compiler_params=pltpu.CompilerParams(collective_id=1))
    def launch(send_sems, recv_sems):
        x, y, c, chips = _place()
        me = 2 * x + y
        sibling = (x, y, 1 - c)
        barrier = pltpu.get_barrier_semaphore()
        for peer in [(cx, cy, c) for cx, cy in chips] + [sibling]:
            pl.semaphore_signal(barrier, inc=1, device_id=peer, device_id_type=MESH)
        pl.semaphore_wait(barrier, 4)

        def remote(i, k, slot, part, to, src=None):
            dst = outs[i].at[slot, part]
            return pltpu.make_async_remote_copy(
                src_ref=dst if src is None else src, dst_ref=dst,
                send_sem=send_sems.at[i, k], recv_sem=recv_sems.at[i, k],
                device_id=to, device_id_type=MESH)

        started = []
        for i in range(n):
            for k, (cx, cy) in enumerate(chips):
                cp = remote(i, k, me, c, (cx, cy, c), src=ins[i].at[c])
                cp.start()
                started.append(cp)
        for i in range(n):
            for k, (cx, cy) in enumerate(chips):
                remote(i, k, 2 * cx + cy, c, (x, y, c)).wait_recv()
                cp = remote(i, 3 + k, 2 * cx + cy, c, sibling)
                cp.start()
                started.append(cp)
        for i in range(n):
            for k, (cx, cy) in enumerate(chips):
                remote(i, 3 + k, 2 * cx + cy, 1 - c, (x, y, c)).wait_recv()
        for cp in started:
            cp.wait_send()

    launch()
    raw = [o[...] for o in outs]

    def finish(after):
        arrived, _ = lax.optimization_barrier((raw, after))
        me = 2 * lax.axis_index("x") + lax.axis_index("y")
        gathered = [lax.dynamic_update_slice(a, s[None], (me, 0, 0, 0)) for a, s in zip(arrived, split)]
        return [g.reshape((4,) + s.shape) for g, s in zip(gathered, shards)]

    return finish


def _pair_send_halves(name, grads):
    n = len(grads)

    def body(*refs):
        ins, outs = refs[:n], refs[n:2 * n]
        send_sems, recv_sems = refs[2 * n:]
        x, y, c, _ = _place()
        cps = []
        for i in range(n):
            half = ins[i].shape[1] // 2
            cp = pltpu.make_async_remote_copy(
                src_ref=ins[i].at[:, pl.ds((1 - c) * half, half)], dst_ref=outs[i],
                send_sem=send_sems.at[i], recv_sem=recv_sems.at[i],
                device_id=(x, y, 1 - c), device_id_type=MESH)
            cp.start()
            cps.append(cp)
        for cp in cps:
            cp.wait()

    return pl.pallas_call(
        body, name=name,
        out_shape=[jax.ShapeDtypeStruct((4, g.shape[1] // 2, g.shape[2]), g.dtype) for g in grads],
        in_specs=[ANY] * n, out_specs=[ANY] * n,
        scratch_shapes=[pltpu.SemaphoreType.DMA((n,)), pltpu.SemaphoreType.DMA((n,))],
    )(*grads)


def _chip_scatter(name, parts):
    n = len(parts)

    def body(*refs):
        _scatter_copies(refs[:n], refs[n:2 * n], *refs[2 * n:])

    arrived = pl.pallas_call(
        body, name=name,
        out_shape=[jax.ShapeDtypeStruct(p.shape, p.dtype) for p in parts],
        in_specs=[ANY] * n, out_specs=[ANY] * n,
        scratch_shapes=[pltpu.SemaphoreType.DMA((n, 3)), pltpu.SemaphoreType.DMA((n, 3))],
    )(*parts)
    return _own_slots(parts, arrived)


def _scatter_copies(ins, outs, send_sems, recv_sems):
    x, y, c, chips = _place()
    me = 2 * x + y
    sends = []
    for i in range(len(ins)):
        for k, (cx, cy) in enumerate(chips):
            cp = pltpu.make_async_remote_copy(
                src_ref=ins[i].at[2 * cx + cy], dst_ref=outs[i].at[me],
                send_sem=send_sems.at[i, k], recv_sem=recv_sems.at[i, k],
                device_id=(cx, cy, c), device_id_type=MESH)
            cp.start()
            sends.append(cp)
    for i in range(len(ins)):
        for k, (cx, cy) in enumerate(chips):
            got = outs[i].at[2 * cx + cy]
            pltpu.make_async_remote_copy(
                src_ref=got, dst_ref=got, send_sem=send_sems.at[i, k], recv_sem=recv_sems.at[i, k],
                device_id=(x, y, c), device_id_type=MESH).wait_recv()
    for cp in sends:
        cp.wait_send()


def _own_slots(parts, arrived):
    me = 2 * lax.axis_index("x") + lax.axis_index("y")
    return [lax.dynamic_update_slice(a, lax.dynamic_slice_in_dim(p, me, 1, axis=0), (me, 0, 0))
            for p, a in zip(parts, arrived)]


def _chip_scatter_async(name, parts, collective_id):
    n = len(parts)
    ins = [jax.new_ref(p, memory_space=pltpu.MemorySpace.HBM) for p in parts]
    outs = [jax.empty_ref(jax.ShapeDtypeStruct(p.shape, p.dtype), memory_space=pltpu.MemorySpace.HBM) for p in parts]

    @pl.kernel(mesh=plsc.ScalarSubcoreMesh(axis_name="sequencer", num_cores=1), name=name,
               scratch_types=(pltpu.SemaphoreType.DMA((n, 3)), pltpu.SemaphoreType.DMA((n, 3))),
               compiler_params=pltpu.CompilerParams(collective_id=collective_id))
    def launch(send_sems, recv_sems):
        x, y, c, chips = _place()
        barrier = pltpu.get_barrier_semaphore()
        for cx, cy in chips:
            pl.semaphore_signal(barrier, inc=1, device_id=(cx, cy, c), device_id_type=MESH)
        pl.semaphore_wait(barrier, 3)
        _scatter_copies(ins, outs, send_sems, recv_sems)

    launch()
    return _own_slots(parts, [o[...] for o in outs])


def _pair_swap(name, halves):
    n = len(halves)

    def body(*refs):
        ins, outs = refs[:n], refs[n:2 * n]
        send_sems, recv_sems = refs[2 * n:]
        x, y, c, _ = _place()
        cps = []
        for i in range(n):
            cp = pltpu.make_async_remote_copy(
                src_ref=ins[i], dst_ref=outs[i], send_sem=send_sems.at[i], recv_sem=recv_sems.at[i],
                device_id=(x, y, 1 - c), device_id_type=MESH)
            cp.start()
            cps.append(cp)
        for cp in cps:
            cp.wait()

    return pl.pallas_call(
        body, name=name,
        out_shape=[jax.ShapeDtypeStruct(h.shape, h.dtype) for h in halves],
        in_specs=[ANY] * n, out_specs=[ANY] * n,
        scratch_shapes=[pltpu.SemaphoreType.DMA((n,)), pltpu.SemaphoreType.DMA((n,))],
    )(*halves)


def _row_block(rows, cols, n_bufs, budget=20 * 1024 * 1024):
    best = min(rows, 16)
    for b in range(16, rows + 1, 16):
        if rows % b == 0 and 2 * n_bufs * b * cols * 4 <= budget:
            best = b
    return best


def _pair_add(tag, grad, got, c_arr, out_dtype):
    _, rows, cols = grad.shape
    half = rows // 2
    bh = _row_block(half, cols, 3)
    nb = half // bh

    def body(c_ref, g_ref, a_ref, o_ref):
        o_ref[...] = (g_ref[...] + a_ref[...]).astype(out_dtype)

    return pl.pallas_call(
        body, name=f"pair_add_{tag}",
        out_shape=jax.ShapeDtypeStruct((4, half, cols), out_dtype),
        grid_spec=pltpu.PrefetchScalarGridSpec(
            num_scalar_prefetch=1, grid=(4, nb),
            in_specs=[pl.BlockSpec((None, bh, cols), lambda j, r, c: (j, c[0] * nb + r, 0)),
                      pl.BlockSpec((None, bh, cols), lambda j, r, c: (j, r, 0))],
            out_specs=pl.BlockSpec((None, bh, cols), lambda j, r, c: (j, r, 0))),
        compiler_params=_params(("parallel", "parallel")),
    )(c_arr, grad, got)


def _chip_add(tag, parts):
    _, half, cols = parts.shape
    bh = _row_block(half, cols, 5)

    def body(p_ref, o_ref):
        a, b, c, d = [p_ref[j].astype(F32) for j in range(4)]
        o_ref[...] = ((a + b) + c) + d

    return pl.pallas_call(
        body, name=f"chip_add_{tag}",
        out_shape=jax.ShapeDtypeStruct((half, cols), F32),
        grid=(half // bh,),
        in_specs=[pl.BlockSpec((4, bh, cols), lambda r: (0, r, 0))],
        out_specs=pl.BlockSpec((bh, cols), lambda r: (r, 0)),
        compiler_params=_params(("parallel",)),
    )(parts)


def _adamw(tag, w, mine, theirs, m, v, c_arr):
    rows, cols = w.shape
    half = rows // 2
    br = _row_block(half, cols, 9)
    nb = half // br

    def body(c_ref, w_ref, a_ref, b_ref, m_ref, v_ref, g_ref, d_ref, mo_ref, vo_ref):
        own = (pl.program_id(0) // nb) == c_ref[0]
        g = jnp.where(own, a_ref[...], b_ref[...])
        g_ref[...] = g
        m_new = ADAM_B1 * m_ref[...] + (1.0 - ADAM_B1) * g
        v_new = ADAM_B2 * v_ref[...] + (1.0 - ADAM_B2) * (g * g)
        m_hat = m_new / (1.0 - ADAM_B1 ** ADAM_STEP)
        v_hat = v_new / (1.0 - ADAM_B2 ** ADAM_STEP)
        d_ref[...] = -ADAM_LR * (m_hat / (jnp.sqrt(v_hat) + ADAM_EPS) + ADAM_WD * w_ref[...])
        mo_ref[...] = m_new
        vo_ref[...] = v_new

    spec = pl.BlockSpec((br, cols), lambda r, c: (r, 0))
    mine_spec = pl.BlockSpec((br, cols), lambda r, c: (jnp.clip(r - c[0] * nb, 0, nb - 1), 0))
    theirs_spec = pl.BlockSpec((br, cols), lambda r, c: (jnp.clip(r - (1 - c[0]) * nb, 0, nb - 1), 0))
    return pl.pallas_call(
        body, name=f"adamw_{tag}",
        out_shape=[jax.ShapeDtypeStruct((rows, cols), F32)] * 4,
        grid_spec=pltpu.PrefetchScalarGridSpec(
            num_scalar_prefetch=1, grid=(rows // br,),
            in_specs=[spec, mine_spec, theirs_spec, spec, spec], out_specs=[spec] * 4),
        compiler_params=_params(("arbitrary",)),
    )(c_arr, w, mine, theirs, m, v)


def _matmul(name, x, w, out_shape, grid, x_spec, w_spec, o_spec, *, nt=False, vmem=None):
    nk = grid[2]
    acc_shape = tuple(d for d in o_spec.block_shape if d is not None)

    def body(x_ref, w_ref, o_ref, acc_ref):
        k = pl.program_id(2)
        part = _dot_nt(x_ref[...], w_ref[...]) if nt else _dot(x_ref[...], w_ref[...])
        if nk == 1:
            o_ref[...] = part.astype(o_ref.dtype)
        else:
            @pl.when(k == 0)
            def _():
                acc_ref[...] = part

            @pl.when(k > 0)
            def _():
                acc_ref[...] += part

            @pl.when(k == nk - 1)
            def _():
                o_ref[...] = acc_ref[...].astype(o_ref.dtype)

    return pl.pallas_call(
        body, name=name, out_shape=out_shape, grid=grid,
        in_specs=[x_spec, w_spec], out_specs=o_spec,
        scratch_shapes=[pltpu.VMEM(acc_shape if nk > 1 else (8, 128), F32)],
        compiler_params=_params(("parallel", "parallel", "arbitrary"), vmem),
    )(x, w)


def _weight_grad(name, xt, dy, bn, out_rows=None):
    m, t = xt.shape
    n = dy.shape[1]
    bm = m if out_rows is None else out_rows
    bk = _k_tile(t)
    return _matmul(
        name, xt, dy, jax.ShapeDtypeStruct((m, n), F32), (m // bm, n // bn, t // bk),
        pl.BlockSpec((bm, bk), lambda a, b, k: (a, k)),
        pl.BlockSpec((bk, bn), lambda a, b, k: (k, b)),
        pl.BlockSpec((bm, bn), lambda a, b, k: (a, b)), vmem=VMEM_BIG)


def _norm_fwd(name, h, g):
    t, d = h.shape
    tm = ROW_TILE

    def body(h_ref, g_ref, n_ref, nt_ref):
        x = h_ref[...]
        y = x * _rstd(x) * g_ref[...]
        n_ref[...] = y.astype(BF16)
        nt_ref[...] = y.T.astype(BF16)

    return pl.pallas_call(
        body, name=name,
        out_shape=[jax.ShapeDtypeStruct((t, d), BF16), jax.ShapeDtypeStruct((d, t), BF16)],
        grid=(t // tm,),
        in_specs=[pl.BlockSpec((tm, d), lambda i: (i, 0)), pl.BlockSpec((1, d), lambda i: (0, 0))],
        out_specs=[pl.BlockSpec((tm, d), lambda i: (i, 0)), pl.BlockSpec((d, tm), lambda i: (0, i))],
        compiler_params=_params(("parallel",)),
    )(h, g)


def _slot_of(kk):
    return (kk % 2) * 2 + kk // 2


def _ffn_in(name, n, w4):
    t, d = n.shape
    cw = w4.shape[2]
    tm = ROW_TILE

    def body(x_ref, wg_ref, wu_ref, ab_ref, s_ref, st_ref):
        x = x_ref[...]
        a = _dot(x, wg_ref[...])
        b = _dot(x, wu_ref[...])
        ab_ref[:, :cw] = a
        ab_ref[:, cw:] = b
        s = a * _sigmoid(a) * b
        s_ref[...] = s.astype(BF16)
        st_ref[...] = s.T.astype(BF16)

    return pl.pallas_call(
        body, name=name,
        out_shape=[jax.ShapeDtypeStruct((t, 4 * cw), F32), jax.ShapeDtypeStruct((t, 2 * cw), BF16),
                   jax.ShapeDtypeStruct((2 * cw, t), BF16)],
        grid=(2, t // tm),
        in_specs=[pl.BlockSpec((tm, d), lambda j, i: (i, 0)),
                  pl.BlockSpec((None, d, cw), lambda j, i: (j, 0, 0)),
                  pl.BlockSpec((None, d, cw), lambda j, i: (2 + j, 0, 0))],
        out_specs=[pl.BlockSpec((tm, 2 * cw), lambda j, i: (i, j)),
                   pl.BlockSpec((tm, cw), lambda j, i: (i, j)),
                   pl.BlockSpec((cw, tm), lambda j, i: (j, i))],
        compiler_params=_params(("parallel", "parallel"), VMEM_BIG),
    )(n, w4, w4)


def _mm_resid_norm(name, x, w, h, g_post, alpha, g_next):
    t, kdim = x.shape
    d = w.shape[1]
    tm = ROW_TILE
    with_next = g_next is not None

    def body(x_ref, w_ref, h_ref, gp_ref, gn_ref, f_ref, hn_ref, *rest):
        f = _dot(x_ref[...], w_ref[...])
        f_ref[...] = f
        hn = h_ref[...] + alpha * (f * _rstd(f) * gp_ref[...])
        hn_ref[...] = hn
        if with_next:
            y = hn * _rstd(hn) * gn_ref[...]
            rest[0][...] = y.astype(BF16)
            rest[1][...] = y.T.astype(BF16)

    row = lambda i: (i, 0)
    vec = pl.BlockSpec((1, d), lambda i: (0, 0))
    out_shape = [jax.ShapeDtypeStruct((t, d), F32), jax.ShapeDtypeStruct((t, d), F32)]
    out_specs = [pl.BlockSpec((tm, d), row), pl.BlockSpec((tm, d), row)]
    if with_next:
        out_shape += [jax.ShapeDtypeStruct((t, d), BF16), jax.ShapeDtypeStruct((d, t), BF16)]
        out_specs += [pl.BlockSpec((tm, d), row), pl.BlockSpec((d, tm), lambda i: (0, i))]
    return pl.pallas_call(
        body, name=name, out_shape=out_shape, grid=(t // tm,),
        in_specs=[pl.BlockSpec((tm, kdim), row), pl.BlockSpec((kdim, d), lambda i: (0, 0)),
                  pl.BlockSpec((tm, d), row), vec, vec],
        out_specs=out_specs,
        compiler_params=_params(("parallel",), VMEM_BIG),
    )(x, w, h, g_post, g_post if g_next is None else g_next)


def _in_proj(u, w):
    t, d = u.shape
    nz = w.shape[1]
    nq = 3 * ATTN_W
    tm = ROW_TILE // 2

    def body(u_ref, w_ref, qkv_ref, z_ref):
        qkv_ref[...] = _dot(u_ref[...], w_ref[:, 0:nq]).astype(BF16)
        z_ref[...] = _dot(u_ref[...], w_ref[:, nq:])

    return pl.pallas_call(
        body, name="mix_in_proj",
        out_shape=[jax.ShapeDtypeStruct((t, nq), BF16), jax.ShapeDtypeStruct((t, nz - nq), F32)],
        grid=(t // tm,),
        in_specs=[pl.BlockSpec((tm, d), lambda i: (i, 0)), pl.BlockSpec((d, nz), lambda i: (0, 0))],
        out_specs=[pl.BlockSpec((tm, nq), lambda i: (i, 0)), pl.BlockSpec((tm, nz - nq), lambda i: (i, 0))],
        compiler_params=_params(("parallel",), VMEM_BIG),
    )(u, w)


def _gate_prep(z, b_pad, f_col):
    t = z.shape[0]
    tm = ROW_TILE

    def body(z_ref, b_ref, f_ref, carry_ref):
        i = pl.program_id(0)

        @pl.when(i == 0)
        def _():
            carry_ref[...] = jnp.zeros_like(carry_ref)

        xs = z_ref[...] + b_ref[...]
        logf = jnp.minimum(xs, 0.0) - jnp.log(1.0 + jnp.exp(-jnp.abs(xs)))
        row = i * tm + lax.broadcasted_iota(jnp.int32, (tm, 1), 0)
        logf = jnp.where(row >= ROW_PAD, logf, 0.0)
        tri = (lax.broadcasted_iota(jnp.int32, (tm, tm), 0) >= lax.broadcasted_iota(jnp.int32, (tm, tm), 1))
        f = jnp.dot(tri.astype(F32), logf, preferred_element_type=F32, precision=lax.Precision.HIGHEST)
        f = f + carry_ref[0:1, :]
        f_ref[...] = f
        carry_ref[...] = jnp.broadcast_to(f[tm - 1:tm, :], carry_ref.shape)

    return pl.pallas_call(
        body, name="forget_gate_cumsum", out_shape=jax.ShapeDtypeStruct((t, 128), F32),
        grid=(t // tm,),
        in_specs=[pl.BlockSpec((tm, 128), lambda i: (i, f_col // 128)), pl.BlockSpec((1, 128), lambda i: (0, 0))],
        out_specs=pl.BlockSpec((tm, 128), lambda i: (i, 0)),
        scratch_shapes=[pltpu.VMEM((8, 128), F32)],
        compiler_params=_params(("arbitrary",)),
    )(z, b_pad)


def _lane_halves():
    lane = lax.broadcasted_iota(jnp.int32, (1, 128), 1)
    return lane < HEAD_DIM


def _causal_mask(tq, tk, row0=0):
    row = row0 + lax.broadcasted_iota(jnp.int32, (tq, 1), 0)
    col = lax.broadcasted_iota(jnp.int32, (1, tk), 1)
    return col <= row


def _lane_one(lane):
    return (lax.broadcasted_iota(jnp.int32, (1, 128), 1) == lane).astype(BF16)


def _split3(x):
    hi = x.astype(BF16)
    rest = x - hi.astype(F32)
    mid = rest.astype(BF16)
    return hi, mid, (rest - mid.astype(F32)).astype(BF16)


def _split3_glue(x):
    hi = lax.reduce_precision(x, 8, 7)
    mid = lax.reduce_precision(x - hi, 8, 7)
    lo = lax.reduce_precision((x - hi) - mid, 8, 7)
    return hi.astype(BF16), mid.astype(BF16), lo.astype(BF16)


def _aug_pairs(cols):
    t = cols[0].shape[0]
    a = jnp.pad(jnp.stack(cols, axis=2), ((0, 0), (0, 0), (0, HEAD_DIM - len(cols))))
    a = a.reshape(t, 4, 2, HEAD_DIM)[:, :, ::-1, :]
    return jnp.transpose(a.reshape(t, 4, 128), (1, 0, 2))


def _attn_bias_operands(f_heads, lse_heads=None):
    t = f_heads.shape[0]
    one = jnp.ones((t, HEADS), BF16)
    row = lax.broadcasted_iota(jnp.int32, (t, 1), 0)
    fq = _split3_glue(f_heads)
    fk = _split3_glue(jnp.where(row < ROW_PAD, 1e9, f_heads))
    q_cols = list(fq) + [one] * 3
    k_cols = [one] * 3 + [-c for c in fk]
    if lse_heads is not None:
        q_cols += [-c for c in _split3_glue(lse_heads)]
        k_cols += [one] * 3
    return _aug_pairs(q_cols), _aug_pairs(k_cols)


def _attn_steps(nq, by_key):
    if by_key:
        pairs = [(qi, ki) for ki in range(nq) for qi in range(ki, nq)]
    else:
        pairs = [(qi, ki) for qi in range(nq) for ki in range(qi + 1)]
    return (jnp.array([p[0] for p in pairs], jnp.int32), jnp.array([p[1] for p in pairs], jnp.int32))


def _attn_fwd(z, aug_q, aug_k):
    t = z.shape[0]
    tq = tk = ROW_TILE
    nq = t // tq
    q_tab, k_tab = _attn_steps(nq, by_key=False)

    def body(qt_ref, kt_ref, q_ref, k_ref, v_ref, aq_ref, ak_ref, o_ref, lse_ref, m_ref, l_ref, acc_ref):
        step = pl.program_id(1)
        qi, ki = qt_ref[step], kt_ref[step]

        @pl.when(ki == 0)
        def _():
            m_ref[...] = jnp.full_like(m_ref, NEG)
            l_ref[...] = jnp.zeros_like(l_ref)
            acc_ref[...] = jnp.zeros_like(acc_ref)

        def sweep(diagonal):
            first = _lane_halves()
            q = (q_ref[...] * (HEAD_DIM ** -0.5)).astype(BF16)
            k = k_ref[...].astype(BF16)
            v = v_ref[...].astype(BF16)
            aq, ak = aq_ref[...], ak_ref[...]
            halves = (first, jnp.logical_not(first))
            qa = [jnp.where(lanes, q, aq) for lanes in halves]
            ka = [jnp.where(lanes, k, ak) for lanes in halves]
            va = [jnp.where(lanes, v, _lane_one(a0)) for lanes, a0 in zip(halves, (HEAD_DIM, 0))]
            chains = [(hh, r) for r in range(ATTN_ROW_PARTS) for hh in range(2)]
            rp = tq // ATTN_ROW_PARTS
            rows = [slice(r * rp, (r + 1) * rp) for _, r in chains]
            s = [_dot_nt(qa[hh][rw], ka[hh]) for (hh, _), rw in zip(chains, rows)]
            if diagonal:
                s = [jnp.where(_causal_mask(rp, tk, rw.start), s_c, NEG) for s_c, rw in zip(s, rows)]
            m_prev = [m_ref[rw, hh * HEAD_DIM:hh * HEAD_DIM + 1] for (hh, _), rw in zip(chains, rows)]
            m_new = [jnp.maximum(mp, jnp.max(s_c, axis=1, keepdims=True)) for mp, s_c in zip(m_prev, s)]
            p = [jnp.exp(s_c - m_c).astype(BF16) for s_c, m_c in zip(s, m_new)]
            pv = [_dot(p_c, va[hh]) for p_c, (hh, _) in zip(p, chains)]
            alpha = [jnp.exp(mp - m_c) for mp, m_c in zip(m_prev, m_new)]
            for r in range(ATTN_ROW_PARTS):
                (m0, m1), (al0, al1), (pv0, pv1) = [x[2 * r:2 * r + 2] for x in (m_new, alpha, pv)]
                rw = rows[2 * r]
                l0 = al0 * l_ref[rw, 0:1] + pv0[:, HEAD_DIM:HEAD_DIM + 1]
                l1 = al1 * l_ref[rw, HEAD_DIM:HEAD_DIM + 1] + pv1[:, 0:1]
                acc_ref[rw, :] = acc_ref[rw, :] * jnp.where(first, al0, al1) + jnp.where(first, pv0, pv1)
                m_ref[rw, :] = jnp.where(first, m0, m1)
                l_ref[rw, :] = jnp.where(first, l0, l1)

        @pl.when(ki < qi)
        def _():
            sweep(False)

        @pl.when(ki == qi)
        def _():
            sweep(True)
            o_ref[...] = acc_ref[...] / l_ref[...]
            lse_ref[...] = m_ref[...] + jnp.log(l_ref[...])

    return pl.pallas_call(
        body, name="attention_fwd",
        out_shape=[jax.ShapeDtypeStruct((t, ATTN_W), F32), jax.ShapeDtypeStruct((t, ATTN_W), F32)],
        grid_spec=pltpu.PrefetchScalarGridSpec(
            num_scalar_prefetch=2, grid=(4, int(q_tab.shape[0])),
            in_specs=[pl.BlockSpec((tq, 128), lambda p, s, qt, kt: (qt[s], p)),
                      pl.BlockSpec((tk, 128), lambda p, s, qt, kt: (kt[s], 4 + p)),
                      pl.BlockSpec((tk, 128), lambda p, s, qt, kt: (kt[s], 8 + p)),
                      pl.BlockSpec((None, tq, 128), lambda p, s, qt, kt: (p, qt[s], 0)),
                      pl.BlockSpec((None, tk, 128), lambda p, s, qt, kt: (p, kt[s], 0))],
            out_specs=[pl.BlockSpec((tq, 128), lambda p, s, qt, kt: (qt[s], p)),
                       pl.BlockSpec((tq, 128), lambda p, s, qt, kt: (qt[s], p))],
            scratch_shapes=[pltpu.VMEM((tq, 128), F32)] * 3),
        compiler_params=_params(("parallel", "arbitrary")),
    )(q_tab, k_tab, z, z, z, aug_q, aug_k)


def _attn_bwd(z, aug_q, aug_k, o, do):
    t = z.shape[0]
    tq = tk = ROW_TILE
    nq = t // tq
    q_tab, k_tab = _attn_steps(nq, by_key=True)
    tn = (((0,), (0,)), ((), ()))

    def body(qt_ref, kt_ref, q_ref, k_ref, v_ref, aq_ref, ak_ref, o_ref, do_ref,
             dq_ref, dk_ref, dv_ref, dfk_ref, dfq_ref):
        step = pl.program_id(1)
        qi, ki = qt_ref[step], kt_ref[step]
        rows = pl.ds(pl.multiple_of(qi * tq, tq), tq)

        @pl.when(ki == 0)
        def _():
            dq_ref[rows, :] = jnp.zeros((tq, 128), F32)
            dfq_ref[rows, :] = jnp.zeros((tq, 128), F32)

        @pl.when(qi == ki)
        def _():
            dk_ref[...] = jnp.zeros_like(dk_ref)
            dv_ref[...] = jnp.zeros_like(dv_ref)
            dfk_ref[...] = jnp.zeros_like(dfk_ref)

        def sweep(diagonal):
            first = _lane_halves()
            lane = lax.broadcasted_iota(jnp.int32, (1, 128), 1)
            scale = HEAD_DIM ** -0.5
            q = (q_ref[...] * scale).astype(BF16)
            k = k_ref[...].astype(BF16)
            v = v_ref[...].astype(BF16)
            do_ = do_ref[...]
            do16 = do_.astype(BF16)
            od = o_ref[...] * do_
            aq, ak = aq_ref[...], ak_ref[...]
            halves = (first, jnp.logical_not(first))
            a0, a1 = HEAD_DIM, 0
            dos, vs = [], []
            for lanes, a in zip(halves, (a0, a1)):
                d_hi, d_mid, d_lo = _split3(jnp.sum(jnp.where(lanes, od, 0.0), axis=1, keepdims=True))
                minus_delta = jnp.where(lane == a, -d_hi, jnp.where(lane == a + 1, -d_mid,
                                        jnp.where(lane == a + 2, -d_lo, jnp.zeros((), BF16))))
                dos.append(jnp.where(lanes, do16, minus_delta))
                vs.append(jnp.where(lanes, v, ((lane >= a) & (lane < a + 3)).astype(BF16)))
            s = [_dot_nt(jnp.where(lanes, q, aq), jnp.where(lanes, k, ak)) for lanes in halves]
            dp = [_dot_nt(do_h, v_h) for do_h, v_h in zip(dos, vs)]
            p = [jnp.exp(s_h) for s_h in s]
            if diagonal:
                p = [jnp.where(_causal_mask(tq, tk), p_h, 0.0) for p_h in p]
            ds16 = [(p_h * dp_h).astype(BF16) for p_h, dp_h in zip(p, dp)]
            dv0, dv1 = [lax.dot_general(p_h.astype(BF16), jnp.where(lanes, do16, jnp.zeros((), BF16)), tn,
                                        preferred_element_type=F32) for p_h, lanes in zip(p, halves)]
            dk0, dk1 = [lax.dot_general(ds_h, jnp.where(lanes, q, _lane_one(a)), tn, preferred_element_type=F32)
                        for ds_h, lanes, a in zip(ds16, halves, (a0, a1))]
            dq0, dq1 = [_dot(ds_h, jnp.where(lanes, k, _lane_one(a))) for ds_h, lanes, a in zip(ds16, halves, (a0, a1))]
            dq_ref[rows, :] += jnp.where(first, dq0, dq1) * scale
            dfq_ref[rows, :] += jnp.where(first, dq0[:, a0:a0 + 1], dq1[:, a1:a1 + 1])
            dk_ref[...] += jnp.where(first, dk0, dk1)
            dfk_ref[...] += jnp.where(first, dk0[:, a0:a0 + 1], dk1[:, a1:a1 + 1])
            dv_ref[...] += dv0 + dv1

        @pl.when(qi > ki)
        def _():
            sweep(False)

        @pl.when(qi == ki)
        def _():
            sweep(True)

    qrow = lambda p, s, qt, kt: (qt[s], p)
    krow = lambda p, s, qt, kt: (kt[s], p)
    return pl.pallas_call(
        body, name="attention_bwd",
        out_shape=[jax.ShapeDtypeStruct((t, ATTN_W), F32)] * 5,
        grid_spec=pltpu.PrefetchScalarGridSpec(
            num_scalar_prefetch=2, grid=(4, int(q_tab.shape[0])),
            in_specs=[pl.BlockSpec((tq, 128), qrow),
                      pl.BlockSpec((tk, 128), lambda p, s, qt, kt: (kt[s], 4 + p)),
                      pl.BlockSpec((tk, 128), lambda p, s, qt, kt: (kt[s], 8 + p)),
                      pl.BlockSpec((None, tq, 128), lambda p, s, qt, kt: (p, qt[s], 0)),
                      pl.BlockSpec((None, tk, 128), lambda p, s, qt, kt: (p, kt[s], 0)),
                      pl.BlockSpec((tq, 128), qrow), pl.BlockSpec((tq, 128), qrow)],
            out_specs=[pl.BlockSpec((t, 128), lambda p, s, qt, kt: (0, p)),
                       pl.BlockSpec((tk, 128), krow), pl.BlockSpec((tk, 128), krow), pl.BlockSpec((tk, 128), krow),
                       pl.BlockSpec((t, 128), lambda p, s, qt, kt: (0, p))]),
        compiler_params=_params(("parallel", "arbitrary"), VMEM_BIG),
    )(q_tab, k_tab, z, z, z, aug_q, aug_k, o, do)


def _shifted(prev_rows, x, shift):
    tm = x.shape[0]
    return pltpu.roll(jnp.concatenate([prev_rows, x], axis=0), shift, 0)[8:8 + tm]


def _ahead(x, next_rows, shift):
    tm = x.shape[0]
    return pltpu.roll(jnp.concatenate([x, next_rows], axis=0), tm + 8 - shift, 0)[0:tm]


def _conv_col0(z):
    return (z.shape[1] - F_PAD - 3 * CONV_W) // CONV_W


def _conv_specs(tm, c0):
    cols = (c0, c0 + 1, c0 + 2)
    tiles = [pl.BlockSpec((tm, CONV_W), functools.partial(lambda i, c: (i, c), c=c)) for c in cols]
    halos = [pl.BlockSpec((8, CONV_W), functools.partial(lambda i, c: (jnp.maximum(i * (tm // 8) - 1, 0), c), c=c))
             for c in cols]
    return tiles, halos


def _conv_gate(z, conv_w):
    t = z.shape[0]
    tm = ROW_TILE
    nt = t // tm

    def body(cb_ref, cc_ref, ci_ref, hc_ref, hi_ref, w_ref, g_ref, gt_ref):
        i = pl.program_id(0)
        cc = cc_ref[...] * ci_ref[...]
        prev = jnp.where(i > 0, hc_ref[...] * hi_ref[...], 0.0)
        conv = w_ref[0:1, :] * _shifted(prev, cc, 2) + w_ref[1:2, :] * _shifted(prev, cc, 1) + w_ref[2:3, :] * cc
        g = cb_ref[...] * conv
        g_ref[...] = g.astype(BF16)
        gt_ref[...] = g.T.astype(BF16)

    (cb, cc, ci), (_, hc, hi) = _conv_specs(tm, _conv_col0(z))
    return pl.pallas_call(
        body, name="conv_gate_fwd",
        out_shape=[jax.ShapeDtypeStruct((t, CONV_W), BF16), jax.ShapeDtypeStruct((CONV_W, t), BF16)],
        grid=(nt,),
        in_specs=[cb, cc, ci, hc, hi, pl.BlockSpec((8, CONV_W), lambda i: (0, 0))],
        out_specs=[pl.BlockSpec((tm, CONV_W), lambda i: (i, 0)), pl.BlockSpec((CONV_W, tm), lambda i: (0, i))],
        compiler_params=_params(("parallel",)),
    )(z, z, z, z, z, conv_w)


def _conv_bwd(z, dg, conv_w):
    t = z.shape[0]
    tm = ROW_TILE
    nt = t // tm

    def body(cb_ref, cc_ref, ci_ref, hc_ref, hi_ref, dg_ref, ncb_ref, ndg_ref, w_ref, dz_ref, dw_ref):
        i = pl.program_id(0)

        @pl.when(i == 0)
        def _():
            dw_ref[...] = jnp.zeros_like(dw_ref)

        cb, c_c, c_in = cb_ref[...], cc_ref[...], ci_ref[...]
        cc = c_c * c_in
        prev = jnp.where(i > 0, hc_ref[...] * hi_ref[...], 0.0)
        cc1, cc2 = _shifted(prev, cc, 1), _shifted(prev, cc, 2)
        w0, w1, w2 = w_ref[0:1, :], w_ref[1:2, :], w_ref[2:3, :]
        conv = w0 * cc2 + w1 * cc1 + w2 * cc
        dgv = dg_ref[...]
        dconv = dgv * cb
        nxt = jnp.where(i < nt - 1, ndg_ref[...] * ncb_ref[...], 0.0)
        dcc = w2 * dconv + w1 * _ahead(dconv, nxt, 1) + w0 * _ahead(dconv, nxt, 2)
        dz_ref[:, 0:CONV_W] = (dgv * conv).astype(BF16)
        dz_ref[:, CONV_W:2 * CONV_W] = (dcc * c_in).astype(BF16)
        dz_ref[:, 2 * CONV_W:] = (dcc * c_c).astype(BF16)
        dw_ref[0:1, :] += jnp.sum(dconv * cc2, axis=0, keepdims=True)
        dw_ref[1:2, :] += jnp.sum(dconv * cc1, axis=0, keepdims=True)
        dw_ref[2:3, :] += jnp.sum(dconv * cc, axis=0, keepdims=True)

    c0 = _conv_col0(z)
    (cb, cc, ci), (_, hc, hi) = _conv_specs(tm, c0)
    nxt = lambda i, c: (jnp.minimum((i + 1) * (tm // 8), t // 8 - 1), c)
    return pl.pallas_call(
        body, name="conv_gate_bwd",
        out_shape=[jax.ShapeDtypeStruct((t, 3 * CONV_W), BF16), jax.ShapeDtypeStruct((8, CONV_W), F32)],
        grid=(nt,),
        in_specs=[cb, cc, ci, hc, hi, pl.BlockSpec((tm, CONV_W), lambda i: (i, 0)),
                  pl.BlockSpec((8, CONV_W), lambda i: nxt(i, c0)), pl.BlockSpec((8, CONV_W), lambda i: nxt(i, 0)),
                  pl.BlockSpec((8, CONV_W), lambda i: (0, 0))],
        out_specs=[pl.BlockSpec((tm, 3 * CONV_W), lambda i: (i, 0)), pl.BlockSpec((8, CONV_W), lambda i: (0, 0))],
        compiler_params=_params(("arbitrary",)),
    )(z, z, z, z, z, dg, z, dg, conv_w)


def _branch_mix(z, o, g, w_ab, w_cb, d):
    t = z.shape[0]
    tm = ROW_TILE
    ga_col = 0

    def body(o_ref, g_ref, ga_ref, gc_ref, wa_ref, wc_ref, mp_ref, mpt_ref, ot_ref):
        o_ = o_ref[...]
        ya = _dot(o_.astype(BF16), wa_ref[...])
        yc = _dot(g_ref[...], wc_ref[...])
        mp = _sigmoid(ga_ref[...]) * ya + _sigmoid(gc_ref[...]) * yc
        mp_ref[...] = mp.astype(BF16)
        mpt_ref[...] = mp.T.astype(BF16)
        ot_ref[...] = o_.T.astype(BF16)

    return pl.pallas_call(
        body, name="branch_mix_fwd",
        out_shape=[jax.ShapeDtypeStruct((t, d), BF16), jax.ShapeDtypeStruct((d, t), BF16),
                   jax.ShapeDtypeStruct((ATTN_W, t), BF16)],
        grid=(t // tm,),
        in_specs=[pl.BlockSpec((tm, ATTN_W), lambda i: (i, 0)), pl.BlockSpec((tm, CONV_W), lambda i: (i, 0)),
                  pl.BlockSpec((tm, d), lambda i: (i, ga_col)), pl.BlockSpec((tm, d), lambda i: (i, ga_col + 1)),
                  pl.BlockSpec((ATTN_W, d), lambda i: (0, 0)), pl.BlockSpec((CONV_W, d), lambda i: (0, 0))],
        out_specs=[pl.BlockSpec((tm, d), lambda i: (i, 0)), pl.BlockSpec((d, tm), lambda i: (0, i)),
                   pl.BlockSpec((ATTN_W, tm), lambda i: (0, i))],
        compiler_params=_params(("parallel",), VMEM_BIG),
    )(o, g, z, z, w_ab, w_cb)


def _branch_bwd(z, o, g, dmixed, w_out, w_ab, w_cb, d):
    t = z.shape[0]
    tm = ROW_TILE // 2
    ga_col = 0

    def body(dm_ref, o_ref, g_ref, ga_ref, gc_ref, wo_ref, wa_ref, wc_ref, dya_ref, dyc_ref, dgt_ref, do_ref, dg_ref):
        dmp = _dot_nt(dm_ref[...], wo_ref[...])
        ya = _dot(o_ref[...].astype(BF16), wa_ref[...])
        yc = _dot(g_ref[...], wc_ref[...])
        sa, sc = _sigmoid(ga_ref[...]), _sigmoid(gc_ref[...])
        dya = (dmp * sa).astype(BF16)
        dyc = (dmp * sc).astype(BF16)
        dya_ref[...] = dya
        dyc_ref[...] = dyc
        dgt_ref[:, :d] = (dmp * ya * sa * (1.0 - sa)).astype(BF16)
        dgt_ref[:, d:] = (dmp * yc * sc * (1.0 - sc)).astype(BF16)
        do_ref[...] = _dot_nt(dya, wa_ref[...])
        dg_ref[...] = _dot_nt(dyc, wc_ref[...])

    row = lambda i: (i, 0)
    fixed = lambda i: (0, 0)
    return pl.pallas_call(
        body, name="branch_mix_bwd",
        out_shape=[jax.ShapeDtypeStruct((t, d), BF16), jax.ShapeDtypeStruct((t, d), BF16),
                   jax.ShapeDtypeStruct((t, 2 * d), BF16), jax.ShapeDtypeStruct((t, ATTN_W), F32),
                   jax.ShapeDtypeStruct((t, CONV_W), F32)],
        grid=(t // tm,),
        in_specs=[pl.BlockSpec((tm, d), row), pl.BlockSpec((tm, ATTN_W), row), pl.BlockSpec((tm, CONV_W), row),
                  pl.BlockSpec((tm, d), lambda i: (i, ga_col)), pl.BlockSpec((tm, d), lambda i: (i, ga_col + 1)),
                  pl.BlockSpec((d, d), fixed), pl.BlockSpec((ATTN_W, d), fixed), pl.BlockSpec((CONV_W, d), fixed)],
        out_specs=[pl.BlockSpec((tm, d), row), pl.BlockSpec((tm, d), row), pl.BlockSpec((tm, 2 * d), row),
                   pl.BlockSpec((tm, ATTN_W), row), pl.BlockSpec((tm, CONV_W), row)],
        compiler_params=_params(("parallel",), VMEM_BIG),
    )(dmixed, o, g, z, z, w_out, w_ab, w_cb)


def _loss_grad(h, target_pad):
    t, d = h.shape
    tm = ROW_TILE

    def body(h_ref, t_ref, dy_ref, loss_ref):
        i = pl.program_id(0)

        @pl.when(i == 0)
        def _():
            loss_ref[...] = jnp.zeros_like(loss_ref)

        row = i * tm + lax.broadcasted_iota(jnp.int32, (tm, 1), 0)
        err = jnp.where(row >= N_FRONT, h_ref[...] - t_ref[...], 0.0)
        dy_ref[...] = err * (1.0 / d)
        per_row = jnp.sum(err * err, axis=1, keepdims=True) * (1.0 / d)
        loss_ref[...] += 0.5 * jnp.sum(per_row, axis=0, keepdims=True)

    return pl.pallas_call(
        body, name="loss_and_grad",
        out_shape=[jax.ShapeDtypeStruct((t, d), F32), jax.ShapeDtypeStruct((1, 128), F32)],
        grid=(t // tm,),
        in_specs=[pl.BlockSpec((tm, d), lambda i: (i, 0))] * 2,
        out_specs=[pl.BlockSpec((tm, d), lambda i: (i, 0)), pl.BlockSpec((1, 128), lambda i: (0, 0))],
        compiler_params=_params(("arbitrary",)),
    )(h, target_pad)


def _norm_bwd(name, x, g, dy, alpha):
    t, d = x.shape
    tm = ROW_TILE

    def body(x_ref, g_ref, dy_ref, dx_ref, dg_ref):
        @pl.when(pl.program_id(0) == 0)
        def _():
            dg_ref[...] = jnp.zeros_like(dg_ref)

        dx, dg = _rms_bwd(x_ref[...], g_ref[...], dy_ref[...])
        dx_ref[...] = (alpha * dx).astype(BF16)
        dg_ref[...] += alpha * dg

    row = pl.BlockSpec((tm, d), lambda i: (i, 0))
    vec = pl.BlockSpec((1, d), lambda i: (0, 0))
    return pl.pallas_call(
        body, name=name,
        out_shape=[jax.ShapeDtypeStruct((t, d), BF16), jax.ShapeDtypeStruct((1, d), F32)],
        grid=(t // tm,), in_specs=[row, vec, row], out_specs=[row, vec],
        compiler_params=_params(("arbitrary",)),
    )(x, g, dy)


def _ffn_bwd_mid(name, df, w_out, ab):
    t, d = df.shape
    cw = ab.shape[1] // 4
    tm = ROW_TILE

    def body(df_ref, w_ref, ab_ref, o_ref):
        ds = _dot_nt(df_ref[...], w_ref[...])
        a = ab_ref[:, :cw]
        b = ab_ref[:, cw:]
        sg = _sigmoid(a)
        o_ref[:, :cw] = (ds * b * (sg * (1.0 + a * (1.0 - sg)))).astype(BF16)
        o_ref[:, cw:] = (ds * (a * sg)).astype(BF16)

    return pl.pallas_call(
        body, name=name, out_shape=jax.ShapeDtypeStruct((t, 4 * cw), BF16),
        grid=(2, t // tm),
        in_specs=[pl.BlockSpec((tm, d), lambda j, i: (i, 0)), pl.BlockSpec((cw, d), lambda j, i: (j, 0)),
                  pl.BlockSpec((tm, 2 * cw), lambda j, i: (i, j))],
        out_specs=pl.BlockSpec((tm, 2 * cw), lambda j, i: (i, j)),
        compiler_params=_params(("parallel", "parallel"), VMEM_BIG),
    )(df, w_out, ab)


def _mm_nt_norm_bwd(name, dy, w, h, g, dh_in):
    t, kdim = dy.shape
    d = h.shape[1]
    tm = ROW_TILE // 2
    slots = w.ndim == 3

    def body(dy_ref, w_ref, h_ref, g_ref, dhi_ref, dh_ref, dg_ref):
        @pl.when(pl.program_id(0) == 0)
        def _():
            dg_ref[...] = jnp.zeros_like(dg_ref)

        if slots:
            cw = w_ref.shape[2]
            dn = _dot_nt(dy_ref[:, 0:cw], w_ref[_slot_of(0)])
            for k in range(1, 4):
                dn += _dot_nt(dy_ref[:, k * cw:(k + 1) * cw], w_ref[_slot_of(k)])
        else:
            dn = _dot_nt(dy_ref[...], w_ref[...])
        dx, dg = _rms_bwd(h_ref[...], g_ref[...], dn)
        dh_ref[...] = dhi_ref[...] + dx
        dg_ref[...] += dg

    row = pl.BlockSpec((tm, d), lambda i: (i, 0))
    vec = pl.BlockSpec((1, d), lambda i: (0, 0))
    return pl.pallas_call(
        body, name=name,
        out_shape=[jax.ShapeDtypeStruct((t, d), F32), jax.ShapeDtypeStruct((1, d), F32)],
        grid=(t // tm,),
        in_specs=[pl.BlockSpec((tm, kdim), lambda i: (i, 0)), pl.BlockSpec(w.shape, lambda i: (0,) * w.ndim),
                  row, vec, row],
        out_specs=[row, vec],
        compiler_params=_params(("arbitrary",), VMEM_BIG),
    )(dy, w, h, g, dh_in)


def _gate_bwd(df_pad, z, b_pad, f_col):
    t = z.shape[0]
    tm = ROW_TILE
    nt = t // tm

    def body(d_ref, z_ref, b_ref, dz_ref, db_ref, carry_ref):
        i = pl.program_id(0)

        @pl.when(i == 0)
        def _():
            carry_ref[...] = jnp.zeros_like(carry_ref)
            db_ref[...] = jnp.zeros_like(db_ref)

        tri = (lax.broadcasted_iota(jnp.int32, (tm, tm), 0) <= lax.broadcasted_iota(jnp.int32, (tm, tm), 1))
        tail = jnp.dot(tri.astype(F32), d_ref[...], preferred_element_type=F32, precision=lax.Precision.HIGHEST)
        tail = tail + carry_ref[0:1, :]
        carry_ref[...] = jnp.broadcast_to(tail[0:1, :], carry_ref.shape)
        row = (nt - 1 - i) * tm + lax.broadcasted_iota(jnp.int32, (tm, 1), 0)
        dlogit = jnp.where(row >= ROW_PAD, tail * _sigmoid(-(z_ref[...] + b_ref[...])), 0.0)
        dz_ref[...] = jnp.zeros_like(dz_ref)
        dz_ref[:, 0:128] = dlogit.astype(BF16)
        db_ref[...] += jnp.sum(dlogit, axis=0, keepdims=True)

    rev = lambda i: (nt - 1 - i, 0)
    return pl.pallas_call(
        body, name="forget_gate_bwd",
        out_shape=[jax.ShapeDtypeStruct((t, F_PAD), BF16), jax.ShapeDtypeStruct((1, 128), F32)],
        grid=(nt,),
        in_specs=[pl.BlockSpec((tm, 128), rev), pl.BlockSpec((tm, 128), lambda i: (nt - 1 - i, f_col // 128)),
                  pl.BlockSpec((1, 128), lambda i: (0, 0))],
        out_specs=[pl.BlockSpec((tm, F_PAD), rev), pl.BlockSpec((1, 128), lambda i: (0, 0))],
        scratch_shapes=[pltpu.VMEM((8, 128), F32)],
        compiler_params=_params(("arbitrary",)),
    )(df_pad, z, b_pad)


def _ffn_fwd(tag, n, w_in4, w_out, h, g_post, g_next):
    ab, s, s_t = _ffn_in(f"{tag}_in_fwd", n, w_in4)
    outs = _mm_resid_norm(f"{tag}_out_fwd", s, w_out, h, g_post, 0.5, g_next)
    return ab, s_t, outs


def _ffn_bwd_weights(tag, dh, f, g_post, ab, s_t, n_t, w_in4, w_out):
    d, cw = w_in4.shape[1], w_in4.shape[2]
    t = dh.shape[0]
    df, dg_post = _norm_bwd(f"{tag}_post_norm_bwd", f, g_post, dh, 0.5)
    dw_out = _weight_grad(f"{tag}_dw_out", s_t, df, d, out_rows=cw // 2)
    dab = _ffn_bwd_mid(f"{tag}_mid_bwd", df, w_out, ab)
    bk = _k_tile(t)
    dw_in = _matmul(
        f"{tag}_dw_in", n_t, dab, jax.ShapeDtypeStruct((4, d, cw), F32), (1, 4, t // bk),
        pl.BlockSpec((d, bk), lambda a, b, k: (0, k)), pl.BlockSpec((bk, cw), lambda a, b, k: (k, b)),
        pl.BlockSpec((None, d, cw), lambda a, b, k: (_slot_of(b), 0, 0)), vmem=VMEM_BIG)
    return dab, dg_post, dw_in, dw_out


def _pack_small(meta, conv, gains, b_forget):
    d = gains[0].shape[1]
    rows = [meta.reshape(4, d), jnp.pad(conv.reshape(1, 3 * 128), ((0, 0), (0, d - 3 * 128)))]
    rows += list(gains) + [jnp.pad(b_forget, ((0, 0), (0, d - HEADS)))]
    return jnp.concatenate(rows + [jnp.zeros((4, d), F32)], axis=0)


def _unpack_small(block):
    d = block.shape[1]
    meta = block[0:4].reshape(N_META, d // 4)
    conv = block[4, :3 * 128].reshape(1, 3, 128)
    gains = [block[5 + i:6 + i] for i in range(6)]
    return meta, conv, gains, block[11:12, :HEADS]


def kernel(x, meta_tokens, w_in, b_forget, conv_w, w_attn_branch, w_conv_branch, w_out, g_ffn1_pre, g_ffn1_post, w_ffn1_in, w_ffn1_out, g_mix_pre, g_mix_post, g_ffn2_pre, g_ffn2_post, w_ffn2_in, w_ffn2_out, loss_target, m_meta_tokens, m_w_in, m_b_forget, m_conv_w, m_w_attn_branch, m_w_conv_branch, m_w_out, m_g_ffn1_pre, m_g_ffn1_post, m_w_ffn1_in, m_w_ffn1_out, m_g_mix_pre, m_g_mix_post, m_g_ffn2_pre, m_g_ffn2_post, m_w_ffn2_in, m_w_ffn2_out, v_meta_tokens, v_w_in, v_b_forget, v_conv_w, v_w_attn_branch, v_w_conv_branch, v_w_out, v_g_ffn1_pre, v_g_ffn1_post, v_w_ffn1_in, v_w_ffn1_out, v_g_mix_pre, v_g_mix_post, v_g_ffn2_pre, v_g_ffn2_post, v_w_ffn2_in, v_w_ffn2_out):
    seq, d = x.shape[1], x.shape[2]
    t = seq + N_FRONT
    n_main = 3 * ATTN_W + 3 * CONV_W + 2 * d
    nz = n_main + F_PAD
    f_lo = 3 * ATTN_W
    c_arr = lax.axis_index("c").astype(jnp.int32).reshape(1)

    big = [w_in[0], w_attn_branch[0], w_conv_branch[0], w_out[0], w_ffn1_in[0], w_ffn1_out[0], w_ffn2_in[0], w_ffn2_out[0]]
    small_gather = jnp.concatenate(
        [meta_tokens.reshape(4, d), jnp.pad(conv_w.reshape(1, 3 * 128), ((0, 0), (0, d - 3 * 128))),
         jnp.zeros((11, d), F32)], axis=0)
    w_f1_in4, w_f1_out4, small4 = _all_gather([big[4].astype(BF16), big[5].astype(BF16), small_gather])
    rest, small4 = lax.optimization_barrier(([big[i].astype(BF16) for i in (0, 1, 2, 3, 6, 7)], small4))
    rest_gathered = _all_gather_async(rest)
    w_f1_out = w_f1_out4.reshape(-1, d)
    meta_full = jnp.transpose(small4[:, 0:4].reshape(4, N_META, d // 4), (1, 0, 2)).reshape(N_META, d)
    conv_full = jnp.transpose(small4[:, 4, :3 * 128].reshape(4, 3, 128), (1, 0, 2)).reshape(3, CONV_W)
    conv_pad = jnp.pad(conv_full, ((0, 5), (0, 0)))
    b_pad = jnp.pad(b_forget, ((0, 0), (0, 128 - HEADS)))

    h0 = jnp.concatenate([jnp.zeros((ROW_PAD, d), F32), meta_full, x[0]], axis=0)
    target_pad = jnp.concatenate([jnp.zeros((N_FRONT, d), F32), loss_target[0]], axis=0)
    n1, n1_t = _norm_fwd("ffn1_pre_norm", h0, g_ffn1_pre)
    ab1, s1_t, (f1, h1, u, u_t) = _ffn_fwd("ffn1", n1, w_f1_in4, w_f1_out, h0, g_ffn1_post, g_mix_pre)

    w_in4, w_ab4, w_cb4, w_out4, w_f2_in4, w_f2_out4 = rest_gathered(u)
    w_in_full = jnp.transpose(w_in4, (1, 0, 2)).reshape(d, 4 * w_in4.shape[2])
    g_lo = f_lo + HEADS + 3 * CONV_W
    w_in_pad = jnp.concatenate(
        [w_in_full[:, :f_lo], w_in_full[:, g_lo:], w_in_full[:, f_lo + HEADS:g_lo], w_in_full[:, f_lo:f_lo + HEADS],
         jnp.zeros((d, F_PAD - HEADS), BF16)], axis=1)
    w_ab = jnp.transpose(w_ab4, (1, 0, 2)).reshape(ATTN_W, d)
    w_cb = jnp.transpose(w_cb4, (1, 0, 2)).reshape(CONV_W, d)
    w_out_full = w_out4.reshape(d, d)
    w_f2_out = w_f2_out4.reshape(-1, d)
    qkv, z = _in_proj(u, w_in_pad)
    f_col = z.shape[1] - F_PAD
    f_cum = _gate_prep(z, b_pad, f_col)
    f_heads = f_cum[:, :HEADS]
    o, lse = _attn_fwd(qkv, *_attn_bias_operands(f_heads))
    g, g_t = _conv_gate(z, conv_pad)
    mp, mp_t, o_t = _branch_mix(z, o, g, w_ab, w_cb, d)
    mixed, h2, n2, n2_t = _mm_resid_norm("mix_out_fwd", mp, w_out_full, h1, g_mix_post, 1.0, g_ffn2_pre)
    ab2, s2_t, (f2, h3) = _ffn_fwd("ffn2", n2, w_f2_in4, w_f2_out, h2, g_ffn2_post, None)
    dh3, loss_part = _loss_grad(h3, target_pad)
    loss = lax.psum(loss_part[0, 0], ("x", "y", "c"))

    reduced = {}

    def reduce_scatter(label, tags, slots, sequencer_id, hold=None):
        got = _pair_send_halves(f"grad_pair_exchange_{label}", slots)
        sums = [_pair_add(tag, s, a, c_arr, F32 if tag == "small" else BF16) for tag, s, a in zip(tags, slots, got)]
        sums, hold = lax.optimization_barrier((sums, hold))
        if sequencer_id is None:
            arrived = _chip_scatter(f"grad_chip_scatter_{label}", sums)
        else:
            arrived = _chip_scatter_async(f"grad_chip_scatter_{label}", sums, sequencer_id)
        mine = [_chip_add(tag, a) for tag, a in zip(tags, arrived)]
        reduced.update(zip(tags, zip(mine, _pair_swap(f"grad_pair_swap_{label}", mine))))
        return hold

    dab2, dg_f2_post, dw_f2_in, dw_f2_out = _ffn_bwd_weights(
        "ffn2", dh3, f2, g_ffn2_post, ab2, s2_t, n2_t, w_f2_in4, w_f2_out)
    dab2 = reduce_scatter("ffn2", ["w_ffn2_in", "w_ffn2_out"], [dw_f2_in, dw_f2_out.reshape(4, -1, d)], 2, dab2)
    dh2, dg_f2_pre = _mm_nt_norm_bwd("ffn2_in_bwd", dab2, w_f2_in4, h2, g_ffn2_pre, dh3)
    dmixed, dg_mix_post = _norm_bwd("mix_post_norm_bwd", mixed, g_mix_post, dh2, 1.0)
    dw_out = _weight_grad("mix_dw_out", mp_t, dmixed, d)
    dya, dyc, dgates, do, dgconv = _branch_bwd(z, o, g, dmixed, w_out_full, w_ab, w_cb, d)
    dw_ab = _weight_grad("mix_dw_attn_branch", o_t, dya, d)
    dw_cb = _weight_grad("mix_dw_conv_branch", g_t, dyc, d)
    dz_conv, dconv_w = _conv_bwd(z, dgconv, conv_pad)
    front = lax.broadcasted_iota(jnp.int32, (t, 1), 0) < ROW_PAD
    lse_heads = jnp.where(front, 1e9, lse[:, ::HEAD_DIM])
    dq, dk, dv, dfk, dfq = _attn_bwd(qkv, *_attn_bias_operands(f_heads, lse_heads), o, do)
    df_pad = jnp.pad((dfq - dfk)[:, ::HEAD_DIM], ((0, 0), (0, 128 - HEADS)))
    dz_f, db_forget = _gate_bwd(df_pad, z, b_pad, f_col)
    dz = jnp.concatenate([dq.astype(BF16), dk.astype(BF16), dv.astype(BF16), dgates, dz_conv, dz_f], axis=1)
    dh1, dg_mix_pre = _mm_nt_norm_bwd("mix_in_bwd", dz, w_in_pad, h1, g_mix_pre, dh2)
    dw_in_pad = _weight_grad("mix_dw_in", u_t, dz, 512)
    cs = w_in4.shape[2]
    c_lo = f_lo + 2 * d
    dw_in_full = jnp.concatenate(
        [dw_in_pad[:, :f_lo], dw_in_pad[:, n_main:n_main + HEADS], dw_in_pad[:, c_lo:n_main], dw_in_pad[:, f_lo:c_lo]],
        axis=1)
    reduce_scatter(
        "mix", ["w_in", "w_attn_branch", "w_conv_branch", "w_out"],
        [jnp.transpose(dw_in_full.reshape(d, 4, cs), (1, 0, 2)),
         jnp.transpose(dw_ab.reshape(ATTN_W, 4, d // 4), (1, 0, 2)),
         jnp.transpose(dw_cb.reshape(CONV_W, 4, d // 4), (1, 0, 2)),
         dw_out.reshape(4, d // 4, d)], 3)
    dab1, dg_f1_post, dw_f1_in, dw_f1_out = _ffn_bwd_weights(
        "ffn1", dh1, f1, g_ffn1_post, ab1, s1_t, n1_t, w_f1_in4, w_f1_out)
    dab1 = reduce_scatter("ffn1", ["w_ffn1_in", "w_ffn1_out"], [dw_f1_in, dw_f1_out.reshape(4, -1, d)], 4, dab1)
    dh0, dg_f1_pre = _mm_nt_norm_bwd("ffn1_in_bwd", dab1, w_f1_in4, h0, g_ffn1_pre, dh1)
    grad_x = dh0[N_FRONT:][None]
    dmeta = dh0[ROW_PAD:N_FRONT]
    small_grad = jnp.stack([
        _pack_small(dmeta[:, j * (d // 4):(j + 1) * (d // 4)], dconv_w[:3, j * 128:(j + 1) * 128],
                    [dg_f1_pre, dg_f1_post, dg_mix_pre, dg_mix_post, dg_f2_pre, dg_f2_post], db_forget[:, :HEADS])
        for j in range(4)])
    reduce_scatter("small", ["small"], [small_grad], None)
    tags =["w_in", "w_attn_branch", "w_conv_branch", "w_out", "w_ffn1_in", "w_ffn1_out", "w_ffn2_in", "w_ffn2_out", "small"]
    halves = [reduced[tag][0] for tag in tags]
    others = [reduced[tag][1] for tag in tags]

    small = [g_ffn1_pre, g_ffn1_post, g_mix_pre, g_mix_post, g_ffn2_pre, g_ffn2_post]
    small_m = [m_g_ffn1_pre, m_g_ffn1_post, m_g_mix_pre, m_g_mix_post, m_g_ffn2_pre, m_g_ffn2_post]
    small_v = [v_g_ffn1_pre, v_g_ffn1_post, v_g_mix_pre, v_g_mix_post, v_g_ffn2_pre, v_g_ffn2_post]
    ws = big + [_pack_small(meta_tokens, conv_w[0], small, b_forget)]
    ms = [m_w_in[0], m_w_attn_branch[0], m_w_conv_branch[0], m_w_out[0], m_w_ffn1_in[0], m_w_ffn1_out[0],
          m_w_ffn2_in[0], m_w_ffn2_out[0], _pack_small(m_meta_tokens, m_conv_w[0], small_m, m_b_forget)]
    vs = [v_w_in[0], v_w_attn_branch[0], v_w_conv_branch[0], v_w_out[0], v_w_ffn1_in[0], v_w_ffn1_out[0],
          v_w_ffn2_in[0], v_w_ffn2_out[0], _pack_small(v_meta_tokens, v_conv_w[0], small_v, v_b_forget)]
    updates = [_adamw(tag, w, a, b, m, v, c_arr) for tag, w, a, b, m, v in zip(tags, ws, halves, others, ms, vs)]

    def leaves(big_vals, small_block):
        meta, conv, gains, bf = _unpack_small(small_block)
        w_in_, w_ab_, w_cb_, w_out_, f1_in, f1_out, f2_in, f2_out = [b[None] for b in big_vals]
        return [meta, w_in_, bf, conv, w_ab_, w_cb_, w_out_, gains[0], gains[1], f1_in, f1_out,
                gains[2], gains[3], gains[4], gains[5], f2_in, f2_out]

    out_g, out_d, out_m, out_v = [leaves([u_[k] for u_ in updates[:8]], updates[8][k]) for k in range(4)]
    return (loss, grad_x, *out_g, *out_d, *out_m, *out_v)
```

```python
import functools

import jax
import jax.numpy as jnp
from jax import lax
from jax.experimental import pallas as pl
from jax.experimental.pallas import tpu as pltpu
from jax.experimental.pallas import tpu_sc as plsc

N_META = 16
ROW_PAD = 112
N_FRONT = ROW_PAD + N_META
HEADS = 8
HEAD_DIM = 64
ATTN_W = HEADS * HEAD_DIM
CONV_W = 512
NORM_EPS = 1e-6
ROW_TILE = 640
F_PAD = 128
ATTN_ROW_PARTS = 1
NEG = -1e30
ADAM_LR = 0.001
ADAM_B1 = 0.9
ADAM_B2 = 0.999
ADAM_EPS = 1e-08
ADAM_WD = 0.01
ADAM_STEP = 10
VMEM_BIG = 56 * 1024 * 1024
MESH = pl.DeviceIdType.MESH
ANY = pl.BlockSpec(memory_space=pl.ANY)
F32 = jnp.float32
BF16 = jnp.bfloat16


def _params(sem, vmem=None):
    return pltpu.CompilerParams(dimension_semantics=sem, vmem_limit_bytes=vmem)


def _sigmoid(x):
    return 1.0 / (1.0 + jnp.exp(-x))


def _rstd(x):
    return lax.rsqrt(jnp.mean(x * x, axis=-1, keepdims=True) + NORM_EPS)


def _rms_bwd(x, g, dy):
    r = _rstd(x)
    xr = x * r
    gdy = g * dy
    dx = r * (gdy - xr * jnp.mean(xr * gdy, axis=-1, keepdims=True))
    return dx, jnp.sum(dy * xr, axis=0, keepdims=True)


def _dot(a, b):
    return jnp.dot(a, b, preferred_element_type=F32)


def _dot_nt(a, b):
    return lax.dot_general(a, b, (((1,), (1,)), ((), ())), preferred_element_type=F32)


def _k_tile(t):
    return 1664 if t % 1664 == 0 else ROW_TILE


def _place():
    x, y, c = lax.axis_index("x"), lax.axis_index("y"), lax.axis_index("c")
    chips = [(1 - x, y), (x, 1 - y), (1 - x, 1 - y)]
    return x, y, c, chips


def _all_gather(shards):
    n = len(shards)
    split = [s.reshape(2, s.shape[0] // 2, s.shape[1]) for s in shards]

    def body(*refs):
        ins, outs = refs[:n], refs[n:2 * n]
        send_sems, recv_sems = refs[2 * n:]
        x, y, c, chips = _place()
        me = 2 * x + y
        sibling = (x, y, 1 - c)

        def remote(i, k, slot, part, to, src=None):
            dst = outs[i].at[slot, part]
            return pltpu.make_async_remote_copy(
                src_ref=dst if src is None else src, dst_ref=dst,
                send_sem=send_sems.at[i, k], recv_sem=recv_sems.at[i, k],
                device_id=to, device_id_type=MESH)

        started = []
        for i in range(n):
            for k, (cx, cy) in enumerate(chips):
                cp = remote(i, k, me, c, (cx, cy, c), src=ins[i].at[c])
                cp.start()
                started.append(cp)
        for i in range(n):
            for k, (cx, cy) in enumerate(chips):
                remote(i, k, 2 * cx + cy, c, (x, y, c)).wait_recv()
                cp = remote(i, 3 + k, 2 * cx + cy, c, sibling)
                cp.start()
                started.append(cp)
        for i in range(n):
            for k, (cx, cy) in enumerate(chips):
                remote(i, 3 + k, 2 * cx + cy, 1 - c, (x, y, c)).wait_recv()
        for cp in started:
            cp.wait_send()

    outs = pl.pallas_call(
        body, name="all_gather_weights",
        out_shape=[jax.ShapeDtypeStruct((4,) + s.shape, s.dtype) for s in split],
        in_specs=[ANY] * n, out_specs=[ANY] * n,
        scratch_shapes=[pltpu.SemaphoreType.DMA((n, 6)), pltpu.SemaphoreType.DMA((n, 6))],
    )(*split)
    me =2 * lax.axis_index("x") + lax.axis_index("y")
    outs = [lax.dynamic_update_slice(o, s[None], (me, 0, 0, 0)) for o, s in zip(outs, split)]
    return [o.reshape((4,) + s.shape) for o, s in zip(outs, shards)]


def _all_gather_async(shards):
    n = len(shards)
    split = [s.reshape(2, s.shape[0] // 2, s.shape[1]) for s in shards]
    ins = [jax.new_ref(s, memory_space=pltpu.MemorySpace.HBM) for s in split]
    outs = [jax.empty_ref(jax.ShapeDtypeStruct((4,) + s.shape, s.dtype), memory_space=pltpu.MemorySpace.HBM)
            for s in split]

    @pl.kernel(mesh=plsc.ScalarSubcoreMesh(axis_name="sequencer", num_cores=1), name="all_gather_rest",
               scratch_types=(pltpu.SemaphoreType.DMA((n, 6)), pltpu.SemaphoreType.DMA((n, 6))),
               compiler_params=pltpu.CompilerParams(collective_id=1))
    def launch(send_sems, recv_sems):
        x, y, c, chips = _place()
        me = 2 * x + y
        sibling = (x, y, 1 - c)
        barrier = pltpu.get_barrier_semaphore()
        for peer in [(cx, cy, c) for cx, cy in chips] + [sibling]:
            pl.semaphore_signal(barrier, inc=1, device_id=peer, device_id_type=MESH)
        pl.semaphore_wait(barrier, 4)

        def remote(i, k, slot, part, to, src=None):
            dst = outs[i].at[slot, part]
            return pltpu.make_async_remote_copy(
                src_ref=dst if src is None else src, dst_ref=dst,
                send_sem=send_sems.at[i, k], recv_sem=recv_sems.at[i, k],
                device_id=to, device_id_type=MESH)

        started = []
        for i in range(n):
            for k, (cx, cy) in enumerate(chips):
                cp = remote(i, k, me, c, (cx, cy, c), src=ins[i].at[c])
                cp.start()
                started.append(cp)
        for i in range(n):
            for k, (cx, cy) in enumerate(chips):
                remote(i, k, 2 * cx + cy, c, (x, y, c)).wait_recv()
                cp = remote(i, 3 + k, 2 * cx + cy, c, sibling)
                cp.start()
                started.append(cp)
        for i in range(n):
            for k, (cx, cy) in enumerate(chips):
                remote(i, 3 + k, 2 * cx + cy, 1 - c, (x, y, c)).wait_recv()
        for cp in started:
            cp.wait_send()

    launch()
    raw = [o[...] for o in outs]

    def finish(after):
        arrived, _ = lax.optimization_barrier((raw, after))
        me = 2 * lax.axis_index("x") + lax.axis_index("y")
        gathered = [lax.dynamic_update_slice(a, s[None], (me, 0, 0, 0)) for a, s in zip(arrived, split)]
        return [g.reshape((4,) + s.shape) for g, s in zip(gathered, shards)]

    return finish


def _pair_send_halves(name, grads):
    n = len(grads)

    def body(*refs):
        ins, outs = refs[:n], refs[n:2 * n]
        send_sems, recv_sems = refs[2 * n:]
        x, y, c, _ = _place()
        cps = []
        for i in range(n):
            half = ins[i].shape[1] // 2
            cp = pltpu.make_async_remote_copy(
                src_ref=ins[i].at[:, pl.ds((1 - c) * half, half)], dst_ref=outs[i],
                send_sem=send_sems.at[i], recv_sem=recv_sems.at[i],
                device_id=(x, y, 1 - c), device_id_type=MESH)
            cp.start()
            cps.append(cp)
        for cp in cps:
            cp.wait()

    return pl.pallas_call(
        body, name=name,
        out_shape=[jax.ShapeDtypeStruct((4, g.shape[1] // 2, g.shape[2]), g.dtype) for g in grads],
        in_specs=[ANY] * n, out_specs=[ANY] * n,
        scratch_shapes=[pltpu.SemaphoreType.DMA((n,)), pltpu.SemaphoreType.DMA((n,))],
    )(*grads)


def _chip_scatter(name, parts):
    n = len(parts)

    def body(*refs):
        _scatter_copies(refs[:n], refs[n:2 * n], *refs[2 * n:])

    arrived = pl.pallas_call(
        body, name=name,
        out_shape=[jax.ShapeDtypeStruct(p.shape, p.dtype) for p in parts],
        in_specs=[ANY] * n, out_specs=[ANY] * n,
        scratch_shapes=[pltpu.SemaphoreType.DMA((n, 3)), pltpu.SemaphoreType.DMA((n, 3))],
    )(*parts)
    return _own_slots(parts, arrived)


def _scatter_copies(ins, outs, send_sems, recv_sems):
    x, y, c, chips = _place()
    me = 2 * x + y
    sends = []
    for i in range(len(ins)):
        for k, (cx, cy) in enumerate(chips):
            cp = pltpu.make_async_remote_copy(
                src_ref=ins[i].at[2 * cx + cy], dst_ref=outs[i].at[me],
                send_sem=send_sems.at[i, k], recv_sem=recv_sems.at[i, k],
                device_id=(cx, cy, c), device_id_type=MESH)
            cp.start()
            sends.append(cp)
    for i in range(len(ins)):
        for k, (cx, cy) in enumerate(chips):
            got = outs[i].at[2 * cx + cy]
            pltpu.make_async_remote_copy(
                src_ref=got, dst_ref=got, send_sem=send_sems.at[i, k], recv_sem=recv_sems.at[i, k],
                device_id=(x, y, c), device_id_type=MESH).wait_recv()
    for cp in sends:
        cp.wait_send()


def _own_slots(parts, arrived):
    me = 2 * lax.axis_index("x") + lax.axis_index("y")
    return [lax.dynamic_update_slice(a, lax.dynamic_slice_in_dim(p, me, 1, axis=0), (me, 0, 0))
            for p, a in zip(parts, arrived)]


def _chip_scatter_async(name, parts, collective_id):
    n = len(parts)
    ins = [jax.new_ref(p, memory_space=pltpu.MemorySpace.HBM) for p in parts]
    outs = [jax.empty_ref(jax.ShapeDtypeStruct(p.shape, p.dtype), memory_space=pltpu.MemorySpace.HBM) for p in parts]

    @pl.kernel(mesh=plsc.ScalarSubcoreMesh(axis_name="sequencer", num_cores=1), name=name,
               scratch_types=(pltpu.SemaphoreType.DMA((n, 3)), pltpu.SemaphoreType.DMA((n, 3))),
               compiler_params=pltpu.CompilerParams(collective_id=collective_id))
    def launch(send_sems, recv_sems):
        x, y, c, chips = _place()
        barrier = pltpu.get_barrier_semaphore()
        for cx, cy in chips:
            pl.semaphore_signal(barrier, inc=1, device_id=(cx, cy, c), device_id_type=MESH)
        pl.semaphore_wait(barrier, 3)
        _scatter_copies(ins, outs, send_sems, recv_sems)

    launch()
    return _own_slots(parts, [o[...] for o in outs])


def _pair_swap(name, halves):
    n = len(halves)

    def body(*refs):
        ins, outs = refs[:n], refs[n:2 * n]
        send_sems, recv_sems = refs[2 * n:]
        x, y, c, _ = _place()
        cps = []
        for i in range(n):
            cp = pltpu.make_async_remote_copy(
                src_ref=ins[i], dst_ref=outs[i], send_sem=send_sems.at[i], recv_sem=recv_sems.at[i],
                device_id=(x, y, 1 - c), device_id_type=MESH)
            cp.start()
            cps.append(cp)
        for cp in cps:
            cp.wait()

    return pl.pallas_call(
        body, name=name,
        out_shape=[jax.ShapeDtypeStruct(h.shape, h.dtype) for h in halves],
        in_specs=[ANY] * n, out_specs=[ANY] * n,
        scratch_shapes=[pltpu.SemaphoreType.DMA((n,)), pltpu.SemaphoreType.DMA((n,))],
    )(*halves)


def _row_block(rows, cols, n_bufs, budget=20 * 1024 * 1024):
    best = min(rows, 16)
    for b in range(16, rows + 1, 16):
        if rows % b == 0 and 2 * n_bufs * b * cols * 4 <= budget:
            best = b
    return best


def _pair_add(tag, grad, got, c_arr, out_dtype):
    _, rows, cols = grad.shape
    half = rows // 2
    bh = _row_block(half, cols, 3)
    nb = half // bh

    def body(c_ref, g_ref, a_ref, o_ref):
        o_ref[...] = (g_ref[...] + a_ref[...]).astype(out_dtype)

    return pl.pallas_call(
        body, name=f"pair_add_{tag}",
        out_shape=jax.ShapeDtypeStruct((4, half, cols), out_dtype),
        grid_spec=pltpu.PrefetchScalarGridSpec(
            num_scalar_prefetch=1, grid=(4, nb),
            in_specs=[pl.BlockSpec((None, bh, cols), lambda j, r, c: (j, c[0] * nb + r, 0)),
                      pl.BlockSpec((None, bh, cols), lambda j, r, c: (j, r, 0))],
            out_specs=pl.BlockSpec((None, bh, cols), lambda j, r, c: (j, r, 0))),
        compiler_params=_params(("parallel", "parallel")),
    )(c_arr, grad, got)


def _chip_add(tag, parts):
    _, half, cols = parts.shape
    bh = _row_block(half, cols, 5)

    def body(p_ref, o_ref):
        a, b, c, d = [p_ref[j].astype(F32) for j in range(4)]
        o_ref[...] = ((a + b) + c) + d

    return pl.pallas_call(
        body, name=f"chip_add_{tag}",
        out_shape=jax.ShapeDtypeStruct((half, cols), F32),
        grid=(half // bh,),
        in_specs=[pl.BlockSpec((4, bh, cols), lambda r: (0, r, 0))],
        out_specs=pl.BlockSpec((bh, cols), lambda r: (r, 0)),
        compiler_params=_params(("parallel",)),
    )(parts)


def _adamw(tag, w, mine, theirs, m, v, c_arr):
    rows, cols = w.shape
    half = rows // 2
    br = _row_block(half, cols, 9)
    nb = half // br

    def body(c_ref, w_ref, a_ref, b_ref, m_ref, v_ref, g_ref, d_ref, mo_ref, vo_ref):
        own = (pl.program_id(0) // nb) == c_ref[0]
        g = jnp.where(own, a_ref[...], b_ref[...])
        g_ref[...] = g
        m_new = ADAM_B1 * m_ref[...] + (1.0 - ADAM_B1) * g
        v_new = ADAM_B2 * v_ref[...] + (1.0 - ADAM_B2) * (g * g)
        m_hat = m_new / (1.0 - ADAM_B1 ** ADAM_STEP)
        v_hat = v_new / (1.0 - ADAM_B2 ** ADAM_STEP)
        d_ref[...] = -ADAM_LR * (m_hat / (jnp.sqrt(v_hat) + ADAM_EPS) + ADAM_WD * w_ref[...])
        mo_ref[...] = m_new
        vo_ref[...] = v_new

    spec = pl.BlockSpec((br, cols), lambda r, c: (r, 0))
    mine_spec = pl.BlockSpec((br, cols), lambda r, c: (jnp.clip(r - c[0] * nb, 0, nb - 1), 0))
    theirs_spec = pl.BlockSpec((br, cols), lambda r, c: (jnp.clip(r - (1 - c[0]) * nb, 0, nb - 1), 0))
    return pl.pallas_call(
        body, name=f"adamw_{tag}",
        out_shape=[jax.ShapeDtypeStruct((rows, cols), F32)] * 4,
        grid_spec=pltpu.PrefetchScalarGridSpec(
            num_scalar_prefetch=1, grid=(rows // br,),
            in_specs=[spec, mine_spec, theirs_spec, spec, spec], out_specs=[spec] * 4),
        compiler_params=_params(("arbitrary",)),
    )(c_arr, w, mine, theirs, m, v)


def _matmul(name, x, w, out_shape, grid, x_spec, w_spec, o_spec, *, nt=False, vmem=None):
    nk = grid[2]
    acc_shape = tuple(d for d in o_spec.block_shape if d is not None)

    def body(x_ref, w_ref, o_ref, acc_ref):
        k = pl.program_id(2)
        part = _dot_nt(x_ref[...], w_ref[...]) if nt else _dot(x_ref[...], w_ref[...])
        if nk == 1:
            o_ref[...] = part.astype(o_ref.dtype)
        else:
            @pl.when(k == 0)
            def _():
                acc_ref[...] = part

            @pl.when(k > 0)
            def _():
                acc_ref[...] += part

            @pl.when(k == nk - 1)
            def _():
                o_ref[...] = acc_ref[...].astype(o_ref.dtype)

    return pl.pallas_call(
        body, name=name, out_shape=out_shape, grid=grid,
        in_specs=[x_spec, w_spec], out_specs=o_spec,
        scratch_shapes=[pltpu.VMEM(acc_shape if nk > 1 else (8, 128), F32)],
        compiler_params=_params(("parallel", "parallel", "arbitrary"), vmem),
    )(x, w)


def _weight_grad(name, xt, dy, bn, out_rows=None):
    m, t = xt.shape
    n = dy.shape[1]
    bm = m if out_rows is None else out_rows
    bk = _k_tile(t)
    return _matmul(
        name, xt, dy, jax.ShapeDtypeStruct((m, n), F32), (m // bm, n // bn, t // bk),
        pl.BlockSpec((bm, bk), lambda a, b, k: (a, k)),
        pl.BlockSpec((bk, bn), lambda a, b, k: (k, b)),
        pl.BlockSpec((bm, bn), lambda a, b, k: (a, b)), vmem=VMEM_BIG)


def _weight_grad_t(name, xt, dy):
    m, t = xt.shape
    n = dy.shape[1]
    bn = min(n, 512)
    bk = _k_tile(t)
    nk = t // bk

    def body(x_ref, dy_ref, o_ref, acc_ref):
        k = pl.program_id(1)
        part = _dot(x_ref[...], dy_ref[...].astype(BF16))

        @pl.when(k == 0)
        def _():
            acc_ref[...] = part

        @pl.when(k > 0)
        def _():
            acc_ref[...] += part

        @pl.when(k == nk - 1)
        def _():
            o_ref[...] = acc_ref[...].T

    return pl.pallas_call(
        body, name=name, out_shape=jax.ShapeDtypeStruct((n, m), F32), grid=(n // bn, nk),
        in_specs=[pl.BlockSpec((m, bk), lambda b, k: (0, k)), pl.BlockSpec((bk, bn), lambda b, k: (k, b))],
        out_specs=pl.BlockSpec((bn, m), lambda b, k: (b, 0)),
        scratch_shapes=[pltpu.VMEM((m, bn), F32)],
        compiler_params=_params(("parallel", "arbitrary"), VMEM_BIG),
    )(xt, dy)


def _mix_in_bwd(pieces, wt, h, g, dh_in):
    t, d = h.shape
    tm = ROW_TILE // 2
    widths = [p.shape[1] for p in pieces]
    n = len(pieces)

    def body(*refs):
        dy_refs, (w_ref, h_ref, g_ref, dhi_ref, dh_ref, dg_ref) = refs[:n], refs[n:]

        @pl.when(pl.program_id(0) == 0)
        def _():
            dg_ref[...] = jnp.zeros_like(dg_ref)

        dn, off = None, 0
        for dy_ref, wd in zip(dy_refs, widths):
            part = _dot(dy_ref[...].astype(BF16), w_ref[off:off + wd, :])
            dn = part if dn is None else dn + part
            off += wd
        dx, dg = _rms_bwd(h_ref[...], g_ref[...], dn)
        dh_ref[...] = dhi_ref[...] + dx
        dg_ref[...] += dg

    row = pl.BlockSpec((tm, d), lambda i: (i, 0))
    vec = pl.BlockSpec((1, d), lambda i: (0, 0))
    return pl.pallas_call(
        body, name="mix_in_bwd",
        out_shape=[jax.ShapeDtypeStruct((t, d), F32), jax.ShapeDtypeStruct((1, d), F32)],
        grid=(t // tm,),
        in_specs=[pl.BlockSpec((tm, wd), lambda i: (i, 0)) for wd in widths]
        + [pl.BlockSpec(wt.shape, lambda i: (0, 0)), row, vec, row],
        out_specs=[row, vec],
        compiler_params=_params(("arbitrary",), VMEM_BIG),
    )(*pieces, wt, h, g, dh_in)


def _norm_fwd(name, h, g):
    t, d = h.shape
    tm = ROW_TILE

    def body(h_ref, g_ref, n_ref, nt_ref):
        x = h_ref[...]
        y = x * _rstd(x) * g_ref[...]
        n_ref[...] = y.astype(BF16)
        nt_ref[...] = y.T.astype(BF16)

    return pl.pallas_call(
        body, name=name,
        out_shape=[jax.ShapeDtypeStruct((t, d), BF16), jax.ShapeDtypeStruct((d, t), BF16)],
        grid=(t // tm,),
        in_specs=[pl.BlockSpec((tm, d), lambda i: (i, 0)), pl.BlockSpec((1, d), lambda i: (0, 0))],
        out_specs=[pl.BlockSpec((tm, d), lambda i: (i, 0)), pl.BlockSpec((d, tm), lambda i: (0, i))],
        compiler_params=_params(("parallel",)),
    )(h, g)


def _slot_of(kk):
    return (kk % 2) * 2 + kk // 2


def _ffn_in(name, n, w4):
    t, d = n.shape
    cw = w4.shape[2]
    tm = ROW_TILE

    def body(x_ref, wg_ref, wu_ref, ab_ref, s_ref, st_ref):
        x = x_ref[...]
        a = _dot(x, wg_ref[...])
        b = _dot(x, wu_ref[...])
        ab_ref[:, :cw] = a
        ab_ref[:, cw:] = b
        s = a * _sigmoid(a) * b
        s_ref[...] = s.astype(BF16)
        st_ref[...] = s.T.astype(BF16)

    return pl.pallas_call(
        body, name=name,
        out_shape=[jax.ShapeDtypeStruct((t, 4 * cw), F32), jax.ShapeDtypeStruct((t, 2 * cw), BF16),
                   jax.ShapeDtypeStruct((2 * cw, t), BF16)],
        grid=(2, t // tm),
        in_specs=[pl.BlockSpec((tm, d), lambda j, i: (i, 0)),
                  pl.BlockSpec((None, d, cw), lambda j, i: (j, 0, 0)),
                  pl.BlockSpec((None, d, cw), lambda j, i: (2 + j, 0, 0))],
        out_specs=[pl.BlockSpec((tm, 2 * cw), lambda j, i: (i, j)),
                   pl.BlockSpec((tm, cw), lambda j, i: (i, j)),
                   pl.BlockSpec((cw, tm), lambda j, i: (j, i))],
        compiler_params=_params(("parallel", "parallel"), VMEM_BIG),
    )(n, w4, w4)


def _mm_resid_norm(name, x, w, h, g_post, alpha, g_next):
    t, kdim = x.shape
    d = w.shape[1]
    tm = ROW_TILE
    with_next = g_next is not None

    def body(x_ref, w_ref, h_ref, gp_ref, gn_ref, f_ref, hn_ref, *rest):
        f = _dot(x_ref[...], w_ref[...])
        f_ref[...] = f
        hn = h_ref[...] + alpha * (f * _rstd(f) * gp_ref[...])
        hn_ref[...] = hn
        if with_next:
            y = hn * _rstd(hn) * gn_ref[...]
            rest[0][...] = y.astype(BF16)
            rest[1][...] = y.T.astype(BF16)

    row = lambda i: (i, 0)
    vec = pl.BlockSpec((1, d), lambda i: (0, 0))
    out_shape = [jax.ShapeDtypeStruct((t, d), F32), jax.ShapeDtypeStruct((t, d), F32)]
    out_specs = [pl.BlockSpec((tm, d), row), pl.BlockSpec((tm, d), row)]
    if with_next:
        out_shape += [jax.ShapeDtypeStruct((t, d), BF16), jax.ShapeDtypeStruct((d, t), BF16)]
        out_specs += [pl.BlockSpec((tm, d), row), pl.BlockSpec((d, tm), lambda i: (0, i))]
    return pl.pallas_call(
        body, name=name, out_shape=out_shape, grid=(t // tm,),
        in_specs=[pl.BlockSpec((tm, kdim), row), pl.BlockSpec((kdim, d), lambda i: (0, 0)),
                  pl.BlockSpec((tm, d), row), vec, vec],
        out_specs=out_specs,
        compiler_params=_params(("parallel",), VMEM_BIG),
    )(x, w, h, g_post, g_post if g_next is None else g_next)


def _in_proj(u, w):
    t, d = u.shape
    nz = w.shape[0]
    nq = 3 * ATTN_W
    tm = ROW_TILE // 2

    def body(u_ref, w_ref, qkv_ref, z_ref):
        qkv_ref[...] = _dot_nt(u_ref[...], w_ref[0:nq, :]).astype(BF16)
        z_ref[...] = _dot_nt(u_ref[...], w_ref[nq:, :])

    return pl.pallas_call(
        body, name="mix_in_proj",
        out_shape=[jax.ShapeDtypeStruct((t, nq), BF16), jax.ShapeDtypeStruct((t, nz - nq), F32)],
        grid=(t // tm,),
        in_specs=[pl.BlockSpec((tm, d), lambda i: (i, 0)), pl.BlockSpec((nz, d), lambda i: (0, 0))],
        out_specs=[pl.BlockSpec((tm, nq), lambda i: (i, 0)), pl.BlockSpec((tm, nz - nq), lambda i: (i, 0))],
        compiler_params=_params(("parallel",), VMEM_BIG),
    )(u, w)


def _gate_prep(z, b_pad, f_col):
    t = z.shape[0]
    tm = ROW_TILE

    def body(z_ref, b_ref, f_ref, carry_ref):
        i = pl.program_id(0)

        @pl.when(i == 0)
        def _():
            carry_ref[...] = jnp.zeros_like(carry_ref)

        xs = z_ref[...] + b_ref[...]
        logf = jnp.minimum(xs, 0.0) - jnp.log(1.0 + jnp.exp(-jnp.abs(xs)))
        row = i * tm + lax.broadcasted_iota(jnp.int32, (tm, 1), 0)
        logf = jnp.where(row >= ROW_PAD, logf, 0.0)
        tri = (lax.broadcasted_iota(jnp.int32, (tm, tm), 0) >= lax.broadcasted_iota(jnp.int32, (tm, tm), 1))
        f = jnp.dot(tri.astype(F32), logf, preferred_element_type=F32, precision=lax.Precision.HIGHEST)
        f = f + carry_ref[0:1, :]
        f_ref[...] = f
        carry_ref[...] = jnp.broadcast_to(f[tm - 1:tm, :], carry_ref.shape)

    return pl.pallas_call(
        body, name="forget_gate_cumsum", out_shape=jax.ShapeDtypeStruct((t, 128), F32),
        grid=(t // tm,),
        in_specs=[pl.BlockSpec((tm, 128), lambda i: (i, f_col // 128)), pl.BlockSpec((1, 128), lambda i: (0, 0))],
        out_specs=pl.BlockSpec((tm, 128), lambda i: (i, 0)),
        scratch_shapes=[pltpu.VMEM((8, 128), F32)],
        compiler_params=_params(("arbitrary",)),
    )(z, b_pad)


def _lane_halves():
    lane = lax.broadcasted_iota(jnp.int32, (1, 128), 1)
    return lane < HEAD_DIM


def _causal_mask(tq, tk, row0=0):
    row = row0 + lax.broadcasted_iota(jnp.int32, (tq, 1), 0)
    col = lax.broadcasted_iota(jnp.int32, (1, tk), 1)
    return col <= row


def _lane_one(lane):
    return (lax.broadcasted_iota(jnp.int32, (1, 128), 1) == lane).astype(BF16)


def _split3(x):
    hi = x.astype(BF16)
    rest = x - hi.astype(F32)
    mid = rest.astype(BF16)
    return hi, mid, (rest - mid.astype(F32)).astype(BF16)


def _split3_glue(x):
    hi = lax.reduce_precision(x, 8, 7)
    mid = lax.reduce_precision(x - hi, 8, 7)
    lo = lax.reduce_precision((x - hi) - mid, 8, 7)
    return hi.astype(BF16), mid.astype(BF16), lo.astype(BF16)


def _aug_pairs(cols):
    t = cols[0].shape[0]
    a = jnp.pad(jnp.stack(cols, axis=2), ((0, 0), (0, 0), (0, HEAD_DIM - len(cols))))
    a = a.reshape(t, 4, 2, HEAD_DIM)[:, :, ::-1, :]
    return jnp.transpose(a.reshape(t, 4, 128), (1, 0, 2))


def _attn_bias_operands(f_heads, lse_heads=None):
    t = f_heads.shape[0]
    one = jnp.ones((t, HEADS), BF16)
    row = lax.broadcasted_iota(jnp.int32, (t, 1), 0)
    fq = _split3_glue(f_heads)
    fk = _split3_glue(jnp.where(row < ROW_PAD, 1e9, f_heads))
    q_cols = list(fq) + [one] * 3
    k_cols = [one] * 3 + [-c for c in fk]
    if lse_heads is not None:
        q_cols += [-c for c in _split3_glue(lse_heads)]
        k_cols += [one] * 3
    return _aug_pairs(q_cols), _aug_pairs(k_cols)


def _attn_steps(nq, by_key):
    if by_key:
        pairs = [(qi, ki) for ki in range(nq) for qi in range(ki, nq)]
    else:
        pairs = [(qi, ki) for qi in range(nq) for ki in range(qi + 1)]
    return (jnp.array([p[0] for p in pairs], jnp.int32), jnp.array([p[1] for p in pairs], jnp.int32))


def _attn_fwd(z, aug_q, aug_k):
    t = z.shape[0]
    tq = tk = ROW_TILE
    nq = t // tq
    q_tab, k_tab = _attn_steps(nq, by_key=False)

    def body(qt_ref, kt_ref, q_ref, k_ref, v_ref, aq_ref, ak_ref, o_ref, lse_ref, m_ref, l_ref, acc_ref):
        step = pl.program_id(1)
        qi, ki = qt_ref[step], kt_ref[step]

        @pl.when(ki == 0)
        def _():
            m_ref[...] = jnp.full_like(m_ref, NEG)
            l_ref[...] = jnp.zeros_like(l_ref)
            acc_ref[...] = jnp.zeros_like(acc_ref)

        def sweep(diagonal):
            first = _lane_halves()
            q = (q_ref[...] * (HEAD_DIM ** -0.5)).astype(BF16)
            k = k_ref[...].astype(BF16)
            v = v_ref[...].astype(BF16)
            aq, ak = aq_ref[...], ak_ref[...]
            halves = (first, jnp.logical_not(first))
            qa = [jnp.where(lanes, q, aq) for lanes in halves]
            ka = [jnp.where(lanes, k, ak) for lanes in halves]
            va = [jnp.where(lanes, v, _lane_one(a0)) for lanes, a0 in zip(halves, (HEAD_DIM, 0))]
            chains = [(hh, r) for r in range(ATTN_ROW_PARTS) for hh in range(2)]
            rp = tq // ATTN_ROW_PARTS
            rows = [slice(r * rp, (r + 1) * rp) for _, r in chains]
            s = [_dot_nt(qa[hh][rw], ka[hh]) for (hh, _), rw in zip(chains, rows)]
            if diagonal:
                s = [jnp.where(_causal_mask(rp, tk, rw.start), s_c, NEG) for s_c, rw in zip(s, rows)]
            m_prev = [m_ref[rw, hh * HEAD_DIM:hh * HEAD_DIM + 1] for (hh, _), rw in zip(chains, rows)]
            m_new = [jnp.maximum(mp, jnp.max(s_c, axis=1, keepdims=True)) for mp, s_c in zip(m_prev, s)]
            p = [jnp.exp(s_c - m_c).astype(BF16) for s_c, m_c in zip(s, m_new)]
            pv = [_dot(p_c, va[hh]) for p_c, (hh, _) in zip(p, chains)]
            alpha = [jnp.exp(mp - m_c) for mp, m_c in zip(m_prev, m_new)]
            for r in range(ATTN_ROW_PARTS):
                (m0, m1), (al0, al1), (pv0, pv1) = [x[2 * r:2 * r + 2] for x in (m_new, alpha, pv)]
                rw = rows[2 * r]
                l0 = al0 * l_ref[rw, 0:1] + pv0[:, HEAD_DIM:HEAD_DIM + 1]
                l1 = al1 * l_ref[rw, HEAD_DIM:HEAD_DIM + 1] + pv1[:, 0:1]
                acc_ref[rw, :] = acc_ref[rw, :] * jnp.where(first, al0, al1) + jnp.where(first, pv0, pv1)
                m_ref[rw, :] = jnp.where(first, m0, m1)
                l_ref[rw, :] = jnp.where(first, l0, l1)

        @pl.when(ki < qi)
        def _():
            sweep(False)

        @pl.when(ki == qi)
        def _():
            sweep(True)
            o_ref[...] = acc_ref[...] / l_ref[...]
            lse_ref[...] = m_ref[...] + jnp.log(l_ref[...])

    return pl.pallas_call(
        body, name="attention_fwd",
        out_shape=[jax.ShapeDtypeStruct((t, ATTN_W), F32), jax.ShapeDtypeStruct((t, ATTN_W), F32)],
        grid_spec=pltpu.PrefetchScalarGridSpec(
            num_scalar_prefetch=2, grid=(4, int(q_tab.shape[0])),
            in_specs=[pl.BlockSpec((tq, 128), lambda p, s, qt, kt: (qt[s], p)),
                      pl.BlockSpec((tk, 128), lambda p, s, qt, kt: (kt[s], 4 + p)),
                      pl.BlockSpec((tk, 128), lambda p, s, qt, kt: (kt[s], 8 + p)),
                      pl.BlockSpec((None, tq, 128), lambda p, s, qt, kt: (p, qt[s], 0)),
                      pl.BlockSpec((None, tk, 128), lambda p, s, qt, kt: (p, kt[s], 0))],
            out_specs=[pl.BlockSpec((tq, 128), lambda p, s, qt, kt: (qt[s], p)),
                       pl.BlockSpec((tq, 128), lambda p, s, qt, kt: (qt[s], p))],
            scratch_shapes=[pltpu.VMEM((tq, 128), F32)] * 3),
        compiler_params=_params(("parallel", "arbitrary")),
    )(q_tab, k_tab, z, z, z, aug_q, aug_k)


def _attn_bwd(z, aug_q, aug_k, o, do):
    t = z.shape[0]
    tq = tk = ROW_TILE
    nq = t // tq
    q_tab, k_tab = _attn_steps(nq, by_key=True)
    tn = (((0,), (0,)), ((), ()))

    def body(qt_ref, kt_ref, q_ref, k_ref, v_ref, aq_ref, ak_ref, o_ref, do_ref,
             dq_ref, dk_ref, dv_ref, dfk_ref, dfq_ref):
        step = pl.program_id(1)
        qi, ki = qt_ref[step], kt_ref[step]
        rows = pl.ds(pl.multiple_of(qi * tq, tq), tq)

        @pl.when(ki == 0)
        def _():
            dq_ref[rows, :] = jnp.zeros((tq, 128), F32)
            dfq_ref[rows, :] = jnp.zeros((tq, 128), F32)

        @pl.when(qi == ki)
        def _():
            dk_ref[...] = jnp.zeros_like(dk_ref)
            dv_ref[...] = jnp.zeros_like(dv_ref)
            dfk_ref[...] = jnp.zeros_like(dfk_ref)

        def sweep(diagonal):
            first = _lane_halves()
            lane = lax.broadcasted_iota(jnp.int32, (1, 128), 1)
            scale = HEAD_DIM ** -0.5
            q = (q_ref[...] * scale).astype(BF16)
            k = k_ref[...].astype(BF16)
            v = v_ref[...].astype(BF16)
            do_ = do_ref[...]
            do16 = do_.astype(BF16)
            od = o_ref[...] * do_
            aq, ak = aq_ref[...], ak_ref[...]
            halves = (first, jnp.logical_not(first))
            a0, a1 = HEAD_DIM, 0
            dos, vs = [], []
            for lanes, a in zip(halves, (a0, a1)):
                d_hi, d_mid, d_lo = _split3(jnp.sum(jnp.where(lanes, od, 0.0), axis=1, keepdims=True))
                minus_delta = jnp.where(lane == a, -d_hi, jnp.where(lane == a + 1, -d_mid,
                                        jnp.where(lane == a + 2, -d_lo, jnp.zeros((), BF16))))
                dos.append(jnp.where(lanes, do16, minus_delta))
                vs.append(jnp.where(lanes, v, ((lane >= a) & (lane < a + 3)).astype(BF16)))
            s = [_dot_nt(jnp.where(lanes, q, aq), jnp.where(lanes, k, ak)) for lanes in halves]
            dp = [_dot_nt(do_h, v_h) for do_h, v_h in zip(dos, vs)]
            p = [jnp.exp(s_h) for s_h in s]
            if diagonal:
                p = [jnp.where(_causal_mask(tq, tk), p_h, 0.0) for p_h in p]
            ds16 = [(p_h * dp_h).astype(BF16) for p_h, dp_h in zip(p, dp)]
            dv0, dv1 = [lax.dot_general(p_h.astype(BF16), jnp.where(lanes, do16, jnp.zeros((), BF16)), tn,
                                        preferred_element_type=F32) for p_h, lanes in zip(p, halves)]
            dk0, dk1 = [lax.dot_general(ds_h, jnp.where(lanes, q, _lane_one(a)), tn, preferred_element_type=F32)
                        for ds_h, lanes, a in zip(ds16, halves, (a0, a1))]
            dq0, dq1 = [_dot(ds_h, jnp.where(lanes, k, _lane_one(a))) for ds_h, lanes, a in zip(ds16, halves, (a0, a1))]
            dq_ref[rows, :] += jnp.where(first, dq0, dq1) * scale
            dfq_ref[rows, :] += jnp.where(first, dq0[:, a0:a0 + 1], dq1[:, a1:a1 + 1])
            dk_ref[...] += jnp.where(first, dk0, dk1)
            dfk_ref[...] += jnp.where(first, dk0[:, a0:a0 + 1], dk1[:, a1:a1 + 1])
            dv_ref[...] += dv0 + dv1

        @pl.when(qi > ki)
        def _():
            sweep(False)

        @pl.when(qi == ki)
        def _():
            sweep(True)

    qrow = lambda p, s, qt, kt: (qt[s], p)
    krow = lambda p, s, qt, kt: (kt[s], p)
    return pl.pallas_call(
        body, name="attention_bwd",
        out_shape=[jax.ShapeDtypeStruct((t, ATTN_W), F32)] * 5,
        grid_spec=pltpu.PrefetchScalarGridSpec(
            num_scalar_prefetch=2, grid=(4, int(q_tab.shape[0])),
            in_specs=[pl.BlockSpec((tq, 128), qrow),
                      pl.BlockSpec((tk, 128), lambda p, s, qt, kt: (kt[s], 4 + p)),
                      pl.BlockSpec((tk, 128), lambda p, s, qt, kt: (kt[s], 8 + p)),
                      pl.BlockSpec((None, tq, 128), lambda p, s, qt, kt: (p, qt[s], 0)),
                      pl.BlockSpec((None, tk, 128), lambda p, s, qt, kt: (p, kt[s], 0)),
                      pl.BlockSpec((tq, 128), qrow), pl.BlockSpec((tq, 128), qrow)],
            out_specs=[pl.BlockSpec((t, 128), lambda p, s, qt, kt: (0, p)),
                       pl.BlockSpec((tk, 128), krow), pl.BlockSpec((tk, 128), krow), pl.BlockSpec((tk, 128), krow),
                       pl.BlockSpec((t, 128), lambda p, s, qt, kt: (0, p))]),
        compiler_params=_params(("parallel", "arbitrary"), VMEM_BIG),
    )(q_tab, k_tab, z, z, z, aug_q, aug_k, o, do)


def _shifted(prev_rows, x, shift):
    tm = x.shape[0]
    return pltpu.roll(jnp.concatenate([prev_rows, x], axis=0), shift, 0)[8:8 + tm]


def _ahead(x, next_rows, shift):
    tm = x.shape[0]
    return pltpu.roll(jnp.concatenate([x, next_rows], axis=0), tm + 8 - shift, 0)[0:tm]


def _conv_col0(z):
    return (z.shape[1] - F_PAD - 3 * CONV_W) // CONV_W


def _conv_specs(tm, c0):
    cols = (c0, c0 + 1, c0 + 2)
    tiles = [pl.BlockSpec((tm, CONV_W), functools.partial(lambda i, c: (i, c), c=c)) for c in cols]
    halos = [pl.BlockSpec((8, CONV_W), functools.partial(lambda i, c: (jnp.maximum(i * (tm // 8) - 1, 0), c), c=c))
             for c in cols]
    return tiles, halos


def _conv_gate(z, conv_w):
    t = z.shape[0]
    tm = ROW_TILE
    nt = t // tm

    def body(cb_ref, cc_ref, ci_ref, hc_ref, hi_ref, w_ref, g_ref, gt_ref):
        i = pl.program_id(0)
        cc = cc_ref[...] * ci_ref[...]
        prev = jnp.where(i > 0, hc_ref[...] * hi_ref[...], 0.0)
        conv = w_ref[0:1, :] * _shifted(prev, cc, 2) + w_ref[1:2, :] * _shifted(prev, cc, 1) + w_ref[2:3, :] * cc
        g = cb_ref[...] * conv
        g_ref[...] = g.astype(BF16)
        gt_ref[...] = g.T.astype(BF16)

    (cb, cc, ci), (_, hc, hi) = _conv_specs(tm, _conv_col0(z))
    return pl.pallas_call(
        body, name="conv_gate_fwd",
        out_shape=[jax.ShapeDtypeStruct((t, CONV_W), BF16), jax.ShapeDtypeStruct((CONV_W, t), BF16)],
        grid=(nt,),
        in_specs=[cb, cc, ci, hc, hi, pl.BlockSpec((8, CONV_W), lambda i: (0, 0))],
        out_specs=[pl.BlockSpec((tm, CONV_W), lambda i: (i, 0)), pl.BlockSpec((CONV_W, tm), lambda i: (0, i))],
        compiler_params=_params(("parallel",)),
    )(z, z, z, z, z, conv_w)


def _conv_bwd(z, dg, conv_w):
    t = z.shape[0]
    tm = ROW_TILE
    nt = t // tm

    def body(cb_ref, cc_ref, ci_ref, hc_ref, hi_ref, dg_ref, ncb_ref, ndg_ref, w_ref, dz_ref, dw_ref):
        i = pl.program_id(0)

        @pl.when(i == 0)
        def _():
            dw_ref[...] = jnp.zeros_like(dw_ref)

        cb, c_c, c_in = cb_ref[...], cc_ref[...], ci_ref[...]
        cc = c_c * c_in
        prev = jnp.where(i > 0, hc_ref[...] * hi_ref[...], 0.0)
        cc1, cc2 = _shifted(prev, cc, 1), _shifted(prev, cc, 2)
        w0, w1, w2 = w_ref[0:1, :], w_ref[1:2, :], w_ref[2:3, :]
        conv = w0 * cc2 + w1 * cc1 + w2 * cc
        dgv = dg_ref[...]
        dconv = dgv * cb
        nxt = jnp.where(i < nt - 1, ndg_ref[...] * ncb_ref[...], 0.0)
        dcc = w2 * dconv + w1 * _ahead(dconv, nxt, 1) + w0 * _ahead(dconv, nxt, 2)
        dz_ref[:, 0:CONV_W] = (dgv * conv).astype(BF16)
        dz_ref[:, CONV_W:2 * CONV_W] = (dcc * c_in).astype(BF16)
        dz_ref[:, 2 * CONV_W:] = (dcc * c_c).astype(BF16)
        dw_ref[0:1, :] += jnp.sum(dconv * cc2, axis=0, keepdims=True)
        dw_ref[1:2, :] += jnp.sum(dconv * cc1, axis=0, keepdims=True)
        dw_ref[2:3, :] += jnp.sum(dconv * cc, axis=0, keepdims=True)

    c0 = _conv_col0(z)
    (cb, cc, ci), (_, hc, hi) = _conv_specs(tm, c0)
    nxt = lambda i, c: (jnp.minimum((i + 1) * (tm // 8), t // 8 - 1), c)
    return pl.pallas_call(
        body, name="conv_gate_bwd",
        out_shape=[jax.ShapeDtypeStruct((t, 3 * CONV_W), BF16), jax.ShapeDtypeStruct((8, CONV_W), F32)],
        grid=(nt,),
        in_specs=[cb, cc, ci, hc, hi, pl.BlockSpec((tm, CONV_W), lambda i: (i, 0)),
                  pl.BlockSpec((8, CONV_W), lambda i: nxt(i, c0)), pl.BlockSpec((8, CONV_W), lambda i: nxt(i, 0)),
                  pl.BlockSpec((8, CONV_W), lambda i: (0, 0))],
        out_specs=[pl.BlockSpec((tm, 3 * CONV_W), lambda i: (i, 0)), pl.BlockSpec((8, CONV_W), lambda i: (0, 0))],
        compiler_params=_params(("arbitrary",)),
    )(z, z, z, z, z, dg, z, dg, conv_w)


def _branch_mix(z, o, g, w_ab, w_cb, d):
    t = z.shape[0]
    tm = ROW_TILE
    ga_col = 0

    def body(o_ref, g_ref, ga_ref, gc_ref, wa_ref, wc_ref, mp_ref, mpt_ref, ot_ref):
        o_ = o_ref[...]
        ya = _dot(o_.astype(BF16), wa_ref[...])
        yc = _dot(g_ref[...], wc_ref[...])
        mp = _sigmoid(ga_ref[...]) * ya + _sigmoid(gc_ref[...]) * yc
        mp_ref[...] = mp.astype(BF16)
        mpt_ref[...] = mp.T.astype(BF16)
        ot_ref[...] = o_.T.astype(BF16)

    return pl.pallas_call(
        body, name="branch_mix_fwd",
        out_shape=[jax.ShapeDtypeStruct((t, d), BF16), jax.ShapeDtypeStruct((d, t), BF16),
                   jax.ShapeDtypeStruct((ATTN_W, t), BF16)],
        grid=(t // tm,),
        in_specs=[pl.BlockSpec((tm, ATTN_W), lambda i: (i, 0)), pl.BlockSpec((tm, CONV_W), lambda i: (i, 0)),
                  pl.BlockSpec((tm, d), lambda i: (i, ga_col)), pl.BlockSpec((tm, d), lambda i: (i, ga_col + 1)),
                  pl.BlockSpec((ATTN_W, d), lambda i: (0, 0)), pl.BlockSpec((CONV_W, d), lambda i: (0, 0))],
        out_specs=[pl.BlockSpec((tm, d), lambda i: (i, 0)), pl.BlockSpec((d, tm), lambda i: (0, i)),
                   pl.BlockSpec((ATTN_W, tm), lambda i: (0, i))],
        compiler_params=_params(("parallel",), VMEM_BIG),
    )(o, g, z, z, w_ab, w_cb)


def _branch_bwd(z, o, g, dmixed, w_out, w_ab, w_cb, d):
    t = z.shape[0]
    tm = ROW_TILE // 2
    ga_col = 0

    def body(dm_ref, o_ref, g_ref, ga_ref, gc_ref, wo_ref, wa_ref, wc_ref, dya_ref, dyc_ref, dgt_ref, do_ref, dg_ref):
        dmp = _dot_nt(dm_ref[...], wo_ref[...])
        ya = _dot(o_ref[...].astype(BF16), wa_ref[...])
        yc = _dot(g_ref[...], wc_ref[...])
        sa, sc = _sigmoid(ga_ref[...]), _sigmoid(gc_ref[...])
        dya = (dmp * sa).astype(BF16)
        dyc = (dmp * sc).astype(BF16)
        dya_ref[...] = dya
        dyc_ref[...] = dyc
        dgt_ref[:, :d] = (dmp * ya * sa * (1.0 - sa)).astype(BF16)
        dgt_ref[:, d:] = (dmp * yc * sc * (1.0 - sc)).astype(BF16)
        do_ref[...] = _dot_nt(dya, wa_ref[...])
        dg_ref[...] = _dot_nt(dyc, wc_ref[...])

    row = lambda i: (i, 0)
    fixed = lambda i: (0, 0)
    return pl.pallas_call(
        body, name="branch_mix_bwd",
        out_shape=[jax.ShapeDtypeStruct((t, d), BF16), jax.ShapeDtypeStruct((t, d), BF16),
                   jax.ShapeDtypeStruct((t, 2 * d), BF16), jax.ShapeDtypeStruct((t, ATTN_W), F32),
                   jax.ShapeDtypeStruct((t, CONV_W), F32)],
        grid=(t // tm,),
        in_specs=[pl.BlockSpec((tm, d), row), pl.BlockSpec((tm, ATTN_W), row), pl.BlockSpec((tm, CONV_W), row),
                  pl.BlockSpec((tm, d), lambda i: (i, ga_col)), pl.BlockSpec((tm, d), lambda i: (i, ga_col + 1)),
                  pl.BlockSpec((d, d), fixed), pl.BlockSpec((ATTN_W, d), fixed), pl.BlockSpec((CONV_W, d), fixed)],
        out_specs=[pl.BlockSpec((tm, d), row), pl.BlockSpec((tm, d), row), pl.BlockSpec((tm, 2 * d), row),
                   pl.BlockSpec((tm, ATTN_W), row), pl.BlockSpec((tm, CONV_W), row)],
        compiler_params=_params(("parallel",), VMEM_BIG),
    )(dmixed, o, g, z, z, w_out, w_ab, w_cb)


def _loss_grad(h, target_pad):
    t, d = h.shape
    tm = ROW_TILE

    def body(h_ref, t_ref, dy_ref, loss_ref):
        i = pl.program_id(0)

        @pl.when(i == 0)
        def _():
            loss_ref[...] = jnp.zeros_like(loss_ref)

        row = i * tm + lax.broadcasted_iota(jnp.int32, (tm, 1), 0)
        err = jnp.where(row >= N_FRONT, h_ref[...] - t_ref[...], 0.0)
        dy_ref[...] = err * (1.0 / d)
        per_row = jnp.sum(err * err, axis=1, keepdims=True) * (1.0 / d)
        loss_ref[...] += 0.5 * jnp.sum(per_row, axis=0, keepdims=True)

    return pl.pallas_call(
        body, name="loss_and_grad",
        out_shape=[jax.ShapeDtypeStruct((t, d), F32), jax.ShapeDtypeStruct((1, 128), F32)],
        grid=(t // tm,),
        in_specs=[pl.BlockSpec((tm, d), lambda i: (i, 0))] * 2,
        out_specs=[pl.BlockSpec((tm, d), lambda i: (i, 0)), pl.BlockSpec((1, 128), lambda i: (0, 0))],
        compiler_params=_params(("arbitrary",)),
    )(h, target_pad)


def _norm_bwd(name, x, g, dy, alpha):
    t, d = x.shape
    tm = ROW_TILE

    def body(x_ref, g_ref, dy_ref, dx_ref, dg_ref):
        @pl.when(pl.program_id(0) == 0)
        def _():
            dg_ref[...] = jnp.zeros_like(dg_ref)

        dx, dg = _rms_bwd(x_ref[...], g_ref[...], dy_ref[...])
        dx_ref[...] = (alpha * dx).astype(BF16)
        dg_ref[...] += alpha * dg

    row = pl.BlockSpec((tm, d), lambda i: (i, 0))
    vec = pl.BlockSpec((1, d), lambda i: (0, 0))
    return pl.pallas_call(
        body, name=name,
        out_shape=[jax.ShapeDtypeStruct((t, d), BF16), jax.ShapeDtypeStruct((1, d), F32)],
        grid=(t // tm,), in_specs=[row, vec, row], out_specs=[row, vec],
        compiler_params=_params(("arbitrary",)),
    )(x, g, dy)


def _ffn_bwd_mid(name, df, w_out, ab):
    t, d = df.shape
    cw = ab.shape[1] // 4
    tm = ROW_TILE

    def body(df_ref, w_ref, ab_ref, o_ref):
        ds = _dot_nt(df_ref[...], w_ref[...])
        a = ab_ref[:, :cw]
        b = ab_ref[:, cw:]
        sg = _sigmoid(a)
        o_ref[:, :cw] = (ds * b * (sg * (1.0 + a * (1.0 - sg)))).astype(BF16)
        o_ref[:, cw:] = (ds * (a * sg)).astype(BF16)

    return pl.pallas_call(
        body, name=name, out_shape=jax.ShapeDtypeStruct((t, 4 * cw), BF16),
        grid=(2, t // tm),
        in_specs=[pl.BlockSpec((tm, d), lambda j, i: (i, 0)), pl.BlockSpec((cw, d), lambda j, i: (j, 0)),
                  pl.BlockSpec((tm, 2 * cw), lambda j, i: (i, j))],
        out_specs=pl.BlockSpec((tm, 2 * cw), lambda j, i: (i, j)),
        compiler_params=_params(("parallel", "parallel"), VMEM_BIG),
    )(df, w_out, ab)


def _mm_nt_norm_bwd(name, dy, w, h, g, dh_in):
    t, kdim = dy.shape
    d = h.shape[1]
    tm = ROW_TILE // 2
    slots = w.ndim == 3

    def body(dy_ref, w_ref, h_ref, g_ref, dhi_ref, dh_ref, dg_ref):
        @pl.when(pl.program_id(0) == 0)
        def _():
            dg_ref[...] = jnp.zeros_like(dg_ref)

        if slots:
            cw = w_ref.shape[2]
            dn = _dot_nt(dy_ref[:, 0:cw], w_ref[_slot_of(0)])
            for k in range(1, 4):
                dn += _dot_nt(dy_ref[:, k * cw:(k + 1) * cw], w_ref[_slot_of(k)])
        else:
            dn = _dot_nt(dy_ref[...], w_ref[...])
        dx, dg = _rms_bwd(h_ref[...], g_ref[...], dn)
        dh_ref[...] = dhi_ref[...] + dx
        dg_ref[...] += dg

    row = pl.BlockSpec((tm, d), lambda i: (i, 0))
    vec = pl.BlockSpec((1, d), lambda i: (0, 0))
    return pl.pallas_call(
        body, name=name,
        out_shape=[jax.ShapeDtypeStruct((t, d), F32), jax.ShapeDtypeStruct((1, d), F32)],
        grid=(t // tm,),
        in_specs=[pl.BlockSpec((tm, kdim), lambda i: (i, 0)), pl.BlockSpec(w.shape, lambda i: (0,) * w.ndim),
                  row, vec, row],
        out_specs=[row, vec],
        compiler_params=_params(("arbitrary",), VMEM_BIG),
    )(dy, w, h, g, dh_in)


def _gate_bwd(df_pad, z, b_pad, f_col):
    t = z.shape[0]
    tm = ROW_TILE
    nt = t // tm

    def body(d_ref, z_ref, b_ref, dz_ref, db_ref, carry_ref):
        i = pl.program_id(0)

        @pl.when(i == 0)
        def _():
            carry_ref[...] = jnp.zeros_like(carry_ref)
            db_ref[...] = jnp.zeros_like(db_ref)

        tri = (lax.broadcasted_iota(jnp.int32, (tm, tm), 0) <= lax.broadcasted_iota(jnp.int32, (tm, tm), 1))
        tail = jnp.dot(tri.astype(F32), d_ref[...], preferred_element_type=F32, precision=lax.Precision.HIGHEST)
        tail = tail + carry_ref[0:1, :]
        carry_ref[...] = jnp.broadcast_to(tail[0:1, :], carry_ref.shape)
        row = (nt - 1 - i) * tm + lax.broadcasted_iota(jnp.int32, (tm, 1), 0)
        dlogit = jnp.where(row >= ROW_PAD, tail * _sigmoid(-(z_ref[...] + b_ref[...])), 0.0)
        dz_ref[...] = jnp.zeros_like(dz_ref)
        dz_ref[:, 0:128] = dlogit.astype(BF16)
        db_ref[...] += jnp.sum(dlogit, axis=0, keepdims=True)

    rev = lambda i: (nt - 1 - i, 0)
    return pl.pallas_call(
        body, name="forget_gate_bwd",
        out_shape=[jax.ShapeDtypeStruct((t, F_PAD), BF16), jax.ShapeDtypeStruct((1, 128), F32)],
        grid=(nt,),
        in_specs=[pl.BlockSpec((tm, 128), rev), pl.BlockSpec((tm, 128), lambda i: (nt - 1 - i, f_col // 128)),
                  pl.BlockSpec((1, 128), lambda i: (0, 0))],
        out_specs=[pl.BlockSpec((tm, F_PAD), rev), pl.BlockSpec((1, 128), lambda i: (0, 0))],
        scratch_shapes=[pltpu.VMEM((8, 128), F32)],
        compiler_params=_params(("arbitrary",)),
    )(df_pad, z, b_pad)


def _ffn_fwd(tag, n, w_in4, w_out, h, g_post, g_next):
    ab, s, s_t = _ffn_in(f"{tag}_in_fwd", n, w_in4)
    outs = _mm_resid_norm(f"{tag}_out_fwd", s, w_out, h, g_post, 0.5, g_next)
    return ab, s_t, outs


def _ffn_bwd_weights(tag, dh, f, g_post, ab, s_t, n_t, w_in4, w_out):
    d, cw = w_in4.shape[1], w_in4.shape[2]
    t = dh.shape[0]
    df, dg_post = _norm_bwd(f"{tag}_post_norm_bwd", f, g_post, dh, 0.5)
    dw_out = _weight_grad(f"{tag}_dw_out", s_t, df, d, out_rows=cw // 2)
    dab = _ffn_bwd_mid(f"{tag}_mid_bwd", df, w_out, ab)
    bk = _k_tile(t)
    dw_in = _matmul(
        f"{tag}_dw_in", n_t, dab, jax.ShapeDtypeStruct((4, d, cw), F32), (1, 4, t // bk),
        pl.BlockSpec((d, bk), lambda a, b, k: (0, k)), pl.BlockSpec((bk, cw), lambda a, b, k: (k, b)),
        pl.BlockSpec((None, d, cw), lambda a, b, k: (_slot_of(b), 0, 0)), vmem=VMEM_BIG)
    return dab, dg_post, dw_in, dw_out


LOSS_ROW = 12


def _pack_small(meta, conv, gains, b_forget, loss=None):
    d = gains[0].shape[1]
    rows = [meta.reshape(4, d), jnp.pad(conv.reshape(1, 3 * 128), ((0, 0), (0, d - 3 * 128)))]
    rows += list(gains) + [jnp.pad(b_forget, ((0, 0), (0, d - HEADS)))]
    last = jnp.zeros((4, d), F32)
    if loss is not None:
        last = jnp.pad(loss.reshape(1, 1), ((0, 3), (0, d - 1)))
    return jnp.concatenate(rows + [last], axis=0)


def _unpack_small(block):
    d = block.shape[1]
    meta = block[0:4].reshape(N_META, d // 4)
    conv = block[4, :3 * 128].reshape(1, 3, 128)
    gains = [block[5 + i:6 + i] for i in range(6)]
    return meta, conv, gains, block[11:12, :HEADS]


def kernel(x, meta_tokens, w_in, b_forget, conv_w, w_attn_branch, w_conv_branch, w_out, g_ffn1_pre, g_ffn1_post, w_ffn1_in, w_ffn1_out, g_mix_pre, g_mix_post, g_ffn2_pre, g_ffn2_post, w_ffn2_in, w_ffn2_out, loss_target, m_meta_tokens, m_w_in, m_b_forget, m_conv_w, m_w_attn_branch, m_w_conv_branch, m_w_out, m_g_ffn1_pre, m_g_ffn1_post, m_w_ffn1_in, m_w_ffn1_out, m_g_mix_pre, m_g_mix_post, m_g_ffn2_pre, m_g_ffn2_post, m_w_ffn2_in, m_w_ffn2_out, v_meta_tokens, v_w_in, v_b_forget, v_conv_w, v_w_attn_branch, v_w_conv_branch, v_w_out, v_g_ffn1_pre, v_g_ffn1_post, v_w_ffn1_in, v_w_ffn1_out, v_g_mix_pre, v_g_mix_post, v_g_ffn2_pre, v_g_ffn2_post, v_w_ffn2_in, v_w_ffn2_out):
    seq, d = x.shape[1], x.shape[2]
    t = seq + N_FRONT
    n_main = 3 * ATTN_W + 3 * CONV_W + 2 * d
    nz = n_main + F_PAD
    f_lo = 3 * ATTN_W
    c_arr = lax.axis_index("c").astype(jnp.int32).reshape(1)

    cs = w_in.shape[2]
    cs_pad = -(-cs // 64) * 64

    def w_in_rows(a):
        return jnp.pad(jnp.transpose(a[0]), ((0, cs_pad - cs), (0, 0)))

    big = [w_in_rows(w_in), w_attn_branch[0], w_conv_branch[0], w_out[0], w_ffn1_in[0], w_ffn1_out[0], w_ffn2_in[0],
           w_ffn2_out[0]]
    small_gather = jnp.concatenate(
        [meta_tokens.reshape(4, d), jnp.pad(conv_w.reshape(1, 3 * 128), ((0, 0), (0, d - 3 * 128))),
         jnp.zeros((11, d), F32)], axis=0)
    w_f1_in4, w_f1_out4, small4 = _all_gather([big[4].astype(BF16), big[5].astype(BF16), small_gather])
    rest, small4 = lax.optimization_barrier(([big[i].astype(BF16) for i in (0, 1, 2, 3, 6, 7)], small4))
    rest_gathered = _all_gather_async(rest)
    w_f1_out = w_f1_out4.reshape(-1, d)
    meta_full = jnp.transpose(small4[:, 0:4].reshape(4, N_META, d // 4), (1, 0, 2)).reshape(N_META, d)
    conv_full = jnp.transpose(small4[:, 4, :3 * 128].reshape(4, 3, 128), (1, 0, 2)).reshape(3, CONV_W)
    conv_pad = jnp.pad(conv_full, ((0, 5), (0, 0)))
    b_pad = jnp.pad(b_forget, ((0, 0), (0, 128 - HEADS)))

    h0 = jnp.concatenate([jnp.zeros((ROW_PAD, d), F32), meta_full, x[0]], axis=0)
    target_pad = jnp.concatenate([jnp.zeros((N_FRONT, d), F32), loss_target[0]], axis=0)
    n1, n1_t = _norm_fwd("ffn1_pre_norm", h0, g_ffn1_pre)
    ab1, s1_t, (f1, h1, u, u_t) = _ffn_fwd("ffn1", n1, w_f1_in4, w_f1_out, h0, g_ffn1_post, g_mix_pre)

    w_in4, w_ab4, w_cb4, w_out4, w_f2_in4, w_f2_out4 = rest_gathered(u)
    w_in_t = w_in4[:, :cs].reshape(4 * cs, d)
    g_lo = f_lo + HEADS + 3 * CONV_W
    w_in_pad = jnp.concatenate(
        [w_in_t[:f_lo], w_in_t[g_lo:], w_in_t[f_lo + HEADS:g_lo], w_in_t[f_lo:f_lo + HEADS],
         jnp.zeros((F_PAD - HEADS, d), BF16)], axis=0)
    w_ab = jnp.transpose(w_ab4, (1, 0, 2)).reshape(ATTN_W, d)
    w_cb = jnp.transpose(w_cb4, (1, 0, 2)).reshape(CONV_W, d)
    w_out_full = w_out4.reshape(d, d)
    w_f2_out = w_f2_out4.reshape(-1, d)
    qkv, z = _in_proj(u, w_in_pad)
    f_col = z.shape[1] - F_PAD
    f_cum = _gate_prep(z, b_pad, f_col)
    f_heads = f_cum[:, :HEADS]
    o, lse = _attn_fwd(qkv, *_attn_bias_operands(f_heads))
    g, g_t = _conv_gate(z, conv_pad)
    mp, mp_t, o_t = _branch_mix(z, o, g, w_ab, w_cb, d)
    mixed, h2, n2, n2_t = _mm_resid_norm("mix_out_fwd", mp, w_out_full, h1, g_mix_post, 1.0, g_ffn2_pre)
    ab2, s2_t, (f2, h3) = _ffn_fwd("ffn2", n2, w_f2_in4, w_f2_out, h2, g_ffn2_post, None)
    dh3, loss_part = _loss_grad(h3, target_pad)

    reduced = {}

    def reduce_scatter(label, tags, slots, sequencer_id, hold=None):
        got = _pair_send_halves(f"grad_pair_exchange_{label}", slots)
        sums = [_pair_add(tag, s, a, c_arr, F32 if tag == "small" else BF16) for tag, s, a in zip(tags, slots, got)]
        sums, hold = lax.optimization_barrier((sums, hold))
        if sequencer_id is None:
            arrived = _chip_scatter(f"grad_chip_scatter_{label}", sums)
        else:
            arrived = _chip_scatter_async(f"grad_chip_scatter_{label}", sums, sequencer_id)
        mine = [_chip_add(tag, a) for tag, a in zip(tags, arrived)]
        reduced.update(zip(tags, zip(mine, _pair_swap(f"grad_pair_swap_{label}", mine))))
        return hold

    dab2, dg_f2_post, dw_f2_in, dw_f2_out = _ffn_bwd_weights(
        "ffn2", dh3, f2, g_ffn2_post, ab2, s2_t, n2_t, w_f2_in4, w_f2_out)
    dab2 = reduce_scatter("ffn2", ["w_ffn2_in", "w_ffn2_out"], [dw_f2_in, dw_f2_out.reshape(4, -1, d)], 2, dab2)
    dh2, dg_f2_pre = _mm_nt_norm_bwd("ffn2_in_bwd", dab2, w_f2_in4, h2, g_ffn2_pre, dh3)
    dmixed, dg_mix_post = _norm_bwd("mix_post_norm_bwd", mixed, g_mix_post, dh2, 1.0)
    dw_out = _weight_grad("mix_dw_out", mp_t, dmixed, d)
    dya, dyc, dgates, do, dgconv = _branch_bwd(z, o, g, dmixed, w_out_full, w_ab, w_cb, d)
    dw_ab = _weight_grad("mix_dw_attn_branch", o_t, dya, d)
    dw_cb = _weight_grad("mix_dw_conv_branch", g_t, dyc, d)
    dz_conv, dconv_w = _conv_bwd(z, dgconv, conv_pad)
    front = lax.broadcasted_iota(jnp.int32, (t, 1), 0) < ROW_PAD
    lse_heads = jnp.where(front, 1e9, lse[:, ::HEAD_DIM])
    dq, dk, dv, dfk, dfq = _attn_bwd(qkv, *_attn_bias_operands(f_heads, lse_heads), o, do)
    df_pad = jnp.pad((dfq - dfk)[:, ::HEAD_DIM], ((0, 0), (0, 128 - HEADS)))
    dz_f, db_forget = _gate_bwd(df_pad, z, b_pad, f_col)
    dz_pieces = {"q": dq, "k": dk, "v": dv, "gates": dgates, "conv": dz_conv, "f": dz_f}
    dh1, dg_mix_pre = _mix_in_bwd(list(dz_pieces.values()), w_in_pad, h1, g_mix_pre, dh2)
    dw_t = {name: _weight_grad_t(f"mix_dw_in_{name}", u_t, piece) for name, piece in dz_pieces.items()}
    dw_in_t = jnp.concatenate(
        [dw_t["q"], dw_t["k"], dw_t["v"], dw_t["f"][:HEADS], dw_t["conv"], dw_t["gates"]], axis=0)
    reduce_scatter(
        "mix", ["w_in", "w_attn_branch", "w_conv_branch", "w_out"],
        [jnp.pad(dw_in_t.reshape(4, cs, d), ((0, 0), (0, cs_pad - cs), (0, 0))),
         jnp.transpose(dw_ab.reshape(ATTN_W, 4, d // 4), (1, 0, 2)),
         jnp.transpose(dw_cb.reshape(CONV_W, 4, d // 4), (1, 0, 2)),
         dw_out.reshape(4, d // 4, d)], 3)
    dab1, dg_f1_post, dw_f1_in, dw_f1_out = _ffn_bwd_weights(
        "ffn1", dh1, f1, g_ffn1_post, ab1, s1_t, n1_t, w_f1_in4, w_f1_out)
    dab1 = reduce_scatter("ffn1", ["w_ffn1_in", "w_ffn1_out"], [dw_f1_in, dw_f1_out.reshape(4, -1, d)], 4, dab1)
    dh0, dg_f1_pre = _mm_nt_norm_bwd("ffn1_in_bwd", dab1, w_f1_in4, h0, g_ffn1_pre, dh1)
    grad_x = dh0[N_FRONT:][None]
    dmeta = dh0[ROW_PAD:N_FRONT]
    small_grad = jnp.stack([
        _pack_small(dmeta[:, j * (d // 4):(j + 1) * (d // 4)], dconv_w[:3, j * 128:(j + 1) * 128],
                    [dg_f1_pre, dg_f1_post, dg_mix_pre, dg_mix_post, dg_f2_pre, dg_f2_post], db_forget[:, :HEADS],
                    loss_part[0, 0])
        for j in range(4)])
    reduce_scatter("small", ["small"], [small_grad], None)
    tags =["w_in", "w_attn_branch", "w_conv_branch", "w_out", "w_ffn1_in", "w_ffn1_out", "w_ffn2_in", "w_ffn2_out", "small"]
    halves = [reduced[tag][0] for tag in tags]
    others = [reduced[tag][1] for tag in tags]

    small = [g_ffn1_pre, g_ffn1_post, g_mix_pre, g_mix_post, g_ffn2_pre, g_ffn2_post]
    small_m = [m_g_ffn1_pre, m_g_ffn1_post, m_g_mix_pre, m_g_mix_post, m_g_ffn2_pre, m_g_ffn2_post]
    small_v = [v_g_ffn1_pre, v_g_ffn1_post, v_g_mix_pre, v_g_mix_post, v_g_ffn2_pre, v_g_ffn2_post]
    ws = big + [_pack_small(meta_tokens, conv_w[0], small, b_forget)]
    ms = [w_in_rows(m_w_in), m_w_attn_branch[0], m_w_conv_branch[0], m_w_out[0], m_w_ffn1_in[0], m_w_ffn1_out[0],
          m_w_ffn2_in[0], m_w_ffn2_out[0], _pack_small(m_meta_tokens, m_conv_w[0], small_m, m_b_forget)]
    vs = [w_in_rows(v_w_in), v_w_attn_branch[0], v_w_conv_branch[0], v_w_out[0], v_w_ffn1_in[0], v_w_ffn1_out[0],
          v_w_ffn2_in[0], v_w_ffn2_out[0], _pack_small(v_meta_tokens, v_conv_w[0], small_v, v_b_forget)]
    updates = [_adamw(tag, w, a, b, m, v, c_arr) for tag, w, a, b, m, v in zip(tags, ws, halves, others, ms, vs)]

    def leaves(big_vals, small_block):
        meta, conv, gains, bf = _unpack_small(small_block)
        w_in_t_, w_ab_, w_cb_, w_out_, f1_in, f1_out, f2_in, f2_out = [b[None] for b in big_vals]
        w_in_ = jnp.transpose(w_in_t_[:, :cs], (0, 2, 1))
        return [meta, w_in_, bf, conv, w_ab_, w_cb_, w_out_, gains[0], gains[1], f1_in, f1_out,
                gains[2], gains[3], gains[4], gains[5], f2_in, f2_out]

    out_g, out_d, out_m, out_v = [leaves([u_[k] for u_ in updates[:8]], updates[8][k]) for k in range(4)]
    loss = updates[8][0][LOSS_ROW, 0]
    return (loss, grad_x, *out_g, *out_d, *out_m, *out_v)
```

```python
import functools

import jax
import jax.numpy as jnp
from jax import lax
from jax.experimental import pallas as pl
from jax.experimental.pallas import tpu as pltpu
from jax.experimental.pallas import tpu_sc as plsc

N_META = 16
ROW_PAD = 112
N_FRONT = ROW_PAD + N_META
HEADS = 8
HEAD_DIM = 64
ATTN_W = HEADS * HEAD_DIM
CONV_W = 512
NORM_EPS = 1e-6
ROW_TILE = 640
F_PAD = 128
ATTN_ROW_PARTS = 1
NEG = -1e30
ADAM_LR = 0.001
ADAM_B1 = 0.9
ADAM_B2 = 0.999
ADAM_EPS = 1e-08
ADAM_WD = 0.01
ADAM_STEP = 10
VMEM_BIG = 56 * 1024 * 1024
MESH = pl.DeviceIdType.MESH
ANY = pl.BlockSpec(memory_space=pl.ANY)
F32 = jnp.float32
BF16 = jnp.bfloat16


def _params(sem, vmem=None):
    return pltpu.CompilerParams(dimension_semantics=sem, vmem_limit_bytes=vmem)


def _sigmoid(x):
    return 1.0 / (1.0 + jnp.exp(-x))


def _rstd(x):
    return lax.rsqrt(jnp.mean(x * x, axis=-1, keepdims=True) + NORM_EPS)


def _rms_bwd(x, g, dy):
    r = _rstd(x)
    xr = x * r
    gdy = g * dy
    dx = r * (gdy - xr * jnp.mean(xr * gdy, axis=-1, keepdims=True))
    return dx, jnp.sum(dy * xr, axis=0, keepdims=True)


def _dot(a, b):
    return jnp.dot(a, b, preferred_element_type=F32)


def _dot_nt(a, b):
    return lax.dot_general(a, b, (((1,), (1,)), ((), ())), preferred_element_type=F32)


def _k_tile(t):
    return 1664 if t % 1664 == 0 else ROW_TILE


def _place():
    x, y, c = lax.axis_index("x"), lax.axis_index("y"), lax.axis_index("c")
    chips = [(1 - x, y), (x, 1 - y), (1 - x, 1 - y)]
    return x, y, c, chips


def _all_gather(shards):
    n = len(shards)
    split = [s.reshape(2, s.shape[0] // 2, s.shape[1]) for s in shards]

    def body(*refs):
        ins, outs = refs[:n], refs[n:2 * n]
        send_sems, recv_sems = refs[2 * n:]
        x, y, c, chips = _place()
        me = 2 * x + y
        sibling = (x, y, 1 - c)

        def remote(i, k, slot, part, to, src=None):
            dst = outs[i].at[slot, part]
            return pltpu.make_async_remote_copy(
                src_ref=dst if src is None else src, dst_ref=dst,
                send_sem=send_sems.at[i, k], recv_sem=recv_sems.at[i, k],
                device_id=to, device_id_type=MESH)

        started = []
        for i in range(n):
            for k, (cx, cy) in enumerate(chips):
                cp = remote(i, k, me, c, (cx, cy, c), src=ins[i].at[c])
                cp.start()
                started.append(cp)
        for i in range(n):
            for k, (cx, cy) in enumerate(chips):
                remote(i, k, 2 * cx + cy, c, (x, y, c)).wait_recv()
                cp = remote(i, 3 + k, 2 * cx + cy, c, sibling)
                cp.start()
                started.append(cp)
        for i in range(n):
            for k, (cx, cy) in enumerate(chips):
                remote(i, 3 + k, 2 * cx + cy, 1 - c, (x, y, c)).wait_recv()
        for cp in started:
            cp.wait_send()

    outs = pl.pallas_call(
        body, name="all_gather_weights",
        out_shape=[jax.ShapeDtypeStruct((4,) + s.shape, s.dtype) for s in split],
        in_specs=[ANY] * n, out_specs=[ANY] * n,
        scratch_shapes=[pltpu.SemaphoreType.DMA((n, 6)), pltpu.SemaphoreType.DMA((n, 6))],
    )(*split)
    me =2 * lax.axis_index("x") + lax.axis_index("y")
    outs = [lax.dynamic_update_slice(o, s[None], (me, 0, 0, 0)) for o, s in zip(outs, split)]
    return [o.reshape((4,) + s.shape) for o, s in zip(outs, shards)]


def _all_gather_async(shards):
    n = len(shards)
    split = [s.reshape(2, s.shape[0] // 2, s.shape[1]) for s in shards]
    ins = [jax.new_ref(s, memory_space=pltpu.MemorySpace.HBM) for s in split]
    outs = [jax.empty_ref(jax.ShapeDtypeStruct((4,) + s.shape, s.dtype), memory_space=pltpu.MemorySpace.HBM)
            for s in split]

    @pl.kernel(mesh=plsc.ScalarSubcoreMesh(axis_name="sequencer", num_cores=1), name="all_gather_rest",
               scratch_types=(pltpu.SemaphoreType.DMA((n, 6)), pltpu.SemaphoreType.DMA((n, 6))),
               compiler_params=pltpu.CompilerParams(collective_id=1))
    def launch(send_sems, recv_sems):
        x, y, c, chips = _place()
        me = 2 * x + y
        sibling = (x, y, 1 - c)
        barrier = pltpu.get_barrier_semaphore()
        for peer in [(cx, cy, c) for cx, cy in chips] + [sibling]:
            pl.semaphore_signal(barrier, inc=1, device_id=peer, device_id_type=MESH)
        pl.semaphore_wait(barrier, 4)

        def remote(i, k, slot, part, to, src=None):
            dst = outs[i].at[slot, part]
            return pltpu.make_async_remote_copy(
                src_ref=dst if src is None else src, dst_ref=dst,
                send_sem=send_sems.at[i, k], recv_sem=recv_sems.at[i, k],
                device_id=to, device_id_type=MESH)

        started = []
        for i in range(n):
            for k, (cx, cy) in enumerate(chips):
                cp = remote(i, k, me, c, (cx, cy, c), src=ins[i].at[c])
                cp.start()
                started.append(cp)
        for i in range(n):
            for k, (cx, cy) in enumerate(chips):
                remote(i, k, 2 * cx + cy, c, (x, y, c)).wait_recv()
                cp = remote(i, 3 + k, 2 * cx + cy, c, sibling)
                cp.start()
                started.append(cp)
        for i in range(n):
            for k, (cx, cy) in enumerate(chips):
                remote(i, 3 + k, 2 * cx + cy, 1 - c, (x, y, c)).wait_recv()
        for cp in started:
            cp.wait_send()

    launch()
    raw = [o[...] for o in outs]

    def finish(after):
        arrived, _ = lax.optimization_barrier((raw, after))
        me = 2 * lax.axis_index("x") + lax.axis_index("y")
        gathered = [lax.dynamic_update_slice(a, s[None], (me, 0, 0, 0)) for a, s in zip(arrived, split)]
        return [g.reshape((4,) + s.shape) for g, s in zip(gathered, shards)]

    return finish


def _pair_send_halves(name, grads):
    n = len(grads)

    def body(*refs):
        ins, outs = refs[:n], refs[n:2 * n]
        send_sems, recv_sems = refs[2 * n:]
        x, y, c, _ = _place()
        cps = []
        for i in range(n):
            half = ins[i].shape[1] // 2
            cp = pltpu.make_async_remote_copy(
                src_ref=ins[i].at[:, pl.ds((1 - c) * half, half)], dst_ref=outs[i],
                send_sem=send_sems.at[i], recv_sem=recv_sems.at[i],
                device_id=(x, y, 1 - c), device_id_type=MESH)
            cp.start()
            cps.append(cp)
        for cp in cps:
            cp.wait()

    return pl.pallas_call(
        body, name=name,
        out_shape=[jax.ShapeDtypeStruct((4, g.shape[1] // 2, g.shape[2]), g.dtype) for g in grads],
        in_specs=[ANY] * n, out_specs=[ANY] * n,
        scratch_shapes=[pltpu.SemaphoreType.DMA((n,)), pltpu.SemaphoreType.DMA((n,))],
    )(*grads)


def _chip_scatter(name, parts):
    n = len(parts)

    def body(*refs):
        _scatter_copies(refs[:n], refs[n:2 * n], *refs[2 * n:])

    arrived = pl.pallas_call(
        body, name=name,
        out_shape=[jax.ShapeDtypeStruct(p.shape, p.dtype) for p in parts],
        in_specs=[ANY] * n, out_specs=[ANY] * n,
        scratch_shapes=[pltpu.SemaphoreType.DMA((n, 3)), pltpu.SemaphoreType.DMA((n, 3))],
    )(*parts)
    return _own_slots(parts, arrived)


def _scatter_copies(ins, outs, send_sems, recv_sems):
    x, y, c, chips = _place()
    me = 2 * x + y
    sends = []
    for i in range(len(ins)):
        for k, (cx, cy) in enumerate(chips):
            cp = pltpu.make_async_remote_copy(
                src_ref=ins[i].at[2 * cx + cy], dst_ref=outs[i].at[me],
                send_sem=send_sems.at[i, k], recv_sem=recv_sems.at[i, k],
                device_id=(cx, cy, c), device_id_type=MESH)
            cp.start()
            sends.append(cp)
    for i in range(len(ins)):
        for k, (cx, cy) in enumerate(chips):
            got = outs[i].at[2 * cx + cy]
            pltpu.make_async_remote_copy(
                src_ref=got, dst_ref=got, send_sem=send_sems.at[i, k], recv_sem=recv_sems.at[i, k],
                device_id=(x, y, c), device_id_type=MESH).wait_recv()
    for cp in sends:
        cp.wait_send()


def _own_slots(parts, arrived):
    me = 2 * lax.axis_index("x") + lax.axis_index("y")
    return [lax.dynamic_update_slice(a, lax.dynamic_slice_in_dim(p, me, 1, axis=0), (me, 0, 0))
            for p, a in zip(parts, arrived)]


def _chip_scatter_async(name, parts, collective_id):
    n = len(parts)
    ins = [jax.new_ref(p, memory_space=pltpu.MemorySpace.HBM) for p in parts]
    outs = [jax.empty_ref(jax.ShapeDtypeStruct(p.shape, p.dtype), memory_space=pltpu.MemorySpace.HBM) for p in parts]

    @pl.kernel(mesh=plsc.ScalarSubcoreMesh(axis_name="sequencer", num_cores=1), name=name,
               scratch_types=(pltpu.SemaphoreType.DMA((n, 3)), pltpu.SemaphoreType.DMA((n, 3))),
               compiler_params=pltpu.CompilerParams(collective_id=collective_id))
    def launch(send_sems, recv_sems):
        x, y, c, chips = _place()
        barrier = pltpu.get_barrier_semaphore()
        for cx, cy in chips:
            pl.semaphore_signal(barrier, inc=1, device_id=(cx, cy, c), device_id_type=MESH)
        pl.semaphore_wait(barrier, 3)
        _scatter_copies(ins, outs, send_sems, recv_sems)

    launch()
    return _own_slots(parts, [o[...] for o in outs])


def _pair_swap(name, halves):
    n = len(halves)

    def body(*refs):
        ins, outs = refs[:n], refs[n:2 * n]
        send_sems, recv_sems = refs[2 * n:]
        x, y, c, _ = _place()
        cps = []
        for i in range(n):
            cp = pltpu.make_async_remote_copy(
                src_ref=ins[i], dst_ref=outs[i], send_sem=send_sems.at[i], recv_sem=recv_sems.at[i],
                device_id=(x, y, 1 - c), device_id_type=MESH)
            cp.start()
            cps.append(cp)
        for cp in cps:
            cp.wait()

    return pl.pallas_call(
        body, name=name,
        out_shape=[jax.ShapeDtypeStruct(h.shape, h.dtype) for h in halves],
        in_specs=[ANY] * n, out_specs=[ANY] * n,
        scratch_shapes=[pltpu.SemaphoreType.DMA((n,)), pltpu.SemaphoreType.DMA((n,))],
    )(*halves)


def _row_block(rows, cols, n_bufs, budget=20 * 1024 * 1024):
    best = min(rows, 16)
    for b in range(16, rows + 1, 16):
        if rows % b == 0 and 2 * n_bufs * b * cols * 4 <= budget:
            best = b
    return best


def _pair_add(tag, grad, got, c_arr, out_dtype):
    _, rows, cols = grad.shape
    half = rows // 2
    bh = _row_block(half, cols, 3)
    nb = half // bh

    def body(c_ref, g_ref, a_ref, o_ref):
        o_ref[...] = (g_ref[...] + a_ref[...]).astype(out_dtype)

    return pl.pallas_call(
        body, name=f"pair_add_{tag}",
        out_shape=jax.ShapeDtypeStruct((4, half, cols), out_dtype),
        grid_spec=pltpu.PrefetchScalarGridSpec(
            num_scalar_prefetch=1, grid=(4, nb),
            in_specs=[pl.BlockSpec((None, bh, cols), lambda j, r, c: (j, c[0] * nb + r, 0)),
                      pl.BlockSpec((None, bh, cols), lambda j, r, c: (j, r, 0))],
            out_specs=pl.BlockSpec((None, bh, cols), lambda j, r, c: (j, r, 0))),
        compiler_params=_params(("parallel", "parallel")),
    )(c_arr, grad, got)


def _chip_add(tag, parts):
    _, half, cols = parts.shape
    bh = _row_block(half, cols, 5)

    def body(p_ref, o_ref):
        a, b, c, d = [p_ref[j].astype(F32) for j in range(4)]
        o_ref[...] = ((a + b) + c) + d

    return pl.pallas_call(
        body, name=f"chip_add_{tag}",
        out_shape=jax.ShapeDtypeStruct((half, cols), F32),
        grid=(half // bh,),
        in_specs=[pl.BlockSpec((4, bh, cols), lambda r: (0, r, 0))],
        out_specs=pl.BlockSpec((bh, cols), lambda r: (r, 0)),
        compiler_params=_params(("parallel",)),
    )(parts)


def _adamw(tag, w, mine, theirs, m, v, c_arr):
    rows, cols = w.shape
    half = rows // 2
    br = _row_block(half, cols, 9)
    nb = half // br

    def body(c_ref, w_ref, a_ref, b_ref, m_ref, v_ref, g_ref, d_ref, mo_ref, vo_ref):
        own = (pl.program_id(0) // nb) == c_ref[0]
        g = jnp.where(own, a_ref[...], b_ref[...])
        g_ref[...] = g
        m_new = ADAM_B1 * m_ref[...] + (1.0 - ADAM_B1) * g
        v_new = ADAM_B2 * v_ref[...] + (1.0 - ADAM_B2) * (g * g)
        m_hat = m_new / (1.0 - ADAM_B1 ** ADAM_STEP)
        v_hat = v_new / (1.0 - ADAM_B2 ** ADAM_STEP)
        d_ref[...] = -ADAM_LR * (m_hat / (jnp.sqrt(v_hat) + ADAM_EPS) + ADAM_WD * w_ref[...])
        mo_ref[...] = m_new
        vo_ref[...] = v_new

    spec = pl.BlockSpec((br, cols), lambda r, c: (r, 0))
    mine_spec = pl.BlockSpec((br, cols), lambda r, c: (jnp.clip(r - c[0] * nb, 0, nb - 1), 0))
    theirs_spec = pl.BlockSpec((br, cols), lambda r, c: (jnp.clip(r - (1 - c[0]) * nb, 0, nb - 1), 0))
    return pl.pallas_call(
        body, name=f"adamw_{tag}",
        out_shape=[jax.ShapeDtypeStruct((rows, cols), F32)] * 4,
        grid_spec=pltpu.PrefetchScalarGridSpec(
            num_scalar_prefetch=1, grid=(rows // br,),
            in_specs=[spec, mine_spec, theirs_spec, spec, spec], out_specs=[spec] * 4),
        compiler_params=_params(("arbitrary",)),
    )(c_arr, w, mine, theirs, m, v)


def _matmul(name, x, w, out_shape, grid, x_spec, w_spec, o_spec, *, nt=False, vmem=None):
    nk = grid[2]
    acc_shape = tuple(d for d in o_spec.block_shape if d is not None)

    def body(x_ref, w_ref, o_ref, acc_ref):
        k = pl.program_id(2)
        part = _dot_nt(x_ref[...], w_ref[...]) if nt else _dot(x_ref[...], w_ref[...])
        if nk == 1:
            o_ref[...] = part.astype(o_ref.dtype)
        else:
            @pl.when(k == 0)
            def _():
                acc_ref[...] = part

            @pl.when(k > 0)
            def _():
                acc_ref[...] += part

            @pl.when(k == nk - 1)
            def _():
                o_ref[...] = acc_ref[...].astype(o_ref.dtype)

    return pl.pallas_call(
        body, name=name, out_shape=out_shape, grid=grid,
        in_specs=[x_spec, w_spec], out_specs=o_spec,
        scratch_shapes=[pltpu.VMEM(acc_shape if nk > 1 else (8, 128), F32)],
        compiler_params=_params(("parallel", "parallel", "arbitrary"), vmem),
    )(x, w)


def _weight_grad(name, xt, dy, bn, out_rows=None):
    m, t = xt.shape
    n = dy.shape[1]
    bm = m if out_rows is None else out_rows
    bk = _k_tile(t)
    return _matmul(
        name, xt, dy, jax.ShapeDtypeStruct((m, n), F32), (m // bm, n // bn, t // bk),
        pl.BlockSpec((bm, bk), lambda a, b, k: (a, k)),
        pl.BlockSpec((bk, bn), lambda a, b, k: (k, b)),
        pl.BlockSpec((bm, bn), lambda a, b, k: (a, b)), vmem=VMEM_BIG)


def _weight_grad_t(name, xt, dy):
    m, t = xt.shape
    n = dy.shape[1]
    bn = min(n, 512)
    bk = _k_tile(t)
    nk = t // bk

    def body(x_ref, dy_ref, o_ref, acc_ref):
        k = pl.program_id(1)
        part = _dot(x_ref[...], dy_ref[...].astype(BF16))

        @pl.when(k == 0)
        def _():
            acc_ref[...] = part

        @pl.when(k > 0)
        def _():
            acc_ref[...] += part

        @pl.when(k == nk - 1)
        def _():
            o_ref[...] = acc_ref[...].T

    return pl.pallas_call(
        body, name=name, out_shape=jax.ShapeDtypeStruct((n, m), F32), grid=(n // bn, nk),
        in_specs=[pl.BlockSpec((m, bk), lambda b, k: (0, k)), pl.BlockSpec((bk, bn), lambda b, k: (k, b))],
        out_specs=pl.BlockSpec((bn, m), lambda b, k: (b, 0)),
        scratch_shapes=[pltpu.VMEM((m, bn), F32)],
        compiler_params=_params(("parallel", "arbitrary"), VMEM_BIG),
    )(xt, dy)


def _mix_in_bwd(pieces, wt, h, g, dh_in):
    t, d = h.shape
    tm = ROW_TILE // 2
    widths = [p.shape[1] for p in pieces]
    n = len(pieces)

    def body(*refs):
        dy_refs, (w_ref, h_ref, g_ref, dhi_ref, dh_ref, dg_ref) = refs[:n], refs[n:]

        @pl.when(pl.program_id(0) == 0)
        def _():
            dg_ref[...] = jnp.zeros_like(dg_ref)

        dn, off = None, 0
        for dy_ref, wd in zip(dy_refs, widths):
            part = _dot(dy_ref[...].astype(BF16), w_ref[off:off + wd, :])
            dn = part if dn is None else dn + part
            off += wd
        dx, dg = _rms_bwd(h_ref[...], g_ref[...], dn)
        dh_ref[...] = dhi_ref[...] + dx
        dg_ref[...] += dg

    row = pl.BlockSpec((tm, d), lambda i: (i, 0))
    vec = pl.BlockSpec((1, d), lambda i: (0, 0))
    return pl.pallas_call(
        body, name="mix_in_bwd",
        out_shape=[jax.ShapeDtypeStruct((t, d), F32), jax.ShapeDtypeStruct((1, d), F32)],
        grid=(t // tm,),
        in_specs=[pl.BlockSpec((tm, wd), lambda i: (i, 0)) for wd in widths]
        + [pl.BlockSpec(wt.shape, lambda i: (0, 0)), row, vec, row],
        out_specs=[row, vec],
        compiler_params=_params(("arbitrary",), VMEM_BIG),
    )(*pieces, wt, h, g, dh_in)


def _norm_fwd(name, h, g):
    t, d = h.shape
    tm = ROW_TILE

    def body(h_ref, g_ref, n_ref, nt_ref):
        x = h_ref[...]
        y = x * _rstd(x) * g_ref[...]
        n_ref[...] = y.astype(BF16)
        nt_ref[...] = y.T.astype(BF16)

    return pl.pallas_call(
        body, name=name,
        out_shape=[jax.ShapeDtypeStruct((t, d), BF16), jax.ShapeDtypeStruct((d, t), BF16)],
        grid=(t // tm,),
        in_specs=[pl.BlockSpec((tm, d), lambda i: (i, 0)), pl.BlockSpec((1, d), lambda i: (0, 0))],
        out_specs=[pl.BlockSpec((tm, d), lambda i: (i, 0)), pl.BlockSpec((d, tm), lambda i: (0, i))],
        compiler_params=_params(("parallel",)),
    )(h, g)


def _slot_of(kk):
    return (kk % 2) * 2 + kk // 2


def _ffn_in(name, n, w4):
    t, d = n.shape
    cw = w4.shape[2]
    tm = ROW_TILE

    def body(x_ref, wg_ref, wu_ref, ab_ref, s_ref, st_ref):
        x = x_ref[...]
        a = _dot(x, wg_ref[...])
        b = _dot(x, wu_ref[...])
        ab_ref[:, :cw] = a.astype(BF16)
        ab_ref[:, cw:] = b.astype(BF16)
        s = a * _sigmoid(a) * b
        s_ref[...] = s.astype(BF16)
        st_ref[...] = s.T.astype(BF16)

    return pl.pallas_call(
        body, name=name,
        out_shape=[jax.ShapeDtypeStruct((t, 4 * cw), BF16), jax.ShapeDtypeStruct((t, 2 * cw), BF16),
                   jax.ShapeDtypeStruct((2 * cw, t), BF16)],
        grid=(2, t // tm),
        in_specs=[pl.BlockSpec((tm, d), lambda j, i: (i, 0)),
                  pl.BlockSpec((None, d, cw), lambda j, i: (j, 0, 0)),
                  pl.BlockSpec((None, d, cw), lambda j, i: (2 + j, 0, 0))],
        out_specs=[pl.BlockSpec((tm, 2 * cw), lambda j, i: (i, j)),
                   pl.BlockSpec((tm, cw), lambda j, i: (i, j)),
                   pl.BlockSpec((cw, tm), lambda j, i: (j, i))],
        compiler_params=_params(("parallel", "parallel"), VMEM_BIG),
    )(n, w4, w4)


def _mm_resid_norm(name, x, w, h, g_post, alpha, g_next):
    t, kdim = x.shape
    d = w.shape[1]
    tm = ROW_TILE
    with_next = g_next is not None

    def body(x_ref, w_ref, h_ref, gp_ref, gn_ref, f_ref, hn_ref, *rest):
        f = _dot(x_ref[...], w_ref[...])
        f_ref[...] = f
        hn = h_ref[...] + alpha * (f * _rstd(f) * gp_ref[...])
        hn_ref[...] = hn
        if with_next:
            y = hn * _rstd(hn) * gn_ref[...]
            rest[0][...] = y.astype(BF16)
            rest[1][...] = y.T.astype(BF16)

    row = lambda i: (i, 0)
    vec = pl.BlockSpec((1, d), lambda i: (0, 0))
    out_shape = [jax.ShapeDtypeStruct((t, d), F32), jax.ShapeDtypeStruct((t, d), F32)]
    out_specs = [pl.BlockSpec((tm, d), row), pl.BlockSpec((tm, d), row)]
    if with_next:
        out_shape += [jax.ShapeDtypeStruct((t, d), BF16), jax.ShapeDtypeStruct((d, t), BF16)]
        out_specs += [pl.BlockSpec((tm, d), row), pl.BlockSpec((d, tm), lambda i: (0, i))]
    return pl.pallas_call(
        body, name=name, out_shape=out_shape, grid=(t // tm,),
        in_specs=[pl.BlockSpec((tm, kdim), row), pl.BlockSpec((kdim, d), lambda i: (0, 0)),
                  pl.BlockSpec((tm, d), row), vec, vec],
        out_specs=out_specs,
        compiler_params=_params(("parallel",), VMEM_BIG),
    )(x, w, h, g_post, g_post if g_next is None else g_next)


def _in_proj(u, w):
    t, d = u.shape
    nz = w.shape[0]
    nq = 3 * ATTN_W
    tm = ROW_TILE // 2

    def body(u_ref, w_ref, qkv_ref, z_ref):
        qkv_ref[...] = _dot_nt(u_ref[...], w_ref[0:nq, :]).astype(BF16)
        z_ref[...] = _dot_nt(u_ref[...], w_ref[nq:, :])

    return pl.pallas_call(
        body, name="mix_in_proj",
        out_shape=[jax.ShapeDtypeStruct((t, nq), BF16), jax.ShapeDtypeStruct((t, nz - nq), F32)],
        grid=(t // tm,),
        in_specs=[pl.BlockSpec((tm, d), lambda i: (i, 0)), pl.BlockSpec((nz, d), lambda i: (0, 0))],
        out_specs=[pl.BlockSpec((tm, nq), lambda i: (i, 0)), pl.BlockSpec((tm, nz - nq), lambda i: (i, 0))],
        compiler_params=_params(("parallel",), VMEM_BIG),
    )(u, w)


def _gate_prep(z, b_pad, f_col):
    t = z.shape[0]
    tm = ROW_TILE

    def body(z_ref, b_ref, f_ref, carry_ref):
        i = pl.program_id(0)

        @pl.when(i == 0)
        def _():
            carry_ref[...] = jnp.zeros_like(carry_ref)

        xs = z_ref[...] + b_ref[...]
        logf = jnp.minimum(xs, 0.0) - jnp.log(1.0 + jnp.exp(-jnp.abs(xs)))
        row = i * tm + lax.broadcasted_iota(jnp.int32, (tm, 1), 0)
        logf = jnp.where(row >= ROW_PAD, logf, 0.0)
        tri = (lax.broadcasted_iota(jnp.int32, (tm, tm), 0) >= lax.broadcasted_iota(jnp.int32, (tm, tm), 1))
        f = jnp.dot(tri.astype(F32), logf, preferred_element_type=F32, precision=lax.Precision.HIGHEST)
        f = f + carry_ref[0:1, :]
        f_ref[...] = f
        carry_ref[...] = jnp.broadcast_to(f[tm - 1:tm, :], carry_ref.shape)

    return pl.pallas_call(
        body, name="forget_gate_cumsum", out_shape=jax.ShapeDtypeStruct((t, 128), F32),
        grid=(t // tm,),
        in_specs=[pl.BlockSpec((tm, 128), lambda i: (i, f_col // 128)), pl.BlockSpec((1, 128), lambda i: (0, 0))],
        out_specs=pl.BlockSpec((tm, 128), lambda i: (i, 0)),
        scratch_shapes=[pltpu.VMEM((8, 128), F32)],
        compiler_params=_params(("arbitrary",)),
    )(z, b_pad)


def _lane_halves():
    lane = lax.broadcasted_iota(jnp.int32, (1, 128), 1)
    return lane < HEAD_DIM


def _causal_mask(tq, tk, row0=0):
    row = row0 + lax.broadcasted_iota(jnp.int32, (tq, 1), 0)
    col = lax.broadcasted_iota(jnp.int32, (1, tk), 1)
    return col <= row


def _lane_one(lane):
    return (lax.broadcasted_iota(jnp.int32, (1, 128), 1) == lane).astype(BF16)


def _split3(x):
    hi = x.astype(BF16)
    rest = x - hi.astype(F32)
    mid = rest.astype(BF16)
    return hi, mid, (rest - mid.astype(F32)).astype(BF16)


def _split3_glue(x):
    hi = lax.reduce_precision(x, 8, 7)
    mid = lax.reduce_precision(x - hi, 8, 7)
    lo = lax.reduce_precision((x - hi) - mid, 8, 7)
    return hi.astype(BF16), mid.astype(BF16), lo.astype(BF16)


def _aug_pairs(cols):
    t = cols[0].shape[0]
    a = jnp.pad(jnp.stack(cols, axis=2), ((0, 0), (0, 0), (0, HEAD_DIM - len(cols))))
    a = a.reshape(t, 4, 2, HEAD_DIM)[:, :, ::-1, :]
    return jnp.transpose(a.reshape(t, 4, 128), (1, 0, 2))


def _attn_bias_operands(f_heads, lse_heads=None):
    t = f_heads.shape[0]
    one = jnp.ones((t, HEADS), BF16)
    row = lax.broadcasted_iota(jnp.int32, (t, 1), 0)
    fq = _split3_glue(f_heads)
    fk = _split3_glue(jnp.where(row < ROW_PAD, 1e9, f_heads))
    q_cols = list(fq) + [one] * 3
    k_cols = [one] * 3 + [-c for c in fk]
    if lse_heads is not None:
        q_cols += [-c for c in _split3_glue(lse_heads)]
        k_cols += [one] * 3
    return _aug_pairs(q_cols), _aug_pairs(k_cols)


def _attn_steps(nq, by_key):
    if by_key:
        pairs = [(qi, ki) for ki in range(nq) for qi in range(ki, nq)]
    else:
        pairs = [(qi, ki) for qi in range(nq) for ki in range(qi + 1)]
    return (jnp.array([p[0] for p in pairs], jnp.int32), jnp.array([p[1] for p in pairs], jnp.int32))


def _attn_fwd(z, aug_q, aug_k):
    t = z.shape[0]
    tq = tk = ROW_TILE
    nq = t // tq
    q_tab, k_tab = _attn_steps(nq, by_key=False)

    def body(qt_ref, kt_ref, q_ref, k_ref, v_ref, aq_ref, ak_ref, o_ref, lse_ref, m_ref, l_ref, acc_ref):
        step = pl.program_id(1)
        qi, ki = qt_ref[step], kt_ref[step]

        @pl.when(ki == 0)
        def _():
            m_ref[...] = jnp.full_like(m_ref, NEG)
            l_ref[...] = jnp.zeros_like(l_ref)
            acc_ref[...] = jnp.zeros_like(acc_ref)

        def sweep(diagonal):
            first = _lane_halves()
            q = (q_ref[...] * (HEAD_DIM ** -0.5)).astype(BF16)
            k = k_ref[...].astype(BF16)
            v = v_ref[...].astype(BF16)
            aq, ak = aq_ref[...], ak_ref[...]
            halves = (first, jnp.logical_not(first))
            qa = [jnp.where(lanes, q, aq) for lanes in halves]
            ka = [jnp.where(lanes, k, ak) for lanes in halves]
            va = [jnp.where(lanes, v, _lane_one(a0)) for lanes, a0 in zip(halves, (HEAD_DIM, 0))]
            chains = [(hh, r) for r in range(ATTN_ROW_PARTS) for hh in range(2)]
            rp = tq // ATTN_ROW_PARTS
            rows = [slice(r * rp, (r + 1) * rp) for _, r in chains]
            s = [_dot_nt(qa[hh][rw], ka[hh]) for (hh, _), rw in zip(chains, rows)]
            if diagonal:
                s = [jnp.where(_causal_mask(rp, tk, rw.start), s_c, NEG) for s_c, rw in zip(s, rows)]
            m_prev = [m_ref[rw, hh * HEAD_DIM:hh * HEAD_DIM + 1] for (hh, _), rw in zip(chains, rows)]
            m_new = [jnp.maximum(mp, jnp.max(s_c, axis=1, keepdims=True)) for mp, s_c in zip(m_prev, s)]
            p = [jnp.exp(s_c - m_c).astype(BF16) for s_c, m_c in zip(s, m_new)]
            pv = [_dot(p_c, va[hh]) for p_c, (hh, _) in zip(p, chains)]
            alpha = [jnp.exp(mp - m_c) for mp, m_c in zip(m_prev, m_new)]
            for r in range(ATTN_ROW_PARTS):
                (m0, m1), (al0, al1), (pv0, pv1) = [x[2 * r:2 * r + 2] for x in (m_new, alpha, pv)]
                rw = rows[2 * r]
                l0 = al0 * l_ref[rw, 0:1] + pv0[:, HEAD_DIM:HEAD_DIM + 1]
                l1 = al1 * l_ref[rw, HEAD_DIM:HEAD_DIM + 1] + pv1[:, 0:1]
                acc_ref[rw, :] = acc_ref[rw, :] * jnp.where(first, al0, al1) + jnp.where(first, pv0, pv1)
                m_ref[rw, :] = jnp.where(first, m0, m1)
                l_ref[rw, :] = jnp.where(first, l0, l1)

        @pl.when(ki < qi)
        def _():
            sweep(False)

        @pl.when(ki == qi)
        def _():
            sweep(True)
            o_ref[...] = acc_ref[...] / l_ref[...]
            lse_ref[...] = m_ref[...] + jnp.log(l_ref[...])

    return pl.pallas_call(
        body, name="attention_fwd",
        out_shape=[jax.ShapeDtypeStruct((t, ATTN_W), F32), jax.ShapeDtypeStruct((t, ATTN_W), F32)],
        grid_spec=pltpu.PrefetchScalarGridSpec(
            num_scalar_prefetch=2, grid=(4, int(q_tab.shape[0])),
            in_specs=[pl.BlockSpec((tq, 128), lambda p, s, qt, kt: (qt[s], p)),
                      pl.BlockSpec((tk, 128), lambda p, s, qt, kt: (kt[s], 4 + p)),
                      pl.BlockSpec((tk, 128), lambda p, s, qt, kt: (kt[s], 8 + p)),
                      pl.BlockSpec((None, tq, 128), lambda p, s, qt, kt: (p, qt[s], 0)),
                      pl.BlockSpec((None, tk, 128), lambda p, s, qt, kt: (p, kt[s], 0))],
            out_specs=[pl.BlockSpec((tq, 128), lambda p, s, qt, kt: (qt[s], p)),
                       pl.BlockSpec((tq, 128), lambda p, s, qt, kt: (qt[s], p))],
            scratch_shapes=[pltpu.VMEM((tq, 128), F32)] * 3),
        compiler_params=_params(("parallel", "arbitrary")),
    )(q_tab, k_tab, z, z, z, aug_q, aug_k)


def _attn_bwd(z, aug_q, aug_k, o, do):
    t = z.shape[0]
    tq = tk = ROW_TILE
    nq = t // tq
    q_tab, k_tab = _attn_steps(nq, by_key=True)
    tn = (((0,), (0,)), ((), ()))

    def body(qt_ref, kt_ref, q_ref, k_ref, v_ref, aq_ref, ak_ref, o_ref, do_ref,
             dq_ref, dk_ref, dv_ref, dfk_ref, dfq_ref):
        step = pl.program_id(1)
        qi, ki = qt_ref[step], kt_ref[step]
        rows = pl.ds(pl.multiple_of(qi * tq, tq), tq)

        @pl.when(ki == 0)
        def _():
            dq_ref[rows, :] = jnp.zeros((tq, 128), F32)
            dfq_ref[rows, :] = jnp.zeros((tq, 128), F32)

        @pl.when(qi == ki)
        def _():
            dk_ref[...] = jnp.zeros_like(dk_ref)
            dv_ref[...] = jnp.zeros_like(dv_ref)
            dfk_ref[...] = jnp.zeros_like(dfk_ref)

        def sweep(diagonal):
            first = _lane_halves()
            lane = lax.broadcasted_iota(jnp.int32, (1, 128), 1)
            scale = HEAD_DIM ** -0.5
            q = (q_ref[...] * scale).astype(BF16)
            k = k_ref[...].astype(BF16)
            v = v_ref[...].astype(BF16)
            do_ = do_ref[...]
            do16 = do_.astype(BF16)
            od = o_ref[...] * do_
            aq, ak = aq_ref[...], ak_ref[...]
            halves = (first, jnp.logical_not(first))
            a0, a1 = HEAD_DIM, 0
            dos, vs = [], []
            for lanes, a in zip(halves, (a0, a1)):
                d_hi, d_mid, d_lo = _split3(jnp.sum(jnp.where(lanes, od, 0.0), axis=1, keepdims=True))
                minus_delta = jnp.where(lane == a, -d_hi, jnp.where(lane == a + 1, -d_mid,
                                        jnp.where(lane == a + 2, -d_lo, jnp.zeros((), BF16))))
                dos.append(jnp.where(lanes, do16, minus_delta))
                vs.append(jnp.where(lanes, v, ((lane >= a) & (lane < a + 3)).astype(BF16)))
            s = [_dot_nt(jnp.where(lanes, q, aq), jnp.where(lanes, k, ak)) for lanes in halves]
            dp = [_dot_nt(do_h, v_h) for do_h, v_h in zip(dos, vs)]
            p = [jnp.exp(s_h) for s_h in s]
            if diagonal:
                p = [jnp.where(_causal_mask(tq, tk), p_h, 0.0) for p_h in p]
            ds16 = [(p_h * dp_h).astype(BF16) for p_h, dp_h in zip(p, dp)]
            dv0, dv1 = [lax.dot_general(p_h.astype(BF16), jnp.where(lanes, do16, jnp.zeros((), BF16)), tn,
                                        preferred_element_type=F32) for p_h, lanes in zip(p, halves)]
            dk0, dk1 = [lax.dot_general(ds_h, jnp.where(lanes, q, _lane_one(a)), tn, preferred_element_type=F32)
                        for ds_h, lanes, a in zip(ds16, halves, (a0, a1))]
            dq0, dq1 = [_dot(ds_h, jnp.where(lanes, k, _lane_one(a))) for ds_h, lanes, a in zip(ds16, halves, (a0, a1))]
            dq_ref[rows, :] += jnp.where(first, dq0, dq1) * scale
            dfq_ref[rows, :] += jnp.where(first, dq0[:, a0:a0 + 1], dq1[:, a1:a1 + 1])
            dk_ref[...] += jnp.where(first, dk0, dk1)
            dfk_ref[...] += jnp.where(first, dk0[:, a0:a0 + 1], dk1[:, a1:a1 + 1])
            dv_ref[...] += dv0 + dv1

        @pl.when(qi > ki)
        def _():
            sweep(False)

        @pl.when(qi == ki)
        def _():
            sweep(True)

    qrow = lambda p, s, qt, kt: (qt[s], p)
    krow = lambda p, s, qt, kt: (kt[s], p)
    return pl.pallas_call(
        body, name="attention_bwd",
        out_shape=[jax.ShapeDtypeStruct((t, ATTN_W), F32)] * 5,
        grid_spec=pltpu.PrefetchScalarGridSpec(
            num_scalar_prefetch=2, grid=(4, int(q_tab.shape[0])),
            in_specs=[pl.BlockSpec((tq, 128), qrow),
                      pl.BlockSpec((tk, 128), lambda p, s, qt, kt: (kt[s], 4 + p)),
                      pl.BlockSpec((tk, 128), lambda p, s, qt, kt: (kt[s], 8 + p)),
                      pl.BlockSpec((None, tq, 128), lambda p, s, qt, kt: (p, qt[s], 0)),
                      pl.BlockSpec((None, tk, 128), lambda p, s, qt, kt: (p, kt[s], 0)),
                      pl.BlockSpec((tq, 128), qrow), pl.BlockSpec((tq, 128), qrow)],
            out_specs=[pl.BlockSpec((t, 128), lambda p, s, qt, kt: (0, p)),
                       pl.BlockSpec((tk, 128), krow), pl.BlockSpec((tk, 128), krow), pl.BlockSpec((tk, 128), krow),
                       pl.BlockSpec((t, 128), lambda p, s, qt, kt: (0, p))]),
        compiler_params=_params(("parallel", "arbitrary"), VMEM_BIG),
    )(q_tab, k_tab, z, z, z, aug_q, aug_k, o, do)


def _shifted(prev_rows, x, shift):
    tm = x.shape[0]
    return pltpu.roll(jnp.concatenate([prev_rows, x], axis=0), shift, 0)[8:8 + tm]


def _ahead(x, next_rows, shift):
    tm = x.shape[0]
    return pltpu.roll(jnp.concatenate([x, next_rows], axis=0), tm + 8 - shift, 0)[0:tm]


def _conv_col0(z):
    return (z.shape[1] - F_PAD - 3 * CONV_W) // CONV_W


def _conv_specs(tm, c0):
    cols = (c0, c0 + 1, c0 + 2)
    tiles = [pl.BlockSpec((tm, CONV_W), functools.partial(lambda i, c: (i, c), c=c)) for c in cols]
    halos = [pl.BlockSpec((8, CONV_W), functools.partial(lambda i, c: (jnp.maximum(i * (tm // 8) - 1, 0), c), c=c))
             for c in cols]
    return tiles, halos


def _conv_gate(z, conv_w):
    t = z.shape[0]
    tm = ROW_TILE
    nt = t // tm

    def body(cb_ref, cc_ref, ci_ref, hc_ref, hi_ref, w_ref, g_ref, gt_ref):
        i = pl.program_id(0)
        cc = cc_ref[...] * ci_ref[...]
        prev = jnp.where(i > 0, hc_ref[...] * hi_ref[...], 0.0)
        conv = w_ref[0:1, :] * _shifted(prev, cc, 2) + w_ref[1:2, :] * _shifted(prev, cc, 1) + w_ref[2:3, :] * cc
        g = cb_ref[...] * conv
        g_ref[...] = g.astype(BF16)
        gt_ref[...] = g.T.astype(BF16)

    (cb, cc, ci), (_, hc, hi) = _conv_specs(tm, _conv_col0(z))
    return pl.pallas_call(
        body, name="conv_gate_fwd",
        out_shape=[jax.ShapeDtypeStruct((t, CONV_W), BF16), jax.ShapeDtypeStruct((CONV_W, t), BF16)],
        grid=(nt,),
        in_specs=[cb, cc, ci, hc, hi, pl.BlockSpec((8, CONV_W), lambda i: (0, 0))],
        out_specs=[pl.BlockSpec((tm, CONV_W), lambda i: (i, 0)), pl.BlockSpec((CONV_W, tm), lambda i: (0, i))],
        compiler_params=_params(("parallel",)),
    )(z, z, z, z, z, conv_w)


def _conv_bwd(z, dg, conv_w):
    t = z.shape[0]
    tm = ROW_TILE
    nt = t // tm

    def body(cb_ref, cc_ref, ci_ref, hc_ref, hi_ref, dg_ref, ncb_ref, ndg_ref, w_ref, dz_ref, dw_ref):
        i = pl.program_id(0)

        @pl.when(i == 0)
        def _():
            dw_ref[...] = jnp.zeros_like(dw_ref)

        cb, c_c, c_in = cb_ref[...], cc_ref[...], ci_ref[...]
        cc = c_c * c_in
        prev = jnp.where(i > 0, hc_ref[...] * hi_ref[...], 0.0)
        cc1, cc2 = _shifted(prev, cc, 1), _shifted(prev, cc, 2)
        w0, w1, w2 = w_ref[0:1, :], w_ref[1:2, :], w_ref[2:3, :]
        conv = w0 * cc2 + w1 * cc1 + w2 * cc
        dgv = dg_ref[...]
        dconv = dgv * cb
        nxt = jnp.where(i < nt - 1, ndg_ref[...] * ncb_ref[...], 0.0)
        dcc = w2 * dconv + w1 * _ahead(dconv, nxt, 1) + w0 * _ahead(dconv, nxt, 2)
        dz_ref[:, 0:CONV_W] = (dgv * conv).astype(BF16)
        dz_ref[:, CONV_W:2 * CONV_W] = (dcc * c_in).astype(BF16)
        dz_ref[:, 2 * CONV_W:] = (dcc * c_c).astype(BF16)
        dw_ref[0:1, :] += jnp.sum(dconv * cc2, axis=0, keepdims=True)
        dw_ref[1:2, :] += jnp.sum(dconv * cc1, axis=0, keepdims=True)
        dw_ref[2:3, :] += jnp.sum(dconv * cc, axis=0, keepdims=True)

    c0 = _conv_col0(z)
    (cb, cc, ci), (_, hc, hi) = _conv_specs(tm, c0)
    nxt = lambda i, c: (jnp.minimum((i + 1) * (tm // 8), t // 8 - 1), c)
    return pl.pallas_call(
        body, name="conv_gate_bwd",
        out_shape=[jax.ShapeDtypeStruct((t, 3 * CONV_W), BF16), jax.ShapeDtypeStruct((8, CONV_W), F32)],
        grid=(nt,),
        in_specs=[cb, cc, ci, hc, hi, pl.BlockSpec((tm, CONV_W), lambda i: (i, 0)),
                  pl.BlockSpec((8, CONV_W), lambda i: nxt(i, c0)), pl.BlockSpec((8, CONV_W), lambda i: nxt(i, 0)),
                  pl.BlockSpec((8, CONV_W), lambda i: (0, 0))],
        out_specs=[pl.BlockSpec((tm, 3 * CONV_W), lambda i: (i, 0)), pl.BlockSpec((8, CONV_W), lambda i: (0, 0))],
        compiler_params=_params(("arbitrary",)),
    )(z, z, z, z, z, dg, z, dg, conv_w)


def _branch_mix(z, o, g, w_ab, w_cb, d):
    t = z.shape[0]
    tm = ROW_TILE
    ga_col = 0

    def body(o_ref, g_ref, ga_ref, gc_ref, wa_ref, wc_ref, mp_ref, mpt_ref, ot_ref):
        o_ = o_ref[...]
        ya = _dot(o_.astype(BF16), wa_ref[...])
        yc = _dot(g_ref[...], wc_ref[...])
        mp = _sigmoid(ga_ref[...]) * ya + _sigmoid(gc_ref[...]) * yc
        mp_ref[...] = mp.astype(BF16)
        mpt_ref[...] = mp.T.astype(BF16)
        ot_ref[...] = o_.T.astype(BF16)

    return pl.pallas_call(
        body, name="branch_mix_fwd",
        out_shape=[jax.ShapeDtypeStruct((t, d), BF16), jax.ShapeDtypeStruct((d, t), BF16),
                   jax.ShapeDtypeStruct((ATTN_W, t), BF16)],
        grid=(t // tm,),
        in_specs=[pl.BlockSpec((tm, ATTN_W), lambda i: (i, 0)), pl.BlockSpec((tm, CONV_W), lambda i: (i, 0)),
                  pl.BlockSpec((tm, d), lambda i: (i, ga_col)), pl.BlockSpec((tm, d), lambda i: (i, ga_col + 1)),
                  pl.BlockSpec((ATTN_W, d), lambda i: (0, 0)), pl.BlockSpec((CONV_W, d), lambda i: (0, 0))],
        out_specs=[pl.BlockSpec((tm, d), lambda i: (i, 0)), pl.BlockSpec((d, tm), lambda i: (0, i)),
                   pl.BlockSpec((ATTN_W, tm), lambda i: (0, i))],
        compiler_params=_params(("parallel",), VMEM_BIG),
    )(o, g, z, z, w_ab, w_cb)


def _branch_bwd(z, o, g, dmixed, w_out, w_ab, w_cb, d):
    t = z.shape[0]
    tm = ROW_TILE // 2
    ga_col = 0

    def body(dm_ref, o_ref, g_ref, ga_ref, gc_ref, wo_ref, wa_ref, wc_ref, dya_ref, dyc_ref, dgt_ref, do_ref, dg_ref):
        dmp = _dot_nt(dm_ref[...], wo_ref[...])
        ya = _dot(o_ref[...].astype(BF16), wa_ref[...])
        yc = _dot(g_ref[...], wc_ref[...])
        sa, sc = _sigmoid(ga_ref[...]), _sigmoid(gc_ref[...])
        dya = (dmp * sa).astype(BF16)
        dyc = (dmp * sc).astype(BF16)
        dya_ref[...] = dya
        dyc_ref[...] = dyc
        dgt_ref[:, :d] = (dmp * ya * sa * (1.0 - sa)).astype(BF16)
        dgt_ref[:, d:] = (dmp * yc * sc * (1.0 - sc)).astype(BF16)
        do_ref[...] = _dot_nt(dya, wa_ref[...])
        dg_ref[...] = _dot_nt(dyc, wc_ref[...])

    row = lambda i: (i, 0)
    fixed = lambda i: (0, 0)
    return pl.pallas_call(
        body, name="branch_mix_bwd",
        out_shape=[jax.ShapeDtypeStruct((t, d), BF16), jax.ShapeDtypeStruct((t, d), BF16),
                   jax.ShapeDtypeStruct((t, 2 * d), BF16), jax.ShapeDtypeStruct((t, ATTN_W), F32),
                   jax.ShapeDtypeStruct((t, CONV_W), F32)],
        grid=(t // tm,),
        in_specs=[pl.BlockSpec((tm, d), row), pl.BlockSpec((tm, ATTN_W), row), pl.BlockSpec((tm, CONV_W), row),
                  pl.BlockSpec((tm, d), lambda i: (i, ga_col)), pl.BlockSpec((tm, d), lambda i: (i, ga_col + 1)),
                  pl.BlockSpec((d, d), fixed), pl.BlockSpec((ATTN_W, d), fixed), pl.BlockSpec((CONV_W, d), fixed)],
        out_specs=[pl.BlockSpec((tm, d), row), pl.BlockSpec((tm, d), row), pl.BlockSpec((tm, 2 * d), row),
                   pl.BlockSpec((tm, ATTN_W), row), pl.BlockSpec((tm, CONV_W), row)],
        compiler_params=_params(("parallel",), VMEM_BIG),
    )(dmixed, o, g, z, z, w_out, w_ab, w_cb)


def _loss_norm_bwd(h, target, f, g_post, alpha):
    t, d = h.shape
    tm = N_FRONT

    def body(h_ref, t_ref, f_ref, g_ref, dh_ref, df_ref, dg_ref, loss_ref):
        i = pl.program_id(0)

        @pl.when(i == 0)
        def _():
            loss_ref[...] = jnp.zeros_like(loss_ref)
            dg_ref[...] = jnp.zeros_like(dg_ref)

        err = jnp.where(i > 0, h_ref[...] - t_ref[...], 0.0)
        dy = err * (1.0 / d)
        dh_ref[...] = dy
        per_row = jnp.sum(err * err, axis=1, keepdims=True) * (1.0 / d)
        loss_ref[...] += 0.5 * jnp.sum(per_row, axis=0, keepdims=True)
        dx, dg = _rms_bwd(f_ref[...], g_ref[...], dy)
        df_ref[...] = (alpha * dx).astype(BF16)
        dg_ref[...] += alpha * dg

    row = pl.BlockSpec((tm, d), lambda i: (i, 0))
    vec = pl.BlockSpec((1, d), lambda i: (0, 0))
    return pl.pallas_call(
        body, name="loss_and_post_norm_bwd",
        out_shape=[jax.ShapeDtypeStruct((t, d), F32), jax.ShapeDtypeStruct((t, d), BF16),
                   jax.ShapeDtypeStruct((1, d), F32), jax.ShapeDtypeStruct((1, 128), F32)],
        grid=(t // tm,),
        in_specs=[row, pl.BlockSpec((tm, d), lambda i: (jnp.maximum(i - 1, 0), 0)), row, vec],
        out_specs=[row, row, vec, pl.BlockSpec((1, 128), lambda i: (0, 0))],
        compiler_params=_params(("arbitrary",)),
    )(h, target, f, g_post)


def _norm_bwd(name, x, g, dy, alpha):
    t, d = x.shape
    tm = ROW_TILE

    def body(x_ref, g_ref, dy_ref, dx_ref, dg_ref):
        @pl.when(pl.program_id(0) == 0)
        def _():
            dg_ref[...] = jnp.zeros_like(dg_ref)

        dx, dg = _rms_bwd(x_ref[...], g_ref[...], dy_ref[...])
        dx_ref[...] = (alpha * dx).astype(BF16)
        dg_ref[...] += alpha * dg

    row = pl.BlockSpec((tm, d), lambda i: (i, 0))
    vec = pl.BlockSpec((1, d), lambda i: (0, 0))
    return pl.pallas_call(
        body, name=name,
        out_shape=[jax.ShapeDtypeStruct((t, d), BF16), jax.ShapeDtypeStruct((1, d), F32)],
        grid=(t // tm,), in_specs=[row, vec, row], out_specs=[row, vec],
        compiler_params=_params(("arbitrary",)),
    )(x, g, dy)


def _ffn_bwd_mid(name, df, w_out, ab):
    t, d = df.shape
    cw = ab.shape[1] // 4
    tm = ROW_TILE

    def body(df_ref, w_ref, ab_ref, o_ref):
        ds = _dot_nt(df_ref[...], w_ref[...])
        a = ab_ref[:, :cw].astype(F32)
        b = ab_ref[:, cw:].astype(F32)
        sg = _sigmoid(a)
        o_ref[:, :cw] = (ds * b * (sg * (1.0 + a * (1.0 - sg)))).astype(BF16)
        o_ref[:, cw:] = (ds * (a * sg)).astype(BF16)

    return pl.pallas_call(
        body, name=name, out_shape=jax.ShapeDtypeStruct((t, 4 * cw), BF16),
        grid=(2, t // tm),
        in_specs=[pl.BlockSpec((tm, d), lambda j, i: (i, 0)), pl.BlockSpec((cw, d), lambda j, i: (j, 0)),
                  pl.BlockSpec((tm, 2 * cw), lambda j, i: (i, j))],
        out_specs=pl.BlockSpec((tm, 2 * cw), lambda j, i: (i, j)),
        compiler_params=_params(("parallel", "parallel"), VMEM_BIG),
    )(df, w_out, ab)


def _mm_nt_norm_bwd(name, dy, w, h, g, dh_in):
    t, kdim = dy.shape
    d = h.shape[1]
    tm = ROW_TILE // 2
    slots = w.ndim == 3

    def body(dy_ref, w_ref, h_ref, g_ref, dhi_ref, dh_ref, dg_ref):
        @pl.when(pl.program_id(0) == 0)
        def _():
            dg_ref[...] = jnp.zeros_like(dg_ref)

        if slots:
            cw = w_ref.shape[2]
            dn = _dot_nt(dy_ref[:, 0:cw], w_ref[_slot_of(0)])
            for k in range(1, 4):
                dn += _dot_nt(dy_ref[:, k * cw:(k + 1) * cw], w_ref[_slot_of(k)])
        else:
            dn = _dot_nt(dy_ref[...], w_ref[...])
        dx, dg = _rms_bwd(h_ref[...], g_ref[...], dn)
        dh_ref[...] = dhi_ref[...] + dx
        dg_ref[...] += dg

    row = pl.BlockSpec((tm, d), lambda i: (i, 0))
    vec = pl.BlockSpec((1, d), lambda i: (0, 0))
    return pl.pallas_call(
        body, name=name,
        out_shape=[jax.ShapeDtypeStruct((t, d), F32), jax.ShapeDtypeStruct((1, d), F32)],
        grid=(t // tm,),
        in_specs=[pl.BlockSpec((tm, kdim), lambda i: (i, 0)), pl.BlockSpec(w.shape, lambda i: (0,) * w.ndim),
                  row, vec, row],
        out_specs=[row, vec],
        compiler_params=_params(("arbitrary",), VMEM_BIG),
    )(dy, w, h, g, dh_in)


def _gate_bwd(dfq, dfk, z, b_pad, f_col):
    t = z.shape[0]
    tm = ROW_TILE
    nt = t // tm

    def body(dq_ref, dk_ref, z_ref, b_ref, dz_ref, db_ref, carry_ref):
        i = pl.program_id(0)

        @pl.when(i == 0)
        def _():
            carry_ref[...] = jnp.zeros_like(carry_ref)
            db_ref[...] = jnp.zeros_like(db_ref)

        pick = (lax.broadcasted_iota(jnp.int32, (ATTN_W, 128), 0)
                == HEAD_DIM * lax.broadcasted_iota(jnp.int32, (ATTN_W, 128), 1)).astype(F32)
        d_heads = jnp.dot(dq_ref[...] - dk_ref[...], pick, preferred_element_type=F32,
                          precision=lax.Precision.HIGHEST)
        tri = (lax.broadcasted_iota(jnp.int32, (tm, tm), 0) <= lax.broadcasted_iota(jnp.int32, (tm, tm), 1))
        tail = jnp.dot(tri.astype(F32), d_heads, preferred_element_type=F32, precision=lax.Precision.HIGHEST)
        tail = tail + carry_ref[0:1, :]
        carry_ref[...] = jnp.broadcast_to(tail[0:1, :], carry_ref.shape)
        row = (nt - 1 - i) * tm + lax.broadcasted_iota(jnp.int32, (tm, 1), 0)
        dlogit = jnp.where(row >= ROW_PAD, tail * _sigmoid(-(z_ref[...] + b_ref[...])), 0.0)
        dz_ref[...] = jnp.zeros_like(dz_ref)
        dz_ref[:, 0:128] = dlogit.astype(BF16)
        db_ref[...] += jnp.sum(dlogit, axis=0, keepdims=True)

    rev = lambda i: (nt - 1 - i, 0)
    return pl.pallas_call(
        body, name="forget_gate_bwd",
        out_shape=[jax.ShapeDtypeStruct((t, F_PAD), BF16), jax.ShapeDtypeStruct((1, 128), F32)],
        grid=(nt,),
        in_specs=[pl.BlockSpec((tm, ATTN_W), rev), pl.BlockSpec((tm, ATTN_W), rev),
                  pl.BlockSpec((tm, 128), lambda i: (nt - 1 - i, f_col // 128)),
                  pl.BlockSpec((1, 128), lambda i: (0, 0))],
        out_specs=[pl.BlockSpec((tm, F_PAD), rev), pl.BlockSpec((1, 128), lambda i: (0, 0))],
        scratch_shapes=[pltpu.VMEM((8, 128), F32)],
        compiler_params=_params(("arbitrary",)),
    )(dfq, dfk, z, b_pad)


def _ffn_fwd(tag, n, w_in4, w_out, h, g_post, g_next):
    ab, s, s_t = _ffn_in(f"{tag}_in_fwd", n, w_in4)
    outs = _mm_resid_norm(f"{tag}_out_fwd", s, w_out, h, g_post, 0.5, g_next)
    return ab, s_t, outs


def _ffn_bwd_weights(tag, df, ab, s_t, n_t, w_in4, w_out):
    d, cw = w_in4.shape[1], w_in4.shape[2]
    t = df.shape[0]
    dw_out = _weight_grad(f"{tag}_dw_out", s_t, df, d, out_rows=cw // 2)
    dab = _ffn_bwd_mid(f"{tag}_mid_bwd", df, w_out, ab)
    bk = _k_tile(t)
    dw_in = _matmul(
        f"{tag}_dw_in", n_t, dab, jax.ShapeDtypeStruct((4, d, cw), F32), (1, 4, t // bk),
        pl.BlockSpec((d, bk), lambda a, b, k: (0, k)), pl.BlockSpec((bk, cw), lambda a, b, k: (k, b)),
        pl.BlockSpec((None, d, cw), lambda a, b, k: (_slot_of(b), 0, 0)), vmem=VMEM_BIG)
    return dab, dw_in, dw_out


LOSS_ROW = 12


def _pack_small(meta, conv, gains, b_forget, loss=None):
    d = gains[0].shape[1]
    rows = [meta.reshape(4, d), jnp.pad(conv.reshape(1, 3 * 128), ((0, 0), (0, d - 3 * 128)))]
    rows += list(gains) + [jnp.pad(b_forget, ((0, 0), (0, d - HEADS)))]
    last = jnp.zeros((4, d), F32)
    if loss is not None:
        last = jnp.pad(loss.reshape(1, 1), ((0, 3), (0, d - 1)))
    return jnp.concatenate(rows + [last], axis=0)


def _unpack_small(block):
    d = block.shape[1]
    meta = block[0:4].reshape(N_META, d // 4)
    conv = block[4, :3 * 128].reshape(1, 3, 128)
    gains = [block[5 + i:6 + i] for i in range(6)]
    return meta, conv, gains, block[11:12, :HEADS]


def kernel(x, meta_tokens, w_in, b_forget, conv_w, w_attn_branch, w_conv_branch, w_out, g_ffn1_pre, g_ffn1_post, w_ffn1_in, w_ffn1_out, g_mix_pre, g_mix_post, g_ffn2_pre, g_ffn2_post, w_ffn2_in, w_ffn2_out, loss_target, m_meta_tokens, m_w_in, m_b_forget, m_conv_w, m_w_attn_branch, m_w_conv_branch, m_w_out, m_g_ffn1_pre, m_g_ffn1_post, m_w_ffn1_in, m_w_ffn1_out, m_g_mix_pre, m_g_mix_post, m_g_ffn2_pre, m_g_ffn2_post, m_w_ffn2_in, m_w_ffn2_out, v_meta_tokens, v_w_in, v_b_forget, v_conv_w, v_w_attn_branch, v_w_conv_branch, v_w_out, v_g_ffn1_pre, v_g_ffn1_post, v_w_ffn1_in, v_w_ffn1_out, v_g_mix_pre, v_g_mix_post, v_g_ffn2_pre, v_g_ffn2_post, v_w_ffn2_in, v_w_ffn2_out):
    seq, d = x.shape[1], x.shape[2]
    t = seq + N_FRONT
    n_main = 3 * ATTN_W + 3 * CONV_W + 2 * d
    nz = n_main + F_PAD
    f_lo = 3 * ATTN_W
    c_arr = lax.axis_index("c").astype(jnp.int32).reshape(1)

    cs = w_in.shape[2]
    cs_pad = -(-cs // 64) * 64

    def w_in_rows(a):
        return jnp.pad(jnp.transpose(a[0]), ((0, cs_pad - cs), (0, 0)))

    big = [w_in_rows(w_in), w_attn_branch[0], w_conv_branch[0], w_out[0], w_ffn1_in[0], w_ffn1_out[0], w_ffn2_in[0],
           w_ffn2_out[0]]
    small_gather = jnp.concatenate(
        [meta_tokens.reshape(4, d), jnp.pad(conv_w.reshape(1, 3 * 128), ((0, 0), (0, d - 3 * 128))),
         jnp.zeros((11, d), F32)], axis=0)
    w_f1_in4, w_f1_out4, small4 = _all_gather([big[4].astype(BF16), big[5].astype(BF16), small_gather])
    rest, small4 = lax.optimization_barrier(([big[i].astype(BF16) for i in (0, 1, 2, 3, 6, 7)], small4))
    rest_gathered = _all_gather_async(rest)
    w_f1_out = w_f1_out4.reshape(-1, d)
    meta_full = jnp.transpose(small4[:, 0:4].reshape(4, N_META, d // 4), (1, 0, 2)).reshape(N_META, d)
    conv_full = jnp.transpose(small4[:, 4, :3 * 128].reshape(4, 3, 128), (1, 0, 2)).reshape(3, CONV_W)
    conv_pad = jnp.pad(conv_full, ((0, 5), (0, 0)))
    b_pad = jnp.pad(b_forget, ((0, 0), (0, 128 - HEADS)))

    h0 = jnp.concatenate([jnp.zeros((ROW_PAD, d), F32), meta_full, x[0]], axis=0)
    n1, n1_t = _norm_fwd("ffn1_pre_norm", h0, g_ffn1_pre)
    ab1, s1_t, (f1, h1, u, u_t) = _ffn_fwd("ffn1", n1, w_f1_in4, w_f1_out, h0, g_ffn1_post, g_mix_pre)

    w_in4, w_ab4, w_cb4, w_out4, w_f2_in4, w_f2_out4 = rest_gathered(u)
    w_in_t = w_in4[:, :cs].reshape(4 * cs, d)
    g_lo = f_lo + HEADS + 3 * CONV_W
    w_in_pad = jnp.concatenate(
        [w_in_t[:f_lo], w_in_t[g_lo:], w_in_t[f_lo + HEADS:g_lo], w_in_t[f_lo:f_lo + HEADS],
         jnp.zeros((F_PAD - HEADS, d), BF16)], axis=0)
    w_ab = jnp.transpose(w_ab4, (1, 0, 2)).reshape(ATTN_W, d)
    w_cb = jnp.transpose(w_cb4, (1, 0, 2)).reshape(CONV_W, d)
    w_out_full = w_out4.reshape(d, d)
    w_f2_out = w_f2_out4.reshape(-1, d)
    qkv, z = _in_proj(u, w_in_pad)
    f_col = z.shape[1] - F_PAD
    f_cum = _gate_prep(z, b_pad, f_col)
    f_heads = f_cum[:, :HEADS]
    o, lse = _attn_fwd(qkv, *_attn_bias_operands(f_heads))
    g, g_t = _conv_gate(z, conv_pad)
    mp, mp_t, o_t = _branch_mix(z, o, g, w_ab, w_cb, d)
    mixed, h2, n2, n2_t = _mm_resid_norm("mix_out_fwd", mp, w_out_full, h1, g_mix_post, 1.0, g_ffn2_pre)
    ab2, s2_t, (f2, h3) = _ffn_fwd("ffn2", n2, w_f2_in4, w_f2_out, h2, g_ffn2_post, None)
    dh3, df2, dg_f2_post, loss_part = _loss_norm_bwd(h3, loss_target[0], f2, g_ffn2_post, 0.5)

    reduced = {}

    def reduce_scatter(label, tags, slots, sequencer_id, hold=None):
        got = _pair_send_halves(f"grad_pair_exchange_{label}", slots)
        sums = [_pair_add(tag, s, a, c_arr, F32 if tag == "small" else BF16) for tag, s, a in zip(tags, slots, got)]
        sums, hold = lax.optimization_barrier((sums, hold))
        if sequencer_id is None:
            arrived = _chip_scatter(f"grad_chip_scatter_{label}", sums)
        else:
            arrived = _chip_scatter_async(f"grad_chip_scatter_{label}", sums, sequencer_id)
        mine = [_chip_add(tag, a) for tag, a in zip(tags, arrived)]
        reduced.update(zip(tags, zip(mine, _pair_swap(f"grad_pair_swap_{label}", mine))))
        return hold

    dab2, dw_f2_in, dw_f2_out = _ffn_bwd_weights("ffn2", df2, ab2, s2_t, n2_t, w_f2_in4, w_f2_out)
    dab2 = reduce_scatter("ffn2", ["w_ffn2_in", "w_ffn2_out"], [dw_f2_in, dw_f2_out.reshape(4, -1, d)], 2, dab2)
    dh2, dg_f2_pre = _mm_nt_norm_bwd("ffn2_in_bwd", dab2, w_f2_in4, h2, g_ffn2_pre, dh3)
    dmixed, dg_mix_post = _norm_bwd("mix_post_norm_bwd", mixed, g_mix_post, dh2, 1.0)
    dw_out = _weight_grad("mix_dw_out", mp_t, dmixed, d)
    dya, dyc, dgates, do, dgconv = _branch_bwd(z, o, g, dmixed, w_out_full, w_ab, w_cb, d)
    dw_ab = _weight_grad("mix_dw_attn_branch", o_t, dya, d)
    dw_cb = _weight_grad("mix_dw_conv_branch", g_t, dyc, d)
    dz_conv, dconv_w = _conv_bwd(z, dgconv, conv_pad)
    front = lax.broadcasted_iota(jnp.int32, (t, 1), 0) < ROW_PAD
    lse_heads = jnp.where(front, 1e9, lse[:, ::HEAD_DIM])
    dq, dk, dv, dfk, dfq = _attn_bwd(qkv, *_attn_bias_operands(f_heads, lse_heads), o, do)
    dz_f, db_forget = _gate_bwd(dfq, dfk, z, b_pad, f_col)
    dz_pieces = {"q": dq, "k": dk, "v": dv, "gates": dgates, "conv": dz_conv, "f": dz_f}
    dh1, dg_mix_pre = _mix_in_bwd(list(dz_pieces.values()), w_in_pad, h1, g_mix_pre, dh2)
    dw_t = {name: _weight_grad_t(f"mix_dw_in_{name}", u_t, piece) for name, piece in dz_pieces.items()}
    dw_in_t = jnp.concatenate(
        [dw_t["q"], dw_t["k"], dw_t["v"], dw_t["f"][:HEADS], dw_t["conv"], dw_t["gates"]], axis=0)
    reduce_scatter(
        "mix", ["w_in", "w_attn_branch", "w_conv_branch", "w_out"],
        [jnp.pad(dw_in_t.reshape(4, cs, d), ((0, 0), (0, cs_pad - cs), (0, 0))),
         jnp.transpose(dw_ab.reshape(ATTN_W, 4, d // 4), (1, 0, 2)),
         jnp.transpose(dw_cb.reshape(CONV_W, 4, d // 4), (1, 0, 2)),
         dw_out.reshape(4, d // 4, d)], 3)
    df1, dg_f1_post = _norm_bwd("ffn1_post_norm_bwd", f1, g_ffn1_post, dh1, 0.5)
    dab1, dw_f1_in, dw_f1_out = _ffn_bwd_weights("ffn1", df1, ab1, s1_t, n1_t, w_f1_in4, w_f1_out)
    dab1 = reduce_scatter("ffn1", ["w_ffn1_in", "w_ffn1_out"], [dw_f1_in, dw_f1_out.reshape(4, -1, d)], 4, dab1)
    dh0, dg_f1_pre = _mm_nt_norm_bwd("ffn1_in_bwd", dab1, w_f1_in4, h0, g_ffn1_pre, dh1)
    grad_x = dh0[N_FRONT:][None]
    dmeta = dh0[ROW_PAD:N_FRONT]
    small_grad = jnp.stack([
        _pack_small(dmeta[:, j * (d // 4):(j + 1) * (d // 4)], dconv_w[:3, j * 128:(j + 1) * 128],
                    [dg_f1_pre, dg_f1_post, dg_mix_pre, dg_mix_post, dg_f2_pre, dg_f2_post], db_forget[:, :HEADS],
                    loss_part[0, 0])
        for j in range(4)])
    reduce_scatter("small", ["small"], [small_grad], None)
    tags =["w_in", "w_attn_branch", "w_conv_branch", "w_out", "w_ffn1_in", "w_ffn1_out", "w_ffn2_in", "w_ffn2_out", "small"]
    halves = [reduced[tag][0] for tag in tags]
    others = [reduced[tag][1] for tag in tags]

    small = [g_ffn1_pre, g_ffn1_post, g_mix_pre, g_mix_post, g_ffn2_pre, g_ffn2_post]
    small_m = [m_g_ffn1_pre, m_g_ffn1_post, m_g_mix_pre, m_g_mix_post, m_g_ffn2_pre, m_g_ffn2_post]
    small_v = [v_g_ffn1_pre, v_g_ffn1_post, v_g_mix_pre, v_g_mix_post, v_g_ffn2_pre, v_g_ffn2_post]
    ws = big + [_pack_small(meta_tokens, conv_w[0], small, b_forget)]
    ms = [w_in_rows(m_w_in), m_w_attn_branch[0], m_w_conv_branch[0], m_w_out[0], m_w_ffn1_in[0], m_w_ffn1_out[0],
          m_w_ffn2_in[0], m_w_ffn2_out[0], _pack_small(m_meta_tokens, m_conv_w[0], small_m, m_b_forget)]
    vs = [w_in_rows(v_w_in), v_w_attn_branch[0], v_w_conv_branch[0], v_w_out[0], v_w_ffn1_in[0], v_w_ffn1_out[0],
          v_w_ffn2_in[0], v_w_ffn2_out[0], _pack_small(v_meta_tokens, v_conv_w[0], small_v, v_b_forget)]
    updates = [_adamw(tag, w, a, b, m, v, c_arr) for tag, w, a, b, m, v in zip(tags, ws, halves, others, ms, vs)]

    def leaves(big_vals, small_block):
        meta, conv, gains, bf = _unpack_small(small_block)
        w_in_t_, w_ab_, w_cb_, w_out_, f1_in, f1_out, f2_in, f2_out = [b[None] for b in big_vals]
        w_in_ = jnp.transpose(w_in_t_[:, :cs], (0, 2, 1))
        return [meta, w_in_, bf, conv, w_ab_, w_cb_, w_out_, gains[0], gains[1], f1_in, f1_out,
                gains[2], gains[3], gains[4], gains[5], f2_in, f2_out]

    out_g, out_d, out_m, out_v = [leaves([u_[k] for u_ in updates[:8]], updates[8][k]) for k in range(4)]
    loss = updates[8][0][LOSS_ROW, 0]
    return (loss, grad_x, *out_g, *out_d, *out_m, *out_v)
```

```python
import functools

import jax
import jax.numpy as jnp
from jax import lax
from jax.experimental import pallas as pl
from jax.experimental.pallas import tpu as pltpu
from jax.experimental.pallas import tpu_sc as plsc

N_META = 16
ROW_PAD = 112
N_FRONT = ROW_PAD + N_META
HEADS = 8
HEAD_DIM = 64
ATTN_W = HEADS * HEAD_DIM
CONV_W = 512
NORM_EPS = 1e-6
ROW_TILE = 640
F_PAD = 128
ATTN_ROW_PARTS = 1
NEG = -1e30
ADAM_LR = 0.001
ADAM_B1 = 0.9
ADAM_B2 = 0.999
ADAM_EPS = 1e-08
ADAM_WD = 0.01
ADAM_STEP = 10
VMEM_BIG = 56 * 1024 * 1024
MESH = pl.DeviceIdType.MESH
ANY = pl.BlockSpec(memory_space=pl.ANY)
F32 = jnp.float32
BF16 = jnp.bfloat16


def _params(sem, vmem=None):
    return pltpu.CompilerParams(dimension_semantics=sem, vmem_limit_bytes=vmem)


def _sigmoid(x):
    return 1.0 / (1.0 + jnp.exp(-x))


def _rstd(x):
    return lax.rsqrt(jnp.mean(x * x, axis=-1, keepdims=True) + NORM_EPS)


def _rms_bwd(x, g, dy):
    r = _rstd(x)
    xr = x * r
    gdy = g * dy
    dx = r * (gdy - xr * jnp.mean(xr * gdy, axis=-1, keepdims=True))
    return dx, jnp.sum(dy * xr, axis=0, keepdims=True)


def _dot(a, b):
    return jnp.dot(a, b, preferred_element_type=F32)


def _dot_nt(a, b):
    return lax.dot_general(a, b, (((1,), (1,)), ((), ())), preferred_element_type=F32)


def _k_tile(t):
    return 1664 if t % 1664 == 0 else ROW_TILE


def _place():
    x, y, c = lax.axis_index("x"), lax.axis_index("y"), lax.axis_index("c")
    chips = [(1 - x, y), (x, 1 - y), (1 - x, 1 - y)]
    return x, y, c, chips


def _all_gather(shards):
    n = len(shards)
    split = [s.reshape(2, s.shape[0] // 2, s.shape[1]) for s in shards]

    def body(*refs):
        ins, outs = refs[:n], refs[n:2 * n]
        send_sems, recv_sems = refs[2 * n:]
        x, y, c, chips = _place()
        me = 2 * x + y
        sibling = (x, y, 1 - c)

        def remote(i, k, slot, part, to, src=None):
            dst = outs[i].at[slot, part]
            return pltpu.make_async_remote_copy(
                src_ref=dst if src is None else src, dst_ref=dst,
                send_sem=send_sems.at[i, k], recv_sem=recv_sems.at[i, k],
                device_id=to, device_id_type=MESH)

        started = []
        for i in range(n):
            for k, (cx, cy) in enumerate(chips):
                cp = remote(i, k, me, c, (cx, cy, c), src=ins[i].at[c])
                cp.start()
                started.append(cp)
        for i in range(n):
            for k, (cx, cy) in enumerate(chips):
                remote(i, k, 2 * cx + cy, c, (x, y, c)).wait_recv()
                cp = remote(i, 3 + k, 2 * cx + cy, c, sibling)
                cp.start()
                started.append(cp)
        for i in range(n):
            for k, (cx, cy) in enumerate(chips):
                remote(i, 3 + k, 2 * cx + cy, 1 - c, (x, y, c)).wait_recv()
        for cp in started:
            cp.wait_send()

    outs = pl.pallas_call(
        body, name="all_gather_weights",
        out_shape=[jax.ShapeDtypeStruct((4,) + s.shape, s.dtype) for s in split],
        in_specs=[ANY] * n, out_specs=[ANY] * n,
        scratch_shapes=[pltpu.SemaphoreType.DMA((n, 6)), pltpu.SemaphoreType.DMA((n, 6))],
    )(*split)
    me =2 * lax.axis_index("x") + lax.axis_index("y")
    outs = [lax.dynamic_update_slice(o, s[None], (me, 0, 0, 0)) for o, s in zip(outs, split)]
    return [o.reshape((4,) + s.shape) for o, s in zip(outs, shards)]


def _all_gather_async(name, shards, collective_id):
    n = len(shards)
    split = [s.reshape(2, s.shape[0] // 2, s.shape[1]) for s in shards]
    ins = [jax.new_ref(s, memory_space=pltpu.MemorySpace.HBM) for s in split]
    outs = [jax.empty_ref(jax.ShapeDtypeStruct((4,) + s.shape, s.dtype), memory_space=pltpu.MemorySpace.HBM)
            for s in split]

    @pl.kernel(mesh=plsc.ScalarSubcoreMesh(axis_name="sequencer", num_cores=1), name=name,
               scratch_types=(pltpu.SemaphoreType.DMA((n, 6)), pltpu.SemaphoreType.DMA((n, 6))),
               compiler_params=pltpu.CompilerParams(collective_id=collective_id))
    def launch(send_sems, recv_sems):
        x, y, c, chips = _place()
        me = 2 * x + y
        sibling = (x, y, 1 - c)
        barrier = pltpu.get_barrier_semaphore()
        for peer in [(cx, cy, c) for cx, cy in chips] + [sibling]:
            pl.semaphore_signal(barrier, inc=1, device_id=peer, device_id_type=MESH)
        pl.semaphore_wait(barrier, 4)

        def remote(i, k, slot, part, to, src=None):
            dst = outs[i].at[slot, part]
            return pltpu.make_async_remote_copy(
                src_ref=dst if src is None else src, dst_ref=dst,
                send_sem=send_sems.at[i, k], recv_sem=recv_sems.at[i, k],
                device_id=to, device_id_type=MESH)

        started = []
        for i in range(n):
            for k, (cx, cy) in enumerate(chips):
                cp = remote(i, k, me, c, (cx, cy, c), src=ins[i].at[c])
                cp.start()
                started.append(cp)
        for i in range(n):
            for k, (cx, cy) in enumerate(chips):
                remote(i, k, 2 * cx + cy, c, (x, y, c)).wait_recv()
                cp = remote(i, 3 + k, 2 * cx + cy, c, sibling)
                cp.start()
                started.append(cp)
        for i in range(n):
            for k, (cx, cy) in enumerate(chips):
                remote(i, 3 + k, 2 * cx + cy, 1 - c, (x, y, c)).wait_recv()
        for cp in started:
            cp.wait_send()

    launch()
    raw = [o[...] for o in outs]

    def finish(after, which):
        arrived, _ = lax.optimization_barrier(([raw[i] for i in which], after))
        me = 2 * lax.axis_index("x") + lax.axis_index("y")
        gathered = [lax.dynamic_update_slice(a, split[i][None], (me, 0, 0, 0)) for a, i in zip(arrived, which)]
        return [g.reshape((4,) + shards[i].shape) for g, i in zip(gathered, which)]

    return finish


def _pair_send_halves(name, grads):
    n = len(grads)

    def body(*refs):
        ins, outs = refs[:n], refs[n:2 * n]
        send_sems, recv_sems = refs[2 * n:]
        x, y, c, _ = _place()
        cps = []
        for i in range(n):
            half = ins[i].shape[1] // 2
            cp = pltpu.make_async_remote_copy(
                src_ref=ins[i].at[:, pl.ds((1 - c) * half, half)], dst_ref=outs[i],
                send_sem=send_sems.at[i], recv_sem=recv_sems.at[i],
                device_id=(x, y, 1 - c), device_id_type=MESH)
            cp.start()
            cps.append(cp)
        for cp in cps:
            cp.wait()

    return pl.pallas_call(
        body, name=name,
        out_shape=[jax.ShapeDtypeStruct((4, g.shape[1] // 2, g.shape[2]), g.dtype) for g in grads],
        in_specs=[ANY] * n, out_specs=[ANY] * n,
        scratch_shapes=[pltpu.SemaphoreType.DMA((n,)), pltpu.SemaphoreType.DMA((n,))],
    )(*grads)


def _pair_send_halves_async(name, grads, collective_id):
    n = len(grads)
    ins = [jax.new_ref(g, memory_space=pltpu.MemorySpace.HBM) for g in grads]
    outs = [jax.empty_ref(jax.ShapeDtypeStruct((4, g.shape[1] // 2, g.shape[2]), g.dtype),
                          memory_space=pltpu.MemorySpace.HBM) for g in grads]

    @pl.kernel(mesh=plsc.ScalarSubcoreMesh(axis_name="sequencer", num_cores=1), name=name,
               scratch_types=(pltpu.SemaphoreType.DMA((n,)), pltpu.SemaphoreType.DMA((n,))),
               compiler_params=pltpu.CompilerParams(collective_id=collective_id))
    def launch(send_sems, recv_sems):
        x, y, c, _ = _place()
        barrier = pltpu.get_barrier_semaphore()
        pl.semaphore_signal(barrier, inc=1, device_id=(x, y, 1 - c), device_id_type=MESH)
        pl.semaphore_wait(barrier, 1)
        cps = []
        for i in range(n):
            half = ins[i].shape[1] // 2
            cp = pltpu.make_async_remote_copy(
                src_ref=ins[i].at[:, pl.ds((1 - c) * half, half)], dst_ref=outs[i],
                send_sem=send_sems.at[i], recv_sem=recv_sems.at[i],
                device_id=(x, y, 1 - c), device_id_type=MESH)
            cp.start()
            cps.append(cp)
        for cp in cps:
            cp.wait()

    launch()
    return [o[...] for o in outs]


def _chip_scatter(name, parts):
    n = len(parts)

    def body(*refs):
        _scatter_copies(refs[:n], refs[n:2 * n], *refs[2 * n:])

    arrived = pl.pallas_call(
        body, name=name,
        out_shape=[jax.ShapeDtypeStruct(p.shape, p.dtype) for p in parts],
        in_specs=[ANY] * n, out_specs=[ANY] * n,
        scratch_shapes=[pltpu.SemaphoreType.DMA((n, 3)), pltpu.SemaphoreType.DMA((n, 3))],
    )(*parts)
    return _own_slots(parts, arrived)


def _scatter_copies(ins, outs, send_sems, recv_sems):
    x, y, c, chips = _place()
    me = 2 * x + y
    sends = []
    for i in range(len(ins)):
        for k, (cx, cy) in enumerate(chips):
            cp = pltpu.make_async_remote_copy(
                src_ref=ins[i].at[2 * cx + cy], dst_ref=outs[i].at[me],
                send_sem=send_sems.at[i, k], recv_sem=recv_sems.at[i, k],
                device_id=(cx, cy, c), device_id_type=MESH)
            cp.start()
            sends.append(cp)
    for i in range(len(ins)):
        for k, (cx, cy) in enumerate(chips):
            got = outs[i].at[2 * cx + cy]
            pltpu.make_async_remote_copy(
                src_ref=got, dst_ref=got, send_sem=send_sems.at[i, k], recv_sem=recv_sems.at[i, k],
                device_id=(x, y, c), device_id_type=MESH).wait_recv()
    for cp in sends:
        cp.wait_send()


def _own_slots(parts, arrived):
    me = 2 * lax.axis_index("x") + lax.axis_index("y")
    return [lax.dynamic_update_slice(a, lax.dynamic_slice_in_dim(p, me, 1, axis=0), (me, 0, 0))
            for p, a in zip(parts, arrived)]


def _chip_scatter_async(name, parts, collective_id):
    n = len(parts)
    ins = [jax.new_ref(p, memory_space=pltpu.MemorySpace.HBM) for p in parts]
    outs = [jax.empty_ref(jax.ShapeDtypeStruct(p.shape, p.dtype), memory_space=pltpu.MemorySpace.HBM) for p in parts]

    @pl.kernel(mesh=plsc.ScalarSubcoreMesh(axis_name="sequencer", num_cores=1), name=name,
               scratch_types=(pltpu.SemaphoreType.DMA((n, 3)), pltpu.SemaphoreType.DMA((n, 3))),
               compiler_params=pltpu.CompilerParams(collective_id=collective_id))
    def launch(send_sems, recv_sems):
        x, y, c, chips = _place()
        barrier = pltpu.get_barrier_semaphore()
        for cx, cy in chips:
            pl.semaphore_signal(barrier, inc=1, device_id=(cx, cy, c), device_id_type=MESH)
        pl.semaphore_wait(barrier, 3)
        _scatter_copies(ins, outs, send_sems, recv_sems)

    launch()
    return _own_slots(parts, [o[...] for o in outs])


def _pair_swap(name, halves):
    n = len(halves)

    def body(*refs):
        ins, outs = refs[:n], refs[n:2 * n]
        send_sems, recv_sems = refs[2 * n:]
        x, y, c, _ = _place()
        cps = []
        for i in range(n):
            cp = pltpu.make_async_remote_copy(
                src_ref=ins[i], dst_ref=outs[i], send_sem=send_sems.at[i], recv_sem=recv_sems.at[i],
                device_id=(x, y, 1 - c), device_id_type=MESH)
            cp.start()
            cps.append(cp)
        for cp in cps:
            cp.wait()

    return pl.pallas_call(
        body, name=name,
        out_shape=[jax.ShapeDtypeStruct(h.shape, h.dtype) for h in halves],
        in_specs=[ANY] * n, out_specs=[ANY] * n,
        scratch_shapes=[pltpu.SemaphoreType.DMA((n,)), pltpu.SemaphoreType.DMA((n,))],
    )(*halves)


def _row_block(rows, cols, n_bufs, budget=20 * 1024 * 1024):
    best = min(rows, 16)
    for b in range(16, rows + 1, 16):
        if rows % b == 0 and 2 * n_bufs * b * cols * 4 <= budget:
            best = b
    return best


def _pair_add(tag, grad, got, c_arr, out_dtype):
    _, rows, cols = grad.shape
    half = rows // 2
    bh = _row_block(half, cols, 3)
    nb = half // bh

    def body(c_ref, g_ref, a_ref, o_ref):
        o_ref[...] = (g_ref[...] + a_ref[...]).astype(out_dtype)

    return pl.pallas_call(
        body, name=f"pair_add_{tag}",
        out_shape=jax.ShapeDtypeStruct((4, half, cols), out_dtype),
        grid_spec=pltpu.PrefetchScalarGridSpec(
            num_scalar_prefetch=1, grid=(4, nb),
            in_specs=[pl.BlockSpec((None, bh, cols), lambda j, r, c: (j, c[0] * nb + r, 0)),
                      pl.BlockSpec((None, bh, cols), lambda j, r, c: (j, r, 0))],
            out_specs=pl.BlockSpec((None, bh, cols), lambda j, r, c: (j, r, 0))),
        compiler_params=_params(("parallel", "parallel")),
    )(c_arr, grad, got)


def _chip_add(tag, parts):
    _, half, cols = parts.shape
    bh = _row_block(half, cols, 5)

    def body(p_ref, o_ref):
        a, b, c, d = [p_ref[j].astype(F32) for j in range(4)]
        o_ref[...] = ((a + b) + c) + d

    return pl.pallas_call(
        body, name=f"chip_add_{tag}",
        out_shape=jax.ShapeDtypeStruct((half, cols), F32),
        grid=(half // bh,),
        in_specs=[pl.BlockSpec((4, bh, cols), lambda r: (0, r, 0))],
        out_specs=pl.BlockSpec((bh, cols), lambda r: (r, 0)),
        compiler_params=_params(("parallel",)),
    )(parts)


def _adamw(tag, w, mine, theirs, m, v, c_arr):
    rows, cols = w.shape
    half = rows // 2
    br = _row_block(half, cols, 9)
    nb = half // br

    def body(c_ref, w_ref, a_ref, b_ref, m_ref, v_ref, g_ref, d_ref, mo_ref, vo_ref):
        own = (pl.program_id(0) // nb) == c_ref[0]
        g = jnp.where(own, a_ref[...], b_ref[...])
        g_ref[...] = g
        m_new = ADAM_B1 * m_ref[...] + (1.0 - ADAM_B1) * g
        v_new = ADAM_B2 * v_ref[...] + (1.0 - ADAM_B2) * (g * g)
        m_hat = m_new / (1.0 - ADAM_B1 ** ADAM_STEP)
        v_hat = v_new / (1.0 - ADAM_B2 ** ADAM_STEP)
        d_ref[...] = -ADAM_LR * (m_hat / (jnp.sqrt(v_hat) + ADAM_EPS) + ADAM_WD * w_ref[...])
        mo_ref[...] = m_new
        vo_ref[...] = v_new

    spec = pl.BlockSpec((br, cols), lambda r, c: (r, 0))
    mine_spec = pl.BlockSpec((br, cols), lambda r, c: (jnp.clip(r - c[0] * nb, 0, nb - 1), 0))
    theirs_spec = pl.BlockSpec((br, cols), lambda r, c: (jnp.clip(r - (1 - c[0]) * nb, 0, nb - 1), 0))
    return pl.pallas_call(
        body, name=f"adamw_{tag}",
        out_shape=[jax.ShapeDtypeStruct((rows, cols), F32)] * 4,
        grid_spec=pltpu.PrefetchScalarGridSpec(
            num_scalar_prefetch=1, grid=(rows // br,),
            in_specs=[spec, mine_spec, theirs_spec, spec, spec], out_specs=[spec] * 4),
        compiler_params=_params(("arbitrary",)),
    )(c_arr, w, mine, theirs, m, v)


def _matmul(name, x, w, out_shape, grid, x_spec, w_spec, o_spec, *, nt=False, vmem=None):
    nk = grid[2]
    acc_shape = tuple(d for d in o_spec.block_shape if d is not None)

    def body(x_ref, w_ref, o_ref, acc_ref):
        k = pl.program_id(2)
        part = _dot_nt(x_ref[...], w_ref[...]) if nt else _dot(x_ref[...], w_ref[...])
        if nk == 1:
            o_ref[...] = part.astype(o_ref.dtype)
        else:
            @pl.when(k == 0)
            def _():
                acc_ref[...] = part

            @pl.when(k > 0)
            def _():
                acc_ref[...] += part

            @pl.when(k == nk - 1)
            def _():
                o_ref[...] = acc_ref[...].astype(o_ref.dtype)

    return pl.pallas_call(
        body, name=name, out_shape=out_shape, grid=grid,
        in_specs=[x_spec, w_spec], out_specs=o_spec,
        scratch_shapes=[pltpu.VMEM(acc_shape if nk > 1 else (8, 128), F32)],
        compiler_params=_params(("parallel", "parallel", "arbitrary"), vmem),
    )(x, w)


def _weight_grad(name, xt, dy, bn, out_rows=None):
    m, t = xt.shape
    n = dy.shape[1]
    bm = m if out_rows is None else out_rows
    bk = _k_tile(t)
    return _matmul(
        name, xt, dy, jax.ShapeDtypeStruct((m, n), F32), (m // bm, n // bn, t // bk),
        pl.BlockSpec((bm, bk), lambda a, b, k: (a, k)),
        pl.BlockSpec((bk, bn), lambda a, b, k: (k, b)),
        pl.BlockSpec((bm, bn), lambda a, b, k: (a, b)), vmem=VMEM_BIG)


def _weight_grad_t(name, xt, dy):
    m, t = xt.shape
    n = dy.shape[1]
    bn = min(n, 512)
    bk = _k_tile(t)
    nk = t // bk

    def body(x_ref, dy_ref, o_ref, acc_ref):
        k = pl.program_id(1)
        part = _dot(x_ref[...], dy_ref[...].astype(BF16))

        @pl.when(k == 0)
        def _():
            acc_ref[...] = part

        @pl.when(k > 0)
        def _():
            acc_ref[...] += part

        @pl.when(k == nk - 1)
        def _():
            o_ref[...] = acc_ref[...].T

    return pl.pallas_call(
        body, name=name, out_shape=jax.ShapeDtypeStruct((n, m), F32), grid=(n // bn, nk),
        in_specs=[pl.BlockSpec((m, bk), lambda b, k: (0, k)), pl.BlockSpec((bk, bn), lambda b, k: (k, b))],
        out_specs=pl.BlockSpec((bn, m), lambda b, k: (b, 0)),
        scratch_shapes=[pltpu.VMEM((m, bn), F32)],
        compiler_params=_params(("parallel", "arbitrary"), VMEM_BIG),
    )(xt, dy)


def _mix_in_bwd(pieces, wt, h, g, dh_in):
    t, d = h.shape
    tm = ROW_TILE // 2
    widths = [p.shape[1] for p in pieces]
    n = len(pieces)

    def body(*refs):
        dy_refs, (w_ref, h_ref, g_ref, dhi_ref, dh_ref, dg_ref) = refs[:n], refs[n:]

        @pl.when(pl.program_id(0) == 0)
        def _():
            dg_ref[...] = jnp.zeros_like(dg_ref)

        dn, off = None, 0
        for dy_ref, wd in zip(dy_refs, widths):
            part = _dot(dy_ref[...].astype(BF16), w_ref[off:off + wd, :])
            dn = part if dn is None else dn + part
            off += wd
        dx, dg = _rms_bwd(h_ref[...], g_ref[...], dn)
        dh_ref[...] = dhi_ref[...] + dx
        dg_ref[...] += dg

    row = pl.BlockSpec((tm, d), lambda i: (i, 0))
    vec = pl.BlockSpec((1, d), lambda i: (0, 0))
    return pl.pallas_call(
        body, name="mix_in_bwd",
        out_shape=[jax.ShapeDtypeStruct((t, d), F32), jax.ShapeDtypeStruct((1, d), F32)],
        grid=(t // tm,),
        in_specs=[pl.BlockSpec((tm, wd), lambda i: (i, 0)) for wd in widths]
        + [pl.BlockSpec(wt.shape, lambda i: (0, 0)), row, vec, row],
        out_specs=[row, vec],
        compiler_params=_params(("arbitrary",), VMEM_BIG),
    )(*pieces, wt, h, g, dh_in)


def _norm_fwd(name, h, g):
    t, d = h.shape
    tm = ROW_TILE

    def body(h_ref, g_ref, n_ref, nt_ref):
        x = h_ref[...]
        y = x * _rstd(x) * g_ref[...]
        n_ref[...] = y.astype(BF16)
        nt_ref[...] = y.T.astype(BF16)

    return pl.pallas_call(
        body, name=name,
        out_shape=[jax.ShapeDtypeStruct((t, d), BF16), jax.ShapeDtypeStruct((d, t), BF16)],
        grid=(t // tm,),
        in_specs=[pl.BlockSpec((tm, d), lambda i: (i, 0)), pl.BlockSpec((1, d), lambda i: (0, 0))],
        out_specs=[pl.BlockSpec((tm, d), lambda i: (i, 0)), pl.BlockSpec((d, tm), lambda i: (0, i))],
        compiler_params=_params(("parallel",)),
    )(h, g)


def _slot_of(kk):
    return (kk % 2) * 2 + kk // 2


def _ffn_in(name, n, w4):
    t, d = n.shape
    cw = w4.shape[2]
    tm = ROW_TILE

    def body(x_ref, wg_ref, wu_ref, ab_ref, s_ref, st_ref):
        x = x_ref[...]
        a = _dot(x, wg_ref[...])
        b = _dot(x, wu_ref[...])
        ab_ref[:, :cw] = a.astype(BF16)
        ab_ref[:, cw:] = b.astype(BF16)
        s = a * _sigmoid(a) * b
        s_ref[...] = s.astype(BF16)
        st_ref[...] = s.T.astype(BF16)

    return pl.pallas_call(
        body, name=name,
        out_shape=[jax.ShapeDtypeStruct((t, 4 * cw), BF16), jax.ShapeDtypeStruct((t, 2 * cw), BF16),
                   jax.ShapeDtypeStruct((2 * cw, t), BF16)],
        grid=(2, t // tm),
        in_specs=[pl.BlockSpec((tm, d), lambda j, i: (i, 0)),
                  pl.BlockSpec((None, d, cw), lambda j, i: (j, 0, 0)),
                  pl.BlockSpec((None, d, cw), lambda j, i: (2 + j, 0, 0))],
        out_specs=[pl.BlockSpec((tm, 2 * cw), lambda j, i: (i, j)),
                   pl.BlockSpec((tm, cw), lambda j, i: (i, j)),
                   pl.BlockSpec((cw, tm), lambda j, i: (j, i))],
        compiler_params=_params(("parallel", "parallel"), VMEM_BIG),
    )(n, w4, w4)


def _mm_resid_norm(name, x, w, h, g_post, alpha, g_next):
    t, kdim = x.shape
    d = w.shape[1]
    tm = ROW_TILE
    with_next = g_next is not None

    def body(x_ref, w_ref, h_ref, gp_ref, gn_ref, f_ref, hn_ref, *rest):
        f = _dot(x_ref[...], w_ref[...])
        f_ref[...] = f
        hn = h_ref[...] + alpha * (f * _rstd(f) * gp_ref[...])
        hn_ref[...] = hn
        if with_next:
            y = hn * _rstd(hn) * gn_ref[...]
            rest[0][...] = y.astype(BF16)
            rest[1][...] = y.T.astype(BF16)

    row = lambda i: (i, 0)
    vec = pl.BlockSpec((1, d), lambda i: (0, 0))
    out_shape = [jax.ShapeDtypeStruct((t, d), F32), jax.ShapeDtypeStruct((t, d), F32)]
    out_specs = [pl.BlockSpec((tm, d), row), pl.BlockSpec((tm, d), row)]
    if with_next:
        out_shape += [jax.ShapeDtypeStruct((t, d), BF16), jax.ShapeDtypeStruct((d, t), BF16)]
        out_specs += [pl.BlockSpec((tm, d), row), pl.BlockSpec((d, tm), lambda i: (0, i))]
    return pl.pallas_call(
        body, name=name, out_shape=out_shape, grid=(t // tm,),
        in_specs=[pl.BlockSpec((tm, kdim), row), pl.BlockSpec((kdim, d), lambda i: (0, 0)),
                  pl.BlockSpec((tm, d), row), vec, vec],
        out_specs=out_specs,
        compiler_params=_params(("parallel",), VMEM_BIG),
    )(x, w, h, g_post, g_post if g_next is None else g_next)


def _in_proj(u, w):
    t, d = u.shape
    nz = w.shape[0]
    nq = 3 * ATTN_W
    tm = ROW_TILE // 2

    def body(u_ref, w_ref, qkv_ref, z_ref):
        qkv_ref[...] = _dot_nt(u_ref[...], w_ref[0:nq, :]).astype(BF16)
        z_ref[...] = _dot_nt(u_ref[...], w_ref[nq:, :])

    return pl.pallas_call(
        body, name="mix_in_proj",
        out_shape=[jax.ShapeDtypeStruct((t, nq), BF16), jax.ShapeDtypeStruct((t, nz - nq), F32)],
        grid=(t // tm,),
        in_specs=[pl.BlockSpec((tm, d), lambda i: (i, 0)), pl.BlockSpec((nz, d), lambda i: (0, 0))],
        out_specs=[pl.BlockSpec((tm, nq), lambda i: (i, 0)), pl.BlockSpec((tm, nz - nq), lambda i: (i, 0))],
        compiler_params=_params(("parallel",), VMEM_BIG),
    )(u, w)


def _gate_prep(z, b_pad, f_col):
    t = z.shape[0]
    tm = ROW_TILE

    def body(z_ref, b_ref, f_ref, carry_ref):
        i = pl.program_id(0)

        @pl.when(i == 0)
        def _():
            carry_ref[...] = jnp.zeros_like(carry_ref)

        xs = z_ref[...] + b_ref[...]
        logf = jnp.minimum(xs, 0.0) - jnp.log(1.0 + jnp.exp(-jnp.abs(xs)))
        row = i * tm + lax.broadcasted_iota(jnp.int32, (tm, 1), 0)
        logf = jnp.where(row >= ROW_PAD, logf, 0.0)
        tri = (lax.broadcasted_iota(jnp.int32, (tm, tm), 0) >= lax.broadcasted_iota(jnp.int32, (tm, tm), 1))
        f = jnp.dot(tri.astype(F32), logf, preferred_element_type=F32, precision=lax.Precision.HIGHEST)
        f = f + carry_ref[0:1, :]
        f_ref[...] = f
        carry_ref[...] = jnp.broadcast_to(f[tm - 1:tm, :], carry_ref.shape)

    return pl.pallas_call(
        body, name="forget_gate_cumsum", out_shape=jax.ShapeDtypeStruct((t, 128), F32),
        grid=(t // tm,),
        in_specs=[pl.BlockSpec((tm, 128), lambda i: (i, f_col // 128)), pl.BlockSpec((1, 128), lambda i: (0, 0))],
        out_specs=pl.BlockSpec((tm, 128), lambda i: (i, 0)),
        scratch_shapes=[pltpu.VMEM((8, 128), F32)],
        compiler_params=_params(("arbitrary",)),
    )(z, b_pad)


def _lane_halves():
    lane = lax.broadcasted_iota(jnp.int32, (1, 128), 1)
    return lane < HEAD_DIM


def _causal_mask(tq, tk, row0=0):
    row = row0 + lax.broadcasted_iota(jnp.int32, (tq, 1), 0)
    col = lax.broadcasted_iota(jnp.int32, (1, tk), 1)
    return col <= row


def _lane_one(lane):
    return (lax.broadcasted_iota(jnp.int32, (1, 128), 1) == lane).astype(BF16)


def _split3(x):
    hi = x.astype(BF16)
    rest = x - hi.astype(F32)
    mid = rest.astype(BF16)
    return hi, mid, (rest - mid.astype(F32)).astype(BF16)


def _split3_glue(x):
    hi = lax.reduce_precision(x, 8, 7)
    mid = lax.reduce_precision(x - hi, 8, 7)
    lo = lax.reduce_precision((x - hi) - mid, 8, 7)
    return hi.astype(BF16), mid.astype(BF16), lo.astype(BF16)


def _aug_pairs(cols):
    t = cols[0].shape[0]
    a = jnp.pad(jnp.stack(cols, axis=2), ((0, 0), (0, 0), (0, HEAD_DIM - len(cols))))
    a = a.reshape(t, 4, 2, HEAD_DIM)[:, :, ::-1, :]
    return jnp.transpose(a.reshape(t, 4, 128), (1, 0, 2))


def _attn_bias_operands(f_heads, lse_heads=None):
    t = f_heads.shape[0]
    one = jnp.ones((t, HEADS), BF16)
    row = lax.broadcasted_iota(jnp.int32, (t, 1), 0)
    fq = _split3_glue(f_heads)
    fk = _split3_glue(jnp.where(row < ROW_PAD, 1e9, f_heads))
    q_cols = list(fq) + [one] * 3
    k_cols = [one] * 3 + [-c for c in fk]
    if lse_heads is not None:
        q_cols += [-c for c in _split3_glue(lse_heads)]
        k_cols += [one] * 3
    return _aug_pairs(q_cols), _aug_pairs(k_cols)


def _attn_steps(nq, by_key):
    if by_key:
        pairs = [(qi, ki) for ki in range(nq) for qi in range(ki, nq)]
    else:
        pairs = [(qi, ki) for qi in range(nq) for ki in range(qi + 1)]
    return (jnp.array([p[0] for p in pairs], jnp.int32), jnp.array([p[1] for p in pairs], jnp.int32))


def _attn_fwd(z, aug_q, aug_k):
    t = z.shape[0]
    tq = tk = ROW_TILE
    nq = t // tq
    q_tab, k_tab = _attn_steps(nq, by_key=False)

    def body(qt_ref, kt_ref, q_ref, k_ref, v_ref, aq_ref, ak_ref, o_ref, lse_ref, m_ref, l_ref, acc_ref):
        step = pl.program_id(1)
        qi, ki = qt_ref[step], kt_ref[step]

        @pl.when(ki == 0)
        def _():
            m_ref[...] = jnp.full_like(m_ref, NEG)
            l_ref[...] = jnp.zeros_like(l_ref)
            acc_ref[...] = jnp.zeros_like(acc_ref)

        def sweep(diagonal):
            first = _lane_halves()
            q = (q_ref[...] * (HEAD_DIM ** -0.5)).astype(BF16)
            k = k_ref[...].astype(BF16)
            v = v_ref[...].astype(BF16)
            aq, ak = aq_ref[...], ak_ref[...]
            halves = (first, jnp.logical_not(first))
            qa = [jnp.where(lanes, q, aq) for lanes in halves]
            ka = [jnp.where(lanes, k, ak) for lanes in halves]
            va = [jnp.where(lanes, v, _lane_one(a0)) for lanes, a0 in zip(halves, (HEAD_DIM, 0))]
            chains = [(hh, r) for r in range(ATTN_ROW_PARTS) for hh in range(2)]
            rp = tq // ATTN_ROW_PARTS
            rows = [slice(r * rp, (r + 1) * rp) for _, r in chains]
            s = [_dot_nt(qa[hh][rw], ka[hh]) for (hh, _), rw in zip(chains, rows)]
            if diagonal:
                s = [jnp.where(_causal_mask(rp, tk, rw.start), s_c, NEG) for s_c, rw in zip(s, rows)]
            m_prev = [m_ref[rw, hh * HEAD_DIM:hh * HEAD_DIM + 1] for (hh, _), rw in zip(chains, rows)]
            m_new = [jnp.maximum(mp, jnp.max(s_c, axis=1, keepdims=True)) for mp, s_c in zip(m_prev, s)]
            p = [jnp.exp(s_c - m_c).astype(BF16) for s_c, m_c in zip(s, m_new)]
            pv = [_dot(p_c, va[hh]) for p_c, (hh, _) in zip(p, chains)]
            alpha = [jnp.exp(mp - m_c) for mp, m_c in zip(m_prev, m_new)]
            for r in range(ATTN_ROW_PARTS):
                (m0, m1), (al0, al1), (pv0, pv1) = [x[2 * r:2 * r + 2] for x in (m_new, alpha, pv)]
                rw = rows[2 * r]
                l0 = al0 * l_ref[rw, 0:1] + pv0[:, HEAD_DIM:HEAD_DIM + 1]
                l1 = al1 * l_ref[rw, HEAD_DIM:HEAD_DIM + 1] + pv1[:, 0:1]
                acc_ref[rw, :] = acc_ref[rw, :] * jnp.where(first, al0, al1) + jnp.where(first, pv0, pv1)
                m_ref[rw, :] = jnp.where(first, m0, m1)
                l_ref[rw, :] = jnp.where(first, l0, l1)

        @pl.when(ki < qi)
        def _():
            sweep(False)

        @pl.when(ki == qi)
        def _():
            sweep(True)
            o_ref[...] = acc_ref[...] / l_ref[...]
            lse_ref[...] = m_ref[...] + jnp.log(l_ref[...])

    return pl.pallas_call(
        body, name="attention_fwd",
        out_shape=[jax.ShapeDtypeStruct((t, ATTN_W), F32), jax.ShapeDtypeStruct((t, ATTN_W), F32)],
        grid_spec=pltpu.PrefetchScalarGridSpec(
            num_scalar_prefetch=2, grid=(4, int(q_tab.shape[0])),
            in_specs=[pl.BlockSpec((tq, 128), lambda p, s, qt, kt: (qt[s], p)),
                      pl.BlockSpec((tk, 128), lambda p, s, qt, kt: (kt[s], 4 + p)),
                      pl.BlockSpec((tk, 128), lambda p, s, qt, kt: (kt[s], 8 + p)),
                      pl.BlockSpec((None, tq, 128), lambda p, s, qt, kt: (p, qt[s], 0)),
                      pl.BlockSpec((None, tk, 128), lambda p, s, qt, kt: (p, kt[s], 0))],
            out_specs=[pl.BlockSpec((tq, 128), lambda p, s, qt, kt: (qt[s], p)),
                       pl.BlockSpec((tq, 128), lambda p, s, qt, kt: (qt[s], p))],
            scratch_shapes=[pltpu.VMEM((tq, 128), F32)] * 3),
        compiler_params=_params(("parallel", "arbitrary")),
    )(q_tab, k_tab, z, z, z, aug_q, aug_k)


def _attn_bwd(z, aug_q, aug_k, o, do):
    t = z.shape[0]
    tq = tk = ROW_TILE
    nq = t // tq
    q_tab, k_tab = _attn_steps(nq, by_key=True)
    tn = (((0,), (0,)), ((), ()))

    def body(qt_ref, kt_ref, q_ref, k_ref, v_ref, aq_ref, ak_ref, o_ref, do_ref,
             dq_ref, dk_ref, dv_ref, dfk_ref, dfq_ref):
        step = pl.program_id(1)
        qi, ki = qt_ref[step], kt_ref[step]
        rows = pl.ds(pl.multiple_of(qi * tq, tq), tq)

        @pl.when(ki == 0)
        def _():
            dq_ref[rows, :] = jnp.zeros((tq, 128), F32)
            dfq_ref[rows, :] = jnp.zeros((tq, 128), F32)

        @pl.when(qi == ki)
        def _():
            dk_ref[...] = jnp.zeros_like(dk_ref)
            dv_ref[...] = jnp.zeros_like(dv_ref)
            dfk_ref[...] = jnp.zeros_like(dfk_ref)

        def sweep(diagonal):
            first = _lane_halves()
            lane = lax.broadcasted_iota(jnp.int32, (1, 128), 1)
            scale = HEAD_DIM ** -0.5
            q = (q_ref[...] * scale).astype(BF16)
            k = k_ref[...].astype(BF16)
            v = v_ref[...].astype(BF16)
            do_ = do_ref[...]
            do16 = do_.astype(BF16)
            od = o_ref[...] * do_
            aq, ak = aq_ref[...], ak_ref[...]
            halves = (first, jnp.logical_not(first))
            a0, a1 = HEAD_DIM, 0
            dos, vs = [], []
            for lanes, a in zip(halves, (a0, a1)):
                d_hi, d_mid, d_lo = _split3(jnp.sum(jnp.where(lanes, od, 0.0), axis=1, keepdims=True))
                minus_delta = jnp.where(lane == a, -d_hi, jnp.where(lane == a + 1, -d_mid,
                                        jnp.where(lane == a + 2, -d_lo, jnp.zeros((), BF16))))
                dos.append(jnp.where(lanes, do16, minus_delta))
                vs.append(jnp.where(lanes, v, ((lane >= a) & (lane < a + 3)).astype(BF16)))
            s = [_dot_nt(jnp.where(lanes, q, aq), jnp.where(lanes, k, ak)) for lanes in halves]
            dp = [_dot_nt(do_h, v_h) for do_h, v_h in zip(dos, vs)]
            p = [jnp.exp(s_h) for s_h in s]
            if diagonal:
                p = [jnp.where(_causal_mask(tq, tk), p_h, 0.0) for p_h in p]
            ds16 = [(p_h * dp_h).astype(BF16) for p_h, dp_h in zip(p, dp)]
            dv0, dv1 = [lax.dot_general(p_h.astype(BF16), jnp.where(lanes, do16, jnp.zeros((), BF16)), tn,
                                        preferred_element_type=F32) for p_h, lanes in zip(p, halves)]
            dk0, dk1 = [lax.dot_general(ds_h, jnp.where(lanes, q, _lane_one(a)), tn, preferred_element_type=F32)
                        for ds_h, lanes, a in zip(ds16, halves, (a0, a1))]
            dq0, dq1 = [_dot(ds_h, jnp.where(lanes, k, _lane_one(a))) for ds_h, lanes, a in zip(ds16, halves, (a0, a1))]
            dq_ref[rows, :] += jnp.where(first, dq0, dq1) * scale
            dfq_ref[rows, :] += jnp.where(first, dq0[:, a0:a0 + 1], dq1[:, a1:a1 + 1])
            dk_ref[...] += jnp.where(first, dk0, dk1)
            dfk_ref[...] += jnp.where(first, dk0[:, a0:a0 + 1], dk1[:, a1:a1 + 1])
            dv_ref[...] += dv0 + dv1

        @pl.when(qi > ki)
        def _():
            sweep(False)

        @pl.when(qi == ki)
        def _():
            sweep(True)

    qrow = lambda p, s, qt, kt: (qt[s], p)
    krow = lambda p, s, qt, kt: (kt[s], p)
    return pl.pallas_call(
        body, name="attention_bwd",
        out_shape=[jax.ShapeDtypeStruct((t, ATTN_W), F32)] * 5,
        grid_spec=pltpu.PrefetchScalarGridSpec(
            num_scalar_prefetch=2, grid=(4, int(q_tab.shape[0])),
            in_specs=[pl.BlockSpec((tq, 128), qrow),
                      pl.BlockSpec((tk, 128), lambda p, s, qt, kt: (kt[s], 4 + p)),
                      pl.BlockSpec((tk, 128), lambda p, s, qt, kt: (kt[s], 8 + p)),
                      pl.BlockSpec((None, tq, 128), lambda p, s, qt, kt: (p, qt[s], 0)),
                      pl.BlockSpec((None, tk, 128), lambda p, s, qt, kt: (p, kt[s], 0)),
                      pl.BlockSpec((tq, 128), qrow), pl.BlockSpec((tq, 128), qrow)],
            out_specs=[pl.BlockSpec((t, 128), lambda p, s, qt, kt: (0, p)),
                       pl.BlockSpec((tk, 128), krow), pl.BlockSpec((tk, 128), krow), pl.BlockSpec((tk, 128), krow),
                       pl.BlockSpec((t, 128), lambda p, s, qt, kt: (0, p))]),
        compiler_params=_params(("parallel", "arbitrary"), VMEM_BIG),
    )(q_tab, k_tab, z, z, z, aug_q, aug_k, o, do)


def _shifted(prev_rows, x, shift):
    tm = x.shape[0]
    return pltpu.roll(jnp.concatenate([prev_rows, x], axis=0), shift, 0)[8:8 + tm]


def _ahead(x, next_rows, shift):
    tm = x.shape[0]
    return pltpu.roll(jnp.concatenate([x, next_rows], axis=0), tm + 8 - shift, 0)[0:tm]


def _conv_col0(z):
    return (z.shape[1] - F_PAD - 3 * CONV_W) // CONV_W


def _conv_specs(tm, c0):
    cols = (c0, c0 + 1, c0 + 2)
    tiles = [pl.BlockSpec((tm, CONV_W), functools.partial(lambda i, c: (i, c), c=c)) for c in cols]
    halos = [pl.BlockSpec((8, CONV_W), functools.partial(lambda i, c: (jnp.maximum(i * (tm // 8) - 1, 0), c), c=c))
             for c in cols]
    return tiles, halos


def _conv_gate(z, conv_w):
    t = z.shape[0]
    tm = ROW_TILE
    nt = t // tm

    def body(cb_ref, cc_ref, ci_ref, hc_ref, hi_ref, w_ref, g_ref, gt_ref):
        i = pl.program_id(0)
        cc = cc_ref[...] * ci_ref[...]
        prev = jnp.where(i > 0, hc_ref[...] * hi_ref[...], 0.0)
        conv = w_ref[0:1, :] * _shifted(prev, cc, 2) + w_ref[1:2, :] * _shifted(prev, cc, 1) + w_ref[2:3, :] * cc
        g = cb_ref[...] * conv
        g_ref[...] = g.astype(BF16)
        gt_ref[...] = g.T.astype(BF16)

    (cb, cc, ci), (_, hc, hi) = _conv_specs(tm, _conv_col0(z))
    return pl.pallas_call(
        body, name="conv_gate_fwd",
        out_shape=[jax.ShapeDtypeStruct((t, CONV_W), BF16), jax.ShapeDtypeStruct((CONV_W, t), BF16)],
        grid=(nt,),
        in_specs=[cb, cc, ci, hc, hi, pl.BlockSpec((8, CONV_W), lambda i: (0, 0))],
        out_specs=[pl.BlockSpec((tm, CONV_W), lambda i: (i, 0)), pl.BlockSpec((CONV_W, tm), lambda i: (0, i))],
        compiler_params=_params(("parallel",)),
    )(z, z, z, z, z, conv_w)


def _conv_bwd(z, dg, conv_w):
    t = z.shape[0]
    tm = ROW_TILE
    nt = t // tm

    def body(cb_ref, cc_ref, ci_ref, hc_ref, hi_ref, dg_ref, ncb_ref, ndg_ref, w_ref, dz_ref, dw_ref):
        i = pl.program_id(0)

        @pl.when(i == 0)
        def _():
            dw_ref[...] = jnp.zeros_like(dw_ref)

        cb, c_c, c_in = cb_ref[...], cc_ref[...], ci_ref[...]
        cc = c_c * c_in
        prev = jnp.where(i > 0, hc_ref[...] * hi_ref[...], 0.0)
        cc1, cc2 = _shifted(prev, cc, 1), _shifted(prev, cc, 2)
        w0, w1, w2 = w_ref[0:1, :], w_ref[1:2, :], w_ref[2:3, :]
        conv = w0 * cc2 + w1 * cc1 + w2 * cc
        dgv = dg_ref[...]
        dconv = dgv * cb
        nxt = jnp.where(i < nt - 1, ndg_ref[...] * ncb_ref[...], 0.0)
        dcc = w2 * dconv + w1 * _ahead(dconv, nxt, 1) + w0 * _ahead(dconv, nxt, 2)
        dz_ref[:, 0:CONV_W] = (dgv * conv).astype(BF16)
        dz_ref[:, CONV_W:2 * CONV_W] = (dcc * c_in).astype(BF16)
        dz_ref[:, 2 * CONV_W:] = (dcc * c_c).astype(BF16)
        dw_ref[0:1, :] += jnp.sum(dconv * cc2, axis=0, keepdims=True)
        dw_ref[1:2, :] += jnp.sum(dconv * cc1, axis=0, keepdims=True)
        dw_ref[2:3, :] += jnp.sum(dconv * cc, axis=0, keepdims=True)

    c0 = _conv_col0(z)
    (cb, cc, ci), (_, hc, hi) = _conv_specs(tm, c0)
    nxt = lambda i, c: (jnp.minimum((i + 1) * (tm // 8), t // 8 - 1), c)
    return pl.pallas_call(
        body, name="conv_gate_bwd",
        out_shape=[jax.ShapeDtypeStruct((t, 3 * CONV_W), BF16), jax.ShapeDtypeStruct((8, CONV_W), F32)],
        grid=(nt,),
        in_specs=[cb, cc, ci, hc, hi, pl.BlockSpec((tm, CONV_W), lambda i: (i, 0)),
                  pl.BlockSpec((8, CONV_W), lambda i: nxt(i, c0)), pl.BlockSpec((8, CONV_W), lambda i: nxt(i, 0)),
                  pl.BlockSpec((8, CONV_W), lambda i: (0, 0))],
        out_specs=[pl.BlockSpec((tm, 3 * CONV_W), lambda i: (i, 0)), pl.BlockSpec((8, CONV_W), lambda i: (0, 0))],
        compiler_params=_params(("arbitrary",)),
    )(z, z, z, z, z, dg, z, dg, conv_w)


def _branch_mix(z, o, g, w_ab, w_cb, d):
    t = z.shape[0]
    tm = ROW_TILE
    ga_col = 0

    def body(o_ref, g_ref, ga_ref, gc_ref, wa_ref, wc_ref, mp_ref, mpt_ref, ot_ref):
        o_ = o_ref[...]
        ya = _dot(o_.astype(BF16), wa_ref[...])
        yc = _dot(g_ref[...], wc_ref[...])
        mp = _sigmoid(ga_ref[...]) * ya + _sigmoid(gc_ref[...]) * yc
        mp_ref[...] = mp.astype(BF16)
        mpt_ref[...] = mp.T.astype(BF16)
        ot_ref[...] = o_.T.astype(BF16)

    return pl.pallas_call(
        body, name="branch_mix_fwd",
        out_shape=[jax.ShapeDtypeStruct((t, d), BF16), jax.ShapeDtypeStruct((d, t), BF16),
                   jax.ShapeDtypeStruct((ATTN_W, t), BF16)],
        grid=(t // tm,),
        in_specs=[pl.BlockSpec((tm, ATTN_W), lambda i: (i, 0)), pl.BlockSpec((tm, CONV_W), lambda i: (i, 0)),
                  pl.BlockSpec((tm, d), lambda i: (i, ga_col)), pl.BlockSpec((tm, d), lambda i: (i, ga_col + 1)),
                  pl.BlockSpec((ATTN_W, d), lambda i: (0, 0)), pl.BlockSpec((CONV_W, d), lambda i: (0, 0))],
        out_specs=[pl.BlockSpec((tm, d), lambda i: (i, 0)), pl.BlockSpec((d, tm), lambda i: (0, i)),
                   pl.BlockSpec((ATTN_W, tm), lambda i: (0, i))],
        compiler_params=_params(("parallel",), VMEM_BIG),
    )(o, g, z, z, w_ab, w_cb)


def _branch_bwd(z, o, g, dmixed, w_out, w_ab, w_cb, d):
    t = z.shape[0]
    tm = ROW_TILE // 2
    ga_col = 0

    def body(dm_ref, o_ref, g_ref, ga_ref, gc_ref, wo_ref, wa_ref, wc_ref, dya_ref, dyc_ref, dgt_ref, do_ref, dg_ref):
        dmp = _dot_nt(dm_ref[...], wo_ref[...])
        ya = _dot(o_ref[...].astype(BF16), wa_ref[...])
        yc = _dot(g_ref[...], wc_ref[...])
        sa, sc = _sigmoid(ga_ref[...]), _sigmoid(gc_ref[...])
        dya = (dmp * sa).astype(BF16)
        dyc = (dmp * sc).astype(BF16)
        dya_ref[...] = dya
        dyc_ref[...] = dyc
        dgt_ref[:, :d] = (dmp * ya * sa * (1.0 - sa)).astype(BF16)
        dgt_ref[:, d:] = (dmp * yc * sc * (1.0 - sc)).astype(BF16)
        do_ref[...] = _dot_nt(dya, wa_ref[...])
        dg_ref[...] = _dot_nt(dyc, wc_ref[...])

    row = lambda i: (i, 0)
    fixed = lambda i: (0, 0)
    return pl.pallas_call(
        body, name="branch_mix_bwd",
        out_shape=[jax.ShapeDtypeStruct((t, d), BF16), jax.ShapeDtypeStruct((t, d), BF16),
                   jax.ShapeDtypeStruct((t, 2 * d), BF16), jax.ShapeDtypeStruct((t, ATTN_W), F32),
                   jax.ShapeDtypeStruct((t, CONV_W), F32)],
        grid=(t // tm,),
        in_specs=[pl.BlockSpec((tm, d), row), pl.BlockSpec((tm, ATTN_W), row), pl.BlockSpec((tm, CONV_W), row),
                  pl.BlockSpec((tm, d), lambda i: (i, ga_col)), pl.BlockSpec((tm, d), lambda i: (i, ga_col + 1)),
                  pl.BlockSpec((d, d), fixed), pl.BlockSpec((ATTN_W, d), fixed), pl.BlockSpec((CONV_W, d), fixed)],
        out_specs=[pl.BlockSpec((tm, d), row), pl.BlockSpec((tm, d), row), pl.BlockSpec((tm, 2 * d), row),
                   pl.BlockSpec((tm, ATTN_W), row), pl.BlockSpec((tm, CONV_W), row)],
        compiler_params=_params(("parallel",), VMEM_BIG),
    )(dmixed, o, g, z, z, w_out, w_ab, w_cb)


def _loss_norm_bwd(h, target, f, g_post, alpha):
    t, d = h.shape
    tm = N_FRONT

    def body(h_ref, t_ref, f_ref, g_ref, dh_ref, df_ref, dg_ref, loss_ref):
        i = pl.program_id(0)

        @pl.when(i == 0)
        def _():
            loss_ref[...] = jnp.zeros_like(loss_ref)
            dg_ref[...] = jnp.zeros_like(dg_ref)

        err = jnp.where(i > 0, h_ref[...] - t_ref[...], 0.0)
        dy = err * (1.0 / d)
        dh_ref[...] = dy
        per_row = jnp.sum(err * err, axis=1, keepdims=True) * (1.0 / d)
        loss_ref[...] += 0.5 * jnp.sum(per_row, axis=0, keepdims=True)
        dx, dg = _rms_bwd(f_ref[...], g_ref[...], dy)
        df_ref[...] = (alpha * dx).astype(BF16)
        dg_ref[...] += alpha * dg

    row = pl.BlockSpec((tm, d), lambda i: (i, 0))
    vec = pl.BlockSpec((1, d), lambda i: (0, 0))
    return pl.pallas_call(
        body, name="loss_and_post_norm_bwd",
        out_shape=[jax.ShapeDtypeStruct((t, d), F32), jax.ShapeDtypeStruct((t, d), BF16),
                   jax.ShapeDtypeStruct((1, d), F32), jax.ShapeDtypeStruct((1, 128), F32)],
        grid=(t // tm,),
        in_specs=[row, pl.BlockSpec((tm, d), lambda i: (jnp.maximum(i - 1, 0), 0)), row, vec],
        out_specs=[row, row, vec, pl.BlockSpec((1, 128), lambda i: (0, 0))],
        compiler_params=_params(("arbitrary",)),
    )(h, target, f, g_post)


def _norm_bwd(name, x, g, dy, alpha):
    t, d = x.shape
    tm = ROW_TILE

    def body(x_ref, g_ref, dy_ref, dx_ref, dg_ref):
        @pl.when(pl.program_id(0) == 0)
        def _():
            dg_ref[...] = jnp.zeros_like(dg_ref)

        dx, dg = _rms_bwd(x_ref[...], g_ref[...], dy_ref[...])
        dx_ref[...] = (alpha * dx).astype(BF16)
        dg_ref[...] += alpha * dg

    row = pl.BlockSpec((tm, d), lambda i: (i, 0))
    vec = pl.BlockSpec((1, d), lambda i: (0, 0))
    return pl.pallas_call(
        body, name=name,
        out_shape=[jax.ShapeDtypeStruct((t, d), BF16), jax.ShapeDtypeStruct((1, d), F32)],
        grid=(t // tm,), in_specs=[row, vec, row], out_specs=[row, vec],
        compiler_params=_params(("arbitrary",)),
    )(x, g, dy)


def _ffn_bwd_mid(name, df, w_out, ab):
    t, d = df.shape
    cw = ab.shape[1] // 4
    tm = ROW_TILE

    def body(df_ref, w_ref, ab_ref, o_ref):
        ds = _dot_nt(df_ref[...], w_ref[...])
        a = ab_ref[:, :cw].astype(F32)
        b = ab_ref[:, cw:].astype(F32)
        sg = _sigmoid(a)
        o_ref[:, :cw] = (ds * b * (sg * (1.0 + a * (1.0 - sg)))).astype(BF16)
        o_ref[:, cw:] = (ds * (a * sg)).astype(BF16)

    return pl.pallas_call(
        body, name=name, out_shape=jax.ShapeDtypeStruct((t, 4 * cw), BF16),
        grid=(2, t // tm),
        in_specs=[pl.BlockSpec((tm, d), lambda j, i: (i, 0)), pl.BlockSpec((cw, d), lambda j, i: (j, 0)),
                  pl.BlockSpec((tm, 2 * cw), lambda j, i: (i, j))],
        out_specs=pl.BlockSpec((tm, 2 * cw), lambda j, i: (i, j)),
        compiler_params=_params(("parallel", "parallel"), VMEM_BIG),
    )(df, w_out, ab)


def _mm_nt_norm_bwd(name, dy, w, h, g, dh_in):
    t, kdim = dy.shape
    d = h.shape[1]
    tm = ROW_TILE // 2
    slots = w.ndim == 3

    def body(dy_ref, w_ref, h_ref, g_ref, dhi_ref, dh_ref, dg_ref):
        @pl.when(pl.program_id(0) == 0)
        def _():
            dg_ref[...] = jnp.zeros_like(dg_ref)

        if slots:
            cw = w_ref.shape[2]
            dn = _dot_nt(dy_ref[:, 0:cw], w_ref[_slot_of(0)])
            for k in range(1, 4):
                dn += _dot_nt(dy_ref[:, k * cw:(k + 1) * cw], w_ref[_slot_of(k)])
        else:
            dn = _dot_nt(dy_ref[...], w_ref[...])
        dx, dg = _rms_bwd(h_ref[...], g_ref[...], dn)
        dh_ref[...] = dhi_ref[...] + dx
        dg_ref[...] += dg

    row = pl.BlockSpec((tm, d), lambda i: (i, 0))
    vec = pl.BlockSpec((1, d), lambda i: (0, 0))
    return pl.pallas_call(
        body, name=name,
        out_shape=[jax.ShapeDtypeStruct((t, d), F32), jax.ShapeDtypeStruct((1, d), F32)],
        grid=(t // tm,),
        in_specs=[pl.BlockSpec((tm, kdim), lambda i: (i, 0)), pl.BlockSpec(w.shape, lambda i: (0,) * w.ndim),
                  row, vec, row],
        out_specs=[row, vec],
        compiler_params=_params(("arbitrary",), VMEM_BIG),
    )(dy, w, h, g, dh_in)


def _gate_bwd(dfq, dfk, z, b_pad, f_col):
    t = z.shape[0]
    tm = ROW_TILE
    nt = t // tm

    def body(dq_ref, dk_ref, z_ref, b_ref, dz_ref, db_ref, carry_ref):
        i = pl.program_id(0)

        @pl.when(i == 0)
        def _():
            carry_ref[...] = jnp.zeros_like(carry_ref)
            db_ref[...] = jnp.zeros_like(db_ref)

        pick = (lax.broadcasted_iota(jnp.int32, (ATTN_W, 128), 0)
                == HEAD_DIM * lax.broadcasted_iota(jnp.int32, (ATTN_W, 128), 1)).astype(F32)
        d_heads = jnp.dot(dq_ref[...] - dk_ref[...], pick, preferred_element_type=F32,
                          precision=lax.Precision.HIGHEST)
        tri = (lax.broadcasted_iota(jnp.int32, (tm, tm), 0) <= lax.broadcasted_iota(jnp.int32, (tm, tm), 1))
        tail = jnp.dot(tri.astype(F32), d_heads, preferred_element_type=F32, precision=lax.Precision.HIGHEST)
        tail = tail + carry_ref[0:1, :]
        carry_ref[...] = jnp.broadcast_to(tail[0:1, :], carry_ref.shape)
        row = (nt - 1 - i) * tm + lax.broadcasted_iota(jnp.int32, (tm, 1), 0)
        dlogit = jnp.where(row >= ROW_PAD, tail * _sigmoid(-(z_ref[...] + b_ref[...])), 0.0)
        dz_ref[...] = jnp.zeros_like(dz_ref)
        dz_ref[:, 0:128] = dlogit.astype(BF16)
        db_ref[...] += jnp.sum(dlogit, axis=0, keepdims=True)

    rev = lambda i: (nt - 1 - i, 0)
    return pl.pallas_call(
        body, name="forget_gate_bwd",
        out_shape=[jax.ShapeDtypeStruct((t, F_PAD), BF16), jax.ShapeDtypeStruct((1, 128), F32)],
        grid=(nt,),
        in_specs=[pl.BlockSpec((tm, ATTN_W), rev), pl.BlockSpec((tm, ATTN_W), rev),
                  pl.BlockSpec((tm, 128), lambda i: (nt - 1 - i, f_col // 128)),
                  pl.BlockSpec((1, 128), lambda i: (0, 0))],
        out_specs=[pl.BlockSpec((tm, F_PAD), rev), pl.BlockSpec((1, 128), lambda i: (0, 0))],
        scratch_shapes=[pltpu.VMEM((8, 128), F32)],
        compiler_params=_params(("arbitrary",)),
    )(dfq, dfk, z, b_pad)


def _ffn_fwd(tag, n, w_in4, w_out, h, g_post, g_next):
    ab, s, s_t = _ffn_in(f"{tag}_in_fwd", n, w_in4)
    outs = _mm_resid_norm(f"{tag}_out_fwd", s, w_out, h, g_post, 0.5, g_next)
    return ab, s_t, outs


def _ffn_bwd_weights(tag, df, ab, s_t, n_t, w_in4, w_out):
    d, cw = w_in4.shape[1], w_in4.shape[2]
    t = df.shape[0]
    dw_out = _weight_grad(f"{tag}_dw_out", s_t, df, d, out_rows=cw // 2)
    dab = _ffn_bwd_mid(f"{tag}_mid_bwd", df, w_out, ab)
    bk = _k_tile(t)
    dw_in = _matmul(
        f"{tag}_dw_in", n_t, dab, jax.ShapeDtypeStruct((4, d, cw), F32), (1, 4, t // bk),
        pl.BlockSpec((d, bk), lambda a, b, k: (0, k)), pl.BlockSpec((bk, cw), lambda a, b, k: (k, b)),
        pl.BlockSpec((None, d, cw), lambda a, b, k: (_slot_of(b), 0, 0)), vmem=VMEM_BIG)
    return dab, dw_in, dw_out


LOSS_ROW = 12


def _pack_small(meta, conv, gains, b_forget, loss=None):
    d = gains[0].shape[1]
    rows = [meta.reshape(4, d), jnp.pad(conv.reshape(1, 3 * 128), ((0, 0), (0, d - 3 * 128)))]
    rows += list(gains) + [jnp.pad(b_forget, ((0, 0), (0, d - HEADS)))]
    last = jnp.zeros((4, d), F32)
    if loss is not None:
        last = jnp.pad(loss.reshape(1, 1), ((0, 3), (0, d - 1)))
    return jnp.concatenate(rows + [last], axis=0)


def _unpack_small(block):
    d = block.shape[1]
    meta = block[0:4].reshape(N_META, d // 4)
    conv = block[4, :3 * 128].reshape(1, 3, 128)
    gains = [block[5 + i:6 + i] for i in range(6)]
    return meta, conv, gains, block[11:12, :HEADS]


def kernel(x, meta_tokens, w_in, b_forget, conv_w, w_attn_branch, w_conv_branch, w_out, g_ffn1_pre, g_ffn1_post, w_ffn1_in, w_ffn1_out, g_mix_pre, g_mix_post, g_ffn2_pre, g_ffn2_post, w_ffn2_in, w_ffn2_out, loss_target, m_meta_tokens, m_w_in, m_b_forget, m_conv_w, m_w_attn_branch, m_w_conv_branch, m_w_out, m_g_ffn1_pre, m_g_ffn1_post, m_w_ffn1_in, m_w_ffn1_out, m_g_mix_pre, m_g_mix_post, m_g_ffn2_pre, m_g_ffn2_post, m_w_ffn2_in, m_w_ffn2_out, v_meta_tokens, v_w_in, v_b_forget, v_conv_w, v_w_attn_branch, v_w_conv_branch, v_w_out, v_g_ffn1_pre, v_g_ffn1_post, v_w_ffn1_in, v_w_ffn1_out, v_g_mix_pre, v_g_mix_post, v_g_ffn2_pre, v_g_ffn2_post, v_w_ffn2_in, v_w_ffn2_out):
    seq, d = x.shape[1], x.shape[2]
    t = seq + N_FRONT
    n_main = 3 * ATTN_W + 3 * CONV_W + 2 * d
    nz = n_main + F_PAD
    f_lo = 3 * ATTN_W
    c_arr = lax.axis_index("c").astype(jnp.int32).reshape(1)

    cs = w_in.shape[2]
    cs_pad = -(-cs // 64) * 64

    def w_in_rows(a):
        return jnp.pad(jnp.transpose(a[0]), ((0, cs_pad - cs), (0, 0)))

    big = [w_in_rows(w_in), w_attn_branch[0], w_conv_branch[0], w_out[0], w_ffn1_in[0], w_ffn1_out[0], w_ffn2_in[0],
           w_ffn2_out[0]]
    small_gather = jnp.concatenate(
        [meta_tokens.reshape(4, d), jnp.pad(conv_w.reshape(1, 3 * 128), ((0, 0), (0, d - 3 * 128))),
         jnp.zeros((11, d), F32)], axis=0)
    w_f1_in4, small4 = _all_gather([big[4].astype(BF16), small_gather])
    (second, rest), small4 = lax.optimization_barrier(
        (([big[5].astype(BF16)], [big[i].astype(BF16) for i in (0, 1, 2, 3, 6, 7)]), small4))
    second_gathered = _all_gather_async("all_gather_ffn1_out", second, 5)
    rest_gathered = _all_gather_async("all_gather_rest", rest, 1)
    meta_full = jnp.transpose(small4[:, 0:4].reshape(4, N_META, d // 4), (1, 0, 2)).reshape(N_META, d)
    conv_full = jnp.transpose(small4[:, 4, :3 * 128].reshape(4, 3, 128), (1, 0, 2)).reshape(3, CONV_W)
    conv_pad = jnp.pad(conv_full, ((0, 5), (0, 0)))
    b_pad = jnp.pad(b_forget, ((0, 0), (0, 128 - HEADS)))

    h0 = jnp.concatenate([jnp.zeros((ROW_PAD, d), F32), meta_full, x[0]], axis=0)
    n1, n1_t = _norm_fwd("ffn1_pre_norm", h0, g_ffn1_pre)
    ab1, s1, s1_t = _ffn_in("ffn1_in_fwd", n1, w_f1_in4)
    w_f1_out = second_gathered(s1, [0])[0].reshape(-1, d)
    f1, h1, u, u_t = _mm_resid_norm("ffn1_out_fwd", s1, w_f1_out, h0, g_ffn1_post, 0.5, g_mix_pre)

    w_in4, w_ab4, w_cb4, w_out4, w_f2_in4, w_f2_out4 = rest_gathered(u, range(6))
    w_in_t = w_in4[:, :cs].reshape(4 * cs, d)
    g_lo = f_lo + HEADS + 3 * CONV_W
    w_in_pad = jnp.concatenate(
        [w_in_t[:f_lo], w_in_t[g_lo:], w_in_t[f_lo + HEADS:g_lo], w_in_t[f_lo:f_lo + HEADS],
         jnp.zeros((F_PAD - HEADS, d), BF16)], axis=0)
    w_ab = jnp.transpose(w_ab4, (1, 0, 2)).reshape(ATTN_W, d)
    w_cb = jnp.transpose(w_cb4, (1, 0, 2)).reshape(CONV_W, d)
    w_out_full = w_out4.reshape(d, d)
    w_f2_out = w_f2_out4.reshape(-1, d)
    qkv, z = _in_proj(u, w_in_pad)
    f_col = z.shape[1] - F_PAD
    f_cum = _gate_prep(z, b_pad, f_col)
    f_heads = f_cum[:, :HEADS]
    o, lse = _attn_fwd(qkv, *_attn_bias_operands(f_heads))
    g, g_t = _conv_gate(z, conv_pad)
    mp, mp_t, o_t = _branch_mix(z, o, g, w_ab, w_cb, d)
    mixed, h2, n2, n2_t = _mm_resid_norm("mix_out_fwd", mp, w_out_full, h1, g_mix_post, 1.0, g_ffn2_pre)
    ab2, s2_t, (f2, h3) = _ffn_fwd("ffn2", n2, w_f2_in4, w_f2_out, h2, g_ffn2_post, None)
    dh3, df2, dg_f2_post, loss_part = _loss_norm_bwd(h3, loss_target[0], f2, g_ffn2_post, 0.5)

    reduced = {}

    def reduce_scatter(label, tags, slots, sequencer_id, hold=None, got=None, after=None):
        if got is None:
            got = _pair_send_halves(f"grad_pair_exchange_{label}", slots)
        else:
            got, _ = lax.optimization_barrier((got, after))
        sums = [_pair_add(tag, s, a, c_arr, F32 if tag == "small" else BF16) for tag, s, a in zip(tags, slots, got)]
        sums, hold = lax.optimization_barrier((sums, hold))
        if sequencer_id is None:
            arrived = _chip_scatter(f"grad_chip_scatter_{label}", sums)
        else:
            arrived = _chip_scatter_async(f"grad_chip_scatter_{label}", sums, sequencer_id)
        mine = [_chip_add(tag, a) for tag, a in zip(tags, arrived)]
        reduced.update(zip(tags, zip(mine, _pair_swap(f"grad_pair_swap_{label}", mine))))
        return hold

    dab2, dw_f2_in, dw_f2_out = _ffn_bwd_weights("ffn2", df2, ab2, s2_t, n2_t, w_f2_in4, w_f2_out)
    ffn2_slots = [dw_f2_in, dw_f2_out.reshape(4, -1, d)]
    ffn2_got = _pair_send_halves_async("grad_pair_exchange_ffn2", ffn2_slots, 6)
    dh2, dg_f2_pre = _mm_nt_norm_bwd("ffn2_in_bwd", dab2, w_f2_in4, h2, g_ffn2_pre, dh3)
    reduce_scatter("ffn2", ["w_ffn2_in", "w_ffn2_out"], ffn2_slots, 2, got=ffn2_got, after=dh2)
    dmixed, dg_mix_post = _norm_bwd("mix_post_norm_bwd", mixed, g_mix_post, dh2, 1.0)
    dw_out = _weight_grad("mix_dw_out", mp_t, dmixed, d)
    dya, dyc, dgates, do, dgconv = _branch_bwd(z, o, g, dmixed, w_out_full, w_ab, w_cb, d)
    dw_ab = _weight_grad("mix_dw_attn_branch", o_t, dya, d)
    dw_cb = _weight_grad("mix_dw_conv_branch", g_t, dyc, d)
    dz_conv, dconv_w = _conv_bwd(z, dgconv, conv_pad)
    front = lax.broadcasted_iota(jnp.int32, (t, 1), 0) < ROW_PAD
    lse_heads = jnp.where(front, 1e9, lse[:, ::HEAD_DIM])
    dq, dk, dv, dfk, dfq = _attn_bwd(qkv, *_attn_bias_operands(f_heads, lse_heads), o, do)
    dz_f, db_forget = _gate_bwd(dfq, dfk, z, b_pad, f_col)
    dz_pieces = {"q": dq, "k": dk, "v": dv, "gates": dgates, "conv": dz_conv, "f": dz_f}
    dh1, dg_mix_pre = _mix_in_bwd(list(dz_pieces.values()), w_in_pad, h1, g_mix_pre, dh2)
    dw_t = {name: _weight_grad_t(f"mix_dw_in_{name}", u_t, piece) for name, piece in dz_pieces.items()}
    dw_in_t = jnp.concatenate(
        [dw_t["q"], dw_t["k"], dw_t["v"], dw_t["f"][:HEADS], dw_t["conv"], dw_t["gates"]], axis=0)
    mix_slots = [jnp.pad(dw_in_t.reshape(4, cs, d), ((0, 0), (0, cs_pad - cs), (0, 0))),
                 jnp.transpose(dw_ab.reshape(ATTN_W, 4, d // 4), (1, 0, 2)),
                 jnp.transpose(dw_cb.reshape(CONV_W, 4, d // 4), (1, 0, 2)),
                 dw_out.reshape(4, d // 4, d)]
    mix_got = _pair_send_halves_async("grad_pair_exchange_mix", mix_slots, 7)
    df1, dg_f1_post = _norm_bwd("ffn1_post_norm_bwd", f1, g_ffn1_post, dh1, 0.5)
    reduce_scatter("mix", ["w_in", "w_attn_branch", "w_conv_branch", "w_out"], mix_slots, 3, got=mix_got, after=df1)
    dab1, dw_f1_in, dw_f1_out = _ffn_bwd_weights("ffn1", df1, ab1, s1_t, n1_t, w_f1_in4, w_f1_out)
    dab1 = reduce_scatter("ffn1", ["w_ffn1_in", "w_ffn1_out"], [dw_f1_in, dw_f1_out.reshape(4, -1, d)], 4, dab1)
    dh0, dg_f1_pre = _mm_nt_norm_bwd("ffn1_in_bwd", dab1, w_f1_in4, h0, g_ffn1_pre, dh1)
    grad_x = dh0[N_FRONT:][None]
    dmeta = dh0[ROW_PAD:N_FRONT]
    small_grad = jnp.stack([
        _pack_small(dmeta[:, j * (d // 4):(j + 1) * (d // 4)], dconv_w[:3, j * 128:(j + 1) * 128],
                    [dg_f1_pre, dg_f1_post, dg_mix_pre, dg_mix_post, dg_f2_pre, dg_f2_post], db_forget[:, :HEADS],
                    loss_part[0, 0])
        for j in range(4)])
    reduce_scatter("small", ["small"], [small_grad], None)
    tags =["w_in", "w_attn_branch", "w_conv_branch", "w_out", "w_ffn1_in", "w_ffn1_out", "w_ffn2_in", "w_ffn2_out", "small"]
    halves = [reduced[tag][0] for tag in tags]
    others = [reduced[tag][1] for tag in tags]

    small = [g_ffn1_pre, g_ffn1_post, g_mix_pre, g_mix_post, g_ffn2_pre, g_ffn2_post]
    small_m = [m_g_ffn1_pre, m_g_ffn1_post, m_g_mix_pre, m_g_mix_post, m_g_ffn2_pre, m_g_ffn2_post]
    small_v = [v_g_ffn1_pre, v_g_ffn1_post, v_g_mix_pre, v_g_mix_post, v_g_ffn2_pre, v_g_ffn2_post]
    ws = big + [_pack_small(meta_tokens, conv_w[0], small, b_forget)]
    ms = [w_in_rows(m_w_in), m_w_attn_branch[0], m_w_conv_branch[0], m_w_out[0], m_w_ffn1_in[0], m_w_ffn1_out[0],
          m_w_ffn2_in[0], m_w_ffn2_out[0], _pack_small(m_meta_tokens, m_conv_w[0], small_m, m_b_forget)]
    vs = [w_in_rows(v_w_in), v_w_attn_branch[0], v_w_conv_branch[0], v_w_out[0], v_w_ffn1_in[0], v_w_ffn1_out[0],
          v_w_ffn2_in[0], v_w_ffn2_out[0], _pack_small(v_meta_tokens, v_conv_w[0], small_v, v_b_forget)]
    updates = [_adamw(tag, w, a, b, m, v, c_arr) for tag, w, a, b, m, v in zip(tags, ws, halves, others, ms, vs)]

    def leaves(big_vals, small_block):
        meta, conv, gains, bf = _unpack_small(small_block)
        w_in_t_, w_ab_, w_cb_, w_out_, f1_in, f1_out, f2_in, f2_out = [b[None] for b in big_vals]
        w_in_ = jnp.transpose(w_in_t_[:, :cs], (0, 2, 1))
        return [meta, w_in_, bf, conv, w_ab_, w_cb_, w_out_, gains[0], gains[1], f1_in, f1_out,
                gains[2], gains[3], gains[4], gains[5], f2_in, f2_out]

    out_g, out_d, out_m, out_v = [leaves([u_[k] for u_ in updates[:8]], updates[8][k]) for k in range(4)]
    loss = updates[8][0][LOSS_ROW, 0]
    return (loss, grad_x, *out_g, *out_d, *out_m, *out_v)
```

```python
import functools

import jax
import jax.numpy as jnp
from jax import lax
from jax.experimental import pallas as pl
from jax.experimental.pallas import tpu as pltpu
from jax.experimental.pallas import tpu_sc as plsc

N_META = 16
ROW_PAD = 112
N_FRONT = ROW_PAD + N_META
HEADS = 8
HEAD_DIM = 64
ATTN_W = HEADS * HEAD_DIM
CONV_W = 512
NORM_EPS = 1e-6
ROW_TILE = 640
F_PAD = 128
ATTN_ROW_PARTS = 1
NEG = -1e30
ADAM_LR = 0.001
ADAM_B1 = 0.9
ADAM_B2 = 0.999
ADAM_EPS = 1e-08
ADAM_WD = 0.01
ADAM_STEP = 10
VMEM_BIG = 56 * 1024 * 1024
MESH = pl.DeviceIdType.MESH
ANY = pl.BlockSpec(memory_space=pl.ANY)
F32 = jnp.float32
BF16 = jnp.bfloat16


def _params(sem, vmem=None):
    return pltpu.CompilerParams(dimension_semantics=sem, vmem_limit_bytes=vmem)


def _sigmoid(x):
    return 1.0 / (1.0 + jnp.exp(-x))


def _rstd(x):
    return lax.rsqrt(jnp.mean(x * x, axis=-1, keepdims=True) + NORM_EPS)


def _rms_bwd(x, g, dy):
    r = _rstd(x)
    xr = x * r
    gdy = g * dy
    dx = r * (gdy - xr * jnp.mean(xr * gdy, axis=-1, keepdims=True))
    return dx, jnp.sum(dy * xr, axis=0, keepdims=True)


def _dot(a, b):
    return jnp.dot(a, b, preferred_element_type=F32)


def _dot_nt(a, b):
    return lax.dot_general(a, b, (((1,), (1,)), ((), ())), preferred_element_type=F32)


def _k_tile(t):
    return 1664 if t % 1664 == 0 else ROW_TILE


def _place():
    x, y, c = lax.axis_index("x"), lax.axis_index("y"), lax.axis_index("c")
    chips = [(1 - x, y), (x, 1 - y), (1 - x, 1 - y)]
    return x, y, c, chips


def _all_gather(shards):
    n = len(shards)
    split = [s.reshape(2, s.shape[0] // 2, s.shape[1]) for s in shards]

    def body(*refs):
        ins, outs = refs[:n], refs[n:2 * n]
        send_sems, recv_sems = refs[2 * n:]
        x, y, c, chips = _place()
        me = 2 * x + y
        sibling = (x, y, 1 - c)

        def remote(i, k, slot, part, to, src=None):
            dst = outs[i].at[slot, part]
            return pltpu.make_async_remote_copy(
                src_ref=dst if src is None else src, dst_ref=dst,
                send_sem=send_sems.at[i, k], recv_sem=recv_sems.at[i, k],
                device_id=to, device_id_type=MESH)

        started = []
        for i in range(n):
            for k, (cx, cy) in enumerate(chips):
                cp = remote(i, k, me, c, (cx, cy, c), src=ins[i].at[c])
                cp.start()
                started.append(cp)
        for i in range(n):
            for k, (cx, cy) in enumerate(chips):
                remote(i, k, 2 * cx + cy, c, (x, y, c)).wait_recv()
                cp = remote(i, 3 + k, 2 * cx + cy, c, sibling)
                cp.start()
                started.append(cp)
        for i in range(n):
            for k, (cx, cy) in enumerate(chips):
                remote(i, 3 + k, 2 * cx + cy, 1 - c, (x, y, c)).wait_recv()
        for cp in started:
            cp.wait_send()

    outs = pl.pallas_call(
        body, name="all_gather_weights",
        out_shape=[jax.ShapeDtypeStruct((4,) + s.shape, s.dtype) for s in split],
        in_specs=[ANY] * n, out_specs=[ANY] * n,
        scratch_shapes=[pltpu.SemaphoreType.DMA((n, 6)), pltpu.SemaphoreType.DMA((n, 6))],
    )(*split)
    me =2 * lax.axis_index("x") + lax.axis_index("y")
    outs = [lax.dynamic_update_slice(o, s[None], (me, 0, 0, 0)) for o, s in zip(outs, split)]
    return [o.reshape((4,) + s.shape) for o, s in zip(outs, shards)]


def _all_gather_async(name, shards, collective_id):
    n = len(shards)
    split = [s.reshape(2, s.shape[0] // 2, s.shape[1]) for s in shards]
    ins = [jax.new_ref(s, memory_space=pltpu.MemorySpace.HBM) for s in split]
    outs = [jax.empty_ref(jax.ShapeDtypeStruct((4,) + s.shape, s.dtype), memory_space=pltpu.MemorySpace.HBM)
            for s in split]

    @pl.kernel(mesh=plsc.ScalarSubcoreMesh(axis_name="sequencer", num_cores=1), name=name,
               scratch_types=(pltpu.SemaphoreType.DMA((n, 6)), pltpu.SemaphoreType.DMA((n, 6))),
               compiler_params=pltpu.CompilerParams(collective_id=collective_id))
    def launch(send_sems, recv_sems):
        x, y, c, chips = _place()
        me = 2 * x + y
        sibling = (x, y, 1 - c)
        barrier = pltpu.get_barrier_semaphore()
        for peer in [(cx, cy, c) for cx, cy in chips] + [sibling]:
            pl.semaphore_signal(barrier, inc=1, device_id=peer, device_id_type=MESH)
        pl.semaphore_wait(barrier, 4)

        def remote(i, k, slot, part, to, src=None):
            dst = outs[i].at[slot, part]
            return pltpu.make_async_remote_copy(
                src_ref=dst if src is None else src, dst_ref=dst,
                send_sem=send_sems.at[i, k], recv_sem=recv_sems.at[i, k],
                device_id=to, device_id_type=MESH)

        started = []
        for i in range(n):
            for k, (cx, cy) in enumerate(chips):
                cp = remote(i, k, me, c, (cx, cy, c), src=ins[i].at[c])
                cp.start()
                started.append(cp)
        for i in range(n):
            for k, (cx, cy) in enumerate(chips):
                remote(i, k, 2 * cx + cy, c, (x, y, c)).wait_recv()
                cp = remote(i, 3 + k, 2 * cx + cy, c, sibling)
                cp.start()
                started.append(cp)
        for i in range(n):
            for k, (cx, cy) in enumerate(chips):
                remote(i, 3 + k, 2 * cx + cy, 1 - c, (x, y, c)).wait_recv()
        for cp in started:
            cp.wait_send()

    launch()
    raw = [o[...] for o in outs]

    def finish(after, which):
        arrived, _ = lax.optimization_barrier(([raw[i] for i in which], after))
        me = 2 * lax.axis_index("x") + lax.axis_index("y")
        gathered = [lax.dynamic_update_slice(a, split[i][None], (me, 0, 0, 0)) for a, i in zip(arrived, which)]
        return [g.reshape((4,) + shards[i].shape) for g, i in zip(gathered, which)]

    return finish


def _pair_send_halves(name, grads):
    n = len(grads)

    def body(*refs):
        ins, outs = refs[:n], refs[n:2 * n]
        send_sems, recv_sems = refs[2 * n:]
        x, y, c, _ = _place()
        cps = []
        for i in range(n):
            half = ins[i].shape[1] // 2
            cp = pltpu.make_async_remote_copy(
                src_ref=ins[i].at[:, pl.ds((1 - c) * half, half)], dst_ref=outs[i],
                send_sem=send_sems.at[i], recv_sem=recv_sems.at[i],
                device_id=(x, y, 1 - c), device_id_type=MESH)
            cp.start()
            cps.append(cp)
        for cp in cps:
            cp.wait()

    return pl.pallas_call(
        body, name=name,
        out_shape=[jax.ShapeDtypeStruct((4, g.shape[1] // 2, g.shape[2]), g.dtype) for g in grads],
        in_specs=[ANY] * n, out_specs=[ANY] * n,
        scratch_shapes=[pltpu.SemaphoreType.DMA((n,)), pltpu.SemaphoreType.DMA((n,))],
    )(*grads)


def _pair_send_halves_async(name, grads, collective_id):
    n = len(grads)
    ins = [jax.new_ref(g, memory_space=pltpu.MemorySpace.HBM) for g in grads]
    outs = [jax.empty_ref(jax.ShapeDtypeStruct((4, g.shape[1] // 2, g.shape[2]), g.dtype),
                          memory_space=pltpu.MemorySpace.HBM) for g in grads]

    @pl.kernel(mesh=plsc.ScalarSubcoreMesh(axis_name="sequencer", num_cores=1), name=name,
               scratch_types=(pltpu.SemaphoreType.DMA((n,)), pltpu.SemaphoreType.DMA((n,))),
               compiler_params=pltpu.CompilerParams(collective_id=collective_id))
    def launch(send_sems, recv_sems):
        x, y, c, _ = _place()
        barrier = pltpu.get_barrier_semaphore()
        pl.semaphore_signal(barrier, inc=1, device_id=(x, y, 1 - c), device_id_type=MESH)
        pl.semaphore_wait(barrier, 1)
        cps = []
        for i in range(n):
            half = ins[i].shape[1] // 2
            cp = pltpu.make_async_remote_copy(
                src_ref=ins[i].at[:, pl.ds((1 - c) * half, half)], dst_ref=outs[i],
                send_sem=send_sems.at[i], recv_sem=recv_sems.at[i],
                device_id=(x, y, 1 - c), device_id_type=MESH)
            cp.start()
            cps.append(cp)
        for cp in cps:
            cp.wait()

    launch()
    return [o[...] for o in outs]


def _chip_scatter(name, parts):
    n = len(parts)

    def body(*refs):
        _scatter_copies(refs[:n], refs[n:2 * n], *refs[2 * n:])

    arrived = pl.pallas_call(
        body, name=name,
        out_shape=[jax.ShapeDtypeStruct(p.shape, p.dtype) for p in parts],
        in_specs=[ANY] * n, out_specs=[ANY] * n,
        scratch_shapes=[pltpu.SemaphoreType.DMA((n, 3)), pltpu.SemaphoreType.DMA((n, 3))],
    )(*parts)
    return _own_slots(parts, arrived)


def _scatter_copies(ins, outs, send_sems, recv_sems):
    x, y, c, chips = _place()
    me = 2 * x + y
    sends = []
    for i in range(len(ins)):
        for k, (cx, cy) in enumerate(chips):
            cp = pltpu.make_async_remote_copy(
                src_ref=ins[i].at[2 * cx + cy], dst_ref=outs[i].at[me],
                send_sem=send_sems.at[i, k], recv_sem=recv_sems.at[i, k],
                device_id=(cx, cy, c), device_id_type=MESH)
            cp.start()
            sends.append(cp)
    for i in range(len(ins)):
        for k, (cx, cy) in enumerate(chips):
            got = outs[i].at[2 * cx + cy]
            pltpu.make_async_remote_copy(
                src_ref=got, dst_ref=got, send_sem=send_sems.at[i, k], recv_sem=recv_sems.at[i, k],
                device_id=(x, y, c), device_id_type=MESH).wait_recv()
    for cp in sends:
        cp.wait_send()


def _own_slots(parts, arrived):
    me = 2 * lax.axis_index("x") + lax.axis_index("y")
    return [lax.dynamic_update_slice(a, lax.dynamic_slice_in_dim(p, me, 1, axis=0), (me, 0, 0))
            for p, a in zip(parts, arrived)]


def _chip_scatter_async(name, parts, collective_id):
    n = len(parts)
    ins = [jax.new_ref(p, memory_space=pltpu.MemorySpace.HBM) for p in parts]
    outs = [jax.empty_ref(jax.ShapeDtypeStruct(p.shape, p.dtype), memory_space=pltpu.MemorySpace.HBM) for p in parts]

    @pl.kernel(mesh=plsc.ScalarSubcoreMesh(axis_name="sequencer", num_cores=1), name=name,
               scratch_types=(pltpu.SemaphoreType.DMA((n, 3)), pltpu.SemaphoreType.DMA((n, 3))),
               compiler_params=pltpu.CompilerParams(collective_id=collective_id))
    def launch(send_sems, recv_sems):
        x, y, c, chips = _place()
        barrier = pltpu.get_barrier_semaphore()
        for cx, cy in chips:
            pl.semaphore_signal(barrier, inc=1, device_id=(cx, cy, c), device_id_type=MESH)
        pl.semaphore_wait(barrier, 3)
        _scatter_copies(ins, outs, send_sems, recv_sems)

    launch()
    return _own_slots(parts, [o[...] for o in outs])


def _pair_swap(name, halves):
    n = len(halves)

    def body(*refs):
        ins, outs = refs[:n], refs[n:2 * n]
        send_sems, recv_sems = refs[2 * n:]
        x, y, c, _ = _place()
        cps = []
        for i in range(n):
            cp = pltpu.make_async_remote_copy(
                src_ref=ins[i], dst_ref=outs[i], send_sem=send_sems.at[i], recv_sem=recv_sems.at[i],
                device_id=(x, y, 1 - c), device_id_type=MESH)
            cp.start()
            cps.append(cp)
        for cp in cps:
            cp.wait()

    return pl.pallas_call(
        body, name=name,
        out_shape=[jax.ShapeDtypeStruct(h.shape, h.dtype) for h in halves],
        in_specs=[ANY] * n, out_specs=[ANY] * n,
        scratch_shapes=[pltpu.SemaphoreType.DMA((n,)), pltpu.SemaphoreType.DMA((n,))],
    )(*halves)


def _row_block(rows, cols, n_bufs, budget=20 * 1024 * 1024):
    best = min(rows, 16)
    for b in range(16, rows + 1, 16):
        if rows % b == 0 and 2 * n_bufs * b * cols * 4 <= budget:
            best = b
    return best


def _pair_add(tag, grad, got, c_arr, out_dtype):
    _, rows, cols = grad.shape
    half = rows // 2
    bh = _row_block(half, cols, 3)
    nb = half // bh

    def body(c_ref, g_ref, a_ref, o_ref):
        o_ref[...] = (g_ref[...] + a_ref[...]).astype(out_dtype)

    return pl.pallas_call(
        body, name=f"pair_add_{tag}",
        out_shape=jax.ShapeDtypeStruct((4, half, cols), out_dtype),
        grid_spec=pltpu.PrefetchScalarGridSpec(
            num_scalar_prefetch=1, grid=(4, nb),
            in_specs=[pl.BlockSpec((None, bh, cols), lambda j, r, c: (j, c[0] * nb + r, 0)),
                      pl.BlockSpec((None, bh, cols), lambda j, r, c: (j, r, 0))],
            out_specs=pl.BlockSpec((None, bh, cols), lambda j, r, c: (j, r, 0))),
        compiler_params=_params(("parallel", "parallel")),
    )(c_arr, grad, got)


def _chip_add(tag, parts):
    _, half, cols = parts.shape
    bh = _row_block(half, cols, 5)

    def body(p_ref, o_ref):
        a, b, c, d = [p_ref[j].astype(F32) for j in range(4)]
        o_ref[...] = ((a + b) + c) + d

    return pl.pallas_call(
        body, name=f"chip_add_{tag}",
        out_shape=jax.ShapeDtypeStruct((half, cols), F32),
        grid=(half // bh,),
        in_specs=[pl.BlockSpec((4, bh, cols), lambda r: (0, r, 0))],
        out_specs=pl.BlockSpec((bh, cols), lambda r: (r, 0)),
        compiler_params=_params(("parallel",)),
    )(parts)


def _adamw(tag, w, mine, theirs, m, v, c_arr):
    rows, cols = w.shape
    half = rows // 2
    br = _row_block(half, cols, 9)
    nb = half // br

    def body(c_ref, w_ref, a_ref, b_ref, m_ref, v_ref, g_ref, d_ref, mo_ref, vo_ref):
        own = (pl.program_id(0) // nb) == c_ref[0]
        g = jnp.where(own, a_ref[...], b_ref[...])
        g_ref[...] = g
        m_new = ADAM_B1 * m_ref[...] + (1.0 - ADAM_B1) * g
        v_new = ADAM_B2 * v_ref[...] + (1.0 - ADAM_B2) * (g * g)
        m_hat = m_new / (1.0 - ADAM_B1 ** ADAM_STEP)
        v_hat = v_new / (1.0 - ADAM_B2 ** ADAM_STEP)
        d_ref[...] = -ADAM_LR * (m_hat / (jnp.sqrt(v_hat) + ADAM_EPS) + ADAM_WD * w_ref[...])
        mo_ref[...] = m_new
        vo_ref[...] = v_new

    spec = pl.BlockSpec((br, cols), lambda r, c: (r, 0))
    mine_spec = pl.BlockSpec((br, cols), lambda r, c: (jnp.clip(r - c[0] * nb, 0, nb - 1), 0))
    theirs_spec = pl.BlockSpec((br, cols), lambda r, c: (jnp.clip(r - (1 - c[0]) * nb, 0, nb - 1), 0))
    return pl.pallas_call(
        body, name=f"adamw_{tag}",
        out_shape=[jax.ShapeDtypeStruct((rows, cols), F32)] * 4,
        grid_spec=pltpu.PrefetchScalarGridSpec(
            num_scalar_prefetch=1, grid=(rows // br,),
            in_specs=[spec, mine_spec, theirs_spec, spec, spec], out_specs=[spec] * 4),
        compiler_params=_params(("arbitrary",)),
    )(c_arr, w, mine, theirs, m, v)


def _matmul(name, x, w, out_shape, grid, x_spec, w_spec, o_spec, *, nt=False, vmem=None):
    nk = grid[2]
    acc_shape = tuple(d for d in o_spec.block_shape if d is not None)

    def body(x_ref, w_ref, o_ref, acc_ref):
        k = pl.program_id(2)
        part = _dot_nt(x_ref[...], w_ref[...]) if nt else _dot(x_ref[...], w_ref[...])
        if nk == 1:
            o_ref[...] = part.astype(o_ref.dtype)
        else:
            @pl.when(k == 0)
            def _():
                acc_ref[...] = part

            @pl.when(k > 0)
            def _():
                acc_ref[...] += part

            @pl.when(k == nk - 1)
            def _():
                o_ref[...] = acc_ref[...].astype(o_ref.dtype)

    return pl.pallas_call(
        body, name=name, out_shape=out_shape, grid=grid,
        in_specs=[x_spec, w_spec], out_specs=o_spec,
        scratch_shapes=[pltpu.VMEM(acc_shape if nk > 1 else (8, 128), F32)],
        compiler_params=_params(("parallel", "parallel", "arbitrary"), vmem),
    )(x, w)


def _weight_grad(name, xt, dy, bn, out_rows=None):
    m, t = xt.shape
    n = dy.shape[1]
    bm = m if out_rows is None else out_rows
    bk = _k_tile(t)
    return _matmul(
        name, xt, dy, jax.ShapeDtypeStruct((m, n), F32), (m // bm, n // bn, t // bk),
        pl.BlockSpec((bm, bk), lambda a, b, k: (a, k)),
        pl.BlockSpec((bk, bn), lambda a, b, k: (k, b)),
        pl.BlockSpec((bm, bn), lambda a, b, k: (a, b)), vmem=VMEM_BIG)


def _weight_grad_t(name, xt, dy):
    m, t = xt.shape
    n = dy.shape[1]
    bn = min(n, 512)
    bk = _k_tile(t)
    nk = t // bk

    def body(x_ref, dy_ref, o_ref, acc_ref):
        k = pl.program_id(1)
        part = _dot(x_ref[...], dy_ref[...].astype(BF16))

        @pl.when(k == 0)
        def _():
            acc_ref[...] = part

        @pl.when(k > 0)
        def _():
            acc_ref[...] += part

        @pl.when(k == nk - 1)
        def _():
            o_ref[...] = acc_ref[...].T

    return pl.pallas_call(
        body, name=name, out_shape=jax.ShapeDtypeStruct((n, m), F32), grid=(n // bn, nk),
        in_specs=[pl.BlockSpec((m, bk), lambda b, k: (0, k)), pl.BlockSpec((bk, bn), lambda b, k: (k, b))],
        out_specs=pl.BlockSpec((bn, m), lambda b, k: (b, 0)),
        scratch_shapes=[pltpu.VMEM((m, bn), F32)],
        compiler_params=_params(("parallel", "arbitrary"), VMEM_BIG),
    )(xt, dy)


def _mix_in_bwd(pieces, wt, h, g, dh_in):
    t, d = h.shape
    tm = ROW_TILE // 2
    widths = [p.shape[1] for p in pieces]
    n = len(pieces)

    def body(*refs):
        dy_refs, (w_ref, h_ref, g_ref, dhi_ref, dh_ref, dg_ref) = refs[:n], refs[n:]

        @pl.when(pl.program_id(0) == 0)
        def _():
            dg_ref[...] = jnp.zeros_like(dg_ref)

        dn, off = None, 0
        for dy_ref, wd in zip(dy_refs, widths):
            part = _dot(dy_ref[...].astype(BF16), w_ref[off:off + wd, :])
            dn = part if dn is None else dn + part
            off += wd
        dx, dg = _rms_bwd(h_ref[...], g_ref[...], dn)
        dh_ref[...] = dhi_ref[...] + dx
        dg_ref[...] += dg

    row = pl.BlockSpec((tm, d), lambda i: (i, 0))
    vec = pl.BlockSpec((1, d), lambda i: (0, 0))
    return pl.pallas_call(
        body, name="mix_in_bwd",
        out_shape=[jax.ShapeDtypeStruct((t, d), F32), jax.ShapeDtypeStruct((1, d), F32)],
        grid=(t // tm,),
        in_specs=[pl.BlockSpec((tm, wd), lambda i: (i, 0)) for wd in widths]
        + [pl.BlockSpec(wt.shape, lambda i: (0, 0)), row, vec, row],
        out_specs=[row, vec],
        compiler_params=_params(("arbitrary",), VMEM_BIG),
    )(*pieces, wt, h, g, dh_in)


def _norm_fwd(name, h, g):
    t, d = h.shape
    tm = ROW_TILE

    def body(h_ref, g_ref, n_ref, nt_ref):
        x = h_ref[...]
        y = x * _rstd(x) * g_ref[...]
        n_ref[...] = y.astype(BF16)
        nt_ref[...] = y.T.astype(BF16)

    return pl.pallas_call(
        body, name=name,
        out_shape=[jax.ShapeDtypeStruct((t, d), BF16), jax.ShapeDtypeStruct((d, t), BF16)],
        grid=(t // tm,),
        in_specs=[pl.BlockSpec((tm, d), lambda i: (i, 0)), pl.BlockSpec((1, d), lambda i: (0, 0))],
        out_specs=[pl.BlockSpec((tm, d), lambda i: (i, 0)), pl.BlockSpec((d, tm), lambda i: (0, i))],
        compiler_params=_params(("parallel",)),
    )(h, g)


def _slot_of(kk):
    return (kk % 2) * 2 + kk // 2


def _ffn_in(name, n, w4):
    t, d = n.shape
    cw = w4.shape[2]
    tm = ROW_TILE

    def body(x_ref, wg_ref, wu_ref, ab_ref, s_ref, st_ref):
        x = x_ref[...]
        a = _dot(x, wg_ref[...])
        b = _dot(x, wu_ref[...])
        ab_ref[:, :cw] = a.astype(BF16)
        ab_ref[:, cw:] = b.astype(BF16)
        s = a * _sigmoid(a) * b
        s_ref[...] = s.astype(BF16)
        st_ref[...] = s.T.astype(BF16)

    return pl.pallas_call(
        body, name=name,
        out_shape=[jax.ShapeDtypeStruct((t, 4 * cw), BF16), jax.ShapeDtypeStruct((t, 2 * cw), BF16),
                   jax.ShapeDtypeStruct((2 * cw, t), BF16)],
        grid=(2, t // tm),
        in_specs=[pl.BlockSpec((tm, d), lambda j, i: (i, 0)),
                  pl.BlockSpec((None, d, cw), lambda j, i: (j, 0, 0)),
                  pl.BlockSpec((None, d, cw), lambda j, i: (2 + j, 0, 0))],
        out_specs=[pl.BlockSpec((tm, 2 * cw), lambda j, i: (i, j)),
                   pl.BlockSpec((tm, cw), lambda j, i: (i, j)),
                   pl.BlockSpec((cw, tm), lambda j, i: (j, i))],
        compiler_params=_params(("parallel", "parallel"), VMEM_BIG),
    )(n, w4, w4)


def _mm_resid_norm(name, x, w, h, g_post, alpha, g_next):
    t, kdim = x.shape
    d = w.shape[1]
    tm = ROW_TILE
    with_next = g_next is not None

    def body(x_ref, w_ref, h_ref, gp_ref, gn_ref, f_ref, hn_ref, *rest):
        f = _dot(x_ref[...], w_ref[...])
        f_ref[...] = f
        hn = h_ref[...] + alpha * (f * _rstd(f) * gp_ref[...])
        hn_ref[...] = hn
        if with_next:
            y = hn * _rstd(hn) * gn_ref[...]
            rest[0][...] = y.astype(BF16)
            rest[1][...] = y.T.astype(BF16)

    row = lambda i: (i, 0)
    vec = pl.BlockSpec((1, d), lambda i: (0, 0))
    out_shape = [jax.ShapeDtypeStruct((t, d), F32), jax.ShapeDtypeStruct((t, d), F32)]
    out_specs = [pl.BlockSpec((tm, d), row), pl.BlockSpec((tm, d), row)]
    if with_next:
        out_shape += [jax.ShapeDtypeStruct((t, d), BF16), jax.ShapeDtypeStruct((d, t), BF16)]
        out_specs += [pl.BlockSpec((tm, d), row), pl.BlockSpec((d, tm), lambda i: (0, i))]
    return pl.pallas_call(
        body, name=name, out_shape=out_shape, grid=(t // tm,),
        in_specs=[pl.BlockSpec((tm, kdim), row), pl.BlockSpec((kdim, d), lambda i: (0, 0)),
                  pl.BlockSpec((tm, d), row), vec, vec],
        out_specs=out_specs,
        compiler_params=_params(("parallel",), VMEM_BIG),
    )(x, w, h, g_post, g_post if g_next is None else g_next)


def _in_proj(u, w):
    t, d = u.shape
    nz = w.shape[0]
    nq = 3 * ATTN_W
    tm = ROW_TILE // 2

    def body(u_ref, w_ref, qkv_ref, z_ref):
        qkv_ref[...] = _dot_nt(u_ref[...], w_ref[0:nq, :]).astype(BF16)
        z_ref[...] = _dot_nt(u_ref[...], w_ref[nq:, :])

    return pl.pallas_call(
        body, name="mix_in_proj",
        out_shape=[jax.ShapeDtypeStruct((t, nq), BF16), jax.ShapeDtypeStruct((t, nz - nq), F32)],
        grid=(t // tm,),
        in_specs=[pl.BlockSpec((tm, d), lambda i: (i, 0)), pl.BlockSpec((nz, d), lambda i: (0, 0))],
        out_specs=[pl.BlockSpec((tm, nq), lambda i: (i, 0)), pl.BlockSpec((tm, nz - nq), lambda i: (i, 0))],
        compiler_params=_params(("parallel",), VMEM_BIG),
    )(u, w)


def _gate_prep(z, b_pad, f_col):
    t = z.shape[0]
    tm = ROW_TILE

    def body(z_ref, b_ref, f_ref, carry_ref):
        i = pl.program_id(0)

        @pl.when(i == 0)
        def _():
            carry_ref[...] = jnp.zeros_like(carry_ref)

        xs = z_ref[...] + b_ref[...]
        logf = jnp.minimum(xs, 0.0) - jnp.log(1.0 + jnp.exp(-jnp.abs(xs)))
        row = i * tm + lax.broadcasted_iota(jnp.int32, (tm, 1), 0)
        logf = jnp.where(row >= ROW_PAD, logf, 0.0)
        tri = (lax.broadcasted_iota(jnp.int32, (tm, tm), 0) >= lax.broadcasted_iota(jnp.int32, (tm, tm), 1))
        f = jnp.dot(tri.astype(F32), logf, preferred_element_type=F32, precision=lax.Precision.HIGHEST)
        f = f + carry_ref[0:1, :]
        f_ref[...] = f
        carry_ref[...] = jnp.broadcast_to(f[tm - 1:tm, :], carry_ref.shape)

    return pl.pallas_call(
        body, name="forget_gate_cumsum", out_shape=jax.ShapeDtypeStruct((t, 128), F32),
        grid=(t // tm,),
        in_specs=[pl.BlockSpec((tm, 128), lambda i: (i, f_col // 128)), pl.BlockSpec((1, 128), lambda i: (0, 0))],
        out_specs=pl.BlockSpec((tm, 128), lambda i: (i, 0)),
        scratch_shapes=[pltpu.VMEM((8, 128), F32)],
        compiler_params=_params(("arbitrary",)),
    )(z, b_pad)


def _lane_halves():
    lane = lax.broadcasted_iota(jnp.int32, (1, 128), 1)
    return lane < HEAD_DIM


def _causal_mask(tq, tk, row0=0):
    row = row0 + lax.broadcasted_iota(jnp.int32, (tq, 1), 0)
    col = lax.broadcasted_iota(jnp.int32, (1, tk), 1)
    return col <= row


def _lane_one(lane):
    return (lax.broadcasted_iota(jnp.int32, (1, 128), 1) == lane).astype(BF16)


def _split3(x):
    hi = x.astype(BF16)
    rest = x - hi.astype(F32)
    mid = rest.astype(BF16)
    return hi, mid, (rest - mid.astype(F32)).astype(BF16)


def _split3_glue(x):
    hi = lax.reduce_precision(x, 8, 7)
    mid = lax.reduce_precision(x - hi, 8, 7)
    lo = lax.reduce_precision((x - hi) - mid, 8, 7)
    return hi.astype(BF16), mid.astype(BF16), lo.astype(BF16)


def _aug_pairs(cols):
    t = cols[0].shape[0]
    a = jnp.pad(jnp.stack(cols, axis=2), ((0, 0), (0, 0), (0, HEAD_DIM - len(cols))))
    a = a.reshape(t, 4, 2, HEAD_DIM)[:, :, ::-1, :]
    return jnp.transpose(a.reshape(t, 4, 128), (1, 0, 2))


def _attn_bias_operands(f_heads, lse_heads=None):
    t = f_heads.shape[0]
    one = jnp.ones((t, HEADS), BF16)
    row = lax.broadcasted_iota(jnp.int32, (t, 1), 0)
    fq = _split3_glue(f_heads)
    fk = _split3_glue(jnp.where(row < ROW_PAD, 1e9, f_heads))
    q_cols = list(fq) + [one] * 3
    k_cols = [one] * 3 + [-c for c in fk]
    if lse_heads is not None:
        q_cols += [-c for c in _split3_glue(lse_heads)]
        k_cols += [one] * 3
    return _aug_pairs(q_cols), _aug_pairs(k_cols)


def _attn_steps(nq, by_key):
    if by_key:
        pairs = [(qi, ki) for ki in range(nq) for qi in range(ki, nq)]
    else:
        pairs = [(qi, ki) for qi in range(nq) for ki in range(qi + 1)]
    return (jnp.array([p[0] for p in pairs], jnp.int32), jnp.array([p[1] for p in pairs], jnp.int32))


def _attn_fwd(z, aug_q, aug_k):
    t = z.shape[0]
    tq = tk = ROW_TILE
    nq = t // tq
    steps = [(qi, ka) for qi in range(nq) for ka in range(0, qi + 1, 2)]
    q_tab = jnp.array([qi for qi, _ in steps], jnp.int32)
    k_tab = jnp.array([ka for _, ka in steps], jnp.int32)

    def body(qt_ref, kt_ref, q_ref, k0_ref, k1_ref, v0_ref, v1_ref, aq_ref, ak0_ref, ak1_ref,
             o_ref, lse_ref, m_ref, l_ref, acc_ref):
        step = pl.program_id(1)
        qi, ka = qt_ref[step], kt_ref[step]

        @pl.when(ka == 0)
        def _():
            m_ref[...] = jnp.full_like(m_ref, NEG)
            l_ref[...] = jnp.zeros_like(l_ref)
            acc_ref[...] = jnp.zeros_like(acc_ref)

        def sweep(diagonal):
            first = _lane_halves()
            halves = (first, jnp.logical_not(first))
            q = (q_ref[...] * (HEAD_DIM ** -0.5)).astype(BF16)
            aq = aq_ref[...]
            qa = [jnp.where(lanes, q, aq) for lanes in halves]
            blocks = list(zip((k0_ref, k1_ref), (v0_ref, v1_ref), (ak0_ref, ak1_ref), diagonal))
            s = []
            for k_ref, _, ak_ref, diag in blocks:
                k, ak = k_ref[...].astype(BF16), ak_ref[...]
                for hh, lanes in enumerate(halves):
                    s_c = _dot_nt(qa[hh], jnp.where(lanes, k, ak))
                    s.append(jnp.where(_causal_mask(tq, tk), s_c, NEG) if diag else s_c)
            nb = len(blocks)
            m_prev = [m_ref[:, c0:c0 + 1] for c0 in (0, HEAD_DIM)]
            m_new = []
            for hh in range(2):
                m_h = m_prev[hh]
                for b in range(nb):
                    m_h = jnp.maximum(m_h, jnp.max(s[2 * b + hh], axis=1, keepdims=True))
                m_new.append(m_h)
            pv = [None, None]
            for b, (_, v_ref, _, _) in enumerate(blocks):
                v = v_ref[...].astype(BF16)
                for hh, (lanes, a0) in enumerate(zip(halves, (HEAD_DIM, 0))):
                    part = _dot(jnp.exp(s[2 * b + hh] - m_new[hh]).astype(BF16), jnp.where(lanes, v, _lane_one(a0)))
                    pv[hh] = part if pv[hh] is None else pv[hh] + part
            al0, al1 = [jnp.exp(mp - m_h) for mp, m_h in zip(m_prev, m_new)]
            l0 = al0 * l_ref[:, 0:1] + pv[0][:, HEAD_DIM:HEAD_DIM + 1]
            l1 = al1 * l_ref[:, HEAD_DIM:HEAD_DIM + 1] + pv[1][:, 0:1]
            acc_ref[...] = acc_ref[...] * jnp.where(first, al0, al1) + jnp.where(first, pv[0], pv[1])
            m_ref[...] = jnp.where(first, m_new[0], m_new[1])
            l_ref[...] = jnp.where(first, l0, l1)

        def finish():
            o_ref[...] = acc_ref[...] / l_ref[...]
            lse_ref[...] = m_ref[...] + jnp.log(l_ref[...])

        @pl.when(ka + 1 < qi)
        def _():
            sweep((False, False))

        @pl.when(ka + 1 == qi)
        def _():
            sweep((False, True))
            finish()

        @pl.when(ka == qi)
        def _():
            sweep((True,))
            finish()

    kb = lambda s, qt, kt: jnp.minimum(kt[s] + 1, qt[s])
    return pl.pallas_call(
        body, name="attention_fwd",
        out_shape=[jax.ShapeDtypeStruct((t, ATTN_W), F32), jax.ShapeDtypeStruct((t, ATTN_W), F32)],
        grid_spec=pltpu.PrefetchScalarGridSpec(
            num_scalar_prefetch=2, grid=(4, len(steps)),
            in_specs=[pl.BlockSpec((tq, 128), lambda p, s, qt, kt: (qt[s], p)),
                      pl.BlockSpec((tk, 128), lambda p, s, qt, kt: (kt[s], 4 + p)),
                      pl.BlockSpec((tk, 128), lambda p, s, qt, kt: (kb(s, qt, kt), 4 + p)),
                      pl.BlockSpec((tk, 128), lambda p, s, qt, kt: (kt[s], 8 + p)),
                      pl.BlockSpec((tk, 128), lambda p, s, qt, kt: (kb(s, qt, kt), 8 + p)),
                      pl.BlockSpec((None, tq, 128), lambda p, s, qt, kt: (p, qt[s], 0)),
                      pl.BlockSpec((None, tk, 128), lambda p, s, qt, kt: (p, kt[s], 0)),
                      pl.BlockSpec((None, tk, 128), lambda p, s, qt, kt: (p, kb(s, qt, kt), 0))],
            out_specs=[pl.BlockSpec((tq, 128), lambda p, s, qt, kt: (qt[s], p)),
                       pl.BlockSpec((tq, 128), lambda p, s, qt, kt: (qt[s], p))],
            scratch_shapes=[pltpu.VMEM((tq, 128), F32)] * 3),
        compiler_params=_params(("parallel", "arbitrary")),
    )(q_tab, k_tab, z, z, z, z, z, aug_q, aug_k, aug_k)


def _attn_bwd(z, aug_q, aug_k, o, do):
    t = z.shape[0]
    tq = tk = ROW_TILE
    nq = t // tq
    q_tab, k_tab = _attn_steps(nq, by_key=True)
    tn = (((0,), (0,)), ((), ()))

    def body(qt_ref, kt_ref, q_ref, k_ref, v_ref, aq_ref, ak_ref, o_ref, do_ref,
             dq_ref, dk_ref, dv_ref, dfk_ref, dfq_ref):
        step = pl.program_id(1)
        qi, ki = qt_ref[step], kt_ref[step]
        rows = pl.ds(pl.multiple_of(qi * tq, tq), tq)

        @pl.when(ki == 0)
        def _():
            dq_ref[rows, :] = jnp.zeros((tq, 128), F32)
            dfq_ref[rows, :] = jnp.zeros((tq, 128), F32)

        @pl.when(qi == ki)
        def _():
            dk_ref[...] = jnp.zeros_like(dk_ref)
            dv_ref[...] = jnp.zeros_like(dv_ref)
            dfk_ref[...] = jnp.zeros_like(dfk_ref)

        def sweep(diagonal):
            first = _lane_halves()
            lane = lax.broadcasted_iota(jnp.int32, (1, 128), 1)
            scale = HEAD_DIM ** -0.5
            q = (q_ref[...] * scale).astype(BF16)
            k = k_ref[...].astype(BF16)
            v = v_ref[...].astype(BF16)
            do_ = do_ref[...]
            do16 = do_.astype(BF16)
            od = o_ref[...] * do_
            aq, ak = aq_ref[...], ak_ref[...]
            halves = (first, jnp.logical_not(first))
            a0, a1 = HEAD_DIM, 0
            dos, vs = [], []
            for lanes, a in zip(halves, (a0, a1)):
                d_hi, d_mid, d_lo = _split3(jnp.sum(jnp.where(lanes, od, 0.0), axis=1, keepdims=True))
                minus_delta = jnp.where(lane == a, -d_hi, jnp.where(lane == a + 1, -d_mid,
                                        jnp.where(lane == a + 2, -d_lo, jnp.zeros((), BF16))))
                dos.append(jnp.where(lanes, do16, minus_delta))
                vs.append(jnp.where(lanes, v, ((lane >= a) & (lane < a + 3)).astype(BF16)))
            s = [_dot_nt(jnp.where(lanes, q, aq), jnp.where(lanes, k, ak)) for lanes in halves]
            dp = [_dot_nt(do_h, v_h) for do_h, v_h in zip(dos, vs)]
            p = [jnp.exp(s_h) for s_h in s]
            if diagonal:
                p = [jnp.where(_causal_mask(tq, tk), p_h, 0.0) for p_h in p]
            ds16 = [(p_h * dp_h).astype(BF16) for p_h, dp_h in zip(p, dp)]
            dv0, dv1 = [lax.dot_general(p_h.astype(BF16), jnp.where(lanes, do16, jnp.zeros((), BF16)), tn,
                                        preferred_element_type=F32) for p_h, lanes in zip(p, halves)]
            dk0, dk1 = [lax.dot_general(ds_h, jnp.where(lanes, q, _lane_one(a)), tn, preferred_element_type=F32)
                        for ds_h, lanes, a in zip(ds16, halves, (a0, a1))]
            dq0, dq1 = [_dot(ds_h, jnp.where(lanes, k, _lane_one(a))) for ds_h, lanes, a in zip(ds16, halves, (a0, a1))]
            dq_ref[rows, :] += jnp.where(first, dq0, dq1) * scale
            dfq_ref[rows, :] += jnp.where(first, dq0[:, a0:a0 + 1], dq1[:, a1:a1 + 1])
            dk_ref[...] += jnp.where(first, dk0, dk1)
            dfk_ref[...] += jnp.where(first, dk0[:, a0:a0 + 1], dk1[:, a1:a1 + 1])
            dv_ref[...] += dv0 + dv1

        @pl.when(qi > ki)
        def _():
            sweep(False)

        @pl.when(qi == ki)
        def _():
            sweep(True)

    qrow = lambda p, s, qt, kt: (qt[s], p)
    krow = lambda p, s, qt, kt: (kt[s], p)
    return pl.pallas_call(
        body, name="attention_bwd",
        out_shape=[jax.ShapeDtypeStruct((t, ATTN_W), F32)] * 5,
        grid_spec=pltpu.PrefetchScalarGridSpec(
            num_scalar_prefetch=2, grid=(4, int(q_tab.shape[0])),
            in_specs=[pl.BlockSpec((tq, 128), qrow),
                      pl.BlockSpec((tk, 128), lambda p, s, qt, kt: (kt[s], 4 + p)),
                      pl.BlockSpec((tk, 128), lambda p, s, qt, kt: (kt[s], 8 + p)),
                      pl.BlockSpec((None, tq, 128), lambda p, s, qt, kt: (p, qt[s], 0)),
                      pl.BlockSpec((None, tk, 128), lambda p, s, qt, kt: (p, kt[s], 0)),
                      pl.BlockSpec((tq, 128), qrow), pl.BlockSpec((tq, 128), qrow)],
            out_specs=[pl.BlockSpec((t, 128), lambda p, s, qt, kt: (0, p)),
                       pl.BlockSpec((tk, 128), krow), pl.BlockSpec((tk, 128), krow), pl.BlockSpec((tk, 128), krow),
                       pl.BlockSpec((t, 128), lambda p, s, qt, kt: (0, p))]),
        compiler_params=_params(("parallel", "arbitrary"), VMEM_BIG),
    )(q_tab, k_tab, z, z, z, aug_q, aug_k, o, do)


def _shifted(prev_rows, x, shift):
    tm = x.shape[0]
    return pltpu.roll(jnp.concatenate([prev_rows, x], axis=0), shift, 0)[8:8 + tm]


def _ahead(x, next_rows, shift):
    tm = x.shape[0]
    return pltpu.roll(jnp.concatenate([x, next_rows], axis=0), tm + 8 - shift, 0)[0:tm]


def _conv_col0(z):
    return (z.shape[1] - F_PAD - 3 * CONV_W) // CONV_W


def _conv_specs(tm, c0):
    cols = (c0, c0 + 1, c0 + 2)
    tiles = [pl.BlockSpec((tm, CONV_W), functools.partial(lambda i, c: (i, c), c=c)) for c in cols]
    halos = [pl.BlockSpec((8, CONV_W), functools.partial(lambda i, c: (jnp.maximum(i * (tm // 8) - 1, 0), c), c=c))
             for c in cols]
    return tiles, halos


def _conv_gate(z, conv_w):
    t = z.shape[0]
    tm = ROW_TILE
    nt = t // tm

    def body(cb_ref, cc_ref, ci_ref, hc_ref, hi_ref, w_ref, g_ref, gt_ref):
        i = pl.program_id(0)
        cc = cc_ref[...] * ci_ref[...]
        prev = jnp.where(i > 0, hc_ref[...] * hi_ref[...], 0.0)
        conv = w_ref[0:1, :] * _shifted(prev, cc, 2) + w_ref[1:2, :] * _shifted(prev, cc, 1) + w_ref[2:3, :] * cc
        g = cb_ref[...] * conv
        g_ref[...] = g.astype(BF16)
        gt_ref[...] = g.T.astype(BF16)

    (cb, cc, ci), (_, hc, hi) = _conv_specs(tm, _conv_col0(z))
    return pl.pallas_call(
        body, name="conv_gate_fwd",
        out_shape=[jax.ShapeDtypeStruct((t, CONV_W), BF16), jax.ShapeDtypeStruct((CONV_W, t), BF16)],
        grid=(nt,),
        in_specs=[cb, cc, ci, hc, hi, pl.BlockSpec((8, CONV_W), lambda i: (0, 0))],
        out_specs=[pl.BlockSpec((tm, CONV_W), lambda i: (i, 0)), pl.BlockSpec((CONV_W, tm), lambda i: (0, i))],
        compiler_params=_params(("parallel",)),
    )(z, z, z, z, z, conv_w)


def _conv_bwd(z, dg, conv_w):
    t = z.shape[0]
    tm = ROW_TILE
    nt = t // tm

    def body(cb_ref, cc_ref, ci_ref, hc_ref, hi_ref, dg_ref, ncb_ref, ndg_ref, w_ref, dz_ref, dw_ref):
        i = pl.program_id(0)

        @pl.when(i == 0)
        def _():
            dw_ref[...] = jnp.zeros_like(dw_ref)

        cb, c_c, c_in = cb_ref[...], cc_ref[...], ci_ref[...]
        cc = c_c * c_in
        prev = jnp.where(i > 0, hc_ref[...] * hi_ref[...], 0.0)
        cc1, cc2 = _shifted(prev, cc, 1), _shifted(prev, cc, 2)
        w0, w1, w2 = w_ref[0:1, :], w_ref[1:2, :], w_ref[2:3, :]
        conv = w0 * cc2 + w1 * cc1 + w2 * cc
        dgv = dg_ref[...]
        dconv = dgv * cb
        nxt = jnp.where(i < nt - 1, ndg_ref[...] * ncb_ref[...], 0.0)
        dcc = w2 * dconv + w1 * _ahead(dconv, nxt, 1) + w0 * _ahead(dconv, nxt, 2)
        dz_ref[:, 0:CONV_W] = (dgv * conv).astype(BF16)
        dz_ref[:, CONV_W:2 * CONV_W] = (dcc * c_in).astype(BF16)
        dz_ref[:, 2 * CONV_W:] = (dcc * c_c).astype(BF16)
        dw_ref[0:1, :] += jnp.sum(dconv * cc2, axis=0, keepdims=True)
        dw_ref[1:2, :] += jnp.sum(dconv * cc1, axis=0, keepdims=True)
        dw_ref[2:3, :] += jnp.sum(dconv * cc, axis=0, keepdims=True)

    c0 = _conv_col0(z)
    (cb, cc, ci), (_, hc, hi) = _conv_specs(tm, c0)
    nxt = lambda i, c: (jnp.minimum((i + 1) * (tm // 8), t // 8 - 1), c)
    return pl.pallas_call(
        body, name="conv_gate_bwd",
        out_shape=[jax.ShapeDtypeStruct((t, 3 * CONV_W), BF16), jax.ShapeDtypeStruct((8, CONV_W), F32)],
        grid=(nt,),
        in_specs=[cb, cc, ci, hc, hi, pl.BlockSpec((tm, CONV_W), lambda i: (i, 0)),
                  pl.BlockSpec((8, CONV_W), lambda i: nxt(i, c0)), pl.BlockSpec((8, CONV_W), lambda i: nxt(i, 0)),
                  pl.BlockSpec((8, CONV_W), lambda i: (0, 0))],
        out_specs=[pl.BlockSpec((tm, 3 * CONV_W), lambda i: (i, 0)), pl.BlockSpec((8, CONV_W), lambda i: (0, 0))],
        compiler_params=_params(("arbitrary",)),
    )(z, z, z, z, z, dg, z, dg, conv_w)


def _branch_mix(z, o, g, w_ab, w_cb, d):
    t = z.shape[0]
    tm = ROW_TILE
    ga_col = 0

    def body(o_ref, g_ref, ga_ref, gc_ref, wa_ref, wc_ref, mp_ref, mpt_ref, ot_ref):
        o_ = o_ref[...]
        ya = _dot(o_.astype(BF16), wa_ref[...])
        yc = _dot(g_ref[...], wc_ref[...])
        mp = _sigmoid(ga_ref[...]) * ya + _sigmoid(gc_ref[...]) * yc
        mp_ref[...] = mp.astype(BF16)
        mpt_ref[...] = mp.T.astype(BF16)
        ot_ref[...] = o_.T.astype(BF16)

    return pl.pallas_call(
        body, name="branch_mix_fwd",
        out_shape=[jax.ShapeDtypeStruct((t, d), BF16), jax.ShapeDtypeStruct((d, t), BF16),
                   jax.ShapeDtypeStruct((ATTN_W, t), BF16)],
        grid=(t // tm,),
        in_specs=[pl.BlockSpec((tm, ATTN_W), lambda i: (i, 0)), pl.BlockSpec((tm, CONV_W), lambda i: (i, 0)),
                  pl.BlockSpec((tm, d), lambda i: (i, ga_col)), pl.BlockSpec((tm, d), lambda i: (i, ga_col + 1)),
                  pl.BlockSpec((ATTN_W, d), lambda i: (0, 0)), pl.BlockSpec((CONV_W, d), lambda i: (0, 0))],
        out_specs=[pl.BlockSpec((tm, d), lambda i: (i, 0)), pl.BlockSpec((d, tm), lambda i: (0, i)),
                   pl.BlockSpec((ATTN_W, tm), lambda i: (0, i))],
        compiler_params=_params(("parallel",), VMEM_BIG),
    )(o, g, z, z, w_ab, w_cb)


def _branch_bwd(z, o, g, dmixed, w_out, w_ab, w_cb, d):
    t = z.shape[0]
    tm = ROW_TILE // 2
    ga_col = 0

    def body(dm_ref, o_ref, g_ref, ga_ref, gc_ref, wo_ref, wa_ref, wc_ref, dya_ref, dyc_ref, dgt_ref, do_ref, dg_ref):
        dmp = _dot_nt(dm_ref[...], wo_ref[...])
        ya = _dot(o_ref[...].astype(BF16), wa_ref[...])
        yc = _dot(g_ref[...], wc_ref[...])
        sa, sc = _sigmoid(ga_ref[...]), _sigmoid(gc_ref[...])
        dya = (dmp * sa).astype(BF16)
        dyc = (dmp * sc).astype(BF16)
        dya_ref[...] = dya
        dyc_ref[...] = dyc
        dgt_ref[:, :d] = (dmp * ya * sa * (1.0 - sa)).astype(BF16)
        dgt_ref[:, d:] = (dmp * yc * sc * (1.0 - sc)).astype(BF16)
        do_ref[...] = _dot_nt(dya, wa_ref[...])
        dg_ref[...] = _dot_nt(dyc, wc_ref[...])

    row = lambda i: (i, 0)
    fixed = lambda i: (0, 0)
    return pl.pallas_call(
        body, name="branch_mix_bwd",
        out_shape=[jax.ShapeDtypeStruct((t, d), BF16), jax.ShapeDtypeStruct((t, d), BF16),
                   jax.ShapeDtypeStruct((t, 2 * d), BF16), jax.ShapeDtypeStruct((t, ATTN_W), F32),
                   jax.ShapeDtypeStruct((t, CONV_W), F32)],
        grid=(t // tm,),
        in_specs=[pl.BlockSpec((tm, d), row), pl.BlockSpec((tm, ATTN_W), row), pl.BlockSpec((tm, CONV_W), row),
                  pl.BlockSpec((tm, d), lambda i: (i, ga_col)), pl.BlockSpec((tm, d), lambda i: (i, ga_col + 1)),
                  pl.BlockSpec((d, d), fixed), pl.BlockSpec((ATTN_W, d), fixed), pl.BlockSpec((CONV_W, d), fixed)],
        out_specs=[pl.BlockSpec((tm, d), row), pl.BlockSpec((tm, d), row), pl.BlockSpec((tm, 2 * d), row),
                   pl.BlockSpec((tm, ATTN_W), row), pl.BlockSpec((tm, CONV_W), row)],
        compiler_params=_params(("parallel",), VMEM_BIG),
    )(dmixed, o, g, z, z, w_out, w_ab, w_cb)


def _loss_norm_bwd(h, target, f, g_post, alpha):
    t, d = h.shape
    tm = N_FRONT

    def body(h_ref, t_ref, f_ref, g_ref, dh_ref, df_ref, dg_ref, loss_ref):
        i = pl.program_id(0)

        @pl.when(i == 0)
        def _():
            loss_ref[...] = jnp.zeros_like(loss_ref)
            dg_ref[...] = jnp.zeros_like(dg_ref)

        err = jnp.where(i > 0, h_ref[...] - t_ref[...], 0.0)
        dy = err * (1.0 / d)
        dh_ref[...] = dy
        per_row = jnp.sum(err * err, axis=1, keepdims=True) * (1.0 / d)
        loss_ref[...] += 0.5 * jnp.sum(per_row, axis=0, keepdims=True)
        dx, dg = _rms_bwd(f_ref[...], g_ref[...], dy)
        df_ref[...] = (alpha * dx).astype(BF16)
        dg_ref[...] += alpha * dg

    row = pl.BlockSpec((tm, d), lambda i: (i, 0))
    vec = pl.BlockSpec((1, d), lambda i: (0, 0))
    return pl.pallas_call(
        body, name="loss_and_post_norm_bwd",
        out_shape=[jax.ShapeDtypeStruct((t, d), F32), jax.ShapeDtypeStruct((t, d), BF16),
                   jax.ShapeDtypeStruct((1, d), F32), jax.ShapeDtypeStruct((1, 128), F32)],
        grid=(t // tm,),
        in_specs=[row, pl.BlockSpec((tm, d), lambda i: (jnp.maximum(i - 1, 0), 0)), row, vec],
        out_specs=[row, row, vec, pl.BlockSpec((1, 128), lambda i: (0, 0))],
        compiler_params=_params(("arbitrary",)),
    )(h, target, f, g_post)


def _norm_bwd(name, x, g, dy, alpha):
    t, d = x.shape
    tm = ROW_TILE

    def body(x_ref, g_ref, dy_ref, dx_ref, dg_ref):
        @pl.when(pl.program_id(0) == 0)
        def _():
            dg_ref[...] = jnp.zeros_like(dg_ref)

        dx, dg = _rms_bwd(x_ref[...], g_ref[...], dy_ref[...])
        dx_ref[...] = (alpha * dx).astype(BF16)
        dg_ref[...] += alpha * dg

    row = pl.BlockSpec((tm, d), lambda i: (i, 0))
    vec = pl.BlockSpec((1, d), lambda i: (0, 0))
    return pl.pallas_call(
        body, name=name,
        out_shape=[jax.ShapeDtypeStruct((t, d), BF16), jax.ShapeDtypeStruct((1, d), F32)],
        grid=(t // tm,), in_specs=[row, vec, row], out_specs=[row, vec],
        compiler_params=_params(("arbitrary",)),
    )(x, g, dy)


def _ffn_bwd_mid(name, df, w_out, ab):
    t, d = df.shape
    cw = ab.shape[1] // 4
    tm = ROW_TILE

    def body(df_ref, w_ref, ab_ref, o_ref):
        ds = _dot_nt(df_ref[...], w_ref[...])
        a = ab_ref[:, :cw].astype(F32)
        b = ab_ref[:, cw:].astype(F32)
        sg = _sigmoid(a)
        o_ref[:, :cw] = (ds * b * (sg * (1.0 + a * (1.0 - sg)))).astype(BF16)
        o_ref[:, cw:] = (ds * (a * sg)).astype(BF16)

    return pl.pallas_call(
        body, name=name, out_shape=jax.ShapeDtypeStruct((t, 4 * cw), BF16),
        grid=(2, t // tm),
        in_specs=[pl.BlockSpec((tm, d), lambda j, i: (i, 0)), pl.BlockSpec((cw, d), lambda j, i: (j, 0)),
                  pl.BlockSpec((tm, 2 * cw), lambda j, i: (i, j))],
        out_specs=pl.BlockSpec((tm, 2 * cw), lambda j, i: (i, j)),
        compiler_params=_params(("parallel", "parallel"), VMEM_BIG),
    )(df, w_out, ab)


def _mm_nt_norm_bwd(name, dy, w, h, g, dh_in):
    t, kdim = dy.shape
    d = h.shape[1]
    tm = ROW_TILE // 2
    slots = w.ndim == 3

    def body(dy_ref, w_ref, h_ref, g_ref, dhi_ref, dh_ref, dg_ref):
        @pl.when(pl.program_id(0) == 0)
        def _():
            dg_ref[...] = jnp.zeros_like(dg_ref)

        if slots:
            cw = w_ref.shape[2]
            dn = _dot_nt(dy_ref[:, 0:cw], w_ref[_slot_of(0)])
            for k in range(1, 4):
                dn += _dot_nt(dy_ref[:, k * cw:(k + 1) * cw], w_ref[_slot_of(k)])
        else:
            dn = _dot_nt(dy_ref[...], w_ref[...])
        dx, dg = _rms_bwd(h_ref[...], g_ref[...], dn)
        dh_ref[...] = dhi_ref[...] + dx
        dg_ref[...] += dg

    row = pl.BlockSpec((tm, d), lambda i: (i, 0))
    vec = pl.BlockSpec((1, d), lambda i: (0, 0))
    return pl.pallas_call(
        body, name=name,
        out_shape=[jax.ShapeDtypeStruct((t, d), F32), jax.ShapeDtypeStruct((1, d), F32)],
        grid=(t // tm,),
        in_specs=[pl.BlockSpec((tm, kdim), lambda i: (i, 0)), pl.BlockSpec(w.shape, lambda i: (0,) * w.ndim),
                  row, vec, row],
        out_specs=[row, vec],
        compiler_params=_params(("arbitrary",), VMEM_BIG),
    )(dy, w, h, g, dh_in)


def _gate_bwd(dfq, dfk, z, b_pad, f_col):
    t = z.shape[0]
    tm = ROW_TILE
    nt = t // tm

    def body(dq_ref, dk_ref, z_ref, b_ref, dz_ref, db_ref, carry_ref):
        i = pl.program_id(0)

        @pl.when(i == 0)
        def _():
            carry_ref[...] = jnp.zeros_like(carry_ref)
            db_ref[...] = jnp.zeros_like(db_ref)

        pick = (lax.broadcasted_iota(jnp.int32, (ATTN_W, 128), 0)
                == HEAD_DIM * lax.broadcasted_iota(jnp.int32, (ATTN_W, 128), 1)).astype(F32)
        d_heads = jnp.dot(dq_ref[...] - dk_ref[...], pick, preferred_element_type=F32,
                          precision=lax.Precision.HIGHEST)
        tri = (lax.broadcasted_iota(jnp.int32, (tm, tm), 0) <= lax.broadcasted_iota(jnp.int32, (tm, tm), 1))
        tail = jnp.dot(tri.astype(F32), d_heads, preferred_element_type=F32, precision=lax.Precision.HIGHEST)
        tail = tail + carry_ref[0:1, :]
        carry_ref[...] = jnp.broadcast_to(tail[0:1, :], carry_ref.shape)
        row = (nt - 1 - i) * tm + lax.broadcasted_iota(jnp.int32, (tm, 1), 0)
        dlogit = jnp.where(row >= ROW_PAD, tail * _sigmoid(-(z_ref[...] + b_ref[...])), 0.0)
        dz_ref[...] = jnp.zeros_like(dz_ref)
        dz_ref[:, 0:128] = dlogit.astype(BF16)
        db_ref[...] += jnp.sum(dlogit, axis=0, keepdims=True)

    rev = lambda i: (nt - 1 - i, 0)
    return pl.pallas_call(
        body, name="forget_gate_bwd",
        out_shape=[jax.ShapeDtypeStruct((t, F_PAD), BF16), jax.ShapeDtypeStruct((1, 128), F32)],
        grid=(nt,),
        in_specs=[pl.BlockSpec((tm, ATTN_W), rev), pl.BlockSpec((tm, ATTN_W), rev),
                  pl.BlockSpec((tm, 128), lambda i: (nt - 1 - i, f_col // 128)),
                  pl.BlockSpec((1, 128), lambda i: (0, 0))],
        out_specs=[pl.BlockSpec((tm, F_PAD), rev), pl.BlockSpec((1, 128), lambda i: (0, 0))],
        scratch_shapes=[pltpu.VMEM((8, 128), F32)],
        compiler_params=_params(("arbitrary",)),
    )(dfq, dfk, z, b_pad)


def _ffn_fwd(tag, n, w_in4, w_out, h, g_post, g_next):
    ab, s, s_t = _ffn_in(f"{tag}_in_fwd", n, w_in4)
    outs = _mm_resid_norm(f"{tag}_out_fwd", s, w_out, h, g_post, 0.5, g_next)
    return ab, s_t, outs


def _ffn_bwd_weights(tag, df, ab, s_t, n_t, w_in4, w_out):
    d, cw = w_in4.shape[1], w_in4.shape[2]
    t = df.shape[0]
    dw_out = _weight_grad(f"{tag}_dw_out", s_t, df, d, out_rows=cw // 2)
    dab = _ffn_bwd_mid(f"{tag}_mid_bwd", df, w_out, ab)
    bk = _k_tile(t)
    dw_in = _matmul(
        f"{tag}_dw_in", n_t, dab, jax.ShapeDtypeStruct((4, d, cw), F32), (1, 4, t // bk),
        pl.BlockSpec((d, bk), lambda a, b, k: (0, k)), pl.BlockSpec((bk, cw), lambda a, b, k: (k, b)),
        pl.BlockSpec((None, d, cw), lambda a, b, k: (_slot_of(b), 0, 0)), vmem=VMEM_BIG)
    return dab, dw_in, dw_out


LOSS_ROW = 12


def _pack_small(meta, conv, gains, b_forget, loss=None):
    d = gains[0].shape[1]
    rows = [meta.reshape(4, d), jnp.pad(conv.reshape(1, 3 * 128), ((0, 0), (0, d - 3 * 128)))]
    rows += list(gains) + [jnp.pad(b_forget, ((0, 0), (0, d - HEADS)))]
    last = jnp.zeros((4, d), F32)
    if loss is not None:
        last = jnp.pad(loss.reshape(1, 1), ((0, 3), (0, d - 1)))
    return jnp.concatenate(rows + [last], axis=0)


def _unpack_small(block):
    d = block.shape[1]
    meta = block[0:4].reshape(N_META, d // 4)
    conv = block[4, :3 * 128].reshape(1, 3, 128)
    gains = [block[5 + i:6 + i] for i in range(6)]
    return meta, conv, gains, block[11:12, :HEADS]


def kernel(x, meta_tokens, w_in, b_forget, conv_w, w_attn_branch, w_conv_branch, w_out, g_ffn1_pre, g_ffn1_post, w_ffn1_in, w_ffn1_out, g_mix_pre, g_mix_post, g_ffn2_pre, g_ffn2_post, w_ffn2_in, w_ffn2_out, loss_target, m_meta_tokens, m_w_in, m_b_forget, m_conv_w, m_w_attn_branch, m_w_conv_branch, m_w_out, m_g_ffn1_pre, m_g_ffn1_post, m_w_ffn1_in, m_w_ffn1_out, m_g_mix_pre, m_g_mix_post, m_g_ffn2_pre, m_g_ffn2_post, m_w_ffn2_in, m_w_ffn2_out, v_meta_tokens, v_w_in, v_b_forget, v_conv_w, v_w_attn_branch, v_w_conv_branch, v_w_out, v_g_ffn1_pre, v_g_ffn1_post, v_w_ffn1_in, v_w_ffn1_out, v_g_mix_pre, v_g_mix_post, v_g_ffn2_pre, v_g_ffn2_post, v_w_ffn2_in, v_w_ffn2_out):
    seq, d = x.shape[1], x.shape[2]
    t = seq + N_FRONT
    n_main = 3 * ATTN_W + 3 * CONV_W + 2 * d
    nz = n_main + F_PAD
    f_lo = 3 * ATTN_W
    c_arr = lax.axis_index("c").astype(jnp.int32).reshape(1)

    cs = w_in.shape[2]
    cs_pad = -(-cs // 64) * 64

    def w_in_rows(a):
        return jnp.pad(jnp.transpose(a[0]), ((0, cs_pad - cs), (0, 0)))

    big = [w_in_rows(w_in), w_attn_branch[0], w_conv_branch[0], w_out[0], w_ffn1_in[0], w_ffn1_out[0], w_ffn2_in[0],
           w_ffn2_out[0]]
    small_gather = jnp.concatenate(
        [meta_tokens.reshape(4, d), jnp.pad(conv_w.reshape(1, 3 * 128), ((0, 0), (0, d - 3 * 128))),
         jnp.zeros((11, d), F32)], axis=0)
    w_f1_in4, small4 = _all_gather([big[4].astype(BF16), small_gather])
    (second, rest), small4 = lax.optimization_barrier(
        (([big[5].astype(BF16)], [big[i].astype(BF16) for i in (0, 1, 2, 3, 6, 7)]), small4))
    second_gathered = _all_gather_async("all_gather_ffn1_out", second, 5)
    rest_gathered = _all_gather_async("all_gather_rest", rest, 1)
    meta_full = jnp.transpose(small4[:, 0:4].reshape(4, N_META, d // 4), (1, 0, 2)).reshape(N_META, d)
    conv_full = jnp.transpose(small4[:, 4, :3 * 128].reshape(4, 3, 128), (1, 0, 2)).reshape(3, CONV_W)
    conv_pad = jnp.pad(conv_full, ((0, 5), (0, 0)))
    b_pad = jnp.pad(b_forget, ((0, 0), (0, 128 - HEADS)))

    h0 = jnp.concatenate([jnp.zeros((ROW_PAD, d), F32), meta_full, x[0]], axis=0)
    n1, n1_t = _norm_fwd("ffn1_pre_norm", h0, g_ffn1_pre)
    ab1, s1, s1_t = _ffn_in("ffn1_in_fwd", n1, w_f1_in4)
    w_f1_out = second_gathered(s1, [0])[0].reshape(-1, d)
    f1, h1, u, u_t = _mm_resid_norm("ffn1_out_fwd", s1, w_f1_out, h0, g_ffn1_post, 0.5, g_mix_pre)

    w_in4, w_ab4, w_cb4, w_out4, w_f2_in4, w_f2_out4 = rest_gathered(u, range(6))
    w_in_t = w_in4[:, :cs].reshape(4 * cs, d)
    g_lo = f_lo + HEADS + 3 * CONV_W
    w_in_pad = jnp.concatenate(
        [w_in_t[:f_lo], w_in_t[g_lo:], w_in_t[f_lo + HEADS:g_lo], w_in_t[f_lo:f_lo + HEADS],
         jnp.zeros((F_PAD - HEADS, d), BF16)], axis=0)
    w_ab = jnp.transpose(w_ab4, (1, 0, 2)).reshape(ATTN_W, d)
    w_cb = jnp.transpose(w_cb4, (1, 0, 2)).reshape(CONV_W, d)
    w_out_full = w_out4.reshape(d, d)
    w_f2_out = w_f2_out4.reshape(-1, d)
    qkv, z = _in_proj(u, w_in_pad)
    f_col = z.shape[1] - F_PAD
    f_cum = _gate_prep(z, b_pad, f_col)
    f_heads = f_cum[:, :HEADS]
    o, lse = _attn_fwd(qkv, *_attn_bias_operands(f_heads))
    g, g_t = _conv_gate(z, conv_pad)
    mp, mp_t, o_t = _branch_mix(z, o, g, w_ab, w_cb, d)
    mixed, h2, n2, n2_t = _mm_resid_norm("mix_out_fwd", mp, w_out_full, h1, g_mix_post, 1.0, g_ffn2_pre)
    ab2, s2_t, (f2, h3) = _ffn_fwd("ffn2", n2, w_f2_in4, w_f2_out, h2, g_ffn2_post, None)
    dh3, df2, dg_f2_post, loss_part = _loss_norm_bwd(h3, loss_target[0], f2, g_ffn2_post, 0.5)

    reduced = {}

    def reduce_scatter(label, tags, slots, sequencer_id, hold=None, got=None, after=None):
        if got is None:
            got = _pair_send_halves(f"grad_pair_exchange_{label}", slots)
        else:
            got, _ = lax.optimization_barrier((got, after))
        sums = [_pair_add(tag, s, a, c_arr, F32 if tag == "small" else BF16) for tag, s, a in zip(tags, slots, got)]
        sums, hold = lax.optimization_barrier((sums, hold))
        if sequencer_id is None:
            arrived = _chip_scatter(f"grad_chip_scatter_{label}", sums)
        else:
            arrived = _chip_scatter_async(f"grad_chip_scatter_{label}", sums, sequencer_id)
        mine = [_chip_add(tag, a) for tag, a in zip(tags, arrived)]
        reduced.update(zip(tags, zip(mine, _pair_swap(f"grad_pair_swap_{label}", mine))))
        return hold

    dab2, dw_f2_in, dw_f2_out = _ffn_bwd_weights("ffn2", df2, ab2, s2_t, n2_t, w_f2_in4, w_f2_out)
    ffn2_slots = [dw_f2_in, dw_f2_out.reshape(4, -1, d)]
    ffn2_got = _pair_send_halves_async("grad_pair_exchange_ffn2", ffn2_slots, 6)
    dh2, dg_f2_pre = _mm_nt_norm_bwd("ffn2_in_bwd", dab2, w_f2_in4, h2, g_ffn2_pre, dh3)
    reduce_scatter("ffn2", ["w_ffn2_in", "w_ffn2_out"], ffn2_slots, 2, got=ffn2_got, after=dh2)
    dmixed, dg_mix_post = _norm_bwd("mix_post_norm_bwd", mixed, g_mix_post, dh2, 1.0)
    dw_out = _weight_grad("mix_dw_out", mp_t, dmixed, d)
    dya, dyc, dgates, do, dgconv = _branch_bwd(z, o, g, dmixed, w_out_full, w_ab, w_cb, d)
    dw_ab = _weight_grad("mix_dw_attn_branch", o_t, dya, d)
    dw_cb = _weight_grad("mix_dw_conv_branch", g_t, dyc, d)
    dz_conv, dconv_w = _conv_bwd(z, dgconv, conv_pad)
    front = lax.broadcasted_iota(jnp.int32, (t, 1), 0) < ROW_PAD
    lse_heads = jnp.where(front, 1e9, lse[:, ::HEAD_DIM])
    dq, dk, dv, dfk, dfq = _attn_bwd(qkv, *_attn_bias_operands(f_heads, lse_heads), o, do)
    dz_f, db_forget = _gate_bwd(dfq, dfk, z, b_pad, f_col)
    dz_pieces = {"q": dq, "k": dk, "v": dv, "gates": dgates, "conv": dz_conv, "f": dz_f}
    dh1, dg_mix_pre = _mix_in_bwd(list(dz_pieces.values()), w_in_pad, h1, g_mix_pre, dh2)
    dw_t = {name: _weight_grad_t(f"mix_dw_in_{name}", u_t, piece) for name, piece in dz_pieces.items()}
    dw_in_t = jnp.concatenate(
        [dw_t["q"], dw_t["k"], dw_t["v"], dw_t["f"][:HEADS], dw_t["conv"], dw_t["gates"]], axis=0)
    mix_slots = [jnp.pad(dw_in_t.reshape(4, cs, d), ((0, 0), (0, cs_pad - cs), (0, 0))),
                 jnp.transpose(dw_ab.reshape(ATTN_W, 4, d // 4), (1, 0, 2)),
                 jnp.transpose(dw_cb.reshape(CONV_W, 4, d // 4), (1, 0, 2)),
                 dw_out.reshape(4, d // 4, d)]
    mix_got = _pair_send_halves_async("grad_pair_exchange_mix", mix_slots, 7)
    df1, dg_f1_post = _norm_bwd("ffn1_post_norm_bwd", f1, g_ffn1_post, dh1, 0.5)
    reduce_scatter("mix", ["w_in", "w_attn_branch", "w_conv_branch", "w_out"], mix_slots, 3, got=mix_got, after=df1)
    dab1, dw_f1_in, dw_f1_out = _ffn_bwd_weights("ffn1", df1, ab1, s1_t, n1_t, w_f1_in4, w_f1_out)
    dab1 = reduce_scatter("ffn1", ["w_ffn1_in", "w_ffn1_out"], [dw_f1_in, dw_f1_out.reshape(4, -1, d)], 4, dab1)
    dh0, dg_f1_pre = _mm_nt_norm_bwd("ffn1_in_bwd", dab1, w_f1_in4, h0, g_ffn1_pre, dh1)
    grad_x = dh0[N_FRONT:][None]
    dmeta = dh0[ROW_PAD:N_FRONT]
    small_grad = jnp.stack([
        _pack_small(dmeta[:, j * (d // 4):(j + 1) * (d // 4)], dconv_w[:3, j * 128:(j + 1) * 128],
                    [dg_f1_pre, dg_f1_post, dg_mix_pre, dg_mix_post, dg_f2_pre, dg_f2_post], db_forget[:, :HEADS],
                    loss_part[0, 0])
        for j in range(4)])
    reduce_scatter("small", ["small"], [small_grad], None)
    tags =["w_in", "w_attn_branch", "w_conv_branch", "w_out", "w_ffn1_in", "w_ffn1_out", "w_ffn2_in", "w_ffn2_out", "small"]
    halves = [reduced[tag][0] for tag in tags]
    others = [reduced[tag][1] for tag in tags]

    small = [g_ffn1_pre, g_ffn1_post, g_mix_pre, g_mix_post, g_ffn2_pre, g_ffn2_post]
    small_m = [m_g_ffn1_pre, m_g_ffn1_post, m_g_mix_pre, m_g_mix_post, m_g_ffn2_pre, m_g_ffn2_post]
    small_v = [v_g_ffn1_pre, v_g_ffn1_post, v_g_mix_pre, v_g_mix_post, v_g_ffn2_pre, v_g_ffn2_post]
    ws = big + [_pack_small(meta_tokens, conv_w[0], small, b_forget)]
    ms = [w_in_rows(m_w_in), m_w_attn_branch[0], m_w_conv_branch[0], m_w_out[0], m_w_ffn1_in[0], m_w_ffn1_out[0],
          m_w_ffn2_in[0], m_w_ffn2_out[0], _pack_small(m_meta_tokens, m_conv_w[0], small_m, m_b_forget)]
    vs = [w_in_rows(v_w_in), v_w_attn_branch[0], v_w_conv_branch[0], v_w_out[0], v_w_ffn1_in[0], v_w_ffn1_out[0],
          v_w_ffn2_in[0], v_w_ffn2_out[0], _pack_small(v_meta_tokens, v_conv_w[0], small_v, v_b_forget)]
    updates = [_adamw(tag, w, a, b, m, v, c_arr) for tag, w, a, b, m, v in zip(tags, ws, halves, others, ms, vs)]

    def leaves(big_vals, small_block):
        meta, conv, gains, bf = _unpack_small(small_block)
        w_in_t_, w_ab_, w_cb_, w_out_, f1_in, f1_out, f2_in, f2_out = [b[None] for b in big_vals]
        w_in_ = jnp.transpose(w_in_t_[:, :cs], (0, 2, 1))
        return [meta, w_in_, bf, conv, w_ab_, w_cb_, w_out_, gains[0], gains[1], f1_in, f1_out,
                gains[2], gains[3], gains[4], gains[5], f2_in, f2_out]

    out_g, out_d, out_m, out_v = [leaves([u_[k] for u_ in updates[:8]], updates[8][k]) for k in range(4)]
    loss = updates[8][0][LOSS_ROW, 0]
    return (loss, grad_x, *out_g, *out_d, *out_m, *out_v)
```

```python
import functools

import jax
import jax.numpy as jnp
from jax import lax
from jax.experimental import pallas as pl
from jax.experimental.pallas import tpu as pltpu
from jax.experimental.pallas import tpu_sc as plsc

N_META = 16
ROW_PAD = 112
N_FRONT = ROW_PAD + N_META
HEADS = 8
HEAD_DIM = 64
ATTN_W = HEADS * HEAD_DIM
CONV_W = 512
NORM_EPS = 1e-6
ROW_TILE = 640
F_PAD = 128
ATTN_KV_GROUP = 4
NEG = -1e30
ADAM_LR = 0.001
ADAM_B1 = 0.9
ADAM_B2 = 0.999
ADAM_EPS = 1e-08
ADAM_WD = 0.01
ADAM_STEP = 10
VMEM_BIG = 56 * 1024 * 1024
MESH = pl.DeviceIdType.MESH
ANY = pl.BlockSpec(memory_space=pl.ANY)
F32 = jnp.float32
BF16 = jnp.bfloat16


def _params(sem, vmem=None):
    return pltpu.CompilerParams(dimension_semantics=sem, vmem_limit_bytes=vmem)


def _sigmoid(x):
    return 1.0 / (1.0 + jnp.exp(-x))


def _rstd(x):
    return lax.rsqrt(jnp.mean(x * x, axis=-1, keepdims=True) + NORM_EPS)


def _rms_bwd(x, g, dy):
    r = _rstd(x)
    xr = x * r
    gdy = g * dy
    dx = r * (gdy - xr * jnp.mean(xr * gdy, axis=-1, keepdims=True))
    return dx, jnp.sum(dy * xr, axis=0, keepdims=True)


def _dot(a, b):
    return jnp.dot(a, b, preferred_element_type=F32)


def _dot_nt(a, b):
    return lax.dot_general(a, b, (((1,), (1,)), ((), ())), preferred_element_type=F32)


def _k_tile(t):
    return 1664 if t % 1664 == 0 else ROW_TILE


def _place():
    x, y, c = lax.axis_index("x"), lax.axis_index("y"), lax.axis_index("c")
    chips = [(1 - x, y), (x, 1 - y), (1 - x, 1 - y)]
    return x, y, c, chips


def _all_gather(shards):
    n = len(shards)
    split = [s.reshape(2, s.shape[0] // 2, s.shape[1]) for s in shards]

    def body(*refs):
        ins, outs = refs[:n], refs[n:2 * n]
        send_sems, recv_sems = refs[2 * n:]
        x, y, c, chips = _place()
        me = 2 * x + y
        sibling = (x, y, 1 - c)

        def remote(i, k, slot, part, to, src=None):
            dst = outs[i].at[slot, part]
            return pltpu.make_async_remote_copy(
                src_ref=dst if src is None else src, dst_ref=dst,
                send_sem=send_sems.at[i, k], recv_sem=recv_sems.at[i, k],
                device_id=to, device_id_type=MESH)

        started = []
        for i in range(n):
            for k, (cx, cy) in enumerate(chips):
                cp = remote(i, k, me, c, (cx, cy, c), src=ins[i].at[c])
                cp.start()
                started.append(cp)
        for i in range(n):
            for k, (cx, cy) in enumerate(chips):
                remote(i, k, 2 * cx + cy, c, (x, y, c)).wait_recv()
                cp = remote(i, 3 + k, 2 * cx + cy, c, sibling)
                cp.start()
                started.append(cp)
        for i in range(n):
            for k, (cx, cy) in enumerate(chips):
                remote(i, 3 + k, 2 * cx + cy, 1 - c, (x, y, c)).wait_recv()
        for cp in started:
            cp.wait_send()

    outs = pl.pallas_call(
        body, name="all_gather_weights",
        out_shape=[jax.ShapeDtypeStruct((4,) + s.shape, s.dtype) for s in split],
        in_specs=[ANY] * n, out_specs=[ANY] * n,
        scratch_shapes=[pltpu.SemaphoreType.DMA((n, 6)), pltpu.SemaphoreType.DMA((n, 6))],
    )(*split)
    me =2 * lax.axis_index("x") + lax.axis_index("y")
    outs = [lax.dynamic_update_slice(o, s[None], (me, 0, 0, 0)) for o, s in zip(outs, split)]
    return [o.reshape((4,) + s.shape) for o, s in zip(outs, shards)]


def _all_gather_async(name, shards, collective_id):
    n = len(shards)
    split = [s.reshape(2, s.shape[0] // 2, s.shape[1]) for s in shards]
    ins = [jax.new_ref(s, memory_space=pltpu.MemorySpace.HBM) for s in split]
    outs = [jax.empty_ref(jax.ShapeDtypeStruct((4,) + s.shape, s.dtype), memory_space=pltpu.MemorySpace.HBM)
            for s in split]

    @pl.kernel(mesh=plsc.ScalarSubcoreMesh(axis_name="sequencer", num_cores=1), name=name,
               scratch_types=(pltpu.SemaphoreType.DMA((n, 6)), pltpu.SemaphoreType.DMA((n, 6))),
               compiler_params=pltpu.CompilerParams(collective_id=collective_id))
    def launch(send_sems, recv_sems):
        x, y, c, chips = _place()
        me = 2 * x + y
        sibling = (x, y, 1 - c)
        barrier = pltpu.get_barrier_semaphore()
        for peer in [(cx, cy, c) for cx, cy in chips] + [sibling]:
            pl.semaphore_signal(barrier, inc=1, device_id=peer, device_id_type=MESH)
        pl.semaphore_wait(barrier, 4)

        def remote(i, k, slot, part, to, src=None):
            dst = outs[i].at[slot, part]
            return pltpu.make_async_remote_copy(
                src_ref=dst if src is None else src, dst_ref=dst,
                send_sem=send_sems.at[i, k], recv_sem=recv_sems.at[i, k],
                device_id=to, device_id_type=MESH)

        started = []
        for i in range(n):
            for k, (cx, cy) in enumerate(chips):
                cp = remote(i, k, me, c, (cx, cy, c), src=ins[i].at[c])
                cp.start()
                started.append(cp)
        for i in range(n):
            for k, (cx, cy) in enumerate(chips):
                remote(i, k, 2 * cx + cy, c, (x, y, c)).wait_recv()
                cp = remote(i, 3 + k, 2 * cx + cy, c, sibling)
                cp.start()
                started.append(cp)
        for i in range(n):
            for k, (cx, cy) in enumerate(chips):
                remote(i, 3 + k, 2 * cx + cy, 1 - c, (x, y, c)).wait_recv()
        for cp in started:
            cp.wait_send()

    launch()
    raw = [o[...] for o in outs]

    def finish(after, which):
        arrived, _ = lax.optimization_barrier(([raw[i] for i in which], after))
        me = 2 * lax.axis_index("x") + lax.axis_index("y")
        gathered = [lax.dynamic_update_slice(a, split[i][None], (me, 0, 0, 0)) for a, i in zip(arrived, which)]
        return [g.reshape((4,) + shards[i].shape) for g, i in zip(gathered, which)]

    return finish


def _pair_send_halves(name, grads):
    n = len(grads)

    def body(*refs):
        ins, outs = refs[:n], refs[n:2 * n]
        send_sems, recv_sems = refs[2 * n:]
        x, y, c, _ = _place()
        cps = []
        for i in range(n):
            half = ins[i].shape[1] // 2
            cp = pltpu.make_async_remote_copy(
                src_ref=ins[i].at[:, pl.ds((1 - c) * half, half)], dst_ref=outs[i],
                send_sem=send_sems.at[i], recv_sem=recv_sems.at[i],
                device_id=(x, y, 1 - c), device_id_type=MESH)
            cp.start()
            cps.append(cp)
        for cp in cps:
            cp.wait()

    return pl.pallas_call(
        body, name=name,
        out_shape=[jax.ShapeDtypeStruct((4, g.shape[1] // 2, g.shape[2]), g.dtype) for g in grads],
        in_specs=[ANY] * n, out_specs=[ANY] * n,
        scratch_shapes=[pltpu.SemaphoreType.DMA((n,)), pltpu.SemaphoreType.DMA((n,))],
    )(*grads)


def _pair_send_halves_async(name, grads, collective_id):
    n = len(grads)
    ins = [jax.new_ref(g, memory_space=pltpu.MemorySpace.HBM) for g in grads]
    outs = [jax.empty_ref(jax.ShapeDtypeStruct((4, g.shape[1] // 2, g.shape[2]), g.dtype),
                          memory_space=pltpu.MemorySpace.HBM) for g in grads]

    @pl.kernel(mesh=plsc.ScalarSubcoreMesh(axis_name="sequencer", num_cores=1), name=name,
               scratch_types=(pltpu.SemaphoreType.DMA((n,)), pltpu.SemaphoreType.DMA((n,))),
               compiler_params=pltpu.CompilerParams(collective_id=collective_id))
    def launch(send_sems, recv_sems):
        x, y, c, _ = _place()
        barrier = pltpu.get_barrier_semaphore()
        pl.semaphore_signal(barrier, inc=1, device_id=(x, y, 1 - c), device_id_type=MESH)
        pl.semaphore_wait(barrier, 1)
        cps = []
        for i in range(n):
            half = ins[i].shape[1] // 2
            cp = pltpu.make_async_remote_copy(
                src_ref=ins[i].at[:, pl.ds((1 - c) * half, half)], dst_ref=outs[i],
                send_sem=send_sems.at[i], recv_sem=recv_sems.at[i],
                device_id=(x, y, 1 - c), device_id_type=MESH)
            cp.start()
            cps.append(cp)
        for cp in cps:
            cp.wait()

    launch()
    return [o[...] for o in outs]


def _chip_scatter(name, parts):
    n = len(parts)

    def body(*refs):
        _scatter_copies(refs[:n], refs[n:2 * n], *refs[2 * n:])

    arrived = pl.pallas_call(
        body, name=name,
        out_shape=[jax.ShapeDtypeStruct(p.shape, p.dtype) for p in parts],
        in_specs=[ANY] * n, out_specs=[ANY] * n,
        scratch_shapes=[pltpu.SemaphoreType.DMA((n, 3)), pltpu.SemaphoreType.DMA((n, 3))],
    )(*parts)
    return _own_slots(parts, arrived)


def _scatter_copies(ins, outs, send_sems, recv_sems):
    x, y, c, chips = _place()
    me = 2 * x + y
    sends = []
    for i in range(len(ins)):
        for k, (cx, cy) in enumerate(chips):
            cp = pltpu.make_async_remote_copy(
                src_ref=ins[i].at[2 * cx + cy], dst_ref=outs[i].at[me],
                send_sem=send_sems.at[i, k], recv_sem=recv_sems.at[i, k],
                device_id=(cx, cy, c), device_id_type=MESH)
            cp.start()
            sends.append(cp)
    for i in range(len(ins)):
        for k, (cx, cy) in enumerate(chips):
            got = outs[i].at[2 * cx + cy]
            pltpu.make_async_remote_copy(
                src_ref=got, dst_ref=got, send_sem=send_sems.at[i, k], recv_sem=recv_sems.at[i, k],
                device_id=(x, y, c), device_id_type=MESH).wait_recv()
    for cp in sends:
        cp.wait_send()


def _own_slots(parts, arrived):
    me = 2 * lax.axis_index("x") + lax.axis_index("y")
    return [lax.dynamic_update_slice(a, lax.dynamic_slice_in_dim(p, me, 1, axis=0), (me, 0, 0))
            for p, a in zip(parts, arrived)]


def _chip_scatter_async(name, parts, collective_id):
    n = len(parts)
    ins = [jax.new_ref(p, memory_space=pltpu.MemorySpace.HBM) for p in parts]
    outs = [jax.empty_ref(jax.ShapeDtypeStruct(p.shape, p.dtype), memory_space=pltpu.MemorySpace.HBM) for p in parts]

    @pl.kernel(mesh=plsc.ScalarSubcoreMesh(axis_name="sequencer", num_cores=1), name=name,
               scratch_types=(pltpu.SemaphoreType.DMA((n, 3)), pltpu.SemaphoreType.DMA((n, 3))),
               compiler_params=pltpu.CompilerParams(collective_id=collective_id))
    def launch(send_sems, recv_sems):
        x, y, c, chips = _place()
        barrier = pltpu.get_barrier_semaphore()
        for cx, cy in chips:
            pl.semaphore_signal(barrier, inc=1, device_id=(cx, cy, c), device_id_type=MESH)
        pl.semaphore_wait(barrier, 3)
        _scatter_copies(ins, outs, send_sems, recv_sems)

    launch()
    return _own_slots(parts, [o[...] for o in outs])


def _pair_swap(name, halves):
    n = len(halves)

    def body(*refs):
        ins, outs = refs[:n], refs[n:2 * n]
        send_sems, recv_sems = refs[2 * n:]
        x, y, c, _ = _place()
        cps = []
        for i in range(n):
            cp = pltpu.make_async_remote_copy(
                src_ref=ins[i], dst_ref=outs[i], send_sem=send_sems.at[i], recv_sem=recv_sems.at[i],
                device_id=(x, y, 1 - c), device_id_type=MESH)
            cp.start()
            cps.append(cp)
        for cp in cps:
            cp.wait()

    return pl.pallas_call(
        body, name=name,
        out_shape=[jax.ShapeDtypeStruct(h.shape, h.dtype) for h in halves],
        in_specs=[ANY] * n, out_specs=[ANY] * n,
        scratch_shapes=[pltpu.SemaphoreType.DMA((n,)), pltpu.SemaphoreType.DMA((n,))],
    )(*halves)


def _row_block(rows, cols, n_bufs, budget=20 * 1024 * 1024):
    best = min(rows, 16)
    for b in range(16, rows + 1, 16):
        if rows % b == 0 and 2 * n_bufs * b * cols * 4 <= budget:
            best = b
    return best


def _pair_add(tag, grad, got, c_arr, out_dtype):
    _, rows, cols = grad.shape
    half = rows // 2
    bh = _row_block(half, cols, 3)
    nb = half // bh

    def body(c_ref, g_ref, a_ref, o_ref):
        o_ref[...] = (g_ref[...] + a_ref[...]).astype(out_dtype)

    return pl.pallas_call(
        body, name=f"pair_add_{tag}",
        out_shape=jax.ShapeDtypeStruct((4, half, cols), out_dtype),
        grid_spec=pltpu.PrefetchScalarGridSpec(
            num_scalar_prefetch=1, grid=(4, nb),
            in_specs=[pl.BlockSpec((None, bh, cols), lambda j, r, c: (j, c[0] * nb + r, 0)),
                      pl.BlockSpec((None, bh, cols), lambda j, r, c: (j, r, 0))],
            out_specs=pl.BlockSpec((None, bh, cols), lambda j, r, c: (j, r, 0))),
        compiler_params=_params(("parallel", "parallel")),
    )(c_arr, grad, got)


def _chip_add(tag, parts):
    _, half, cols = parts.shape
    bh = _row_block(half, cols, 5)

    def body(p_ref, o_ref):
        a, b, c, d = [p_ref[j].astype(F32) for j in range(4)]
        o_ref[...] = ((a + b) + c) + d

    return pl.pallas_call(
        body, name=f"chip_add_{tag}",
        out_shape=jax.ShapeDtypeStruct((half, cols), F32),
        grid=(half // bh,),
        in_specs=[pl.BlockSpec((4, bh, cols), lambda r: (0, r, 0))],
        out_specs=pl.BlockSpec((bh, cols), lambda r: (r, 0)),
        compiler_params=_params(("parallel",)),
    )(parts)


def _adamw(tag, w, mine, theirs, m, v, c_arr):
    rows, cols = w.shape
    half = rows // 2
    br = _row_block(half, cols, 9)
    nb = half // br

    def body(c_ref, w_ref, a_ref, b_ref, m_ref, v_ref, g_ref, d_ref, mo_ref, vo_ref):
        own = (pl.program_id(0) // nb) == c_ref[0]
        g = jnp.where(own, a_ref[...], b_ref[...])
        g_ref[...] = g
        m_new = ADAM_B1 * m_ref[...] + (1.0 - ADAM_B1) * g
        v_new = ADAM_B2 * v_ref[...] + (1.0 - ADAM_B2) * (g * g)
        m_hat = m_new / (1.0 - ADAM_B1 ** ADAM_STEP)
        v_hat = v_new / (1.0 - ADAM_B2 ** ADAM_STEP)
        d_ref[...] = -ADAM_LR * (m_hat / (jnp.sqrt(v_hat) + ADAM_EPS) + ADAM_WD * w_ref[...])
        mo_ref[...] = m_new
        vo_ref[...] = v_new

    spec = pl.BlockSpec((br, cols), lambda r, c: (r, 0))
    mine_spec = pl.BlockSpec((br, cols), lambda r, c: (jnp.clip(r - c[0] * nb, 0, nb - 1), 0))
    theirs_spec = pl.BlockSpec((br, cols), lambda r, c: (jnp.clip(r - (1 - c[0]) * nb, 0, nb - 1), 0))
    return pl.pallas_call(
        body, name=f"adamw_{tag}",
        out_shape=[jax.ShapeDtypeStruct((rows, cols), F32)] * 4,
        grid_spec=pltpu.PrefetchScalarGridSpec(
            num_scalar_prefetch=1, grid=(rows // br,),
            in_specs=[spec, mine_spec, theirs_spec, spec, spec], out_specs=[spec] * 4),
        compiler_params=_params(("arbitrary",)),
    )(c_arr, w, mine, theirs, m, v)


def _matmul(name, x, w, out_shape, grid, x_spec, w_spec, o_spec, *, nt=False, vmem=None):
    nk = grid[2]
    acc_shape = tuple(d for d in o_spec.block_shape if d is not None)

    def body(x_ref, w_ref, o_ref, acc_ref):
        k = pl.program_id(2)
        part = _dot_nt(x_ref[...], w_ref[...]) if nt else _dot(x_ref[...], w_ref[...])
        if nk == 1:
            o_ref[...] = part.astype(o_ref.dtype)
        else:
            @pl.when(k == 0)
            def _():
                acc_ref[...] = part

            @pl.when(k > 0)
            def _():
                acc_ref[...] += part

            @pl.when(k == nk - 1)
            def _():
                o_ref[...] = acc_ref[...].astype(o_ref.dtype)

    return pl.pallas_call(
        body, name=name, out_shape=out_shape, grid=grid,
        in_specs=[x_spec, w_spec], out_specs=o_spec,
        scratch_shapes=[pltpu.VMEM(acc_shape if nk > 1 else (8, 128), F32)],
        compiler_params=_params(("parallel", "parallel", "arbitrary"), vmem),
    )(x, w)


def _weight_grad(name, xt, dy, bn, out_rows=None):
    m, t = xt.shape
    n = dy.shape[1]
    bm = m if out_rows is None else out_rows
    bk = _k_tile(t)
    return _matmul(
        name, xt, dy, jax.ShapeDtypeStruct((m, n), F32), (m // bm, n // bn, t // bk),
        pl.BlockSpec((bm, bk), lambda a, b, k: (a, k)),
        pl.BlockSpec((bk, bn), lambda a, b, k: (k, b)),
        pl.BlockSpec((bm, bn), lambda a, b, k: (a, b)), vmem=VMEM_BIG)


def _weight_grad_t(name, xt, dy):
    m, t = xt.shape
    n = dy.shape[1]
    bn = min(n, 512)
    bk = _k_tile(t)
    nk = t // bk

    def body(x_ref, dy_ref, o_ref, acc_ref):
        k = pl.program_id(1)
        part = _dot(x_ref[...], dy_ref[...].astype(BF16))

        @pl.when(k == 0)
        def _():
            acc_ref[...] = part

        @pl.when(k > 0)
        def _():
            acc_ref[...] += part

        @pl.when(k == nk - 1)
        def _():
            o_ref[...] = acc_ref[...].T

    return pl.pallas_call(
        body, name=name, out_shape=jax.ShapeDtypeStruct((n, m), F32), grid=(n // bn, nk),
        in_specs=[pl.BlockSpec((m, bk), lambda b, k: (0, k)), pl.BlockSpec((bk, bn), lambda b, k: (k, b))],
        out_specs=pl.BlockSpec((bn, m), lambda b, k: (b, 0)),
        scratch_shapes=[pltpu.VMEM((m, bn), F32)],
        compiler_params=_params(("parallel", "arbitrary"), VMEM_BIG),
    )(xt, dy)


def _mix_in_bwd(pieces, wt, h, g, dh_in):
    t, d = h.shape
    tm = ROW_TILE // 2
    widths = [p.shape[1] for p in pieces]
    n = len(pieces)

    def body(*refs):
        dy_refs, (w_ref, h_ref, g_ref, dhi_ref, dh_ref, dg_ref) = refs[:n], refs[n:]

        @pl.when(pl.program_id(0) == 0)
        def _():
            dg_ref[...] = jnp.zeros_like(dg_ref)

        dn, off = None, 0
        for dy_ref, wd in zip(dy_refs, widths):
            part = _dot(dy_ref[...].astype(BF16), w_ref[off:off + wd, :])
            dn = part if dn is None else dn + part
            off += wd
        dx, dg = _rms_bwd(h_ref[...], g_ref[...], dn)
        dh_ref[...] = dhi_ref[...] + dx
        dg_ref[...] += dg

    row = pl.BlockSpec((tm, d), lambda i: (i, 0))
    vec = pl.BlockSpec((1, d), lambda i: (0, 0))
    return pl.pallas_call(
        body, name="mix_in_bwd",
        out_shape=[jax.ShapeDtypeStruct((t, d), F32), jax.ShapeDtypeStruct((1, d), F32)],
        grid=(t // tm,),
        in_specs=[pl.BlockSpec((tm, wd), lambda i: (i, 0)) for wd in widths]
        + [pl.BlockSpec(wt.shape, lambda i: (0, 0)), row, vec, row],
        out_specs=[row, vec],
        compiler_params=_params(("arbitrary",), VMEM_BIG),
    )(*pieces, wt, h, g, dh_in)


def _norm_fwd(name, h, g):
    t, d = h.shape
    tm = ROW_TILE

    def body(h_ref, g_ref, n_ref, nt_ref):
        x = h_ref[...]
        y = x * _rstd(x) * g_ref[...]
        n_ref[...] = y.astype(BF16)
        nt_ref[...] = y.T.astype(BF16)

    return pl.pallas_call(
        body, name=name,
        out_shape=[jax.ShapeDtypeStruct((t, d), BF16), jax.ShapeDtypeStruct((d, t), BF16)],
        grid=(t // tm,),
        in_specs=[pl.BlockSpec((tm, d), lambda i: (i, 0)), pl.BlockSpec((1, d), lambda i: (0, 0))],
        out_specs=[pl.BlockSpec((tm, d), lambda i: (i, 0)), pl.BlockSpec((d, tm), lambda i: (0, i))],
        compiler_params=_params(("parallel",)),
    )(h, g)


def _slot_of(kk):
    return (kk % 2) * 2 + kk // 2


def _ffn_in(name, n, w4):
    t, d = n.shape
    cw = w4.shape[2]
    tm = ROW_TILE

    def body(x_ref, wg_ref, wu_ref, ab_ref, s_ref, st_ref):
        x = x_ref[...]
        a = _dot(x, wg_ref[...])
        b = _dot(x, wu_ref[...])
        ab_ref[:, :cw] = a.astype(BF16)
        ab_ref[:, cw:] = b.astype(BF16)
        s = a * _sigmoid(a) * b
        s_ref[...] = s.astype(BF16)
        st_ref[...] = s.T.astype(BF16)

    return pl.pallas_call(
        body, name=name,
        out_shape=[jax.ShapeDtypeStruct((t, 4 * cw), BF16), jax.ShapeDtypeStruct((t, 2 * cw), BF16),
                   jax.ShapeDtypeStruct((2 * cw, t), BF16)],
        grid=(2, t // tm),
        in_specs=[pl.BlockSpec((tm, d), lambda j, i: (i, 0)),
                  pl.BlockSpec((None, d, cw), lambda j, i: (j, 0, 0)),
                  pl.BlockSpec((None, d, cw), lambda j, i: (2 + j, 0, 0))],
        out_specs=[pl.BlockSpec((tm, 2 * cw), lambda j, i: (i, j)),
                   pl.BlockSpec((tm, cw), lambda j, i: (i, j)),
                   pl.BlockSpec((cw, tm), lambda j, i: (j, i))],
        compiler_params=_params(("parallel", "parallel"), VMEM_BIG),
    )(n, w4, w4)


def _mm_resid_norm(name, x, w, h, g_post, alpha, g_next):
    t, kdim = x.shape
    d = w.shape[1]
    tm = ROW_TILE
    with_next = g_next is not None

    def body(x_ref, w_ref, h_ref, gp_ref, gn_ref, f_ref, hn_ref, *rest):
        f = _dot(x_ref[...], w_ref[...])
        f_ref[...] = f
        hn = h_ref[...] + alpha * (f * _rstd(f) * gp_ref[...])
        hn_ref[...] = hn
        if with_next:
            y = hn * _rstd(hn) * gn_ref[...]
            rest[0][...] = y.astype(BF16)
            rest[1][...] = y.T.astype(BF16)

    row = lambda i: (i, 0)
    vec = pl.BlockSpec((1, d), lambda i: (0, 0))
    out_shape = [jax.ShapeDtypeStruct((t, d), F32), jax.ShapeDtypeStruct((t, d), F32)]
    out_specs = [pl.BlockSpec((tm, d), row), pl.BlockSpec((tm, d), row)]
    if with_next:
        out_shape += [jax.ShapeDtypeStruct((t, d), BF16), jax.ShapeDtypeStruct((d, t), BF16)]
        out_specs += [pl.BlockSpec((tm, d), row), pl.BlockSpec((d, tm), lambda i: (0, i))]
    return pl.pallas_call(
        body, name=name, out_shape=out_shape, grid=(t // tm,),
        in_specs=[pl.BlockSpec((tm, kdim), row), pl.BlockSpec((kdim, d), lambda i: (0, 0)),
                  pl.BlockSpec((tm, d), row), vec, vec],
        out_specs=out_specs,
        compiler_params=_params(("parallel",), VMEM_BIG),
    )(x, w, h, g_post, g_post if g_next is None else g_next)


def _in_proj(u, w):
    t, d = u.shape
    nz = w.shape[0]
    nq = 3 * ATTN_W
    tm = ROW_TILE // 2

    def body(u_ref, w_ref, qkv_ref, z_ref):
        qkv_ref[...] = _dot_nt(u_ref[...], w_ref[0:nq, :]).astype(BF16)
        z_ref[...] = _dot_nt(u_ref[...], w_ref[nq:, :])

    return pl.pallas_call(
        body, name="mix_in_proj",
        out_shape=[jax.ShapeDtypeStruct((t, nq), BF16), jax.ShapeDtypeStruct((t, nz - nq), F32)],
        grid=(t // tm,),
        in_specs=[pl.BlockSpec((tm, d), lambda i: (i, 0)), pl.BlockSpec((nz, d), lambda i: (0, 0))],
        out_specs=[pl.BlockSpec((tm, nq), lambda i: (i, 0)), pl.BlockSpec((tm, nz - nq), lambda i: (i, 0))],
        compiler_params=_params(("parallel",), VMEM_BIG),
    )(u, w)


def _gate_prep(z, b_pad, f_col):
    t = z.shape[0]
    tm = ROW_TILE

    def body(z_ref, b_ref, f_ref, carry_ref):
        i = pl.program_id(0)

        @pl.when(i == 0)
        def _():
            carry_ref[...] = jnp.zeros_like(carry_ref)

        xs = z_ref[...] + b_ref[...]
        logf = jnp.minimum(xs, 0.0) - jnp.log(1.0 + jnp.exp(-jnp.abs(xs)))
        row = i * tm + lax.broadcasted_iota(jnp.int32, (tm, 1), 0)
        logf = jnp.where(row >= ROW_PAD, logf, 0.0)
        tri = (lax.broadcasted_iota(jnp.int32, (tm, tm), 0) >= lax.broadcasted_iota(jnp.int32, (tm, tm), 1))
        f = jnp.dot(tri.astype(F32), logf, preferred_element_type=F32, precision=lax.Precision.HIGHEST)
        f = f + carry_ref[0:1, :]
        f_ref[...] = f
        carry_ref[...] = jnp.broadcast_to(f[tm - 1:tm, :], carry_ref.shape)

    return pl.pallas_call(
        body, name="forget_gate_cumsum", out_shape=jax.ShapeDtypeStruct((t, 128), F32),
        grid=(t // tm,),
        in_specs=[pl.BlockSpec((tm, 128), lambda i: (i, f_col // 128)), pl.BlockSpec((1, 128), lambda i: (0, 0))],
        out_specs=pl.BlockSpec((tm, 128), lambda i: (i, 0)),
        scratch_shapes=[pltpu.VMEM((8, 128), F32)],
        compiler_params=_params(("arbitrary",)),
    )(z, b_pad)


def _lane_halves():
    lane = lax.broadcasted_iota(jnp.int32, (1, 128), 1)
    return lane < HEAD_DIM


def _causal_mask(tq, tk, row0=0):
    row = row0 + lax.broadcasted_iota(jnp.int32, (tq, 1), 0)
    col = lax.broadcasted_iota(jnp.int32, (1, tk), 1)
    return col <= row


def _lane_one(lane):
    return (lax.broadcasted_iota(jnp.int32, (1, 128), 1) == lane).astype(BF16)


def _split3(x):
    hi = x.astype(BF16)
    rest = x - hi.astype(F32)
    mid = rest.astype(BF16)
    return hi, mid, (rest - mid.astype(F32)).astype(BF16)


def _split3_glue(x):
    hi = lax.reduce_precision(x, 8, 7)
    mid = lax.reduce_precision(x - hi, 8, 7)
    lo = lax.reduce_precision((x - hi) - mid, 8, 7)
    return hi.astype(BF16), mid.astype(BF16), lo.astype(BF16)


def _aug_pairs(cols):
    t = cols[0].shape[0]
    a = jnp.pad(jnp.stack(cols, axis=2), ((0, 0), (0, 0), (0, HEAD_DIM - len(cols))))
    a = a.reshape(t, 4, 2, HEAD_DIM)[:, :, ::-1, :]
    return jnp.transpose(a.reshape(t, 4, 128), (1, 0, 2))


def _attn_bias_operands(f_heads, lse_heads=None):
    t = f_heads.shape[0]
    one = jnp.ones((t, HEADS), BF16)
    row = lax.broadcasted_iota(jnp.int32, (t, 1), 0)
    fq = _split3_glue(f_heads)
    fk = _split3_glue(jnp.where(row < ROW_PAD, 1e9, f_heads))
    q_cols = list(fq) + [one] * 3
    k_cols = [one] * 3 + [-c for c in fk]
    if lse_heads is not None:
        q_cols += [-c for c in _split3_glue(lse_heads)]
        k_cols += [one] * 3
    return _aug_pairs(q_cols), _aug_pairs(k_cols)


def _attn_steps(nq, by_key):
    if by_key:
        pairs = [(qi, ki) for ki in range(nq) for qi in range(ki, nq)]
    else:
        pairs = [(qi, ki) for qi in range(nq) for ki in range(qi + 1)]
    return (jnp.array([p[0] for p in pairs], jnp.int32), jnp.array([p[1] for p in pairs], jnp.int32))


def _attn_fwd(z, aug_q, aug_k):
    t = z.shape[0]
    tq = tk = ROW_TILE
    nq = t // tq
    grp = ATTN_KV_GROUP
    steps = [(qi, ka) for qi in range(nq) for ka in range(0, qi + 1, grp)]
    q_tab = jnp.array([qi for qi, _ in steps], jnp.int32)
    k_tab = jnp.array([ka for _, ka in steps], jnp.int32)

    def body(qt_ref, kt_ref, q_ref, *refs):
        k_refs, v_refs, aq_ref, ak_refs = refs[:grp], refs[grp:2 * grp], refs[2 * grp], refs[2 * grp + 1:3 * grp + 1]
        o_ref, lse_ref, m_ref, l_ref, acc_ref = refs[3 * grp + 1:]
        step = pl.program_id(1)
        qi, ka = qt_ref[step], kt_ref[step]

        @pl.when(ka == 0)
        def _():
            m_ref[...] = jnp.full_like(m_ref, NEG)
            l_ref[...] = jnp.zeros_like(l_ref)
            acc_ref[...] = jnp.zeros_like(acc_ref)

        def sweep(diagonal):
            first = _lane_halves()
            halves = (first, jnp.logical_not(first))
            q = (q_ref[...] * (HEAD_DIM ** -0.5)).astype(BF16)
            aq = aq_ref[...]
            qa = [jnp.where(lanes, q, aq) for lanes in halves]
            blocks = list(zip(k_refs, v_refs, ak_refs, diagonal))
            s = []
            for k_ref, _, ak_ref, diag in blocks:
                k, ak = k_ref[...].astype(BF16), ak_ref[...]
                for hh, lanes in enumerate(halves):
                    s_c = _dot_nt(qa[hh], jnp.where(lanes, k, ak))
                    s.append(jnp.where(_causal_mask(tq, tk), s_c, NEG) if diag else s_c)
            nb = len(blocks)
            m_prev = [m_ref[:, c0:c0 + 1] for c0 in (0, HEAD_DIM)]
            m_new = []
            for hh in range(2):
                m_h = m_prev[hh]
                for b in range(nb):
                    m_h = jnp.maximum(m_h, jnp.max(s[2 * b + hh], axis=1, keepdims=True))
                m_new.append(m_h)
            pv = [None, None]
            for b, (_, v_ref, _, _) in enumerate(blocks):
                v = v_ref[...].astype(BF16)
                for hh, (lanes, a0) in enumerate(zip(halves, (HEAD_DIM, 0))):
                    part = _dot(jnp.exp(s[2 * b + hh] - m_new[hh]).astype(BF16), jnp.where(lanes, v, _lane_one(a0)))
                    pv[hh] = part if pv[hh] is None else pv[hh] + part
            al0, al1 = [jnp.exp(mp - m_h) for mp, m_h in zip(m_prev, m_new)]
            l0 = al0 * l_ref[:, 0:1] + pv[0][:, HEAD_DIM:HEAD_DIM + 1]
            l1 = al1 * l_ref[:, HEAD_DIM:HEAD_DIM + 1] + pv[1][:, 0:1]
            acc_ref[...] = acc_ref[...] * jnp.where(first, al0, al1) + jnp.where(first, pv[0], pv[1])
            m_ref[...] = jnp.where(first, m_new[0], m_new[1])
            l_ref[...] = jnp.where(first, l0, l1)

        def finish():
            o_ref[...] = acc_ref[...] / l_ref[...]
            lse_ref[...] = m_ref[...] + jnp.log(l_ref[...])

        @pl.when(ka + grp - 1 < qi)
        def _():
            sweep((False,) * grp)

        for nb in range(1, grp + 1):
            @pl.when(ka + nb - 1 == qi)
            def _(nb=nb):
                sweep((False,) * (nb - 1) + (True,))
                finish()

    def kblock(j):
        return lambda s, qt, kt: jnp.minimum(kt[s] + j, qt[s])

    kbs = [kblock(j) for j in range(grp)]
    return pl.pallas_call(
        body, name="attention_fwd",
        out_shape=[jax.ShapeDtypeStruct((t, ATTN_W), F32), jax.ShapeDtypeStruct((t, ATTN_W), F32)],
        grid_spec=pltpu.PrefetchScalarGridSpec(
            num_scalar_prefetch=2, grid=(4, len(steps)),
            in_specs=[pl.BlockSpec((tq, 128), lambda p, s, qt, kt: (qt[s], p))]
            + [pl.BlockSpec((tk, 128), functools.partial(lambda p, s, qt, kt, kb: (kb(s, qt, kt), 4 + p), kb=kb))
               for kb in kbs]
            + [pl.BlockSpec((tk, 128), functools.partial(lambda p, s, qt, kt, kb: (kb(s, qt, kt), 8 + p), kb=kb))
               for kb in kbs]
            + [pl.BlockSpec((None, tq, 128), lambda p, s, qt, kt: (p, qt[s], 0))]
            + [pl.BlockSpec((None, tk, 128), functools.partial(lambda p, s, qt, kt, kb: (p, kb(s, qt, kt), 0), kb=kb))
               for kb in kbs],
            out_specs=[pl.BlockSpec((tq, 128), lambda p, s, qt, kt: (qt[s], p)),
                       pl.BlockSpec((tq, 128), lambda p, s, qt, kt: (qt[s], p))],
            scratch_shapes=[pltpu.VMEM((tq, 128), F32)] * 3),
        compiler_params=_params(("parallel", "arbitrary"), VMEM_BIG),
    )(q_tab, k_tab, z, *([z] * (2 * grp)), aug_q, *([aug_k] * grp))


def _attn_bwd(z, aug_q, aug_k, o, do):
    t = z.shape[0]
    tq = tk = ROW_TILE
    nq = t // tq
    q_tab, k_tab = _attn_steps(nq, by_key=True)
    tn = (((0,), (0,)), ((), ()))

    def body(qt_ref, kt_ref, q_ref, k_ref, v_ref, aq_ref, ak_ref, o_ref, do_ref,
             dq_ref, dk_ref, dv_ref, dfk_ref, dfq_ref):
        step = pl.program_id(1)
        qi, ki = qt_ref[step], kt_ref[step]
        rows = pl.ds(pl.multiple_of(qi * tq, tq), tq)

        @pl.when(ki == 0)
        def _():
            dq_ref[rows, :] = jnp.zeros((tq, 128), F32)
            dfq_ref[rows, :] = jnp.zeros((tq, 128), F32)

        @pl.when(qi == ki)
        def _():
            dk_ref[...] = jnp.zeros_like(dk_ref)
            dv_ref[...] = jnp.zeros_like(dv_ref)
            dfk_ref[...] = jnp.zeros_like(dfk_ref)

        def sweep(diagonal):
            first = _lane_halves()
            lane = lax.broadcasted_iota(jnp.int32, (1, 128), 1)
            scale = HEAD_DIM ** -0.5
            q = (q_ref[...] * scale).astype(BF16)
            k = k_ref[...].astype(BF16)
            v = v_ref[...].astype(BF16)
            do_ = do_ref[...]
            do16 = do_.astype(BF16)
            od = o_ref[...] * do_
            aq, ak = aq_ref[...], ak_ref[...]
            halves = (first, jnp.logical_not(first))
            a0, a1 = HEAD_DIM, 0
            dos, vs = [], []
            for lanes, a in zip(halves, (a0, a1)):
                d_hi, d_mid, d_lo = _split3(jnp.sum(jnp.where(lanes, od, 0.0), axis=1, keepdims=True))
                minus_delta = jnp.where(lane == a, -d_hi, jnp.where(lane == a + 1, -d_mid,
                                        jnp.where(lane == a + 2, -d_lo, jnp.zeros((), BF16))))
                dos.append(jnp.where(lanes, do16, minus_delta))
                vs.append(jnp.where(lanes, v, ((lane >= a) & (lane < a + 3)).astype(BF16)))
            s = [_dot_nt(jnp.where(lanes, q, aq), jnp.where(lanes, k, ak)) for lanes in halves]
            dp = [_dot_nt(do_h, v_h) for do_h, v_h in zip(dos, vs)]
            p = [jnp.exp(s_h) for s_h in s]
            if diagonal:
                p = [jnp.where(_causal_mask(tq, tk), p_h, 0.0) for p_h in p]
            ds16 = [(p_h * dp_h).astype(BF16) for p_h, dp_h in zip(p, dp)]
            dv0, dv1 = [lax.dot_general(p_h.astype(BF16), jnp.where(lanes, do16, jnp.zeros((), BF16)), tn,
                                        preferred_element_type=F32) for p_h, lanes in zip(p, halves)]
            dk0, dk1 = [lax.dot_general(ds_h, jnp.where(lanes, q, _lane_one(a)), tn, preferred_element_type=F32)
                        for ds_h, lanes, a in zip(ds16, halves, (a0, a1))]
            dq0, dq1 = [_dot(ds_h, jnp.where(lanes, k, _lane_one(a))) for ds_h, lanes, a in zip(ds16, halves, (a0, a1))]
            dq_ref[rows, :] += jnp.where(first, dq0, dq1) * scale
            dfq_ref[rows, :] += jnp.where(first, dq0[:, a0:a0 + 1], dq1[:, a1:a1 + 1])
            dk_ref[...] += jnp.where(first, dk0, dk1)
            dfk_ref[...] += jnp.where(first, dk0[:, a0:a0 + 1], dk1[:, a1:a1 + 1])
            dv_ref[...] += dv0 + dv1

        @pl.when(qi > ki)
        def _():
            sweep(False)

        @pl.when(qi == ki)
        def _():
            sweep(True)

    qrow = lambda p, s, qt, kt: (qt[s], p)
    krow = lambda p, s, qt, kt: (kt[s], p)
    return pl.pallas_call(
        body, name="attention_bwd",
        out_shape=[jax.ShapeDtypeStruct((t, ATTN_W), F32)] * 5,
        grid_spec=pltpu.PrefetchScalarGridSpec(
            num_scalar_prefetch=2, grid=(4, int(q_tab.shape[0])),
            in_specs=[pl.BlockSpec((tq, 128), qrow),
                      pl.BlockSpec((tk, 128), lambda p, s, qt, kt: (kt[s], 4 + p)),
                      pl.BlockSpec((tk, 128), lambda p, s, qt, kt: (kt[s], 8 + p)),
                      pl.BlockSpec((None, tq, 128), lambda p, s, qt, kt: (p, qt[s], 0)),
                      pl.BlockSpec((None, tk, 128), lambda p, s, qt, kt: (p, kt[s], 0)),
                      pl.BlockSpec((tq, 128), qrow), pl.BlockSpec((tq, 128), qrow)],
            out_specs=[pl.BlockSpec((t, 128), lambda p, s, qt, kt: (0, p)),
                       pl.BlockSpec((tk, 128), krow), pl.BlockSpec((tk, 128), krow), pl.BlockSpec((tk, 128), krow),
                       pl.BlockSpec((t, 128), lambda p, s, qt, kt: (0, p))]),
        compiler_params=_params(("parallel", "arbitrary"), VMEM_BIG),
    )(q_tab, k_tab, z, z, z, aug_q, aug_k, o, do)


def _shifted(prev_rows, x, shift):
    tm = x.shape[0]
    return pltpu.roll(jnp.concatenate([prev_rows, x], axis=0), shift, 0)[8:8 + tm]


def _ahead(x, next_rows, shift):
    tm = x.shape[0]
    return pltpu.roll(jnp.concatenate([x, next_rows], axis=0), tm + 8 - shift, 0)[0:tm]


def _conv_col0(z):
    return (z.shape[1] - F_PAD - 3 * CONV_W) // CONV_W


def _conv_specs(tm, c0):
    cols = (c0, c0 + 1, c0 + 2)
    tiles = [pl.BlockSpec((tm, CONV_W), functools.partial(lambda i, c: (i, c), c=c)) for c in cols]
    halos = [pl.BlockSpec((8, CONV_W), functools.partial(lambda i, c: (jnp.maximum(i * (tm // 8) - 1, 0), c), c=c))
             for c in cols]
    return tiles, halos


def _conv_gate(z, conv_w):
    t = z.shape[0]
    tm = ROW_TILE
    nt = t // tm

    def body(cb_ref, cc_ref, ci_ref, hc_ref, hi_ref, w_ref, g_ref, gt_ref):
        i = pl.program_id(0)
        cc = cc_ref[...] * ci_ref[...]
        prev = jnp.where(i > 0, hc_ref[...] * hi_ref[...], 0.0)
        conv = w_ref[0:1, :] * _shifted(prev, cc, 2) + w_ref[1:2, :] * _shifted(prev, cc, 1) + w_ref[2:3, :] * cc
        g = cb_ref[...] * conv
        g_ref[...] = g.astype(BF16)
        gt_ref[...] = g.T.astype(BF16)

    (cb, cc, ci), (_, hc, hi) = _conv_specs(tm, _conv_col0(z))
    return pl.pallas_call(
        body, name="conv_gate_fwd",
        out_shape=[jax.ShapeDtypeStruct((t, CONV_W), BF16), jax.ShapeDtypeStruct((CONV_W, t), BF16)],
        grid=(nt,),
        in_specs=[cb, cc, ci, hc, hi, pl.BlockSpec((8, CONV_W), lambda i: (0, 0))],
        out_specs=[pl.BlockSpec((tm, CONV_W), lambda i: (i, 0)), pl.BlockSpec((CONV_W, tm), lambda i: (0, i))],
        compiler_params=_params(("parallel",)),
    )(z, z, z, z, z, conv_w)


def _conv_bwd(z, dg, conv_w):
    t = z.shape[0]
    tm = ROW_TILE
    nt = t // tm

    def body(cb_ref, cc_ref, ci_ref, hc_ref, hi_ref, dg_ref, ncb_ref, ndg_ref, w_ref, dz_ref, dw_ref):
        i = pl.program_id(0)

        @pl.when(i == 0)
        def _():
            dw_ref[...] = jnp.zeros_like(dw_ref)

        cb, c_c, c_in = cb_ref[...], cc_ref[...], ci_ref[...]
        cc = c_c * c_in
        prev = jnp.where(i > 0, hc_ref[...] * hi_ref[...], 0.0)
        cc1, cc2 = _shifted(prev, cc, 1), _shifted(prev, cc, 2)
        w0, w1, w2 = w_ref[0:1, :], w_ref[1:2, :], w_ref[2:3, :]
        conv = w0 * cc2 + w1 * cc1 + w2 * cc
        dgv = dg_ref[...]
        dconv = dgv * cb
        nxt = jnp.where(i < nt - 1, ndg_ref[...] * ncb_ref[...], 0.0)
        dcc = w2 * dconv + w1 * _ahead(dconv, nxt, 1) + w0 * _ahead(dconv, nxt, 2)
        dz_ref[:, 0:CONV_W] = (dgv * conv).astype(BF16)
        dz_ref[:, CONV_W:2 * CONV_W] = (dcc * c_in).astype(BF16)
        dz_ref[:, 2 * CONV_W:] = (dcc * c_c).astype(BF16)
        dw_ref[0:1, :] += jnp.sum(dconv * cc2, axis=0, keepdims=True)
        dw_ref[1:2, :] += jnp.sum(dconv * cc1, axis=0, keepdims=True)
        dw_ref[2:3, :] += jnp.sum(dconv * cc, axis=0, keepdims=True)

    c0 = _conv_col0(z)
    (cb, cc, ci), (_, hc, hi) = _conv_specs(tm, c0)
    nxt = lambda i, c: (jnp.minimum((i + 1) * (tm // 8), t // 8 - 1), c)
    return pl.pallas_call(
        body, name="conv_gate_bwd",
        out_shape=[jax.ShapeDtypeStruct((t, 3 * CONV_W), BF16), jax.ShapeDtypeStruct((8, CONV_W), F32)],
        grid=(nt,),
        in_specs=[cb, cc, ci, hc, hi, pl.BlockSpec((tm, CONV_W), lambda i: (i, 0)),
                  pl.BlockSpec((8, CONV_W), lambda i: nxt(i, c0)), pl.BlockSpec((8, CONV_W), lambda i: nxt(i, 0)),
                  pl.BlockSpec((8, CONV_W), lambda i: (0, 0))],
        out_specs=[pl.BlockSpec((tm, 3 * CONV_W), lambda i: (i, 0)), pl.BlockSpec((8, CONV_W), lambda i: (0, 0))],
        compiler_params=_params(("arbitrary",)),
    )(z, z, z, z, z, dg, z, dg, conv_w)


def _branch_mix(z, o, g, w_ab, w_cb, d):
    t = z.shape[0]
    tm = ROW_TILE
    ga_col = 0

    def body(o_ref, g_ref, ga_ref, gc_ref, wa_ref, wc_ref, mp_ref, mpt_ref, ot_ref):
        o_ = o_ref[...]
        ya = _dot(o_.astype(BF16), wa_ref[...])
        yc = _dot(g_ref[...], wc_ref[...])
        mp = _sigmoid(ga_ref[...]) * ya + _sigmoid(gc_ref[...]) * yc
        mp_ref[...] = mp.astype(BF16)
        mpt_ref[...] = mp.T.astype(BF16)
        ot_ref[...] = o_.T.astype(BF16)

    return pl.pallas_call(
        body, name="branch_mix_fwd",
        out_shape=[jax.ShapeDtypeStruct((t, d), BF16), jax.ShapeDtypeStruct((d, t), BF16),
                   jax.ShapeDtypeStruct((ATTN_W, t), BF16)],
        grid=(t // tm,),
        in_specs=[pl.BlockSpec((tm, ATTN_W), lambda i: (i, 0)), pl.BlockSpec((tm, CONV_W), lambda i: (i, 0)),
                  pl.BlockSpec((tm, d), lambda i: (i, ga_col)), pl.BlockSpec((tm, d), lambda i: (i, ga_col + 1)),
                  pl.BlockSpec((ATTN_W, d), lambda i: (0, 0)), pl.BlockSpec((CONV_W, d), lambda i: (0, 0))],
        out_specs=[pl.BlockSpec((tm, d), lambda i: (i, 0)), pl.BlockSpec((d, tm), lambda i: (0, i)),
                   pl.BlockSpec((ATTN_W, tm), lambda i: (0, i))],
        compiler_params=_params(("parallel",), VMEM_BIG),
    )(o, g, z, z, w_ab, w_cb)


def _branch_bwd(z, o, g, dmixed, w_out, w_ab, w_cb, d):
    t = z.shape[0]
    tm = ROW_TILE // 2
    ga_col = 0

    def body(dm_ref, o_ref, g_ref, ga_ref, gc_ref, wo_ref, wa_ref, wc_ref, dya_ref, dyc_ref, dgt_ref, do_ref, dg_ref):
        dmp = _dot_nt(dm_ref[...], wo_ref[...])
        ya = _dot(o_ref[...].astype(BF16), wa_ref[...])
        yc = _dot(g_ref[...], wc_ref[...])
        sa, sc = _sigmoid(ga_ref[...]), _sigmoid(gc_ref[...])
        dya = (dmp * sa).astype(BF16)
        dyc = (dmp * sc).astype(BF16)
        dya_ref[...] = dya
        dyc_ref[...] = dyc
        dgt_ref[:, :d] = (dmp * ya * sa * (1.0 - sa)).astype(BF16)
        dgt_ref[:, d:] = (dmp * yc * sc * (1.0 - sc)).astype(BF16)
        do_ref[...] = _dot_nt(dya, wa_ref[...])
        dg_ref[...] = _dot_nt(dyc, wc_ref[...])

    row = lambda i: (i, 0)
    fixed = lambda i: (0, 0)
    return pl.pallas_call(
        body, name="branch_mix_bwd",
        out_shape=[jax.ShapeDtypeStruct((t, d), BF16), jax.ShapeDtypeStruct((t, d), BF16),
                   jax.ShapeDtypeStruct((t, 2 * d), BF16), jax.ShapeDtypeStruct((t, ATTN_W), F32),
                   jax.ShapeDtypeStruct((t, CONV_W), F32)],
        grid=(t // tm,),
        in_specs=[pl.BlockSpec((tm, d), row), pl.BlockSpec((tm, ATTN_W), row), pl.BlockSpec((tm, CONV_W), row),
                  pl.BlockSpec((tm, d), lambda i: (i, ga_col)), pl.BlockSpec((tm, d), lambda i: (i, ga_col + 1)),
                  pl.BlockSpec((d, d), fixed), pl.BlockSpec((ATTN_W, d), fixed), pl.BlockSpec((CONV_W, d), fixed)],
        out_specs=[pl.BlockSpec((tm, d), row), pl.BlockSpec((tm, d), row), pl.BlockSpec((tm, 2 * d), row),
                   pl.BlockSpec((tm, ATTN_W), row), pl.BlockSpec((tm, CONV_W), row)],
        compiler_params=_params(("parallel",), VMEM_BIG),
    )(dmixed, o, g, z, z, w_out, w_ab, w_cb)


def _loss_norm_bwd(h, target, f, g_post, alpha):
    t, d = h.shape
    tm = N_FRONT

    def body(h_ref, t_ref, f_ref, g_ref, dh_ref, df_ref, dg_ref, loss_ref):
        i = pl.program_id(0)

        @pl.when(i == 0)
        def _():
            loss_ref[...] = jnp.zeros_like(loss_ref)
            dg_ref[...] = jnp.zeros_like(dg_ref)

        err = jnp.where(i > 0, h_ref[...] - t_ref[...], 0.0)
        dy = err * (1.0 / d)
        dh_ref[...] = dy
        per_row = jnp.sum(err * err, axis=1, keepdims=True) * (1.0 / d)
        loss_ref[...] += 0.5 * jnp.sum(per_row, axis=0, keepdims=True)
        dx, dg = _rms_bwd(f_ref[...], g_ref[...], dy)
        df_ref[...] = (alpha * dx).astype(BF16)
        dg_ref[...] += alpha * dg

    row = pl.BlockSpec((tm, d), lambda i: (i, 0))
    vec = pl.BlockSpec((1, d), lambda i: (0, 0))
    return pl.pallas_call(
        body, name="loss_and_post_norm_bwd",
        out_shape=[jax.ShapeDtypeStruct((t, d), F32), jax.ShapeDtypeStruct((t, d), BF16),
                   jax.ShapeDtypeStruct((1, d), F32), jax.ShapeDtypeStruct((1, 128), F32)],
        grid=(t // tm,),
        in_specs=[row, pl.BlockSpec((tm, d), lambda i: (jnp.maximum(i - 1, 0), 0)), row, vec],
        out_specs=[row, row, vec, pl.BlockSpec((1, 128), lambda i: (0, 0))],
        compiler_params=_params(("arbitrary",)),
    )(h, target, f, g_post)


def _norm_bwd(name, x, g, dy, alpha):
    t, d = x.shape
    tm = ROW_TILE

    def body(x_ref, g_ref, dy_ref, dx_ref, dg_ref):
        @pl.when(pl.program_id(0) == 0)
        def _():
            dg_ref[...] = jnp.zeros_like(dg_ref)

        dx, dg = _rms_bwd(x_ref[...], g_ref[...], dy_ref[...])
        dx_ref[...] = (alpha * dx).astype(BF16)
        dg_ref[...] += alpha * dg

    row = pl.BlockSpec((tm, d), lambda i: (i, 0))
    vec = pl.BlockSpec((1, d), lambda i: (0, 0))
    return pl.pallas_call(
        body, name=name,
        out_shape=[jax.ShapeDtypeStruct((t, d), BF16), jax.ShapeDtypeStruct((1, d), F32)],
        grid=(t // tm,), in_specs=[row, vec, row], out_specs=[row, vec],
        compiler_params=_params(("arbitrary",)),
    )(x, g, dy)


def _ffn_bwd_mid(name, df, w_out, ab):
    t, d = df.shape
    cw = ab.shape[1] // 4
    tm = ROW_TILE

    def body(df_ref, w_ref, ab_ref, o_ref):
        ds = _dot_nt(df_ref[...], w_ref[...])
        a = ab_ref[:, :cw].astype(F32)
        b = ab_ref[:, cw:].astype(F32)
        sg = _sigmoid(a)
        o_ref[:, :cw] = (ds * b * (sg * (1.0 + a * (1.0 - sg)))).astype(BF16)
        o_ref[:, cw:] = (ds * (a * sg)).astype(BF16)

    return pl.pallas_call(
        body, name=name, out_shape=jax.ShapeDtypeStruct((t, 4 * cw), BF16),
        grid=(2, t // tm),
        in_specs=[pl.BlockSpec((tm, d), lambda j, i: (i, 0)), pl.BlockSpec((cw, d), lambda j, i: (j, 0)),
                  pl.BlockSpec((tm, 2 * cw), lambda j, i: (i, j))],
        out_specs=pl.BlockSpec((tm, 2 * cw), lambda j, i: (i, j)),
        compiler_params=_params(("parallel", "parallel"), VMEM_BIG),
    )(df, w_out, ab)


def _mm_nt_norm_bwd(name, dy, w, h, g, dh_in):
    t, kdim = dy.shape
    d = h.shape[1]
    tm = ROW_TILE // 2
    slots = w.ndim == 3

    def body(dy_ref, w_ref, h_ref, g_ref, dhi_ref, dh_ref, dg_ref):
        @pl.when(pl.program_id(0) == 0)
        def _():
            dg_ref[...] = jnp.zeros_like(dg_ref)

        if slots:
            cw = w_ref.shape[2]
            dn = _dot_nt(dy_ref[:, 0:cw], w_ref[_slot_of(0)])
            for k in range(1, 4):
                dn += _dot_nt(dy_ref[:, k * cw:(k + 1) * cw], w_ref[_slot_of(k)])
        else:
            dn = _dot_nt(dy_ref[...], w_ref[...])
        dx, dg = _rms_bwd(h_ref[...], g_ref[...], dn)
        dh_ref[...] = dhi_ref[...] + dx
        dg_ref[...] += dg

    row = pl.BlockSpec((tm, d), lambda i: (i, 0))
    vec = pl.BlockSpec((1, d), lambda i: (0, 0))
    return pl.pallas_call(
        body, name=name,
        out_shape=[jax.ShapeDtypeStruct((t, d), F32), jax.ShapeDtypeStruct((1, d), F32)],
        grid=(t // tm,),
        in_specs=[pl.BlockSpec((tm, kdim), lambda i: (i, 0)), pl.BlockSpec(w.shape, lambda i: (0,) * w.ndim),
                  row, vec, row],
        out_specs=[row, vec],
        compiler_params=_params(("arbitrary",), VMEM_BIG),
    )(dy, w, h, g, dh_in)


def _gate_bwd(dfq, dfk, z, b_pad, f_col):
    t = z.shape[0]
    tm = ROW_TILE
    nt = t // tm

    def body(dq_ref, dk_ref, z_ref, b_ref, dz_ref, db_ref, carry_ref):
        i = pl.program_id(0)

        @pl.when(i == 0)
        def _():
            carry_ref[...] = jnp.zeros_like(carry_ref)
            db_ref[...] = jnp.zeros_like(db_ref)

        pick = (lax.broadcasted_iota(jnp.int32, (ATTN_W, 128), 0)
                == HEAD_DIM * lax.broadcasted_iota(jnp.int32, (ATTN_W, 128), 1)).astype(F32)
        d_heads = jnp.dot(dq_ref[...] - dk_ref[...], pick, preferred_element_type=F32,
                          precision=lax.Precision.HIGHEST)
        tri = (lax.broadcasted_iota(jnp.int32, (tm, tm), 0) <= lax.broadcasted_iota(jnp.int32, (tm, tm), 1))
        tail = jnp.dot(tri.astype(F32), d_heads, preferred_element_type=F32, precision=lax.Precision.HIGHEST)
        tail = tail + carry_ref[0:1, :]
        carry_ref[...] = jnp.broadcast_to(tail[0:1, :], carry_ref.shape)
        row = (nt - 1 - i) * tm + lax.broadcasted_iota(jnp.int32, (tm, 1), 0)
        dlogit = jnp.where(row >= ROW_PAD, tail * _sigmoid(-(z_ref[...] + b_ref[...])), 0.0)
        dz_ref[...] = jnp.zeros_like(dz_ref)
        dz_ref[:, 0:128] = dlogit.astype(BF16)
        db_ref[...] += jnp.sum(dlogit, axis=0, keepdims=True)

    rev = lambda i: (nt - 1 - i, 0)
    return pl.pallas_call(
        body, name="forget_gate_bwd",
        out_shape=[jax.ShapeDtypeStruct((t, F_PAD), BF16), jax.ShapeDtypeStruct((1, 128), F32)],
        grid=(nt,),
        in_specs=[pl.BlockSpec((tm, ATTN_W), rev), pl.BlockSpec((tm, ATTN_W), rev),
                  pl.BlockSpec((tm, 128), lambda i: (nt - 1 - i, f_col // 128)),
                  pl.BlockSpec((1, 128), lambda i: (0, 0))],
        out_specs=[pl.BlockSpec((tm, F_PAD), rev), pl.BlockSpec((1, 128), lambda i: (0, 0))],
        scratch_shapes=[pltpu.VMEM((8, 128), F32)],
        compiler_params=_params(("arbitrary",)),
    )(dfq, dfk, z, b_pad)


def _ffn_fwd(tag, n, w_in4, w_out, h, g_post, g_next):
    ab, s, s_t = _ffn_in(f"{tag}_in_fwd", n, w_in4)
    outs = _mm_resid_norm(f"{tag}_out_fwd", s, w_out, h, g_post, 0.5, g_next)
    return ab, s_t, outs


def _ffn_bwd_weights(tag, df, ab, s_t, n_t, w_in4, w_out):
    d, cw = w_in4.shape[1], w_in4.shape[2]
    t = df.shape[0]
    dw_out = _weight_grad(f"{tag}_dw_out", s_t, df, d, out_rows=cw // 2)
    dab = _ffn_bwd_mid(f"{tag}_mid_bwd", df, w_out, ab)
    bk = _k_tile(t)
    dw_in = _matmul(
        f"{tag}_dw_in", n_t, dab, jax.ShapeDtypeStruct((4, d, cw), F32), (1, 4, t // bk),
        pl.BlockSpec((d, bk), lambda a, b, k: (0, k)), pl.BlockSpec((bk, cw), lambda a, b, k: (k, b)),
        pl.BlockSpec((None, d, cw), lambda a, b, k: (_slot_of(b), 0, 0)), vmem=VMEM_BIG)
    return dab, dw_in, dw_out


LOSS_ROW = 12


def _pack_small(meta, conv, gains, b_forget, loss=None):
    d = gains[0].shape[1]
    rows = [meta.reshape(4, d), jnp.pad(conv.reshape(1, 3 * 128), ((0, 0), (0, d - 3 * 128)))]
    rows += list(gains) + [jnp.pad(b_forget, ((0, 0), (0, d - HEADS)))]
    last = jnp.zeros((4, d), F32)
    if loss is not None:
        last = jnp.pad(loss.reshape(1, 1), ((0, 3), (0, d - 1)))
    return jnp.concatenate(rows + [last], axis=0)


def _unpack_small(block):
    d = block.shape[1]
    meta = block[0:4].reshape(N_META, d // 4)
    conv = block[4, :3 * 128].reshape(1, 3, 128)
    gains = [block[5 + i:6 + i] for i in range(6)]
    return meta, conv, gains, block[11:12, :HEADS]


def kernel(x, meta_tokens, w_in, b_forget, conv_w, w_attn_branch, w_conv_branch, w_out, g_ffn1_pre, g_ffn1_post, w_ffn1_in, w_ffn1_out, g_mix_pre, g_mix_post, g_ffn2_pre, g_ffn2_post, w_ffn2_in, w_ffn2_out, loss_target, m_meta_tokens, m_w_in, m_b_forget, m_conv_w, m_w_attn_branch, m_w_conv_branch, m_w_out, m_g_ffn1_pre, m_g_ffn1_post, m_w_ffn1_in, m_w_ffn1_out, m_g_mix_pre, m_g_mix_post, m_g_ffn2_pre, m_g_ffn2_post, m_w_ffn2_in, m_w_ffn2_out, v_meta_tokens, v_w_in, v_b_forget, v_conv_w, v_w_attn_branch, v_w_conv_branch, v_w_out, v_g_ffn1_pre, v_g_ffn1_post, v_w_ffn1_in, v_w_ffn1_out, v_g_mix_pre, v_g_mix_post, v_g_ffn2_pre, v_g_ffn2_post, v_w_ffn2_in, v_w_ffn2_out):
    seq, d = x.shape[1], x.shape[2]
    t = seq + N_FRONT
    n_main = 3 * ATTN_W + 3 * CONV_W + 2 * d
    nz = n_main + F_PAD
    f_lo = 3 * ATTN_W
    c_arr = lax.axis_index("c").astype(jnp.int32).reshape(1)

    cs = w_in.shape[2]
    cs_pad = -(-cs // 64) * 64

    def w_in_rows(a):
        return jnp.pad(jnp.transpose(a[0]), ((0, cs_pad - cs), (0, 0)))

    big = [w_in_rows(w_in), w_attn_branch[0], w_conv_branch[0], w_out[0], w_ffn1_in[0], w_ffn1_out[0], w_ffn2_in[0],
           w_ffn2_out[0]]
    small_gather = jnp.concatenate(
        [meta_tokens.reshape(4, d), jnp.pad(conv_w.reshape(1, 3 * 128), ((0, 0), (0, d - 3 * 128))),
         jnp.zeros((11, d), F32)], axis=0)
    w_f1_in4, small4 = _all_gather([big[4].astype(BF16), small_gather])
    (second, rest), small4 = lax.optimization_barrier(
        (([big[5].astype(BF16)], [big[i].astype(BF16) for i in (0, 1, 2, 3, 6, 7)]), small4))
    second_gathered = _all_gather_async("all_gather_ffn1_out", second, 5)
    rest_gathered = _all_gather_async("all_gather_rest", rest, 1)
    meta_full = jnp.transpose(small4[:, 0:4].reshape(4, N_META, d // 4), (1, 0, 2)).reshape(N_META, d)
    conv_full = jnp.transpose(small4[:, 4, :3 * 128].reshape(4, 3, 128), (1, 0, 2)).reshape(3, CONV_W)
    conv_pad = jnp.pad(conv_full, ((0, 5), (0, 0)))
    b_pad = jnp.pad(b_forget, ((0, 0), (0, 128 - HEADS)))

    h0 = jnp.concatenate([jnp.zeros((ROW_PAD, d), F32), meta_full, x[0]], axis=0)
    n1, n1_t = _norm_fwd("ffn1_pre_norm", h0, g_ffn1_pre)
    ab1, s1, s1_t = _ffn_in("ffn1_in_fwd", n1, w_f1_in4)
    w_f1_out = second_gathered(s1, [0])[0].reshape(-1, d)
    f1, h1, u, u_t = _mm_resid_norm("ffn1_out_fwd", s1, w_f1_out, h0, g_ffn1_post, 0.5, g_mix_pre)

    w_in4, w_ab4, w_cb4, w_out4, w_f2_in4, w_f2_out4 = rest_gathered(u, range(6))
    w_in_t = w_in4[:, :cs].reshape(4 * cs, d)
    g_lo = f_lo + HEADS + 3 * CONV_W
    w_in_pad = jnp.concatenate(
        [w_in_t[:f_lo], w_in_t[g_lo:], w_in_t[f_lo + HEADS:g_lo], w_in_t[f_lo:f_lo + HEADS],
         jnp.zeros((F_PAD - HEADS, d), BF16)], axis=0)
    w_ab = jnp.transpose(w_ab4, (1, 0, 2)).reshape(ATTN_W, d)
    w_cb = jnp.transpose(w_cb4, (1, 0, 2)).reshape(CONV_W, d)
    w_out_full = w_out4.reshape(d, d)
    w_f2_out = w_f2_out4.reshape(-1, d)
    qkv, z = _in_proj(u, w_in_pad)
    f_col = z.shape[1] - F_PAD
    f_cum = _gate_prep(z, b_pad, f_col)
    f_heads = f_cum[:, :HEADS]
    o, lse = _attn_fwd(qkv, *_attn_bias_operands(f_heads))
    g, g_t = _conv_gate(z, conv_pad)
    mp, mp_t, o_t = _branch_mix(z, o, g, w_ab, w_cb, d)
    mixed, h2, n2, n2_t = _mm_resid_norm("mix_out_fwd", mp, w_out_full, h1, g_mix_post, 1.0, g_ffn2_pre)
    ab2, s2_t, (f2, h3) = _ffn_fwd("ffn2", n2, w_f2_in4, w_f2_out, h2, g_ffn2_post, None)
    dh3, df2, dg_f2_post, loss_part = _loss_norm_bwd(h3, loss_target[0], f2, g_ffn2_post, 0.5)

    reduced = {}

    def reduce_scatter(label, tags, slots, sequencer_id, hold=None, got=None, after=None):
        if got is None:
            got = _pair_send_halves(f"grad_pair_exchange_{label}", slots)
        else:
            got, _ = lax.optimization_barrier((got, after))
        sums = [_pair_add(tag, s, a, c_arr, F32 if tag == "small" else BF16) for tag, s, a in zip(tags, slots, got)]
        sums, hold = lax.optimization_barrier((sums, hold))
        if sequencer_id is None:
            arrived = _chip_scatter(f"grad_chip_scatter_{label}", sums)
        else:
            arrived = _chip_scatter_async(f"grad_chip_scatter_{label}", sums, sequencer_id)
        mine = [_chip_add(tag, a) for tag, a in zip(tags, arrived)]
        reduced.update(zip(tags, zip(mine, _pair_swap(f"grad_pair_swap_{label}", mine))))
        return hold

    dab2, dw_f2_in, dw_f2_out = _ffn_bwd_weights("ffn2", df2, ab2, s2_t, n2_t, w_f2_in4, w_f2_out)
    ffn2_slots = [dw_f2_in, dw_f2_out.reshape(4, -1, d)]
    ffn2_got = _pair_send_halves_async("grad_pair_exchange_ffn2", ffn2_slots, 6)
    dh2, dg_f2_pre = _mm_nt_norm_bwd("ffn2_in_bwd", dab2, w_f2_in4, h2, g_ffn2_pre, dh3)
    reduce_scatter("ffn2", ["w_ffn2_in", "w_ffn2_out"], ffn2_slots, 2, got=ffn2_got, after=dh2)
    dmixed, dg_mix_post = _norm_bwd("mix_post_norm_bwd", mixed, g_mix_post, dh2, 1.0)
    dw_out = _weight_grad("mix_dw_out", mp_t, dmixed, d)
    dya, dyc, dgates, do, dgconv = _branch_bwd(z, o, g, dmixed, w_out_full, w_ab, w_cb, d)
    dw_ab = _weight_grad("mix_dw_attn_branch", o_t, dya, d)
    dw_cb = _weight_grad("mix_dw_conv_branch", g_t, dyc, d)
    dz_conv, dconv_w = _conv_bwd(z, dgconv, conv_pad)
    front = lax.broadcasted_iota(jnp.int32, (t, 1), 0) < ROW_PAD
    lse_heads = jnp.where(front, 1e9, lse[:, ::HEAD_DIM])
    dq, dk, dv, dfk, dfq = _attn_bwd(qkv, *_attn_bias_operands(f_heads, lse_heads), o, do)
    dz_f, db_forget = _gate_bwd(dfq, dfk, z, b_pad, f_col)
    dz_pieces = {"q": dq, "k": dk, "v": dv, "gates": dgates, "conv": dz_conv, "f": dz_f}
    dh1, dg_mix_pre = _mix_in_bwd(list(dz_pieces.values()), w_in_pad, h1, g_mix_pre, dh2)
    dw_t = {name: _weight_grad_t(f"mix_dw_in_{name}", u_t, piece) for name, piece in dz_pieces.items()}
    dw_in_t = jnp.concatenate(
        [dw_t["q"], dw_t["k"], dw_t["v"], dw_t["f"][:HEADS], dw_t["conv"], dw_t["gates"]], axis=0)
    mix_slots = [jnp.pad(dw_in_t.reshape(4, cs, d), ((0, 0), (0, cs_pad - cs), (0, 0))),
                 jnp.transpose(dw_ab.reshape(ATTN_W, 4, d // 4), (1, 0, 2)),
                 jnp.transpose(dw_cb.reshape(CONV_W, 4, d // 4), (1, 0, 2)),
                 dw_out.reshape(4, d // 4, d)]
    mix_got = _pair_send_halves_async("grad_pair_exchange_mix", mix_slots, 7)
    df1, dg_f1_post = _norm_bwd("ffn1_post_norm_bwd", f1, g_ffn1_post, dh1, 0.5)
    reduce_scatter("mix", ["w_in", "w_attn_branch", "w_conv_branch", "w_out"], mix_slots, 3, got=mix_got, after=df1)
    dab1, dw_f1_in, dw_f1_out = _ffn_bwd_weights("ffn1", df1, ab1, s1_t, n1_t, w_f1_in4, w_f1_out)
    dab1 = reduce_scatter("ffn1", ["w_ffn1_in", "w_ffn1_out"], [dw_f1_in, dw_f1_out.reshape(4, -1, d)], 4, dab1)
    dh0, dg_f1_pre = _mm_nt_norm_bwd("ffn1_in_bwd", dab1, w_f1_in4, h0, g_ffn1_pre, dh1)
    grad_x = dh0[N_FRONT:][None]
    dmeta = dh0[ROW_PAD:N_FRONT]
    small_grad = jnp.stack([
        _pack_small(dmeta[:, j * (d // 4):(j + 1) * (d // 4)], dconv_w[:3, j * 128:(j + 1) * 128],
                    [dg_f1_pre, dg_f1_post, dg_mix_pre, dg_mix_post, dg_f2_pre, dg_f2_post], db_forget[:, :HEADS],
                    loss_part[0, 0])
        for j in range(4)])
    reduce_scatter("small", ["small"], [small_grad], None)
    tags =["w_in", "w_attn_branch", "w_conv_branch", "w_out", "w_ffn1_in", "w_ffn1_out", "w_ffn2_in", "w_ffn2_out", "small"]
    halves = [reduced[tag][0] for tag in tags]
    others = [reduced[tag][1] for tag in tags]

    small = [g_ffn1_pre, g_ffn1_post, g_mix_pre, g_mix_post, g_ffn2_pre, g_ffn2_post]
    small_m = [m_g_ffn1_pre, m_g_ffn1_post, m_g_mix_pre, m_g_mix_post, m_g_ffn2_pre, m_g_ffn2_post]
    small_v = [v_g_ffn1_pre, v_g_ffn1_post, v_g_mix_pre, v_g_mix_post, v_g_ffn2_pre, v_g_ffn2_post]
    ws = big + [_pack_small(meta_tokens, conv_w[0], small, b_forget)]
    ms = [w_in_rows(m_w_in), m_w_attn_branch[0], m_w_conv_branch[0], m_w_out[0], m_w_ffn1_in[0], m_w_ffn1_out[0],
          m_w_ffn2_in[0], m_w_ffn2_out[0], _pack_small(m_meta_tokens, m_conv_w[0], small_m, m_b_forget)]
    vs = [w_in_rows(v_w_in), v_w_attn_branch[0], v_w_conv_branch[0], v_w_out[0], v_w_ffn1_in[0], v_w_ffn1_out[0],
          v_w_ffn2_in[0], v_w_ffn2_out[0], _pack_small(v_meta_tokens, v_conv_w[0], small_v, v_b_forget)]
    updates = [_adamw(tag, w, a, b, m, v, c_arr) for tag, w, a, b, m, v in zip(tags, ws, halves, others, ms, vs)]

    def leaves(big_vals, small_block):
        meta, conv, gains, bf = _unpack_small(small_block)
        w_in_t_, w_ab_, w_cb_, w_out_, f1_in, f1_out, f2_in, f2_out = [b[None] for b in big_vals]
        w_in_ = jnp.transpose(w_in_t_[:, :cs], (0, 2, 1))
        return [meta, w_in_, bf, conv, w_ab_, w_cb_, w_out_, gains[0], gains[1], f1_in, f1_out,
                gains[2], gains[3], gains[4], gains[5], f2_in, f2_out]

    out_g, out_d, out_m, out_v = [leaves([u_[k] for u_ in updates[:8]], updates[8][k]) for k in range(4)]
    loss = updates[8][0][LOSS_ROW, 0]
    return (loss, grad_x, *out_g, *out_d, *out_m, *out_v)
```

```python
import functools

import jax
import jax.numpy as jnp
from jax import lax
from jax.experimental import pallas as pl
from jax.experimental.pallas import tpu as pltpu
from jax.experimental.pallas import tpu_sc as plsc

N_META = 16
ROW_PAD = 112
N_FRONT = ROW_PAD + N_META
HEADS = 8
HEAD_DIM = 64
ATTN_W = HEADS * HEAD_DIM
CONV_W = 512
NORM_EPS = 1e-6
ROW_TILE = 640
F_PAD = 128
ATTN_KV_GROUP = 4
NEG = -1e30
ADAM_LR = 0.001
ADAM_B1 = 0.9
ADAM_B2 = 0.999
ADAM_EPS = 1e-08
ADAM_WD = 0.01
ADAM_STEP = 10
VMEM_BIG = 56 * 1024 * 1024
MESH = pl.DeviceIdType.MESH
ANY = pl.BlockSpec(memory_space=pl.ANY)
F32 = jnp.float32
BF16 = jnp.bfloat16


def _params(sem, vmem=None):
    return pltpu.CompilerParams(dimension_semantics=sem, vmem_limit_bytes=vmem)


def _sigmoid(x):
    return 1.0 / (1.0 + jnp.exp(-x))


def _rstd(x):
    return lax.rsqrt(jnp.mean(x * x, axis=-1, keepdims=True) + NORM_EPS)


def _rms_bwd(x, g, dy):
    r = _rstd(x)
    xr = x * r
    gdy = g * dy
    dx = r * (gdy - xr * jnp.mean(xr * gdy, axis=-1, keepdims=True))
    return dx, jnp.sum(dy * xr, axis=0, keepdims=True)


def _dot(a, b):
    return jnp.dot(a, b, preferred_element_type=F32)


def _dot_nt(a, b):
    return lax.dot_general(a, b, (((1,), (1,)), ((), ())), preferred_element_type=F32)


def _k_tile(t):
    return 1664 if t % 1664 == 0 else ROW_TILE


def _place():
    x, y, c = lax.axis_index("x"), lax.axis_index("y"), lax.axis_index("c")
    chips = [(1 - x, y), (x, 1 - y), (1 - x, 1 - y)]
    return x, y, c, chips


def _all_gather(shards):
    n = len(shards)
    split = [s.reshape(2, s.shape[0] // 2, s.shape[1]) for s in shards]

    def body(*refs):
        ins, outs = refs[:n], refs[n:2 * n]
        send_sems, recv_sems = refs[2 * n:]
        x, y, c, chips = _place()
        me = 2 * x + y
        sibling = (x, y, 1 - c)

        def remote(i, k, slot, part, to, src=None):
            dst = outs[i].at[slot, part]
            return pltpu.make_async_remote_copy(
                src_ref=dst if src is None else src, dst_ref=dst,
                send_sem=send_sems.at[i, k], recv_sem=recv_sems.at[i, k],
                device_id=to, device_id_type=MESH)

        started = []
        for i in range(n):
            for k, (cx, cy) in enumerate(chips):
                cp = remote(i, k, me, c, (cx, cy, c), src=ins[i].at[c])
                cp.start()
                started.append(cp)
        for i in range(n):
            for k, (cx, cy) in enumerate(chips):
                remote(i, k, 2 * cx + cy, c, (x, y, c)).wait_recv()
                cp = remote(i, 3 + k, 2 * cx + cy, c, sibling)
                cp.start()
                started.append(cp)
        for i in range(n):
            for k, (cx, cy) in enumerate(chips):
                remote(i, 3 + k, 2 * cx + cy, 1 - c, (x, y, c)).wait_recv()
        for cp in started:
            cp.wait_send()

    outs = pl.pallas_call(
        body, name="all_gather_weights",
        out_shape=[jax.ShapeDtypeStruct((4,) + s.shape, s.dtype) for s in split],
        in_specs=[ANY] * n, out_specs=[ANY] * n,
        scratch_shapes=[pltpu.SemaphoreType.DMA((n, 6)), pltpu.SemaphoreType.DMA((n, 6))],
    )(*split)
    me =2 * lax.axis_index("x") + lax.axis_index("y")
    outs = [lax.dynamic_update_slice(o, s[None], (me, 0, 0, 0)) for o, s in zip(outs, split)]
    return [o.reshape((4,) + s.shape) for o, s in zip(outs, shards)]


def _all_gather_async(name, shards, collective_id):
    n = len(shards)
    split = [s.reshape(2, s.shape[0] // 2, s.shape[1]) for s in shards]
    ins = [jax.new_ref(s, memory_space=pltpu.MemorySpace.HBM) for s in split]
    outs = [jax.empty_ref(jax.ShapeDtypeStruct((4,) + s.shape, s.dtype), memory_space=pltpu.MemorySpace.HBM)
            for s in split]

    @pl.kernel(mesh=plsc.ScalarSubcoreMesh(axis_name="sequencer", num_cores=1), name=name,
               scratch_types=(pltpu.SemaphoreType.DMA((n, 6)), pltpu.SemaphoreType.DMA((n, 6))),
               compiler_params=pltpu.CompilerParams(collective_id=collective_id))
    def launch(send_sems, recv_sems):
        x, y, c, chips = _place()
        me = 2 * x + y
        sibling = (x, y, 1 - c)
        barrier = pltpu.get_barrier_semaphore()
        for peer in [(cx, cy, c) for cx, cy in chips] + [sibling]:
            pl.semaphore_signal(barrier, inc=1, device_id=peer, device_id_type=MESH)
        pl.semaphore_wait(barrier, 4)

        def remote(i, k, slot, part, to, src=None):
            dst = outs[i].at[slot, part]
            return pltpu.make_async_remote_copy(
                src_ref=dst if src is None else src, dst_ref=dst,
                send_sem=send_sems.at[i, k], recv_sem=recv_sems.at[i, k],
                device_id=to, device_id_type=MESH)

        started = []
        for i in range(n):
            for k, (cx, cy) in enumerate(chips):
                cp = remote(i, k, me, c, (cx, cy, c), src=ins[i].at[c])
                cp.start()
                started.append(cp)
        for i in range(n):
            for k, (cx, cy) in enumerate(chips):
                remote(i, k, 2 * cx + cy, c, (x, y, c)).wait_recv()
                cp = remote(i, 3 + k, 2 * cx + cy, c, sibling)
                cp.start()
                started.append(cp)
        for i in range(n):
            for k, (cx, cy) in enumerate(chips):
                remote(i, 3 + k, 2 * cx + cy, 1 - c, (x, y, c)).wait_recv()
        for cp in started:
            cp.wait_send()

    launch()
    raw = [o[...] for o in outs]

    def finish(after, which):
        arrived, _ = lax.optimization_barrier(([raw[i] for i in which], after))
        me = 2 * lax.axis_index("x") + lax.axis_index("y")
        gathered = [lax.dynamic_update_slice(a, split[i][None], (me, 0, 0, 0)) for a, i in zip(arrived, which)]
        return [g.reshape((4,) + shards[i].shape) for g, i in zip(gathered, which)]

    return finish


def _pair_send_halves(name, grads):
    n = len(grads)

    def body(*refs):
        ins, outs = refs[:n], refs[n:2 * n]
        send_sems, recv_sems = refs[2 * n:]
        x, y, c, _ = _place()
        cps = []
        for i in range(n):
            half = ins[i].shape[1] // 2
            cp = pltpu.make_async_remote_copy(
                src_ref=ins[i].at[:, pl.ds((1 - c) * half, half)], dst_ref=outs[i],
                send_sem=send_sems.at[i], recv_sem=recv_sems.at[i],
                device_id=(x, y, 1 - c), device_id_type=MESH)
            cp.start()
            cps.append(cp)
        for cp in cps:
            cp.wait()

    return pl.pallas_call(
        body, name=name,
        out_shape=[jax.ShapeDtypeStruct((4, g.shape[1] // 2, g.shape[2]), g.dtype) for g in grads],
        in_specs=[ANY] * n, out_specs=[ANY] * n,
        scratch_shapes=[pltpu.SemaphoreType.DMA((n,)), pltpu.SemaphoreType.DMA((n,))],
    )(*grads)


def _pair_send_halves_async(name, grads, collective_id):
    n = len(grads)
    ins = [jax.new_ref(g, memory_space=pltpu.MemorySpace.HBM) for g in grads]
    outs = [jax.empty_ref(jax.ShapeDtypeStruct((4, g.shape[1] // 2, g.shape[2]), g.dtype),
                          memory_space=pltpu.MemorySpace.HBM) for g in grads]

    @pl.kernel(mesh=plsc.ScalarSubcoreMesh(axis_name="sequencer", num_cores=1), name=name,
               scratch_types=(pltpu.SemaphoreType.DMA((n,)), pltpu.SemaphoreType.DMA((n,))),
               compiler_params=pltpu.CompilerParams(collective_id=collective_id))
    def launch(send_sems, recv_sems):
        x, y, c, _ = _place()
        barrier = pltpu.get_barrier_semaphore()
        pl.semaphore_signal(barrier, inc=1, device_id=(x, y, 1 - c), device_id_type=MESH)
        pl.semaphore_wait(barrier, 1)
        cps = []
        for i in range(n):
            half = ins[i].shape[1] // 2
            cp = pltpu.make_async_remote_copy(
                src_ref=ins[i].at[:, pl.ds((1 - c) * half, half)], dst_ref=outs[i],
                send_sem=send_sems.at[i], recv_sem=recv_sems.at[i],
                device_id=(x, y, 1 - c), device_id_type=MESH)
            cp.start()
            cps.append(cp)
        for cp in cps:
            cp.wait()

    launch()
    return [o[...] for o in outs]


def _chip_scatter(name, parts):
    n = len(parts)

    def body(*refs):
        _scatter_copies(refs[:n], refs[n:2 * n], *refs[2 * n:])

    arrived = pl.pallas_call(
        body, name=name,
        out_shape=[jax.ShapeDtypeStruct(p.shape, p.dtype) for p in parts],
        in_specs=[ANY] * n, out_specs=[ANY] * n,
        scratch_shapes=[pltpu.SemaphoreType.DMA((n, 3)), pltpu.SemaphoreType.DMA((n, 3))],
    )(*parts)
    return _own_slots(parts, arrived)


def _scatter_copies(ins, outs, send_sems, recv_sems):
    x, y, c, chips = _place()
    me = 2 * x + y
    sends = []
    for i in range(len(ins)):
        for k, (cx, cy) in enumerate(chips):
            cp = pltpu.make_async_remote_copy(
                src_ref=ins[i].at[2 * cx + cy], dst_ref=outs[i].at[me],
                send_sem=send_sems.at[i, k], recv_sem=recv_sems.at[i, k],
                device_id=(cx, cy, c), device_id_type=MESH)
            cp.start()
            sends.append(cp)
    for i in range(len(ins)):
        for k, (cx, cy) in enumerate(chips):
            got = outs[i].at[2 * cx + cy]
            pltpu.make_async_remote_copy(
                src_ref=got, dst_ref=got, send_sem=send_sems.at[i, k], recv_sem=recv_sems.at[i, k],
                device_id=(x, y, c), device_id_type=MESH).wait_recv()
    for cp in sends:
        cp.wait_send()


def _own_slots(parts, arrived):
    me = 2 * lax.axis_index("x") + lax.axis_index("y")
    return [lax.dynamic_update_slice(a, lax.dynamic_slice_in_dim(p, me, 1, axis=0), (me, 0, 0))
            for p, a in zip(parts, arrived)]


def _chip_scatter_async(name, parts, collective_id):
    n = len(parts)
    ins = [jax.new_ref(p, memory_space=pltpu.MemorySpace.HBM) for p in parts]
    outs = [jax.empty_ref(jax.ShapeDtypeStruct(p.shape, p.dtype), memory_space=pltpu.MemorySpace.HBM) for p in parts]

    @pl.kernel(mesh=plsc.ScalarSubcoreMesh(axis_name="sequencer", num_cores=1), name=name,
               scratch_types=(pltpu.SemaphoreType.DMA((n, 3)), pltpu.SemaphoreType.DMA((n, 3))),
               compiler_params=pltpu.CompilerParams(collective_id=collective_id))
    def launch(send_sems, recv_sems):
        x, y, c, chips = _place()
        barrier = pltpu.get_barrier_semaphore()
        for cx, cy in chips:
            pl.semaphore_signal(barrier, inc=1, device_id=(cx, cy, c), device_id_type=MESH)
        pl.semaphore_wait(barrier, 3)
        _scatter_copies(ins, outs, send_sems, recv_sems)

    launch()
    return _own_slots(parts, [o[...] for o in outs])


def _pair_swap(name, halves):
    n = len(halves)

    def body(*refs):
        ins, outs = refs[:n], refs[n:2 * n]
        send_sems, recv_sems = refs[2 * n:]
        x, y, c, _ = _place()
        cps = []
        for i in range(n):
            cp = pltpu.make_async_remote_copy(
                src_ref=ins[i], dst_ref=outs[i], send_sem=send_sems.at[i], recv_sem=recv_sems.at[i],
                device_id=(x, y, 1 - c), device_id_type=MESH)
            cp.start()
            cps.append(cp)
        for cp in cps:
            cp.wait()

    return pl.pallas_call(
        body, name=name,
        out_shape=[jax.ShapeDtypeStruct(h.shape, h.dtype) for h in halves],
        in_specs=[ANY] * n, out_specs=[ANY] * n,
        scratch_shapes=[pltpu.SemaphoreType.DMA((n,)), pltpu.SemaphoreType.DMA((n,))],
    )(*halves)


def _row_block(rows, cols, n_bufs, budget=20 * 1024 * 1024):
    best = min(rows, 16)
    for b in range(16, rows + 1, 16):
        if rows % b == 0 and 2 * n_bufs * b * cols * 4 <= budget:
            best = b
    return best


def _pair_add(tag, grad, got, c_arr, out_dtype):
    _, rows, cols = grad.shape
    half = rows // 2
    bh = _row_block(half, cols, 3)
    nb = half // bh

    def body(c_ref, g_ref, a_ref, o_ref):
        o_ref[...] = (g_ref[...] + a_ref[...]).astype(out_dtype)

    return pl.pallas_call(
        body, name=f"pair_add_{tag}",
        out_shape=jax.ShapeDtypeStruct((4, half, cols), out_dtype),
        grid_spec=pltpu.PrefetchScalarGridSpec(
            num_scalar_prefetch=1, grid=(4, nb),
            in_specs=[pl.BlockSpec((None, bh, cols), lambda j, r, c: (j, c[0] * nb + r, 0)),
                      pl.BlockSpec((None, bh, cols), lambda j, r, c: (j, r, 0))],
            out_specs=pl.BlockSpec((None, bh, cols), lambda j, r, c: (j, r, 0))),
        compiler_params=_params(("parallel", "parallel")),
    )(c_arr, grad, got)


def _chip_add(tag, parts):
    _, half, cols = parts.shape
    bh = _row_block(half, cols, 5)

    def body(p_ref, o_ref):
        a, b, c, d = [p_ref[j].astype(F32) for j in range(4)]
        o_ref[...] = ((a + b) + c) + d

    return pl.pallas_call(
        body, name=f"chip_add_{tag}",
        out_shape=jax.ShapeDtypeStruct((half, cols), F32),
        grid=(half // bh,),
        in_specs=[pl.BlockSpec((4, bh, cols), lambda r: (0, r, 0))],
        out_specs=pl.BlockSpec((bh, cols), lambda r: (r, 0)),
        compiler_params=_params(("parallel",)),
    )(parts)


def _adamw(tag, w, mine, theirs, m, v, c_arr):
    rows, cols = w.shape
    half = rows // 2
    br = _row_block(half, cols, 9)
    nb = half // br

    def body(c_ref, w_ref, a_ref, b_ref, m_ref, v_ref, g_ref, d_ref, mo_ref, vo_ref):
        own = (pl.program_id(0) // nb) == c_ref[0]
        g = jnp.where(own, a_ref[...], b_ref[...])
        g_ref[...] = g
        m_new = ADAM_B1 * m_ref[...] + (1.0 - ADAM_B1) * g
        v_new = ADAM_B2 * v_ref[...] + (1.0 - ADAM_B2) * (g * g)
        m_hat = m_new / (1.0 - ADAM_B1 ** ADAM_STEP)
        v_hat = v_new / (1.0 - ADAM_B2 ** ADAM_STEP)
        d_ref[...] = -ADAM_LR * (m_hat / (jnp.sqrt(v_hat) + ADAM_EPS) + ADAM_WD * w_ref[...])
        mo_ref[...] = m_new
        vo_ref[...] = v_new

    spec = pl.BlockSpec((br, cols), lambda r, c: (r, 0))
    mine_spec = pl.BlockSpec((br, cols), lambda r, c: (jnp.clip(r - c[0] * nb, 0, nb - 1), 0))
    theirs_spec = pl.BlockSpec((br, cols), lambda r, c: (jnp.clip(r - (1 - c[0]) * nb, 0, nb - 1), 0))
    return pl.pallas_call(
        body, name=f"adamw_{tag}",
        out_shape=[jax.ShapeDtypeStruct((rows, cols), F32)] * 4,
        grid_spec=pltpu.PrefetchScalarGridSpec(
            num_scalar_prefetch=1, grid=(rows // br,),
            in_specs=[spec, mine_spec, theirs_spec, spec, spec], out_specs=[spec] * 4),
        compiler_params=_params(("arbitrary",)),
    )(c_arr, w, mine, theirs, m, v)


def _matmul(name, x, w, out_shape, grid, x_spec, w_spec, o_spec, *, nt=False, vmem=None):
    nk = grid[2]
    acc_shape = tuple(d for d in o_spec.block_shape if d is not None)

    def body(x_ref, w_ref, o_ref, acc_ref):
        k = pl.program_id(2)
        part = _dot_nt(x_ref[...], w_ref[...]) if nt else _dot(x_ref[...], w_ref[...])
        if nk == 1:
            o_ref[...] = part.astype(o_ref.dtype)
        else:
            @pl.when(k == 0)
            def _():
                acc_ref[...] = part

            @pl.when(k > 0)
            def _():
                acc_ref[...] += part

            @pl.when(k == nk - 1)
            def _():
                o_ref[...] = acc_ref[...].astype(o_ref.dtype)

    return pl.pallas_call(
        body, name=name, out_shape=out_shape, grid=grid,
        in_specs=[x_spec, w_spec], out_specs=o_spec,
        scratch_shapes=[pltpu.VMEM(acc_shape if nk > 1 else (8, 128), F32)],
        compiler_params=_params(("parallel", "parallel", "arbitrary"), vmem),
    )(x, w)


def _weight_grad(name, xt, dy, bn, out_rows=None):
    m, t = xt.shape
    n = dy.shape[1]
    bm = m if out_rows is None else out_rows
    bk = _k_tile(t)
    return _matmul(
        name, xt, dy, jax.ShapeDtypeStruct((m, n), F32), (m // bm, n // bn, t // bk),
        pl.BlockSpec((bm, bk), lambda a, b, k: (a, k)),
        pl.BlockSpec((bk, bn), lambda a, b, k: (k, b)),
        pl.BlockSpec((bm, bn), lambda a, b, k: (a, b)), vmem=VMEM_BIG)


def _weight_grad_t(name, xt, dy):
    m, t = xt.shape
    n = dy.shape[1]
    bn = min(n, 512)
    bk = _k_tile(t)
    nk = t // bk

    def body(x_ref, dy_ref, o_ref, acc_ref):
        k = pl.program_id(1)
        part = _dot(x_ref[...], dy_ref[...].astype(BF16))

        @pl.when(k == 0)
        def _():
            acc_ref[...] = part

        @pl.when(k > 0)
        def _():
            acc_ref[...] += part

        @pl.when(k == nk - 1)
        def _():
            o_ref[...] = acc_ref[...].T

    return pl.pallas_call(
        body, name=name, out_shape=jax.ShapeDtypeStruct((n, m), F32), grid=(n // bn, nk),
        in_specs=[pl.BlockSpec((m, bk), lambda b, k: (0, k)), pl.BlockSpec((bk, bn), lambda b, k: (k, b))],
        out_specs=pl.BlockSpec((bn, m), lambda b, k: (b, 0)),
        scratch_shapes=[pltpu.VMEM((m, bn), F32)],
        compiler_params=_params(("parallel", "arbitrary"), VMEM_BIG),
    )(xt, dy)


def _mix_in_bwd(pieces, wt, h, g, dh_in):
    t, d = h.shape
    tm = ROW_TILE // 2
    widths = [p.shape[1] for p in pieces]
    n = len(pieces)

    def body(*refs):
        dy_refs, (w_ref, h_ref, g_ref, dhi_ref, dh_ref, dg_ref) = refs[:n], refs[n:]

        @pl.when(pl.program_id(0) == 0)
        def _():
            dg_ref[...] = jnp.zeros_like(dg_ref)

        dn, off = None, 0
        for dy_ref, wd in zip(dy_refs, widths):
            part = _dot(dy_ref[...].astype(BF16), w_ref[off:off + wd, :])
            dn = part if dn is None else dn + part
            off += wd
        dx, dg = _rms_bwd(h_ref[...], g_ref[...], dn)
        dh_ref[...] = dhi_ref[...] + dx
        dg_ref[...] += dg

    row = pl.BlockSpec((tm, d), lambda i: (i, 0))
    vec = pl.BlockSpec((1, d), lambda i: (0, 0))
    return pl.pallas_call(
        body, name="mix_in_bwd",
        out_shape=[jax.ShapeDtypeStruct((t, d), F32), jax.ShapeDtypeStruct((1, d), F32)],
        grid=(t // tm,),
        in_specs=[pl.BlockSpec((tm, wd), lambda i: (i, 0)) for wd in widths]
        + [pl.BlockSpec(wt.shape, lambda i: (0, 0)), row, vec, row],
        out_specs=[row, vec],
        compiler_params=_params(("arbitrary",), VMEM_BIG),
    )(*pieces, wt, h, g, dh_in)


def _norm_fwd(name, h, g):
    t, d = h.shape
    tm = ROW_TILE

    def body(h_ref, g_ref, n_ref, nt_ref):
        x = h_ref[...]
        y = x * _rstd(x) * g_ref[...]
        n_ref[...] = y.astype(BF16)
        nt_ref[...] = y.T.astype(BF16)

    return pl.pallas_call(
        body, name=name,
        out_shape=[jax.ShapeDtypeStruct((t, d), BF16), jax.ShapeDtypeStruct((d, t), BF16)],
        grid=(t // tm,),
        in_specs=[pl.BlockSpec((tm, d), lambda i: (i, 0)), pl.BlockSpec((1, d), lambda i: (0, 0))],
        out_specs=[pl.BlockSpec((tm, d), lambda i: (i, 0)), pl.BlockSpec((d, tm), lambda i: (0, i))],
        compiler_params=_params(("parallel",)),
    )(h, g)


def _slot_of(kk):
    return (kk % 2) * 2 + kk // 2


def _ffn_in(name, n, w4):
    t, d = n.shape
    cw = w4.shape[2]
    tm = ROW_TILE

    def body(x_ref, wg_ref, wu_ref, ab_ref, s_ref, st_ref):
        x = x_ref[...]
        a = _dot(x, wg_ref[...])
        b = _dot(x, wu_ref[...])
        ab_ref[:, :cw] = a.astype(BF16)
        ab_ref[:, cw:] = b.astype(BF16)
        s = a * _sigmoid(a) * b
        s_ref[...] = s.astype(BF16)
        st_ref[...] = s.T.astype(BF16)

    return pl.pallas_call(
        body, name=name,
        out_shape=[jax.ShapeDtypeStruct((t, 4 * cw), BF16), jax.ShapeDtypeStruct((t, 2 * cw), BF16),
                   jax.ShapeDtypeStruct((2 * cw, t), BF16)],
        grid=(2, t // tm),
        in_specs=[pl.BlockSpec((tm, d), lambda j, i: (i, 0)),
                  pl.BlockSpec((None, d, cw), lambda j, i: (j, 0, 0)),
                  pl.BlockSpec((None, d, cw), lambda j, i: (2 + j, 0, 0))],
        out_specs=[pl.BlockSpec((tm, 2 * cw), lambda j, i: (i, j)),
                   pl.BlockSpec((tm, cw), lambda j, i: (i, j)),
                   pl.BlockSpec((cw, tm), lambda j, i: (j, i))],
        compiler_params=_params(("parallel", "parallel"), VMEM_BIG),
    )(n, w4, w4)


def _mm_resid_norm(name, x, w, h, g_post, alpha, g_next):
    t, kdim = x.shape
    d = w.shape[1]
    tm = ROW_TILE
    with_next = g_next is not None

    def body(x_ref, w_ref, h_ref, gp_ref, gn_ref, f_ref, hn_ref, *rest):
        f = _dot(x_ref[...], w_ref[...])
        f_ref[...] = f
        hn = h_ref[...] + alpha * (f * _rstd(f) * gp_ref[...])
        hn_ref[...] = hn
        if with_next:
            y = hn * _rstd(hn) * gn_ref[...]
            rest[0][...] = y.astype(BF16)
            rest[1][...] = y.T.astype(BF16)

    row = lambda i: (i, 0)
    vec = pl.BlockSpec((1, d), lambda i: (0, 0))
    out_shape = [jax.ShapeDtypeStruct((t, d), F32), jax.ShapeDtypeStruct((t, d), F32)]
    out_specs = [pl.BlockSpec((tm, d), row), pl.BlockSpec((tm, d), row)]
    if with_next:
        out_shape += [jax.ShapeDtypeStruct((t, d), BF16), jax.ShapeDtypeStruct((d, t), BF16)]
        out_specs += [pl.BlockSpec((tm, d), row), pl.BlockSpec((d, tm), lambda i: (0, i))]
    return pl.pallas_call(
        body, name=name, out_shape=out_shape, grid=(t // tm,),
        in_specs=[pl.BlockSpec((tm, kdim), row), pl.BlockSpec((kdim, d), lambda i: (0, 0)),
                  pl.BlockSpec((tm, d), row), vec, vec],
        out_specs=out_specs,
        compiler_params=_params(("parallel",), VMEM_BIG),
    )(x, w, h, g_post, g_post if g_next is None else g_next)


def _in_proj(u, w):
    t, d = u.shape
    nz = w.shape[0]
    nq = 3 * ATTN_W
    tm = ROW_TILE // 2

    def body(u_ref, w_ref, qkv_ref, z_ref):
        qkv_ref[...] = _dot_nt(u_ref[...], w_ref[0:nq, :]).astype(BF16)
        z_ref[...] = _dot_nt(u_ref[...], w_ref[nq:, :])

    return pl.pallas_call(
        body, name="mix_in_proj",
        out_shape=[jax.ShapeDtypeStruct((t, nq), BF16), jax.ShapeDtypeStruct((t, nz - nq), F32)],
        grid=(t // tm,),
        in_specs=[pl.BlockSpec((tm, d), lambda i: (i, 0)), pl.BlockSpec((nz, d), lambda i: (0, 0))],
        out_specs=[pl.BlockSpec((tm, nq), lambda i: (i, 0)), pl.BlockSpec((tm, nz - nq), lambda i: (i, 0))],
        compiler_params=_params(("parallel",), VMEM_BIG),
    )(u, w)


def _gate_prep(z, b_pad, f_col):
    t = z.shape[0]
    tm = ROW_TILE

    def body(z_ref, b_ref, f_ref, carry_ref):
        i = pl.program_id(0)

        @pl.when(i == 0)
        def _():
            carry_ref[...] = jnp.zeros_like(carry_ref)

        xs = z_ref[...] + b_ref[...]
        logf = jnp.minimum(xs, 0.0) - jnp.log(1.0 + jnp.exp(-jnp.abs(xs)))
        row = i * tm + lax.broadcasted_iota(jnp.int32, (tm, 1), 0)
        logf = jnp.where(row >= ROW_PAD, logf, 0.0)
        tri = (lax.broadcasted_iota(jnp.int32, (tm, tm), 0) >= lax.broadcasted_iota(jnp.int32, (tm, tm), 1))
        f = jnp.dot(tri.astype(F32), logf, preferred_element_type=F32, precision=lax.Precision.HIGHEST)
        f = f + carry_ref[0:1, :]
        carry_ref[...] = jnp.broadcast_to(f[tm - 1:tm, :], carry_ref.shape)
        spread = (lax.broadcasted_iota(jnp.int32, (128, ATTN_W), 0)
                  == lax.broadcasted_iota(jnp.int32, (128, ATTN_W), 1) // HEAD_DIM).astype(F32)
        f_ref[...] = jnp.dot(f, spread, preferred_element_type=F32, precision=lax.Precision.HIGHEST)

    return pl.pallas_call(
        body, name="forget_gate_cumsum", out_shape=jax.ShapeDtypeStruct((t, ATTN_W), F32),
        grid=(t // tm,),
        in_specs=[pl.BlockSpec((tm, 128), lambda i: (i, f_col // 128)), pl.BlockSpec((1, 128), lambda i: (0, 0))],
        out_specs=pl.BlockSpec((tm, ATTN_W), lambda i: (i, 0)),
        scratch_shapes=[pltpu.VMEM((8, 128), F32)],
        compiler_params=_params(("arbitrary",)),
    )(z, b_pad)


def _lane_halves():
    lane = lax.broadcasted_iota(jnp.int32, (1, 128), 1)
    return lane < HEAD_DIM


def _causal_mask(tq, tk, row0=0):
    row = row0 + lax.broadcasted_iota(jnp.int32, (tq, 1), 0)
    col = lax.broadcasted_iota(jnp.int32, (1, tk), 1)
    return col <= row


def _lane_one(lane):
    return (lax.broadcasted_iota(jnp.int32, (1, 128), 1) == lane).astype(BF16)


def _split3(x):
    hi = x.astype(BF16)
    rest = x - hi.astype(F32)
    mid = rest.astype(BF16)
    return hi, mid, (rest - mid.astype(F32)).astype(BF16)


def _split3_glue(x):
    hi = lax.reduce_precision(x, 8, 7)
    mid = lax.reduce_precision(x - hi, 8, 7)
    lo = lax.reduce_precision((x - hi) - mid, 8, 7)
    return hi.astype(BF16), mid.astype(BF16), lo.astype(BF16)


def _attn_bias_operands(f_b, lse_b=None):
    t = f_b.shape[0]
    row = lax.broadcasted_iota(jnp.int32, (t, 1), 0)
    col = lax.broadcasted_iota(jnp.int32, (1, ATTN_W), 1) % HEAD_DIM

    def other_head(x):
        return jnp.roll(x.reshape(t, ATTN_W // 128, 128), HEAD_DIM, axis=2).reshape(t, ATTN_W)

    def place(cols):
        out = jnp.zeros((t, ATTN_W), BF16)
        for j, c in enumerate(cols):
            out = jnp.where(col == j, c, out)
        return out

    one = jnp.ones((t, ATTN_W), BF16)
    fq = _split3_glue(other_head(f_b))
    fk = _split3_glue(other_head(jnp.where(row < ROW_PAD, 1e9, f_b)))
    q_cols = list(fq) + [one] * 3
    k_cols = [one] * 3 + [-c for c in fk]
    if lse_b is not None:
        q_cols += [-c for c in _split3_glue(other_head(jnp.where(row < ROW_PAD, 1e9, lse_b)))]
        k_cols += [one] * 3
    return place(q_cols), place(k_cols)


def _attn_steps(nq, by_key):
    if by_key:
        pairs = [(qi, ki) for ki in range(nq) for qi in range(ki, nq)]
    else:
        pairs = [(qi, ki) for qi in range(nq) for ki in range(qi + 1)]
    return (jnp.array([p[0] for p in pairs], jnp.int32), jnp.array([p[1] for p in pairs], jnp.int32))


def _attn_fwd(z, aug_q, aug_k):
    t = z.shape[0]
    tq = tk = ROW_TILE
    nq = t // tq
    grp = ATTN_KV_GROUP
    steps = [(qi, ka) for qi in range(nq) for ka in range(0, qi + 1, grp)]
    q_tab = jnp.array([qi for qi, _ in steps], jnp.int32)
    k_tab = jnp.array([ka for _, ka in steps], jnp.int32)

    def body(qt_ref, kt_ref, q_ref, *refs):
        k_refs, v_refs, aq_ref, ak_refs = refs[:grp], refs[grp:2 * grp], refs[2 * grp], refs[2 * grp + 1:3 * grp + 1]
        o_ref, lse_ref, m_ref, l_ref, acc_ref = refs[3 * grp + 1:]
        step = pl.program_id(1)
        qi, ka = qt_ref[step], kt_ref[step]

        @pl.when(ka == 0)
        def _():
            m_ref[...] = jnp.full_like(m_ref, NEG)
            l_ref[...] = jnp.zeros_like(l_ref)
            acc_ref[...] = jnp.zeros_like(acc_ref)

        def sweep(diagonal):
            first = _lane_halves()
            halves = (first, jnp.logical_not(first))
            q = (q_ref[...] * (HEAD_DIM ** -0.5)).astype(BF16)
            aq = aq_ref[...]
            qa = [jnp.where(lanes, q, aq) for lanes in halves]
            blocks = list(zip(k_refs, v_refs, ak_refs, diagonal))
            s = []
            for k_ref, _, ak_ref, diag in blocks:
                k, ak = k_ref[...].astype(BF16), ak_ref[...]
                for hh, lanes in enumerate(halves):
                    s_c = _dot_nt(qa[hh], jnp.where(lanes, k, ak))
                    s.append(jnp.where(_causal_mask(tq, tk), s_c, NEG) if diag else s_c)
            nb = len(blocks)
            m_prev = [m_ref[:, c0:c0 + 1] for c0 in (0, HEAD_DIM)]
            m_new = []
            for hh in range(2):
                m_h = m_prev[hh]
                for b in range(nb):
                    m_h = jnp.maximum(m_h, jnp.max(s[2 * b + hh], axis=1, keepdims=True))
                m_new.append(m_h)
            pv = [None, None]
            for b, (_, v_ref, _, _) in enumerate(blocks):
                v = v_ref[...].astype(BF16)
                for hh, (lanes, a0) in enumerate(zip(halves, (HEAD_DIM, 0))):
                    part = _dot(jnp.exp(s[2 * b + hh] - m_new[hh]).astype(BF16), jnp.where(lanes, v, _lane_one(a0)))
                    pv[hh] = part if pv[hh] is None else pv[hh] + part
            al0, al1 = [jnp.exp(mp - m_h) for mp, m_h in zip(m_prev, m_new)]
            l0 = al0 * l_ref[:, 0:1] + pv[0][:, HEAD_DIM:HEAD_DIM + 1]
            l1 = al1 * l_ref[:, HEAD_DIM:HEAD_DIM + 1] + pv[1][:, 0:1]
            acc_ref[...] = acc_ref[...] * jnp.where(first, al0, al1) + jnp.where(first, pv[0], pv[1])
            m_ref[...] = jnp.where(first, m_new[0], m_new[1])
            l_ref[...] = jnp.where(first, l0, l1)

        def finish():
            o_ref[...] = acc_ref[...] / l_ref[...]
            lse_ref[...] = m_ref[...] + jnp.log(l_ref[...])

        @pl.when(ka + grp - 1 < qi)
        def _():
            sweep((False,) * grp)

        for nb in range(1, grp + 1):
            @pl.when(ka + nb - 1 == qi)
            def _(nb=nb):
                sweep((False,) * (nb - 1) + (True,))
                finish()

    def kblock(j):
        return lambda s, qt, kt: jnp.minimum(kt[s] + j, qt[s])

    kbs = [kblock(j) for j in range(grp)]
    return pl.pallas_call(
        body, name="attention_fwd",
        out_shape=[jax.ShapeDtypeStruct((t, ATTN_W), F32), jax.ShapeDtypeStruct((t, ATTN_W), F32)],
        grid_spec=pltpu.PrefetchScalarGridSpec(
            num_scalar_prefetch=2, grid=(4, len(steps)),
            in_specs=[pl.BlockSpec((tq, 128), lambda p, s, qt, kt: (qt[s], p))]
            + [pl.BlockSpec((tk, 128), functools.partial(lambda p, s, qt, kt, kb: (kb(s, qt, kt), 4 + p), kb=kb))
               for kb in kbs]
            + [pl.BlockSpec((tk, 128), functools.partial(lambda p, s, qt, kt, kb: (kb(s, qt, kt), 8 + p), kb=kb))
               for kb in kbs]
            + [pl.BlockSpec((tq, 128), lambda p, s, qt, kt: (qt[s], p))]
            + [pl.BlockSpec((tk, 128), functools.partial(lambda p, s, qt, kt, kb: (kb(s, qt, kt), p), kb=kb))
               for kb in kbs],
            out_specs=[pl.BlockSpec((tq, 128), lambda p, s, qt, kt: (qt[s], p)),
                       pl.BlockSpec((tq, 128), lambda p, s, qt, kt: (qt[s], p))],
            scratch_shapes=[pltpu.VMEM((tq, 128), F32)] * 3),
        compiler_params=_params(("parallel", "arbitrary"), VMEM_BIG),
    )(q_tab, k_tab, z, *([z] * (2 * grp)), aug_q, *([aug_k] * grp))


def _attn_bwd(z, aug_q, aug_k, o, do):
    t = z.shape[0]
    tq = tk = ROW_TILE
    nq = t // tq
    q_tab, k_tab = _attn_steps(nq, by_key=True)
    tn = (((0,), (0,)), ((), ()))

    def body(qt_ref, kt_ref, q_ref, k_ref, v_ref, aq_ref, ak_ref, o_ref, do_ref,
             dq_ref, dk_ref, dv_ref, dfk_ref, dfq_ref):
        step = pl.program_id(1)
        qi, ki = qt_ref[step], kt_ref[step]
        rows = pl.ds(pl.multiple_of(qi * tq, tq), tq)

        @pl.when(ki == 0)
        def _():
            dq_ref[rows, :] = jnp.zeros((tq, 128), F32)
            dfq_ref[rows, :] = jnp.zeros((tq, 128), F32)

        @pl.when(qi == ki)
        def _():
            dk_ref[...] = jnp.zeros_like(dk_ref)
            dv_ref[...] = jnp.zeros_like(dv_ref)
            dfk_ref[...] = jnp.zeros_like(dfk_ref)

        def sweep(diagonal):
            first = _lane_halves()
            lane = lax.broadcasted_iota(jnp.int32, (1, 128), 1)
            scale = HEAD_DIM ** -0.5
            q = (q_ref[...] * scale).astype(BF16)
            k = k_ref[...].astype(BF16)
            v = v_ref[...].astype(BF16)
            do_ = do_ref[...]
            do16 = do_.astype(BF16)
            od = o_ref[...] * do_
            aq, ak = aq_ref[...], ak_ref[...]
            halves = (first, jnp.logical_not(first))
            a0, a1 = HEAD_DIM, 0
            dos, vs = [], []
            for lanes, a in zip(halves, (a0, a1)):
                d_hi, d_mid, d_lo = _split3(jnp.sum(jnp.where(lanes, od, 0.0), axis=1, keepdims=True))
                minus_delta = jnp.where(lane == a, -d_hi, jnp.where(lane == a + 1, -d_mid,
                                        jnp.where(lane == a + 2, -d_lo, jnp.zeros((), BF16))))
                dos.append(jnp.where(lanes, do16, minus_delta))
                vs.append(jnp.where(lanes, v, ((lane >= a) & (lane < a + 3)).astype(BF16)))
            s = [_dot_nt(jnp.where(lanes, q, aq), jnp.where(lanes, k, ak)) for lanes in halves]
            dp = [_dot_nt(do_h, v_h) for do_h, v_h in zip(dos, vs)]
            p = [jnp.exp(s_h) for s_h in s]
            if diagonal:
                p = [jnp.where(_causal_mask(tq, tk), p_h, 0.0) for p_h in p]
            ds16 = [(p_h * dp_h).astype(BF16) for p_h, dp_h in zip(p, dp)]
            dv0, dv1 = [lax.dot_general(p_h.astype(BF16), jnp.where(lanes, do16, jnp.zeros((), BF16)), tn,
                                        preferred_element_type=F32) for p_h, lanes in zip(p, halves)]
            dk0, dk1 = [lax.dot_general(ds_h, jnp.where(lanes, q, _lane_one(a)), tn, preferred_element_type=F32)
                        for ds_h, lanes, a in zip(ds16, halves, (a0, a1))]
            dq0, dq1 = [_dot(ds_h, jnp.where(lanes, k, _lane_one(a))) for ds_h, lanes, a in zip(ds16, halves, (a0, a1))]
            dq_ref[rows, :] += jnp.where(first, dq0, dq1) * scale
            dfq_ref[rows, :] += jnp.where(first, dq0[:, a0:a0 + 1], dq1[:, a1:a1 + 1])
            dk_ref[...] += jnp.where(first, dk0, dk1)
            dfk_ref[...] += jnp.where(first, dk0[:, a0:a0 + 1], dk1[:, a1:a1 + 1])
            dv_ref[...] += dv0 + dv1

        @pl.when(qi > ki)
        def _():
            sweep(False)

        @pl.when(qi == ki)
        def _():
            sweep(True)

    qrow = lambda p, s, qt, kt: (qt[s], p)
    krow = lambda p, s, qt, kt: (kt[s], p)
    return pl.pallas_call(
        body, name="attention_bwd",
        out_shape=[jax.ShapeDtypeStruct((t, ATTN_W), F32)] * 5,
        grid_spec=pltpu.PrefetchScalarGridSpec(
            num_scalar_prefetch=2, grid=(4, int(q_tab.shape[0])),
            in_specs=[pl.BlockSpec((tq, 128), qrow),
                      pl.BlockSpec((tk, 128), lambda p, s, qt, kt: (kt[s], 4 + p)),
                      pl.BlockSpec((tk, 128), lambda p, s, qt, kt: (kt[s], 8 + p)),
                      pl.BlockSpec((tq, 128), qrow), pl.BlockSpec((tk, 128), krow),
                      pl.BlockSpec((tq, 128), qrow), pl.BlockSpec((tq, 128), qrow)],
            out_specs=[pl.BlockSpec((t, 128), lambda p, s, qt, kt: (0, p)),
                       pl.BlockSpec((tk, 128), krow), pl.BlockSpec((tk, 128), krow), pl.BlockSpec((tk, 128), krow),
                       pl.BlockSpec((t, 128), lambda p, s, qt, kt: (0, p))]),
        compiler_params=_params(("parallel", "arbitrary"), VMEM_BIG),
    )(q_tab, k_tab, z, z, z, aug_q, aug_k, o, do)


def _shifted(prev_rows, x, shift):
    tm = x.shape[0]
    return pltpu.roll(jnp.concatenate([prev_rows, x], axis=0), shift, 0)[8:8 + tm]


def _ahead(x, next_rows, shift):
    tm = x.shape[0]
    return pltpu.roll(jnp.concatenate([x, next_rows], axis=0), tm + 8 - shift, 0)[0:tm]


def _conv_col0(z):
    return (z.shape[1] - F_PAD - 3 * CONV_W) // CONV_W


def _conv_specs(tm, c0):
    cols = (c0, c0 + 1, c0 + 2)
    tiles = [pl.BlockSpec((tm, CONV_W), functools.partial(lambda i, c: (i, c), c=c)) for c in cols]
    halos = [pl.BlockSpec((8, CONV_W), functools.partial(lambda i, c: (jnp.maximum(i * (tm // 8) - 1, 0), c), c=c))
             for c in cols]
    return tiles, halos


def _conv_gate(z, conv_w):
    t = z.shape[0]
    tm = ROW_TILE
    nt = t // tm

    def body(cb_ref, cc_ref, ci_ref, hc_ref, hi_ref, w_ref, g_ref, gt_ref):
        i = pl.program_id(0)
        cc = cc_ref[...] * ci_ref[...]
        prev = jnp.where(i > 0, hc_ref[...] * hi_ref[...], 0.0)
        conv = w_ref[0:1, :] * _shifted(prev, cc, 2) + w_ref[1:2, :] * _shifted(prev, cc, 1) + w_ref[2:3, :] * cc
        g = cb_ref[...] * conv
        g_ref[...] = g.astype(BF16)
        gt_ref[...] = g.T.astype(BF16)

    (cb, cc, ci), (_, hc, hi) = _conv_specs(tm, _conv_col0(z))
    return pl.pallas_call(
        body, name="conv_gate_fwd",
        out_shape=[jax.ShapeDtypeStruct((t, CONV_W), BF16), jax.ShapeDtypeStruct((CONV_W, t), BF16)],
        grid=(nt,),
        in_specs=[cb, cc, ci, hc, hi, pl.BlockSpec((8, CONV_W), lambda i: (0, 0))],
        out_specs=[pl.BlockSpec((tm, CONV_W), lambda i: (i, 0)), pl.BlockSpec((CONV_W, tm), lambda i: (0, i))],
        compiler_params=_params(("parallel",)),
    )(z, z, z, z, z, conv_w)


def _conv_bwd(z, dg, conv_w):
    t = z.shape[0]
    tm = ROW_TILE
    nt = t // tm

    def body(cb_ref, cc_ref, ci_ref, hc_ref, hi_ref, dg_ref, ncb_ref, ndg_ref, w_ref, dz_ref, dw_ref):
        i = pl.program_id(0)

        @pl.when(i == 0)
        def _():
            dw_ref[...] = jnp.zeros_like(dw_ref)

        cb, c_c, c_in = cb_ref[...], cc_ref[...], ci_ref[...]
        cc = c_c * c_in
        prev = jnp.where(i > 0, hc_ref[...] * hi_ref[...], 0.0)
        cc1, cc2 = _shifted(prev, cc, 1), _shifted(prev, cc, 2)
        w0, w1, w2 = w_ref[0:1, :], w_ref[1:2, :], w_ref[2:3, :]
        conv = w0 * cc2 + w1 * cc1 + w2 * cc
        dgv = dg_ref[...]
        dconv = dgv * cb
        nxt = jnp.where(i < nt - 1, ndg_ref[...] * ncb_ref[...], 0.0)
        dcc = w2 * dconv + w1 * _ahead(dconv, nxt, 1) + w0 * _ahead(dconv, nxt, 2)
        dz_ref[:, 0:CONV_W] = (dgv * conv).astype(BF16)
        dz_ref[:, CONV_W:2 * CONV_W] = (dcc * c_in).astype(BF16)
        dz_ref[:, 2 * CONV_W:] = (dcc * c_c).astype(BF16)
        dw_ref[0:1, :] += jnp.sum(dconv * cc2, axis=0, keepdims=True)
        dw_ref[1:2, :] += jnp.sum(dconv * cc1, axis=0, keepdims=True)
        dw_ref[2:3, :] += jnp.sum(dconv * cc, axis=0, keepdims=True)

    c0 = _conv_col0(z)
    (cb, cc, ci), (_, hc, hi) = _conv_specs(tm, c0)
    nxt = lambda i, c: (jnp.minimum((i + 1) * (tm // 8), t // 8 - 1), c)
    return pl.pallas_call(
        body, name="conv_gate_bwd",
        out_shape=[jax.ShapeDtypeStruct((t, 3 * CONV_W), BF16), jax.ShapeDtypeStruct((8, CONV_W), F32)],
        grid=(nt,),
        in_specs=[cb, cc, ci, hc, hi, pl.BlockSpec((tm, CONV_W), lambda i: (i, 0)),
                  pl.BlockSpec((8, CONV_W), lambda i: nxt(i, c0)), pl.BlockSpec((8, CONV_W), lambda i: nxt(i, 0)),
                  pl.BlockSpec((8, CONV_W), lambda i: (0, 0))],
        out_specs=[pl.BlockSpec((tm, 3 * CONV_W), lambda i: (i, 0)), pl.BlockSpec((8, CONV_W), lambda i: (0, 0))],
        compiler_params=_params(("arbitrary",)),
    )(z, z, z, z, z, dg, z, dg, conv_w)


def _branch_mix(z, o, g, w_ab, w_cb, d):
    t = z.shape[0]
    tm = ROW_TILE
    ga_col = 0

    def body(o_ref, g_ref, ga_ref, gc_ref, wa_ref, wc_ref, mp_ref, mpt_ref, ot_ref):
        o_ = o_ref[...]
        ya = _dot(o_.astype(BF16), wa_ref[...])
        yc = _dot(g_ref[...], wc_ref[...])
        mp = _sigmoid(ga_ref[...]) * ya + _sigmoid(gc_ref[...]) * yc
        mp_ref[...] = mp.astype(BF16)
        mpt_ref[...] = mp.T.astype(BF16)
        ot_ref[...] = o_.T.astype(BF16)

    return pl.pallas_call(
        body, name="branch_mix_fwd",
        out_shape=[jax.ShapeDtypeStruct((t, d), BF16), jax.ShapeDtypeStruct((d, t), BF16),
                   jax.ShapeDtypeStruct((ATTN_W, t), BF16)],
        grid=(t // tm,),
        in_specs=[pl.BlockSpec((tm, ATTN_W), lambda i: (i, 0)), pl.BlockSpec((tm, CONV_W), lambda i: (i, 0)),
                  pl.BlockSpec((tm, d), lambda i: (i, ga_col)), pl.BlockSpec((tm, d), lambda i: (i, ga_col + 1)),
                  pl.BlockSpec((ATTN_W, d), lambda i: (0, 0)), pl.BlockSpec((CONV_W, d), lambda i: (0, 0))],
        out_specs=[pl.BlockSpec((tm, d), lambda i: (i, 0)), pl.BlockSpec((d, tm), lambda i: (0, i)),
                   pl.BlockSpec((ATTN_W, tm), lambda i: (0, i))],
        compiler_params=_params(("parallel",), VMEM_BIG),
    )(o, g, z, z, w_ab, w_cb)


def _branch_bwd(z, o, g, dmixed, w_out, w_ab, w_cb, d):
    t = z.shape[0]
    tm = ROW_TILE // 2
    ga_col = 0

    def body(dm_ref, o_ref, g_ref, ga_ref, gc_ref, wo_ref, wa_ref, wc_ref, dya_ref, dyc_ref, dgt_ref, do_ref, dg_ref):
        dmp = _dot_nt(dm_ref[...], wo_ref[...])
        ya = _dot(o_ref[...].astype(BF16), wa_ref[...])
        yc = _dot(g_ref[...], wc_ref[...])
        sa, sc = _sigmoid(ga_ref[...]), _sigmoid(gc_ref[...])
        dya = (dmp * sa).astype(BF16)
        dyc = (dmp * sc).astype(BF16)
        dya_ref[...] = dya
        dyc_ref[...] = dyc
        dgt_ref[:, :d] = (dmp * ya * sa * (1.0 - sa)).astype(BF16)
        dgt_ref[:, d:] = (dmp * yc * sc * (1.0 - sc)).astype(BF16)
        do_ref[...] = _dot_nt(dya, wa_ref[...])
        dg_ref[...] = _dot_nt(dyc, wc_ref[...])

    row = lambda i: (i, 0)
    fixed = lambda i: (0, 0)
    return pl.pallas_call(
        body, name="branch_mix_bwd",
        out_shape=[jax.ShapeDtypeStruct((t, d), BF16), jax.ShapeDtypeStruct((t, d), BF16),
                   jax.ShapeDtypeStruct((t, 2 * d), BF16), jax.ShapeDtypeStruct((t, ATTN_W), F32),
                   jax.ShapeDtypeStruct((t, CONV_W), F32)],
        grid=(t // tm,),
        in_specs=[pl.BlockSpec((tm, d), row), pl.BlockSpec((tm, ATTN_W), row), pl.BlockSpec((tm, CONV_W), row),
                  pl.BlockSpec((tm, d), lambda i: (i, ga_col)), pl.BlockSpec((tm, d), lambda i: (i, ga_col + 1)),
                  pl.BlockSpec((d, d), fixed), pl.BlockSpec((ATTN_W, d), fixed), pl.BlockSpec((CONV_W, d), fixed)],
        out_specs=[pl.BlockSpec((tm, d), row), pl.BlockSpec((tm, d), row), pl.BlockSpec((tm, 2 * d), row),
                   pl.BlockSpec((tm, ATTN_W), row), pl.BlockSpec((tm, CONV_W), row)],
        compiler_params=_params(("parallel",), VMEM_BIG),
    )(dmixed, o, g, z, z, w_out, w_ab, w_cb)


def _loss_norm_bwd(h, target, f, g_post, alpha):
    t, d = h.shape
    tm = N_FRONT

    def body(h_ref, t_ref, f_ref, g_ref, dh_ref, df_ref, dg_ref, loss_ref):
        i = pl.program_id(0)

        @pl.when(i == 0)
        def _():
            loss_ref[...] = jnp.zeros_like(loss_ref)
            dg_ref[...] = jnp.zeros_like(dg_ref)

        err = jnp.where(i > 0, h_ref[...] - t_ref[...], 0.0)
        dy = err * (1.0 / d)
        dh_ref[...] = dy
        per_row = jnp.sum(err * err, axis=1, keepdims=True) * (1.0 / d)
        loss_ref[...] += 0.5 * jnp.sum(per_row, axis=0, keepdims=True)
        dx, dg = _rms_bwd(f_ref[...], g_ref[...], dy)
        df_ref[...] = (alpha * dx).astype(BF16)
        dg_ref[...] += alpha * dg

    row = pl.BlockSpec((tm, d), lambda i: (i, 0))
    vec = pl.BlockSpec((1, d), lambda i: (0, 0))
    return pl.pallas_call(
        body, name="loss_and_post_norm_bwd",
        out_shape=[jax.ShapeDtypeStruct((t, d), F32), jax.ShapeDtypeStruct((t, d), BF16),
                   jax.ShapeDtypeStruct((1, d), F32), jax.ShapeDtypeStruct((1, 128), F32)],
        grid=(t // tm,),
        in_specs=[row, pl.BlockSpec((tm, d), lambda i: (jnp.maximum(i - 1, 0), 0)), row, vec],
        out_specs=[row, row, vec, pl.BlockSpec((1, 128), lambda i: (0, 0))],
        compiler_params=_params(("arbitrary",)),
    )(h, target, f, g_post)


def _norm_bwd(name, x, g, dy, alpha):
    t, d = x.shape
    tm = ROW_TILE

    def body(x_ref, g_ref, dy_ref, dx_ref, dg_ref):
        @pl.when(pl.program_id(0) == 0)
        def _():
            dg_ref[...] = jnp.zeros_like(dg_ref)

        dx, dg = _rms_bwd(x_ref[...], g_ref[...], dy_ref[...])
        dx_ref[...] = (alpha * dx).astype(BF16)
        dg_ref[...] += alpha * dg

    row = pl.BlockSpec((tm, d), lambda i: (i, 0))
    vec = pl.BlockSpec((1, d), lambda i: (0, 0))
    return pl.pallas_call(
        body, name=name,
        out_shape=[jax.ShapeDtypeStruct((t, d), BF16), jax.ShapeDtypeStruct((1, d), F32)],
        grid=(t // tm,), in_specs=[row, vec, row], out_specs=[row, vec],
        compiler_params=_params(("arbitrary",)),
    )(x, g, dy)


def _ffn_bwd_mid(name, df, w_out, ab):
    t, d = df.shape
    cw = ab.shape[1] // 4
    tm = ROW_TILE

    def body(df_ref, w_ref, ab_ref, o_ref):
        ds = _dot_nt(df_ref[...], w_ref[...])
        a = ab_ref[:, :cw].astype(F32)
        b = ab_ref[:, cw:].astype(F32)
        sg = _sigmoid(a)
        o_ref[:, :cw] = (ds * b * (sg * (1.0 + a * (1.0 - sg)))).astype(BF16)
        o_ref[:, cw:] = (ds * (a * sg)).astype(BF16)

    return pl.pallas_call(
        body, name=name, out_shape=jax.ShapeDtypeStruct((t, 4 * cw), BF16),
        grid=(2, t // tm),
        in_specs=[pl.BlockSpec((tm, d), lambda j, i: (i, 0)), pl.BlockSpec((cw, d), lambda j, i: (j, 0)),
                  pl.BlockSpec((tm, 2 * cw), lambda j, i: (i, j))],
        out_specs=pl.BlockSpec((tm, 2 * cw), lambda j, i: (i, j)),
        compiler_params=_params(("parallel", "parallel"), VMEM_BIG),
    )(df, w_out, ab)


def _mm_nt_norm_bwd(name, dy, w, h, g, dh_in):
    t, kdim = dy.shape
    d = h.shape[1]
    tm = ROW_TILE // 2
    slots = w.ndim == 3

    def body(dy_ref, w_ref, h_ref, g_ref, dhi_ref, dh_ref, dg_ref):
        @pl.when(pl.program_id(0) == 0)
        def _():
            dg_ref[...] = jnp.zeros_like(dg_ref)

        if slots:
            cw = w_ref.shape[2]
            dn = _dot_nt(dy_ref[:, 0:cw], w_ref[_slot_of(0)])
            for k in range(1, 4):
                dn += _dot_nt(dy_ref[:, k * cw:(k + 1) * cw], w_ref[_slot_of(k)])
        else:
            dn = _dot_nt(dy_ref[...], w_ref[...])
        dx, dg = _rms_bwd(h_ref[...], g_ref[...], dn)
        dh_ref[...] = dhi_ref[...] + dx
        dg_ref[...] += dg

    row = pl.BlockSpec((tm, d), lambda i: (i, 0))
    vec = pl.BlockSpec((1, d), lambda i: (0, 0))
    return pl.pallas_call(
        body, name=name,
        out_shape=[jax.ShapeDtypeStruct((t, d), F32), jax.ShapeDtypeStruct((1, d), F32)],
        grid=(t // tm,),
        in_specs=[pl.BlockSpec((tm, kdim), lambda i: (i, 0)), pl.BlockSpec(w.shape, lambda i: (0,) * w.ndim),
                  row, vec, row],
        out_specs=[row, vec],
        compiler_params=_params(("arbitrary",), VMEM_BIG),
    )(dy, w, h, g, dh_in)


def _gate_bwd(dfq, dfk, z, b_pad, f_col):
    t = z.shape[0]
    tm = ROW_TILE
    nt = t // tm

    def body(dq_ref, dk_ref, z_ref, b_ref, dz_ref, db_ref, carry_ref):
        i = pl.program_id(0)

        @pl.when(i == 0)
        def _():
            carry_ref[...] = jnp.zeros_like(carry_ref)
            db_ref[...] = jnp.zeros_like(db_ref)

        pick = (lax.broadcasted_iota(jnp.int32, (ATTN_W, 128), 0)
                == HEAD_DIM * lax.broadcasted_iota(jnp.int32, (ATTN_W, 128), 1)).astype(F32)
        d_heads = jnp.dot(dq_ref[...] - dk_ref[...], pick, preferred_element_type=F32,
                          precision=lax.Precision.HIGHEST)
        tri = (lax.broadcasted_iota(jnp.int32, (tm, tm), 0) <= lax.broadcasted_iota(jnp.int32, (tm, tm), 1))
        tail = jnp.dot(tri.astype(F32), d_heads, preferred_element_type=F32, precision=lax.Precision.HIGHEST)
        tail = tail + carry_ref[0:1, :]
        carry_ref[...] = jnp.broadcast_to(tail[0:1, :], carry_ref.shape)
        row = (nt - 1 - i) * tm + lax.broadcasted_iota(jnp.int32, (tm, 1), 0)
        dlogit = jnp.where(row >= ROW_PAD, tail * _sigmoid(-(z_ref[...] + b_ref[...])), 0.0)
        dz_ref[...] = jnp.zeros_like(dz_ref)
        dz_ref[:, 0:128] = dlogit.astype(BF16)
        db_ref[...] += jnp.sum(dlogit, axis=0, keepdims=True)

    rev = lambda i: (nt - 1 - i, 0)
    return pl.pallas_call(
        body, name="forget_gate_bwd",
        out_shape=[jax.ShapeDtypeStruct((t, F_PAD), BF16), jax.ShapeDtypeStruct((1, 128), F32)],
        grid=(nt,),
        in_specs=[pl.BlockSpec((tm, ATTN_W), rev), pl.BlockSpec((tm, ATTN_W), rev),
                  pl.BlockSpec((tm, 128), lambda i: (nt - 1 - i, f_col // 128)),
                  pl.BlockSpec((1, 128), lambda i: (0, 0))],
        out_specs=[pl.BlockSpec((tm, F_PAD), rev), pl.BlockSpec((1, 128), lambda i: (0, 0))],
        scratch_shapes=[pltpu.VMEM((8, 128), F32)],
        compiler_params=_params(("arbitrary",)),
    )(dfq, dfk, z, b_pad)


def _ffn_fwd(tag, n, w_in4, w_out, h, g_post, g_next):
    ab, s, s_t = _ffn_in(f"{tag}_in_fwd", n, w_in4)
    outs = _mm_resid_norm(f"{tag}_out_fwd", s, w_out, h, g_post, 0.5, g_next)
    return ab, s_t, outs


def _ffn_bwd_weights(tag, df, ab, s_t, n_t, w_in4, w_out):
    d, cw = w_in4.shape[1], w_in4.shape[2]
    t = df.shape[0]
    dw_out = _weight_grad(f"{tag}_dw_out", s_t, df, d, out_rows=cw // 2)
    dab = _ffn_bwd_mid(f"{tag}_mid_bwd", df, w_out, ab)
    bk = _k_tile(t)
    dw_in = _matmul(
        f"{tag}_dw_in", n_t, dab, jax.ShapeDtypeStruct((4, d, cw), F32), (1, 4, t // bk),
        pl.BlockSpec((d, bk), lambda a, b, k: (0, k)), pl.BlockSpec((bk, cw), lambda a, b, k: (k, b)),
        pl.BlockSpec((None, d, cw), lambda a, b, k: (_slot_of(b), 0, 0)), vmem=VMEM_BIG)
    return dab, dw_in, dw_out


LOSS_ROW = 12


def _pack_small(meta, conv, gains, b_forget, loss=None):
    d = gains[0].shape[1]
    rows = [meta.reshape(4, d), jnp.pad(conv.reshape(1, 3 * 128), ((0, 0), (0, d - 3 * 128)))]
    rows += list(gains) + [jnp.pad(b_forget, ((0, 0), (0, d - HEADS)))]
    last = jnp.zeros((4, d), F32)
    if loss is not None:
        last = jnp.pad(loss.reshape(1, 1), ((0, 3), (0, d - 1)))
    return jnp.concatenate(rows + [last], axis=0)


def _unpack_small(block):
    d = block.shape[1]
    meta = block[0:4].reshape(N_META, d // 4)
    conv = block[4, :3 * 128].reshape(1, 3, 128)
    gains = [block[5 + i:6 + i] for i in range(6)]
    return meta, conv, gains, block[11:12, :HEADS]


def kernel(x, meta_tokens, w_in, b_forget, conv_w, w_attn_branch, w_conv_branch, w_out, g_ffn1_pre, g_ffn1_post, w_ffn1_in, w_ffn1_out, g_mix_pre, g_mix_post, g_ffn2_pre, g_ffn2_post, w_ffn2_in, w_ffn2_out, loss_target, m_meta_tokens, m_w_in, m_b_forget, m_conv_w, m_w_attn_branch, m_w_conv_branch, m_w_out, m_g_ffn1_pre, m_g_ffn1_post, m_w_ffn1_in, m_w_ffn1_out, m_g_mix_pre, m_g_mix_post, m_g_ffn2_pre, m_g_ffn2_post, m_w_ffn2_in, m_w_ffn2_out, v_meta_tokens, v_w_in, v_b_forget, v_conv_w, v_w_attn_branch, v_w_conv_branch, v_w_out, v_g_ffn1_pre, v_g_ffn1_post, v_w_ffn1_in, v_w_ffn1_out, v_g_mix_pre, v_g_mix_post, v_g_ffn2_pre, v_g_ffn2_post, v_w_ffn2_in, v_w_ffn2_out):
    seq, d = x.shape[1], x.shape[2]
    t = seq + N_FRONT
    n_main = 3 * ATTN_W + 3 * CONV_W + 2 * d
    nz = n_main + F_PAD
    f_lo = 3 * ATTN_W
    c_arr = lax.axis_index("c").astype(jnp.int32).reshape(1)

    cs = w_in.shape[2]
    cs_pad = -(-cs // 64) * 64

    def w_in_rows(a):
        return jnp.pad(jnp.transpose(a[0]), ((0, cs_pad - cs), (0, 0)))

    big = [w_in_rows(w_in), w_attn_branch[0], w_conv_branch[0], w_out[0], w_ffn1_in[0], w_ffn1_out[0], w_ffn2_in[0],
           w_ffn2_out[0]]
    small_gather = jnp.concatenate(
        [meta_tokens.reshape(4, d), jnp.pad(conv_w.reshape(1, 3 * 128), ((0, 0), (0, d - 3 * 128))),
         jnp.zeros((11, d), F32)], axis=0)
    w_f1_in4, small4 = _all_gather([big[4].astype(BF16), small_gather])
    (second, rest), small4 = lax.optimization_barrier(
        (([big[5].astype(BF16)], [big[i].astype(BF16) for i in (0, 1, 2, 3, 6, 7)]), small4))
    second_gathered = _all_gather_async("all_gather_ffn1_out", second, 5)
    rest_gathered = _all_gather_async("all_gather_rest", rest, 1)
    meta_full = jnp.transpose(small4[:, 0:4].reshape(4, N_META, d // 4), (1, 0, 2)).reshape(N_META, d)
    conv_full = jnp.transpose(small4[:, 4, :3 * 128].reshape(4, 3, 128), (1, 0, 2)).reshape(3, CONV_W)
    conv_pad = jnp.pad(conv_full, ((0, 5), (0, 0)))
    b_pad = jnp.pad(b_forget, ((0, 0), (0, 128 - HEADS)))

    h0 = jnp.concatenate([jnp.zeros((ROW_PAD, d), F32), meta_full, x[0]], axis=0)
    n1, n1_t = _norm_fwd("ffn1_pre_norm", h0, g_ffn1_pre)
    ab1, s1, s1_t = _ffn_in("ffn1_in_fwd", n1, w_f1_in4)
    w_f1_out = second_gathered(s1, [0])[0].reshape(-1, d)
    f1, h1, u, u_t = _mm_resid_norm("ffn1_out_fwd", s1, w_f1_out, h0, g_ffn1_post, 0.5, g_mix_pre)

    w_in4, w_ab4, w_cb4, w_out4, w_f2_in4, w_f2_out4 = rest_gathered(u, range(6))
    w_in_t = w_in4[:, :cs].reshape(4 * cs, d)
    g_lo = f_lo + HEADS + 3 * CONV_W
    w_in_pad = jnp.concatenate(
        [w_in_t[:f_lo], w_in_t[g_lo:], w_in_t[f_lo + HEADS:g_lo], w_in_t[f_lo:f_lo + HEADS],
         jnp.zeros((F_PAD - HEADS, d), BF16)], axis=0)
    w_ab = jnp.transpose(w_ab4, (1, 0, 2)).reshape(ATTN_W, d)
    w_cb = jnp.transpose(w_cb4, (1, 0, 2)).reshape(CONV_W, d)
    w_out_full = w_out4.reshape(d, d)
    w_f2_out = w_f2_out4.reshape(-1, d)
    qkv, z = _in_proj(u, w_in_pad)
    f_col = z.shape[1] - F_PAD
    f_b = _gate_prep(z, b_pad, f_col)
    o, lse = _attn_fwd(qkv, *_attn_bias_operands(f_b))
    g, g_t = _conv_gate(z, conv_pad)
    mp, mp_t, o_t = _branch_mix(z, o, g, w_ab, w_cb, d)
    mixed, h2, n2, n2_t = _mm_resid_norm("mix_out_fwd", mp, w_out_full, h1, g_mix_post, 1.0, g_ffn2_pre)
    ab2, s2_t, (f2, h3) = _ffn_fwd("ffn2", n2, w_f2_in4, w_f2_out, h2, g_ffn2_post, None)
    dh3, df2, dg_f2_post, loss_part = _loss_norm_bwd(h3, loss_target[0], f2, g_ffn2_post, 0.5)

    reduced = {}

    def reduce_scatter(label, tags, slots, sequencer_id, hold=None, got=None, after=None):
        if got is None:
            got = _pair_send_halves(f"grad_pair_exchange_{label}", slots)
        else:
            got, _ = lax.optimization_barrier((got, after))
        sums = [_pair_add(tag, s, a, c_arr, F32 if tag == "small" else BF16) for tag, s, a in zip(tags, slots, got)]
        sums, hold = lax.optimization_barrier((sums, hold))
        if sequencer_id is None:
            arrived = _chip_scatter(f"grad_chip_scatter_{label}", sums)
        else:
            arrived = _chip_scatter_async(f"grad_chip_scatter_{label}", sums, sequencer_id)
        mine = [_chip_add(tag, a) for tag, a in zip(tags, arrived)]
        reduced.update(zip(tags, zip(mine, _pair_swap(f"grad_pair_swap_{label}", mine))))
        return hold

    dab2, dw_f2_in, dw_f2_out = _ffn_bwd_weights("ffn2", df2, ab2, s2_t, n2_t, w_f2_in4, w_f2_out)
    ffn2_slots = [dw_f2_in, dw_f2_out.reshape(4, -1, d)]
    ffn2_got = _pair_send_halves_async("grad_pair_exchange_ffn2", ffn2_slots, 6)
    dh2, dg_f2_pre = _mm_nt_norm_bwd("ffn2_in_bwd", dab2, w_f2_in4, h2, g_ffn2_pre, dh3)
    reduce_scatter("ffn2", ["w_ffn2_in", "w_ffn2_out"], ffn2_slots, 2, got=ffn2_got, after=dh2)
    dmixed, dg_mix_post = _norm_bwd("mix_post_norm_bwd", mixed, g_mix_post, dh2, 1.0)
    dw_out = _weight_grad("mix_dw_out", mp_t, dmixed, d)
    dya, dyc, dgates, do, dgconv = _branch_bwd(z, o, g, dmixed, w_out_full, w_ab, w_cb, d)
    dw_ab = _weight_grad("mix_dw_attn_branch", o_t, dya, d)
    dw_cb = _weight_grad("mix_dw_conv_branch", g_t, dyc, d)
    dz_conv, dconv_w = _conv_bwd(z, dgconv, conv_pad)
    dq, dk, dv, dfk, dfq = _attn_bwd(qkv, *_attn_bias_operands(f_b, lse), o, do)
    dz_f, db_forget = _gate_bwd(dfq, dfk, z, b_pad, f_col)
    dz_pieces = {"q": dq, "k": dk, "v": dv, "gates": dgates, "conv": dz_conv, "f": dz_f}
    dh1, dg_mix_pre = _mix_in_bwd(list(dz_pieces.values()), w_in_pad, h1, g_mix_pre, dh2)
    dw_t = {name: _weight_grad_t(f"mix_dw_in_{name}", u_t, piece) for name, piece in dz_pieces.items()}
    dw_in_t = jnp.concatenate(
        [dw_t["q"], dw_t["k"], dw_t["v"], dw_t["f"][:HEADS], dw_t["conv"], dw_t["gates"]], axis=0)
    mix_slots = [jnp.pad(dw_in_t.reshape(4, cs, d), ((0, 0), (0, cs_pad - cs), (0, 0))),
                 jnp.transpose(dw_ab.reshape(ATTN_W, 4, d // 4), (1, 0, 2)),
                 jnp.transpose(dw_cb.reshape(CONV_W, 4, d // 4), (1, 0, 2)),
                 dw_out.reshape(4, d // 4, d)]
    mix_got = _pair_send_halves_async("grad_pair_exchange_mix", mix_slots, 7)
    df1, dg_f1_post = _norm_bwd("ffn1_post_norm_bwd", f1, g_ffn1_post, dh1, 0.5)
    reduce_scatter("mix", ["w_in", "w_attn_branch", "w_conv_branch", "w_out"], mix_slots, 3, got=mix_got, after=df1)
    dab1, dw_f1_in, dw_f1_out = _ffn_bwd_weights("ffn1", df1, ab1, s1_t, n1_t, w_f1_in4, w_f1_out)
    dab1 = reduce_scatter("ffn1", ["w_ffn1_in", "w_ffn1_out"], [dw_f1_in, dw_f1_out.reshape(4, -1, d)], 4, dab1)
    dh0, dg_f1_pre = _mm_nt_norm_bwd("ffn1_in_bwd", dab1, w_f1_in4, h0, g_ffn1_pre, dh1)
    grad_x = dh0[N_FRONT:][None]
    dmeta = dh0[ROW_PAD:N_FRONT]
    small_grad = jnp.stack([
        _pack_small(dmeta[:, j * (d // 4):(j + 1) * (d // 4)], dconv_w[:3, j * 128:(j + 1) * 128],
                    [dg_f1_pre, dg_f1_post, dg_mix_pre, dg_mix_post, dg_f2_pre, dg_f2_post], db_forget[:, :HEADS],
                    loss_part[0, 0])
        for j in range(4)])
    reduce_scatter("small", ["small"], [small_grad], None)
    tags =["w_in", "w_attn_branch", "w_conv_branch", "w_out", "w_ffn1_in", "w_ffn1_out", "w_ffn2_in", "w_ffn2_out", "small"]
    halves = [reduced[tag][0] for tag in tags]
    others = [reduced[tag][1] for tag in tags]

    small = [g_ffn1_pre, g_ffn1_post, g_mix_pre, g_mix_post, g_ffn2_pre, g_ffn2_post]
    small_m = [m_g_ffn1_pre, m_g_ffn1_post, m_g_mix_pre, m_g_mix_post, m_g_ffn2_pre, m_g_ffn2_post]
    small_v = [v_g_ffn1_pre, v_g_ffn1_post, v_g_mix_pre, v_g_mix_post, v_g_ffn2_pre, v_g_ffn2_post]
    ws = big + [_pack_small(meta_tokens, conv_w[0], small, b_forget)]
    ms = [w_in_rows(m_w_in), m_w_attn_branch[0], m_w_conv_branch[0], m_w_out[0], m_w_ffn1_in[0], m_w_ffn1_out[0],
          m_w_ffn2_in[0], m_w_ffn2_out[0], _pack_small(m_meta_tokens, m_conv_w[0], small_m, m_b_forget)]
    vs = [w_in_rows(v_w_in), v_w_attn_branch[0], v_w_conv_branch[0], v_w_out[0], v_w_ffn1_in[0], v_w_ffn1_out[0],
          v_w_ffn2_in[0], v_w_ffn2_out[0], _pack_small(v_meta_tokens, v_conv_w[0], small_v, v_b_forget)]
    updates = [_adamw(tag, w, a, b, m, v, c_arr) for tag, w, a, b, m, v in zip(tags, ws, halves, others, ms, vs)]

    def leaves(big_vals, small_block):
        meta, conv, gains, bf = _unpack_small(small_block)
        w_in_t_, w_ab_, w_cb_, w_out_, f1_in, f1_out, f2_in, f2_out = [b[None] for b in big_vals]
        w_in_ = jnp.transpose(w_in_t_[:, :cs], (0, 2, 1))
        return [meta, w_in_, bf, conv, w_ab_, w_cb_, w_out_, gains[0], gains[1], f1_in, f1_out,
                gains[2], gains[3], gains[4], gains[5], f2_in, f2_out]

    out_g, out_d, out_m, out_v = [leaves([u_[k] for u_ in updates[:8]], updates[8][k]) for k in range(4)]
    loss = updates[8][0][LOSS_ROW, 0]
    return (loss, grad_x, *out_g, *out_d, *out_m, *out_v)
```

```python
import functools

import jax
import jax.numpy as jnp
from jax import lax
from jax.experimental import pallas as pl
from jax.experimental.pallas import tpu as pltpu
from jax.experimental.pallas import tpu_sc as plsc

N_META = 16
ROW_PAD = 112
N_FRONT = ROW_PAD + N_META
HEADS = 8
HEAD_DIM = 64
ATTN_W = HEADS * HEAD_DIM
CONV_W = 512
NORM_EPS = 1e-6
ROW_TILE = 640
F_PAD = 128
ATTN_Q_GROUP = 2
ATTN_KV_GROUP = 6
NEG = -1e30
ADAM_LR = 0.001
ADAM_B1 = 0.9
ADAM_B2 = 0.999
ADAM_EPS = 1e-08
ADAM_WD = 0.01
ADAM_STEP = 10
VMEM_BIG = 56 * 1024 * 1024
MESH = pl.DeviceIdType.MESH
ANY = pl.BlockSpec(memory_space=pl.ANY)
F32 = jnp.float32
BF16 = jnp.bfloat16


def _params(sem, vmem=None):
    return pltpu.CompilerParams(dimension_semantics=sem, vmem_limit_bytes=vmem)


def _sigmoid(x):
    return 1.0 / (1.0 + jnp.exp(-x))


def _rstd(x):
    return lax.rsqrt(jnp.mean(x * x, axis=-1, keepdims=True) + NORM_EPS)


def _rms_bwd(x, g, dy):
    r = _rstd(x)
    xr = x * r
    gdy = g * dy
    dx = r * (gdy - xr * jnp.mean(xr * gdy, axis=-1, keepdims=True))
    return dx, jnp.sum(dy * xr, axis=0, keepdims=True)


def _dot(a, b):
    return jnp.dot(a, b, preferred_element_type=F32)


def _dot_nt(a, b):
    return lax.dot_general(a, b, (((1,), (1,)), ((), ())), preferred_element_type=F32)


def _k_tile(t):
    return 1664 if t % 1664 == 0 else ROW_TILE


def _place():
    x, y, c = lax.axis_index("x"), lax.axis_index("y"), lax.axis_index("c")
    chips = [(1 - x, y), (x, 1 - y), (1 - x, 1 - y)]
    return x, y, c, chips


def _all_gather(shards):
    n = len(shards)
    split = [s.reshape(2, s.shape[0] // 2, s.shape[1]) for s in shards]

    def body(*refs):
        ins, outs = refs[:n], refs[n:2 * n]
        send_sems, recv_sems = refs[2 * n:]
        x, y, c, chips = _place()
        me = 2 * x + y
        sibling = (x, y, 1 - c)

        def remote(i, k, slot, part, to, src=None):
            dst = outs[i].at[slot, part]
            return pltpu.make_async_remote_copy(
                src_ref=dst if src is None else src, dst_ref=dst,
                send_sem=send_sems.at[i, k], recv_sem=recv_sems.at[i, k],
                device_id=to, device_id_type=MESH)

        started = []
        for i in range(n):
            for k, (cx, cy) in enumerate(chips):
                cp = remote(i, k, me, c, (cx, cy, c), src=ins[i].at[c])
                cp.start()
                started.append(cp)
        for i in range(n):
            for k, (cx, cy) in enumerate(chips):
                remote(i, k, 2 * cx + cy, c, (x, y, c)).wait_recv()
                cp = remote(i, 3 + k, 2 * cx + cy, c, sibling)
                cp.start()
                started.append(cp)
        for i in range(n):
            for k, (cx, cy) in enumerate(chips):
                remote(i, 3 + k, 2 * cx + cy, 1 - c, (x, y, c)).wait_recv()
        for cp in started:
            cp.wait_send()

    outs = pl.pallas_call(
        body, name="all_gather_weights",
        out_shape=[jax.ShapeDtypeStruct((4,) + s.shape, s.dtype) for s in split],
        in_specs=[ANY] * n, out_specs=[ANY] * n,
        scratch_shapes=[pltpu.SemaphoreType.DMA((n, 6)), pltpu.SemaphoreType.DMA((n, 6))],
    )(*split)
    me =2 * lax.axis_index("x") + lax.axis_index("y")
    outs = [lax.dynamic_update_slice(o, s[None], (me, 0, 0, 0)) for o, s in zip(outs, split)]
    return [o.reshape((4,) + s.shape) for o, s in zip(outs, shards)]


def _all_gather_async(name, shards, collective_id):
    n = len(shards)
    split = [s.reshape(2, s.shape[0] // 2, s.shape[1]) for s in shards]
    ins = [jax.new_ref(s, memory_space=pltpu.MemorySpace.HBM) for s in split]
    outs = [jax.empty_ref(jax.ShapeDtypeStruct((4,) + s.shape, s.dtype), memory_space=pltpu.MemorySpace.HBM)
            for s in split]

    @pl.kernel(mesh=plsc.ScalarSubcoreMesh(axis_name="sequencer", num_cores=1), name=name,
               scratch_types=(pltpu.SemaphoreType.DMA((n, 6)), pltpu.SemaphoreType.DMA((n, 6))),
               compiler_params=pltpu.CompilerParams(collective_id=collective_id))
    def launch(send_sems, recv_sems):
        x, y, c, chips = _place()
        me = 2 * x + y
        sibling = (x, y, 1 - c)
        barrier = pltpu.get_barrier_semaphore()
        for peer in [(cx, cy, c) for cx, cy in chips] + [sibling]:
            pl.semaphore_signal(barrier, inc=1, device_id=peer, device_id_type=MESH)
        pl.semaphore_wait(barrier, 4)

        def remote(i, k, slot, part, to, src=None):
            dst = outs[i].at[slot, part]
            return pltpu.make_async_remote_copy(
                src_ref=dst if src is None else src, dst_ref=dst,
                send_sem=send_sems.at[i, k], recv_sem=recv_sems.at[i, k],
                device_id=to, device_id_type=MESH)

        started = []
        for i in range(n):
            for k, (cx, cy) in enumerate(chips):
                cp = remote(i, k, me, c, (cx, cy, c), src=ins[i].at[c])
                cp.start()
                started.append(cp)
        for i in range(n):
            for k, (cx, cy) in enumerate(chips):
                remote(i, k, 2 * cx + cy, c, (x, y, c)).wait_recv()
                cp = remote(i, 3 + k, 2 * cx + cy, c, sibling)
                cp.start()
                started.append(cp)
        for i in range(n):
            for k, (cx, cy) in enumerate(chips):
                remote(i, 3 + k, 2 * cx + cy, 1 - c, (x, y, c)).wait_recv()
        for cp in started:
            cp.wait_send()

    launch()
    raw = [o[...] for o in outs]

    def finish(after, which):
        arrived, _ = lax.optimization_barrier(([raw[i] for i in which], after))
        me = 2 * lax.axis_index("x") + lax.axis_index("y")
        gathered = [lax.dynamic_update_slice(a, split[i][None], (me, 0, 0, 0)) for a, i in zip(arrived, which)]
        return [g.reshape((4,) + shards[i].shape) for g, i in zip(gathered, which)]

    return finish


def _pair_send_halves(name, grads):
    n = len(grads)

    def body(*refs):
        ins, outs = refs[:n], refs[n:2 * n]
        send_sems, recv_sems = refs[2 * n:]
        x, y, c, _ = _place()
        cps = []
        for i in range(n):
            half = ins[i].shape[1] // 2
            cp = pltpu.make_async_remote_copy(
                src_ref=ins[i].at[:, pl.ds((1 - c) * half, half)], dst_ref=outs[i],
                send_sem=send_sems.at[i], recv_sem=recv_sems.at[i],
                device_id=(x, y, 1 - c), device_id_type=MESH)
            cp.start()
            cps.append(cp)
        for cp in cps:
            cp.wait()

    return pl.pallas_call(
        body, name=name,
        out_shape=[jax.ShapeDtypeStruct((4, g.shape[1] // 2, g.shape[2]), g.dtype) for g in grads],
        in_specs=[ANY] * n, out_specs=[ANY] * n,
        scratch_shapes=[pltpu.SemaphoreType.DMA((n,)), pltpu.SemaphoreType.DMA((n,))],
    )(*grads)


def _pair_send_halves_async(name, grads, collective_id):
    n = len(grads)
    ins = [jax.new_ref(g, memory_space=pltpu.MemorySpace.HBM) for g in grads]
    outs = [jax.empty_ref(jax.ShapeDtypeStruct((4, g.shape[1] // 2, g.shape[2]), g.dtype),
                          memory_space=pltpu.MemorySpace.HBM) for g in grads]

    @pl.kernel(mesh=plsc.ScalarSubcoreMesh(axis_name="sequencer", num_cores=1), name=name,
               scratch_types=(pltpu.SemaphoreType.DMA((n,)), pltpu.SemaphoreType.DMA((n,))),
               compiler_params=pltpu.CompilerParams(collective_id=collective_id))
    def launch(send_sems, recv_sems):
        x, y, c, _ = _place()
        barrier = pltpu.get_barrier_semaphore()
        pl.semaphore_signal(barrier, inc=1, device_id=(x, y, 1 - c), device_id_type=MESH)
        pl.semaphore_wait(barrier, 1)
        cps = []
        for i in range(n):
            half = ins[i].shape[1] // 2
            cp = pltpu.make_async_remote_copy(
                src_ref=ins[i].at[:, pl.ds((1 - c) * half, half)], dst_ref=outs[i],
                send_sem=send_sems.at[i], recv_sem=recv_sems.at[i],
                device_id=(x, y, 1 - c), device_id_type=MESH)
            cp.start()
            cps.append(cp)
        for cp in cps:
            cp.wait()

    launch()
    return [o[...] for o in outs]


def _chip_scatter(name, parts):
    n = len(parts)

    def body(*refs):
        _scatter_copies(refs[:n], refs[n:2 * n], *refs[2 * n:])

    arrived = pl.pallas_call(
        body, name=name,
        out_shape=[jax.ShapeDtypeStruct(p.shape, p.dtype) for p in parts],
        in_specs=[ANY] * n, out_specs=[ANY] * n,
        scratch_shapes=[pltpu.SemaphoreType.DMA((n, 3)), pltpu.SemaphoreType.DMA((n, 3))],
    )(*parts)
    return _own_slots(parts, arrived)


def _scatter_copies(ins, outs, send_sems, recv_sems):
    x, y, c, chips = _place()
    me = 2 * x + y
    sends = []
    for i in range(len(ins)):
        for k, (cx, cy) in enumerate(chips):
            cp = pltpu.make_async_remote_copy(
                src_ref=ins[i].at[2 * cx + cy], dst_ref=outs[i].at[me],
                send_sem=send_sems.at[i, k], recv_sem=recv_sems.at[i, k],
                device_id=(cx, cy, c), device_id_type=MESH)
            cp.start()
            sends.append(cp)
    for i in range(len(ins)):
        for k, (cx, cy) in enumerate(chips):
            got = outs[i].at[2 * cx + cy]
            pltpu.make_async_remote_copy(
                src_ref=got, dst_ref=got, send_sem=send_sems.at[i, k], recv_sem=recv_sems.at[i, k],
                device_id=(x, y, c), device_id_type=MESH).wait_recv()
    for cp in sends:
        cp.wait_send()


def _own_slots(parts, arrived):
    me = 2 * lax.axis_index("x") + lax.axis_index("y")
    return [lax.dynamic_update_slice(a, lax.dynamic_slice_in_dim(p, me, 1, axis=0), (me, 0, 0))
            for p, a in zip(parts, arrived)]


def _chip_scatter_async(name, parts, collective_id):
    n = len(parts)
    ins = [jax.new_ref(p, memory_space=pltpu.MemorySpace.HBM) for p in parts]
    outs = [jax.empty_ref(jax.ShapeDtypeStruct(p.shape, p.dtype), memory_space=pltpu.MemorySpace.HBM) for p in parts]

    @pl.kernel(mesh=plsc.ScalarSubcoreMesh(axis_name="sequencer", num_cores=1), name=name,
               scratch_types=(pltpu.SemaphoreType.DMA((n, 3)), pltpu.SemaphoreType.DMA((n, 3))),
               compiler_params=pltpu.CompilerParams(collective_id=collective_id))
    def launch(send_sems, recv_sems):
        x, y, c, chips = _place()
        barrier = pltpu.get_barrier_semaphore()
        for cx, cy in chips:
            pl.semaphore_signal(barrier, inc=1, device_id=(cx, cy, c), device_id_type=MESH)
        pl.semaphore_wait(barrier, 3)
        _scatter_copies(ins, outs, send_sems, recv_sems)

    launch()
    return _own_slots(parts, [o[...] for o in outs])


def _pair_swap(name, halves):
    n = len(halves)

    def body(*refs):
        ins, outs = refs[:n], refs[n:2 * n]
        send_sems, recv_sems = refs[2 * n:]
        x, y, c, _ = _place()
        cps = []
        for i in range(n):
            cp = pltpu.make_async_remote_copy(
                src_ref=ins[i], dst_ref=outs[i], send_sem=send_sems.at[i], recv_sem=recv_sems.at[i],
                device_id=(x, y, 1 - c), device_id_type=MESH)
            cp.start()
            cps.append(cp)
        for cp in cps:
            cp.wait()

    return pl.pallas_call(
        body, name=name,
        out_shape=[jax.ShapeDtypeStruct(h.shape, h.dtype) for h in halves],
        in_specs=[ANY] * n, out_specs=[ANY] * n,
        scratch_shapes=[pltpu.SemaphoreType.DMA((n,)), pltpu.SemaphoreType.DMA((n,))],
    )(*halves)


def _row_block(rows, cols, n_bufs, budget=20 * 1024 * 1024):
    best = min(rows, 16)
    for b in range(16, rows + 1, 16):
        if rows % b == 0 and 2 * n_bufs * b * cols * 4 <= budget:
            best = b
    return best


def _pair_add(tag, grad, got, c_arr, out_dtype):
    _, rows, cols = grad.shape
    half = rows // 2
    bh = _row_block(half, cols, 3)
    nb = half // bh

    def body(c_ref, g_ref, a_ref, o_ref):
        o_ref[...] = (g_ref[...] + a_ref[...]).astype(out_dtype)

    return pl.pallas_call(
        body, name=f"pair_add_{tag}",
        out_shape=jax.ShapeDtypeStruct((4, half, cols), out_dtype),
        grid_spec=pltpu.PrefetchScalarGridSpec(
            num_scalar_prefetch=1, grid=(4, nb),
            in_specs=[pl.BlockSpec((None, bh, cols), lambda j, r, c: (j, c[0] * nb + r, 0)),
                      pl.BlockSpec((None, bh, cols), lambda j, r, c: (j, r, 0))],
            out_specs=pl.BlockSpec((None, bh, cols), lambda j, r, c: (j, r, 0))),
        compiler_params=_params(("parallel", "parallel")),
    )(c_arr, grad, got)


def _chip_add(tag, parts):
    _, half, cols = parts.shape
    bh = _row_block(half, cols, 5)

    def body(p_ref, o_ref):
        a, b, c, d = [p_ref[j].astype(F32) for j in range(4)]
        o_ref[...] = ((a + b) + c) + d

    return pl.pallas_call(
        body, name=f"chip_add_{tag}",
        out_shape=jax.ShapeDtypeStruct((half, cols), F32),
        grid=(half // bh,),
        in_specs=[pl.BlockSpec((4, bh, cols), lambda r: (0, r, 0))],
        out_specs=pl.BlockSpec((bh, cols), lambda r: (r, 0)),
        compiler_params=_params(("parallel",)),
    )(parts)


def _adamw(tag, w, mine, theirs, m, v, c_arr):
    rows, cols = w.shape
    half = rows // 2
    br = _row_block(half, cols, 9)
    nb = half // br

    def body(c_ref, w_ref, a_ref, b_ref, m_ref, v_ref, g_ref, d_ref, mo_ref, vo_ref):
        own = (pl.program_id(0) // nb) == c_ref[0]
        g = jnp.where(own, a_ref[...], b_ref[...])
        g_ref[...] = g
        m_new = ADAM_B1 * m_ref[...] + (1.0 - ADAM_B1) * g
        v_new = ADAM_B2 * v_ref[...] + (1.0 - ADAM_B2) * (g * g)
        m_hat = m_new / (1.0 - ADAM_B1 ** ADAM_STEP)
        v_hat = v_new / (1.0 - ADAM_B2 ** ADAM_STEP)
        d_ref[...] = -ADAM_LR * (m_hat / (jnp.sqrt(v_hat) + ADAM_EPS) + ADAM_WD * w_ref[...])
        mo_ref[...] = m_new
        vo_ref[...] = v_new

    spec = pl.BlockSpec((br, cols), lambda r, c: (r, 0))
    mine_spec = pl.BlockSpec((br, cols), lambda r, c: (jnp.clip(r - c[0] * nb, 0, nb - 1), 0))
    theirs_spec = pl.BlockSpec((br, cols), lambda r, c: (jnp.clip(r - (1 - c[0]) * nb, 0, nb - 1), 0))
    return pl.pallas_call(
        body, name=f"adamw_{tag}",
        out_shape=[jax.ShapeDtypeStruct((rows, cols), F32)] * 4,
        grid_spec=pltpu.PrefetchScalarGridSpec(
            num_scalar_prefetch=1, grid=(rows // br,),
            in_specs=[spec, mine_spec, theirs_spec, spec, spec], out_specs=[spec] * 4),
        compiler_params=_params(("arbitrary",)),
    )(c_arr, w, mine, theirs, m, v)


def _matmul(name, x, w, out_shape, grid, x_spec, w_spec, o_spec, *, nt=False, vmem=None):
    nk = grid[2]
    acc_shape = tuple(d for d in o_spec.block_shape if d is not None)

    def body(x_ref, w_ref, o_ref, acc_ref):
        k = pl.program_id(2)
        part = _dot_nt(x_ref[...], w_ref[...]) if nt else _dot(x_ref[...], w_ref[...])
        if nk == 1:
            o_ref[...] = part.astype(o_ref.dtype)
        else:
            @pl.when(k == 0)
            def _():
                acc_ref[...] = part

            @pl.when(k > 0)
            def _():
                acc_ref[...] += part

            @pl.when(k == nk - 1)
            def _():
                o_ref[...] = acc_ref[...].astype(o_ref.dtype)

    return pl.pallas_call(
        body, name=name, out_shape=out_shape, grid=grid,
        in_specs=[x_spec, w_spec], out_specs=o_spec,
        scratch_shapes=[pltpu.VMEM(acc_shape if nk > 1 else (8, 128), F32)],
        compiler_params=_params(("parallel", "parallel", "arbitrary"), vmem),
    )(x, w)


def _weight_grad(name, xt, dy, bn, out_rows=None):
    m, t = xt.shape
    n = dy.shape[1]
    bm = m if out_rows is None else out_rows
    bk = _k_tile(t)
    return _matmul(
        name, xt, dy, jax.ShapeDtypeStruct((m, n), F32), (m // bm, n // bn, t // bk),
        pl.BlockSpec((bm, bk), lambda a, b, k: (a, k)),
        pl.BlockSpec((bk, bn), lambda a, b, k: (k, b)),
        pl.BlockSpec((bm, bn), lambda a, b, k: (a, b)), vmem=VMEM_BIG)


def _weight_grad_t(name, xt, dy):
    m, t = xt.shape
    n = dy.shape[1]
    bn = min(n, 512)
    bk = _k_tile(t)
    nk = t // bk

    def body(x_ref, dy_ref, o_ref, acc_ref):
        k = pl.program_id(1)
        part = _dot(x_ref[...], dy_ref[...].astype(BF16))

        @pl.when(k == 0)
        def _():
            acc_ref[...] = part

        @pl.when(k > 0)
        def _():
            acc_ref[...] += part

        @pl.when(k == nk - 1)
        def _():
            o_ref[...] = acc_ref[...].T

    return pl.pallas_call(
        body, name=name, out_shape=jax.ShapeDtypeStruct((n, m), F32), grid=(n // bn, nk),
        in_specs=[pl.BlockSpec((m, bk), lambda b, k: (0, k)), pl.BlockSpec((bk, bn), lambda b, k: (k, b))],
        out_specs=pl.BlockSpec((bn, m), lambda b, k: (b, 0)),
        scratch_shapes=[pltpu.VMEM((m, bn), F32)],
        compiler_params=_params(("parallel", "arbitrary"), VMEM_BIG),
    )(xt, dy)


def _mix_in_bwd(pieces, wt, h, g, dh_in):
    t, d = h.shape
    tm = ROW_TILE // 2
    widths = [p.shape[1] for p in pieces]
    n = len(pieces)

    def body(*refs):
        dy_refs, (w_ref, h_ref, g_ref, dhi_ref, dh_ref, dg_ref) = refs[:n], refs[n:]

        @pl.when(pl.program_id(0) == 0)
        def _():
            dg_ref[...] = jnp.zeros_like(dg_ref)

        dn, off = None, 0
        for dy_ref, wd in zip(dy_refs, widths):
            part = _dot(dy_ref[...].astype(BF16), w_ref[off:off + wd, :])
            dn = part if dn is None else dn + part
            off += wd
        dx, dg = _rms_bwd(h_ref[...], g_ref[...], dn)
        dh_ref[...] = dhi_ref[...] + dx
        dg_ref[...] += dg

    row = pl.BlockSpec((tm, d), lambda i: (i, 0))
    vec = pl.BlockSpec((1, d), lambda i: (0, 0))
    return pl.pallas_call(
        body, name="mix_in_bwd",
        out_shape=[jax.ShapeDtypeStruct((t, d), F32), jax.ShapeDtypeStruct((1, d), F32)],
        grid=(t // tm,),
        in_specs=[pl.BlockSpec((tm, wd), lambda i: (i, 0)) for wd in widths]
        + [pl.BlockSpec(wt.shape, lambda i: (0, 0)), row, vec, row],
        out_specs=[row, vec],
        compiler_params=_params(("arbitrary",), VMEM_BIG),
    )(*pieces, wt, h, g, dh_in)


def _norm_fwd(name, h, g):
    t, d = h.shape
    tm = ROW_TILE

    def body(h_ref, g_ref, n_ref, nt_ref):
        x = h_ref[...]
        y = x * _rstd(x) * g_ref[...]
        n_ref[...] = y.astype(BF16)
        nt_ref[...] = y.T.astype(BF16)

    return pl.pallas_call(
        body, name=name,
        out_shape=[jax.ShapeDtypeStruct((t, d), BF16), jax.ShapeDtypeStruct((d, t), BF16)],
        grid=(t // tm,),
        in_specs=[pl.BlockSpec((tm, d), lambda i: (i, 0)), pl.BlockSpec((1, d), lambda i: (0, 0))],
        out_specs=[pl.BlockSpec((tm, d), lambda i: (i, 0)), pl.BlockSpec((d, tm), lambda i: (0, i))],
        compiler_params=_params(("parallel",)),
    )(h, g)


def _slot_of(kk):
    return (kk % 2) * 2 + kk // 2


def _ffn_in(name, n, w4):
    t, d = n.shape
    cw = w4.shape[2]
    tm = ROW_TILE

    def body(x_ref, wg_ref, wu_ref, ab_ref, s_ref, st_ref):
        x = x_ref[...]
        a = _dot(x, wg_ref[...])
        b = _dot(x, wu_ref[...])
        ab_ref[:, :cw] = a.astype(BF16)
        ab_ref[:, cw:] = b.astype(BF16)
        s = a * _sigmoid(a) * b
        s_ref[...] = s.astype(BF16)
        st_ref[...] = s.T.astype(BF16)

    return pl.pallas_call(
        body, name=name,
        out_shape=[jax.ShapeDtypeStruct((t, 4 * cw), BF16), jax.ShapeDtypeStruct((t, 2 * cw), BF16),
                   jax.ShapeDtypeStruct((2 * cw, t), BF16)],
        grid=(2, t // tm),
        in_specs=[pl.BlockSpec((tm, d), lambda j, i: (i, 0)),
                  pl.BlockSpec((None, d, cw), lambda j, i: (j, 0, 0)),
                  pl.BlockSpec((None, d, cw), lambda j, i: (2 + j, 0, 0))],
        out_specs=[pl.BlockSpec((tm, 2 * cw), lambda j, i: (i, j)),
                   pl.BlockSpec((tm, cw), lambda j, i: (i, j)),
                   pl.BlockSpec((cw, tm), lambda j, i: (j, i))],
        compiler_params=_params(("parallel", "parallel"), VMEM_BIG),
    )(n, w4, w4)


def _mm_resid_norm(name, x, w, h, g_post, alpha, g_next):
    t, kdim = x.shape
    d = w.shape[1]
    tm = ROW_TILE
    with_next = g_next is not None

    def body(x_ref, w_ref, h_ref, gp_ref, gn_ref, f_ref, hn_ref, *rest):
        f = _dot(x_ref[...], w_ref[...])
        f_ref[...] = f
        hn = h_ref[...] + alpha * (f * _rstd(f) * gp_ref[...])
        hn_ref[...] = hn
        if with_next:
            y = hn * _rstd(hn) * gn_ref[...]
            rest[0][...] = y.astype(BF16)
            rest[1][...] = y.T.astype(BF16)

    row = lambda i: (i, 0)
    vec = pl.BlockSpec((1, d), lambda i: (0, 0))
    out_shape = [jax.ShapeDtypeStruct((t, d), F32), jax.ShapeDtypeStruct((t, d), F32)]
    out_specs = [pl.BlockSpec((tm, d), row), pl.BlockSpec((tm, d), row)]
    if with_next:
        out_shape += [jax.ShapeDtypeStruct((t, d), BF16), jax.ShapeDtypeStruct((d, t), BF16)]
        out_specs += [pl.BlockSpec((tm, d), row), pl.BlockSpec((d, tm), lambda i: (0, i))]
    return pl.pallas_call(
        body, name=name, out_shape=out_shape, grid=(t // tm,),
        in_specs=[pl.BlockSpec((tm, kdim), row), pl.BlockSpec((kdim, d), lambda i: (0, 0)),
                  pl.BlockSpec((tm, d), row), vec, vec],
        out_specs=out_specs,
        compiler_params=_params(("parallel",), VMEM_BIG),
    )(x, w, h, g_post, g_post if g_next is None else g_next)


def _in_proj(u, w):
    t, d = u.shape
    nz = w.shape[0]
    nq = 3 * ATTN_W
    tm = ROW_TILE // 2

    def body(u_ref, w_ref, qkv_ref, z_ref):
        qkv_ref[...] = _dot_nt(u_ref[...], w_ref[0:nq, :]).astype(BF16)
        z_ref[...] = _dot_nt(u_ref[...], w_ref[nq:, :])

    return pl.pallas_call(
        body, name="mix_in_proj",
        out_shape=[jax.ShapeDtypeStruct((t, nq), BF16), jax.ShapeDtypeStruct((t, nz - nq), F32)],
        grid=(t // tm,),
        in_specs=[pl.BlockSpec((tm, d), lambda i: (i, 0)), pl.BlockSpec((nz, d), lambda i: (0, 0))],
        out_specs=[pl.BlockSpec((tm, nq), lambda i: (i, 0)), pl.BlockSpec((tm, nz - nq), lambda i: (i, 0))],
        compiler_params=_params(("parallel",), VMEM_BIG),
    )(u, w)


def _gate_prep(z, b_pad, f_col):
    t = z.shape[0]
    tm = ROW_TILE

    def body(z_ref, b_ref, f_ref, carry_ref):
        i = pl.program_id(0)

        @pl.when(i == 0)
        def _():
            carry_ref[...] = jnp.zeros_like(carry_ref)

        xs = z_ref[...] + b_ref[...]
        logf = jnp.minimum(xs, 0.0) - jnp.log(1.0 + jnp.exp(-jnp.abs(xs)))
        row = i * tm + lax.broadcasted_iota(jnp.int32, (tm, 1), 0)
        logf = jnp.where(row >= ROW_PAD, logf, 0.0)
        tri = (lax.broadcasted_iota(jnp.int32, (tm, tm), 0) >= lax.broadcasted_iota(jnp.int32, (tm, tm), 1))
        f = jnp.dot(tri.astype(F32), logf, preferred_element_type=F32, precision=lax.Precision.HIGHEST)
        f = f + carry_ref[0:1, :]
        f_ref[...] = f
        carry_ref[...] = jnp.broadcast_to(f[tm - 1:tm, :], carry_ref.shape)

    return pl.pallas_call(
        body, name="forget_gate_cumsum", out_shape=jax.ShapeDtypeStruct((t, 128), F32),
        grid=(t // tm,),
        in_specs=[pl.BlockSpec((tm, 128), lambda i: (i, f_col // 128)), pl.BlockSpec((1, 128), lambda i: (0, 0))],
        out_specs=pl.BlockSpec((tm, 128), lambda i: (i, 0)),
        scratch_shapes=[pltpu.VMEM((8, 128), F32)],
        compiler_params=_params(("arbitrary",)),
    )(z, b_pad)


def _lane_halves():
    lane = lax.broadcasted_iota(jnp.int32, (1, 128), 1)
    return lane < HEAD_DIM


def _causal_mask(tq, tk, row0=0):
    row = row0 + lax.broadcasted_iota(jnp.int32, (tq, 1), 0)
    col = lax.broadcasted_iota(jnp.int32, (1, tk), 1)
    return col <= row


def _lane_one(lane):
    return (lax.broadcasted_iota(jnp.int32, (1, 128), 1) == lane).astype(BF16)


def _split3(x):
    hi = x.astype(BF16)
    rest = x - hi.astype(F32)
    mid = rest.astype(BF16)
    return hi, mid, (rest - mid.astype(F32)).astype(BF16)


def _split3_glue(x):
    hi = lax.reduce_precision(x, 8, 7)
    mid = lax.reduce_precision(x - hi, 8, 7)
    lo = lax.reduce_precision((x - hi) - mid, 8, 7)
    return hi.astype(BF16), mid.astype(BF16), lo.astype(BF16)


def _aug_pairs(cols):
    t = cols[0].shape[0]
    a = jnp.pad(jnp.stack(cols, axis=2), ((0, 0), (0, 0), (0, HEAD_DIM - len(cols))))
    a = a.reshape(t, 4, 2, HEAD_DIM)[:, :, ::-1, :]
    return jnp.transpose(a.reshape(t, 4, 128), (1, 0, 2))


def _attn_bias_operands(f_heads, lse_heads=None):
    t = f_heads.shape[0]
    one = jnp.ones((t, HEADS), BF16)
    row = lax.broadcasted_iota(jnp.int32, (t, 1), 0)
    fq = _split3_glue(f_heads)
    fk = _split3_glue(jnp.where(row < ROW_PAD, 1e9, f_heads))
    q_cols = list(fq) + [one] * 3
    k_cols = [one] * 3 + [-c for c in fk]
    if lse_heads is not None:
        q_cols += [-c for c in _split3_glue(lse_heads)]
        k_cols += [one] * 3
    return _aug_pairs(q_cols), _aug_pairs(k_cols)


def _attn_steps(nq, by_key):
    if by_key:
        pairs = [(qi, ki) for ki in range(nq) for qi in range(ki, nq)]
    else:
        pairs = [(qi, ki) for qi in range(nq) for ki in range(qi + 1)]
    return (jnp.array([p[0] for p in pairs], jnp.int32), jnp.array([p[1] for p in pairs], jnp.int32))


def _attn_fwd(z, aug_q, aug_k):
    t = z.shape[0]
    tq = tk = ROW_TILE
    nq = t // tq
    grp = ATTN_KV_GROUP
    steps = [(qi, ka) for qi in range(nq) for ka in range(0, qi + 1, grp)]
    q_tab = jnp.array([qi for qi, _ in steps], jnp.int32)
    k_tab = jnp.array([ka for _, ka in steps], jnp.int32)

    def body(qt_ref, kt_ref, q_ref, *refs):
        k_refs, v_refs, aq_ref, ak_refs = refs[:grp], refs[grp:2 * grp], refs[2 * grp], refs[2 * grp + 1:3 * grp + 1]
        o_ref, lse_ref, m_ref, l_ref, acc_ref = refs[3 * grp + 1:]
        step = pl.program_id(1)
        qi, ka = qt_ref[step], kt_ref[step]

        @pl.when(ka == 0)
        def _():
            m_ref[...] = jnp.full_like(m_ref, NEG)
            l_ref[...] = jnp.zeros_like(l_ref)
            acc_ref[...] = jnp.zeros_like(acc_ref)

        def sweep(diagonal):
            first = _lane_halves()
            halves = (first, jnp.logical_not(first))
            q = (q_ref[...] * (HEAD_DIM ** -0.5)).astype(BF16)
            aq = aq_ref[...]
            qa = [jnp.where(lanes, q, aq) for lanes in halves]
            blocks = list(zip(k_refs, v_refs, ak_refs, diagonal))
            s = []
            for k_ref, _, ak_ref, diag in blocks:
                k, ak = k_ref[...].astype(BF16), ak_ref[...]
                for hh, lanes in enumerate(halves):
                    s_c = _dot_nt(qa[hh], jnp.where(lanes, k, ak))
                    s.append(jnp.where(_causal_mask(tq, tk), s_c, NEG) if diag else s_c)
            nb = len(blocks)
            m_prev = [m_ref[:, c0:c0 + 1] for c0 in (0, HEAD_DIM)]
            m_new = []
            for hh in range(2):
                m_h = m_prev[hh]
                for b in range(nb):
                    m_h = jnp.maximum(m_h, jnp.max(s[2 * b + hh], axis=1, keepdims=True))
                m_new.append(m_h)
            pv = [None, None]
            for b, (_, v_ref, _, _) in enumerate(blocks):
                v = v_ref[...].astype(BF16)
                for hh, (lanes, a0) in enumerate(zip(halves, (HEAD_DIM, 0))):
                    part = _dot(jnp.exp(s[2 * b + hh] - m_new[hh]).astype(BF16), jnp.where(lanes, v, _lane_one(a0)))
                    pv[hh] = part if pv[hh] is None else pv[hh] + part
            al0, al1 = [jnp.exp(mp - m_h) for mp, m_h in zip(m_prev, m_new)]
            l0 = al0 * l_ref[:, 0:1] + pv[0][:, HEAD_DIM:HEAD_DIM + 1]
            l1 = al1 * l_ref[:, HEAD_DIM:HEAD_DIM + 1] + pv[1][:, 0:1]
            acc_ref[...] = acc_ref[...] * jnp.where(first, al0, al1) + jnp.where(first, pv[0], pv[1])
            m_ref[...] = jnp.where(first, m_new[0], m_new[1])
            l_ref[...] = jnp.where(first, l0, l1)

        def finish():
            o_ref[...] = acc_ref[...] / l_ref[...]
            lse_ref[...] = m_ref[...] + jnp.log(l_ref[...])

        @pl.when(ka + grp - 1 < qi)
        def _():
            sweep((False,) * grp)

        for nb in range(1, grp + 1):
            @pl.when(ka + nb - 1 == qi)
            def _(nb=nb):
                sweep((False,) * (nb - 1) + (True,))
                finish()

    def kblock(j):
        return lambda s, qt, kt: jnp.minimum(kt[s] + j, qt[s])

    kbs = [kblock(j) for j in range(grp)]
    return pl.pallas_call(
        body, name="attention_fwd",
        out_shape=[jax.ShapeDtypeStruct((t, ATTN_W), F32), jax.ShapeDtypeStruct((t, ATTN_W), F32)],
        grid_spec=pltpu.PrefetchScalarGridSpec(
            num_scalar_prefetch=2, grid=(4, len(steps)),
            in_specs=[pl.BlockSpec((tq, 128), lambda p, s, qt, kt: (qt[s], p))]
            + [pl.BlockSpec((tk, 128), functools.partial(lambda p, s, qt, kt, kb: (kb(s, qt, kt), 4 + p), kb=kb))
               for kb in kbs]
            + [pl.BlockSpec((tk, 128), functools.partial(lambda p, s, qt, kt, kb: (kb(s, qt, kt), 8 + p), kb=kb))
               for kb in kbs]
            + [pl.BlockSpec((None, tq, 128), lambda p, s, qt, kt: (p, qt[s], 0))]
            + [pl.BlockSpec((None, tk, 128), functools.partial(lambda p, s, qt, kt, kb: (p, kb(s, qt, kt), 0), kb=kb))
               for kb in kbs],
            out_specs=[pl.BlockSpec((tq, 128), lambda p, s, qt, kt: (qt[s], p)),
                       pl.BlockSpec((tq, 128), lambda p, s, qt, kt: (qt[s], p))],
            scratch_shapes=[pltpu.VMEM((tq, 128), F32)] * 3),
        compiler_params=_params(("parallel", "arbitrary"), VMEM_BIG),
    )(q_tab, k_tab, z, *([z] * (2 * grp)), aug_q, *([aug_k] * grp))


def _attn_bwd(z, aug_q, aug_k, o, do):
    t = z.shape[0]
    tq = tk = ROW_TILE
    nq = t // tq
    grp = ATTN_Q_GROUP
    steps = [(qa, ki) for ki in range(nq) for qa in range(ki, nq, grp)]
    q_tab = jnp.array([qa for qa, _ in steps], jnp.int32)
    k_tab = jnp.array([ki for _, ki in steps], jnp.int32)
    tn = (((0,), (0,)), ((), ()))

    def body(qt_ref, kt_ref, *refs):
        q_refs, (k_ref, v_ref) = refs[:grp], refs[grp:grp + 2]
        aq_refs, ak_ref = refs[grp + 2:2 * grp + 2], refs[2 * grp + 2]
        o_refs, do_refs = refs[2 * grp + 3:3 * grp + 3], refs[3 * grp + 3:4 * grp + 3]
        dq_ref, dk_ref, dv_ref, dfk_ref, dfq_ref = refs[4 * grp + 3:]
        step = pl.program_id(1)
        qa, ki = qt_ref[step], kt_ref[step]

        def rows(j):
            return pl.ds(pl.multiple_of((qa + j) * tq, tq), tq)

        for j in range(grp):
            @pl.when((ki == 0) & (qa + j < nq))
            def _(j=j):
                dq_ref[rows(j), :] = jnp.zeros((tq, 128), F32)
                dfq_ref[rows(j), :] = jnp.zeros((tq, 128), F32)

        @pl.when(qa == ki)
        def _():
            dk_ref[...] = jnp.zeros_like(dk_ref)
            dv_ref[...] = jnp.zeros_like(dv_ref)
            dfk_ref[...] = jnp.zeros_like(dfk_ref)

        def sweep(nb, diagonal):
            first = _lane_halves()
            lane = lax.broadcasted_iota(jnp.int32, (1, 128), 1)
            scale = HEAD_DIM ** -0.5
            halves = (first, jnp.logical_not(first))
            spare = (HEAD_DIM, 0)
            k = k_ref[...].astype(BF16)
            v = v_ref[...].astype(BF16)
            ak = ak_ref[...]
            k_bias = [jnp.where(lanes, k, ak) for lanes in halves]
            k_ones = [jnp.where(lanes, k, _lane_one(a)) for lanes, a in zip(halves, spare)]
            v_ones = [jnp.where(lanes, v, ((lane >= a) & (lane < a + 3)).astype(BF16)) for lanes, a in zip(halves, spare)]
            chains = [(j, hh) for j in range(nb) for hh in range(2)]
            q16, do16, dos = [], [], []
            for j in range(nb):
                q16.append((q_refs[j][...] * scale).astype(BF16))
                do_ = do_refs[j][...]
                do16.append(do_.astype(BF16))
                od = o_refs[j][...] * do_
                for lanes, a in zip(halves, spare):
                    d_hi, d_mid, d_lo = _split3(jnp.sum(jnp.where(lanes, od, 0.0), axis=1, keepdims=True))
                    minus_delta = jnp.where(lane == a, -d_hi, jnp.where(lane == a + 1, -d_mid,
                                            jnp.where(lane == a + 2, -d_lo, jnp.zeros((), BF16))))
                    dos.append(jnp.where(lanes, do16[j], minus_delta))
            s = [_dot_nt(jnp.where(halves[hh], q16[j], aq_refs[j][...]), k_bias[hh]) for j, hh in chains]
            dp = [_dot_nt(dos[2 * j + hh], v_ones[hh]) for j, hh in chains]
            p = [jnp.exp(s_c) for s_c in s]
            if diagonal:
                p = [jnp.where(_causal_mask(tq, tk), p_c, 0.0) if j == 0 else p_c for p_c, (j, _) in zip(p, chains)]
            ds16 = [(p_c * dp_c).astype(BF16) for p_c, dp_c in zip(p, dp)]
            dv, dk = [None, None], [None, None]
            for c, (j, hh) in enumerate(chains):
                lanes = halves[hh]
                dv_c = lax.dot_general(p[c].astype(BF16), jnp.where(lanes, do16[j], jnp.zeros((), BF16)), tn,
                                       preferred_element_type=F32)
                dk_c = lax.dot_general(ds16[c], jnp.where(lanes, q16[j], _lane_one(spare[hh])), tn,
                                       preferred_element_type=F32)
                dv[hh] = dv_c if dv[hh] is None else dv[hh] + dv_c
                dk[hh] = dk_c if dk[hh] is None else dk[hh] + dk_c
            for j in range(nb):
                dq0, dq1 = [_dot(ds16[2 * j + hh], k_ones[hh]) for hh in range(2)]
                dq_ref[rows(j), :] += jnp.where(first, dq0, dq1) * scale
                dfq_ref[rows(j), :] += jnp.where(first, dq0[:, HEAD_DIM:HEAD_DIM + 1], dq1[:, 0:1])
            dk_ref[...] += jnp.where(first, dk[0], dk[1])
            dfk_ref[...] += jnp.where(first, dk[0][:, HEAD_DIM:HEAD_DIM + 1], dk[1][:, 0:1])
            dv_ref[...] += dv[0] + dv[1]

        for nb in range(1, grp + 1):
            exists = (qa + grp <= nq) if nb == grp else (qa + nb == nq)
            for diagonal in (False, True):
                @pl.when(exists & ((qa == ki) == diagonal))
                def _(nb=nb, diagonal=diagonal):
                    sweep(nb, diagonal)

    def qblock(j):
        return lambda s, qt: jnp.minimum(qt[s] + j, nq - 1)

    qbs = [qblock(j) for j in range(grp)]
    qcol = [functools.partial(lambda p, s, qt, kt, qb: (qb(s, qt), p), qb=qb) for qb in qbs]
    krow = lambda p, s, qt, kt: (kt[s], p)
    return pl.pallas_call(
        body, name="attention_bwd",
        out_shape=[jax.ShapeDtypeStruct((t, ATTN_W), F32)] * 5,
        grid_spec=pltpu.PrefetchScalarGridSpec(
            num_scalar_prefetch=2, grid=(4, len(steps)),
            in_specs=[pl.BlockSpec((tq, 128), m) for m in qcol]
            + [pl.BlockSpec((tk, 128), lambda p, s, qt, kt: (kt[s], 4 + p)),
               pl.BlockSpec((tk, 128), lambda p, s, qt, kt: (kt[s], 8 + p))]
            + [pl.BlockSpec((None, tq, 128), functools.partial(lambda p, s, qt, kt, qb: (p, qb(s, qt), 0), qb=qb))
               for qb in qbs]
            + [pl.BlockSpec((None, tk, 128), lambda p, s, qt, kt: (p, kt[s], 0))]
            + [pl.BlockSpec((tq, 128), m) for m in qcol] + [pl.BlockSpec((tq, 128), m) for m in qcol],
            out_specs=[pl.BlockSpec((t, 128), lambda p, s, qt, kt: (0, p)),
                       pl.BlockSpec((tk, 128), krow), pl.BlockSpec((tk, 128), krow), pl.BlockSpec((tk, 128), krow),
                       pl.BlockSpec((t, 128), lambda p, s, qt, kt: (0, p))]),
        compiler_params=_params(("parallel", "arbitrary"), VMEM_BIG),
    )(q_tab, k_tab, *([z] * grp), z, z, *([aug_q] * grp), aug_k, *([o] * grp), *([do] * grp))


def _shifted(prev_rows, x, shift):
    tm = x.shape[0]
    return pltpu.roll(jnp.concatenate([prev_rows, x], axis=0), shift, 0)[8:8 + tm]


def _ahead(x, next_rows, shift):
    tm = x.shape[0]
    return pltpu.roll(jnp.concatenate([x, next_rows], axis=0), tm + 8 - shift, 0)[0:tm]


def _conv_col0(z):
    return (z.shape[1] - F_PAD - 3 * CONV_W) // CONV_W


def _conv_specs(tm, c0):
    cols = (c0, c0 + 1, c0 + 2)
    tiles = [pl.BlockSpec((tm, CONV_W), functools.partial(lambda i, c: (i, c), c=c)) for c in cols]
    halos = [pl.BlockSpec((8, CONV_W), functools.partial(lambda i, c: (jnp.maximum(i * (tm // 8) - 1, 0), c), c=c))
             for c in cols]
    return tiles, halos


def _conv_gate(z, conv_w):
    t = z.shape[0]
    tm = ROW_TILE
    nt = t // tm

    def body(cb_ref, cc_ref, ci_ref, hc_ref, hi_ref, w_ref, g_ref, gt_ref):
        i = pl.program_id(0)
        cc = cc_ref[...] * ci_ref[...]
        prev = jnp.where(i > 0, hc_ref[...] * hi_ref[...], 0.0)
        conv = w_ref[0:1, :] * _shifted(prev, cc, 2) + w_ref[1:2, :] * _shifted(prev, cc, 1) + w_ref[2:3, :] * cc
        g = cb_ref[...] * conv
        g_ref[...] = g.astype(BF16)
        gt_ref[...] = g.T.astype(BF16)

    (cb, cc, ci), (_, hc, hi) = _conv_specs(tm, _conv_col0(z))
    return pl.pallas_call(
        body, name="conv_gate_fwd",
        out_shape=[jax.ShapeDtypeStruct((t, CONV_W), BF16), jax.ShapeDtypeStruct((CONV_W, t), BF16)],
        grid=(nt,),
        in_specs=[cb, cc, ci, hc, hi, pl.BlockSpec((8, CONV_W), lambda i: (0, 0))],
        out_specs=[pl.BlockSpec((tm, CONV_W), lambda i: (i, 0)), pl.BlockSpec((CONV_W, tm), lambda i: (0, i))],
        compiler_params=_params(("parallel",)),
    )(z, z, z, z, z, conv_w)


def _conv_bwd(z, dg, conv_w):
    t = z.shape[0]
    tm = ROW_TILE
    nt = t // tm

    def body(cb_ref, cc_ref, ci_ref, hc_ref, hi_ref, dg_ref, ncb_ref, ndg_ref, w_ref, dz_ref, dw_ref):
        i = pl.program_id(0)

        @pl.when(i == 0)
        def _():
            dw_ref[...] = jnp.zeros_like(dw_ref)

        cb, c_c, c_in = cb_ref[...], cc_ref[...], ci_ref[...]
        cc = c_c * c_in
        prev = jnp.where(i > 0, hc_ref[...] * hi_ref[...], 0.0)
        cc1, cc2 = _shifted(prev, cc, 1), _shifted(prev, cc, 2)
        w0, w1, w2 = w_ref[0:1, :], w_ref[1:2, :], w_ref[2:3, :]
        conv = w0 * cc2 + w1 * cc1 + w2 * cc
        dgv = dg_ref[...]
        dconv = dgv * cb
        nxt = jnp.where(i < nt - 1, ndg_ref[...] * ncb_ref[...], 0.0)
        dcc = w2 * dconv + w1 * _ahead(dconv, nxt, 1) + w0 * _ahead(dconv, nxt, 2)
        dz_ref[:, 0:CONV_W] = (dgv * conv).astype(BF16)
        dz_ref[:, CONV_W:2 * CONV_W] = (dcc * c_in).astype(BF16)
        dz_ref[:, 2 * CONV_W:] = (dcc * c_c).astype(BF16)
        dw_ref[0:1, :] += jnp.sum(dconv * cc2, axis=0, keepdims=True)
        dw_ref[1:2, :] += jnp.sum(dconv * cc1, axis=0, keepdims=True)
        dw_ref[2:3, :] += jnp.sum(dconv * cc, axis=0, keepdims=True)

    c0 = _conv_col0(z)
    (cb, cc, ci), (_, hc, hi) = _conv_specs(tm, c0)
    nxt = lambda i, c: (jnp.minimum((i + 1) * (tm // 8), t // 8 - 1), c)
    return pl.pallas_call(
        body, name="conv_gate_bwd",
        out_shape=[jax.ShapeDtypeStruct((t, 3 * CONV_W), BF16), jax.ShapeDtypeStruct((8, CONV_W), F32)],
        grid=(nt,),
        in_specs=[cb, cc, ci, hc, hi, pl.BlockSpec((tm, CONV_W), lambda i: (i, 0)),
                  pl.BlockSpec((8, CONV_W), lambda i: nxt(i, c0)), pl.BlockSpec((8, CONV_W), lambda i: nxt(i, 0)),
                  pl.BlockSpec((8, CONV_W), lambda i: (0, 0))],
        out_specs=[pl.BlockSpec((tm, 3 * CONV_W), lambda i: (i, 0)), pl.BlockSpec((8, CONV_W), lambda i: (0, 0))],
        compiler_params=_params(("arbitrary",)),
    )(z, z, z, z, z, dg, z, dg, conv_w)


def _branch_mix(z, o, g, w_ab, w_cb, d):
    t = z.shape[0]
    tm = ROW_TILE
    ga_col = 0

    def body(o_ref, g_ref, ga_ref, gc_ref, wa_ref, wc_ref, mp_ref, mpt_ref, ot_ref):
        o_ = o_ref[...]
        ya = _dot(o_.astype(BF16), wa_ref[...])
        yc = _dot(g_ref[...], wc_ref[...])
        mp = _sigmoid(ga_ref[...]) * ya + _sigmoid(gc_ref[...]) * yc
        mp_ref[...] = mp.astype(BF16)
        mpt_ref[...] = mp.T.astype(BF16)
        ot_ref[...] = o_.T.astype(BF16)

    return pl.pallas_call(
        body, name="branch_mix_fwd",
        out_shape=[jax.ShapeDtypeStruct((t, d), BF16), jax.ShapeDtypeStruct((d, t), BF16),
                   jax.ShapeDtypeStruct((ATTN_W, t), BF16)],
        grid=(t // tm,),
        in_specs=[pl.BlockSpec((tm, ATTN_W), lambda i: (i, 0)), pl.BlockSpec((tm, CONV_W), lambda i: (i, 0)),
                  pl.BlockSpec((tm, d), lambda i: (i, ga_col)), pl.BlockSpec((tm, d), lambda i: (i, ga_col + 1)),
                  pl.BlockSpec((ATTN_W, d), lambda i: (0, 0)), pl.BlockSpec((CONV_W, d), lambda i: (0, 0))],
        out_specs=[pl.BlockSpec((tm, d), lambda i: (i, 0)), pl.BlockSpec((d, tm), lambda i: (0, i)),
                   pl.BlockSpec((ATTN_W, tm), lambda i: (0, i))],
        compiler_params=_params(("parallel",), VMEM_BIG),
    )(o, g, z, z, w_ab, w_cb)


def _branch_bwd(z, o, g, dmixed, w_out, w_ab, w_cb, d):
    t = z.shape[0]
    tm = ROW_TILE // 2
    ga_col = 0

    def body(dm_ref, o_ref, g_ref, ga_ref, gc_ref, wo_ref, wa_ref, wc_ref, dya_ref, dyc_ref, dgt_ref, do_ref, dg_ref):
        dmp = _dot_nt(dm_ref[...], wo_ref[...])
        ya = _dot(o_ref[...].astype(BF16), wa_ref[...])
        yc = _dot(g_ref[...], wc_ref[...])
        sa, sc = _sigmoid(ga_ref[...]), _sigmoid(gc_ref[...])
        dya = (dmp * sa).astype(BF16)
        dyc = (dmp * sc).astype(BF16)
        dya_ref[...] = dya
        dyc_ref[...] = dyc
        dgt_ref[:, :d] = (dmp * ya * sa * (1.0 - sa)).astype(BF16)
        dgt_ref[:, d:] = (dmp * yc * sc * (1.0 - sc)).astype(BF16)
        do_ref[...] = _dot_nt(dya, wa_ref[...])
        dg_ref[...] = _dot_nt(dyc, wc_ref[...])

    row = lambda i: (i, 0)
    fixed = lambda i: (0, 0)
    return pl.pallas_call(
        body, name="branch_mix_bwd",
        out_shape=[jax.ShapeDtypeStruct((t, d), BF16), jax.ShapeDtypeStruct((t, d), BF16),
                   jax.ShapeDtypeStruct((t, 2 * d), BF16), jax.ShapeDtypeStruct((t, ATTN_W), F32),
                   jax.ShapeDtypeStruct((t, CONV_W), F32)],
        grid=(t // tm,),
        in_specs=[pl.BlockSpec((tm, d), row), pl.BlockSpec((tm, ATTN_W), row), pl.BlockSpec((tm, CONV_W), row),
                  pl.BlockSpec((tm, d), lambda i: (i, ga_col)), pl.BlockSpec((tm, d), lambda i: (i, ga_col + 1)),
                  pl.BlockSpec((d, d), fixed), pl.BlockSpec((ATTN_W, d), fixed), pl.BlockSpec((CONV_W, d), fixed)],
        out_specs=[pl.BlockSpec((tm, d), row), pl.BlockSpec((tm, d), row), pl.BlockSpec((tm, 2 * d), row),
                   pl.BlockSpec((tm, ATTN_W), row), pl.BlockSpec((tm, CONV_W), row)],
        compiler_params=_params(("parallel",), VMEM_BIG),
    )(dmixed, o, g, z, z, w_out, w_ab, w_cb)


def _loss_norm_bwd(h, target, f, g_post, alpha):
    t, d = h.shape
    tm = N_FRONT

    def body(h_ref, t_ref, f_ref, g_ref, dh_ref, df_ref, dg_ref, loss_ref):
        i = pl.program_id(0)

        @pl.when(i == 0)
        def _():
            loss_ref[...] = jnp.zeros_like(loss_ref)
            dg_ref[...] = jnp.zeros_like(dg_ref)

        err = jnp.where(i > 0, h_ref[...] - t_ref[...], 0.0)
        dy = err * (1.0 / d)
        dh_ref[...] = dy
        per_row = jnp.sum(err * err, axis=1, keepdims=True) * (1.0 / d)
        loss_ref[...] += 0.5 * jnp.sum(per_row, axis=0, keepdims=True)
        dx, dg = _rms_bwd(f_ref[...], g_ref[...], dy)
        df_ref[...] = (alpha * dx).astype(BF16)
        dg_ref[...] += alpha * dg

    row = pl.BlockSpec((tm, d), lambda i: (i, 0))
    vec = pl.BlockSpec((1, d), lambda i: (0, 0))
    return pl.pallas_call(
        body, name="loss_and_post_norm_bwd",
        out_shape=[jax.ShapeDtypeStruct((t, d), F32), jax.ShapeDtypeStruct((t, d), BF16),
                   jax.ShapeDtypeStruct((1, d), F32), jax.ShapeDtypeStruct((1, 128), F32)],
        grid=(t // tm,),
        in_specs=[row, pl.BlockSpec((tm, d), lambda i: (jnp.maximum(i - 1, 0), 0)), row, vec],
        out_specs=[row, row, vec, pl.BlockSpec((1, 128), lambda i: (0, 0))],
        compiler_params=_params(("arbitrary",)),
    )(h, target, f, g_post)


def _norm_bwd(name, x, g, dy, alpha):
    t, d = x.shape
    tm = ROW_TILE

    def body(x_ref, g_ref, dy_ref, dx_ref, dg_ref):
        @pl.when(pl.program_id(0) == 0)
        def _():
            dg_ref[...] = jnp.zeros_like(dg_ref)

        dx, dg = _rms_bwd(x_ref[...], g_ref[...], dy_ref[...])
        dx_ref[...] = (alpha * dx).astype(BF16)
        dg_ref[...] += alpha * dg

    row = pl.BlockSpec((tm, d), lambda i: (i, 0))
    vec = pl.BlockSpec((1, d), lambda i: (0, 0))
    return pl.pallas_call(
        body, name=name,
        out_shape=[jax.ShapeDtypeStruct((t, d), BF16), jax.ShapeDtypeStruct((1, d), F32)],
        grid=(t // tm,), in_specs=[row, vec, row], out_specs=[row, vec],
        compiler_params=_params(("arbitrary",)),
    )(x, g, dy)


def _ffn_bwd_mid(name, df, w_out, ab):
    t, d = df.shape
    cw = ab.shape[1] // 4
    tm = ROW_TILE

    def body(df_ref, w_ref, ab_ref, o_ref):
        ds = _dot_nt(df_ref[...], w_ref[...])
        a = ab_ref[:, :cw].astype(F32)
        b = ab_ref[:, cw:].astype(F32)
        sg = _sigmoid(a)
        o_ref[:, :cw] = (ds * b * (sg * (1.0 + a * (1.0 - sg)))).astype(BF16)
        o_ref[:, cw:] = (ds * (a * sg)).astype(BF16)

    return pl.pallas_call(
        body, name=name, out_shape=jax.ShapeDtypeStruct((t, 4 * cw), BF16),
        grid=(2, t // tm),
        in_specs=[pl.BlockSpec((tm, d), lambda j, i: (i, 0)), pl.BlockSpec((cw, d), lambda j, i: (j, 0)),
                  pl.BlockSpec((tm, 2 * cw), lambda j, i: (i, j))],
        out_specs=pl.BlockSpec((tm, 2 * cw), lambda j, i: (i, j)),
        compiler_params=_params(("parallel", "parallel"), VMEM_BIG),
    )(df, w_out, ab)


def _mm_nt_norm_bwd(name, dy, w, h, g, dh_in):
    t, kdim = dy.shape
    d = h.shape[1]
    tm = ROW_TILE // 2
    slots = w.ndim == 3

    def body(dy_ref, w_ref, h_ref, g_ref, dhi_ref, dh_ref, dg_ref):
        @pl.when(pl.program_id(0) == 0)
        def _():
            dg_ref[...] = jnp.zeros_like(dg_ref)

        if slots:
            cw = w_ref.shape[2]
            dn = _dot_nt(dy_ref[:, 0:cw], w_ref[_slot_of(0)])
            for k in range(1, 4):
                dn += _dot_nt(dy_ref[:, k * cw:(k + 1) * cw], w_ref[_slot_of(k)])
        else:
            dn = _dot_nt(dy_ref[...], w_ref[...])
        dx, dg = _rms_bwd(h_ref[...], g_ref[...], dn)
        dh_ref[...] = dhi_ref[...] + dx
        dg_ref[...] += dg

    row = pl.BlockSpec((tm, d), lambda i: (i, 0))
    vec = pl.BlockSpec((1, d), lambda i: (0, 0))
    return pl.pallas_call(
        body, name=name,
        out_shape=[jax.ShapeDtypeStruct((t, d), F32), jax.ShapeDtypeStruct((1, d), F32)],
        grid=(t // tm,),
        in_specs=[pl.BlockSpec((tm, kdim), lambda i: (i, 0)), pl.BlockSpec(w.shape, lambda i: (0,) * w.ndim),
                  row, vec, row],
        out_specs=[row, vec],
        compiler_params=_params(("arbitrary",), VMEM_BIG),
    )(dy, w, h, g, dh_in)


def _gate_bwd(dfq, dfk, z, b_pad, f_col):
    t = z.shape[0]
    tm = ROW_TILE
    nt = t // tm

    def body(dq_ref, dk_ref, z_ref, b_ref, dz_ref, db_ref, carry_ref):
        i = pl.program_id(0)

        @pl.when(i == 0)
        def _():
            carry_ref[...] = jnp.zeros_like(carry_ref)
            db_ref[...] = jnp.zeros_like(db_ref)

        pick = (lax.broadcasted_iota(jnp.int32, (ATTN_W, 128), 0)
                == HEAD_DIM * lax.broadcasted_iota(jnp.int32, (ATTN_W, 128), 1)).astype(F32)
        d_heads = jnp.dot(dq_ref[...] - dk_ref[...], pick, preferred_element_type=F32,
                          precision=lax.Precision.HIGHEST)
        tri = (lax.broadcasted_iota(jnp.int32, (tm, tm), 0) <= lax.broadcasted_iota(jnp.int32, (tm, tm), 1))
        tail = jnp.dot(tri.astype(F32), d_heads, preferred_element_type=F32, precision=lax.Precision.HIGHEST)
        tail = tail + carry_ref[0:1, :]
        carry_ref[...] = jnp.broadcast_to(tail[0:1, :], carry_ref.shape)
        row = (nt - 1 - i) * tm + lax.broadcasted_iota(jnp.int32, (tm, 1), 0)
        dlogit = jnp.where(row >= ROW_PAD, tail * _sigmoid(-(z_ref[...] + b_ref[...])), 0.0)
        dz_ref[...] = jnp.zeros_like(dz_ref)
        dz_ref[:, 0:128] = dlogit.astype(BF16)
        db_ref[...] += jnp.sum(dlogit, axis=0, keepdims=True)

    rev = lambda i: (nt - 1 - i, 0)
    return pl.pallas_call(
        body, name="forget_gate_bwd",
        out_shape=[jax.ShapeDtypeStruct((t, F_PAD), BF16), jax.ShapeDtypeStruct((1, 128), F32)],
        grid=(nt,),
        in_specs=[pl.BlockSpec((tm, ATTN_W), rev), pl.BlockSpec((tm, ATTN_W), rev),
                  pl.BlockSpec((tm, 128), lambda i: (nt - 1 - i, f_col // 128)),
                  pl.BlockSpec((1, 128), lambda i: (0, 0))],
        out_specs=[pl.BlockSpec((tm, F_PAD), rev), pl.BlockSpec((1, 128), lambda i: (0, 0))],
        scratch_shapes=[pltpu.VMEM((8, 128), F32)],
        compiler_params=_params(("arbitrary",)),
    )(dfq, dfk, z, b_pad)


def _ffn_fwd(tag, n, w_in4, w_out, h, g_post, g_next):
    ab, s, s_t = _ffn_in(f"{tag}_in_fwd", n, w_in4)
    outs = _mm_resid_norm(f"{tag}_out_fwd", s, w_out, h, g_post, 0.5, g_next)
    return ab, s_t, outs


def _ffn_bwd_weights(tag, df, ab, s_t, n_t, w_in4, w_out):
    d, cw = w_in4.shape[1], w_in4.shape[2]
    t = df.shape[0]
    dw_out = _weight_grad(f"{tag}_dw_out", s_t, df, d, out_rows=cw // 2)
    dab = _ffn_bwd_mid(f"{tag}_mid_bwd", df, w_out, ab)
    bk = _k_tile(t)
    dw_in = _matmul(
        f"{tag}_dw_in", n_t, dab, jax.ShapeDtypeStruct((4, d, cw), F32), (1, 4, t // bk),
        pl.BlockSpec((d, bk), lambda a, b, k: (0, k)), pl.BlockSpec((bk, cw), lambda a, b, k: (k, b)),
        pl.BlockSpec((None, d, cw), lambda a, b, k: (_slot_of(b), 0, 0)), vmem=VMEM_BIG)
    return dab, dw_in, dw_out


LOSS_ROW = 12


def _pack_small(meta, conv, gains, b_forget, loss=None):
    d = gains[0].shape[1]
    rows = [meta.reshape(4, d), jnp.pad(conv.reshape(1, 3 * 128), ((0, 0), (0, d - 3 * 128)))]
    rows += list(gains) + [jnp.pad(b_forget, ((0, 0), (0, d - HEADS)))]
    last = jnp.zeros((4, d), F32)
    if loss is not None:
        last = jnp.pad(loss.reshape(1, 1), ((0, 3), (0, d - 1)))
    return jnp.concatenate(rows + [last], axis=0)


def _unpack_small(block):
    d = block.shape[1]
    meta = block[0:4].reshape(N_META, d // 4)
    conv = block[4, :3 * 128].reshape(1, 3, 128)
    gains = [block[5 + i:6 + i] for i in range(6)]
    return meta, conv, gains, block[11:12, :HEADS]


def kernel(x, meta_tokens, w_in, b_forget, conv_w, w_attn_branch, w_conv_branch, w_out, g_ffn1_pre, g_ffn1_post, w_ffn1_in, w_ffn1_out, g_mix_pre, g_mix_post, g_ffn2_pre, g_ffn2_post, w_ffn2_in, w_ffn2_out, loss_target, m_meta_tokens, m_w_in, m_b_forget, m_conv_w, m_w_attn_branch, m_w_conv_branch, m_w_out, m_g_ffn1_pre, m_g_ffn1_post, m_w_ffn1_in, m_w_ffn1_out, m_g_mix_pre, m_g_mix_post, m_g_ffn2_pre, m_g_ffn2_post, m_w_ffn2_in, m_w_ffn2_out, v_meta_tokens, v_w_in, v_b_forget, v_conv_w, v_w_attn_branch, v_w_conv_branch, v_w_out, v_g_ffn1_pre, v_g_ffn1_post, v_w_ffn1_in, v_w_ffn1_out, v_g_mix_pre, v_g_mix_post, v_g_ffn2_pre, v_g_ffn2_post, v_w_ffn2_in, v_w_ffn2_out):
    seq, d = x.shape[1], x.shape[2]
    t = seq + N_FRONT
    n_main = 3 * ATTN_W + 3 * CONV_W + 2 * d
    nz = n_main + F_PAD
    f_lo = 3 * ATTN_W
    c_arr = lax.axis_index("c").astype(jnp.int32).reshape(1)

    cs = w_in.shape[2]
    cs_pad = -(-cs // 64) * 64

    def w_in_rows(a):
        return jnp.pad(jnp.transpose(a[0]), ((0, cs_pad - cs), (0, 0)))

    big = [w_in_rows(w_in), w_attn_branch[0], w_conv_branch[0], w_out[0], w_ffn1_in[0], w_ffn1_out[0], w_ffn2_in[0],
           w_ffn2_out[0]]
    small_gather = jnp.concatenate(
        [meta_tokens.reshape(4, d), jnp.pad(conv_w.reshape(1, 3 * 128), ((0, 0), (0, d - 3 * 128))),
         jnp.zeros((11, d), F32)], axis=0)
    w_f1_in4, small4 = _all_gather([big[4].astype(BF16), small_gather])
    (second, rest), small4 = lax.optimization_barrier(
        (([big[5].astype(BF16)], [big[i].astype(BF16) for i in (0, 1, 2, 3, 6, 7)]), small4))
    second_gathered = _all_gather_async("all_gather_ffn1_out", second, 5)
    rest_gathered = _all_gather_async("all_gather_rest", rest, 1)
    meta_full = jnp.transpose(small4[:, 0:4].reshape(4, N_META, d // 4), (1, 0, 2)).reshape(N_META, d)
    conv_full = jnp.transpose(small4[:, 4, :3 * 128].reshape(4, 3, 128), (1, 0, 2)).reshape(3, CONV_W)
    conv_pad = jnp.pad(conv_full, ((0, 5), (0, 0)))
    b_pad = jnp.pad(b_forget, ((0, 0), (0, 128 - HEADS)))

    h0 = jnp.concatenate([jnp.zeros((ROW_PAD, d), F32), meta_full, x[0]], axis=0)
    n1, n1_t = _norm_fwd("ffn1_pre_norm", h0, g_ffn1_pre)
    ab1, s1, s1_t = _ffn_in("ffn1_in_fwd", n1, w_f1_in4)
    w_f1_out = second_gathered(s1, [0])[0].reshape(-1, d)
    f1, h1, u, u_t = _mm_resid_norm("ffn1_out_fwd", s1, w_f1_out, h0, g_ffn1_post, 0.5, g_mix_pre)

    w_in4, w_ab4, w_cb4, w_out4, w_f2_in4, w_f2_out4 = rest_gathered(u, range(6))
    w_in_t = w_in4[:, :cs].reshape(4 * cs, d)
    g_lo = f_lo + HEADS + 3 * CONV_W
    w_in_pad = jnp.concatenate(
        [w_in_t[:f_lo], w_in_t[g_lo:], w_in_t[f_lo + HEADS:g_lo], w_in_t[f_lo:f_lo + HEADS],
         jnp.zeros((F_PAD - HEADS, d), BF16)], axis=0)
    w_ab = jnp.transpose(w_ab4, (1, 0, 2)).reshape(ATTN_W, d)
    w_cb = jnp.transpose(w_cb4, (1, 0, 2)).reshape(CONV_W, d)
    w_out_full = w_out4.reshape(d, d)
    w_f2_out = w_f2_out4.reshape(-1, d)
    qkv, z = _in_proj(u, w_in_pad)
    f_col = z.shape[1] - F_PAD
    f_cum = _gate_prep(z, b_pad, f_col)
    f_heads = f_cum[:, :HEADS]
    o, lse = _attn_fwd(qkv, *_attn_bias_operands(f_heads))
    g, g_t = _conv_gate(z, conv_pad)
    mp, mp_t, o_t = _branch_mix(z, o, g, w_ab, w_cb, d)
    mixed, h2, n2, n2_t = _mm_resid_norm("mix_out_fwd", mp, w_out_full, h1, g_mix_post, 1.0, g_ffn2_pre)
    ab2, s2_t, (f2, h3) = _ffn_fwd("ffn2", n2, w_f2_in4, w_f2_out, h2, g_ffn2_post, None)
    dh3, df2, dg_f2_post, loss_part = _loss_norm_bwd(h3, loss_target[0], f2, g_ffn2_post, 0.5)

    reduced = {}

    def reduce_scatter(label, tags, slots, sequencer_id, hold=None, got=None, after=None):
        if got is None:
            got = _pair_send_halves(f"grad_pair_exchange_{label}", slots)
        else:
            got, _ = lax.optimization_barrier((got, after))
        sums = [_pair_add(tag, s, a, c_arr, F32 if tag == "small" else BF16) for tag, s, a in zip(tags, slots, got)]
        sums, hold = lax.optimization_barrier((sums, hold))
        if sequencer_id is None:
            arrived = _chip_scatter(f"grad_chip_scatter_{label}", sums)
        else:
            arrived = _chip_scatter_async(f"grad_chip_scatter_{label}", sums, sequencer_id)
        mine = [_chip_add(tag, a) for tag, a in zip(tags, arrived)]
        reduced.update(zip(tags, zip(mine, _pair_swap(f"grad_pair_swap_{label}", mine))))
        return hold

    dab2, dw_f2_in, dw_f2_out = _ffn_bwd_weights("ffn2", df2, ab2, s2_t, n2_t, w_f2_in4, w_f2_out)
    ffn2_slots = [dw_f2_in, dw_f2_out.reshape(4, -1, d)]
    ffn2_got = _pair_send_halves_async("grad_pair_exchange_ffn2", ffn2_slots, 6)
    dh2, dg_f2_pre = _mm_nt_norm_bwd("ffn2_in_bwd", dab2, w_f2_in4, h2, g_ffn2_pre, dh3)
    reduce_scatter("ffn2", ["w_ffn2_in", "w_ffn2_out"], ffn2_slots, 2, got=ffn2_got, after=dh2)
    dmixed, dg_mix_post = _norm_bwd("mix_post_norm_bwd", mixed, g_mix_post, dh2, 1.0)
    dw_out = _weight_grad("mix_dw_out", mp_t, dmixed, d)
    dya, dyc, dgates, do, dgconv = _branch_bwd(z, o, g, dmixed, w_out_full, w_ab, w_cb, d)
    dw_ab = _weight_grad("mix_dw_attn_branch", o_t, dya, d)
    dw_cb = _weight_grad("mix_dw_conv_branch", g_t, dyc, d)
    dz_conv, dconv_w = _conv_bwd(z, dgconv, conv_pad)
    front = lax.broadcasted_iota(jnp.int32, (t, 1), 0) < ROW_PAD
    lse_heads = jnp.where(front, 1e9, lse[:, ::HEAD_DIM])
    dq, dk, dv, dfk, dfq = _attn_bwd(qkv, *_attn_bias_operands(f_heads, lse_heads), o, do)
    dz_f, db_forget = _gate_bwd(dfq, dfk, z, b_pad, f_col)
    dz_pieces = {"q": dq, "k": dk, "v": dv, "gates": dgates, "conv": dz_conv, "f": dz_f}
    dh1, dg_mix_pre = _mix_in_bwd(list(dz_pieces.values()), w_in_pad, h1, g_mix_pre, dh2)
    dw_t = {name: _weight_grad_t(f"mix_dw_in_{name}", u_t, piece) for name, piece in dz_pieces.items()}
    dw_in_t = jnp.concatenate(
        [dw_t["q"], dw_t["k"], dw_t["v"], dw_t["f"][:HEADS], dw_t["conv"], dw_t["gates"]], axis=0)
    mix_slots = [jnp.pad(dw_in_t.reshape(4, cs, d), ((0, 0), (0, cs_pad - cs), (0, 0))),
                 jnp.transpose(dw_ab.reshape(ATTN_W, 4, d // 4), (1, 0, 2)),
                 jnp.transpose(dw_cb.reshape(CONV_W, 4, d // 4), (1, 0, 2)),
                 dw_out.reshape(4, d // 4, d)]
    mix_got = _pair_send_halves_async("grad_pair_exchange_mix", mix_slots, 7)
    df1, dg_f1_post = _norm_bwd("ffn1_post_norm_bwd", f1, g_ffn1_post, dh1, 0.5)
    reduce_scatter("mix", ["w_in", "w_attn_branch", "w_conv_branch", "w_out"], mix_slots, 3, got=mix_got, after=df1)
    dab1, dw_f1_in, dw_f1_out = _ffn_bwd_weights("ffn1", df1, ab1, s1_t, n1_t, w_f1_in4, w_f1_out)
    dab1 = reduce_scatter("ffn1", ["w_ffn1_in", "w_ffn1_out"], [dw_f1_in, dw_f1_out.reshape(4, -1, d)], 4, dab1)
    dh0, dg_f1_pre = _mm_nt_norm_bwd("ffn1_in_bwd", dab1, w_f1_in4, h0, g_ffn1_pre, dh1)
    grad_x = dh0[N_FRONT:][None]
    dmeta = dh0[ROW_PAD:N_FRONT]
    small_grad = jnp.stack([
        _pack_small(dmeta[:, j * (d // 4):(j + 1) * (d // 4)], dconv_w[:3, j * 128:(j + 1) * 128],
                    [dg_f1_pre, dg_f1_post, dg_mix_pre, dg_mix_post, dg_f2_pre, dg_f2_post], db_forget[:, :HEADS],
                    loss_part[0, 0])
        for j in range(4)])
    reduce_scatter("small", ["small"], [small_grad], None)
    tags =["w_in", "w_attn_branch", "w_conv_branch", "w_out", "w_ffn1_in", "w_ffn1_out", "w_ffn2_in", "w_ffn2_out", "small"]
    halves = [reduced[tag][0] for tag in tags]
    others = [reduced[tag][1] for tag in tags]

    small = [g_ffn1_pre, g_ffn1_post, g_mix_pre, g_mix_post, g_ffn2_pre, g_ffn2_post]
    small_m = [m_g_ffn1_pre, m_g_ffn1_post, m_g_mix_pre, m_g_mix_post, m_g_ffn2_pre, m_g_ffn2_post]
    small_v = [v_g_ffn1_pre, v_g_ffn1_post, v_g_mix_pre, v_g_mix_post, v_g_ffn2_pre, v_g_ffn2_post]
    ws = big + [_pack_small(meta_tokens, conv_w[0], small, b_forget)]
    ms = [w_in_rows(m_w_in), m_w_attn_branch[0], m_w_conv_branch[0], m_w_out[0], m_w_ffn1_in[0], m_w_ffn1_out[0],
          m_w_ffn2_in[0], m_w_ffn2_out[0], _pack_small(m_meta_tokens, m_conv_w[0], small_m, m_b_forget)]
    vs = [w_in_rows(v_w_in), v_w_attn_branch[0], v_w_conv_branch[0], v_w_out[0], v_w_ffn1_in[0], v_w_ffn1_out[0],
          v_w_ffn2_in[0], v_w_ffn2_out[0], _pack_small(v_meta_tokens, v_conv_w[0], small_v, v_b_forget)]
    updates = [_adamw(tag, w, a, b, m, v, c_arr) for tag, w, a, b, m, v in zip(tags, ws, halves, others, ms, vs)]

    def leaves(big_vals, small_block):
        meta, conv, gains, bf = _unpack_small(small_block)
        w_in_t_, w_ab_, w_cb_, w_out_, f1_in, f1_out, f2_in, f2_out = [b[None] for b in big_vals]
        w_in_ = jnp.transpose(w_in_t_[:, :cs], (0, 2, 1))
        return [meta, w_in_, bf, conv, w_ab_, w_cb_, w_out_, gains[0], gains[1], f1_in, f1_out,
                gains[2], gains[3], gains[4], gains[5], f2_in, f2_out]

    out_g, out_d, out_m, out_v = [leaves([u_[k] for u_ in updates[:8]], updates[8][k]) for k in range(4)]
    loss = updates[8][0][LOSS_ROW, 0]
    return (loss, grad_x, *out_g, *out_d, *out_m, *out_v)
```

```python
import functools

import jax
import jax.numpy as jnp
from jax import lax
from jax.experimental import pallas as pl
from jax.experimental.pallas import tpu as pltpu
from jax.experimental.pallas import tpu_sc as plsc

N_META = 16
ROW_PAD = 112
N_FRONT = ROW_PAD + N_META
HEADS = 8
HEAD_DIM = 64
ATTN_W = HEADS * HEAD_DIM
CONV_W = 512
NORM_EPS = 1e-6
ROW_TILE = 640
F_PAD = 128
ATTN_Q_GROUP = 2
ATTN_KV_GROUP = 4
NEG = -1e30
ADAM_LR = 0.001
ADAM_B1 = 0.9
ADAM_B2 = 0.999
ADAM_EPS = 1e-08
ADAM_WD = 0.01
ADAM_STEP = 10
VMEM_BIG = 56 * 1024 * 1024
MESH = pl.DeviceIdType.MESH
ANY = pl.BlockSpec(memory_space=pl.ANY)
F32 = jnp.float32
BF16 = jnp.bfloat16


def _params(sem, vmem=None):
    return pltpu.CompilerParams(dimension_semantics=sem, vmem_limit_bytes=vmem)


def _sigmoid(x):
    return 1.0 / (1.0 + jnp.exp(-x))


def _rstd(x):
    return lax.rsqrt(jnp.mean(x * x, axis=-1, keepdims=True) + NORM_EPS)


def _rms_bwd(x, g, dy):
    r = _rstd(x)
    xr = x * r
    gdy = g * dy
    dx = r * (gdy - xr * jnp.mean(xr * gdy, axis=-1, keepdims=True))
    return dx, jnp.sum(dy * xr, axis=0, keepdims=True)


def _dot(a, b):
    return jnp.dot(a, b, preferred_element_type=F32)


def _dot_nt(a, b):
    return lax.dot_general(a, b, (((1,), (1,)), ((), ())), preferred_element_type=F32)


def _k_tile(t):
    return 1664 if t % 1664 == 0 else ROW_TILE


def _place():
    x, y, c = lax.axis_index("x"), lax.axis_index("y"), lax.axis_index("c")
    chips = [(1 - x, y), (x, 1 - y), (1 - x, 1 - y)]
    return x, y, c, chips


def _all_gather(shards):
    n = len(shards)
    split = [s.reshape(2, s.shape[0] // 2, s.shape[1]) for s in shards]

    def body(*refs):
        ins, outs = refs[:n], refs[n:2 * n]
        send_sems, recv_sems = refs[2 * n:]
        x, y, c, chips = _place()
        me = 2 * x + y
        sibling = (x, y, 1 - c)

        def remote(i, k, slot, part, to, src=None):
            dst = outs[i].at[slot, part]
            return pltpu.make_async_remote_copy(
                src_ref=dst if src is None else src, dst_ref=dst,
                send_sem=send_sems.at[i, k], recv_sem=recv_sems.at[i, k],
                device_id=to, device_id_type=MESH)

        started = []
        for i in range(n):
            for k, (cx, cy) in enumerate(chips):
                cp = remote(i, k, me, c, (cx, cy, c), src=ins[i].at[c])
                cp.start()
                started.append(cp)
        for i in range(n):
            for k, (cx, cy) in enumerate(chips):
                remote(i, k, 2 * cx + cy, c, (x, y, c)).wait_recv()
                cp = remote(i, 3 + k, 2 * cx + cy, c, sibling)
                cp.start()
                started.append(cp)
        for i in range(n):
            for k, (cx, cy) in enumerate(chips):
                remote(i, 3 + k, 2 * cx + cy, 1 - c, (x, y, c)).wait_recv()
        for cp in started:
            cp.wait_send()

    outs = pl.pallas_call(
        body, name="all_gather_weights",
        out_shape=[jax.ShapeDtypeStruct((4,) + s.shape, s.dtype) for s in split],
        in_specs=[ANY] * n, out_specs=[ANY] * n,
        scratch_shapes=[pltpu.SemaphoreType.DMA((n, 6)), pltpu.SemaphoreType.DMA((n, 6))],
    )(*split)
    me =2 * lax.axis_index("x") + lax.axis_index("y")
    outs = [lax.dynamic_update_slice(o, s[None], (me, 0, 0, 0)) for o, s in zip(outs, split)]
    return [o.reshape((4,) + s.shape) for o, s in zip(outs, shards)]


def _all_gather_async(name, shards, collective_id):
    n = len(shards)
    split = [s.reshape(2, s.shape[0] // 2, s.shape[1]) for s in shards]
    ins = [jax.new_ref(s, memory_space=pltpu.MemorySpace.HBM) for s in split]
    outs = [jax.empty_ref(jax.ShapeDtypeStruct((4,) + s.shape, s.dtype), memory_space=pltpu.MemorySpace.HBM)
            for s in split]

    @pl.kernel(mesh=plsc.ScalarSubcoreMesh(axis_name="sequencer", num_cores=1), name=name,
               scratch_types=(pltpu.SemaphoreType.DMA((n, 6)), pltpu.SemaphoreType.DMA((n, 6))),
               compiler_params=pltpu.CompilerParams(collective_id=collective_id))
    def launch(send_sems, recv_sems):
        x, y, c, chips = _place()
        me = 2 * x + y
        sibling = (x, y, 1 - c)
        barrier = pltpu.get_barrier_semaphore()
        for peer in [(cx, cy, c) for cx, cy in chips] + [sibling]:
            pl.semaphore_signal(barrier, inc=1, device_id=peer, device_id_type=MESH)
        pl.semaphore_wait(barrier, 4)

        def remote(i, k, slot, part, to, src=None):
            dst = outs[i].at[slot, part]
            return pltpu.make_async_remote_copy(
                src_ref=dst if src is None else src, dst_ref=dst,
                send_sem=send_sems.at[i, k], recv_sem=recv_sems.at[i, k],
                device_id=to, device_id_type=MESH)

        started = []
        for i in range(n):
            for k, (cx, cy) in enumerate(chips):
                cp = remote(i, k, me, c, (cx, cy, c), src=ins[i].at[c])
                cp.start()
                started.append(cp)
        for i in range(n):
            for k, (cx, cy) in enumerate(chips):
                remote(i, k, 2 * cx + cy, c, (x, y, c)).wait_recv()
                cp = remote(i, 3 + k, 2 * cx + cy, c, sibling)
                cp.start()
                started.append(cp)
        for i in range(n):
            for k, (cx, cy) in enumerate(chips):
                remote(i, 3 + k, 2 * cx + cy, 1 - c, (x, y, c)).wait_recv()
        for cp in started:
            cp.wait_send()

    launch()
    raw = [o[...] for o in outs]

    def finish(after, which):
        arrived, _ = lax.optimization_barrier(([raw[i] for i in which], after))
        me = 2 * lax.axis_index("x") + lax.axis_index("y")
        gathered = [lax.dynamic_update_slice(a, split[i][None], (me, 0, 0, 0)) for a, i in zip(arrived, which)]
        return [g.reshape((4,) + shards[i].shape) for g, i in zip(gathered, which)]

    return finish


def _pair_send_halves(name, grads):
    n = len(grads)

    def body(*refs):
        ins, outs = refs[:n], refs[n:2 * n]
        send_sems, recv_sems = refs[2 * n:]
        x, y, c, _ = _place()
        cps = []
        for i in range(n):
            half = ins[i].shape[1] // 2
            cp = pltpu.make_async_remote_copy(
                src_ref=ins[i].at[:, pl.ds((1 - c) * half, half)], dst_ref=outs[i],
                send_sem=send_sems.at[i], recv_sem=recv_sems.at[i],
                device_id=(x, y, 1 - c), device_id_type=MESH)
            cp.start()
            cps.append(cp)
        for cp in cps:
            cp.wait()

    return pl.pallas_call(
        body, name=name,
        out_shape=[jax.ShapeDtypeStruct((4, g.shape[1] // 2, g.shape[2]), g.dtype) for g in grads],
        in_specs=[ANY] * n, out_specs=[ANY] * n,
        scratch_shapes=[pltpu.SemaphoreType.DMA((n,)), pltpu.SemaphoreType.DMA((n,))],
    )(*grads)


def _pair_send_halves_async(name, grads, collective_id):
    n = len(grads)
    ins = [jax.new_ref(g, memory_space=pltpu.MemorySpace.HBM) for g in grads]
    outs = [jax.empty_ref(jax.ShapeDtypeStruct((4, g.shape[1] // 2, g.shape[2]), g.dtype),
                          memory_space=pltpu.MemorySpace.HBM) for g in grads]

    @pl.kernel(mesh=plsc.ScalarSubcoreMesh(axis_name="sequencer", num_cores=1), name=name,
               scratch_types=(pltpu.SemaphoreType.DMA((n,)), pltpu.SemaphoreType.DMA((n,))),
               compiler_params=pltpu.CompilerParams(collective_id=collective_id))
    def launch(send_sems, recv_sems):
        x, y, c, _ = _place()
        barrier = pltpu.get_barrier_semaphore()
        pl.semaphore_signal(barrier, inc=1, device_id=(x, y, 1 - c), device_id_type=MESH)
        pl.semaphore_wait(barrier, 1)
        cps = []
        for i in range(n):
            half = ins[i].shape[1] // 2
            cp = pltpu.make_async_remote_copy(
                src_ref=ins[i].at[:, pl.ds((1 - c) * half, half)], dst_ref=outs[i],
                send_sem=send_sems.at[i], recv_sem=recv_sems.at[i],
                device_id=(x, y, 1 - c), device_id_type=MESH)
            cp.start()
            cps.append(cp)
        for cp in cps:
            cp.wait()

    launch()
    return [o[...] for o in outs]


def _chip_scatter(name, parts):
    n = len(parts)

    def body(*refs):
        _scatter_copies(refs[:n], refs[n:2 * n], *refs[2 * n:])

    arrived = pl.pallas_call(
        body, name=name,
        out_shape=[jax.ShapeDtypeStruct(p.shape, p.dtype) for p in parts],
        in_specs=[ANY] * n, out_specs=[ANY] * n,
        scratch_shapes=[pltpu.SemaphoreType.DMA((n, 3)), pltpu.SemaphoreType.DMA((n, 3))],
    )(*parts)
    return _own_slots(parts, arrived)


def _scatter_copies(ins, outs, send_sems, recv_sems):
    x, y, c, chips = _place()
    me = 2 * x + y
    sends = []
    for i in range(len(ins)):
        for k, (cx, cy) in enumerate(chips):
            cp = pltpu.make_async_remote_copy(
                src_ref=ins[i].at[2 * cx + cy], dst_ref=outs[i].at[me],
                send_sem=send_sems.at[i, k], recv_sem=recv_sems.at[i, k],
                device_id=(cx, cy, c), device_id_type=MESH)
            cp.start()
            sends.append(cp)
    for i in range(len(ins)):
        for k, (cx, cy) in enumerate(chips):
            got = outs[i].at[2 * cx + cy]
            pltpu.make_async_remote_copy(
                src_ref=got, dst_ref=got, send_sem=send_sems.at[i, k], recv_sem=recv_sems.at[i, k],
                device_id=(x, y, c), device_id_type=MESH).wait_recv()
    for cp in sends:
        cp.wait_send()


def _own_slots(parts, arrived):
    me = 2 * lax.axis_index("x") + lax.axis_index("y")
    return [lax.dynamic_update_slice(a, lax.dynamic_slice_in_dim(p, me, 1, axis=0), (me, 0, 0))
            for p, a in zip(parts, arrived)]


def _chip_scatter_async(name, parts, collective_id):
    n = len(parts)
    ins = [jax.new_ref(p, memory_space=pltpu.MemorySpace.HBM) for p in parts]
    outs = [jax.empty_ref(jax.ShapeDtypeStruct(p.shape, p.dtype), memory_space=pltpu.MemorySpace.HBM) for p in parts]

    @pl.kernel(mesh=plsc.ScalarSubcoreMesh(axis_name="sequencer", num_cores=1), name=name,
               scratch_types=(pltpu.SemaphoreType.DMA((n, 3)), pltpu.SemaphoreType.DMA((n, 3))),
               compiler_params=pltpu.CompilerParams(collective_id=collective_id))
    def launch(send_sems, recv_sems):
        x, y, c, chips = _place()
        barrier = pltpu.get_barrier_semaphore()
        for cx, cy in chips:
            pl.semaphore_signal(barrier, inc=1, device_id=(cx, cy, c), device_id_type=MESH)
        pl.semaphore_wait(barrier, 3)
        _scatter_copies(ins, outs, send_sems, recv_sems)

    launch()
    return _own_slots(parts, [o[...] for o in outs])


def _pair_swap(name, halves):
    n = len(halves)

    def body(*refs):
        ins, outs = refs[:n], refs[n:2 * n]
        send_sems, recv_sems = refs[2 * n:]
        x, y, c, _ = _place()
        cps = []
        for i in range(n):
            cp = pltpu.make_async_remote_copy(
                src_ref=ins[i], dst_ref=outs[i], send_sem=send_sems.at[i], recv_sem=recv_sems.at[i],
                device_id=(x, y, 1 - c), device_id_type=MESH)
            cp.start()
            cps.append(cp)
        for cp in cps:
            cp.wait()

    return pl.pallas_call(
        body, name=name,
        out_shape=[jax.ShapeDtypeStruct(h.shape, h.dtype) for h in halves],
        in_specs=[ANY] * n, out_specs=[ANY] * n,
        scratch_shapes=[pltpu.SemaphoreType.DMA((n,)), pltpu.SemaphoreType.DMA((n,))],
    )(*halves)


def _row_block(rows, cols, n_bufs, budget=20 * 1024 * 1024):
    best = min(rows, 16)
    for b in range(16, rows + 1, 16):
        if rows % b == 0 and 2 * n_bufs * b * cols * 4 <= budget:
            best = b
    return best


def _pair_add(tag, grad, got, c_arr, out_dtype):
    _, rows, cols = grad.shape
    half = rows // 2
    bh = _row_block(half, cols, 3)
    nb = half // bh

    def body(c_ref, g_ref, a_ref, o_ref):
        o_ref[...] = (g_ref[...] + a_ref[...]).astype(out_dtype)

    return pl.pallas_call(
        body, name=f"pair_add_{tag}",
        out_shape=jax.ShapeDtypeStruct((4, half, cols), out_dtype),
        grid_spec=pltpu.PrefetchScalarGridSpec(
            num_scalar_prefetch=1, grid=(4, nb),
            in_specs=[pl.BlockSpec((None, bh, cols), lambda j, r, c: (j, c[0] * nb + r, 0)),
                      pl.BlockSpec((None, bh, cols), lambda j, r, c: (j, r, 0))],
            out_specs=pl.BlockSpec((None, bh, cols), lambda j, r, c: (j, r, 0))),
        compiler_params=_params(("parallel", "parallel")),
    )(c_arr, grad, got)


def _chip_add(tag, parts):
    _, half, cols = parts.shape
    bh = _row_block(half, cols, 5)

    def body(p_ref, o_ref):
        a, b, c, d = [p_ref[j].astype(F32) for j in range(4)]
        o_ref[...] = ((a + b) + c) + d

    return pl.pallas_call(
        body, name=f"chip_add_{tag}",
        out_shape=jax.ShapeDtypeStruct((half, cols), F32),
        grid=(half // bh,),
        in_specs=[pl.BlockSpec((4, bh, cols), lambda r: (0, r, 0))],
        out_specs=pl.BlockSpec((bh, cols), lambda r: (r, 0)),
        compiler_params=_params(("parallel",)),
    )(parts)


def _adamw(tag, w, mine, theirs, m, v, c_arr):
    rows, cols = w.shape
    half = rows // 2
    br = _row_block(half, cols, 9)
    nb = half // br

    def body(c_ref, w_ref, a_ref, b_ref, m_ref, v_ref, g_ref, d_ref, mo_ref, vo_ref):
        own = (pl.program_id(0) // nb) == c_ref[0]
        g = jnp.where(own, a_ref[...], b_ref[...])
        g_ref[...] = g
        m_new = ADAM_B1 * m_ref[...] + (1.0 - ADAM_B1) * g
        v_new = ADAM_B2 * v_ref[...] + (1.0 - ADAM_B2) * (g * g)
        m_hat = m_new / (1.0 - ADAM_B1 ** ADAM_STEP)
        v_hat = v_new / (1.0 - ADAM_B2 ** ADAM_STEP)
        d_ref[...] = -ADAM_LR * (m_hat / (jnp.sqrt(v_hat) + ADAM_EPS) + ADAM_WD * w_ref[...])
        mo_ref[...] = m_new
        vo_ref[...] = v_new

    spec = pl.BlockSpec((br, cols), lambda r, c: (r, 0))
    mine_spec = pl.BlockSpec((br, cols), lambda r, c: (jnp.clip(r - c[0] * nb, 0, nb - 1), 0))
    theirs_spec = pl.BlockSpec((br, cols), lambda r, c: (jnp.clip(r - (1 - c[0]) * nb, 0, nb - 1), 0))
    return pl.pallas_call(
        body, name=f"adamw_{tag}",
        out_shape=[jax.ShapeDtypeStruct((rows, cols), F32)] * 4,
        grid_spec=pltpu.PrefetchScalarGridSpec(
            num_scalar_prefetch=1, grid=(rows // br,),
            in_specs=[spec, mine_spec, theirs_spec, spec, spec], out_specs=[spec] * 4),
        compiler_params=_params(("arbitrary",)),
    )(c_arr, w, mine, theirs, m, v)


def _matmul(name, x, w, out_shape, grid, x_spec, w_spec, o_spec, *, nt=False, vmem=None):
    nk = grid[2]
    acc_shape = tuple(d for d in o_spec.block_shape if d is not None)

    def body(x_ref, w_ref, o_ref, acc_ref):
        k = pl.program_id(2)
        part = _dot_nt(x_ref[...], w_ref[...]) if nt else _dot(x_ref[...], w_ref[...])
        if nk == 1:
            o_ref[...] = part.astype(o_ref.dtype)
        else:
            @pl.when(k == 0)
            def _():
                acc_ref[...] = part

            @pl.when(k > 0)
            def _():
                acc_ref[...] += part

            @pl.when(k == nk - 1)
            def _():
                o_ref[...] = acc_ref[...].astype(o_ref.dtype)

    return pl.pallas_call(
        body, name=name, out_shape=out_shape, grid=grid,
        in_specs=[x_spec, w_spec], out_specs=o_spec,
        scratch_shapes=[pltpu.VMEM(acc_shape if nk > 1 else (8, 128), F32)],
        compiler_params=_params(("parallel", "parallel", "arbitrary"), vmem),
    )(x, w)


def _weight_grad(name, xt, dy, bn, out_rows=None):
    m, t = xt.shape
    n = dy.shape[1]
    bm = m if out_rows is None else out_rows
    bk = _k_tile(t)
    return _matmul(
        name, xt, dy, jax.ShapeDtypeStruct((m, n), F32), (m // bm, n // bn, t // bk),
        pl.BlockSpec((bm, bk), lambda a, b, k: (a, k)),
        pl.BlockSpec((bk, bn), lambda a, b, k: (k, b)),
        pl.BlockSpec((bm, bn), lambda a, b, k: (a, b)), vmem=VMEM_BIG)


def _weight_grad_t(name, xt, dy):
    m, t = xt.shape
    n = dy.shape[1]
    bn = min(n, 512)
    bk = _k_tile(t)
    nk = t // bk

    def body(x_ref, dy_ref, o_ref, acc_ref):
        k = pl.program_id(1)
        part = _dot(x_ref[...], dy_ref[...].astype(BF16))

        @pl.when(k == 0)
        def _():
            acc_ref[...] = part

        @pl.when(k > 0)
        def _():
            acc_ref[...] += part

        @pl.when(k == nk - 1)
        def _():
            o_ref[...] = acc_ref[...].T

    return pl.pallas_call(
        body, name=name, out_shape=jax.ShapeDtypeStruct((n, m), F32), grid=(n // bn, nk),
        in_specs=[pl.BlockSpec((m, bk), lambda b, k: (0, k)), pl.BlockSpec((bk, bn), lambda b, k: (k, b))],
        out_specs=pl.BlockSpec((bn, m), lambda b, k: (b, 0)),
        scratch_shapes=[pltpu.VMEM((m, bn), F32)],
        compiler_params=_params(("parallel", "arbitrary"), VMEM_BIG),
    )(xt, dy)


def _mix_in_bwd(pieces, wt, h, g, dh_in):
    t, d = h.shape
    tm = ROW_TILE // 2
    widths = [p.shape[1] for p in pieces]
    n = len(pieces)

    def body(*refs):
        dy_refs, (w_ref, h_ref, g_ref, dhi_ref, dh_ref, dg_ref) = refs[:n], refs[n:]

        @pl.when(pl.program_id(0) == 0)
        def _():
            dg_ref[...] = jnp.zeros_like(dg_ref)

        dn, off = None, 0
        for dy_ref, wd in zip(dy_refs, widths):
            part = _dot(dy_ref[...].astype(BF16), w_ref[off:off + wd, :])
            dn = part if dn is None else dn + part
            off += wd
        dx, dg = _rms_bwd(h_ref[...], g_ref[...], dn)
        dh_ref[...] = dhi_ref[...] + dx
        dg_ref[...] += dg

    row = pl.BlockSpec((tm, d), lambda i: (i, 0))
    vec = pl.BlockSpec((1, d), lambda i: (0, 0))
    return pl.pallas_call(
        body, name="mix_in_bwd",
        out_shape=[jax.ShapeDtypeStruct((t, d), F32), jax.ShapeDtypeStruct((1, d), F32)],
        grid=(t // tm,),
        in_specs=[pl.BlockSpec((tm, wd), lambda i: (i, 0)) for wd in widths]
        + [pl.BlockSpec(wt.shape, lambda i: (0, 0)), row, vec, row],
        out_specs=[row, vec],
        compiler_params=_params(("arbitrary",), VMEM_BIG),
    )(*pieces, wt, h, g, dh_in)


def _read_token_rows(src_hbm, buf, sem, i):
    tm = buf.shape[0]

    @pl.when(i == 0)
    def _():
        buf[0:N_FRONT, :] = jnp.zeros((N_FRONT, buf.shape[1]), buf.dtype)
        cp = pltpu.make_async_copy(src_hbm.at[pl.ds(0, tm - N_FRONT)], buf.at[pl.ds(N_FRONT, tm - N_FRONT)], sem)
        cp.start()
        cp.wait()

    @pl.when(i > 0)
    def _():
        cp = pltpu.make_async_copy(src_hbm.at[pl.ds(pl.multiple_of(i * tm - N_FRONT, N_FRONT), tm)], buf, sem)
        cp.start()
        cp.wait()


def _embed_norm(x, meta, g):
    seq, d = x.shape
    t = seq + N_FRONT
    tm = ROW_TILE

    def body(x_hbm, meta_ref, g_ref, h_ref, n_ref, nt_ref, buf, sem):
        i = pl.program_id(0)
        _read_token_rows(x_hbm, buf, sem, i)

        @pl.when(i == 0)
        def _():
            buf[ROW_PAD:N_FRONT, :] = meta_ref[...]

        h = buf[...]
        h_ref[...] = h
        y = h * _rstd(h) * g_ref[...]
        n_ref[...] = y.astype(BF16)
        nt_ref[...] = y.T.astype(BF16)

    row = pl.BlockSpec((tm, d), lambda i: (i, 0))
    return pl.pallas_call(
        body, name="embed_and_ffn1_pre_norm",
        out_shape=[jax.ShapeDtypeStruct((t, d), F32), jax.ShapeDtypeStruct((t, d), BF16),
                   jax.ShapeDtypeStruct((d, t), BF16)],
        grid=(t // tm,),
        in_specs=[ANY, pl.BlockSpec((N_META, d), lambda i: (0, 0)), pl.BlockSpec((1, d), lambda i: (0, 0))],
        out_specs=[row, row, pl.BlockSpec((d, tm), lambda i: (0, i))],
        scratch_shapes=[pltpu.VMEM((tm, d), F32), pltpu.SemaphoreType.DMA],
        compiler_params=_params(("arbitrary",)),
    )(x, meta, g)


def _slot_of(kk):
    return (kk % 2) * 2 + kk // 2


def _ffn_in(name, n, w4):
    t, d = n.shape
    cw = w4.shape[2]
    tm = ROW_TILE

    def body(x_ref, wg_ref, wu_ref, ab_ref, s_ref, st_ref):
        x = x_ref[...]
        a = _dot(x, wg_ref[...])
        b = _dot(x, wu_ref[...])
        ab_ref[:, :cw] = a.astype(BF16)
        ab_ref[:, cw:] = b.astype(BF16)
        s = a * _sigmoid(a) * b
        s_ref[...] = s.astype(BF16)
        st_ref[...] = s.T.astype(BF16)

    return pl.pallas_call(
        body, name=name,
        out_shape=[jax.ShapeDtypeStruct((t, 4 * cw), BF16), jax.ShapeDtypeStruct((t, 2 * cw), BF16),
                   jax.ShapeDtypeStruct((2 * cw, t), BF16)],
        grid=(2, t // tm),
        in_specs=[pl.BlockSpec((tm, d), lambda j, i: (i, 0)),
                  pl.BlockSpec((None, d, cw), lambda j, i: (j, 0, 0)),
                  pl.BlockSpec((None, d, cw), lambda j, i: (2 + j, 0, 0))],
        out_specs=[pl.BlockSpec((tm, 2 * cw), lambda j, i: (i, j)),
                   pl.BlockSpec((tm, cw), lambda j, i: (i, j)),
                   pl.BlockSpec((cw, tm), lambda j, i: (j, i))],
        compiler_params=_params(("parallel", "parallel"), VMEM_BIG),
    )(n, w4, w4)


def _mm_resid_norm(name, x, w, h, g_post, alpha, g_next):
    t, kdim = x.shape
    d = w.shape[1]
    tm = ROW_TILE
    with_next = g_next is not None

    def body(x_ref, w_ref, h_ref, gp_ref, gn_ref, f_ref, hn_ref, *rest):
        f = _dot(x_ref[...], w_ref[...])
        f_ref[...] = f
        hn = h_ref[...] + alpha * (f * _rstd(f) * gp_ref[...])
        hn_ref[...] = hn
        if with_next:
            y = hn * _rstd(hn) * gn_ref[...]
            rest[0][...] = y.astype(BF16)
            rest[1][...] = y.T.astype(BF16)

    row = lambda i: (i, 0)
    vec = pl.BlockSpec((1, d), lambda i: (0, 0))
    out_shape = [jax.ShapeDtypeStruct((t, d), F32), jax.ShapeDtypeStruct((t, d), F32)]
    out_specs = [pl.BlockSpec((tm, d), row), pl.BlockSpec((tm, d), row)]
    if with_next:
        out_shape += [jax.ShapeDtypeStruct((t, d), BF16), jax.ShapeDtypeStruct((d, t), BF16)]
        out_specs += [pl.BlockSpec((tm, d), row), pl.BlockSpec((d, tm), lambda i: (0, i))]
    return pl.pallas_call(
        body, name=name, out_shape=out_shape, grid=(t // tm,),
        in_specs=[pl.BlockSpec((tm, kdim), row), pl.BlockSpec((kdim, d), lambda i: (0, 0)),
                  pl.BlockSpec((tm, d), row), vec, vec],
        out_specs=out_specs,
        compiler_params=_params(("parallel",), VMEM_BIG),
    )(x, w, h, g_post, g_post if g_next is None else g_next)


def _in_proj(u, w):
    t, d = u.shape
    nz = w.shape[0]
    nq = 3 * ATTN_W
    tm = ROW_TILE // 2

    def body(u_ref, w_ref, qkv_ref, z_ref):
        qkv_ref[...] = _dot_nt(u_ref[...], w_ref[0:nq, :]).astype(BF16)
        z_ref[...] = _dot_nt(u_ref[...], w_ref[nq:, :])

    return pl.pallas_call(
        body, name="mix_in_proj",
        out_shape=[jax.ShapeDtypeStruct((t, nq), BF16), jax.ShapeDtypeStruct((t, nz - nq), F32)],
        grid=(t // tm,),
        in_specs=[pl.BlockSpec((tm, d), lambda i: (i, 0)), pl.BlockSpec((nz, d), lambda i: (0, 0))],
        out_specs=[pl.BlockSpec((tm, nq), lambda i: (i, 0)), pl.BlockSpec((tm, nz - nq), lambda i: (i, 0))],
        compiler_params=_params(("parallel",), VMEM_BIG),
    )(u, w)


def _gate_prep(z, b_pad, f_col):
    t = z.shape[0]
    tm = ROW_TILE

    def body(z_ref, b_ref, f_ref, carry_ref):
        i = pl.program_id(0)

        @pl.when(i == 0)
        def _():
            carry_ref[...] = jnp.zeros_like(carry_ref)

        xs = z_ref[...] + b_ref[...]
        logf = jnp.minimum(xs, 0.0) - jnp.log(1.0 + jnp.exp(-jnp.abs(xs)))
        row = i * tm + lax.broadcasted_iota(jnp.int32, (tm, 1), 0)
        logf = jnp.where(row >= ROW_PAD, logf, 0.0)
        tri = (lax.broadcasted_iota(jnp.int32, (tm, tm), 0) >= lax.broadcasted_iota(jnp.int32, (tm, tm), 1))
        f = jnp.dot(tri.astype(F32), logf, preferred_element_type=F32, precision=lax.Precision.HIGHEST)
        f = f + carry_ref[0:1, :]
        f_ref[...] = f
        carry_ref[...] = jnp.broadcast_to(f[tm - 1:tm, :], carry_ref.shape)

    return pl.pallas_call(
        body, name="forget_gate_cumsum", out_shape=jax.ShapeDtypeStruct((t, 128), F32),
        grid=(t // tm,),
        in_specs=[pl.BlockSpec((tm, 128), lambda i: (i, f_col // 128)), pl.BlockSpec((1, 128), lambda i: (0, 0))],
        out_specs=pl.BlockSpec((tm, 128), lambda i: (i, 0)),
        scratch_shapes=[pltpu.VMEM((8, 128), F32)],
        compiler_params=_params(("arbitrary",)),
    )(z, b_pad)


def _lane_halves():
    lane = lax.broadcasted_iota(jnp.int32, (1, 128), 1)
    return lane < HEAD_DIM


def _causal_mask(tq, tk, row0=0):
    row = row0 + lax.broadcasted_iota(jnp.int32, (tq, 1), 0)
    col = lax.broadcasted_iota(jnp.int32, (1, tk), 1)
    return col <= row


def _lane_one(lane):
    return (lax.broadcasted_iota(jnp.int32, (1, 128), 1) == lane).astype(BF16)


def _split3(x):
    hi = x.astype(BF16)
    rest = x - hi.astype(F32)
    mid = rest.astype(BF16)
    return hi, mid, (rest - mid.astype(F32)).astype(BF16)


def _split3_glue(x):
    hi = lax.reduce_precision(x, 8, 7)
    mid = lax.reduce_precision(x - hi, 8, 7)
    lo = lax.reduce_precision((x - hi) - mid, 8, 7)
    return hi.astype(BF16), mid.astype(BF16), lo.astype(BF16)


def _aug_pairs(cols):
    t = cols[0].shape[0]
    a = jnp.pad(jnp.stack(cols, axis=2), ((0, 0), (0, 0), (0, HEAD_DIM - len(cols))))
    a = a.reshape(t, 4, 2, HEAD_DIM)[:, :, ::-1, :]
    return jnp.transpose(a.reshape(t, 4, 128), (1, 0, 2))


def _attn_bias_operands(f_heads, lse_heads=None):
    t = f_heads.shape[0]
    one = jnp.ones((t, HEADS), BF16)
    row = lax.broadcasted_iota(jnp.int32, (t, 1), 0)
    fq = _split3_glue(f_heads)
    fk = _split3_glue(jnp.where(row < ROW_PAD, 1e9, f_heads))
    q_cols = list(fq) + [one] * 3
    k_cols = [one] * 3 + [-c for c in fk]
    if lse_heads is not None:
        q_cols += [-c for c in _split3_glue(lse_heads)]
        k_cols += [one] * 3
    return _aug_pairs(q_cols), _aug_pairs(k_cols)


def _attn_fwd(z, aug_q, aug_k):
    t = z.shape[0]
    tq = tk = ROW_TILE
    nq = t // tq
    grp = ATTN_KV_GROUP
    steps = [(qi, ka) for qi in range(nq) for ka in range(0, qi + 1, grp)]
    q_tab = jnp.array([qi for qi, _ in steps], jnp.int32)
    k_tab = jnp.array([ka for _, ka in steps], jnp.int32)

    def body(qt_ref, kt_ref, q_ref, *refs):
        k_refs, v_refs, aq_ref, ak_refs = refs[:grp], refs[grp:2 * grp], refs[2 * grp], refs[2 * grp + 1:3 * grp + 1]
        o_ref, lse_ref, m_ref, l_ref, acc_ref = refs[3 * grp + 1:]
        step = pl.program_id(1)
        qi, ka = qt_ref[step], kt_ref[step]

        @pl.when(ka == 0)
        def _():
            m_ref[...] = jnp.full_like(m_ref, NEG)
            l_ref[...] = jnp.zeros_like(l_ref)
            acc_ref[...] = jnp.zeros_like(acc_ref)

        def sweep(diagonal):
            first = _lane_halves()
            halves = (first, jnp.logical_not(first))
            q = (q_ref[...] * (HEAD_DIM ** -0.5)).astype(BF16)
            aq = aq_ref[...]
            qa = [jnp.where(lanes, q, aq) for lanes in halves]
            blocks = list(zip(k_refs, v_refs, ak_refs, diagonal))
            s = []
            for k_ref, _, ak_ref, diag in blocks:
                k, ak = k_ref[...].astype(BF16), ak_ref[...]
                for hh, lanes in enumerate(halves):
                    s_c = _dot_nt(qa[hh], jnp.where(lanes, k, ak))
                    s.append(jnp.where(_causal_mask(tq, tk), s_c, NEG) if diag else s_c)
            nb = len(blocks)
            m_prev = [m_ref[:, c0:c0 + 1] for c0 in (0, HEAD_DIM)]
            m_new = []
            for hh in range(2):
                m_h = m_prev[hh]
                for b in range(nb):
                    m_h = jnp.maximum(m_h, jnp.max(s[2 * b + hh], axis=1, keepdims=True))
                m_new.append(m_h)
            pv = [None, None]
            for b, (_, v_ref, _, _) in enumerate(blocks):
                v = v_ref[...].astype(BF16)
                for hh, (lanes, a0) in enumerate(zip(halves, (HEAD_DIM, 0))):
                    part = _dot(jnp.exp(s[2 * b + hh] - m_new[hh]).astype(BF16), jnp.where(lanes, v, _lane_one(a0)))
                    pv[hh] = part if pv[hh] is None else pv[hh] + part
            al0, al1 = [jnp.exp(mp - m_h) for mp, m_h in zip(m_prev, m_new)]
            l0 = al0 * l_ref[:, 0:1] + pv[0][:, HEAD_DIM:HEAD_DIM + 1]
            l1 = al1 * l_ref[:, HEAD_DIM:HEAD_DIM + 1] + pv[1][:, 0:1]
            acc_ref[...] = acc_ref[...] * jnp.where(first, al0, al1) + jnp.where(first, pv[0], pv[1])
            m_ref[...] = jnp.where(first, m_new[0], m_new[1])
            l_ref[...] = jnp.where(first, l0, l1)

        def finish():
            o_ref[...] = acc_ref[...] / l_ref[...]
            lse_ref[...] = m_ref[...] + jnp.log(l_ref[...])

        @pl.when(ka + grp - 1 < qi)
        def _():
            sweep((False,) * grp)

        for nb in range(1, grp + 1):
            @pl.when(ka + nb - 1 == qi)
            def _(nb=nb):
                sweep((False,) * (nb - 1) + (True,))
                finish()

    def kblock(j):
        return lambda s, qt, kt: jnp.minimum(kt[s] + j, qt[s])

    kbs = [kblock(j) for j in range(grp)]
    return pl.pallas_call(
        body, name="attention_fwd",
        out_shape=[jax.ShapeDtypeStruct((t, ATTN_W), F32), jax.ShapeDtypeStruct((t, ATTN_W), F32)],
        grid_spec=pltpu.PrefetchScalarGridSpec(
            num_scalar_prefetch=2, grid=(4, len(steps)),
            in_specs=[pl.BlockSpec((tq, 128), lambda p, s, qt, kt: (qt[s], p))]
            + [pl.BlockSpec((tk, 128), functools.partial(lambda p, s, qt, kt, kb: (kb(s, qt, kt), 4 + p), kb=kb))
               for kb in kbs]
            + [pl.BlockSpec((tk, 128), functools.partial(lambda p, s, qt, kt, kb: (kb(s, qt, kt), 8 + p), kb=kb))
               for kb in kbs]
            + [pl.BlockSpec((None, tq, 128), lambda p, s, qt, kt: (p, qt[s], 0))]
            + [pl.BlockSpec((None, tk, 128), functools.partial(lambda p, s, qt, kt, kb: (p, kb(s, qt, kt), 0), kb=kb))
               for kb in kbs],
            out_specs=[pl.BlockSpec((tq, 128), lambda p, s, qt, kt: (qt[s], p)),
                       pl.BlockSpec((tq, 128), lambda p, s, qt, kt: (qt[s], p))],
            scratch_shapes=[pltpu.VMEM((tq, 128), F32)] * 3),
        compiler_params=_params(("parallel", "arbitrary"), VMEM_BIG),
    )(q_tab, k_tab, z, *([z] * (2 * grp)), aug_q, *([aug_k] * grp))


def _attn_bwd(z, aug_q, aug_k, o, do):
    t = z.shape[0]
    tq = tk = ROW_TILE
    nq = t // tq
    grp = ATTN_Q_GROUP
    steps = [(qa, ki) for ki in range(nq) for qa in range(ki, nq, grp)]
    q_tab = jnp.array([qa for qa, _ in steps], jnp.int32)
    k_tab = jnp.array([ki for _, ki in steps], jnp.int32)
    tn = (((0,), (0,)), ((), ()))

    def body(qt_ref, kt_ref, *refs):
        q_refs, (k_ref, v_ref) = refs[:grp], refs[grp:grp + 2]
        aq_refs, ak_ref = refs[grp + 2:2 * grp + 2], refs[2 * grp + 2]
        o_refs, do_refs = refs[2 * grp + 3:3 * grp + 3], refs[3 * grp + 3:4 * grp + 3]
        dq_ref, dk_ref, dv_ref, dfk_ref, dfq_ref = refs[4 * grp + 3:]
        step = pl.program_id(1)
        qa, ki = qt_ref[step], kt_ref[step]

        def rows(j):
            return pl.ds(pl.multiple_of((qa + j) * tq, tq), tq)

        for j in range(grp):
            @pl.when((ki == 0) & (qa + j < nq))
            def _(j=j):
                dq_ref[rows(j), :] = jnp.zeros((tq, 128), F32)
                dfq_ref[rows(j), :] = jnp.zeros((tq, 128), F32)

        @pl.when(qa == ki)
        def _():
            dk_ref[...] = jnp.zeros_like(dk_ref)
            dv_ref[...] = jnp.zeros_like(dv_ref)
            dfk_ref[...] = jnp.zeros_like(dfk_ref)

        def sweep(nb, diagonal):
            first = _lane_halves()
            lane = lax.broadcasted_iota(jnp.int32, (1, 128), 1)
            scale = HEAD_DIM ** -0.5
            halves = (first, jnp.logical_not(first))
            spare = (HEAD_DIM, 0)
            k = k_ref[...].astype(BF16)
            v = v_ref[...].astype(BF16)
            ak = ak_ref[...]
            k_bias = [jnp.where(lanes, k, ak) for lanes in halves]
            k_ones = [jnp.where(lanes, k, _lane_one(a)) for lanes, a in zip(halves, spare)]
            v_ones = [jnp.where(lanes, v, ((lane >= a) & (lane < a + 3)).astype(BF16)) for lanes, a in zip(halves, spare)]
            chains = [(j, hh) for j in range(nb) for hh in range(2)]
            q16, do16, dos = [], [], []
            for j in range(nb):
                q16.append((q_refs[j][...] * scale).astype(BF16))
                do_ = do_refs[j][...]
                do16.append(do_.astype(BF16))
                od = o_refs[j][...] * do_
                for lanes, a in zip(halves, spare):
                    d_hi, d_mid, d_lo = _split3(jnp.sum(jnp.where(lanes, od, 0.0), axis=1, keepdims=True))
                    minus_delta = jnp.where(lane == a, -d_hi, jnp.where(lane == a + 1, -d_mid,
                                            jnp.where(lane == a + 2, -d_lo, jnp.zeros((), BF16))))
                    dos.append(jnp.where(lanes, do16[j], minus_delta))
            s = [_dot_nt(jnp.where(halves[hh], q16[j], aq_refs[j][...]), k_bias[hh]) for j, hh in chains]
            dp = [_dot_nt(dos[2 * j + hh], v_ones[hh]) for j, hh in chains]
            p = [jnp.exp(s_c) for s_c in s]
            if diagonal:
                p = [jnp.where(_causal_mask(tq, tk), p_c, 0.0) if j == 0 else p_c for p_c, (j, _) in zip(p, chains)]
            ds16 = [(p_c * dp_c).astype(BF16) for p_c, dp_c in zip(p, dp)]
            dv, dk = [None, None], [None, None]
            for c, (j, hh) in enumerate(chains):
                lanes = halves[hh]
                dv_c = lax.dot_general(p[c].astype(BF16), jnp.where(lanes, do16[j], jnp.zeros((), BF16)), tn,
                                       preferred_element_type=F32)
                dk_c = lax.dot_general(ds16[c], jnp.where(lanes, q16[j], _lane_one(spare[hh])), tn,
                                       preferred_element_type=F32)
                dv[hh] = dv_c if dv[hh] is None else dv[hh] + dv_c
                dk[hh] = dk_c if dk[hh] is None else dk[hh] + dk_c
            for j in range(nb):
                dq0, dq1 = [_dot(ds16[2 * j + hh], k_ones[hh]) for hh in range(2)]
                dq_ref[rows(j), :] += jnp.where(first, dq0, dq1) * scale
                dfq_ref[rows(j), :] += jnp.where(first, dq0[:, HEAD_DIM:HEAD_DIM + 1], dq1[:, 0:1])
            dk_ref[...] += jnp.where(first, dk[0], dk[1])
            dfk_ref[...] += jnp.where(first, dk[0][:, HEAD_DIM:HEAD_DIM + 1], dk[1][:, 0:1])
            dv_ref[...] += dv[0] + dv[1]

        for nb in range(1, grp + 1):
            exists = (qa + grp <= nq) if nb == grp else (qa + nb == nq)
            for diagonal in (False, True):
                @pl.when(exists & ((qa == ki) == diagonal))
                def _(nb=nb, diagonal=diagonal):
                    sweep(nb, diagonal)

    def qblock(j):
        return lambda s, qt: jnp.minimum(qt[s] + j, nq - 1)

    qbs = [qblock(j) for j in range(grp)]
    qcol = [functools.partial(lambda p, s, qt, kt, qb: (qb(s, qt), p), qb=qb) for qb in qbs]
    krow = lambda p, s, qt, kt: (kt[s], p)
    return pl.pallas_call(
        body, name="attention_bwd",
        out_shape=[jax.ShapeDtypeStruct((t, ATTN_W), F32)] * 5,
        grid_spec=pltpu.PrefetchScalarGridSpec(
            num_scalar_prefetch=2, grid=(4, len(steps)),
            in_specs=[pl.BlockSpec((tq, 128), m) for m in qcol]
            + [pl.BlockSpec((tk, 128), lambda p, s, qt, kt: (kt[s], 4 + p)),
               pl.BlockSpec((tk, 128), lambda p, s, qt, kt: (kt[s], 8 + p))]
            + [pl.BlockSpec((None, tq, 128), functools.partial(lambda p, s, qt, kt, qb: (p, qb(s, qt), 0), qb=qb))
               for qb in qbs]
            + [pl.BlockSpec((None, tk, 128), lambda p, s, qt, kt: (p, kt[s], 0))]
            + [pl.BlockSpec((tq, 128), m) for m in qcol] + [pl.BlockSpec((tq, 128), m) for m in qcol],
            out_specs=[pl.BlockSpec((t, 128), lambda p, s, qt, kt: (0, p)),
                       pl.BlockSpec((tk, 128), krow), pl.BlockSpec((tk, 128), krow), pl.BlockSpec((tk, 128), krow),
                       pl.BlockSpec((t, 128), lambda p, s, qt, kt: (0, p))]),
        compiler_params=_params(("parallel", "arbitrary"), VMEM_BIG),
    )(q_tab, k_tab, *([z] * grp), z, z, *([aug_q] * grp), aug_k, *([o] * grp), *([do] * grp))


def _shifted(prev_rows, x, shift):
    tm = x.shape[0]
    return pltpu.roll(jnp.concatenate([prev_rows, x], axis=0), shift, 0)[8:8 + tm]


def _ahead(x, next_rows, shift):
    tm = x.shape[0]
    return pltpu.roll(jnp.concatenate([x, next_rows], axis=0), tm + 8 - shift, 0)[0:tm]


def _conv_col0(z):
    return (z.shape[1] - F_PAD - 3 * CONV_W) // CONV_W


def _conv_specs(tm, c0):
    cols = (c0, c0 + 1, c0 + 2)
    tiles = [pl.BlockSpec((tm, CONV_W), functools.partial(lambda i, c: (i, c), c=c)) for c in cols]
    halos = [pl.BlockSpec((8, CONV_W), functools.partial(lambda i, c: (jnp.maximum(i * (tm // 8) - 1, 0), c), c=c))
             for c in cols]
    return tiles, halos


def _conv_gate(z, conv_w):
    t = z.shape[0]
    tm = ROW_TILE
    nt = t // tm

    def body(cb_ref, cc_ref, ci_ref, hc_ref, hi_ref, w_ref, g_ref, gt_ref):
        i = pl.program_id(0)
        cc = cc_ref[...] * ci_ref[...]
        prev = jnp.where(i > 0, hc_ref[...] * hi_ref[...], 0.0)
        conv = w_ref[0:1, :] * _shifted(prev, cc, 2) + w_ref[1:2, :] * _shifted(prev, cc, 1) + w_ref[2:3, :] * cc
        g = cb_ref[...] * conv
        g_ref[...] = g.astype(BF16)
        gt_ref[...] = g.T.astype(BF16)

    (cb, cc, ci), (_, hc, hi) = _conv_specs(tm, _conv_col0(z))
    return pl.pallas_call(
        body, name="conv_gate_fwd",
        out_shape=[jax.ShapeDtypeStruct((t, CONV_W), BF16), jax.ShapeDtypeStruct((CONV_W, t), BF16)],
        grid=(nt,),
        in_specs=[cb, cc, ci, hc, hi, pl.BlockSpec((8, CONV_W), lambda i: (0, 0))],
        out_specs=[pl.BlockSpec((tm, CONV_W), lambda i: (i, 0)), pl.BlockSpec((CONV_W, tm), lambda i: (0, i))],
        compiler_params=_params(("parallel",)),
    )(z, z, z, z, z, conv_w)


def _conv_bwd(z, dg, conv_w):
    t = z.shape[0]
    tm = ROW_TILE
    nt = t // tm

    def body(cb_ref, cc_ref, ci_ref, hc_ref, hi_ref, dg_ref, ncb_ref, ndg_ref, w_ref, dz_ref, dw_ref):
        i = pl.program_id(0)

        @pl.when(i == 0)
        def _():
            dw_ref[...] = jnp.zeros_like(dw_ref)

        cb, c_c, c_in = cb_ref[...], cc_ref[...], ci_ref[...]
        cc = c_c * c_in
        prev = jnp.where(i > 0, hc_ref[...] * hi_ref[...], 0.0)
        cc1, cc2 = _shifted(prev, cc, 1), _shifted(prev, cc, 2)
        w0, w1, w2 = w_ref[0:1, :], w_ref[1:2, :], w_ref[2:3, :]
        conv = w0 * cc2 + w1 * cc1 + w2 * cc
        dgv = dg_ref[...]
        dconv = dgv * cb
        nxt = jnp.where(i < nt - 1, ndg_ref[...] * ncb_ref[...], 0.0)
        dcc = w2 * dconv + w1 * _ahead(dconv, nxt, 1) + w0 * _ahead(dconv, nxt, 2)
        dz_ref[:, 0:CONV_W] = (dgv * conv).astype(BF16)
        dz_ref[:, CONV_W:2 * CONV_W] = (dcc * c_in).astype(BF16)
        dz_ref[:, 2 * CONV_W:] = (dcc * c_c).astype(BF16)
        dw_ref[0:1, :] += jnp.sum(dconv * cc2, axis=0, keepdims=True)
        dw_ref[1:2, :] += jnp.sum(dconv * cc1, axis=0, keepdims=True)
        dw_ref[2:3, :] += jnp.sum(dconv * cc, axis=0, keepdims=True)

    c0 = _conv_col0(z)
    (cb, cc, ci), (_, hc, hi) = _conv_specs(tm, c0)
    nxt = lambda i, c: (jnp.minimum((i + 1) * (tm // 8), t // 8 - 1), c)
    return pl.pallas_call(
        body, name="conv_gate_bwd",
        out_shape=[jax.ShapeDtypeStruct((t, 3 * CONV_W), BF16), jax.ShapeDtypeStruct((8, CONV_W), F32)],
        grid=(nt,),
        in_specs=[cb, cc, ci, hc, hi, pl.BlockSpec((tm, CONV_W), lambda i: (i, 0)),
                  pl.BlockSpec((8, CONV_W), lambda i: nxt(i, c0)), pl.BlockSpec((8, CONV_W), lambda i: nxt(i, 0)),
                  pl.BlockSpec((8, CONV_W), lambda i: (0, 0))],
        out_specs=[pl.BlockSpec((tm, 3 * CONV_W), lambda i: (i, 0)), pl.BlockSpec((8, CONV_W), lambda i: (0, 0))],
        compiler_params=_params(("arbitrary",)),
    )(z, z, z, z, z, dg, z, dg, conv_w)


def _branch_mix(z, o, g, w_ab, w_cb, d):
    t = z.shape[0]
    tm = ROW_TILE
    ga_col = 0

    def body(o_ref, g_ref, ga_ref, gc_ref, wa_ref, wc_ref, mp_ref, mpt_ref, ot_ref):
        o_ = o_ref[...]
        ya = _dot(o_.astype(BF16), wa_ref[...])
        yc = _dot(g_ref[...], wc_ref[...])
        mp = _sigmoid(ga_ref[...]) * ya + _sigmoid(gc_ref[...]) * yc
        mp_ref[...] = mp.astype(BF16)
        mpt_ref[...] = mp.T.astype(BF16)
        ot_ref[...] = o_.T.astype(BF16)

    return pl.pallas_call(
        body, name="branch_mix_fwd",
        out_shape=[jax.ShapeDtypeStruct((t, d), BF16), jax.ShapeDtypeStruct((d, t), BF16),
                   jax.ShapeDtypeStruct((ATTN_W, t), BF16)],
        grid=(t // tm,),
        in_specs=[pl.BlockSpec((tm, ATTN_W), lambda i: (i, 0)), pl.BlockSpec((tm, CONV_W), lambda i: (i, 0)),
                  pl.BlockSpec((tm, d), lambda i: (i, ga_col)), pl.BlockSpec((tm, d), lambda i: (i, ga_col + 1)),
                  pl.BlockSpec((ATTN_W, d), lambda i: (0, 0)), pl.BlockSpec((CONV_W, d), lambda i: (0, 0))],
        out_specs=[pl.BlockSpec((tm, d), lambda i: (i, 0)), pl.BlockSpec((d, tm), lambda i: (0, i)),
                   pl.BlockSpec((ATTN_W, tm), lambda i: (0, i))],
        compiler_params=_params(("parallel",), VMEM_BIG),
    )(o, g, z, z, w_ab, w_cb)


def _branch_bwd(z, o, g, dmixed, w_out, w_ab, w_cb, d):
    t = z.shape[0]
    tm = ROW_TILE // 2
    ga_col = 0

    def body(dm_ref, o_ref, g_ref, ga_ref, gc_ref, wo_ref, wa_ref, wc_ref, dya_ref, dyc_ref, dgt_ref, do_ref, dg_ref):
        dmp = _dot_nt(dm_ref[...], wo_ref[...])
        ya = _dot(o_ref[...].astype(BF16), wa_ref[...])
        yc = _dot(g_ref[...], wc_ref[...])
        sa, sc = _sigmoid(ga_ref[...]), _sigmoid(gc_ref[...])
        dya = (dmp * sa).astype(BF16)
        dyc = (dmp * sc).astype(BF16)
        dya_ref[...] = dya
        dyc_ref[...] = dyc
        dgt_ref[:, :d] = (dmp * ya * sa * (1.0 - sa)).astype(BF16)
        dgt_ref[:, d:] = (dmp * yc * sc * (1.0 - sc)).astype(BF16)
        do_ref[...] = _dot_nt(dya, wa_ref[...])
        dg_ref[...] = _dot_nt(dyc, wc_ref[...])

    row = lambda i: (i, 0)
    fixed = lambda i: (0, 0)
    return pl.pallas_call(
        body, name="branch_mix_bwd",
        out_shape=[jax.ShapeDtypeStruct((t, d), BF16), jax.ShapeDtypeStruct((t, d), BF16),
                   jax.ShapeDtypeStruct((t, 2 * d), BF16), jax.ShapeDtypeStruct((t, ATTN_W), F32),
                   jax.ShapeDtypeStruct((t, CONV_W), F32)],
        grid=(t // tm,),
        in_specs=[pl.BlockSpec((tm, d), row), pl.BlockSpec((tm, ATTN_W), row), pl.BlockSpec((tm, CONV_W), row),
                  pl.BlockSpec((tm, d), lambda i: (i, ga_col)), pl.BlockSpec((tm, d), lambda i: (i, ga_col + 1)),
                  pl.BlockSpec((d, d), fixed), pl.BlockSpec((ATTN_W, d), fixed), pl.BlockSpec((CONV_W, d), fixed)],
        out_specs=[pl.BlockSpec((tm, d), row), pl.BlockSpec((tm, d), row), pl.BlockSpec((tm, 2 * d), row),
                   pl.BlockSpec((tm, ATTN_W), row), pl.BlockSpec((tm, CONV_W), row)],
        compiler_params=_params(("parallel",), VMEM_BIG),
    )(dmixed, o, g, z, z, w_out, w_ab, w_cb)


def _loss_norm_bwd(h, target, f, g_post, alpha):
    t, d = h.shape
    tm = ROW_TILE

    def body(h_ref, t_hbm, f_ref, g_ref, dh_ref, df_ref, dg_ref, loss_ref, t_buf, sem):
        i = pl.program_id(0)

        @pl.when(i == 0)
        def _():
            loss_ref[...] = jnp.zeros_like(loss_ref)
            dg_ref[...] = jnp.zeros_like(dg_ref)

        _read_token_rows(t_hbm, t_buf, sem, i)
        row = i * tm + lax.broadcasted_iota(jnp.int32, (tm, 1), 0)
        err = jnp.where(row >= N_FRONT, h_ref[...] - t_buf[...], 0.0)
        dy = err * (1.0 / d)
        dh_ref[...] = dy
        per_row = jnp.sum(err * err, axis=1, keepdims=True) * (1.0 / d)
        loss_ref[...] += 0.5 * jnp.sum(per_row, axis=0, keepdims=True)
        dx, dg = _rms_bwd(f_ref[...], g_ref[...], dy)
        df_ref[...] = (alpha * dx).astype(BF16)
        dg_ref[...] += alpha * dg

    row = pl.BlockSpec((tm, d), lambda i: (i, 0))
    vec = pl.BlockSpec((1, d), lambda i: (0, 0))
    return pl.pallas_call(
        body, name="loss_and_post_norm_bwd",
        out_shape=[jax.ShapeDtypeStruct((t, d), F32), jax.ShapeDtypeStruct((t, d), BF16),
                   jax.ShapeDtypeStruct((1, d), F32), jax.ShapeDtypeStruct((1, 128), F32)],
        grid=(t // tm,),
        in_specs=[row, ANY, row, vec],
        out_specs=[row, row, vec, pl.BlockSpec((1, 128), lambda i: (0, 0))],
        scratch_shapes=[pltpu.VMEM((tm, d), F32), pltpu.SemaphoreType.DMA],
        compiler_params=_params(("arbitrary",)),
    )(h, target, f, g_post)


def _norm_bwd(name, x, g, dy, alpha):
    t, d = x.shape
    tm = ROW_TILE

    def body(x_ref, g_ref, dy_ref, dx_ref, dg_ref):
        @pl.when(pl.program_id(0) == 0)
        def _():
            dg_ref[...] = jnp.zeros_like(dg_ref)

        dx, dg = _rms_bwd(x_ref[...], g_ref[...], dy_ref[...])
        dx_ref[...] = (alpha * dx).astype(BF16)
        dg_ref[...] += alpha * dg

    row = pl.BlockSpec((tm, d), lambda i: (i, 0))
    vec = pl.BlockSpec((1, d), lambda i: (0, 0))
    return pl.pallas_call(
        body, name=name,
        out_shape=[jax.ShapeDtypeStruct((t, d), BF16), jax.ShapeDtypeStruct((1, d), F32)],
        grid=(t // tm,), in_specs=[row, vec, row], out_specs=[row, vec],
        compiler_params=_params(("arbitrary",)),
    )(x, g, dy)


def _ffn_bwd_mid(name, df, w_out, ab):
    t, d = df.shape
    cw = ab.shape[1] // 4
    tm = ROW_TILE

    def body(df_ref, w_ref, ab_ref, o_ref):
        ds = _dot_nt(df_ref[...], w_ref[...])
        a = ab_ref[:, :cw].astype(F32)
        b = ab_ref[:, cw:].astype(F32)
        sg = _sigmoid(a)
        o_ref[:, :cw] = (ds * b * (sg * (1.0 + a * (1.0 - sg)))).astype(BF16)
        o_ref[:, cw:] = (ds * (a * sg)).astype(BF16)

    return pl.pallas_call(
        body, name=name, out_shape=jax.ShapeDtypeStruct((t, 4 * cw), BF16),
        grid=(2, t // tm),
        in_specs=[pl.BlockSpec((tm, d), lambda j, i: (i, 0)), pl.BlockSpec((cw, d), lambda j, i: (j, 0)),
                  pl.BlockSpec((tm, 2 * cw), lambda j, i: (i, j))],
        out_specs=pl.BlockSpec((tm, 2 * cw), lambda j, i: (i, j)),
        compiler_params=_params(("parallel", "parallel"), VMEM_BIG),
    )(df, w_out, ab)


def _mm_nt_norm_bwd(name, dy, w, h, g, dh_in):
    t, kdim = dy.shape
    d = h.shape[1]
    tm = ROW_TILE // 2
    slots = w.ndim == 3

    def body(dy_ref, w_ref, h_ref, g_ref, dhi_ref, dh_ref, dg_ref):
        @pl.when(pl.program_id(0) == 0)
        def _():
            dg_ref[...] = jnp.zeros_like(dg_ref)

        if slots:
            cw = w_ref.shape[2]
            dn = _dot_nt(dy_ref[:, 0:cw], w_ref[_slot_of(0)])
            for k in range(1, 4):
                dn += _dot_nt(dy_ref[:, k * cw:(k + 1) * cw], w_ref[_slot_of(k)])
        else:
            dn = _dot_nt(dy_ref[...], w_ref[...])
        dx, dg = _rms_bwd(h_ref[...], g_ref[...], dn)
        dh_ref[...] = dhi_ref[...] + dx
        dg_ref[...] += dg

    row = pl.BlockSpec((tm, d), lambda i: (i, 0))
    vec = pl.BlockSpec((1, d), lambda i: (0, 0))
    return pl.pallas_call(
        body, name=name,
        out_shape=[jax.ShapeDtypeStruct((t, d), F32), jax.ShapeDtypeStruct((1, d), F32)],
        grid=(t // tm,),
        in_specs=[pl.BlockSpec((tm, kdim), lambda i: (i, 0)), pl.BlockSpec(w.shape, lambda i: (0,) * w.ndim),
                  row, vec, row],
        out_specs=[row, vec],
        compiler_params=_params(("arbitrary",), VMEM_BIG),
    )(dy, w, h, g, dh_in)


def _gate_bwd(dfq, dfk, z, b_pad, f_col):
    t = z.shape[0]
    tm = ROW_TILE
    nt = t // tm

    def body(dq_ref, dk_ref, z_ref, b_ref, dz_ref, db_ref, carry_ref):
        i = pl.program_id(0)

        @pl.when(i == 0)
        def _():
            carry_ref[...] = jnp.zeros_like(carry_ref)
            db_ref[...] = jnp.zeros_like(db_ref)

        pick = (lax.broadcasted_iota(jnp.int32, (ATTN_W, 128), 0)
                == HEAD_DIM * lax.broadcasted_iota(jnp.int32, (ATTN_W, 128), 1)).astype(F32)
        d_heads = jnp.dot(dq_ref[...] - dk_ref[...], pick, preferred_element_type=F32,
                          precision=lax.Precision.HIGHEST)
        tri = (lax.broadcasted_iota(jnp.int32, (tm, tm), 0) <= lax.broadcasted_iota(jnp.int32, (tm, tm), 1))
        tail = jnp.dot(tri.astype(F32), d_heads, preferred_element_type=F32, precision=lax.Precision.HIGHEST)
        tail = tail + carry_ref[0:1, :]
        carry_ref[...] = jnp.broadcast_to(tail[0:1, :], carry_ref.shape)
        row = (nt - 1 - i) * tm + lax.broadcasted_iota(jnp.int32, (tm, 1), 0)
        dlogit = jnp.where(row >= ROW_PAD, tail * _sigmoid(-(z_ref[...] + b_ref[...])), 0.0)
        dz_ref[...] = jnp.zeros_like(dz_ref)
        dz_ref[:, 0:128] = dlogit.astype(BF16)
        db_ref[...] += jnp.sum(dlogit, axis=0, keepdims=True)

    rev = lambda i: (nt - 1 - i, 0)
    return pl.pallas_call(
        body, name="forget_gate_bwd",
        out_shape=[jax.ShapeDtypeStruct((t, F_PAD), BF16), jax.ShapeDtypeStruct((1, 128), F32)],
        grid=(nt,),
        in_specs=[pl.BlockSpec((tm, ATTN_W), rev), pl.BlockSpec((tm, ATTN_W), rev),
                  pl.BlockSpec((tm, 128), lambda i: (nt - 1 - i, f_col // 128)),
                  pl.BlockSpec((1, 128), lambda i: (0, 0))],
        out_specs=[pl.BlockSpec((tm, F_PAD), rev), pl.BlockSpec((1, 128), lambda i: (0, 0))],
        scratch_shapes=[pltpu.VMEM((8, 128), F32)],
        compiler_params=_params(("arbitrary",)),
    )(dfq, dfk, z, b_pad)


def _ffn_fwd(tag, n, w_in4, w_out, h, g_post, g_next):
    ab, s, s_t = _ffn_in(f"{tag}_in_fwd", n, w_in4)
    outs = _mm_resid_norm(f"{tag}_out_fwd", s, w_out, h, g_post, 0.5, g_next)
    return ab, s_t, outs


def _ffn_bwd_weights(tag, df, ab, s_t, n_t, w_in4, w_out):
    d, cw = w_in4.shape[1], w_in4.shape[2]
    t = df.shape[0]
    dw_out = _weight_grad(f"{tag}_dw_out", s_t, df, d, out_rows=cw // 2)
    dab = _ffn_bwd_mid(f"{tag}_mid_bwd", df, w_out, ab)
    bk = _k_tile(t)
    dw_in = _matmul(
        f"{tag}_dw_in", n_t, dab, jax.ShapeDtypeStruct((4, d, cw), F32), (1, 4, t // bk),
        pl.BlockSpec((d, bk), lambda a, b, k: (0, k)), pl.BlockSpec((bk, cw), lambda a, b, k: (k, b)),
        pl.BlockSpec((None, d, cw), lambda a, b, k: (_slot_of(b), 0, 0)), vmem=VMEM_BIG)
    return dab, dw_in, dw_out


LOSS_ROW = 12


def _pack_small(meta, conv, gains, b_forget, loss=None):
    d = gains[0].shape[1]
    rows = [meta.reshape(4, d), jnp.pad(conv.reshape(1, 3 * 128), ((0, 0), (0, d - 3 * 128)))]
    rows += list(gains) + [jnp.pad(b_forget, ((0, 0), (0, d - HEADS)))]
    last = jnp.zeros((4, d), F32)
    if loss is not None:
        last = jnp.pad(loss.reshape(1, 1), ((0, 3), (0, d - 1)))
    return jnp.concatenate(rows + [last], axis=0)


def _unpack_small(block):
    d = block.shape[1]
    meta = block[0:4].reshape(N_META, d // 4)
    conv = block[4, :3 * 128].reshape(1, 3, 128)
    gains = [block[5 + i:6 + i] for i in range(6)]
    return meta, conv, gains, block[11:12, :HEADS]


def kernel(x, meta_tokens, w_in, b_forget, conv_w, w_attn_branch, w_conv_branch, w_out, g_ffn1_pre, g_ffn1_post, w_ffn1_in, w_ffn1_out, g_mix_pre, g_mix_post, g_ffn2_pre, g_ffn2_post, w_ffn2_in, w_ffn2_out, loss_target, m_meta_tokens, m_w_in, m_b_forget, m_conv_w, m_w_attn_branch, m_w_conv_branch, m_w_out, m_g_ffn1_pre, m_g_ffn1_post, m_w_ffn1_in, m_w_ffn1_out, m_g_mix_pre, m_g_mix_post, m_g_ffn2_pre, m_g_ffn2_post, m_w_ffn2_in, m_w_ffn2_out, v_meta_tokens, v_w_in, v_b_forget, v_conv_w, v_w_attn_branch, v_w_conv_branch, v_w_out, v_g_ffn1_pre, v_g_ffn1_post, v_w_ffn1_in, v_w_ffn1_out, v_g_mix_pre, v_g_mix_post, v_g_ffn2_pre, v_g_ffn2_post, v_w_ffn2_in, v_w_ffn2_out):
    seq, d = x.shape[1], x.shape[2]
    t = seq + N_FRONT
    f_lo = 3 * ATTN_W
    c_arr = lax.axis_index("c").astype(jnp.int32).reshape(1)

    cs = w_in.shape[2]
    cs_pad = -(-cs // 64) * 64

    def w_in_rows(a):
        return jnp.pad(jnp.transpose(a[0]), ((0, cs_pad - cs), (0, 0)))

    big = [w_in_rows(w_in), w_attn_branch[0], w_conv_branch[0], w_out[0], w_ffn1_in[0], w_ffn1_out[0], w_ffn2_in[0],
           w_ffn2_out[0]]
    small_gather = jnp.concatenate(
        [meta_tokens.reshape(4, d), jnp.pad(conv_w.reshape(1, 3 * 128), ((0, 0), (0, d - 3 * 128))),
         jnp.zeros((11, d), F32)], axis=0)
    w_f1_in4, small4 = _all_gather([big[4].astype(BF16), small_gather])
    (second, rest), small4 = lax.optimization_barrier(
        (([big[5].astype(BF16)], [big[i].astype(BF16) for i in (0, 1, 2, 3, 6, 7)]), small4))
    second_gathered = _all_gather_async("all_gather_ffn1_out", second, 5)
    rest_gathered = _all_gather_async("all_gather_rest", rest, 1)
    meta_full = jnp.transpose(small4[:, 0:4].reshape(4, N_META, d // 4), (1, 0, 2)).reshape(N_META, d)
    conv_full = jnp.transpose(small4[:, 4, :3 * 128].reshape(4, 3, 128), (1, 0, 2)).reshape(3, CONV_W)
    conv_pad = jnp.pad(conv_full, ((0, 5), (0, 0)))
    b_pad = jnp.pad(b_forget, ((0, 0), (0, 128 - HEADS)))

    h0, n1, n1_t = _embed_norm(x[0], meta_full, g_ffn1_pre)
    ab1, s1, s1_t = _ffn_in("ffn1_in_fwd", n1, w_f1_in4)
    w_f1_out = second_gathered(s1, [0])[0].reshape(-1, d)
    f1, h1, u, u_t = _mm_resid_norm("ffn1_out_fwd", s1, w_f1_out, h0, g_ffn1_post, 0.5, g_mix_pre)

    w_in4, w_ab4, w_cb4, w_out4, w_f2_in4, w_f2_out4 = rest_gathered(u, range(6))
    w_in_t = w_in4[:, :cs].reshape(4 * cs, d)
    g_lo = f_lo + HEADS + 3 * CONV_W
    w_in_pad = jnp.concatenate(
        [w_in_t[:f_lo], w_in_t[g_lo:], w_in_t[f_lo + HEADS:g_lo], w_in_t[f_lo:f_lo + HEADS],
         jnp.zeros((F_PAD - HEADS, d), BF16)], axis=0)
    w_ab = jnp.transpose(w_ab4, (1, 0, 2)).reshape(ATTN_W, d)
    w_cb = jnp.transpose(w_cb4, (1, 0, 2)).reshape(CONV_W, d)
    w_out_full = w_out4.reshape(d, d)
    w_f2_out = w_f2_out4.reshape(-1, d)
    qkv, z = _in_proj(u, w_in_pad)
    f_col = z.shape[1] - F_PAD
    f_cum = _gate_prep(z, b_pad, f_col)
    f_heads = f_cum[:, :HEADS]
    o, lse = _attn_fwd(qkv, *_attn_bias_operands(f_heads))
    g, g_t = _conv_gate(z, conv_pad)
    mp, mp_t, o_t = _branch_mix(z, o, g, w_ab, w_cb, d)
    mixed, h2, n2, n2_t = _mm_resid_norm("mix_out_fwd", mp, w_out_full, h1, g_mix_post, 1.0, g_ffn2_pre)
    ab2, s2_t, (f2, h3) = _ffn_fwd("ffn2", n2, w_f2_in4, w_f2_out, h2, g_ffn2_post, None)
    dh3, df2, dg_f2_post, loss_part = _loss_norm_bwd(h3, loss_target[0], f2, g_ffn2_post, 0.5)

    reduced = {}

    def reduce_scatter(label, tags, slots, sequencer_id, hold=None, got=None, after=None):
        if got is None:
            got = _pair_send_halves(f"grad_pair_exchange_{label}", slots)
        else:
            got, _ = lax.optimization_barrier((got, after))
        sums = [_pair_add(tag, s, a, c_arr, F32 if tag == "small" else BF16) for tag, s, a in zip(tags, slots, got)]
        sums, hold = lax.optimization_barrier((sums, hold))
        if sequencer_id is None:
            arrived = _chip_scatter(f"grad_chip_scatter_{label}", sums)
        else:
            arrived = _chip_scatter_async(f"grad_chip_scatter_{label}", sums, sequencer_id)
        mine = [_chip_add(tag, a) for tag, a in zip(tags, arrived)]
        reduced.update(zip(tags, zip(mine, _pair_swap(f"grad_pair_swap_{label}", mine))))
        return hold

    dab2, dw_f2_in, dw_f2_out = _ffn_bwd_weights("ffn2", df2, ab2, s2_t, n2_t, w_f2_in4, w_f2_out)
    ffn2_slots = [dw_f2_in, dw_f2_out.reshape(4, -1, d)]
    ffn2_got = _pair_send_halves_async("grad_pair_exchange_ffn2", ffn2_slots, 6)
    dh2, dg_f2_pre = _mm_nt_norm_bwd("ffn2_in_bwd", dab2, w_f2_in4, h2, g_ffn2_pre, dh3)
    reduce_scatter("ffn2", ["w_ffn2_in", "w_ffn2_out"], ffn2_slots, 2, got=ffn2_got, after=dh2)
    dmixed, dg_mix_post = _norm_bwd("mix_post_norm_bwd", mixed, g_mix_post, dh2, 1.0)
    dw_out = _weight_grad("mix_dw_out", mp_t, dmixed, d)
    dya, dyc, dgates, do, dgconv = _branch_bwd(z, o, g, dmixed, w_out_full, w_ab, w_cb, d)
    dw_ab = _weight_grad("mix_dw_attn_branch", o_t, dya, d)
    dw_cb = _weight_grad("mix_dw_conv_branch", g_t, dyc, d)
    dz_conv, dconv_w = _conv_bwd(z, dgconv, conv_pad)
    front = lax.broadcasted_iota(jnp.int32, (t, 1), 0) < ROW_PAD
    lse_heads = jnp.where(front, 1e9, lse[:, ::HEAD_DIM])
    dq, dk, dv, dfk, dfq = _attn_bwd(qkv, *_attn_bias_operands(f_heads, lse_heads), o, do)
    dz_f, db_forget = _gate_bwd(dfq, dfk, z, b_pad, f_col)
    dz_pieces = {"q": dq, "k": dk, "v": dv, "gates": dgates, "conv": dz_conv, "f": dz_f}
    dh1, dg_mix_pre = _mix_in_bwd(list(dz_pieces.values()), w_in_pad, h1, g_mix_pre, dh2)
    dw_t = {name: _weight_grad_t(f"mix_dw_in_{name}", u_t, piece) for name, piece in dz_pieces.items()}
    dw_in_t = jnp.concatenate(
        [dw_t["q"], dw_t["k"], dw_t["v"], dw_t["f"][:HEADS], dw_t["conv"], dw_t["gates"]], axis=0)
    mix_slots = [jnp.pad(dw_in_t.reshape(4, cs, d), ((0, 0), (0, cs_pad - cs), (0, 0))),
                 jnp.transpose(dw_ab.reshape(ATTN_W, 4, d // 4), (1, 0, 2)),
                 jnp.transpose(dw_cb.reshape(CONV_W, 4, d // 4), (1, 0, 2)),
                 dw_out.reshape(4, d // 4, d)]
    mix_got = _pair_send_halves_async("grad_pair_exchange_mix", mix_slots, 7)
    df1, dg_f1_post = _norm_bwd("ffn1_post_norm_bwd", f1, g_ffn1_post, dh1, 0.5)
    reduce_scatter("mix", ["w_in", "w_attn_branch", "w_conv_branch", "w_out"], mix_slots, 3, got=mix_got, after=df1)
    dab1, dw_f1_in, dw_f1_out = _ffn_bwd_weights("ffn1", df1, ab1, s1_t, n1_t, w_f1_in4, w_f1_out)
    dab1 = reduce_scatter("ffn1", ["w_ffn1_in", "w_ffn1_out"], [dw_f1_in, dw_f1_out.reshape(4, -1, d)], 4, dab1)
    dh0, dg_f1_pre = _mm_nt_norm_bwd("ffn1_in_bwd", dab1, w_f1_in4, h0, g_ffn1_pre, dh1)
    grad_x = dh0[N_FRONT:][None]
    dmeta = dh0[ROW_PAD:N_FRONT]
    small_grad = jnp.stack([
        _pack_small(dmeta[:, j * (d // 4):(j + 1) * (d // 4)], dconv_w[:3, j * 128:(j + 1) * 128],
                    [dg_f1_pre, dg_f1_post, dg_mix_pre, dg_mix_post, dg_f2_pre, dg_f2_post], db_forget[:, :HEADS],
                    loss_part[0, 0])
        for j in range(4)])
    reduce_scatter("small", ["small"], [small_grad], None)
    tags =["w_in", "w_attn_branch", "w_conv_branch", "w_out", "w_ffn1_in", "w_ffn1_out", "w_ffn2_in", "w_ffn2_out", "small"]
    halves = [reduced[tag][0] for tag in tags]
    others = [reduced[tag][1] for tag in tags]

    small = [g_ffn1_pre, g_ffn1_post, g_mix_pre, g_mix_post, g_ffn2_pre, g_ffn2_post]
    small_m = [m_g_ffn1_pre, m_g_ffn1_post, m_g_mix_pre, m_g_mix_post, m_g_ffn2_pre, m_g_ffn2_post]
    small_v = [v_g_ffn1_pre, v_g_ffn1_post, v_g_mix_pre, v_g_mix_post, v_g_ffn2_pre, v_g_ffn2_post]
    ws = big + [_pack_small(meta_tokens, conv_w[0], small, b_forget)]
    ms = [w_in_rows(m_w_in), m_w_attn_branch[0], m_w_conv_branch[0], m_w_out[0], m_w_ffn1_in[0], m_w_ffn1_out[0],
          m_w_ffn2_in[0], m_w_ffn2_out[0], _pack_small(m_meta_tokens, m_conv_w[0], small_m, m_b_forget)]
    vs = [w_in_rows(v_w_in), v_w_attn_branch[0], v_w_conv_branch[0], v_w_out[0], v_w_ffn1_in[0], v_w_ffn1_out[0],
          v_w_ffn2_in[0], v_w_ffn2_out[0], _pack_small(v_meta_tokens, v_conv_w[0], small_v, v_b_forget)]
    updates = [_adamw(tag, w, a, b, m, v, c_arr) for tag, w, a, b, m, v in zip(tags, ws, halves, others, ms, vs)]

    def leaves(big_vals, small_block):
        meta, conv, gains, bf = _unpack_small(small_block)
        w_in_t_, w_ab_, w_cb_, w_out_, f1_in, f1_out, f2_in, f2_out = [b[None] for b in big_vals]
        w_in_ = jnp.transpose(w_in_t_[:, :cs], (0, 2, 1))
        return [meta, w_in_, bf, conv, w_ab_, w_cb_, w_out_, gains[0], gains[1], f1_in, f1_out,
                gains[2], gains[3], gains[4], gains[5], f2_in, f2_out]

    out_g, out_d, out_m, out_v = [leaves([u_[k] for u_ in updates[:8]], updates[8][k]) for k in range(4)]
    loss = updates[8][0][LOSS_ROW, 0]
    return (loss, grad_x, *out_g, *out_d, *out_m, *out_v)
```

```python
import functools

import jax
import jax.numpy as jnp
from jax import lax
from jax.experimental import pallas as pl
from jax.experimental.pallas import tpu as pltpu
from jax.experimental.pallas import tpu_sc as plsc

N_META = 16
ROW_PAD = 112
N_FRONT = ROW_PAD + N_META
HEADS = 8
HEAD_DIM = 64
ATTN_W = HEADS * HEAD_DIM
CONV_W = 512
NORM_EPS = 1e-6
ROW_TILE = 640
F_PAD = 128
ATTN_Q_GROUP = 2
ATTN_KV_GROUP = 4
NEG = -1e30
ADAM_LR = 0.001
ADAM_B1 = 0.9
ADAM_B2 = 0.999
ADAM_EPS = 1e-08
ADAM_WD = 0.01
ADAM_STEP = 10
VMEM_BIG = 56 * 1024 * 1024
MESH = pl.DeviceIdType.MESH
ANY = pl.BlockSpec(memory_space=pl.ANY)
F32 = jnp.float32
BF16 = jnp.bfloat16


def _params(sem, vmem=None):
    return pltpu.CompilerParams(dimension_semantics=sem, vmem_limit_bytes=vmem)


def _sigmoid(x):
    return 1.0 / (1.0 + jnp.exp(-x))


def _rstd(x):
    return lax.rsqrt(jnp.mean(x * x, axis=-1, keepdims=True) + NORM_EPS)


def _rms_bwd(x, g, dy):
    r = _rstd(x)
    xr = x * r
    gdy = g * dy
    dx = r * (gdy - xr * jnp.mean(xr * gdy, axis=-1, keepdims=True))
    return dx, jnp.sum(dy * xr, axis=0, keepdims=True)


def _dot(a, b):
    return jnp.dot(a, b, preferred_element_type=F32)


def _dot_nt(a, b):
    return lax.dot_general(a, b, (((1,), (1,)), ((), ())), preferred_element_type=F32)


def _k_tile(t):
    return 1664 if t % 1664 == 0 else ROW_TILE


def _place():
    x, y, c = lax.axis_index("x"), lax.axis_index("y"), lax.axis_index("c")
    chips = [(1 - x, y), (x, 1 - y), (1 - x, 1 - y)]
    return x, y, c, chips


def _all_gather(shards):
    n = len(shards)
    split = [s.reshape(2, s.shape[0] // 2, s.shape[1]) for s in shards]

    def body(*refs):
        ins, outs = refs[:n], refs[n:2 * n]
        send_sems, recv_sems = refs[2 * n:]
        x, y, c, chips = _place()
        me = 2 * x + y
        sibling = (x, y, 1 - c)

        def remote(i, k, slot, part, to, src=None):
            dst = outs[i].at[slot, part]
            return pltpu.make_async_remote_copy(
                src_ref=dst if src is None else src, dst_ref=dst,
                send_sem=send_sems.at[i, k], recv_sem=recv_sems.at[i, k],
                device_id=to, device_id_type=MESH)

        started = []
        for i in range(n):
            for k, (cx, cy) in enumerate(chips):
                cp = remote(i, k, me, c, (cx, cy, c), src=ins[i].at[c])
                cp.start()
                started.append(cp)
        for i in range(n):
            for k, (cx, cy) in enumerate(chips):
                remote(i, k, 2 * cx + cy, c, (x, y, c)).wait_recv()
                cp = remote(i, 3 + k, 2 * cx + cy, c, sibling)
                cp.start()
                started.append(cp)
        for i in range(n):
            for k, (cx, cy) in enumerate(chips):
                remote(i, 3 + k, 2 * cx + cy, 1 - c, (x, y, c)).wait_recv()
        for cp in started:
            cp.wait_send()

    outs = pl.pallas_call(
        body, name="all_gather_weights",
        out_shape=[jax.ShapeDtypeStruct((4,) + s.shape, s.dtype) for s in split],
        in_specs=[ANY] * n, out_specs=[ANY] * n,
        scratch_shapes=[pltpu.SemaphoreType.DMA((n, 6)), pltpu.SemaphoreType.DMA((n, 6))],
    )(*split)
    me =2 * lax.axis_index("x") + lax.axis_index("y")
    outs = [lax.dynamic_update_slice(o, s[None], (me, 0, 0, 0)) for o, s in zip(outs, split)]
    return [o.reshape((4,) + s.shape) for o, s in zip(outs, shards)]


def _all_gather_async(name, shards, collective_id):
    n = len(shards)
    split = [s.reshape(2, s.shape[0] // 2, s.shape[1]) for s in shards]
    ins = [jax.new_ref(s, memory_space=pltpu.MemorySpace.HBM) for s in split]
    outs = [jax.empty_ref(jax.ShapeDtypeStruct((4,) + s.shape, s.dtype), memory_space=pltpu.MemorySpace.HBM)
            for s in split]

    @pl.kernel(mesh=plsc.ScalarSubcoreMesh(axis_name="sequencer", num_cores=1), name=name,
               scratch_types=(pltpu.SemaphoreType.DMA((n, 6)), pltpu.SemaphoreType.DMA((n, 6))),
               compiler_params=pltpu.CompilerParams(collective_id=collective_id))
    def launch(send_sems, recv_sems):
        x, y, c, chips = _place()
        me = 2 * x + y
        sibling = (x, y, 1 - c)
        barrier = pltpu.get_barrier_semaphore()
        for peer in [(cx, cy, c) for cx, cy in chips] + [sibling]:
            pl.semaphore_signal(barrier, inc=1, device_id=peer, device_id_type=MESH)
        pl.semaphore_wait(barrier, 4)

        def remote(i, k, slot, part, to, src=None):
            dst = outs[i].at[slot, part]
            return pltpu.make_async_remote_copy(
                src_ref=dst if src is None else src, dst_ref=dst,
                send_sem=send_sems.at[i, k], recv_sem=recv_sems.at[i, k],
                device_id=to, device_id_type=MESH)

        started = []
        for i in range(n):
            for k, (cx, cy) in enumerate(chips):
                cp = remote(i, k, me, c, (cx, cy, c), src=ins[i].at[c])
                cp.start()
                started.append(cp)
        for i in range(n):
            for k, (cx, cy) in enumerate(chips):
                remote(i, k, 2 * cx + cy, c, (x, y, c)).wait_recv()
                cp = remote(i, 3 + k, 2 * cx + cy, c, sibling)
                cp.start()
                started.append(cp)
        for i in range(n):
            for k, (cx, cy) in enumerate(chips):
                remote(i, 3 + k, 2 * cx + cy, 1 - c, (x, y, c)).wait_recv()
        for cp in started:
            cp.wait_send()

    launch()
    raw = [o[...] for o in outs]

    def finish(after, which):
        arrived, _ = lax.optimization_barrier(([raw[i] for i in which], after))
        me = 2 * lax.axis_index("x") + lax.axis_index("y")
        gathered = [lax.dynamic_update_slice(a, split[i][None], (me, 0, 0, 0)) for a, i in zip(arrived, which)]
        return [g.reshape((4,) + shards[i].shape) for g, i in zip(gathered, which)]

    return finish


def _pair_send_halves(name, grads):
    n = len(grads)

    def body(*refs):
        ins, outs = refs[:n], refs[n:2 * n]
        send_sems, recv_sems = refs[2 * n:]
        x, y, c, _ = _place()
        cps = []
        for i in range(n):
            half = ins[i].shape[1] // 2
            cp = pltpu.make_async_remote_copy(
                src_ref=ins[i].at[:, pl.ds((1 - c) * half, half)], dst_ref=outs[i],
                send_sem=send_sems.at[i], recv_sem=recv_sems.at[i],
                device_id=(x, y, 1 - c), device_id_type=MESH)
            cp.start()
            cps.append(cp)
        for cp in cps:
            cp.wait()

    return pl.pallas_call(
        body, name=name,
        out_shape=[jax.ShapeDtypeStruct((4, g.shape[1] // 2, g.shape[2]), g.dtype) for g in grads],
        in_specs=[ANY] * n, out_specs=[ANY] * n,
        scratch_shapes=[pltpu.SemaphoreType.DMA((n,)), pltpu.SemaphoreType.DMA((n,))],
    )(*grads)


def _pair_send_halves_async(name, grads, collective_id):
    n = len(grads)
    ins = [jax.new_ref(g, memory_space=pltpu.MemorySpace.HBM) for g in grads]
    outs = [jax.empty_ref(jax.ShapeDtypeStruct((4, g.shape[1] // 2, g.shape[2]), g.dtype),
                          memory_space=pltpu.MemorySpace.HBM) for g in grads]

    @pl.kernel(mesh=plsc.ScalarSubcoreMesh(axis_name="sequencer", num_cores=1), name=name,
               scratch_types=(pltpu.SemaphoreType.DMA((n,)), pltpu.SemaphoreType.DMA((n,))),
               compiler_params=pltpu.CompilerParams(collective_id=collective_id))
    def launch(send_sems, recv_sems):
        x, y, c, _ = _place()
        barrier = pltpu.get_barrier_semaphore()
        pl.semaphore_signal(barrier, inc=1, device_id=(x, y, 1 - c), device_id_type=MESH)
        pl.semaphore_wait(barrier, 1)
        cps = []
        for i in range(n):
            half = ins[i].shape[1] // 2
            cp = pltpu.make_async_remote_copy(
                src_ref=ins[i].at[:, pl.ds((1 - c) * half, half)], dst_ref=outs[i],
                send_sem=send_sems.at[i], recv_sem=recv_sems.at[i],
                device_id=(x, y, 1 - c), device_id_type=MESH)
            cp.start()
            cps.append(cp)
        for cp in cps:
            cp.wait()

    launch()
    return [o[...] for o in outs]


def _chip_scatter(name, parts):
    n = len(parts)

    def body(*refs):
        _scatter_copies(refs[:n], refs[n:2 * n], *refs[2 * n:])

    arrived = pl.pallas_call(
        body, name=name,
        out_shape=[jax.ShapeDtypeStruct(p.shape, p.dtype) for p in parts],
        in_specs=[ANY] * n, out_specs=[ANY] * n,
        scratch_shapes=[pltpu.SemaphoreType.DMA((n, 3)), pltpu.SemaphoreType.DMA((n, 3))],
    )(*parts)
    return _own_slots(parts, arrived)


def _scatter_copies(ins, outs, send_sems, recv_sems):
    x, y, c, chips = _place()
    me = 2 * x + y
    sends = []
    for i in range(len(ins)):
        for k, (cx, cy) in enumerate(chips):
            cp = pltpu.make_async_remote_copy(
                src_ref=ins[i].at[2 * cx + cy], dst_ref=outs[i].at[me],
                send_sem=send_sems.at[i, k], recv_sem=recv_sems.at[i, k],
                device_id=(cx, cy, c), device_id_type=MESH)
            cp.start()
            sends.append(cp)
    for i in range(len(ins)):
        for k, (cx, cy) in enumerate(chips):
            got = outs[i].at[2 * cx + cy]
            pltpu.make_async_remote_copy(
                src_ref=got, dst_ref=got, send_sem=send_sems.at[i, k], recv_sem=recv_sems.at[i, k],
                device_id=(x, y, c), device_id_type=MESH).wait_recv()
    for cp in sends:
        cp.wait_send()


def _own_slots(parts, arrived):
    me = 2 * lax.axis_index("x") + lax.axis_index("y")
    return [lax.dynamic_update_slice(a, lax.dynamic_slice_in_dim(p, me, 1, axis=0), (me, 0, 0))
            for p, a in zip(parts, arrived)]


def _chip_scatter_async(name, parts, collective_id):
    n = len(parts)
    ins = [jax.new_ref(p, memory_space=pltpu.MemorySpace.HBM) for p in parts]
    outs = [jax.empty_ref(jax.ShapeDtypeStruct(p.shape, p.dtype), memory_space=pltpu.MemorySpace.HBM) for p in parts]

    @pl.kernel(mesh=plsc.ScalarSubcoreMesh(axis_name="sequencer", num_cores=1), name=name,
               scratch_types=(pltpu.SemaphoreType.DMA((n, 3)), pltpu.SemaphoreType.DMA((n, 3))),
               compiler_params=pltpu.CompilerParams(collective_id=collective_id))
    def launch(send_sems, recv_sems):
        x, y, c, chips = _place()
        barrier = pltpu.get_barrier_semaphore()
        for cx, cy in chips:
            pl.semaphore_signal(barrier, inc=1, device_id=(cx, cy, c), device_id_type=MESH)
        pl.semaphore_wait(barrier, 3)
        _scatter_copies(ins, outs, send_sems, recv_sems)

    launch()
    return _own_slots(parts, [o[...] for o in outs])


def _pair_swap(name, halves):
    n = len(halves)

    def body(*refs):
        ins, outs = refs[:n], refs[n:2 * n]
        send_sems, recv_sems = refs[2 * n:]
        x, y, c, _ = _place()
        cps = []
        for i in range(n):
            cp = pltpu.make_async_remote_copy(
                src_ref=ins[i], dst_ref=outs[i], send_sem=send_sems.at[i], recv_sem=recv_sems.at[i],
                device_id=(x, y, 1 - c), device_id_type=MESH)
            cp.start()
            cps.append(cp)
        for cp in cps:
            cp.wait()

    return pl.pallas_call(
        body, name=name,
        out_shape=[jax.ShapeDtypeStruct(h.shape, h.dtype) for h in halves],
        in_specs=[ANY] * n, out_specs=[ANY] * n,
        scratch_shapes=[pltpu.SemaphoreType.DMA((n,)), pltpu.SemaphoreType.DMA((n,))],
    )(*halves)


def _row_block(rows, cols, n_bufs, budget=20 * 1024 * 1024):
    best = min(rows, 16)
    for b in range(16, rows + 1, 16):
        if rows % b == 0 and 2 * n_bufs * b * cols * 4 <= budget:
            best = b
    return best


def _pair_add(tag, grad, got, c_arr, out_dtype):
    _, rows, cols = grad.shape
    half = rows // 2
    bh = _row_block(half, cols, 3)
    nb = half // bh

    def body(c_ref, g_ref, a_ref, o_ref):
        o_ref[...] = (g_ref[...] + a_ref[...]).astype(out_dtype)

    return pl.pallas_call(
        body, name=f"pair_add_{tag}",
        out_shape=jax.ShapeDtypeStruct((4, half, cols), out_dtype),
        grid_spec=pltpu.PrefetchScalarGridSpec(
            num_scalar_prefetch=1, grid=(4, nb),
            in_specs=[pl.BlockSpec((None, bh, cols), lambda j, r, c: (j, c[0] * nb + r, 0)),
                      pl.BlockSpec((None, bh, cols), lambda j, r, c: (j, r, 0))],
            out_specs=pl.BlockSpec((None, bh, cols), lambda j, r, c: (j, r, 0))),
        compiler_params=_params(("parallel", "parallel")),
    )(c_arr, grad, got)


def _chip_add(tag, parts):
    _, half, cols = parts.shape
    bh = _row_block(half, cols, 5)

    def body(p_ref, o_ref):
        a, b, c, d = [p_ref[j].astype(F32) for j in range(4)]
        o_ref[...] = ((a + b) + c) + d

    return pl.pallas_call(
        body, name=f"chip_add_{tag}",
        out_shape=jax.ShapeDtypeStruct((half, cols), F32),
        grid=(half // bh,),
        in_specs=[pl.BlockSpec((4, bh, cols), lambda r: (0, r, 0))],
        out_specs=pl.BlockSpec((bh, cols), lambda r: (r, 0)),
        compiler_params=_params(("parallel",)),
    )(parts)


def _adamw(tag, w, mine, theirs, m, v, c_arr):
    rows, cols = w.shape
    half = rows // 2
    br = _row_block(half, cols, 9)
    nb = half // br

    def body(c_ref, w_ref, a_ref, b_ref, m_ref, v_ref, g_ref, d_ref, mo_ref, vo_ref):
        own = (pl.program_id(0) // nb) == c_ref[0]
        g = jnp.where(own, a_ref[...], b_ref[...])
        g_ref[...] = g
        m_new = ADAM_B1 * m_ref[...] + (1.0 - ADAM_B1) * g
        v_new = ADAM_B2 * v_ref[...] + (1.0 - ADAM_B2) * (g * g)
        m_hat = m_new / (1.0 - ADAM_B1 ** ADAM_STEP)
        v_hat = v_new / (1.0 - ADAM_B2 ** ADAM_STEP)
        d_ref[...] = -ADAM_LR * (m_hat / (jnp.sqrt(v_hat) + ADAM_EPS) + ADAM_WD * w_ref[...])
        mo_ref[...] = m_new
        vo_ref[...] = v_new

    spec = pl.BlockSpec((br, cols), lambda r, c: (r, 0))
    mine_spec = pl.BlockSpec((br, cols), lambda r, c: (jnp.clip(r - c[0] * nb, 0, nb - 1), 0))
    theirs_spec = pl.BlockSpec((br, cols), lambda r, c: (jnp.clip(r - (1 - c[0]) * nb, 0, nb - 1), 0))
    return pl.pallas_call(
        body, name=f"adamw_{tag}",
        out_shape=[jax.ShapeDtypeStruct((rows, cols), F32)] * 4,
        grid_spec=pltpu.PrefetchScalarGridSpec(
            num_scalar_prefetch=1, grid=(rows // br,),
            in_specs=[spec, mine_spec, theirs_spec, spec, spec], out_specs=[spec] * 4),
        compiler_params=_params(("arbitrary",)),
    )(c_arr, w, mine, theirs, m, v)


def _matmul(name, x, w, out_shape, grid, x_spec, w_spec, o_spec, *, nt=False, vmem=None):
    nk = grid[2]
    acc_shape = tuple(d for d in o_spec.block_shape if d is not None)

    def body(x_ref, w_ref, o_ref, acc_ref):
        k = pl.program_id(2)
        part = _dot_nt(x_ref[...], w_ref[...]) if nt else _dot(x_ref[...], w_ref[...])
        if nk == 1:
            o_ref[...] = part.astype(o_ref.dtype)
        else:
            @pl.when(k == 0)
            def _():
                acc_ref[...] = part

            @pl.when(k > 0)
            def _():
                acc_ref[...] += part

            @pl.when(k == nk - 1)
            def _():
                o_ref[...] = acc_ref[...].astype(o_ref.dtype)

    return pl.pallas_call(
        body, name=name, out_shape=out_shape, grid=grid,
        in_specs=[x_spec, w_spec], out_specs=o_spec,
        scratch_shapes=[pltpu.VMEM(acc_shape if nk > 1 else (8, 128), F32)],
        compiler_params=_params(("parallel", "parallel", "arbitrary"), vmem),
    )(x, w)


def _weight_grad(name, xt, dy, bn, out_rows=None):
    m, t = xt.shape
    n = dy.shape[1]
    bm = m if out_rows is None else out_rows
    bk = _k_tile(t)
    return _matmul(
        name, xt, dy, jax.ShapeDtypeStruct((m, n), F32), (m // bm, n // bn, t // bk),
        pl.BlockSpec((bm, bk), lambda a, b, k: (a, k)),
        pl.BlockSpec((bk, bn), lambda a, b, k: (k, b)),
        pl.BlockSpec((bm, bn), lambda a, b, k: (a, b)), vmem=VMEM_BIG)


def _weight_grad_t(name, xt, dy):
    m, t = xt.shape
    n = dy.shape[1]
    bn = min(n, 512)
    bk = _k_tile(t)
    nk = t // bk

    def body(x_ref, dy_ref, o_ref, acc_ref):
        k = pl.program_id(1)
        part = _dot(x_ref[...], dy_ref[...].astype(BF16))

        @pl.when(k == 0)
        def _():
            acc_ref[...] = part

        @pl.when(k > 0)
        def _():
            acc_ref[...] += part

        @pl.when(k == nk - 1)
        def _():
            o_ref[...] = acc_ref[...].T

    return pl.pallas_call(
        body, name=name, out_shape=jax.ShapeDtypeStruct((n, m), F32), grid=(n // bn, nk),
        in_specs=[pl.BlockSpec((m, bk), lambda b, k: (0, k)), pl.BlockSpec((bk, bn), lambda b, k: (k, b))],
        out_specs=pl.BlockSpec((bn, m), lambda b, k: (b, 0)),
        scratch_shapes=[pltpu.VMEM((m, bn), F32)],
        compiler_params=_params(("parallel", "arbitrary"), VMEM_BIG),
    )(xt, dy)


def _mix_in_bwd(pieces, wt, h, g, dh_in):
    t, d = h.shape
    tm = ROW_TILE // 2
    widths = [p.shape[1] for p in pieces]
    n = len(pieces)

    def body(*refs):
        dy_refs, (w_ref, h_ref, g_ref, dhi_ref, dh_ref, dg_ref) = refs[:n], refs[n:]

        @pl.when(pl.program_id(0) == 0)
        def _():
            dg_ref[...] = jnp.zeros_like(dg_ref)

        dn, off = None, 0
        for dy_ref, wd in zip(dy_refs, widths):
            part = _dot(dy_ref[...].astype(BF16), w_ref[off:off + wd, :])
            dn = part if dn is None else dn + part
            off += wd
        dx, dg = _rms_bwd(h_ref[...], g_ref[...], dn)
        dh_ref[...] = dhi_ref[...] + dx
        dg_ref[...] += dg

    row = pl.BlockSpec((tm, d), lambda i: (i, 0))
    vec = pl.BlockSpec((1, d), lambda i: (0, 0))
    return pl.pallas_call(
        body, name="mix_in_bwd",
        out_shape=[jax.ShapeDtypeStruct((t, d), F32), jax.ShapeDtypeStruct((1, d), F32)],
        grid=(t // tm,),
        in_specs=[pl.BlockSpec((tm, wd), lambda i: (i, 0)) for wd in widths]
        + [pl.BlockSpec(wt.shape, lambda i: (0, 0)), row, vec, row],
        out_specs=[row, vec],
        compiler_params=_params(("arbitrary",), VMEM_BIG),
    )(*pieces, wt, h, g, dh_in)


def _read_token_rows(src_hbm, buf, sems, i, n):
    tm = buf.shape[1]

    def first_tile():
        return pltpu.make_async_copy(src_hbm.at[pl.ds(0, tm - N_FRONT)], buf.at[0, pl.ds(N_FRONT, tm - N_FRONT)],
                                     sems.at[0])

    def tile(j):
        return pltpu.make_async_copy(src_hbm.at[pl.ds(pl.multiple_of(j * tm - N_FRONT, N_FRONT), tm)],
                                     buf.at[j % 2], sems.at[j % 2])

    @pl.when(i == 0)
    def _():
        buf[0, 0:N_FRONT, :] = jnp.zeros((N_FRONT, buf.shape[2]), buf.dtype)
        first_tile().start()

    @pl.when(i + 1 < n)
    def _():
        tile(i + 1).start()

    @pl.when(i == 0)
    def _():
        first_tile().wait()

    @pl.when(i > 0)
    def _():
        tile(i).wait()

    return buf.at[i % 2]


def _embed_norm(x, meta, g):
    seq, d = x.shape
    t = seq + N_FRONT
    tm = ROW_TILE

    def body(x_hbm, meta_ref, g_ref, h_ref, n_ref, nt_ref, buf, sems):
        i = pl.program_id(0)
        rows = _read_token_rows(x_hbm, buf, sems, i, t // tm)

        @pl.when(i == 0)
        def _():
            buf[0, ROW_PAD:N_FRONT, :] = meta_ref[...]

        h = rows[...]
        h_ref[...] = h
        y = h * _rstd(h) * g_ref[...]
        n_ref[...] = y.astype(BF16)
        nt_ref[...] = y.T.astype(BF16)

    row = pl.BlockSpec((tm, d), lambda i: (i, 0))
    return pl.pallas_call(
        body, name="embed_and_ffn1_pre_norm",
        out_shape=[jax.ShapeDtypeStruct((t, d), F32), jax.ShapeDtypeStruct((t, d), BF16),
                   jax.ShapeDtypeStruct((d, t), BF16)],
        grid=(t // tm,),
        in_specs=[ANY, pl.BlockSpec((N_META, d), lambda i: (0, 0)), pl.BlockSpec((1, d), lambda i: (0, 0))],
        out_specs=[row, row, pl.BlockSpec((d, tm), lambda i: (0, i))],
        scratch_shapes=[pltpu.VMEM((2, tm, d), F32), pltpu.SemaphoreType.DMA((2,))],
        compiler_params=_params(("arbitrary",)),
    )(x, meta, g)


def _slot_of(kk):
    return (kk % 2) * 2 + kk // 2


def _ffn_in(name, n, w4):
    t, d = n.shape
    cw = w4.shape[2]
    tm = ROW_TILE

    def body(x_ref, wg_ref, wu_ref, ab_ref, s_ref, st_ref):
        x = x_ref[...]
        a = _dot(x, wg_ref[...])
        b = _dot(x, wu_ref[...])
        ab_ref[:, :cw] = a.astype(BF16)
        ab_ref[:, cw:] = b.astype(BF16)
        s = a * _sigmoid(a) * b
        s_ref[...] = s.astype(BF16)
        st_ref[...] = s.T.astype(BF16)

    return pl.pallas_call(
        body, name=name,
        out_shape=[jax.ShapeDtypeStruct((t, 4 * cw), BF16), jax.ShapeDtypeStruct((t, 2 * cw), BF16),
                   jax.ShapeDtypeStruct((2 * cw, t), BF16)],
        grid=(2, t // tm),
        in_specs=[pl.BlockSpec((tm, d), lambda j, i: (i, 0)),
                  pl.BlockSpec((None, d, cw), lambda j, i: (j, 0, 0)),
                  pl.BlockSpec((None, d, cw), lambda j, i: (2 + j, 0, 0))],
        out_specs=[pl.BlockSpec((tm, 2 * cw), lambda j, i: (i, j)),
                   pl.BlockSpec((tm, cw), lambda j, i: (i, j)),
                   pl.BlockSpec((cw, tm), lambda j, i: (j, i))],
        compiler_params=_params(("parallel", "parallel"), VMEM_BIG),
    )(n, w4, w4)


def _mm_resid_norm(name, x, w, h, g_post, alpha, g_next):
    t, kdim = x.shape
    d = w.shape[1]
    tm = ROW_TILE
    with_next = g_next is not None

    def body(x_ref, w_ref, h_ref, gp_ref, gn_ref, f_ref, hn_ref, *rest):
        f = _dot(x_ref[...], w_ref[...])
        f_ref[...] = f
        hn = h_ref[...] + alpha * (f * _rstd(f) * gp_ref[...])
        hn_ref[...] = hn
        if with_next:
            y = hn * _rstd(hn) * gn_ref[...]
            rest[0][...] = y.astype(BF16)
            rest[1][...] = y.T.astype(BF16)

    row = lambda i: (i, 0)
    vec = pl.BlockSpec((1, d), lambda i: (0, 0))
    out_shape = [jax.ShapeDtypeStruct((t, d), F32), jax.ShapeDtypeStruct((t, d), F32)]
    out_specs = [pl.BlockSpec((tm, d), row), pl.BlockSpec((tm, d), row)]
    if with_next:
        out_shape += [jax.ShapeDtypeStruct((t, d), BF16), jax.ShapeDtypeStruct((d, t), BF16)]
        out_specs += [pl.BlockSpec((tm, d), row), pl.BlockSpec((d, tm), lambda i: (0, i))]
    return pl.pallas_call(
        body, name=name, out_shape=out_shape, grid=(t // tm,),
        in_specs=[pl.BlockSpec((tm, kdim), row), pl.BlockSpec((kdim, d), lambda i: (0, 0)),
                  pl.BlockSpec((tm, d), row), vec, vec],
        out_specs=out_specs,
        compiler_params=_params(("parallel",), VMEM_BIG),
    )(x, w, h, g_post, g_post if g_next is None else g_next)


def _in_proj(u, w):
    t, d = u.shape
    nz = w.shape[0]
    nq = 3 * ATTN_W
    tm = ROW_TILE // 2

    def body(u_ref, w_ref, qkv_ref, z_ref):
        qkv_ref[...] = _dot_nt(u_ref[...], w_ref[0:nq, :]).astype(BF16)
        z_ref[...] = _dot_nt(u_ref[...], w_ref[nq:, :])

    return pl.pallas_call(
        body, name="mix_in_proj",
        out_shape=[jax.ShapeDtypeStruct((t, nq), BF16), jax.ShapeDtypeStruct((t, nz - nq), F32)],
        grid=(t // tm,),
        in_specs=[pl.BlockSpec((tm, d), lambda i: (i, 0)), pl.BlockSpec((nz, d), lambda i: (0, 0))],
        out_specs=[pl.BlockSpec((tm, nq), lambda i: (i, 0)), pl.BlockSpec((tm, nz - nq), lambda i: (i, 0))],
        compiler_params=_params(("parallel",), VMEM_BIG),
    )(u, w)


def _gate_prep(z, b_pad, f_col):
    t = z.shape[0]
    tm = ROW_TILE

    def body(z_ref, b_ref, f_ref, carry_ref):
        i = pl.program_id(0)

        @pl.when(i == 0)
        def _():
            carry_ref[...] = jnp.zeros_like(carry_ref)

        xs = z_ref[...] + b_ref[...]
        logf = jnp.minimum(xs, 0.0) - jnp.log(1.0 + jnp.exp(-jnp.abs(xs)))
        row = i * tm + lax.broadcasted_iota(jnp.int32, (tm, 1), 0)
        logf = jnp.where(row >= ROW_PAD, logf, 0.0)
        tri = (lax.broadcasted_iota(jnp.int32, (tm, tm), 0) >= lax.broadcasted_iota(jnp.int32, (tm, tm), 1))
        f = jnp.dot(tri.astype(F32), logf, preferred_element_type=F32, precision=lax.Precision.HIGHEST)
        f = f + carry_ref[0:1, :]
        f_ref[...] = f
        carry_ref[...] = jnp.broadcast_to(f[tm - 1:tm, :], carry_ref.shape)

    return pl.pallas_call(
        body, name="forget_gate_cumsum", out_shape=jax.ShapeDtypeStruct((t, 128), F32),
        grid=(t // tm,),
        in_specs=[pl.BlockSpec((tm, 128), lambda i: (i, f_col // 128)), pl.BlockSpec((1, 128), lambda i: (0, 0))],
        out_specs=pl.BlockSpec((tm, 128), lambda i: (i, 0)),
        scratch_shapes=[pltpu.VMEM((8, 128), F32)],
        compiler_params=_params(("arbitrary",)),
    )(z, b_pad)


def _lane_halves():
    lane = lax.broadcasted_iota(jnp.int32, (1, 128), 1)
    return lane < HEAD_DIM


def _causal_mask(tq, tk, row0=0):
    row = row0 + lax.broadcasted_iota(jnp.int32, (tq, 1), 0)
    col = lax.broadcasted_iota(jnp.int32, (1, tk), 1)
    return col <= row


def _lane_one(lane):
    return (lax.broadcasted_iota(jnp.int32, (1, 128), 1) == lane).astype(BF16)


def _split3(x):
    hi = x.astype(BF16)
    rest = x - hi.astype(F32)
    mid = rest.astype(BF16)
    return hi, mid, (rest - mid.astype(F32)).astype(BF16)


def _split3_glue(x):
    hi = lax.reduce_precision(x, 8, 7)
    mid = lax.reduce_precision(x - hi, 8, 7)
    lo = lax.reduce_precision((x - hi) - mid, 8, 7)
    return hi.astype(BF16), mid.astype(BF16), lo.astype(BF16)


def _aug_pairs(cols):
    t = cols[0].shape[0]
    a = jnp.pad(jnp.stack(cols, axis=2), ((0, 0), (0, 0), (0, HEAD_DIM - len(cols))))
    a = a.reshape(t, 4, 2, HEAD_DIM)[:, :, ::-1, :]
    return jnp.transpose(a.reshape(t, 4, 128), (1, 0, 2))


def _attn_bias_operands(f_heads, lse_heads=None):
    t = f_heads.shape[0]
    one = jnp.ones((t, HEADS), BF16)
    row = lax.broadcasted_iota(jnp.int32, (t, 1), 0)
    fq = _split3_glue(f_heads)
    fk = _split3_glue(jnp.where(row < ROW_PAD, 1e9, f_heads))
    q_cols = list(fq) + [one] * 3
    k_cols = [one] * 3 + [-c for c in fk]
    if lse_heads is not None:
        q_cols += [-c for c in _split3_glue(lse_heads)]
        k_cols += [one] * 3
    return _aug_pairs(q_cols), _aug_pairs(k_cols)


def _attn_fwd(z, aug_q, aug_k):
    t = z.shape[0]
    tq = tk = ROW_TILE
    nq = t // tq
    grp = ATTN_KV_GROUP
    steps = [(qi, ka) for qi in range(nq) for ka in range(0, qi + 1, grp)]
    q_tab = jnp.array([qi for qi, _ in steps], jnp.int32)
    k_tab = jnp.array([ka for _, ka in steps], jnp.int32)

    def body(qt_ref, kt_ref, q_ref, *refs):
        k_refs, v_refs, aq_ref, ak_refs = refs[:grp], refs[grp:2 * grp], refs[2 * grp], refs[2 * grp + 1:3 * grp + 1]
        o_ref, lse_ref, m_ref, l_ref, acc_ref = refs[3 * grp + 1:]
        step = pl.program_id(1)
        qi, ka = qt_ref[step], kt_ref[step]

        @pl.when(ka == 0)
        def _():
            m_ref[...] = jnp.full_like(m_ref, NEG)
            l_ref[...] = jnp.zeros_like(l_ref)
            acc_ref[...] = jnp.zeros_like(acc_ref)

        def sweep(diagonal):
            first = _lane_halves()
            halves = (first, jnp.logical_not(first))
            q = (q_ref[...] * (HEAD_DIM ** -0.5)).astype(BF16)
            aq = aq_ref[...]
            qa = [jnp.where(lanes, q, aq) for lanes in halves]
            blocks = list(zip(k_refs, v_refs, ak_refs, diagonal))
            s = []
            for k_ref, _, ak_ref, diag in blocks:
                k, ak = k_ref[...].astype(BF16), ak_ref[...]
                for hh, lanes in enumerate(halves):
                    s_c = _dot_nt(qa[hh], jnp.where(lanes, k, ak))
                    s.append(jnp.where(_causal_mask(tq, tk), s_c, NEG) if diag else s_c)
            nb = len(blocks)
            m_prev = [m_ref[:, c0:c0 + 1] for c0 in (0, HEAD_DIM)]
            m_new = []
            for hh in range(2):
                m_h = m_prev[hh]
                for b in range(nb):
                    m_h = jnp.maximum(m_h, jnp.max(s[2 * b + hh], axis=1, keepdims=True))
                m_new.append(m_h)
            pv = [None, None]
            for b, (_, v_ref, _, _) in enumerate(blocks):
                v = v_ref[...].astype(BF16)
                for hh, (lanes, a0) in enumerate(zip(halves, (HEAD_DIM, 0))):
                    part = _dot(jnp.exp(s[2 * b + hh] - m_new[hh]).astype(BF16), jnp.where(lanes, v, _lane_one(a0)))
                    pv[hh] = part if pv[hh] is None else pv[hh] + part
            al0, al1 = [jnp.exp(mp - m_h) for mp, m_h in zip(m_prev, m_new)]
            l0 = al0 * l_ref[:, 0:1] + pv[0][:, HEAD_DIM:HEAD_DIM + 1]
            l1 = al1 * l_ref[:, HEAD_DIM:HEAD_DIM + 1] + pv[1][:, 0:1]
            acc_ref[...] = acc_ref[...] * jnp.where(first, al0, al1) + jnp.where(first, pv[0], pv[1])
            m_ref[...] = jnp.where(first, m_new[0], m_new[1])
            l_ref[...] = jnp.where(first, l0, l1)

        def finish():
            o_ref[...] = acc_ref[...] / l_ref[...]
            lse_ref[...] = m_ref[...] + jnp.log(l_ref[...])

        @pl.when(ka + grp - 1 < qi)
        def _():
            sweep((False,) * grp)

        for nb in range(1, grp + 1):
            @pl.when(ka + nb - 1 == qi)
            def _(nb=nb):
                sweep((False,) * (nb - 1) + (True,))
                finish()

    def kblock(j):
        return lambda s, qt, kt: jnp.minimum(kt[s] + j, qt[s])

    kbs = [kblock(j) for j in range(grp)]
    return pl.pallas_call(
        body, name="attention_fwd",
        out_shape=[jax.ShapeDtypeStruct((t, ATTN_W), F32), jax.ShapeDtypeStruct((t, ATTN_W), F32)],
        grid_spec=pltpu.PrefetchScalarGridSpec(
            num_scalar_prefetch=2, grid=(4, len(steps)),
            in_specs=[pl.BlockSpec((tq, 128), lambda p, s, qt, kt: (qt[s], p))]
            + [pl.BlockSpec((tk, 128), functools.partial(lambda p, s, qt, kt, kb: (kb(s, qt, kt), 4 + p), kb=kb))
               for kb in kbs]
            + [pl.BlockSpec((tk, 128), functools.partial(lambda p, s, qt, kt, kb: (kb(s, qt, kt), 8 + p), kb=kb))
               for kb in kbs]
            + [pl.BlockSpec((None, tq, 128), lambda p, s, qt, kt: (p, qt[s], 0))]
            + [pl.BlockSpec((None, tk, 128), functools.partial(lambda p, s, qt, kt, kb: (p, kb(s, qt, kt), 0), kb=kb))
               for kb in kbs],
            out_specs=[pl.BlockSpec((tq, 128), lambda p, s, qt, kt: (qt[s], p)),
                       pl.BlockSpec((tq, 128), lambda p, s, qt, kt: (qt[s], p))],
            scratch_shapes=[pltpu.VMEM((tq, 128), F32)] * 3),
        compiler_params=_params(("parallel", "arbitrary"), VMEM_BIG),
    )(q_tab, k_tab, z, *([z] * (2 * grp)), aug_q, *([aug_k] * grp))


def _attn_bwd(z, aug_q, aug_k, o, do):
    t = z.shape[0]
    tq = tk = ROW_TILE
    nq = t // tq
    grp = ATTN_Q_GROUP
    steps = [(qa, ki) for ki in range(nq) for qa in range(ki, nq, grp)]
    q_tab = jnp.array([qa for qa, _ in steps], jnp.int32)
    k_tab = jnp.array([ki for _, ki in steps], jnp.int32)
    tn = (((0,), (0,)), ((), ()))

    def body(qt_ref, kt_ref, *refs):
        q_refs, (k_ref, v_ref) = refs[:grp], refs[grp:grp + 2]
        aq_refs, ak_ref = refs[grp + 2:2 * grp + 2], refs[2 * grp + 2]
        o_refs, do_refs = refs[2 * grp + 3:3 * grp + 3], refs[3 * grp + 3:4 * grp + 3]
        dq_ref, dk_ref, dv_ref, dfk_ref, dfq_ref = refs[4 * grp + 3:]
        step = pl.program_id(1)
        qa, ki = qt_ref[step], kt_ref[step]

        def rows(j):
            return pl.ds(pl.multiple_of((qa + j) * tq, tq), tq)

        for j in range(grp):
            @pl.when((ki == 0) & (qa + j < nq))
            def _(j=j):
                dq_ref[rows(j), :] = jnp.zeros((tq, 128), F32)
                dfq_ref[rows(j), :] = jnp.zeros((tq, 128), F32)

        @pl.when(qa == ki)
        def _():
            dk_ref[...] = jnp.zeros_like(dk_ref)
            dv_ref[...] = jnp.zeros_like(dv_ref)
            dfk_ref[...] = jnp.zeros_like(dfk_ref)

        def sweep(nb, diagonal):
            first = _lane_halves()
            lane = lax.broadcasted_iota(jnp.int32, (1, 128), 1)
            scale = HEAD_DIM ** -0.5
            halves = (first, jnp.logical_not(first))
            spare = (HEAD_DIM, 0)
            k = k_ref[...].astype(BF16)
            v = v_ref[...].astype(BF16)
            ak = ak_ref[...]
            k_bias = [jnp.where(lanes, k, ak) for lanes in halves]
            k_ones = [jnp.where(lanes, k, _lane_one(a)) for lanes, a in zip(halves, spare)]
            v_ones = [jnp.where(lanes, v, ((lane >= a) & (lane < a + 3)).astype(BF16)) for lanes, a in zip(halves, spare)]
            chains = [(j, hh) for j in range(nb) for hh in range(2)]
            q16, do16, dos = [], [], []
            for j in range(nb):
                q16.append((q_refs[j][...] * scale).astype(BF16))
                do_ = do_refs[j][...]
                do16.append(do_.astype(BF16))
                od = o_refs[j][...] * do_
                for lanes, a in zip(halves, spare):
                    d_hi, d_mid, d_lo = _split3(jnp.sum(jnp.where(lanes, od, 0.0), axis=1, keepdims=True))
                    minus_delta = jnp.where(lane == a, -d_hi, jnp.where(lane == a + 1, -d_mid,
                                            jnp.where(lane == a + 2, -d_lo, jnp.zeros((), BF16))))
                    dos.append(jnp.where(lanes, do16[j], minus_delta))
            s = [_dot_nt(jnp.where(halves[hh], q16[j], aq_refs[j][...]), k_bias[hh]) for j, hh in chains]
            dp = [_dot_nt(dos[2 * j + hh], v_ones[hh]) for j, hh in chains]
            p = [jnp.exp(s_c) for s_c in s]
            if diagonal:
                p = [jnp.where(_causal_mask(tq, tk), p_c, 0.0) if j == 0 else p_c for p_c, (j, _) in zip(p, chains)]
            ds16 = [(p_c * dp_c).astype(BF16) for p_c, dp_c in zip(p, dp)]
            dv, dk = [None, None], [None, None]
            for c, (j, hh) in enumerate(chains):
                lanes = halves[hh]
                dv_c = lax.dot_general(p[c].astype(BF16), jnp.where(lanes, do16[j], jnp.zeros((), BF16)), tn,
                                       preferred_element_type=F32)
                dk_c = lax.dot_general(ds16[c], jnp.where(lanes, q16[j], _lane_one(spare[hh])), tn,
                                       preferred_element_type=F32)
                dv[hh] = dv_c if dv[hh] is None else dv[hh] + dv_c
                dk[hh] = dk_c if dk[hh] is None else dk[hh] + dk_c
            for j in range(nb):
                dq0, dq1 = [_dot(ds16[2 * j + hh], k_ones[hh]) for hh in range(2)]
                dq_ref[rows(j), :] += jnp.where(first, dq0, dq1) * scale
                dfq_ref[rows(j), :] += jnp.where(first, dq0[:, HEAD_DIM:HEAD_DIM + 1], dq1[:, 0:1])
            dk_ref[...] += jnp.where(first, dk[0], dk[1])
            dfk_ref[...] += jnp.where(first, dk[0][:, HEAD_DIM:HEAD_DIM + 1], dk[1][:, 0:1])
            dv_ref[...] += dv[0] + dv[1]

        for nb in range(1, grp + 1):
            exists = (qa + grp <= nq) if nb == grp else (qa + nb == nq)
            for diagonal in (False, True):
                @pl.when(exists & ((qa == ki) == diagonal))
                def _(nb=nb, diagonal=diagonal):
                    sweep(nb, diagonal)

    def qblock(j):
        return lambda s, qt: jnp.minimum(qt[s] + j, nq - 1)

    qbs = [qblock(j) for j in range(grp)]
    qcol = [functools.partial(lambda p, s, qt, kt, qb: (qb(s, qt), p), qb=qb) for qb in qbs]
    krow = lambda p, s, qt, kt: (kt[s], p)
    return pl.pallas_call(
        body, name="attention_bwd",
        out_shape=[jax.ShapeDtypeStruct((t, ATTN_W), F32)] * 5,
        grid_spec=pltpu.PrefetchScalarGridSpec(
            num_scalar_prefetch=2, grid=(4, len(steps)),
            in_specs=[pl.BlockSpec((tq, 128), m) for m in qcol]
            + [pl.BlockSpec((tk, 128), lambda p, s, qt, kt: (kt[s], 4 + p)),
               pl.BlockSpec((tk, 128), lambda p, s, qt, kt: (kt[s], 8 + p))]
            + [pl.BlockSpec((None, tq, 128), functools.partial(lambda p, s, qt, kt, qb: (p, qb(s, qt), 0), qb=qb))
               for qb in qbs]
            + [pl.BlockSpec((None, tk, 128), lambda p, s, qt, kt: (p, kt[s], 0))]
            + [pl.BlockSpec((tq, 128), m) for m in qcol] + [pl.BlockSpec((tq, 128), m) for m in qcol],
            out_specs=[pl.BlockSpec((t, 128), lambda p, s, qt, kt: (0, p)),
                       pl.BlockSpec((tk, 128), krow), pl.BlockSpec((tk, 128), krow), pl.BlockSpec((tk, 128), krow),
                       pl.BlockSpec((t, 128), lambda p, s, qt, kt: (0, p))]),
        compiler_params=_params(("parallel", "arbitrary"), VMEM_BIG),
    )(q_tab, k_tab, *([z] * grp), z, z, *([aug_q] * grp), aug_k, *([o] * grp), *([do] * grp))


def _shifted(prev_rows, x, shift):
    tm = x.shape[0]
    return pltpu.roll(jnp.concatenate([prev_rows, x], axis=0), shift, 0)[8:8 + tm]


def _ahead(x, next_rows, shift):
    tm = x.shape[0]
    return pltpu.roll(jnp.concatenate([x, next_rows], axis=0), tm + 8 - shift, 0)[0:tm]


def _conv_col0(z):
    return (z.shape[1] - F_PAD - 3 * CONV_W) // CONV_W


def _conv_specs(tm, c0):
    cols = (c0, c0 + 1, c0 + 2)
    tiles = [pl.BlockSpec((tm, CONV_W), functools.partial(lambda i, c: (i, c), c=c)) for c in cols]
    halos = [pl.BlockSpec((8, CONV_W), functools.partial(lambda i, c: (jnp.maximum(i * (tm // 8) - 1, 0), c), c=c))
             for c in cols]
    return tiles, halos


def _conv_gate(z, conv_w):
    t = z.shape[0]
    tm = ROW_TILE
    nt = t // tm

    def body(cb_ref, cc_ref, ci_ref, hc_ref, hi_ref, w_ref, g_ref, gt_ref):
        i = pl.program_id(0)
        cc = cc_ref[...] * ci_ref[...]
        prev = jnp.where(i > 0, hc_ref[...] * hi_ref[...], 0.0)
        conv = w_ref[0:1, :] * _shifted(prev, cc, 2) + w_ref[1:2, :] * _shifted(prev, cc, 1) + w_ref[2:3, :] * cc
        g = cb_ref[...] * conv
        g_ref[...] = g.astype(BF16)
        gt_ref[...] = g.T.astype(BF16)

    (cb, cc, ci), (_, hc, hi) = _conv_specs(tm, _conv_col0(z))
    return pl.pallas_call(
        body, name="conv_gate_fwd",
        out_shape=[jax.ShapeDtypeStruct((t, CONV_W), BF16), jax.ShapeDtypeStruct((CONV_W, t), BF16)],
        grid=(nt,),
        in_specs=[cb, cc, ci, hc, hi, pl.BlockSpec((8, CONV_W), lambda i: (0, 0))],
        out_specs=[pl.BlockSpec((tm, CONV_W), lambda i: (i, 0)), pl.BlockSpec((CONV_W, tm), lambda i: (0, i))],
        compiler_params=_params(("parallel",)),
    )(z, z, z, z, z, conv_w)


def _conv_bwd(z, dg, conv_w):
    t = z.shape[0]
    tm = ROW_TILE
    nt = t // tm

    def body(cb_ref, cc_ref, ci_ref, hc_ref, hi_ref, dg_ref, ncb_ref, ndg_ref, w_ref, dz_ref, dw_ref):
        i = pl.program_id(0)

        @pl.when(i == 0)
        def _():
            dw_ref[...] = jnp.zeros_like(dw_ref)

        cb, c_c, c_in = cb_ref[...], cc_ref[...], ci_ref[...]
        cc = c_c * c_in
        prev = jnp.where(i > 0, hc_ref[...] * hi_ref[...], 0.0)
        cc1, cc2 = _shifted(prev, cc, 1), _shifted(prev, cc, 2)
        w0, w1, w2 = w_ref[0:1, :], w_ref[1:2, :], w_ref[2:3, :]
        conv = w0 * cc2 + w1 * cc1 + w2 * cc
        dgv = dg_ref[...]
        dconv = dgv * cb
        nxt = jnp.where(i < nt - 1, ndg_ref[...] * ncb_ref[...], 0.0)
        dcc = w2 * dconv + w1 * _ahead(dconv, nxt, 1) + w0 * _ahead(dconv, nxt, 2)
        dz_ref[:, 0:CONV_W] = (dgv * conv).astype(BF16)
        dz_ref[:, CONV_W:2 * CONV_W] = (dcc * c_in).astype(BF16)
        dz_ref[:, 2 * CONV_W:] = (dcc * c_c).astype(BF16)
        dw_ref[0:1, :] += jnp.sum(dconv * cc2, axis=0, keepdims=True)
        dw_ref[1:2, :] += jnp.sum(dconv * cc1, axis=0, keepdims=True)
        dw_ref[2:3, :] += jnp.sum(dconv * cc, axis=0, keepdims=True)

    c0 = _conv_col0(z)
    (cb, cc, ci), (_, hc, hi) = _conv_specs(tm, c0)
    nxt = lambda i, c: (jnp.minimum((i + 1) * (tm // 8), t // 8 - 1), c)
    return pl.pallas_call(
        body, name="conv_gate_bwd",
        out_shape=[jax.ShapeDtypeStruct((t, 3 * CONV_W), BF16), jax.ShapeDtypeStruct((8, CONV_W), F32)],
        grid=(nt,),
        in_specs=[cb, cc, ci, hc, hi, pl.BlockSpec((tm, CONV_W), lambda i: (i, 0)),
                  pl.BlockSpec((8, CONV_W), lambda i: nxt(i, c0)), pl.BlockSpec((8, CONV_W), lambda i: nxt(i, 0)),
                  pl.BlockSpec((8, CONV_W), lambda i: (0, 0))],
        out_specs=[pl.BlockSpec((tm, 3 * CONV_W), lambda i: (i, 0)), pl.BlockSpec((8, CONV_W), lambda i: (0, 0))],
        compiler_params=_params(("arbitrary",)),
    )(z, z, z, z, z, dg, z, dg, conv_w)


def _branch_mix(z, o, g, w_ab, w_cb, d):
    t = z.shape[0]
    tm = ROW_TILE
    ga_col = 0

    def body(o_ref, g_ref, ga_ref, gc_ref, wa_ref, wc_ref, mp_ref, mpt_ref, ot_ref):
        o_ = o_ref[...]
        ya = _dot(o_.astype(BF16), wa_ref[...])
        yc = _dot(g_ref[...], wc_ref[...])
        mp = _sigmoid(ga_ref[...]) * ya + _sigmoid(gc_ref[...]) * yc
        mp_ref[...] = mp.astype(BF16)
        mpt_ref[...] = mp.T.astype(BF16)
        ot_ref[...] = o_.T.astype(BF16)

    return pl.pallas_call(
        body, name="branch_mix_fwd",
        out_shape=[jax.ShapeDtypeStruct((t, d), BF16), jax.ShapeDtypeStruct((d, t), BF16),
                   jax.ShapeDtypeStruct((ATTN_W, t), BF16)],
        grid=(t // tm,),
        in_specs=[pl.BlockSpec((tm, ATTN_W), lambda i: (i, 0)), pl.BlockSpec((tm, CONV_W), lambda i: (i, 0)),
                  pl.BlockSpec((tm, d), lambda i: (i, ga_col)), pl.BlockSpec((tm, d), lambda i: (i, ga_col + 1)),
                  pl.BlockSpec((ATTN_W, d), lambda i: (0, 0)), pl.BlockSpec((CONV_W, d), lambda i: (0, 0))],
        out_specs=[pl.BlockSpec((tm, d), lambda i: (i, 0)), pl.BlockSpec((d, tm), lambda i: (0, i)),
                   pl.BlockSpec((ATTN_W, tm), lambda i: (0, i))],
        compiler_params=_params(("parallel",), VMEM_BIG),
    )(o, g, z, z, w_ab, w_cb)


def _branch_bwd(z, o, g, dmixed, w_out, w_ab, w_cb, d):
    t = z.shape[0]
    tm = ROW_TILE // 2
    ga_col = 0

    def body(dm_ref, o_ref, g_ref, ga_ref, gc_ref, wo_ref, wa_ref, wc_ref, dya_ref, dyc_ref, dgt_ref, do_ref, dg_ref):
        dmp = _dot_nt(dm_ref[...], wo_ref[...])
        ya = _dot(o_ref[...].astype(BF16), wa_ref[...])
        yc = _dot(g_ref[...], wc_ref[...])
        sa, sc = _sigmoid(ga_ref[...]), _sigmoid(gc_ref[...])
        dya = (dmp * sa).astype(BF16)
        dyc = (dmp * sc).astype(BF16)
        dya_ref[...] = dya
        dyc_ref[...] = dyc
        dgt_ref[:, :d] = (dmp * ya * sa * (1.0 - sa)).astype(BF16)
        dgt_ref[:, d:] = (dmp * yc * sc * (1.0 - sc)).astype(BF16)
        do_ref[...] = _dot_nt(dya, wa_ref[...])
        dg_ref[...] = _dot_nt(dyc, wc_ref[...])

    row = lambda i: (i, 0)
    fixed = lambda i: (0, 0)
    return pl.pallas_call(
        body, name="branch_mix_bwd",
        out_shape=[jax.ShapeDtypeStruct((t, d), BF16), jax.ShapeDtypeStruct((t, d), BF16),
                   jax.ShapeDtypeStruct((t, 2 * d), BF16), jax.ShapeDtypeStruct((t, ATTN_W), F32),
                   jax.ShapeDtypeStruct((t, CONV_W), F32)],
        grid=(t // tm,),
        in_specs=[pl.BlockSpec((tm, d), row), pl.BlockSpec((tm, ATTN_W), row), pl.BlockSpec((tm, CONV_W), row),
                  pl.BlockSpec((tm, d), lambda i: (i, ga_col)), pl.BlockSpec((tm, d), lambda i: (i, ga_col + 1)),
                  pl.BlockSpec((d, d), fixed), pl.BlockSpec((ATTN_W, d), fixed), pl.BlockSpec((CONV_W, d), fixed)],
        out_specs=[pl.BlockSpec((tm, d), row), pl.BlockSpec((tm, d), row), pl.BlockSpec((tm, 2 * d), row),
                   pl.BlockSpec((tm, ATTN_W), row), pl.BlockSpec((tm, CONV_W), row)],
        compiler_params=_params(("parallel",), VMEM_BIG),
    )(dmixed, o, g, z, z, w_out, w_ab, w_cb)


def _loss_norm_bwd(h, target, f, g_post, alpha):
    t, d = h.shape
    tm = ROW_TILE

    def body(h_ref, t_hbm, f_ref, g_ref, dh_ref, df_ref, dg_ref, loss_ref, t_buf, sems):
        i = pl.program_id(0)

        @pl.when(i == 0)
        def _():
            loss_ref[...] = jnp.zeros_like(loss_ref)
            dg_ref[...] = jnp.zeros_like(dg_ref)

        target = _read_token_rows(t_hbm, t_buf, sems, i, t // tm)
        row = i * tm + lax.broadcasted_iota(jnp.int32, (tm, 1), 0)
        err = jnp.where(row >= N_FRONT, h_ref[...] - target[...], 0.0)
        dy = err * (1.0 / d)
        dh_ref[...] = dy
        per_row = jnp.sum(err * err, axis=1, keepdims=True) * (1.0 / d)
        loss_ref[...] += 0.5 * jnp.sum(per_row, axis=0, keepdims=True)
        dx, dg = _rms_bwd(f_ref[...], g_ref[...], dy)
        df_ref[...] = (alpha * dx).astype(BF16)
        dg_ref[...] += alpha * dg

    row = pl.BlockSpec((tm, d), lambda i: (i, 0))
    vec = pl.BlockSpec((1, d), lambda i: (0, 0))
    return pl.pallas_call(
        body, name="loss_and_post_norm_bwd",
        out_shape=[jax.ShapeDtypeStruct((t, d), F32), jax.ShapeDtypeStruct((t, d), BF16),
                   jax.ShapeDtypeStruct((1, d), F32), jax.ShapeDtypeStruct((1, 128), F32)],
        grid=(t // tm,),
        in_specs=[row, ANY, row, vec],
        out_specs=[row, row, vec, pl.BlockSpec((1, 128), lambda i: (0, 0))],
        scratch_shapes=[pltpu.VMEM((2, tm, d), F32), pltpu.SemaphoreType.DMA((2,))],
        compiler_params=_params(("arbitrary",)),
    )(h, target, f, g_post)


def _norm_bwd(name, x, g, dy, alpha):
    t, d = x.shape
    tm = ROW_TILE

    def body(x_ref, g_ref, dy_ref, dx_ref, dg_ref):
        @pl.when(pl.program_id(0) == 0)
        def _():
            dg_ref[...] = jnp.zeros_like(dg_ref)

        dx, dg = _rms_bwd(x_ref[...], g_ref[...], dy_ref[...])
        dx_ref[...] = (alpha * dx).astype(BF16)
        dg_ref[...] += alpha * dg

    row = pl.BlockSpec((tm, d), lambda i: (i, 0))
    vec = pl.BlockSpec((1, d), lambda i: (0, 0))
    return pl.pallas_call(
        body, name=name,
        out_shape=[jax.ShapeDtypeStruct((t, d), BF16), jax.ShapeDtypeStruct((1, d), F32)],
        grid=(t // tm,), in_specs=[row, vec, row], out_specs=[row, vec],
        compiler_params=_params(("arbitrary",)),
    )(x, g, dy)


def _ffn_bwd_mid(name, df, w_out, ab):
    t, d = df.shape
    cw = ab.shape[1] // 4
    tm = ROW_TILE

    def body(df_ref, w_ref, ab_ref, o_ref):
        ds = _dot_nt(df_ref[...], w_ref[...])
        a = ab_ref[:, :cw].astype(F32)
        b = ab_ref[:, cw:].astype(F32)
        sg = _sigmoid(a)
        o_ref[:, :cw] = (ds * b * (sg * (1.0 + a * (1.0 - sg)))).astype(BF16)
        o_ref[:, cw:] = (ds * (a * sg)).astype(BF16)

    return pl.pallas_call(
        body, name=name, out_shape=jax.ShapeDtypeStruct((t, 4 * cw), BF16),
        grid=(2, t // tm),
        in_specs=[pl.BlockSpec((tm, d), lambda j, i: (i, 0)), pl.BlockSpec((cw, d), lambda j, i: (j, 0)),
                  pl.BlockSpec((tm, 2 * cw), lambda j, i: (i, j))],
        out_specs=pl.BlockSpec((tm, 2 * cw), lambda j, i: (i, j)),
        compiler_params=_params(("parallel", "parallel"), VMEM_BIG),
    )(df, w_out, ab)


def _mm_nt_norm_bwd(name, dy, w, h, g, dh_in):
    t, kdim = dy.shape
    d = h.shape[1]
    tm = ROW_TILE // 2
    slots = w.ndim == 3

    def body(dy_ref, w_ref, h_ref, g_ref, dhi_ref, dh_ref, dg_ref):
        @pl.when(pl.program_id(0) == 0)
        def _():
            dg_ref[...] = jnp.zeros_like(dg_ref)

        if slots:
            cw = w_ref.shape[2]
            dn = _dot_nt(dy_ref[:, 0:cw], w_ref[_slot_of(0)])
            for k in range(1, 4):
                dn += _dot_nt(dy_ref[:, k * cw:(k + 1) * cw], w_ref[_slot_of(k)])
        else:
            dn = _dot_nt(dy_ref[...], w_ref[...])
        dx, dg = _rms_bwd(h_ref[...], g_ref[...], dn)
        dh_ref[...] = dhi_ref[...] + dx
        dg_ref[...] += dg

    row = pl.BlockSpec((tm, d), lambda i: (i, 0))
    vec = pl.BlockSpec((1, d), lambda i: (0, 0))
    return pl.pallas_call(
        body, name=name,
        out_shape=[jax.ShapeDtypeStruct((t, d), F32), jax.ShapeDtypeStruct((1, d), F32)],
        grid=(t // tm,),
        in_specs=[pl.BlockSpec((tm, kdim), lambda i: (i, 0)), pl.BlockSpec(w.shape, lambda i: (0,) * w.ndim),
                  row, vec, row],
        out_specs=[row, vec],
        compiler_params=_params(("arbitrary",), VMEM_BIG),
    )(dy, w, h, g, dh_in)


def _gate_bwd(dfq, dfk, z, b_pad, f_col):
    t = z.shape[0]
    tm = ROW_TILE
    nt = t // tm

    def body(dq_ref, dk_ref, z_ref, b_ref, dz_ref, db_ref, carry_ref):
        i = pl.program_id(0)

        @pl.when(i == 0)
        def _():
            carry_ref[...] = jnp.zeros_like(carry_ref)
            db_ref[...] = jnp.zeros_like(db_ref)

        pick = (lax.broadcasted_iota(jnp.int32, (ATTN_W, 128), 0)
                == HEAD_DIM * lax.broadcasted_iota(jnp.int32, (ATTN_W, 128), 1)).astype(F32)
        d_heads = jnp.dot(dq_ref[...] - dk_ref[...], pick, preferred_element_type=F32,
                          precision=lax.Precision.HIGHEST)
        tri = (lax.broadcasted_iota(jnp.int32, (tm, tm), 0) <= lax.broadcasted_iota(jnp.int32, (tm, tm), 1))
        tail = jnp.dot(tri.astype(F32), d_heads, preferred_element_type=F32, precision=lax.Precision.HIGHEST)
        tail = tail + carry_ref[0:1, :]
        carry_ref[...] = jnp.broadcast_to(tail[0:1, :], carry_ref.shape)
        row = (nt - 1 - i) * tm + lax.broadcasted_iota(jnp.int32, (tm, 1), 0)
        dlogit = jnp.where(row >= ROW_PAD, tail * _sigmoid(-(z_ref[...] + b_ref[...])), 0.0)
        dz_ref[...] = jnp.zeros_like(dz_ref)
        dz_ref[:, 0:128] = dlogit.astype(BF16)
        db_ref[...] += jnp.sum(dlogit, axis=0, keepdims=True)

    rev = lambda i: (nt - 1 - i, 0)
    return pl.pallas_call(
        body, name="forget_gate_bwd",
        out_shape=[jax.ShapeDtypeStruct((t, F_PAD), BF16), jax.ShapeDtypeStruct((1, 128), F32)],
        grid=(nt,),
        in_specs=[pl.BlockSpec((tm, ATTN_W), rev), pl.BlockSpec((tm, ATTN_W), rev),
                  pl.BlockSpec((tm, 128), lambda i: (nt - 1 - i, f_col // 128)),
                  pl.BlockSpec((1, 128), lambda i: (0, 0))],
        out_specs=[pl.BlockSpec((tm, F_PAD), rev), pl.BlockSpec((1, 128), lambda i: (0, 0))],
        scratch_shapes=[pltpu.VMEM((8, 128), F32)],
        compiler_params=_params(("arbitrary",)),
    )(dfq, dfk, z, b_pad)


def _ffn_fwd(tag, n, w_in4, w_out, h, g_post, g_next):
    ab, s, s_t = _ffn_in(f"{tag}_in_fwd", n, w_in4)
    outs = _mm_resid_norm(f"{tag}_out_fwd", s, w_out, h, g_post, 0.5, g_next)
    return ab, s_t, outs


def _ffn_bwd_weights(tag, df, ab, s_t, n_t, w_in4, w_out):
    d, cw = w_in4.shape[1], w_in4.shape[2]
    t = df.shape[0]
    dw_out = _weight_grad(f"{tag}_dw_out", s_t, df, d, out_rows=cw // 2)
    dab = _ffn_bwd_mid(f"{tag}_mid_bwd", df, w_out, ab)
    bk = _k_tile(t)
    dw_in = _matmul(
        f"{tag}_dw_in", n_t, dab, jax.ShapeDtypeStruct((4, d, cw), F32), (1, 4, t // bk),
        pl.BlockSpec((d, bk), lambda a, b, k: (0, k)), pl.BlockSpec((bk, cw), lambda a, b, k: (k, b)),
        pl.BlockSpec((None, d, cw), lambda a, b, k: (_slot_of(b), 0, 0)), vmem=VMEM_BIG)
    return dab, dw_in, dw_out


LOSS_ROW = 12


def _pack_small(meta, conv, gains, b_forget, loss=None):
    d = gains[0].shape[1]
    rows = [meta.reshape(4, d), jnp.pad(conv.reshape(1, 3 * 128), ((0, 0), (0, d - 3 * 128)))]
    rows += list(gains) + [jnp.pad(b_forget, ((0, 0), (0, d - HEADS)))]
    last = jnp.zeros((4, d), F32)
    if loss is not None:
        last = jnp.pad(loss.reshape(1, 1), ((0, 3), (0, d - 1)))
    return jnp.concatenate(rows + [last], axis=0)


def _unpack_small(block):
    d = block.shape[1]
    meta = block[0:4].reshape(N_META, d // 4)
    conv = block[4, :3 * 128].reshape(1, 3, 128)
    gains = [block[5 + i:6 + i] for i in range(6)]
    return meta, conv, gains, block[11:12, :HEADS]


def kernel(x, meta_tokens, w_in, b_forget, conv_w, w_attn_branch, w_conv_branch, w_out, g_ffn1_pre, g_ffn1_post, w_ffn1_in, w_ffn1_out, g_mix_pre, g_mix_post, g_ffn2_pre, g_ffn2_post, w_ffn2_in, w_ffn2_out, loss_target, m_meta_tokens, m_w_in, m_b_forget, m_conv_w, m_w_attn_branch, m_w_conv_branch, m_w_out, m_g_ffn1_pre, m_g_ffn1_post, m_w_ffn1_in, m_w_ffn1_out, m_g_mix_pre, m_g_mix_post, m_g_ffn2_pre, m_g_ffn2_post, m_w_ffn2_in, m_w_ffn2_out, v_meta_tokens, v_w_in, v_b_forget, v_conv_w, v_w_attn_branch, v_w_conv_branch, v_w_out, v_g_ffn1_pre, v_g_ffn1_post, v_w_ffn1_in, v_w_ffn1_out, v_g_mix_pre, v_g_mix_post, v_g_ffn2_pre, v_g_ffn2_post, v_w_ffn2_in, v_w_ffn2_out):
    seq, d = x.shape[1], x.shape[2]
    t = seq + N_FRONT
    f_lo = 3 * ATTN_W
    c_arr = lax.axis_index("c").astype(jnp.int32).reshape(1)

    cs = w_in.shape[2]
    cs_pad = -(-cs // 64) * 64

    def w_in_rows(a):
        return jnp.pad(jnp.transpose(a[0]), ((0, cs_pad - cs), (0, 0)))

    big = [w_in_rows(w_in), w_attn_branch[0], w_conv_branch[0], w_out[0], w_ffn1_in[0], w_ffn1_out[0], w_ffn2_in[0],
           w_ffn2_out[0]]
    small_gather = jnp.concatenate(
        [meta_tokens.reshape(4, d), jnp.pad(conv_w.reshape(1, 3 * 128), ((0, 0), (0, d - 3 * 128))),
         jnp.zeros((11, d), F32)], axis=0)
    w_f1_in4, small4 = _all_gather([big[4].astype(BF16), small_gather])
    (second, rest), small4 = lax.optimization_barrier(
        (([big[5].astype(BF16)], [big[i].astype(BF16) for i in (0, 1, 2, 3, 6, 7)]), small4))
    second_gathered = _all_gather_async("all_gather_ffn1_out", second, 5)
    rest_gathered = _all_gather_async("all_gather_rest", rest, 1)
    meta_full = jnp.transpose(small4[:, 0:4].reshape(4, N_META, d // 4), (1, 0, 2)).reshape(N_META, d)
    conv_full = jnp.transpose(small4[:, 4, :3 * 128].reshape(4, 3, 128), (1, 0, 2)).reshape(3, CONV_W)
    conv_pad = jnp.pad(conv_full, ((0, 5), (0, 0)))
    b_pad = jnp.pad(b_forget, ((0, 0), (0, 128 - HEADS)))

    h0, n1, n1_t = _embed_norm(x[0], meta_full, g_ffn1_pre)
    ab1, s1, s1_t = _ffn_in("ffn1_in_fwd", n1, w_f1_in4)
    w_f1_out = second_gathered(s1, [0])[0].reshape(-1, d)
    f1, h1, u, u_t = _mm_resid_norm("ffn1_out_fwd", s1, w_f1_out, h0, g_ffn1_post, 0.5, g_mix_pre)

    w_in4, w_ab4, w_cb4, w_out4, w_f2_in4, w_f2_out4 = rest_gathered(u, range(6))
    w_in_t = w_in4[:, :cs].reshape(4 * cs, d)
    g_lo = f_lo + HEADS + 3 * CONV_W
    w_in_pad = jnp.concatenate(
        [w_in_t[:f_lo], w_in_t[g_lo:], w_in_t[f_lo + HEADS:g_lo], w_in_t[f_lo:f_lo + HEADS],
         jnp.zeros((F_PAD - HEADS, d), BF16)], axis=0)
    w_ab = jnp.transpose(w_ab4, (1, 0, 2)).reshape(ATTN_W, d)
    w_cb = jnp.transpose(w_cb4, (1, 0, 2)).reshape(CONV_W, d)
    w_out_full = w_out4.reshape(d, d)
    w_f2_out = w_f2_out4.reshape(-1, d)
    qkv, z = _in_proj(u, w_in_pad)
    f_col = z.shape[1] - F_PAD
    f_cum = _gate_prep(z, b_pad, f_col)
    f_heads = f_cum[:, :HEADS]
    o, lse = _attn_fwd(qkv, *_attn_bias_operands(f_heads))
    g, g_t = _conv_gate(z, conv_pad)
    mp, mp_t, o_t = _branch_mix(z, o, g, w_ab, w_cb, d)
    mixed, h2, n2, n2_t = _mm_resid_norm("mix_out_fwd", mp, w_out_full, h1, g_mix_post, 1.0, g_ffn2_pre)
    ab2, s2_t, (f2, h3) = _ffn_fwd("ffn2", n2, w_f2_in4, w_f2_out, h2, g_ffn2_post, None)
    dh3, df2, dg_f2_post, loss_part = _loss_norm_bwd(h3, loss_target[0], f2, g_ffn2_post, 0.5)

    reduced = {}

    def reduce_scatter(label, tags, slots, sequencer_id, hold=None, got=None, after=None):
        if got is None:
            got = _pair_send_halves(f"grad_pair_exchange_{label}", slots)
        else:
            got, _ = lax.optimization_barrier((got, after))
        sums = [_pair_add(tag, s, a, c_arr, F32 if tag == "small" else BF16) for tag, s, a in zip(tags, slots, got)]
        sums, hold = lax.optimization_barrier((sums, hold))
        if sequencer_id is None:
            arrived = _chip_scatter(f"grad_chip_scatter_{label}", sums)
        else:
            arrived = _chip_scatter_async(f"grad_chip_scatter_{label}", sums, sequencer_id)
        mine = [_chip_add(tag, a) for tag, a in zip(tags, arrived)]
        reduced.update(zip(tags, zip(mine, _pair_swap(f"grad_pair_swap_{label}", mine))))
        return hold

    dab2, dw_f2_in, dw_f2_out = _ffn_bwd_weights("ffn2", df2, ab2, s2_t, n2_t, w_f2_in4, w_f2_out)
    ffn2_slots = [dw_f2_in, dw_f2_out.reshape(4, -1, d)]
    ffn2_got = _pair_send_halves_async("grad_pair_exchange_ffn2", ffn2_slots, 6)
    dh2, dg_f2_pre = _mm_nt_norm_bwd("ffn2_in_bwd", dab2, w_f2_in4, h2, g_ffn2_pre, dh3)
    reduce_scatter("ffn2", ["w_ffn2_in", "w_ffn2_out"], ffn2_slots, 2, got=ffn2_got, after=dh2)
    dmixed, dg_mix_post = _norm_bwd("mix_post_norm_bwd", mixed, g_mix_post, dh2, 1.0)
    dw_out = _weight_grad("mix_dw_out", mp_t, dmixed, d)
    dya, dyc, dgates, do, dgconv = _branch_bwd(z, o, g, dmixed, w_out_full, w_ab, w_cb, d)
    dw_ab = _weight_grad("mix_dw_attn_branch", o_t, dya, d)
    dw_cb = _weight_grad("mix_dw_conv_branch", g_t, dyc, d)
    dz_conv, dconv_w = _conv_bwd(z, dgconv, conv_pad)
    front = lax.broadcasted_iota(jnp.int32, (t, 1), 0) < ROW_PAD
    lse_heads = jnp.where(front, 1e9, lse[:, ::HEAD_DIM])
    dq, dk, dv, dfk, dfq = _attn_bwd(qkv, *_attn_bias_operands(f_heads, lse_heads), o, do)
    dz_f, db_forget = _gate_bwd(dfq, dfk, z, b_pad, f_col)
    dz_pieces = {"q": dq, "k": dk, "v": dv, "gates": dgates, "conv": dz_conv, "f": dz_f}
    dh1, dg_mix_pre = _mix_in_bwd(list(dz_pieces.values()), w_in_pad, h1, g_mix_pre, dh2)
    dw_t = {name: _weight_grad_t(f"mix_dw_in_{name}", u_t, piece) for name, piece in dz_pieces.items()}
    dw_in_t = jnp.concatenate(
        [dw_t["q"], dw_t["k"], dw_t["v"], dw_t["f"][:HEADS], dw_t["conv"], dw_t["gates"]], axis=0)
    mix_slots = [jnp.pad(dw_in_t.reshape(4, cs, d), ((0, 0), (0, cs_pad - cs), (0, 0))),
                 jnp.transpose(dw_ab.reshape(ATTN_W, 4, d // 4), (1, 0, 2)),
                 jnp.transpose(dw_cb.reshape(CONV_W, 4, d // 4), (1, 0, 2)),
                 dw_out.reshape(4, d // 4, d)]
    mix_got = _pair_send_halves_async("grad_pair_exchange_mix", mix_slots, 7)
    df1, dg_f1_post = _norm_bwd("ffn1_post_norm_bwd", f1, g_ffn1_post, dh1, 0.5)
    reduce_scatter("mix", ["w_in", "w_attn_branch", "w_conv_branch", "w_out"], mix_slots, 3, got=mix_got, after=df1)
    dab1, dw_f1_in, dw_f1_out = _ffn_bwd_weights("ffn1", df1, ab1, s1_t, n1_t, w_f1_in4, w_f1_out)
    dab1 = reduce_scatter("ffn1", ["w_ffn1_in", "w_ffn1_out"], [dw_f1_in, dw_f1_out.reshape(4, -1, d)], 4, dab1)
    dh0, dg_f1_pre = _mm_nt_norm_bwd("ffn1_in_bwd", dab1, w_f1_in4, h0, g_ffn1_pre, dh1)
    grad_x = dh0[N_FRONT:][None]
    dmeta = dh0[ROW_PAD:N_FRONT]
    small_grad = jnp.stack([
        _pack_small(dmeta[:, j * (d // 4):(j + 1) * (d // 4)], dconv_w[:3, j * 128:(j + 1) * 128],
                    [dg_f1_pre, dg_f1_post, dg_mix_pre, dg_mix_post, dg_f2_pre, dg_f2_post], db_forget[:, :HEADS],
                    loss_part[0, 0])
        for j in range(4)])
    reduce_scatter("small", ["small"], [small_grad], None)
    tags =["w_in", "w_attn_branch", "w_conv_branch", "w_out", "w_ffn1_in", "w_ffn1_out", "w_ffn2_in", "w_ffn2_out", "small"]
    halves = [reduced[tag][0] for tag in tags]
    others = [reduced[tag][1] for tag in tags]

    small = [g_ffn1_pre, g_ffn1_post, g_mix_pre, g_mix_post, g_ffn2_pre, g_ffn2_post]
    small_m = [m_g_ffn1_pre, m_g_ffn1_post, m_g_mix_pre, m_g_mix_post, m_g_ffn2_pre, m_g_ffn2_post]
    small_v = [v_g_ffn1_pre, v_g_ffn1_post, v_g_mix_pre, v_g_mix_post, v_g_ffn2_pre, v_g_ffn2_post]
    ws = big + [_pack_small(meta_tokens, conv_w[0], small, b_forget)]
    ms = [w_in_rows(m_w_in), m_w_attn_branch[0], m_w_conv_branch[0], m_w_out[0], m_w_ffn1_in[0], m_w_ffn1_out[0],
          m_w_ffn2_in[0], m_w_ffn2_out[0], _pack_small(m_meta_tokens, m_conv_w[0], small_m, m_b_forget)]
    vs = [w_in_rows(v_w_in), v_w_attn_branch[0], v_w_conv_branch[0], v_w_out[0], v_w_ffn1_in[0], v_w_ffn1_out[0],
          v_w_ffn2_in[0], v_w_ffn2_out[0], _pack_small(v_meta_tokens, v_conv_w[0], small_v, v_b_forget)]
    updates = [_adamw(tag, w, a, b, m, v, c_arr) for tag, w, a, b, m, v in zip(tags, ws, halves, others, ms, vs)]

    def leaves(big_vals, small_block):
        meta, conv, gains, bf = _unpack_small(small_block)
        w_in_t_, w_ab_, w_cb_, w_out_, f1_in, f1_out, f2_in, f2_out = [b[None] for b in big_vals]
        w_in_ = jnp.transpose(w_in_t_[:, :cs], (0, 2, 1))
        return [meta, w_in_, bf, conv, w_ab_, w_cb_, w_out_, gains[0], gains[1], f1_in, f1_out,
                gains[2], gains[3], gains[4], gains[5], f2_in, f2_out]

    out_g, out_d, out_m, out_v = [leaves([u_[k] for u_ in updates[:8]], updates[8][k]) for k in range(4)]
    loss = updates[8][0][LOSS_ROW, 0]
    return (loss, grad_x, *out_g, *out_d, *out_m, *out_v)
```

```python
import functools

import jax
import jax.numpy as jnp
from jax import lax
from jax.experimental import pallas as pl
from jax.experimental.pallas import tpu as pltpu
from jax.experimental.pallas import tpu_sc as plsc

N_META = 16
ROW_PAD = 112
N_FRONT = ROW_PAD + N_META
HEADS = 8
HEAD_DIM = 64
ATTN_W = HEADS * HEAD_DIM
CONV_W = 512
NORM_EPS = 1e-6
ROW_TILE = 640
F_PAD = 128
ATTN_Q_GROUP = 2
ATTN_KV_GROUP = 4
NEG = -1e30
ADAM_LR = 0.001
ADAM_B1 = 0.9
ADAM_B2 = 0.999
ADAM_EPS = 1e-08
ADAM_WD = 0.01
ADAM_STEP = 10
VMEM_BIG = 56 * 1024 * 1024
MESH = pl.DeviceIdType.MESH
ANY = pl.BlockSpec(memory_space=pl.ANY)
F32 = jnp.float32
BF16 = jnp.bfloat16


def _params(sem, vmem=None):
    return pltpu.CompilerParams(dimension_semantics=sem, vmem_limit_bytes=vmem)


def _sigmoid(x):
    return 1.0 / (1.0 + jnp.exp(-x))


def _rstd(x):
    return lax.rsqrt(jnp.mean(x * x, axis=-1, keepdims=True) + NORM_EPS)


def _rms_bwd(x, g, dy):
    r = _rstd(x)
    xr = x * r
    gdy = g * dy
    dx = r * (gdy - xr * jnp.mean(xr * gdy, axis=-1, keepdims=True))
    return dx, jnp.sum(dy * xr, axis=0, keepdims=True)


def _dot(a, b):
    return jnp.dot(a, b, preferred_element_type=F32)


def _dot_nt(a, b):
    return lax.dot_general(a, b, (((1,), (1,)), ((), ())), preferred_element_type=F32)


def _k_tile(t):
    return 1664 if t % 1664 == 0 else ROW_TILE


def _place():
    x, y, c = lax.axis_index("x"), lax.axis_index("y"), lax.axis_index("c")
    chips = [(1 - x, y), (x, 1 - y), (1 - x, 1 - y)]
    return x, y, c, chips


def _all_gather(shards):
    n = len(shards)
    split = [s.reshape(2, s.shape[0] // 2, s.shape[1]) for s in shards]

    def body(*refs):
        ins, outs = refs[:n], refs[n:2 * n]
        send_sems, recv_sems = refs[2 * n:]
        x, y, c, chips = _place()
        me = 2 * x + y
        sibling = (x, y, 1 - c)

        def remote(i, k, slot, part, to, src=None):
            dst = outs[i].at[slot, part]
            return pltpu.make_async_remote_copy(
                src_ref=dst if src is None else src, dst_ref=dst,
                send_sem=send_sems.at[i, k], recv_sem=recv_sems.at[i, k],
                device_id=to, device_id_type=MESH)

        started = []
        for i in range(n):
            for k, (cx, cy) in enumerate(chips):
                cp = remote(i, k, me, c, (cx, cy, c), src=ins[i].at[c])
                cp.start()
                started.append(cp)
        for i in range(n):
            for k, (cx, cy) in enumerate(chips):
                remote(i, k, 2 * cx + cy, c, (x, y, c)).wait_recv()
                cp = remote(i, 3 + k, 2 * cx + cy, c, sibling)
                cp.start()
                started.append(cp)
        for i in range(n):
            for k, (cx, cy) in enumerate(chips):
                remote(i, 3 + k, 2 * cx + cy, 1 - c, (x, y, c)).wait_recv()
        for cp in started:
            cp.wait_send()

    outs = pl.pallas_call(
        body, name="all_gather_weights",
        out_shape=[jax.ShapeDtypeStruct((4,) + s.shape, s.dtype) for s in split],
        in_specs=[ANY] * n, out_specs=[ANY] * n,
        scratch_shapes=[pltpu.SemaphoreType.DMA((n, 6)), pltpu.SemaphoreType.DMA((n, 6))],
    )(*split)
    me =2 * lax.axis_index("x") + lax.axis_index("y")
    outs = [lax.dynamic_update_slice(o, s[None], (me, 0, 0, 0)) for o, s in zip(outs, split)]
    return [o.reshape((4,) + s.shape) for o, s in zip(outs, shards)]


def _all_gather_async(name, shards, collective_id):
    n = len(shards)
    split = [s.reshape(2, s.shape[0] // 2, s.shape[1]) for s in shards]
    ins = [jax.new_ref(s, memory_space=pltpu.MemorySpace.HBM) for s in split]
    outs = [jax.empty_ref(jax.ShapeDtypeStruct((4,) + s.shape, s.dtype), memory_space=pltpu.MemorySpace.HBM)
            for s in split]

    @pl.kernel(mesh=plsc.ScalarSubcoreMesh(axis_name="sequencer", num_cores=1), name=name,
               scratch_types=(pltpu.SemaphoreType.DMA((n, 6)), pltpu.SemaphoreType.DMA((n, 6))),
               compiler_params=pltpu.CompilerParams(collective_id=collective_id))
    def launch(send_sems, recv_sems):
        x, y, c, chips = _place()
        me = 2 * x + y
        sibling = (x, y, 1 - c)
        barrier = pltpu.get_barrier_semaphore()
        for peer in [(cx, cy, c) for cx, cy in chips] + [sibling]:
            pl.semaphore_signal(barrier, inc=1, device_id=peer, device_id_type=MESH)
        pl.semaphore_wait(barrier, 4)

        def remote(i, k, slot, part, to, src=None):
            dst = outs[i].at[slot, part]
            return pltpu.make_async_remote_copy(
                src_ref=dst if src is None else src, dst_ref=dst,
                send_sem=send_sems.at[i, k], recv_sem=recv_sems.at[i, k],
                device_id=to, device_id_type=MESH)

        started = []
        for i in range(n):
            for k, (cx, cy) in enumerate(chips):
                cp = remote(i, k, me, c, (cx, cy, c), src=ins[i].at[c])
                cp.start()
                started.append(cp)
        for i in range(n):
            for k, (cx, cy) in enumerate(chips):
                remote(i, k, 2 * cx + cy, c, (x, y, c)).wait_recv()
                cp = remote(i, 3 + k, 2 * cx + cy, c, sibling)
                cp.start()
                started.append(cp)
        for i in range(n):
            for k, (cx, cy) in enumerate(chips):
                remote(i, 3 + k, 2 * cx + cy, 1 - c, (x, y, c)).wait_recv()
        for cp in started:
            cp.wait_send()

    launch()
    raw = [o[...] for o in outs]

    def finish(after, which):
        arrived, _ = lax.optimization_barrier(([raw[i] for i in which], after))
        me = 2 * lax.axis_index("x") + lax.axis_index("y")
        gathered = [lax.dynamic_update_slice(a, split[i][None], (me, 0, 0, 0)) for a, i in zip(arrived, which)]
        return [g.reshape((4,) + shards[i].shape) for g, i in zip(gathered, which)]

    return finish


def _pair_send_halves(name, grads):
    n = len(grads)

    def body(*refs):
        ins, outs = refs[:n], refs[n:2 * n]
        send_sems, recv_sems = refs[2 * n:]
        x, y, c, _ = _place()
        cps = []
        for i in range(n):
            half = ins[i].shape[1] // 2
            cp = pltpu.make_async_remote_copy(
                src_ref=ins[i].at[:, pl.ds((1 - c) * half, half)], dst_ref=outs[i],
                send_sem=send_sems.at[i], recv_sem=recv_sems.at[i],
                device_id=(x, y, 1 - c), device_id_type=MESH)
            cp.start()
            cps.append(cp)
        for cp in cps:
            cp.wait()

    return pl.pallas_call(
        body, name=name,
        out_shape=[jax.ShapeDtypeStruct((4, g.shape[1] // 2, g.shape[2]), g.dtype) for g in grads],
        in_specs=[ANY] * n, out_specs=[ANY] * n,
        scratch_shapes=[pltpu.SemaphoreType.DMA((n,)), pltpu.SemaphoreType.DMA((n,))],
    )(*grads)


def _pair_send_halves_async(name, grads, collective_id):
    n = len(grads)
    ins = [jax.new_ref(g, memory_space=pltpu.MemorySpace.HBM) for g in grads]
    outs = [jax.empty_ref(jax.ShapeDtypeStruct((4, g.shape[1] // 2, g.shape[2]), g.dtype),
                          memory_space=pltpu.MemorySpace.HBM) for g in grads]

    @pl.kernel(mesh=plsc.ScalarSubcoreMesh(axis_name="sequencer", num_cores=1), name=name,
               scratch_types=(pltpu.SemaphoreType.DMA((n,)), pltpu.SemaphoreType.DMA((n,))),
               compiler_params=pltpu.CompilerParams(collective_id=collective_id))
    def launch(send_sems, recv_sems):
        x, y, c, _ = _place()
        barrier = pltpu.get_barrier_semaphore()
        pl.semaphore_signal(barrier, inc=1, device_id=(x, y, 1 - c), device_id_type=MESH)
        pl.semaphore_wait(barrier, 1)
        cps = []
        for i in range(n):
            half = ins[i].shape[1] // 2
            cp = pltpu.make_async_remote_copy(
                src_ref=ins[i].at[:, pl.ds((1 - c) * half, half)], dst_ref=outs[i],
                send_sem=send_sems.at[i], recv_sem=recv_sems.at[i],
                device_id=(x, y, 1 - c), device_id_type=MESH)
            cp.start()
            cps.append(cp)
        for cp in cps:
            cp.wait()

    launch()
    return [o[...] for o in outs]


def _chip_scatter(name, parts):
    n = len(parts)

    def body(*refs):
        _scatter_copies(refs[:n], refs[n:2 * n], *refs[2 * n:])

    arrived = pl.pallas_call(
        body, name=name,
        out_shape=[jax.ShapeDtypeStruct(p.shape, p.dtype) for p in parts],
        in_specs=[ANY] * n, out_specs=[ANY] * n,
        scratch_shapes=[pltpu.SemaphoreType.DMA((n, 3)), pltpu.SemaphoreType.DMA((n, 3))],
    )(*parts)
    return _own_slots(parts, arrived)


def _scatter_copies(ins, outs, send_sems, recv_sems):
    x, y, c, chips = _place()
    me = 2 * x + y
    sends = []
    for i in range(len(ins)):
        for k, (cx, cy) in enumerate(chips):
            cp = pltpu.make_async_remote_copy(
                src_ref=ins[i].at[2 * cx + cy], dst_ref=outs[i].at[me],
                send_sem=send_sems.at[i, k], recv_sem=recv_sems.at[i, k],
                device_id=(cx, cy, c), device_id_type=MESH)
            cp.start()
            sends.append(cp)
    for i in range(len(ins)):
        for k, (cx, cy) in enumerate(chips):
            got = outs[i].at[2 * cx + cy]
            pltpu.make_async_remote_copy(
                src_ref=got, dst_ref=got, send_sem=send_sems.at[i, k], recv_sem=recv_sems.at[i, k],
                device_id=(x, y, c), device_id_type=MESH).wait_recv()
    for cp in sends:
        cp.wait_send()


def _own_slots(parts, arrived):
    me = 2 * lax.axis_index("x") + lax.axis_index("y")
    return [lax.dynamic_update_slice(a, lax.dynamic_slice_in_dim(p, me, 1, axis=0), (me, 0, 0))
            for p, a in zip(parts, arrived)]


def _chip_scatter_async(name, parts, collective_id):
    n = len(parts)
    ins = [jax.new_ref(p, memory_space=pltpu.MemorySpace.HBM) for p in parts]
    outs = [jax.empty_ref(jax.ShapeDtypeStruct(p.shape, p.dtype), memory_space=pltpu.MemorySpace.HBM) for p in parts]

    @pl.kernel(mesh=plsc.ScalarSubcoreMesh(axis_name="sequencer", num_cores=1), name=name,
               scratch_types=(pltpu.SemaphoreType.DMA((n, 3)), pltpu.SemaphoreType.DMA((n, 3))),
               compiler_params=pltpu.CompilerParams(collective_id=collective_id))
    def launch(send_sems, recv_sems):
        x, y, c, chips = _place()
        barrier = pltpu.get_barrier_semaphore()
        for cx, cy in chips:
            pl.semaphore_signal(barrier, inc=1, device_id=(cx, cy, c), device_id_type=MESH)
        pl.semaphore_wait(barrier, 3)
        _scatter_copies(ins, outs, send_sems, recv_sems)

    launch()
    return _own_slots(parts, [o[...] for o in outs])


def _pair_swap(name, halves):
    n = len(halves)

    def body(*refs):
        ins, outs = refs[:n], refs[n:2 * n]
        send_sems, recv_sems = refs[2 * n:]
        x, y, c, _ = _place()
        cps = []
        for i in range(n):
            cp = pltpu.make_async_remote_copy(
                src_ref=ins[i], dst_ref=outs[i], send_sem=send_sems.at[i], recv_sem=recv_sems.at[i],
                device_id=(x, y, 1 - c), device_id_type=MESH)
            cp.start()
            cps.append(cp)
        for cp in cps:
            cp.wait()

    return pl.pallas_call(
        body, name=name,
        out_shape=[jax.ShapeDtypeStruct(h.shape, h.dtype) for h in halves],
        in_specs=[ANY] * n, out_specs=[ANY] * n,
        scratch_shapes=[pltpu.SemaphoreType.DMA((n,)), pltpu.SemaphoreType.DMA((n,))],
    )(*halves)


def _row_block(rows, cols, n_bufs, budget=20 * 1024 * 1024):
    best = min(rows, 16)
    for b in range(16, rows + 1, 16):
        if rows % b == 0 and 2 * n_bufs * b * cols * 4 <= budget:
            best = b
    return best


def _pair_add(tag, grad, got, c_arr, out_dtype):
    _, rows, cols = grad.shape
    half = rows // 2
    bh = _row_block(half, cols, 3)
    nb = half // bh

    def body(c_ref, g_ref, a_ref, o_ref):
        o_ref[...] = (g_ref[...] + a_ref[...]).astype(out_dtype)

    return pl.pallas_call(
        body, name=f"pair_add_{tag}",
        out_shape=jax.ShapeDtypeStruct((4, half, cols), out_dtype),
        grid_spec=pltpu.PrefetchScalarGridSpec(
            num_scalar_prefetch=1, grid=(4, nb),
            in_specs=[pl.BlockSpec((None, bh, cols), lambda j, r, c: (j, c[0] * nb + r, 0)),
                      pl.BlockSpec((None, bh, cols), lambda j, r, c: (j, r, 0))],
            out_specs=pl.BlockSpec((None, bh, cols), lambda j, r, c: (j, r, 0))),
        compiler_params=_params(("parallel", "parallel")),
    )(c_arr, grad, got)


def _chip_add(tag, parts):
    _, half, cols = parts.shape
    bh = _row_block(half, cols, 5)

    def body(p_ref, o_ref):
        a, b, c, d = [p_ref[j].astype(F32) for j in range(4)]
        o_ref[...] = ((a + b) + c) + d

    return pl.pallas_call(
        body, name=f"chip_add_{tag}",
        out_shape=jax.ShapeDtypeStruct((half, cols), F32),
        grid=(half // bh,),
        in_specs=[pl.BlockSpec((4, bh, cols), lambda r: (0, r, 0))],
        out_specs=pl.BlockSpec((bh, cols), lambda r: (r, 0)),
        compiler_params=_params(("parallel",)),
    )(parts)


def _adamw(tag, w, mine, theirs, m, v, c_arr):
    rows, cols = w.shape
    half = rows // 2
    br = _row_block(half, cols, 9)
    nb = half // br

    def body(c_ref, w_ref, a_ref, b_ref, m_ref, v_ref, g_ref, d_ref, mo_ref, vo_ref):
        own = (pl.program_id(0) // nb) == c_ref[0]
        g = jnp.where(own, a_ref[...], b_ref[...])
        g_ref[...] = g
        m_new = ADAM_B1 * m_ref[...] + (1.0 - ADAM_B1) * g
        v_new = ADAM_B2 * v_ref[...] + (1.0 - ADAM_B2) * (g * g)
        m_hat = m_new / (1.0 - ADAM_B1 ** ADAM_STEP)
        v_hat = v_new / (1.0 - ADAM_B2 ** ADAM_STEP)
        d_ref[...] = -ADAM_LR * (m_hat / (jnp.sqrt(v_hat) + ADAM_EPS) + ADAM_WD * w_ref[...])
        mo_ref[...] = m_new
        vo_ref[...] = v_new

    spec = pl.BlockSpec((br, cols), lambda r, c: (r, 0))
    mine_spec = pl.BlockSpec((br, cols), lambda r, c: (jnp.clip(r - c[0] * nb, 0, nb - 1), 0))
    theirs_spec = pl.BlockSpec((br, cols), lambda r, c: (jnp.clip(r - (1 - c[0]) * nb, 0, nb - 1), 0))
    return pl.pallas_call(
        body, name=f"adamw_{tag}",
        out_shape=[jax.ShapeDtypeStruct((rows, cols), F32)] * 4,
        grid_spec=pltpu.PrefetchScalarGridSpec(
            num_scalar_prefetch=1, grid=(rows // br,),
            in_specs=[spec, mine_spec, theirs_spec, spec, spec], out_specs=[spec] * 4),
        compiler_params=_params(("arbitrary",)),
    )(c_arr, w, mine, theirs, m, v)


def _matmul(name, x, w, out_shape, grid, x_spec, w_spec, o_spec, *, nt=False, vmem=None):
    nk = grid[2]
    acc_shape = tuple(d for d in o_spec.block_shape if d is not None)

    def body(x_ref, w_ref, o_ref, acc_ref):
        k = pl.program_id(2)
        part = _dot_nt(x_ref[...], w_ref[...]) if nt else _dot(x_ref[...], w_ref[...])
        if nk == 1:
            o_ref[...] = part.astype(o_ref.dtype)
        else:
            @pl.when(k == 0)
            def _():
                acc_ref[...] = part

            @pl.when(k > 0)
            def _():
                acc_ref[...] += part

            @pl.when(k == nk - 1)
            def _():
                o_ref[...] = acc_ref[...].astype(o_ref.dtype)

    return pl.pallas_call(
        body, name=name, out_shape=out_shape, grid=grid,
        in_specs=[x_spec, w_spec], out_specs=o_spec,
        scratch_shapes=[pltpu.VMEM(acc_shape if nk > 1 else (8, 128), F32)],
        compiler_params=_params(("parallel", "parallel", "arbitrary"), vmem),
    )(x, w)


def _weight_grad(name, xt, dy, bn, out_rows=None):
    m, t = xt.shape
    n = dy.shape[1]
    bm = m if out_rows is None else out_rows
    bk = _k_tile(t)
    return _matmul(
        name, xt, dy, jax.ShapeDtypeStruct((m, n), F32), (m // bm, n // bn, t // bk),
        pl.BlockSpec((bm, bk), lambda a, b, k: (a, k)),
        pl.BlockSpec((bk, bn), lambda a, b, k: (k, b)),
        pl.BlockSpec((bm, bn), lambda a, b, k: (a, b)), vmem=VMEM_BIG)


def _weight_grad_t(name, xt, dy):
    m, t = xt.shape
    n = dy.shape[1]
    bn = min(n, 512)
    bk = _k_tile(t)
    nk = t // bk

    def body(x_ref, dy_ref, o_ref, acc_ref):
        k = pl.program_id(1)
        part = _dot(x_ref[...], dy_ref[...].astype(BF16))

        @pl.when(k == 0)
        def _():
            acc_ref[...] = part

        @pl.when(k > 0)
        def _():
            acc_ref[...] += part

        @pl.when(k == nk - 1)
        def _():
            o_ref[...] = acc_ref[...].T

    return pl.pallas_call(
        body, name=name, out_shape=jax.ShapeDtypeStruct((n, m), F32), grid=(n // bn, nk),
        in_specs=[pl.BlockSpec((m, bk), lambda b, k: (0, k)), pl.BlockSpec((bk, bn), lambda b, k: (k, b))],
        out_specs=pl.BlockSpec((bn, m), lambda b, k: (b, 0)),
        scratch_shapes=[pltpu.VMEM((m, bn), F32)],
        compiler_params=_params(("parallel", "arbitrary"), VMEM_BIG),
    )(xt, dy)


def _mix_in_bwd(pieces, wt, h, g, dh_in, post):
    t, d = h.shape
    tm = ROW_TILE // 2
    widths = [p.shape[1] for p in pieces]
    n = len(pieces)

    def body(*refs):
        dy_refs = refs[:n]
        w_ref, h_ref, g_ref, dhi_ref, xp_ref, gp_ref, dh_ref, dg_ref, dxp_ref, dgp_ref = refs[n:]
        first = pl.program_id(0) == 0

        @pl.when(first)
        def _():
            dg_ref[...] = jnp.zeros_like(dg_ref)

        dn, off = None, 0
        for dy_ref, wd in zip(dy_refs, widths):
            part = _dot(dy_ref[...].astype(BF16), w_ref[off:off + wd, :])
            dn = part if dn is None else dn + part
            off += wd
        dx, dg = _rms_bwd(h_ref[...], g_ref[...], dn)
        dh = dhi_ref[...] + dx
        dh_ref[...] = dh
        dg_ref[...] += dg
        _next_post_norm_bwd(dh, (xp_ref, gp_ref, dxp_ref, dgp_ref), post[2], first)

    row = pl.BlockSpec((tm, d), lambda i: (i, 0))
    vec = pl.BlockSpec((1, d), lambda i: (0, 0))
    return pl.pallas_call(
        body, name="mix_in_bwd",
        out_shape=[jax.ShapeDtypeStruct((t, d), F32), jax.ShapeDtypeStruct((1, d), F32),
                   jax.ShapeDtypeStruct((t, d), BF16), jax.ShapeDtypeStruct((1, d), F32)],
        grid=(t // tm,),
        in_specs=[pl.BlockSpec((tm, wd), lambda i: (i, 0)) for wd in widths]
        + [pl.BlockSpec(wt.shape, lambda i: (0, 0)), row, vec, row, row, vec],
        out_specs=[row, vec, row, vec],
        compiler_params=_params(("arbitrary",), VMEM_BIG),
    )(*pieces, wt, h, g, dh_in, post[0], post[1])


def _read_token_rows(src_hbm, buf, sems, i, n):
    tm = buf.shape[1]

    def first_tile():
        return pltpu.make_async_copy(src_hbm.at[pl.ds(0, tm - N_FRONT)], buf.at[0, pl.ds(N_FRONT, tm - N_FRONT)],
                                     sems.at[0])

    def tile(j):
        return pltpu.make_async_copy(src_hbm.at[pl.ds(pl.multiple_of(j * tm - N_FRONT, N_FRONT), tm)],
                                     buf.at[j % 2], sems.at[j % 2])

    @pl.when(i == 0)
    def _():
        buf[0, 0:N_FRONT, :] = jnp.zeros((N_FRONT, buf.shape[2]), buf.dtype)
        first_tile().start()

    @pl.when(i + 1 < n)
    def _():
        tile(i + 1).start()

    @pl.when(i == 0)
    def _():
        first_tile().wait()

    @pl.when(i > 0)
    def _():
        tile(i).wait()

    return buf.at[i % 2]


def _embed_norm(x, meta, g):
    seq, d = x.shape
    t = seq + N_FRONT
    tm = ROW_TILE

    def body(x_hbm, meta_ref, g_ref, h_ref, n_ref, nt_ref, buf, sems):
        i = pl.program_id(0)
        rows = _read_token_rows(x_hbm, buf, sems, i, t // tm)

        @pl.when(i == 0)
        def _():
            buf[0, ROW_PAD:N_FRONT, :] = meta_ref[...]

        h = rows[...]
        h_ref[...] = h
        y = h * _rstd(h) * g_ref[...]
        n_ref[...] = y.astype(BF16)
        nt_ref[...] = y.T.astype(BF16)

    row = pl.BlockSpec((tm, d), lambda i: (i, 0))
    return pl.pallas_call(
        body, name="embed_and_ffn1_pre_norm",
        out_shape=[jax.ShapeDtypeStruct((t, d), F32), jax.ShapeDtypeStruct((t, d), BF16),
                   jax.ShapeDtypeStruct((d, t), BF16)],
        grid=(t // tm,),
        in_specs=[ANY, pl.BlockSpec((N_META, d), lambda i: (0, 0)), pl.BlockSpec((1, d), lambda i: (0, 0))],
        out_specs=[row, row, pl.BlockSpec((d, tm), lambda i: (0, i))],
        scratch_shapes=[pltpu.VMEM((2, tm, d), F32), pltpu.SemaphoreType.DMA((2,))],
        compiler_params=_params(("arbitrary",)),
    )(x, meta, g)


def _slot_of(kk):
    return (kk % 2) * 2 + kk // 2


def _ffn_in(name, n, w4):
    t, d = n.shape
    cw = w4.shape[2]
    tm = ROW_TILE

    def body(x_ref, wg_ref, wu_ref, ab_ref, s_ref, st_ref):
        x = x_ref[...]
        a = _dot(x, wg_ref[...])
        b = _dot(x, wu_ref[...])
        ab_ref[:, :cw] = a.astype(BF16)
        ab_ref[:, cw:] = b.astype(BF16)
        s = a * _sigmoid(a) * b
        s_ref[...] = s.astype(BF16)
        st_ref[...] = s.T.astype(BF16)

    return pl.pallas_call(
        body, name=name,
        out_shape=[jax.ShapeDtypeStruct((t, 4 * cw), BF16), jax.ShapeDtypeStruct((t, 2 * cw), BF16),
                   jax.ShapeDtypeStruct((2 * cw, t), BF16)],
        grid=(2, t // tm),
        in_specs=[pl.BlockSpec((tm, d), lambda j, i: (i, 0)),
                  pl.BlockSpec((None, d, cw), lambda j, i: (j, 0, 0)),
                  pl.BlockSpec((None, d, cw), lambda j, i: (2 + j, 0, 0))],
        out_specs=[pl.BlockSpec((tm, 2 * cw), lambda j, i: (i, j)),
                   pl.BlockSpec((tm, cw), lambda j, i: (i, j)),
                   pl.BlockSpec((cw, tm), lambda j, i: (j, i))],
        compiler_params=_params(("parallel", "parallel"), VMEM_BIG),
    )(n, w4, w4)


def _mm_resid_norm(name, x, w, h, g_post, alpha, g_next):
    t, kdim = x.shape
    d = w.shape[1]
    tm = ROW_TILE
    with_next = g_next is not None

    def body(x_ref, w_ref, h_ref, gp_ref, gn_ref, f_ref, hn_ref, *rest):
        f = _dot(x_ref[...], w_ref[...])
        f_ref[...] = f
        hn = h_ref[...] + alpha * (f * _rstd(f) * gp_ref[...])
        hn_ref[...] = hn
        if with_next:
            y = hn * _rstd(hn) * gn_ref[...]
            rest[0][...] = y.astype(BF16)
            rest[1][...] = y.T.astype(BF16)

    row = lambda i: (i, 0)
    vec = pl.BlockSpec((1, d), lambda i: (0, 0))
    out_shape = [jax.ShapeDtypeStruct((t, d), F32), jax.ShapeDtypeStruct((t, d), F32)]
    out_specs = [pl.BlockSpec((tm, d), row), pl.BlockSpec((tm, d), row)]
    if with_next:
        out_shape += [jax.ShapeDtypeStruct((t, d), BF16), jax.ShapeDtypeStruct((d, t), BF16)]
        out_specs += [pl.BlockSpec((tm, d), row), pl.BlockSpec((d, tm), lambda i: (0, i))]
    return pl.pallas_call(
        body, name=name, out_shape=out_shape, grid=(t // tm,),
        in_specs=[pl.BlockSpec((tm, kdim), row), pl.BlockSpec((kdim, d), lambda i: (0, 0)),
                  pl.BlockSpec((tm, d), row), vec, vec],
        out_specs=out_specs,
        compiler_params=_params(("parallel",), VMEM_BIG),
    )(x, w, h, g_post, g_post if g_next is None else g_next)


def _in_proj(u, w):
    t, d = u.shape
    nz = w.shape[0]
    nq = 3 * ATTN_W
    tm = ROW_TILE // 2

    def body(u_ref, w_ref, qkv_ref, z_ref):
        qkv_ref[...] = _dot_nt(u_ref[...], w_ref[0:nq, :]).astype(BF16)
        z_ref[...] = _dot_nt(u_ref[...], w_ref[nq:, :])

    return pl.pallas_call(
        body, name="mix_in_proj",
        out_shape=[jax.ShapeDtypeStruct((t, nq), BF16), jax.ShapeDtypeStruct((t, nz - nq), F32)],
        grid=(t // tm,),
        in_specs=[pl.BlockSpec((tm, d), lambda i: (i, 0)), pl.BlockSpec((nz, d), lambda i: (0, 0))],
        out_specs=[pl.BlockSpec((tm, nq), lambda i: (i, 0)), pl.BlockSpec((tm, nz - nq), lambda i: (i, 0))],
        compiler_params=_params(("parallel",), VMEM_BIG),
    )(u, w)


def _gate_prep(z, b_pad, f_col):
    t = z.shape[0]
    tm = ROW_TILE

    def body(z_ref, b_ref, f_ref, carry_ref):
        i = pl.program_id(0)

        @pl.when(i == 0)
        def _():
            carry_ref[...] = jnp.zeros_like(carry_ref)

        xs = z_ref[...] + b_ref[...]
        logf = jnp.minimum(xs, 0.0) - jnp.log(1.0 + jnp.exp(-jnp.abs(xs)))
        row = i * tm + lax.broadcasted_iota(jnp.int32, (tm, 1), 0)
        logf = jnp.where(row >= ROW_PAD, logf, 0.0)
        tri = (lax.broadcasted_iota(jnp.int32, (tm, tm), 0) >= lax.broadcasted_iota(jnp.int32, (tm, tm), 1))
        f = jnp.dot(tri.astype(F32), logf, preferred_element_type=F32, precision=lax.Precision.HIGHEST)
        f = f + carry_ref[0:1, :]
        f_ref[...] = f
        carry_ref[...] = jnp.broadcast_to(f[tm - 1:tm, :], carry_ref.shape)

    return pl.pallas_call(
        body, name="forget_gate_cumsum", out_shape=jax.ShapeDtypeStruct((t, 128), F32),
        grid=(t // tm,),
        in_specs=[pl.BlockSpec((tm, 128), lambda i: (i, f_col // 128)), pl.BlockSpec((1, 128), lambda i: (0, 0))],
        out_specs=pl.BlockSpec((tm, 128), lambda i: (i, 0)),
        scratch_shapes=[pltpu.VMEM((8, 128), F32)],
        compiler_params=_params(("arbitrary",)),
    )(z, b_pad)


def _lane_halves():
    lane = lax.broadcasted_iota(jnp.int32, (1, 128), 1)
    return lane < HEAD_DIM


def _causal_mask(tq, tk, row0=0):
    row = row0 + lax.broadcasted_iota(jnp.int32, (tq, 1), 0)
    col = lax.broadcasted_iota(jnp.int32, (1, tk), 1)
    return col <= row


def _lane_one(lane):
    return (lax.broadcasted_iota(jnp.int32, (1, 128), 1) == lane).astype(BF16)


def _split3(x):
    hi = x.astype(BF16)
    rest = x - hi.astype(F32)
    mid = rest.astype(BF16)
    return hi, mid, (rest - mid.astype(F32)).astype(BF16)


def _split3_glue(x):
    hi = lax.reduce_precision(x, 8, 7)
    mid = lax.reduce_precision(x - hi, 8, 7)
    lo = lax.reduce_precision((x - hi) - mid, 8, 7)
    return hi.astype(BF16), mid.astype(BF16), lo.astype(BF16)


def _aug_pairs(cols):
    t = cols[0].shape[0]
    a = jnp.pad(jnp.stack(cols, axis=2), ((0, 0), (0, 0), (0, HEAD_DIM - len(cols))))
    a = a.reshape(t, 4, 2, HEAD_DIM)[:, :, ::-1, :]
    return jnp.transpose(a.reshape(t, 4, 128), (1, 0, 2))


def _attn_bias_operands(f_heads, lse_heads=None):
    t = f_heads.shape[0]
    one = jnp.ones((t, HEADS), BF16)
    row = lax.broadcasted_iota(jnp.int32, (t, 1), 0)
    fq = _split3_glue(f_heads)
    fk = _split3_glue(jnp.where(row < ROW_PAD, 1e9, f_heads))
    q_cols = list(fq) + [one] * 3
    k_cols = [one] * 3 + [-c for c in fk]
    if lse_heads is not None:
        q_cols += [-c for c in _split3_glue(lse_heads)]
        k_cols += [one] * 3
    return _aug_pairs(q_cols), _aug_pairs(k_cols)


def _attn_fwd(z, aug_q, aug_k):
    t = z.shape[0]
    tq = tk = ROW_TILE
    nq = t // tq
    grp = ATTN_KV_GROUP
    steps = [(qi, ka) for qi in range(nq) for ka in range(0, qi + 1, grp)]
    q_tab = jnp.array([qi for qi, _ in steps], jnp.int32)
    k_tab = jnp.array([ka for _, ka in steps], jnp.int32)

    def body(qt_ref, kt_ref, q_ref, *refs):
        k_refs, v_refs, aq_ref, ak_refs = refs[:grp], refs[grp:2 * grp], refs[2 * grp], refs[2 * grp + 1:3 * grp + 1]
        o_ref, lse_ref, m_ref, l_ref, acc_ref = refs[3 * grp + 1:]
        step = pl.program_id(1)
        qi, ka = qt_ref[step], kt_ref[step]

        @pl.when(ka == 0)
        def _():
            m_ref[...] = jnp.full_like(m_ref, NEG)
            l_ref[...] = jnp.zeros_like(l_ref)
            acc_ref[...] = jnp.zeros_like(acc_ref)

        def sweep(diagonal):
            first = _lane_halves()
            halves = (first, jnp.logical_not(first))
            q = (q_ref[...] * (HEAD_DIM ** -0.5)).astype(BF16)
            aq = aq_ref[...]
            qa = [jnp.where(lanes, q, aq) for lanes in halves]
            blocks = list(zip(k_refs, v_refs, ak_refs, diagonal))
            s = []
            for k_ref, _, ak_ref, diag in blocks:
                k, ak = k_ref[...].astype(BF16), ak_ref[...]
                for hh, lanes in enumerate(halves):
                    s_c = _dot_nt(qa[hh], jnp.where(lanes, k, ak))
                    s.append(jnp.where(_causal_mask(tq, tk), s_c, NEG) if diag else s_c)
            nb = len(blocks)
            m_prev = [m_ref[:, c0:c0 + 1] for c0 in (0, HEAD_DIM)]
            m_new = []
            for hh in range(2):
                m_h = m_prev[hh]
                for b in range(nb):
                    m_h = jnp.maximum(m_h, jnp.max(s[2 * b + hh], axis=1, keepdims=True))
                m_new.append(m_h)
            pv = [None, None]
            for b, (_, v_ref, _, _) in enumerate(blocks):
                v = v_ref[...].astype(BF16)
                for hh, (lanes, a0) in enumerate(zip(halves, (HEAD_DIM, 0))):
                    part = _dot(jnp.exp(s[2 * b + hh] - m_new[hh]).astype(BF16), jnp.where(lanes, v, _lane_one(a0)))
                    pv[hh] = part if pv[hh] is None else pv[hh] + part
            al0, al1 = [jnp.exp(mp - m_h) for mp, m_h in zip(m_prev, m_new)]
            l0 = al0 * l_ref[:, 0:1] + pv[0][:, HEAD_DIM:HEAD_DIM + 1]
            l1 = al1 * l_ref[:, HEAD_DIM:HEAD_DIM + 1] + pv[1][:, 0:1]
            acc_ref[...] = acc_ref[...] * jnp.where(first, al0, al1) + jnp.where(first, pv[0], pv[1])
            m_ref[...] = jnp.where(first, m_new[0], m_new[1])
            l_ref[...] = jnp.where(first, l0, l1)

        def finish():
            o_ref[...] = acc_ref[...] / l_ref[...]
            lse_ref[...] = m_ref[...] + jnp.log(l_ref[...])

        @pl.when(ka + grp - 1 < qi)
        def _():
            sweep((False,) * grp)

        for nb in range(1, grp + 1):
            @pl.when(ka + nb - 1 == qi)
            def _(nb=nb):
                sweep((False,) * (nb - 1) + (True,))
                finish()

    def kblock(j):
        return lambda s, qt, kt: jnp.minimum(kt[s] + j, qt[s])

    kbs = [kblock(j) for j in range(grp)]
    return pl.pallas_call(
        body, name="attention_fwd",
        out_shape=[jax.ShapeDtypeStruct((t, ATTN_W), F32), jax.ShapeDtypeStruct((t, ATTN_W), F32)],
        grid_spec=pltpu.PrefetchScalarGridSpec(
            num_scalar_prefetch=2, grid=(4, len(steps)),
            in_specs=[pl.BlockSpec((tq, 128), lambda p, s, qt, kt: (qt[s], p))]
            + [pl.BlockSpec((tk, 128), functools.partial(lambda p, s, qt, kt, kb: (kb(s, qt, kt), 4 + p), kb=kb))
               for kb in kbs]
            + [pl.BlockSpec((tk, 128), functools.partial(lambda p, s, qt, kt, kb: (kb(s, qt, kt), 8 + p), kb=kb))
               for kb in kbs]
            + [pl.BlockSpec((None, tq, 128), lambda p, s, qt, kt: (p, qt[s], 0))]
            + [pl.BlockSpec((None, tk, 128), functools.partial(lambda p, s, qt, kt, kb: (p, kb(s, qt, kt), 0), kb=kb))
               for kb in kbs],
            out_specs=[pl.BlockSpec((tq, 128), lambda p, s, qt, kt: (qt[s], p)),
                       pl.BlockSpec((tq, 128), lambda p, s, qt, kt: (qt[s], p))],
            scratch_shapes=[pltpu.VMEM((tq, 128), F32)] * 3),
        compiler_params=_params(("parallel", "arbitrary"), VMEM_BIG),
    )(q_tab, k_tab, z, *([z] * (2 * grp)), aug_q, *([aug_k] * grp))


def _attn_bwd(z, aug_q, aug_k, o, do):
    t = z.shape[0]
    tq = tk = ROW_TILE
    nq = t // tq
    grp = ATTN_Q_GROUP
    steps = [(qa, ki) for ki in range(nq) for qa in range(ki, nq, grp)]
    q_tab = jnp.array([qa for qa, _ in steps], jnp.int32)
    k_tab = jnp.array([ki for _, ki in steps], jnp.int32)
    tn = (((0,), (0,)), ((), ()))

    def body(qt_ref, kt_ref, *refs):
        q_refs, (k_ref, v_ref) = refs[:grp], refs[grp:grp + 2]
        aq_refs, ak_ref = refs[grp + 2:2 * grp + 2], refs[2 * grp + 2]
        o_refs, do_refs = refs[2 * grp + 3:3 * grp + 3], refs[3 * grp + 3:4 * grp + 3]
        dq_ref, dk_ref, dv_ref, dfk_ref, dfq_ref = refs[4 * grp + 3:]
        step = pl.program_id(1)
        qa, ki = qt_ref[step], kt_ref[step]

        def rows(j):
            return pl.ds(pl.multiple_of((qa + j) * tq, tq), tq)

        for j in range(grp):
            @pl.when((ki == 0) & (qa + j < nq))
            def _(j=j):
                dq_ref[rows(j), :] = jnp.zeros((tq, 128), F32)
                dfq_ref[rows(j), :] = jnp.zeros((tq, 128), F32)

        @pl.when(qa == ki)
        def _():
            dk_ref[...] = jnp.zeros_like(dk_ref)
            dv_ref[...] = jnp.zeros_like(dv_ref)
            dfk_ref[...] = jnp.zeros_like(dfk_ref)

        def sweep(nb, diagonal):
            first = _lane_halves()
            lane = lax.broadcasted_iota(jnp.int32, (1, 128), 1)
            scale = HEAD_DIM ** -0.5
            halves = (first, jnp.logical_not(first))
            spare = (HEAD_DIM, 0)
            k = k_ref[...].astype(BF16)
            v = v_ref[...].astype(BF16)
            ak = ak_ref[...]
            k_bias = [jnp.where(lanes, k, ak) for lanes in halves]
            k_ones = [jnp.where(lanes, k, _lane_one(a)) for lanes, a in zip(halves, spare)]
            v_ones = [jnp.where(lanes, v, ((lane >= a) & (lane < a + 3)).astype(BF16)) for lanes, a in zip(halves, spare)]
            chains = [(j, hh) for j in range(nb) for hh in range(2)]
            q16, do16, dos = [], [], []
            for j in range(nb):
                q16.append((q_refs[j][...] * scale).astype(BF16))
                do_ = do_refs[j][...]
                do16.append(do_.astype(BF16))
                od = o_refs[j][...] * do_
                for lanes, a in zip(halves, spare):
                    d_hi, d_mid, d_lo = _split3(jnp.sum(jnp.where(lanes, od, 0.0), axis=1, keepdims=True))
                    minus_delta = jnp.where(lane == a, -d_hi, jnp.where(lane == a + 1, -d_mid,
                                            jnp.where(lane == a + 2, -d_lo, jnp.zeros((), BF16))))
                    dos.append(jnp.where(lanes, do16[j], minus_delta))
            s = [_dot_nt(jnp.where(halves[hh], q16[j], aq_refs[j][...]), k_bias[hh]) for j, hh in chains]
            dp = [_dot_nt(dos[2 * j + hh], v_ones[hh]) for j, hh in chains]
            p = [jnp.exp(s_c) for s_c in s]
            if diagonal:
                p = [jnp.where(_causal_mask(tq, tk), p_c, 0.0) if j == 0 else p_c for p_c, (j, _) in zip(p, chains)]
            ds16 = [(p_c * dp_c).astype(BF16) for p_c, dp_c in zip(p, dp)]
            dv, dk = [None, None], [None, None]
            for c, (j, hh) in enumerate(chains):
                lanes = halves[hh]
                dv_c = lax.dot_general(p[c].astype(BF16), jnp.where(lanes, do16[j], jnp.zeros((), BF16)), tn,
                                       preferred_element_type=F32)
                dk_c = lax.dot_general(ds16[c], jnp.where(lanes, q16[j], _lane_one(spare[hh])), tn,
                                       preferred_element_type=F32)
                dv[hh] = dv_c if dv[hh] is None else dv[hh] + dv_c
                dk[hh] = dk_c if dk[hh] is None else dk[hh] + dk_c
            for j in range(nb):
                dq0, dq1 = [_dot(ds16[2 * j + hh], k_ones[hh]) for hh in range(2)]
                dq_ref[rows(j), :] += jnp.where(first, dq0, dq1) * scale
                dfq_ref[rows(j), :] += jnp.where(first, dq0[:, HEAD_DIM:HEAD_DIM + 1], dq1[:, 0:1])
            dk_ref[...] += jnp.where(first, dk[0], dk[1])
            dfk_ref[...] += jnp.where(first, dk[0][:, HEAD_DIM:HEAD_DIM + 1], dk[1][:, 0:1])
            dv_ref[...] += dv[0] + dv[1]

        for nb in range(1, grp + 1):
            exists = (qa + grp <= nq) if nb == grp else (qa + nb == nq)
            for diagonal in (False, True):
                @pl.when(exists & ((qa == ki) == diagonal))
                def _(nb=nb, diagonal=diagonal):
                    sweep(nb, diagonal)

    def qblock(j):
        return lambda s, qt: jnp.minimum(qt[s] + j, nq - 1)

    qbs = [qblock(j) for j in range(grp)]
    qcol = [functools.partial(lambda p, s, qt, kt, qb: (qb(s, qt), p), qb=qb) for qb in qbs]
    krow = lambda p, s, qt, kt: (kt[s], p)
    return pl.pallas_call(
        body, name="attention_bwd",
        out_shape=[jax.ShapeDtypeStruct((t, ATTN_W), F32)] * 5,
        grid_spec=pltpu.PrefetchScalarGridSpec(
            num_scalar_prefetch=2, grid=(4, len(steps)),
            in_specs=[pl.BlockSpec((tq, 128), m) for m in qcol]
            + [pl.BlockSpec((tk, 128), lambda p, s, qt, kt: (kt[s], 4 + p)),
               pl.BlockSpec((tk, 128), lambda p, s, qt, kt: (kt[s], 8 + p))]
            + [pl.BlockSpec((None, tq, 128), functools.partial(lambda p, s, qt, kt, qb: (p, qb(s, qt), 0), qb=qb))
               for qb in qbs]
            + [pl.BlockSpec((None, tk, 128), lambda p, s, qt, kt: (p, kt[s], 0))]
            + [pl.BlockSpec((tq, 128), m) for m in qcol] + [pl.BlockSpec((tq, 128), m) for m in qcol],
            out_specs=[pl.BlockSpec((t, 128), lambda p, s, qt, kt: (0, p)),
                       pl.BlockSpec((tk, 128), krow), pl.BlockSpec((tk, 128), krow), pl.BlockSpec((tk, 128), krow),
                       pl.BlockSpec((t, 128), lambda p, s, qt, kt: (0, p))]),
        compiler_params=_params(("parallel", "arbitrary"), VMEM_BIG),
    )(q_tab, k_tab, *([z] * grp), z, z, *([aug_q] * grp), aug_k, *([o] * grp), *([do] * grp))


def _shifted(prev_rows, x, shift):
    tm = x.shape[0]
    return pltpu.roll(jnp.concatenate([prev_rows, x], axis=0), shift, 0)[8:8 + tm]


def _ahead(x, next_rows, shift):
    tm = x.shape[0]
    return pltpu.roll(jnp.concatenate([x, next_rows], axis=0), tm + 8 - shift, 0)[0:tm]


def _conv_col0(z):
    return (z.shape[1] - F_PAD - 3 * CONV_W) // CONV_W


def _conv_specs(tm, c0):
    cols = (c0, c0 + 1, c0 + 2)
    tiles = [pl.BlockSpec((tm, CONV_W), functools.partial(lambda i, c: (i, c), c=c)) for c in cols]
    halos = [pl.BlockSpec((8, CONV_W), functools.partial(lambda i, c: (jnp.maximum(i * (tm // 8) - 1, 0), c), c=c))
             for c in cols]
    return tiles, halos


def _conv_gate(z, conv_w):
    t = z.shape[0]
    tm = ROW_TILE
    nt = t // tm

    def body(cb_ref, cc_ref, ci_ref, hc_ref, hi_ref, w_ref, g_ref, gt_ref):
        i = pl.program_id(0)
        cc = cc_ref[...] * ci_ref[...]
        prev = jnp.where(i > 0, hc_ref[...] * hi_ref[...], 0.0)
        conv = w_ref[0:1, :] * _shifted(prev, cc, 2) + w_ref[1:2, :] * _shifted(prev, cc, 1) + w_ref[2:3, :] * cc
        g = cb_ref[...] * conv
        g_ref[...] = g.astype(BF16)
        gt_ref[...] = g.T.astype(BF16)

    (cb, cc, ci), (_, hc, hi) = _conv_specs(tm, _conv_col0(z))
    return pl.pallas_call(
        body, name="conv_gate_fwd",
        out_shape=[jax.ShapeDtypeStruct((t, CONV_W), BF16), jax.ShapeDtypeStruct((CONV_W, t), BF16)],
        grid=(nt,),
        in_specs=[cb, cc, ci, hc, hi, pl.BlockSpec((8, CONV_W), lambda i: (0, 0))],
        out_specs=[pl.BlockSpec((tm, CONV_W), lambda i: (i, 0)), pl.BlockSpec((CONV_W, tm), lambda i: (0, i))],
        compiler_params=_params(("parallel",)),
    )(z, z, z, z, z, conv_w)


def _conv_bwd(z, dg, conv_w):
    t = z.shape[0]
    tm = ROW_TILE
    nt = t // tm

    def body(cb_ref, cc_ref, ci_ref, hc_ref, hi_ref, dg_ref, ncb_ref, ndg_ref, w_ref, dz_ref, dw_ref):
        i = pl.program_id(0)

        @pl.when(i == 0)
        def _():
            dw_ref[...] = jnp.zeros_like(dw_ref)

        cb, c_c, c_in = cb_ref[...], cc_ref[...], ci_ref[...]
        cc = c_c * c_in
        prev = jnp.where(i > 0, hc_ref[...] * hi_ref[...], 0.0)
        cc1, cc2 = _shifted(prev, cc, 1), _shifted(prev, cc, 2)
        w0, w1, w2 = w_ref[0:1, :], w_ref[1:2, :], w_ref[2:3, :]
        conv = w0 * cc2 + w1 * cc1 + w2 * cc
        dgv = dg_ref[...]
        dconv = dgv * cb
        nxt = jnp.where(i < nt - 1, ndg_ref[...] * ncb_ref[...], 0.0)
        dcc = w2 * dconv + w1 * _ahead(dconv, nxt, 1) + w0 * _ahead(dconv, nxt, 2)
        dz_ref[:, 0:CONV_W] = (dgv * conv).astype(BF16)
        dz_ref[:, CONV_W:2 * CONV_W] = (dcc * c_in).astype(BF16)
        dz_ref[:, 2 * CONV_W:] = (dcc * c_c).astype(BF16)
        dw_ref[0:1, :] += jnp.sum(dconv * cc2, axis=0, keepdims=True)
        dw_ref[1:2, :] += jnp.sum(dconv * cc1, axis=0, keepdims=True)
        dw_ref[2:3, :] += jnp.sum(dconv * cc, axis=0, keepdims=True)

    c0 = _conv_col0(z)
    (cb, cc, ci), (_, hc, hi) = _conv_specs(tm, c0)
    nxt = lambda i, c: (jnp.minimum((i + 1) * (tm // 8), t // 8 - 1), c)
    return pl.pallas_call(
        body, name="conv_gate_bwd",
        out_shape=[jax.ShapeDtypeStruct((t, 3 * CONV_W), BF16), jax.ShapeDtypeStruct((8, CONV_W), F32)],
        grid=(nt,),
        in_specs=[cb, cc, ci, hc, hi, pl.BlockSpec((tm, CONV_W), lambda i: (i, 0)),
                  pl.BlockSpec((8, CONV_W), lambda i: nxt(i, c0)), pl.BlockSpec((8, CONV_W), lambda i: nxt(i, 0)),
                  pl.BlockSpec((8, CONV_W), lambda i: (0, 0))],
        out_specs=[pl.BlockSpec((tm, 3 * CONV_W), lambda i: (i, 0)), pl.BlockSpec((8, CONV_W), lambda i: (0, 0))],
        compiler_params=_params(("arbitrary",)),
    )(z, z, z, z, z, dg, z, dg, conv_w)


def _branch_mix(z, o, g, w_ab, w_cb, d):
    t = z.shape[0]
    tm = ROW_TILE
    ga_col = 0

    def body(o_ref, g_ref, ga_ref, gc_ref, wa_ref, wc_ref, mp_ref, mpt_ref, ot_ref):
        o_ = o_ref[...]
        ya = _dot(o_.astype(BF16), wa_ref[...])
        yc = _dot(g_ref[...], wc_ref[...])
        mp = _sigmoid(ga_ref[...]) * ya + _sigmoid(gc_ref[...]) * yc
        mp_ref[...] = mp.astype(BF16)
        mpt_ref[...] = mp.T.astype(BF16)
        ot_ref[...] = o_.T.astype(BF16)

    return pl.pallas_call(
        body, name="branch_mix_fwd",
        out_shape=[jax.ShapeDtypeStruct((t, d), BF16), jax.ShapeDtypeStruct((d, t), BF16),
                   jax.ShapeDtypeStruct((ATTN_W, t), BF16)],
        grid=(t // tm,),
        in_specs=[pl.BlockSpec((tm, ATTN_W), lambda i: (i, 0)), pl.BlockSpec((tm, CONV_W), lambda i: (i, 0)),
                  pl.BlockSpec((tm, d), lambda i: (i, ga_col)), pl.BlockSpec((tm, d), lambda i: (i, ga_col + 1)),
                  pl.BlockSpec((ATTN_W, d), lambda i: (0, 0)), pl.BlockSpec((CONV_W, d), lambda i: (0, 0))],
        out_specs=[pl.BlockSpec((tm, d), lambda i: (i, 0)), pl.BlockSpec((d, tm), lambda i: (0, i)),
                   pl.BlockSpec((ATTN_W, tm), lambda i: (0, i))],
        compiler_params=_params(("parallel",), VMEM_BIG),
    )(o, g, z, z, w_ab, w_cb)


def _branch_bwd(z, o, g, dmixed, w_out, w_ab, w_cb, d):
    t = z.shape[0]
    tm = ROW_TILE // 2
    ga_col = 0

    def body(dm_ref, o_ref, g_ref, ga_ref, gc_ref, wo_ref, wa_ref, wc_ref, dya_ref, dyc_ref, dgt_ref, do_ref, dg_ref):
        dmp = _dot_nt(dm_ref[...], wo_ref[...])
        ya = _dot(o_ref[...].astype(BF16), wa_ref[...])
        yc = _dot(g_ref[...], wc_ref[...])
        sa, sc = _sigmoid(ga_ref[...]), _sigmoid(gc_ref[...])
        dya = (dmp * sa).astype(BF16)
        dyc = (dmp * sc).astype(BF16)
        dya_ref[...] = dya
        dyc_ref[...] = dyc
        dgt_ref[:, :d] = (dmp * ya * sa * (1.0 - sa)).astype(BF16)
        dgt_ref[:, d:] = (dmp * yc * sc * (1.0 - sc)).astype(BF16)
        do_ref[...] = _dot_nt(dya, wa_ref[...])
        dg_ref[...] = _dot_nt(dyc, wc_ref[...])

    row = lambda i: (i, 0)
    fixed = lambda i: (0, 0)
    return pl.pallas_call(
        body, name="branch_mix_bwd",
        out_shape=[jax.ShapeDtypeStruct((t, d), BF16), jax.ShapeDtypeStruct((t, d), BF16),
                   jax.ShapeDtypeStruct((t, 2 * d), BF16), jax.ShapeDtypeStruct((t, ATTN_W), F32),
                   jax.ShapeDtypeStruct((t, CONV_W), F32)],
        grid=(t // tm,),
        in_specs=[pl.BlockSpec((tm, d), row), pl.BlockSpec((tm, ATTN_W), row), pl.BlockSpec((tm, CONV_W), row),
                  pl.BlockSpec((tm, d), lambda i: (i, ga_col)), pl.BlockSpec((tm, d), lambda i: (i, ga_col + 1)),
                  pl.BlockSpec((d, d), fixed), pl.BlockSpec((ATTN_W, d), fixed), pl.BlockSpec((CONV_W, d), fixed)],
        out_specs=[pl.BlockSpec((tm, d), row), pl.BlockSpec((tm, d), row), pl.BlockSpec((tm, 2 * d), row),
                   pl.BlockSpec((tm, ATTN_W), row), pl.BlockSpec((tm, CONV_W), row)],
        compiler_params=_params(("parallel",), VMEM_BIG),
    )(dmixed, o, g, z, z, w_out, w_ab, w_cb)


def _loss_norm_bwd(h, target, f, g_post, alpha):
    t, d = h.shape
    tm = ROW_TILE

    def body(h_ref, t_hbm, f_ref, g_ref, dh_ref, df_ref, dg_ref, loss_ref, t_buf, sems):
        i = pl.program_id(0)

        @pl.when(i == 0)
        def _():
            loss_ref[...] = jnp.zeros_like(loss_ref)
            dg_ref[...] = jnp.zeros_like(dg_ref)

        target = _read_token_rows(t_hbm, t_buf, sems, i, t // tm)
        row = i * tm + lax.broadcasted_iota(jnp.int32, (tm, 1), 0)
        err = jnp.where(row >= N_FRONT, h_ref[...] - target[...], 0.0)
        dy = err * (1.0 / d)
        dh_ref[...] = dy
        per_row = jnp.sum(err * err, axis=1, keepdims=True) * (1.0 / d)
        loss_ref[...] += 0.5 * jnp.sum(per_row, axis=0, keepdims=True)
        dx, dg = _rms_bwd(f_ref[...], g_ref[...], dy)
        df_ref[...] = (alpha * dx).astype(BF16)
        dg_ref[...] += alpha * dg

    row = pl.BlockSpec((tm, d), lambda i: (i, 0))
    vec = pl.BlockSpec((1, d), lambda i: (0, 0))
    return pl.pallas_call(
        body, name="loss_and_post_norm_bwd",
        out_shape=[jax.ShapeDtypeStruct((t, d), F32), jax.ShapeDtypeStruct((t, d), BF16),
                   jax.ShapeDtypeStruct((1, d), F32), jax.ShapeDtypeStruct((1, 128), F32)],
        grid=(t // tm,),
        in_specs=[row, ANY, row, vec],
        out_specs=[row, row, vec, pl.BlockSpec((1, 128), lambda i: (0, 0))],
        scratch_shapes=[pltpu.VMEM((2, tm, d), F32), pltpu.SemaphoreType.DMA((2,))],
        compiler_params=_params(("arbitrary",)),
    )(h, target, f, g_post)


def _ffn_bwd_mid(name, df, w_out, ab):
    t, d = df.shape
    cw = ab.shape[1] // 4
    tm = ROW_TILE

    def body(df_ref, w_ref, ab_ref, o_ref):
        ds = _dot_nt(df_ref[...], w_ref[...])
        a = ab_ref[:, :cw].astype(F32)
        b = ab_ref[:, cw:].astype(F32)
        sg = _sigmoid(a)
        o_ref[:, :cw] = (ds * b * (sg * (1.0 + a * (1.0 - sg)))).astype(BF16)
        o_ref[:, cw:] = (ds * (a * sg)).astype(BF16)

    return pl.pallas_call(
        body, name=name, out_shape=jax.ShapeDtypeStruct((t, 4 * cw), BF16),
        grid=(2, t // tm),
        in_specs=[pl.BlockSpec((tm, d), lambda j, i: (i, 0)), pl.BlockSpec((cw, d), lambda j, i: (j, 0)),
                  pl.BlockSpec((tm, 2 * cw), lambda j, i: (i, j))],
        out_specs=pl.BlockSpec((tm, 2 * cw), lambda j, i: (i, j)),
        compiler_params=_params(("parallel", "parallel"), VMEM_BIG),
    )(df, w_out, ab)


def _next_post_norm_bwd(dh, post_refs, alpha, first):
    x_ref, g_ref, dx_ref, dg_ref = post_refs

    @pl.when(first)
    def _():
        dg_ref[...] = jnp.zeros_like(dg_ref)

    dx, dg = _rms_bwd(x_ref[...], g_ref[...], dh)
    dx_ref[...] = (alpha * dx).astype(BF16)
    dg_ref[...] += alpha * dg


def _mm_nt_norm_bwd(name, dy, w, h, g, dh_in, post=None):
    t, kdim = dy.shape
    d = h.shape[1]
    tm = ROW_TILE // 2

    def body(dy_ref, w_ref, h_ref, g_ref, dhi_ref, *rest):
        dh_ref, dg_ref = rest[-4:-2] if post else rest
        first = pl.program_id(0) == 0

        @pl.when(first)
        def _():
            dg_ref[...] = jnp.zeros_like(dg_ref)

        cw = w_ref.shape[2]
        dn = _dot_nt(dy_ref[:, 0:cw], w_ref[_slot_of(0)])
        for k in range(1, 4):
            dn += _dot_nt(dy_ref[:, k * cw:(k + 1) * cw], w_ref[_slot_of(k)])
        dx, dg = _rms_bwd(h_ref[...], g_ref[...], dn)
        dh = dhi_ref[...] + dx
        dh_ref[...] = dh
        dg_ref[...] += dg
        if post:
            _next_post_norm_bwd(dh, rest[0:2] + rest[-2:], post[2], first)

    row = pl.BlockSpec((tm, d), lambda i: (i, 0))
    vec = pl.BlockSpec((1, d), lambda i: (0, 0))
    out_shape = [jax.ShapeDtypeStruct((t, d), F32), jax.ShapeDtypeStruct((1, d), F32)]
    if post:
        out_shape += [jax.ShapeDtypeStruct((t, d), BF16), jax.ShapeDtypeStruct((1, d), F32)]
    return pl.pallas_call(
        body, name=name, out_shape=out_shape, grid=(t // tm,),
        in_specs=[pl.BlockSpec((tm, kdim), lambda i: (i, 0)), pl.BlockSpec(w.shape, lambda i: (0,) * w.ndim),
                  row, vec, row] + ([row, vec] if post else []),
        out_specs=[row, vec] + ([row, vec] if post else []),
        compiler_params=_params(("arbitrary",), VMEM_BIG),
    )(dy, w, h, g, dh_in, *(post[:2] if post else ()))


def _gate_bwd(dfq, dfk, z, b_pad, f_col):
    t = z.shape[0]
    tm = ROW_TILE
    nt = t // tm

    def body(dq_ref, dk_ref, z_ref, b_ref, dz_ref, db_ref, carry_ref):
        i = pl.program_id(0)

        @pl.when(i == 0)
        def _():
            carry_ref[...] = jnp.zeros_like(carry_ref)
            db_ref[...] = jnp.zeros_like(db_ref)

        pick = (lax.broadcasted_iota(jnp.int32, (ATTN_W, 128), 0)
                == HEAD_DIM * lax.broadcasted_iota(jnp.int32, (ATTN_W, 128), 1)).astype(F32)
        d_heads = jnp.dot(dq_ref[...] - dk_ref[...], pick, preferred_element_type=F32,
                          precision=lax.Precision.HIGHEST)
        tri = (lax.broadcasted_iota(jnp.int32, (tm, tm), 0) <= lax.broadcasted_iota(jnp.int32, (tm, tm), 1))
        tail = jnp.dot(tri.astype(F32), d_heads, preferred_element_type=F32, precision=lax.Precision.HIGHEST)
        tail = tail + carry_ref[0:1, :]
        carry_ref[...] = jnp.broadcast_to(tail[0:1, :], carry_ref.shape)
        row = (nt - 1 - i) * tm + lax.broadcasted_iota(jnp.int32, (tm, 1), 0)
        dlogit = jnp.where(row >= ROW_PAD, tail * _sigmoid(-(z_ref[...] + b_ref[...])), 0.0)
        dz_ref[...] = jnp.zeros_like(dz_ref)
        dz_ref[:, 0:128] = dlogit.astype(BF16)
        db_ref[...] += jnp.sum(dlogit, axis=0, keepdims=True)

    rev = lambda i: (nt - 1 - i, 0)
    return pl.pallas_call(
        body, name="forget_gate_bwd",
        out_shape=[jax.ShapeDtypeStruct((t, F_PAD), BF16), jax.ShapeDtypeStruct((1, 128), F32)],
        grid=(nt,),
        in_specs=[pl.BlockSpec((tm, ATTN_W), rev), pl.BlockSpec((tm, ATTN_W), rev),
                  pl.BlockSpec((tm, 128), lambda i: (nt - 1 - i, f_col // 128)),
                  pl.BlockSpec((1, 128), lambda i: (0, 0))],
        out_specs=[pl.BlockSpec((tm, F_PAD), rev), pl.BlockSpec((1, 128), lambda i: (0, 0))],
        scratch_shapes=[pltpu.VMEM((8, 128), F32)],
        compiler_params=_params(("arbitrary",)),
    )(dfq, dfk, z, b_pad)


def _ffn_fwd(tag, n, w_in4, w_out, h, g_post, g_next):
    ab, s, s_t = _ffn_in(f"{tag}_in_fwd", n, w_in4)
    outs = _mm_resid_norm(f"{tag}_out_fwd", s, w_out, h, g_post, 0.5, g_next)
    return ab, s_t, outs


def _ffn_bwd_weights(tag, df, ab, s_t, n_t, w_in4, w_out):
    d, cw = w_in4.shape[1], w_in4.shape[2]
    t = df.shape[0]
    dw_out = _weight_grad(f"{tag}_dw_out", s_t, df, d, out_rows=cw // 2)
    dab = _ffn_bwd_mid(f"{tag}_mid_bwd", df, w_out, ab)
    bk = _k_tile(t)
    dw_in = _matmul(
        f"{tag}_dw_in", n_t, dab, jax.ShapeDtypeStruct((4, d, cw), F32), (1, 4, t // bk),
        pl.BlockSpec((d, bk), lambda a, b, k: (0, k)), pl.BlockSpec((bk, cw), lambda a, b, k: (k, b)),
        pl.BlockSpec((None, d, cw), lambda a, b, k: (_slot_of(b), 0, 0)), vmem=VMEM_BIG)
    return dab, dw_in, dw_out


LOSS_ROW = 12


def _pack_small(meta, conv, gains, b_forget, loss=None):
    d = gains[0].shape[1]
    rows = [meta.reshape(4, d), jnp.pad(conv.reshape(1, 3 * 128), ((0, 0), (0, d - 3 * 128)))]
    rows += list(gains) + [jnp.pad(b_forget, ((0, 0), (0, d - HEADS)))]
    last = jnp.zeros((4, d), F32)
    if loss is not None:
        last = jnp.pad(loss.reshape(1, 1), ((0, 3), (0, d - 1)))
    return jnp.concatenate(rows + [last], axis=0)


def _unpack_small(block):
    d = block.shape[1]
    meta = block[0:4].reshape(N_META, d // 4)
    conv = block[4, :3 * 128].reshape(1, 3, 128)
    gains = [block[5 + i:6 + i] for i in range(6)]
    return meta, conv, gains, block[11:12, :HEADS]


def kernel(x, meta_tokens, w_in, b_forget, conv_w, w_attn_branch, w_conv_branch, w_out, g_ffn1_pre, g_ffn1_post, w_ffn1_in, w_ffn1_out, g_mix_pre, g_mix_post, g_ffn2_pre, g_ffn2_post, w_ffn2_in, w_ffn2_out, loss_target, m_meta_tokens, m_w_in, m_b_forget, m_conv_w, m_w_attn_branch, m_w_conv_branch, m_w_out, m_g_ffn1_pre, m_g_ffn1_post, m_w_ffn1_in, m_w_ffn1_out, m_g_mix_pre, m_g_mix_post, m_g_ffn2_pre, m_g_ffn2_post, m_w_ffn2_in, m_w_ffn2_out, v_meta_tokens, v_w_in, v_b_forget, v_conv_w, v_w_attn_branch, v_w_conv_branch, v_w_out, v_g_ffn1_pre, v_g_ffn1_post, v_w_ffn1_in, v_w_ffn1_out, v_g_mix_pre, v_g_mix_post, v_g_ffn2_pre, v_g_ffn2_post, v_w_ffn2_in, v_w_ffn2_out):
    seq, d = x.shape[1], x.shape[2]
    t = seq + N_FRONT
    f_lo = 3 * ATTN_W
    c_arr = lax.axis_index("c").astype(jnp.int32).reshape(1)

    cs = w_in.shape[2]
    cs_pad = -(-cs // 64) * 64

    def w_in_rows(a):
        return jnp.pad(jnp.transpose(a[0]), ((0, cs_pad - cs), (0, 0)))

    big = [w_in_rows(w_in), w_attn_branch[0], w_conv_branch[0], w_out[0], w_ffn1_in[0], w_ffn1_out[0], w_ffn2_in[0],
           w_ffn2_out[0]]
    small_gather = jnp.concatenate(
        [meta_tokens.reshape(4, d), jnp.pad(conv_w.reshape(1, 3 * 128), ((0, 0), (0, d - 3 * 128))),
         jnp.zeros((11, d), F32)], axis=0)
    w_f1_in4, small4 = _all_gather([big[4].astype(BF16), small_gather])
    (second, rest), small4 = lax.optimization_barrier(
        (([big[5].astype(BF16)], [big[i].astype(BF16) for i in (0, 1, 2, 3, 6, 7)]), small4))
    second_gathered = _all_gather_async("all_gather_ffn1_out", second, 5)
    rest_gathered = _all_gather_async("all_gather_rest", rest, 1)
    meta_full = jnp.transpose(small4[:, 0:4].reshape(4, N_META, d // 4), (1, 0, 2)).reshape(N_META, d)
    conv_full = jnp.transpose(small4[:, 4, :3 * 128].reshape(4, 3, 128), (1, 0, 2)).reshape(3, CONV_W)
    conv_pad = jnp.pad(conv_full, ((0, 5), (0, 0)))
    b_pad = jnp.pad(b_forget, ((0, 0), (0, 128 - HEADS)))

    h0, n1, n1_t = _embed_norm(x[0], meta_full, g_ffn1_pre)
    ab1, s1, s1_t = _ffn_in("ffn1_in_fwd", n1, w_f1_in4)
    w_f1_out = second_gathered(s1, [0])[0].reshape(-1, d)
    f1, h1, u, u_t = _mm_resid_norm("ffn1_out_fwd", s1, w_f1_out, h0, g_ffn1_post, 0.5, g_mix_pre)

    w_in4, w_ab4, w_cb4, w_out4, w_f2_in4, w_f2_out4 = rest_gathered(u, range(6))
    w_in_t = w_in4[:, :cs].reshape(4 * cs, d)
    g_lo = f_lo + HEADS + 3 * CONV_W
    w_in_pad = jnp.concatenate(
        [w_in_t[:f_lo], w_in_t[g_lo:], w_in_t[f_lo + HEADS:g_lo], w_in_t[f_lo:f_lo + HEADS],
         jnp.zeros((F_PAD - HEADS, d), BF16)], axis=0)
    w_ab = jnp.transpose(w_ab4, (1, 0, 2)).reshape(ATTN_W, d)
    w_cb = jnp.transpose(w_cb4, (1, 0, 2)).reshape(CONV_W, d)
    w_out_full = w_out4.reshape(d, d)
    w_f2_out = w_f2_out4.reshape(-1, d)
    qkv, z = _in_proj(u, w_in_pad)
    f_col = z.shape[1] - F_PAD
    f_cum = _gate_prep(z, b_pad, f_col)
    f_heads = f_cum[:, :HEADS]
    o, lse = _attn_fwd(qkv, *_attn_bias_operands(f_heads))
    g, g_t = _conv_gate(z, conv_pad)
    mp, mp_t, o_t = _branch_mix(z, o, g, w_ab, w_cb, d)
    mixed, h2, n2, n2_t = _mm_resid_norm("mix_out_fwd", mp, w_out_full, h1, g_mix_post, 1.0, g_ffn2_pre)
    ab2, s2_t, (f2, h3) = _ffn_fwd("ffn2", n2, w_f2_in4, w_f2_out, h2, g_ffn2_post, None)
    dh3, df2, dg_f2_post, loss_part = _loss_norm_bwd(h3, loss_target[0], f2, g_ffn2_post, 0.5)

    reduced = {}

    def reduce_scatter(label, tags, slots, sequencer_id, hold=None, got=None, after=None):
        if got is None:
            got = _pair_send_halves(f"grad_pair_exchange_{label}", slots)
        else:
            got, _ = lax.optimization_barrier((got, after))
        sums = [_pair_add(tag, s, a, c_arr, F32 if tag == "small" else BF16) for tag, s, a in zip(tags, slots, got)]
        sums, hold = lax.optimization_barrier((sums, hold))
        if sequencer_id is None:
            arrived = _chip_scatter(f"grad_chip_scatter_{label}", sums)
        else:
            arrived = _chip_scatter_async(f"grad_chip_scatter_{label}", sums, sequencer_id)
        mine = [_chip_add(tag, a) for tag, a in zip(tags, arrived)]
        reduced.update(zip(tags, zip(mine, _pair_swap(f"grad_pair_swap_{label}", mine))))
        return hold

    dab2, dw_f2_in, dw_f2_out = _ffn_bwd_weights("ffn2", df2, ab2, s2_t, n2_t, w_f2_in4, w_f2_out)
    ffn2_slots = [dw_f2_in, dw_f2_out.reshape(4, -1, d)]
    ffn2_got = _pair_send_halves_async("grad_pair_exchange_ffn2", ffn2_slots, 6)
    dh2, dg_f2_pre, dmixed, dg_mix_post = _mm_nt_norm_bwd(
        "ffn2_in_bwd", dab2, w_f2_in4, h2, g_ffn2_pre, dh3, post=(mixed, g_mix_post, 1.0))
    reduce_scatter("ffn2", ["w_ffn2_in", "w_ffn2_out"], ffn2_slots, 2, got=ffn2_got, after=dh2)
    dw_out = _weight_grad("mix_dw_out", mp_t, dmixed, d)
    dya, dyc, dgates, do, dgconv = _branch_bwd(z, o, g, dmixed, w_out_full, w_ab, w_cb, d)
    dw_ab = _weight_grad("mix_dw_attn_branch", o_t, dya, d)
    dw_cb = _weight_grad("mix_dw_conv_branch", g_t, dyc, d)
    dz_conv, dconv_w = _conv_bwd(z, dgconv, conv_pad)
    front = lax.broadcasted_iota(jnp.int32, (t, 1), 0) < ROW_PAD
    lse_heads = jnp.where(front, 1e9, lse[:, ::HEAD_DIM])
    dq, dk, dv, dfk, dfq = _attn_bwd(qkv, *_attn_bias_operands(f_heads, lse_heads), o, do)
    dz_f, db_forget = _gate_bwd(dfq, dfk, z, b_pad, f_col)
    dz_pieces = {"q": dq, "k": dk, "v": dv, "gates": dgates, "conv": dz_conv, "f": dz_f}
    dh1, dg_mix_pre, df1, dg_f1_post = _mix_in_bwd(
        list(dz_pieces.values()), w_in_pad, h1, g_mix_pre, dh2, (f1, g_ffn1_post, 0.5))
    dw_t = {name: _weight_grad_t(f"mix_dw_in_{name}", u_t, piece) for name, piece in dz_pieces.items()}
    dw_in_t = jnp.concatenate(
        [dw_t["q"], dw_t["k"], dw_t["v"], dw_t["f"][:HEADS], dw_t["conv"], dw_t["gates"]], axis=0)
    mix_slots = [jnp.pad(dw_in_t.reshape(4, cs, d), ((0, 0), (0, cs_pad - cs), (0, 0))),
                 jnp.transpose(dw_ab.reshape(ATTN_W, 4, d // 4), (1, 0, 2)),
                 jnp.transpose(dw_cb.reshape(CONV_W, 4, d // 4), (1, 0, 2)),
                 dw_out.reshape(4, d // 4, d)]
    mix_got = _pair_send_halves_async("grad_pair_exchange_mix", mix_slots, 7)
    dab1, dw_f1_in, dw_f1_out = _ffn_bwd_weights("ffn1", df1, ab1, s1_t, n1_t, w_f1_in4, w_f1_out)
    reduce_scatter("mix", ["w_in", "w_attn_branch", "w_conv_branch", "w_out"], mix_slots, 3, got=mix_got,
                   after=dw_f1_out)
    dab1 = reduce_scatter("ffn1", ["w_ffn1_in", "w_ffn1_out"], [dw_f1_in, dw_f1_out.reshape(4, -1, d)], 4, dab1)
    dh0, dg_f1_pre = _mm_nt_norm_bwd("ffn1_in_bwd", dab1, w_f1_in4, h0, g_ffn1_pre, dh1)
    grad_x = dh0[N_FRONT:][None]
    dmeta = dh0[ROW_PAD:N_FRONT]
    small_grad = jnp.stack([
        _pack_small(dmeta[:, j * (d // 4):(j + 1) * (d // 4)], dconv_w[:3, j * 128:(j + 1) * 128],
                    [dg_f1_pre, dg_f1_post, dg_mix_pre, dg_mix_post, dg_f2_pre, dg_f2_post], db_forget[:, :HEADS],
                    loss_part[0, 0])
        for j in range(4)])
    reduce_scatter("small", ["small"], [small_grad], None)
    tags =["w_in", "w_attn_branch", "w_conv_branch", "w_out", "w_ffn1_in", "w_ffn1_out", "w_ffn2_in", "w_ffn2_out", "small"]
    halves = [reduced[tag][0] for tag in tags]
    others = [reduced[tag][1] for tag in tags]

    small = [g_ffn1_pre, g_ffn1_post, g_mix_pre, g_mix_post, g_ffn2_pre, g_ffn2_post]
    small_m = [m_g_ffn1_pre, m_g_ffn1_post, m_g_mix_pre, m_g_mix_post, m_g_ffn2_pre, m_g_ffn2_post]
    small_v = [v_g_ffn1_pre, v_g_ffn1_post, v_g_mix_pre, v_g_mix_post, v_g_ffn2_pre, v_g_ffn2_post]
    ws = big + [_pack_small(meta_tokens, conv_w[0], small, b_forget)]
    ms = [w_in_rows(m_w_in), m_w_attn_branch[0], m_w_conv_branch[0], m_w_out[0], m_w_ffn1_in[0], m_w_ffn1_out[0],
          m_w_ffn2_in[0], m_w_ffn2_out[0], _pack_small(m_meta_tokens, m_conv_w[0], small_m, m_b_forget)]
    vs = [w_in_rows(v_w_in), v_w_attn_branch[0], v_w_conv_branch[0], v_w_out[0], v_w_ffn1_in[0], v_w_ffn1_out[0],
          v_w_ffn2_in[0], v_w_ffn2_out[0], _pack_small(v_meta_tokens, v_conv_w[0], small_v, v_b_forget)]
    updates = [_adamw(tag, w, a, b, m, v, c_arr) for tag, w, a, b, m, v in zip(tags, ws, halves, others, ms, vs)]

    def leaves(big_vals, small_block):
        meta, conv, gains, bf = _unpack_small(small_block)
        w_in_t_, w_ab_, w_cb_, w_out_, f1_in, f1_out, f2_in, f2_out = [b[None] for b in big_vals]
        w_in_ = jnp.transpose(w_in_t_[:, :cs], (0, 2, 1))
        return [meta, w_in_, bf, conv, w_ab_, w_cb_, w_out_, gains[0], gains[1], f1_in, f1_out,
                gains[2], gains[3], gains[4], gains[5], f2_in, f2_out]

    out_g, out_d, out_m, out_v = [leaves([u_[k] for u_ in updates[:8]], updates[8][k]) for k in range(4)]
    loss = updates[8][0][LOSS_ROW, 0]
    return (loss, grad_x, *out_g, *out_d, *out_m, *out_v)
```

```python
import functools

import jax
import jax.numpy as jnp
from jax import lax
from jax.experimental import pallas as pl
from jax.experimental.pallas import tpu as pltpu
from jax.experimental.pallas import tpu_sc as plsc

N_META = 16
ROW_PAD = 112
N_FRONT = ROW_PAD + N_META
HEADS = 8
HEAD_DIM = 64
ATTN_W = HEADS * HEAD_DIM
CONV_W = 512
NORM_EPS = 1e-6
ROW_TILE = 640
F_PAD = 128
ATTN_Q_GROUP = 2
ATTN_KV_GROUP = 4
NEG = -1e30
ADAM_LR = 0.001
ADAM_B1 = 0.9
ADAM_B2 = 0.999
ADAM_EPS = 1e-08
ADAM_WD = 0.01
ADAM_STEP = 10
VMEM_BIG = 56 * 1024 * 1024
MESH = pl.DeviceIdType.MESH
ANY = pl.BlockSpec(memory_space=pl.ANY)
F32 = jnp.float32
BF16 = jnp.bfloat16


def _params(sem, vmem=None):
    return pltpu.CompilerParams(dimension_semantics=sem, vmem_limit_bytes=vmem)


def _sigmoid(x):
    return 1.0 / (1.0 + jnp.exp(-x))


def _rstd(x):
    return lax.rsqrt(jnp.mean(x * x, axis=-1, keepdims=True) + NORM_EPS)


def _rms_bwd(x, g, dy):
    r = _rstd(x)
    xr = x * r
    gdy = g * dy
    dx = r * (gdy - xr * jnp.mean(xr * gdy, axis=-1, keepdims=True))
    return dx, jnp.sum(dy * xr, axis=0, keepdims=True)


def _dot(a, b):
    return jnp.dot(a, b, preferred_element_type=F32)


def _dot_nt(a, b):
    return lax.dot_general(a, b, (((1,), (1,)), ((), ())), preferred_element_type=F32)


def _k_tile(t):
    return 1664 if t % 1664 == 0 else ROW_TILE


def _place():
    x, y, c = lax.axis_index("x"), lax.axis_index("y"), lax.axis_index("c")
    chips = [(1 - x, y), (x, 1 - y), (1 - x, 1 - y)]
    return x, y, c, chips


def _all_gather(shards):
    n = len(shards)
    split = [s.reshape(2, s.shape[0] // 2, s.shape[1]) for s in shards]

    def body(*refs):
        ins, outs = refs[:n], refs[n:2 * n]
        send_sems, recv_sems = refs[2 * n:]
        x, y, c, chips = _place()
        me = 2 * x + y
        sibling = (x, y, 1 - c)

        def remote(i, k, slot, part, to, src=None):
            dst = outs[i].at[slot, part]
            return pltpu.make_async_remote_copy(
                src_ref=dst if src is None else src, dst_ref=dst,
                send_sem=send_sems.at[i, k], recv_sem=recv_sems.at[i, k],
                device_id=to, device_id_type=MESH)

        started = []
        for i in range(n):
            for k, (cx, cy) in enumerate(chips):
                cp = remote(i, k, me, c, (cx, cy, c), src=ins[i].at[c])
                cp.start()
                started.append(cp)
        for i in range(n):
            for k, (cx, cy) in enumerate(chips):
                remote(i, k, 2 * cx + cy, c, (x, y, c)).wait_recv()
                cp = remote(i, 3 + k, 2 * cx + cy, c, sibling)
                cp.start()
                started.append(cp)
        for i in range(n):
            for k, (cx, cy) in enumerate(chips):
                remote(i, 3 + k, 2 * cx + cy, 1 - c, (x, y, c)).wait_recv()
        for cp in started:
            cp.wait_send()

    outs = pl.pallas_call(
        body, name="all_gather_weights",
        out_shape=[jax.ShapeDtypeStruct((4,) + s.shape, s.dtype) for s in split],
        in_specs=[ANY] * n, out_specs=[ANY] * n,
        scratch_shapes=[pltpu.SemaphoreType.DMA((n, 6)), pltpu.SemaphoreType.DMA((n, 6))],
    )(*split)
    me =2 * lax.axis_index("x") + lax.axis_index("y")
    outs = [lax.dynamic_update_slice(o, s[None], (me, 0, 0, 0)) for o, s in zip(outs, split)]
    return [o.reshape((4,) + s.shape) for o, s in zip(outs, shards)]


def _all_gather_async(name, shards, collective_id):
    n = len(shards)
    split = [s.reshape(2, s.shape[0] // 2, s.shape[1]) for s in shards]
    ins = [jax.new_ref(s, memory_space=pltpu.MemorySpace.HBM) for s in split]
    outs = [jax.empty_ref(jax.ShapeDtypeStruct((4,) + s.shape, s.dtype), memory_space=pltpu.MemorySpace.HBM)
            for s in split]

    @pl.kernel(mesh=plsc.ScalarSubcoreMesh(axis_name="sequencer", num_cores=1), name=name,
               scratch_types=(pltpu.SemaphoreType.DMA((n, 6)), pltpu.SemaphoreType.DMA((n, 6))),
               compiler_params=pltpu.CompilerParams(collective_id=collective_id))
    def launch(send_sems, recv_sems):
        x, y, c, chips = _place()
        me = 2 * x + y
        sibling = (x, y, 1 - c)
        barrier = pltpu.get_barrier_semaphore()
        for peer in [(cx, cy, c) for cx, cy in chips] + [sibling]:
            pl.semaphore_signal(barrier, inc=1, device_id=peer, device_id_type=MESH)
        pl.semaphore_wait(barrier, 4)

        def remote(i, k, slot, part, to, src=None):
            dst = outs[i].at[slot, part]
            return pltpu.make_async_remote_copy(
                src_ref=dst if src is None else src, dst_ref=dst,
                send_sem=send_sems.at[i, k], recv_sem=recv_sems.at[i, k],
                device_id=to, device_id_type=MESH)

        started = []
        for i in range(n):
            for k, (cx, cy) in enumerate(chips):
                cp = remote(i, k, me, c, (cx, cy, c), src=ins[i].at[c])
                cp.start()
                started.append(cp)
        for i in range(n):
            for k, (cx, cy) in enumerate(chips):
                remote(i, k, 2 * cx + cy, c, (x, y, c)).wait_recv()
                cp = remote(i, 3 + k, 2 * cx + cy, c, sibling)
                cp.start()
                started.append(cp)
        for i in range(n):
            for k, (cx, cy) in enumerate(chips):
                remote(i, 3 + k, 2 * cx + cy, 1 - c, (x, y, c)).wait_recv()
        for cp in started:
            cp.wait_send()

    launch()
    raw = [o[...] for o in outs]

    def finish(after, which):
        arrived, _ = lax.optimization_barrier(([raw[i] for i in which], after))
        me = 2 * lax.axis_index("x") + lax.axis_index("y")
        gathered = [lax.dynamic_update_slice(a, split[i][None], (me, 0, 0, 0)) for a, i in zip(arrived, which)]
        return [g.reshape((4,) + shards[i].shape) for g, i in zip(gathered, which)]

    return finish


def _pair_send_halves(name, grads):
    n = len(grads)

    def body(*refs):
        ins, outs = refs[:n], refs[n:2 * n]
        send_sems, recv_sems = refs[2 * n:]
        x, y, c, _ = _place()
        cps = []
        for i in range(n):
            half = ins[i].shape[1] // 2
            cp = pltpu.make_async_remote_copy(
                src_ref=ins[i].at[:, pl.ds((1 - c) * half, half)], dst_ref=outs[i],
                send_sem=send_sems.at[i], recv_sem=recv_sems.at[i],
                device_id=(x, y, 1 - c), device_id_type=MESH)
            cp.start()
            cps.append(cp)
        for cp in cps:
            cp.wait()

    return pl.pallas_call(
        body, name=name,
        out_shape=[jax.ShapeDtypeStruct((4, g.shape[1] // 2, g.shape[2]), g.dtype) for g in grads],
        in_specs=[ANY] * n, out_specs=[ANY] * n,
        scratch_shapes=[pltpu.SemaphoreType.DMA((n,)), pltpu.SemaphoreType.DMA((n,))],
    )(*grads)


def _pair_send_halves_async(name, grads, collective_id):
    n = len(grads)
    ins = [jax.new_ref(g, memory_space=pltpu.MemorySpace.HBM) for g in grads]
    outs = [jax.empty_ref(jax.ShapeDtypeStruct((4, g.shape[1] // 2, g.shape[2]), g.dtype),
                          memory_space=pltpu.MemorySpace.HBM) for g in grads]

    @pl.kernel(mesh=plsc.ScalarSubcoreMesh(axis_name="sequencer", num_cores=1), name=name,
               scratch_types=(pltpu.SemaphoreType.DMA((n,)), pltpu.SemaphoreType.DMA((n,))),
               compiler_params=pltpu.CompilerParams(collective_id=collective_id))
    def launch(send_sems, recv_sems):
        x, y, c, _ = _place()
        barrier = pltpu.get_barrier_semaphore()
        pl.semaphore_signal(barrier, inc=1, device_id=(x, y, 1 - c), device_id_type=MESH)
        pl.semaphore_wait(barrier, 1)
        cps = []
        for i in range(n):
            half = ins[i].shape[1] // 2
            cp = pltpu.make_async_remote_copy(
                src_ref=ins[i].at[:, pl.ds((1 - c) * half, half)], dst_ref=outs[i],
                send_sem=send_sems.at[i], recv_sem=recv_sems.at[i],
                device_id=(x, y, 1 - c), device_id_type=MESH)
            cp.start()
            cps.append(cp)
        for cp in cps:
            cp.wait()

    launch()
    return [o[...] for o in outs]


def _chip_scatter(name, parts):
    n = len(parts)

    def body(*refs):
        _scatter_copies(refs[:n], refs[n:2 * n], *refs[2 * n:])

    arrived = pl.pallas_call(
        body, name=name,
        out_shape=[jax.ShapeDtypeStruct(p.shape, p.dtype) for p in parts],
        in_specs=[ANY] * n, out_specs=[ANY] * n,
        scratch_shapes=[pltpu.SemaphoreType.DMA((n, 3)), pltpu.SemaphoreType.DMA((n, 3))],
    )(*parts)
    return _own_slots(parts, arrived)


def _scatter_copies(ins, outs, send_sems, recv_sems):
    x, y, c, chips = _place()
    me = 2 * x + y
    sends = []
    for i in range(len(ins)):
        for k, (cx, cy) in enumerate(chips):
            cp = pltpu.make_async_remote_copy(
                src_ref=ins[i].at[2 * cx + cy], dst_ref=outs[i].at[me],
                send_sem=send_sems.at[i, k], recv_sem=recv_sems.at[i, k],
                device_id=(cx, cy, c), device_id_type=MESH)
            cp.start()
            sends.append(cp)
    for i in range(len(ins)):
        for k, (cx, cy) in enumerate(chips):
            got = outs[i].at[2 * cx + cy]
            pltpu.make_async_remote_copy(
                src_ref=got, dst_ref=got, send_sem=send_sems.at[i, k], recv_sem=recv_sems.at[i, k],
                device_id=(x, y, c), device_id_type=MESH).wait_recv()
    for cp in sends:
        cp.wait_send()


def _own_slots(parts, arrived):
    me = 2 * lax.axis_index("x") + lax.axis_index("y")
    return [lax.dynamic_update_slice(a, lax.dynamic_slice_in_dim(p, me, 1, axis=0), (me, 0, 0))
            for p, a in zip(parts, arrived)]


def _chip_scatter_async(name, parts, collective_id):
    n = len(parts)
    ins = [jax.new_ref(p, memory_space=pltpu.MemorySpace.HBM) for p in parts]
    outs = [jax.empty_ref(jax.ShapeDtypeStruct(p.shape, p.dtype), memory_space=pltpu.MemorySpace.HBM) for p in parts]

    @pl.kernel(mesh=plsc.ScalarSubcoreMesh(axis_name="sequencer", num_cores=1), name=name,
               scratch_types=(pltpu.SemaphoreType.DMA((n, 3)), pltpu.SemaphoreType.DMA((n, 3))),
               compiler_params=pltpu.CompilerParams(collective_id=collective_id))
    def launch(send_sems, recv_sems):
        x, y, c, chips = _place()
        barrier = pltpu.get_barrier_semaphore()
        for cx, cy in chips:
            pl.semaphore_signal(barrier, inc=1, device_id=(cx, cy, c), device_id_type=MESH)
        pl.semaphore_wait(barrier, 3)
        _scatter_copies(ins, outs, send_sems, recv_sems)

    launch()
    return _own_slots(parts, [o[...] for o in outs])


def _pair_swap(name, halves):
    n = len(halves)

    def body(*refs):
        ins, outs = refs[:n], refs[n:2 * n]
        send_sems, recv_sems = refs[2 * n:]
        x, y, c, _ = _place()
        cps = []
        for i in range(n):
            cp = pltpu.make_async_remote_copy(
                src_ref=ins[i], dst_ref=outs[i], send_sem=send_sems.at[i], recv_sem=recv_sems.at[i],
                device_id=(x, y, 1 - c), device_id_type=MESH)
            cp.start()
            cps.append(cp)
        for cp in cps:
            cp.wait()

    return pl.pallas_call(
        body, name=name,
        out_shape=[jax.ShapeDtypeStruct(h.shape, h.dtype) for h in halves],
        in_specs=[ANY] * n, out_specs=[ANY] * n,
        scratch_shapes=[pltpu.SemaphoreType.DMA((n,)), pltpu.SemaphoreType.DMA((n,))],
    )(*halves)


def _row_block(rows, cols, n_bufs, budget=20 * 1024 * 1024):
    best = min(rows, 16)
    for b in range(16, rows + 1, 16):
        if rows % b == 0 and 2 * n_bufs * b * cols * 4 <= budget:
            best = b
    return best


def _pair_add(tag, grad, got, c_arr, out_dtype):
    _, rows, cols = grad.shape
    half = rows // 2
    bh = _row_block(half, cols, 3)
    nb = half // bh

    def body(c_ref, g_ref, a_ref, o_ref):
        o_ref[...] = (g_ref[...] + a_ref[...]).astype(out_dtype)

    return pl.pallas_call(
        body, name=f"pair_add_{tag}",
        out_shape=jax.ShapeDtypeStruct((4, half, cols), out_dtype),
        grid_spec=pltpu.PrefetchScalarGridSpec(
            num_scalar_prefetch=1, grid=(4, nb),
            in_specs=[pl.BlockSpec((None, bh, cols), lambda j, r, c: (j, c[0] * nb + r, 0)),
                      pl.BlockSpec((None, bh, cols), lambda j, r, c: (j, r, 0))],
            out_specs=pl.BlockSpec((None, bh, cols), lambda j, r, c: (j, r, 0))),
        compiler_params=_params(("parallel", "parallel")),
    )(c_arr, grad, got)


def _chip_add(tag, parts):
    _, half, cols = parts.shape
    bh = _row_block(half, cols, 5)

    def body(p_ref, o_ref):
        a, b, c, d = [p_ref[j].astype(F32) for j in range(4)]
        o_ref[...] = ((a + b) + c) + d

    return pl.pallas_call(
        body, name=f"chip_add_{tag}",
        out_shape=jax.ShapeDtypeStruct((half, cols), F32),
        grid=(half // bh,),
        in_specs=[pl.BlockSpec((4, bh, cols), lambda r: (0, r, 0))],
        out_specs=pl.BlockSpec((bh, cols), lambda r: (r, 0)),
        compiler_params=_params(("parallel",)),
    )(parts)


def _adamw(tag, w, mine, theirs, m, v, c_arr):
    rows, cols = w.shape
    half = rows // 2
    br = _row_block(half, cols, 9)
    nb = half // br

    def body(c_ref, w_ref, a_ref, b_ref, m_ref, v_ref, g_ref, d_ref, mo_ref, vo_ref):
        own = (pl.program_id(0) // nb) == c_ref[0]
        g = jnp.where(own, a_ref[...], b_ref[...])
        g_ref[...] = g
        m_new = ADAM_B1 * m_ref[...] + (1.0 - ADAM_B1) * g
        v_new = ADAM_B2 * v_ref[...] + (1.0 - ADAM_B2) * (g * g)
        m_hat = m_new / (1.0 - ADAM_B1 ** ADAM_STEP)
        v_hat = v_new / (1.0 - ADAM_B2 ** ADAM_STEP)
        d_ref[...] = -ADAM_LR * (m_hat / (jnp.sqrt(v_hat) + ADAM_EPS) + ADAM_WD * w_ref[...])
        mo_ref[...] = m_new
        vo_ref[...] = v_new

    spec = pl.BlockSpec((br, cols), lambda r, c: (r, 0))
    mine_spec = pl.BlockSpec((br, cols), lambda r, c: (jnp.clip(r - c[0] * nb, 0, nb - 1), 0))
    theirs_spec = pl.BlockSpec((br, cols), lambda r, c: (jnp.clip(r - (1 - c[0]) * nb, 0, nb - 1), 0))
    return pl.pallas_call(
        body, name=f"adamw_{tag}",
        out_shape=[jax.ShapeDtypeStruct((rows, cols), F32)] * 4,
        grid_spec=pltpu.PrefetchScalarGridSpec(
            num_scalar_prefetch=1, grid=(rows // br,),
            in_specs=[spec, mine_spec, theirs_spec, spec, spec], out_specs=[spec] * 4),
        compiler_params=_params(("arbitrary",)),
    )(c_arr, w, mine, theirs, m, v)


def _matmul(name, x, w, out_shape, grid, x_spec, w_spec, o_spec, *, nt=False, vmem=None):
    nk = grid[2]
    acc_shape = tuple(d for d in o_spec.block_shape if d is not None)

    def body(x_ref, w_ref, o_ref, acc_ref):
        k = pl.program_id(2)
        part = _dot_nt(x_ref[...], w_ref[...]) if nt else _dot(x_ref[...], w_ref[...])
        if nk == 1:
            o_ref[...] = part.astype(o_ref.dtype)
        else:
            @pl.when(k == 0)
            def _():
                acc_ref[...] = part

            @pl.when(k > 0)
            def _():
                acc_ref[...] += part

            @pl.when(k == nk - 1)
            def _():
                o_ref[...] = acc_ref[...].astype(o_ref.dtype)

    return pl.pallas_call(
        body, name=name, out_shape=out_shape, grid=grid,
        in_specs=[x_spec, w_spec], out_specs=o_spec,
        scratch_shapes=[pltpu.VMEM(acc_shape if nk > 1 else (8, 128), F32)],
        compiler_params=_params(("parallel", "parallel", "arbitrary"), vmem),
    )(x, w)


def _weight_grad(name, xt, dy, bn, out_rows=None):
    m, t = xt.shape
    n = dy.shape[1]
    bm = m if out_rows is None else out_rows
    bk = _k_tile(t)
    return _matmul(
        name, xt, dy, jax.ShapeDtypeStruct((m, n), F32), (m // bm, n // bn, t // bk),
        pl.BlockSpec((bm, bk), lambda a, b, k: (a, k)),
        pl.BlockSpec((bk, bn), lambda a, b, k: (k, b)),
        pl.BlockSpec((bm, bn), lambda a, b, k: (a, b)), vmem=VMEM_BIG)


def _weight_grad_t(name, xt, dy):
    m, t = xt.shape
    n = dy.shape[1]
    bn = min(n, 512)
    bk = _k_tile(t)
    nk = t // bk

    def body(x_ref, dy_ref, o_ref, acc_ref):
        k = pl.program_id(1)
        part = _dot(x_ref[...], dy_ref[...].astype(BF16))

        @pl.when(k == 0)
        def _():
            acc_ref[...] = part

        @pl.when(k > 0)
        def _():
            acc_ref[...] += part

        @pl.when(k == nk - 1)
        def _():
            o_ref[...] = acc_ref[...].T

    return pl.pallas_call(
        body, name=name, out_shape=jax.ShapeDtypeStruct((n, m), F32), grid=(n // bn, nk),
        in_specs=[pl.BlockSpec((m, bk), lambda b, k: (0, k)), pl.BlockSpec((bk, bn), lambda b, k: (k, b))],
        out_specs=pl.BlockSpec((bn, m), lambda b, k: (b, 0)),
        scratch_shapes=[pltpu.VMEM((m, bn), F32)],
        compiler_params=_params(("parallel", "arbitrary"), VMEM_BIG),
    )(xt, dy)


def _mix_in_bwd(pieces, wt, h, g, dh_in, post):
    t, d = h.shape
    tm = ROW_TILE // 2
    widths = [p.shape[1] for p in pieces]
    n = len(pieces)

    def body(*refs):
        dy_refs = refs[:n]
        w_ref, h_ref, g_ref, dhi_ref, xp_ref, gp_ref, dh_ref, dg_ref, dxp_ref, dgp_ref = refs[n:]
        first = pl.program_id(0) == 0

        @pl.when(first)
        def _():
            dg_ref[...] = jnp.zeros_like(dg_ref)

        dn, off = None, 0
        for dy_ref, wd in zip(dy_refs, widths):
            part = _dot(dy_ref[...].astype(BF16), w_ref[off:off + wd, :])
            dn = part if dn is None else dn + part
            off += wd
        dx, dg = _rms_bwd(h_ref[...], g_ref[...], dn)
        dh = dhi_ref[...] + dx
        dh_ref[...] = dh
        dg_ref[...] += dg
        _next_post_norm_bwd(dh, (xp_ref, gp_ref, dxp_ref, dgp_ref), post[2], first)

    row = pl.BlockSpec((tm, d), lambda i: (i, 0))
    vec = pl.BlockSpec((1, d), lambda i: (0, 0))
    return pl.pallas_call(
        body, name="mix_in_bwd",
        out_shape=[jax.ShapeDtypeStruct((t, d), F32), jax.ShapeDtypeStruct((1, d), F32),
                   jax.ShapeDtypeStruct((t, d), BF16), jax.ShapeDtypeStruct((1, d), F32)],
        grid=(t // tm,),
        in_specs=[pl.BlockSpec((tm, wd), lambda i: (i, 0)) for wd in widths]
        + [pl.BlockSpec(wt.shape, lambda i: (0, 0)), row, vec, row, row, vec],
        out_specs=[row, vec, row, vec],
        compiler_params=_params(("arbitrary",), VMEM_BIG),
    )(*pieces, wt, h, g, dh_in, post[0], post[1])


def _read_token_rows(src_hbm, buf, sems, i, n):
    tm = buf.shape[1]

    def first_tile():
        return pltpu.make_async_copy(src_hbm.at[pl.ds(0, tm - N_FRONT)], buf.at[0, pl.ds(N_FRONT, tm - N_FRONT)],
                                     sems.at[0])

    def tile(j):
        return pltpu.make_async_copy(src_hbm.at[pl.ds(pl.multiple_of(j * tm - N_FRONT, N_FRONT), tm)],
                                     buf.at[j % 2], sems.at[j % 2])

    @pl.when(i == 0)
    def _():
        buf[0, 0:N_FRONT, :] = jnp.zeros((N_FRONT, buf.shape[2]), buf.dtype)
        first_tile().start()

    @pl.when(i + 1 < n)
    def _():
        tile(i + 1).start()

    @pl.when(i == 0)
    def _():
        first_tile().wait()

    @pl.when(i > 0)
    def _():
        tile(i).wait()

    return buf.at[i % 2]


def _embed_norm(x, meta, g):
    seq, d = x.shape
    t = seq + N_FRONT
    tm = ROW_TILE

    def body(x_hbm, meta_ref, g_ref, h_ref, n_ref, nt_ref, buf, sems):
        i = pl.program_id(0)
        rows = _read_token_rows(x_hbm, buf, sems, i, t // tm)

        @pl.when(i == 0)
        def _():
            buf[0, ROW_PAD:N_FRONT, :] = meta_ref[...]

        h = rows[...]
        h_ref[...] = h
        y = h * _rstd(h) * g_ref[...]
        n_ref[...] = y.astype(BF16)
        nt_ref[...] = y.T.astype(BF16)

    row = pl.BlockSpec((tm, d), lambda i: (i, 0))
    return pl.pallas_call(
        body, name="embed_and_ffn1_pre_norm",
        out_shape=[jax.ShapeDtypeStruct((t, d), F32), jax.ShapeDtypeStruct((t, d), BF16),
                   jax.ShapeDtypeStruct((d, t), BF16)],
        grid=(t // tm,),
        in_specs=[ANY, pl.BlockSpec((N_META, d), lambda i: (0, 0)), pl.BlockSpec((1, d), lambda i: (0, 0))],
        out_specs=[row, row, pl.BlockSpec((d, tm), lambda i: (0, i))],
        scratch_shapes=[pltpu.VMEM((2, tm, d), F32), pltpu.SemaphoreType.DMA((2,))],
        compiler_params=_params(("arbitrary",)),
    )(x, meta, g)


def _slot_of(kk):
    return (kk % 2) * 2 + kk // 2


def _ffn_in(name, n, w4):
    t, d = n.shape
    cw = w4.shape[2]
    tm = ROW_TILE

    def body(x_ref, wg_ref, wu_ref, ab_ref, s_ref, st_ref):
        x = x_ref[...]
        a = _dot(x, wg_ref[...])
        b = _dot(x, wu_ref[...])
        ab_ref[:, :cw] = a.astype(BF16)
        ab_ref[:, cw:] = b.astype(BF16)
        s = a * _sigmoid(a) * b
        s_ref[...] = s.astype(BF16)
        st_ref[...] = s.T.astype(BF16)

    return pl.pallas_call(
        body, name=name,
        out_shape=[jax.ShapeDtypeStruct((t, 4 * cw), BF16), jax.ShapeDtypeStruct((t, 2 * cw), BF16),
                   jax.ShapeDtypeStruct((2 * cw, t), BF16)],
        grid=(2, t // tm),
        in_specs=[pl.BlockSpec((tm, d), lambda j, i: (i, 0)),
                  pl.BlockSpec((None, d, cw), lambda j, i: (j, 0, 0)),
                  pl.BlockSpec((None, d, cw), lambda j, i: (2 + j, 0, 0))],
        out_specs=[pl.BlockSpec((tm, 2 * cw), lambda j, i: (i, j)),
                   pl.BlockSpec((tm, cw), lambda j, i: (i, j)),
                   pl.BlockSpec((cw, tm), lambda j, i: (j, i))],
        compiler_params=_params(("parallel", "parallel"), VMEM_BIG),
    )(n, w4, w4)


def _mm_resid_norm(name, x, w, h, g_post, alpha, g_next):
    t, kdim = x.shape
    d = w.shape[1]
    tm = ROW_TILE
    with_next = g_next is not None

    def body(x_ref, w_ref, h_ref, gp_ref, gn_ref, f_ref, hn_ref, *rest):
        f = _dot(x_ref[...], w_ref[...])
        f_ref[...] = f
        hn = h_ref[...] + alpha * (f * _rstd(f) * gp_ref[...])
        hn_ref[...] = hn
        if with_next:
            y = hn * _rstd(hn) * gn_ref[...]
            rest[0][...] = y.astype(BF16)
            rest[1][...] = y.T.astype(BF16)

    row = lambda i: (i, 0)
    vec = pl.BlockSpec((1, d), lambda i: (0, 0))
    out_shape = [jax.ShapeDtypeStruct((t, d), F32), jax.ShapeDtypeStruct((t, d), F32)]
    out_specs = [pl.BlockSpec((tm, d), row), pl.BlockSpec((tm, d), row)]
    if with_next:
        out_shape += [jax.ShapeDtypeStruct((t, d), BF16), jax.ShapeDtypeStruct((d, t), BF16)]
        out_specs += [pl.BlockSpec((tm, d), row), pl.BlockSpec((d, tm), lambda i: (0, i))]
    return pl.pallas_call(
        body, name=name, out_shape=out_shape, grid=(t // tm,),
        in_specs=[pl.BlockSpec((tm, kdim), row), pl.BlockSpec((kdim, d), lambda i: (0, 0)),
                  pl.BlockSpec((tm, d), row), vec, vec],
        out_specs=out_specs,
        compiler_params=_params(("parallel",), VMEM_BIG),
    )(x, w, h, g_post, g_post if g_next is None else g_next)


def _in_proj(u, w):
    t, d = u.shape
    nz = w.shape[0]
    nq = 3 * ATTN_W
    tm = ROW_TILE // 2

    def body(u_ref, w_ref, qkv_ref, z_ref):
        qkv_ref[...] = _dot_nt(u_ref[...], w_ref[0:nq, :]).astype(BF16)
        z_ref[...] = _dot_nt(u_ref[...], w_ref[nq:, :])

    return pl.pallas_call(
        body, name="mix_in_proj",
        out_shape=[jax.ShapeDtypeStruct((t, nq), BF16), jax.ShapeDtypeStruct((t, nz - nq), F32)],
        grid=(t // tm,),
        in_specs=[pl.BlockSpec((tm, d), lambda i: (i, 0)), pl.BlockSpec((nz, d), lambda i: (0, 0))],
        out_specs=[pl.BlockSpec((tm, nq), lambda i: (i, 0)), pl.BlockSpec((tm, nz - nq), lambda i: (i, 0))],
        compiler_params=_params(("parallel",), VMEM_BIG),
    )(u, w)


def _gate_prep(z, b_pad, f_col):
    t = z.shape[0]
    tm = ROW_TILE

    def body(z_ref, b_ref, f_ref, carry_ref):
        i = pl.program_id(0)

        @pl.when(i == 0)
        def _():
            carry_ref[...] = jnp.zeros_like(carry_ref)

        xs = z_ref[...] + b_ref[...]
        logf = jnp.minimum(xs, 0.0) - jnp.log(1.0 + jnp.exp(-jnp.abs(xs)))
        row = i * tm + lax.broadcasted_iota(jnp.int32, (tm, 1), 0)
        logf = jnp.where(row >= ROW_PAD, logf, 0.0)
        tri = (lax.broadcasted_iota(jnp.int32, (tm, tm), 0) >= lax.broadcasted_iota(jnp.int32, (tm, tm), 1))
        f = jnp.dot(tri.astype(F32), logf, preferred_element_type=F32, precision=lax.Precision.HIGHEST)
        f = f + carry_ref[0:1, :]
        f_ref[...] = f
        carry_ref[...] = jnp.broadcast_to(f[tm - 1:tm, :], carry_ref.shape)

    return pl.pallas_call(
        body, name="forget_gate_cumsum", out_shape=jax.ShapeDtypeStruct((t, 128), F32),
        grid=(t // tm,),
        in_specs=[pl.BlockSpec((tm, 128), lambda i: (i, f_col // 128)), pl.BlockSpec((1, 128), lambda i: (0, 0))],
        out_specs=pl.BlockSpec((tm, 128), lambda i: (i, 0)),
        scratch_shapes=[pltpu.VMEM((8, 128), F32)],
        compiler_params=_params(("arbitrary",)),
    )(z, b_pad)


def _lane_halves():
    lane = lax.broadcasted_iota(jnp.int32, (1, 128), 1)
    return lane < HEAD_DIM


def _causal_mask(tq, tk, row0=0):
    row = row0 + lax.broadcasted_iota(jnp.int32, (tq, 1), 0)
    col = lax.broadcasted_iota(jnp.int32, (1, tk), 1)
    return col <= row


def _lane_one(lane):
    return (lax.broadcasted_iota(jnp.int32, (1, 128), 1) == lane).astype(BF16)


def _split3(x):
    hi = x.astype(BF16)
    rest = x - hi.astype(F32)
    mid = rest.astype(BF16)
    return hi, mid, (rest - mid.astype(F32)).astype(BF16)


def _split3_glue(x):
    hi = lax.reduce_precision(x, 8, 7)
    mid = lax.reduce_precision(x - hi, 8, 7)
    lo = lax.reduce_precision((x - hi) - mid, 8, 7)
    return hi.astype(BF16), mid.astype(BF16), lo.astype(BF16)


def _aug_pairs(cols):
    t = cols[0].shape[0]
    a = jnp.pad(jnp.stack(cols, axis=2), ((0, 0), (0, 0), (0, HEAD_DIM - len(cols))))
    a = a.reshape(t, 4, 2, HEAD_DIM)[:, :, ::-1, :]
    return jnp.transpose(a.reshape(t, 4, 128), (1, 0, 2))


def _attn_bias_operands(f_heads, lse_heads=None):
    t = f_heads.shape[0]
    one = jnp.ones((t, HEADS), BF16)
    row = lax.broadcasted_iota(jnp.int32, (t, 1), 0)
    fq = _split3_glue(f_heads)
    fk = _split3_glue(jnp.where(row < ROW_PAD, 1e9, f_heads))
    q_cols = list(fq) + [one] * 3
    k_cols = [one] * 3 + [-c for c in fk]
    if lse_heads is not None:
        q_cols += [-c for c in _split3_glue(lse_heads)]
        k_cols += [one] * 3
    return _aug_pairs(q_cols), _aug_pairs(k_cols)


def _attn_fwd(z, aug_q, aug_k):
    t = z.shape[0]
    tq = tk = ROW_TILE
    nq = t // tq
    grp = ATTN_KV_GROUP
    steps = [(qi, ka) for qi in range(nq) for ka in range(0, qi + 1, grp)]
    q_tab = jnp.array([qi for qi, _ in steps], jnp.int32)
    k_tab = jnp.array([ka for _, ka in steps], jnp.int32)

    def body(qt_ref, kt_ref, q_ref, *refs):
        k_refs, v_refs, aq_ref, ak_refs = refs[:grp], refs[grp:2 * grp], refs[2 * grp], refs[2 * grp + 1:3 * grp + 1]
        o_ref, lse_ref, m_ref, l_ref, acc_ref = refs[3 * grp + 1:]
        step = pl.program_id(1)
        qi, ka = qt_ref[step], kt_ref[step]

        @pl.when(ka == 0)
        def _():
            m_ref[...] = jnp.full_like(m_ref, NEG)
            l_ref[...] = jnp.zeros_like(l_ref)
            acc_ref[...] = jnp.zeros_like(acc_ref)

        def sweep(diagonal):
            first = _lane_halves()
            halves = (first, jnp.logical_not(first))
            q = (q_ref[...] * (HEAD_DIM ** -0.5)).astype(BF16)
            aq = aq_ref[...]
            qa = [jnp.where(lanes, q, aq) for lanes in halves]
            blocks = list(zip(k_refs, v_refs, ak_refs, diagonal))
            s = []
            for k_ref, _, ak_ref, diag in blocks:
                k, ak = k_ref[...].astype(BF16), ak_ref[...]
                for hh, lanes in enumerate(halves):
                    s_c = _dot_nt(qa[hh], jnp.where(lanes, k, ak))
                    s.append(jnp.where(_causal_mask(tq, tk), s_c, NEG) if diag else s_c)
            nb = len(blocks)
            m_prev = [m_ref[:, c0:c0 + 1] for c0 in (0, HEAD_DIM)]
            m_new = []
            for hh in range(2):
                m_h = m_prev[hh]
                for b in range(nb):
                    m_h = jnp.maximum(m_h, jnp.max(s[2 * b + hh], axis=1, keepdims=True))
                m_new.append(m_h)
            pv = [None, None]
            for b, (_, v_ref, _, _) in enumerate(blocks):
                v = v_ref[...].astype(BF16)
                for hh, (lanes, a0) in enumerate(zip(halves, (HEAD_DIM, 0))):
                    part = _dot(jnp.exp(s[2 * b + hh] - m_new[hh]).astype(BF16), jnp.where(lanes, v, _lane_one(a0)))
                    pv[hh] = part if pv[hh] is None else pv[hh] + part
            al0, al1 = [jnp.exp(mp - m_h) for mp, m_h in zip(m_prev, m_new)]
            l0 = al0 * l_ref[:, 0:1] + pv[0][:, HEAD_DIM:HEAD_DIM + 1]
            l1 = al1 * l_ref[:, HEAD_DIM:HEAD_DIM + 1] + pv[1][:, 0:1]
            acc_ref[...] = acc_ref[...] * jnp.where(first, al0, al1) + jnp.where(first, pv[0], pv[1])
            m_ref[...] = jnp.where(first, m_new[0], m_new[1])
            l_ref[...] = jnp.where(first, l0, l1)

        def finish():
            o_ref[...] = acc_ref[...] / l_ref[...]
            lse_ref[...] = m_ref[...] + jnp.log(l_ref[...])

        @pl.when(ka + grp - 1 < qi)
        def _():
            sweep((False,) * grp)

        for nb in range(1, grp + 1):
            @pl.when(ka + nb - 1 == qi)
            def _(nb=nb):
                sweep((False,) * (nb - 1) + (True,))
                finish()

    def kblock(j):
        return lambda s, qt, kt: jnp.minimum(kt[s] + j, qt[s])

    kbs = [kblock(j) for j in range(grp)]
    return pl.pallas_call(
        body, name="attention_fwd",
        out_shape=[jax.ShapeDtypeStruct((t, ATTN_W), F32), jax.ShapeDtypeStruct((t, ATTN_W), F32)],
        grid_spec=pltpu.PrefetchScalarGridSpec(
            num_scalar_prefetch=2, grid=(4, len(steps)),
            in_specs=[pl.BlockSpec((tq, 128), lambda p, s, qt, kt: (qt[s], p))]
            + [pl.BlockSpec((tk, 128), functools.partial(lambda p, s, qt, kt, kb: (kb(s, qt, kt), 4 + p), kb=kb))
               for kb in kbs]
            + [pl.BlockSpec((tk, 128), functools.partial(lambda p, s, qt, kt, kb: (kb(s, qt, kt), 8 + p), kb=kb))
               for kb in kbs]
            + [pl.BlockSpec((None, tq, 128), lambda p, s, qt, kt: (p, qt[s], 0))]
            + [pl.BlockSpec((None, tk, 128), functools.partial(lambda p, s, qt, kt, kb: (p, kb(s, qt, kt), 0), kb=kb))
               for kb in kbs],
            out_specs=[pl.BlockSpec((tq, 128), lambda p, s, qt, kt: (qt[s], p)),
                       pl.BlockSpec((tq, 128), lambda p, s, qt, kt: (qt[s], p))],
            scratch_shapes=[pltpu.VMEM((tq, 128), F32)] * 3),
        compiler_params=_params(("parallel", "arbitrary"), VMEM_BIG),
    )(q_tab, k_tab, z, *([z] * (2 * grp)), aug_q, *([aug_k] * grp))


def _attn_bwd(z, aug_q, aug_k, o, do):
    t = z.shape[0]
    tq = tk = ROW_TILE
    nq = t // tq
    grp = ATTN_Q_GROUP
    steps = [(qa, ki) for ki in range(nq) for qa in range(ki, nq, grp)]
    q_tab = jnp.array([qa for qa, _ in steps], jnp.int32)
    k_tab = jnp.array([ki for _, ki in steps], jnp.int32)
    tn = (((0,), (0,)), ((), ()))

    def body(qt_ref, kt_ref, *refs):
        q_refs, (k_ref, v_ref) = refs[:grp], refs[grp:grp + 2]
        aq_refs, ak_ref = refs[grp + 2:2 * grp + 2], refs[2 * grp + 2]
        o_refs, do_refs = refs[2 * grp + 3:3 * grp + 3], refs[3 * grp + 3:4 * grp + 3]
        dq_ref, dk_ref, dv_ref, dfk_ref, dfq_ref = refs[4 * grp + 3:]
        step = pl.program_id(1)
        qa, ki = qt_ref[step], kt_ref[step]

        def rows(j):
            return pl.ds(pl.multiple_of((qa + j) * tq, tq), tq)

        for j in range(grp):
            @pl.when((ki == 0) & (qa + j < nq))
            def _(j=j):
                dq_ref[rows(j), :] = jnp.zeros((tq, 128), F32)
                dfq_ref[rows(j), :] = jnp.zeros((tq, 128), F32)

        @pl.when(qa == ki)
        def _():
            dk_ref[...] = jnp.zeros_like(dk_ref)
            dv_ref[...] = jnp.zeros_like(dv_ref)
            dfk_ref[...] = jnp.zeros_like(dfk_ref)

        def sweep(nb, diagonal):
            first = _lane_halves()
            lane = lax.broadcasted_iota(jnp.int32, (1, 128), 1)
            scale = HEAD_DIM ** -0.5
            halves = (first, jnp.logical_not(first))
            spare = (HEAD_DIM, 0)
            k = k_ref[...].astype(BF16)
            v = v_ref[...].astype(BF16)
            ak = ak_ref[...]
            k_bias = [jnp.where(lanes, k, ak) for lanes in halves]
            k_ones = [jnp.where(lanes, k, _lane_one(a)) for lanes, a in zip(halves, spare)]
            v_ones = [jnp.where(lanes, v, ((lane >= a) & (lane < a + 3)).astype(BF16)) for lanes, a in zip(halves, spare)]
            chains = [(j, hh) for j in range(nb) for hh in range(2)]
            q16, do16, dos = [], [], []
            for j in range(nb):
                q16.append((q_refs[j][...] * scale).astype(BF16))
                do_ = do_refs[j][...]
                do16.append(do_.astype(BF16))
                od = o_refs[j][...] * do_
                for lanes, a in zip(halves, spare):
                    d_hi, d_mid, d_lo = _split3(jnp.sum(jnp.where(lanes, od, 0.0), axis=1, keepdims=True))
                    minus_delta = jnp.where(lane == a, -d_hi, jnp.where(lane == a + 1, -d_mid,
                                            jnp.where(lane == a + 2, -d_lo, jnp.zeros((), BF16))))
                    dos.append(jnp.where(lanes, do16[j], minus_delta))
            s = [_dot_nt(jnp.where(halves[hh], q16[j], aq_refs[j][...]), k_bias[hh]) for j, hh in chains]
            dp = [_dot_nt(dos[2 * j + hh], v_ones[hh]) for j, hh in chains]
            p = [jnp.exp(s_c) for s_c in s]
            if diagonal:
                p = [jnp.where(_causal_mask(tq, tk), p_c, 0.0) if j == 0 else p_c for p_c, (j, _) in zip(p, chains)]
            ds16 = [(p_c * dp_c).astype(BF16) for p_c, dp_c in zip(p, dp)]
            dv, dk = [None, None], [None, None]
            for c, (j, hh) in enumerate(chains):
                lanes = halves[hh]
                dv_c = lax.dot_general(jnp.where(lanes, do16[j], jnp.zeros((), BF16)), p[c].astype(BF16), tn,
                                       preferred_element_type=F32)
                dk_c = lax.dot_general(jnp.where(lanes, q16[j], _lane_one(spare[hh])), ds16[c], tn,
                                       preferred_element_type=F32)
                dv[hh] = dv_c if dv[hh] is None else dv[hh] + dv_c
                dk[hh] = dk_c if dk[hh] is None else dk[hh] + dk_c
            dv = [x.T for x in dv]
            dk = [x.T for x in dk]
            for j in range(nb):
                dq0, dq1 = [_dot(ds16[2 * j + hh], k_ones[hh]) for hh in range(2)]
                dq_ref[rows(j), :] += jnp.where(first, dq0, dq1) * scale
                dfq_ref[rows(j), :] += jnp.where(first, dq0[:, HEAD_DIM:HEAD_DIM + 1], dq1[:, 0:1])
            dk_ref[...] += jnp.where(first, dk[0], dk[1])
            dfk_ref[...] += jnp.where(first, dk[0][:, HEAD_DIM:HEAD_DIM + 1], dk[1][:, 0:1])
            dv_ref[...] += dv[0] + dv[1]

        for nb in range(1, grp + 1):
            exists = (qa + grp <= nq) if nb == grp else (qa + nb == nq)
            for diagonal in (False, True):
                @pl.when(exists & ((qa == ki) == diagonal))
                def _(nb=nb, diagonal=diagonal):
                    sweep(nb, diagonal)

    def qblock(j):
        return lambda s, qt: jnp.minimum(qt[s] + j, nq - 1)

    qbs = [qblock(j) for j in range(grp)]
    qcol = [functools.partial(lambda p, s, qt, kt, qb: (qb(s, qt), p), qb=qb) for qb in qbs]
    krow = lambda p, s, qt, kt: (kt[s], p)
    return pl.pallas_call(
        body, name="attention_bwd",
        out_shape=[jax.ShapeDtypeStruct((t, ATTN_W), F32)] * 5,
        grid_spec=pltpu.PrefetchScalarGridSpec(
            num_scalar_prefetch=2, grid=(4, len(steps)),
            in_specs=[pl.BlockSpec((tq, 128), m) for m in qcol]
            + [pl.BlockSpec((tk, 128), lambda p, s, qt, kt: (kt[s], 4 + p)),
               pl.BlockSpec((tk, 128), lambda p, s, qt, kt: (kt[s], 8 + p))]
            + [pl.BlockSpec((None, tq, 128), functools.partial(lambda p, s, qt, kt, qb: (p, qb(s, qt), 0), qb=qb))
               for qb in qbs]
            + [pl.BlockSpec((None, tk, 128), lambda p, s, qt, kt: (p, kt[s], 0))]
            + [pl.BlockSpec((tq, 128), m) for m in qcol] + [pl.BlockSpec((tq, 128), m) for m in qcol],
            out_specs=[pl.BlockSpec((t, 128), lambda p, s, qt, kt: (0, p)),
                       pl.BlockSpec((tk, 128), krow), pl.BlockSpec((tk, 128), krow), pl.BlockSpec((tk, 128), krow),
                       pl.BlockSpec((t, 128), lambda p, s, qt, kt: (0, p))]),
        compiler_params=_params(("parallel", "arbitrary"), VMEM_BIG),
    )(q_tab, k_tab, *([z] * grp), z, z, *([aug_q] * grp), aug_k, *([o] * grp), *([do] * grp))


def _shifted(prev_rows, x, shift):
    tm = x.shape[0]
    return pltpu.roll(jnp.concatenate([prev_rows, x], axis=0), shift, 0)[8:8 + tm]


def _ahead(x, next_rows, shift):
    tm = x.shape[0]
    return pltpu.roll(jnp.concatenate([x, next_rows], axis=0), tm + 8 - shift, 0)[0:tm]


def _conv_col0(z):
    return (z.shape[1] - F_PAD - 3 * CONV_W) // CONV_W


def _conv_specs(tm, c0):
    cols = (c0, c0 + 1, c0 + 2)
    tiles = [pl.BlockSpec((tm, CONV_W), functools.partial(lambda i, c: (i, c), c=c)) for c in cols]
    halos = [pl.BlockSpec((8, CONV_W), functools.partial(lambda i, c: (jnp.maximum(i * (tm // 8) - 1, 0), c), c=c))
             for c in cols]
    return tiles, halos


def _conv_gate(z, conv_w):
    t = z.shape[0]
    tm = ROW_TILE
    nt = t // tm

    def body(cb_ref, cc_ref, ci_ref, hc_ref, hi_ref, w_ref, g_ref, gt_ref):
        i = pl.program_id(0)
        cc = cc_ref[...] * ci_ref[...]
        prev = jnp.where(i > 0, hc_ref[...] * hi_ref[...], 0.0)
        conv = w_ref[0:1, :] * _shifted(prev, cc, 2) + w_ref[1:2, :] * _shifted(prev, cc, 1) + w_ref[2:3, :] * cc
        g = cb_ref[...] * conv
        g_ref[...] = g.astype(BF16)
        gt_ref[...] = g.T.astype(BF16)

    (cb, cc, ci), (_, hc, hi) = _conv_specs(tm, _conv_col0(z))
    return pl.pallas_call(
        body, name="conv_gate_fwd",
        out_shape=[jax.ShapeDtypeStruct((t, CONV_W), BF16), jax.ShapeDtypeStruct((CONV_W, t), BF16)],
        grid=(nt,),
        in_specs=[cb, cc, ci, hc, hi, pl.BlockSpec((8, CONV_W), lambda i: (0, 0))],
        out_specs=[pl.BlockSpec((tm, CONV_W), lambda i: (i, 0)), pl.BlockSpec((CONV_W, tm), lambda i: (0, i))],
        compiler_params=_params(("parallel",)),
    )(z, z, z, z, z, conv_w)


def _conv_bwd(z, dg, conv_w):
    t = z.shape[0]
    tm = ROW_TILE
    nt = t // tm

    def body(cb_ref, cc_ref, ci_ref, hc_ref, hi_ref, dg_ref, ncb_ref, ndg_ref, w_ref, dz_ref, dw_ref):
        i = pl.program_id(0)

        @pl.when(i == 0)
        def _():
            dw_ref[...] = jnp.zeros_like(dw_ref)

        cb, c_c, c_in = cb_ref[...], cc_ref[...], ci_ref[...]
        cc = c_c * c_in
        prev = jnp.where(i > 0, hc_ref[...] * hi_ref[...], 0.0)
        cc1, cc2 = _shifted(prev, cc, 1), _shifted(prev, cc, 2)
        w0, w1, w2 = w_ref[0:1, :], w_ref[1:2, :], w_ref[2:3, :]
        conv = w0 * cc2 + w1 * cc1 + w2 * cc
        dgv = dg_ref[...]
        dconv = dgv * cb
        nxt = jnp.where(i < nt - 1, ndg_ref[...] * ncb_ref[...], 0.0)
        dcc = w2 * dconv + w1 * _ahead(dconv, nxt, 1) + w0 * _ahead(dconv, nxt, 2)
        dz_ref[:, 0:CONV_W] = (dgv * conv).astype(BF16)
        dz_ref[:, CONV_W:2 * CONV_W] = (dcc * c_in).astype(BF16)
        dz_ref[:, 2 * CONV_W:] = (dcc * c_c).astype(BF16)
        dw_ref[0:1, :] += jnp.sum(dconv * cc2, axis=0, keepdims=True)
        dw_ref[1:2, :] += jnp.sum(dconv * cc1, axis=0, keepdims=True)
        dw_ref[2:3, :] += jnp.sum(dconv * cc, axis=0, keepdims=True)

    c0 = _conv_col0(z)
    (cb, cc, ci), (_, hc, hi) = _conv_specs(tm, c0)
    nxt = lambda i, c: (jnp.minimum((i + 1) * (tm // 8), t // 8 - 1), c)
    return pl.pallas_call(
        body, name="conv_gate_bwd",
        out_shape=[jax.ShapeDtypeStruct((t, 3 * CONV_W), BF16), jax.ShapeDtypeStruct((8, CONV_W), F32)],
        grid=(nt,),
        in_specs=[cb, cc, ci, hc, hi, pl.BlockSpec((tm, CONV_W), lambda i: (i, 0)),
                  pl.BlockSpec((8, CONV_W), lambda i: nxt(i, c0)), pl.BlockSpec((8, CONV_W), lambda i: nxt(i, 0)),
                  pl.BlockSpec((8, CONV_W), lambda i: (0, 0))],
        out_specs=[pl.BlockSpec((tm, 3 * CONV_W), lambda i: (i, 0)), pl.BlockSpec((8, CONV_W), lambda i: (0, 0))],
        compiler_params=_params(("arbitrary",)),
    )(z, z, z, z, z, dg, z, dg, conv_w)


def _branch_mix(z, o, g, w_ab, w_cb, d):
    t = z.shape[0]
    tm = ROW_TILE
    ga_col = 0

    def body(o_ref, g_ref, ga_ref, gc_ref, wa_ref, wc_ref, mp_ref, mpt_ref, ot_ref):
        o_ = o_ref[...]
        ya = _dot(o_.astype(BF16), wa_ref[...])
        yc = _dot(g_ref[...], wc_ref[...])
        mp = _sigmoid(ga_ref[...]) * ya + _sigmoid(gc_ref[...]) * yc
        mp_ref[...] = mp.astype(BF16)
        mpt_ref[...] = mp.T.astype(BF16)
        ot_ref[...] = o_.T.astype(BF16)

    return pl.pallas_call(
        body, name="branch_mix_fwd",
        out_shape=[jax.ShapeDtypeStruct((t, d), BF16), jax.ShapeDtypeStruct((d, t), BF16),
                   jax.ShapeDtypeStruct((ATTN_W, t), BF16)],
        grid=(t // tm,),
        in_specs=[pl.BlockSpec((tm, ATTN_W), lambda i: (i, 0)), pl.BlockSpec((tm, CONV_W), lambda i: (i, 0)),
                  pl.BlockSpec((tm, d), lambda i: (i, ga_col)), pl.BlockSpec((tm, d), lambda i: (i, ga_col + 1)),
                  pl.BlockSpec((ATTN_W, d), lambda i: (0, 0)), pl.BlockSpec((CONV_W, d), lambda i: (0, 0))],
        out_specs=[pl.BlockSpec((tm, d), lambda i: (i, 0)), pl.BlockSpec((d, tm), lambda i: (0, i)),
                   pl.BlockSpec((ATTN_W, tm), lambda i: (0, i))],
        compiler_params=_params(("parallel",), VMEM_BIG),
    )(o, g, z, z, w_ab, w_cb)


def _branch_bwd(z, o, g, dmixed, w_out, w_ab, w_cb, d):
    t = z.shape[0]
    tm = ROW_TILE // 2
    ga_col = 0

    def body(dm_ref, o_ref, g_ref, ga_ref, gc_ref, wo_ref, wa_ref, wc_ref, dya_ref, dyc_ref, dgt_ref, do_ref, dg_ref):
        dmp = _dot_nt(dm_ref[...], wo_ref[...])
        ya = _dot(o_ref[...].astype(BF16), wa_ref[...])
        yc = _dot(g_ref[...], wc_ref[...])
        sa, sc = _sigmoid(ga_ref[...]), _sigmoid(gc_ref[...])
        dya = (dmp * sa).astype(BF16)
        dyc = (dmp * sc).astype(BF16)
        dya_ref[...] = dya
        dyc_ref[...] = dyc
        dgt_ref[:, :d] = (dmp * ya * sa * (1.0 - sa)).astype(BF16)
        dgt_ref[:, d:] = (dmp * yc * sc * (1.0 - sc)).astype(BF16)
        do_ref[...] = _dot_nt(dya, wa_ref[...])
        dg_ref[...] = _dot_nt(dyc, wc_ref[...])

    row = lambda i: (i, 0)
    fixed = lambda i: (0, 0)
    return pl.pallas_call(
        body, name="branch_mix_bwd",
        out_shape=[jax.ShapeDtypeStruct((t, d), BF16), jax.ShapeDtypeStruct((t, d), BF16),
                   jax.ShapeDtypeStruct((t, 2 * d), BF16), jax.ShapeDtypeStruct((t, ATTN_W), F32),
                   jax.ShapeDtypeStruct((t, CONV_W), F32)],
        grid=(t // tm,),
        in_specs=[pl.BlockSpec((tm, d), row), pl.BlockSpec((tm, ATTN_W), row), pl.BlockSpec((tm, CONV_W), row),
                  pl.BlockSpec((tm, d), lambda i: (i, ga_col)), pl.BlockSpec((tm, d), lambda i: (i, ga_col + 1)),
                  pl.BlockSpec((d, d), fixed), pl.BlockSpec((ATTN_W, d), fixed), pl.BlockSpec((CONV_W, d), fixed)],
        out_specs=[pl.BlockSpec((tm, d), row), pl.BlockSpec((tm, d), row), pl.BlockSpec((tm, 2 * d), row),
                   pl.BlockSpec((tm, ATTN_W), row), pl.BlockSpec((tm, CONV_W), row)],
        compiler_params=_params(("parallel",), VMEM_BIG),
    )(dmixed, o, g, z, z, w_out, w_ab, w_cb)


def _loss_norm_bwd(h, target, f, g_post, alpha):
    t, d = h.shape
    tm = ROW_TILE

    def body(h_ref, t_hbm, f_ref, g_ref, dh_ref, df_ref, dg_ref, loss_ref, t_buf, sems):
        i = pl.program_id(0)

        @pl.when(i == 0)
        def _():
            loss_ref[...] = jnp.zeros_like(loss_ref)
            dg_ref[...] = jnp.zeros_like(dg_ref)

        target = _read_token_rows(t_hbm, t_buf, sems, i, t // tm)
        row = i * tm + lax.broadcasted_iota(jnp.int32, (tm, 1), 0)
        err = jnp.where(row >= N_FRONT, h_ref[...] - target[...], 0.0)
        dy = err * (1.0 / d)
        dh_ref[...] = dy
        per_row = jnp.sum(err * err, axis=1, keepdims=True) * (1.0 / d)
        loss_ref[...] += 0.5 * jnp.sum(per_row, axis=0, keepdims=True)
        dx, dg = _rms_bwd(f_ref[...], g_ref[...], dy)
        df_ref[...] = (alpha * dx).astype(BF16)
        dg_ref[...] += alpha * dg

    row = pl.BlockSpec((tm, d), lambda i: (i, 0))
    vec = pl.BlockSpec((1, d), lambda i: (0, 0))
    return pl.pallas_call(
        body, name="loss_and_post_norm_bwd",
        out_shape=[jax.ShapeDtypeStruct((t, d), F32), jax.ShapeDtypeStruct((t, d), BF16),
                   jax.ShapeDtypeStruct((1, d), F32), jax.ShapeDtypeStruct((1, 128), F32)],
        grid=(t // tm,),
        in_specs=[row, ANY, row, vec],
        out_specs=[row, row, vec, pl.BlockSpec((1, 128), lambda i: (0, 0))],
        scratch_shapes=[pltpu.VMEM((2, tm, d), F32), pltpu.SemaphoreType.DMA((2,))],
        compiler_params=_params(("arbitrary",)),
    )(h, target, f, g_post)


def _ffn_bwd_mid(name, df, w_out, ab):
    t, d = df.shape
    cw = ab.shape[1] // 4
    tm = ROW_TILE

    def body(df_ref, w_ref, ab_ref, o_ref):
        ds = _dot_nt(df_ref[...], w_ref[...])
        a = ab_ref[:, :cw].astype(F32)
        b = ab_ref[:, cw:].astype(F32)
        sg = _sigmoid(a)
        o_ref[:, :cw] = (ds * b * (sg * (1.0 + a * (1.0 - sg)))).astype(BF16)
        o_ref[:, cw:] = (ds * (a * sg)).astype(BF16)

    return pl.pallas_call(
        body, name=name, out_shape=jax.ShapeDtypeStruct((t, 4 * cw), BF16),
        grid=(2, t // tm),
        in_specs=[pl.BlockSpec((tm, d), lambda j, i: (i, 0)), pl.BlockSpec((cw, d), lambda j, i: (j, 0)),
                  pl.BlockSpec((tm, 2 * cw), lambda j, i: (i, j))],
        out_specs=pl.BlockSpec((tm, 2 * cw), lambda j, i: (i, j)),
        compiler_params=_params(("parallel", "parallel"), VMEM_BIG),
    )(df, w_out, ab)


def _next_post_norm_bwd(dh, post_refs, alpha, first):
    x_ref, g_ref, dx_ref, dg_ref = post_refs

    @pl.when(first)
    def _():
        dg_ref[...] = jnp.zeros_like(dg_ref)

    dx, dg = _rms_bwd(x_ref[...], g_ref[...], dh)
    dx_ref[...] = (alpha * dx).astype(BF16)
    dg_ref[...] += alpha * dg


def _mm_nt_norm_bwd(name, dy, w, h, g, dh_in, post=None):
    t, kdim = dy.shape
    d = h.shape[1]
    tm = ROW_TILE // 2

    def body(dy_ref, w_ref, h_ref, g_ref, dhi_ref, *rest):
        dh_ref, dg_ref = rest[-4:-2] if post else rest
        first = pl.program_id(0) == 0

        @pl.when(first)
        def _():
            dg_ref[...] = jnp.zeros_like(dg_ref)

        cw = w_ref.shape[2]
        dn = _dot_nt(dy_ref[:, 0:cw], w_ref[_slot_of(0)])
        for k in range(1, 4):
            dn += _dot_nt(dy_ref[:, k * cw:(k + 1) * cw], w_ref[_slot_of(k)])
        dx, dg = _rms_bwd(h_ref[...], g_ref[...], dn)
        dh = dhi_ref[...] + dx
        dh_ref[...] = dh
        dg_ref[...] += dg
        if post:
            _next_post_norm_bwd(dh, rest[0:2] + rest[-2:], post[2], first)

    row = pl.BlockSpec((tm, d), lambda i: (i, 0))
    vec = pl.BlockSpec((1, d), lambda i: (0, 0))
    out_shape = [jax.ShapeDtypeStruct((t, d), F32), jax.ShapeDtypeStruct((1, d), F32)]
    if post:
        out_shape += [jax.ShapeDtypeStruct((t, d), BF16), jax.ShapeDtypeStruct((1, d), F32)]
    return pl.pallas_call(
        body, name=name, out_shape=out_shape, grid=(t // tm,),
        in_specs=[pl.BlockSpec((tm, kdim), lambda i: (i, 0)), pl.BlockSpec(w.shape, lambda i: (0,) * w.ndim),
                  row, vec, row] + ([row, vec] if post else []),
        out_specs=[row, vec] + ([row, vec] if post else []),
        compiler_params=_params(("arbitrary",), VMEM_BIG),
    )(dy, w, h, g, dh_in, *(post[:2] if post else ()))


def _gate_bwd(dfq, dfk, z, b_pad, f_col):
    t = z.shape[0]
    tm = ROW_TILE
    nt = t // tm

    def body(dq_ref, dk_ref, z_ref, b_ref, dz_ref, db_ref, carry_ref):
        i = pl.program_id(0)

        @pl.when(i == 0)
        def _():
            carry_ref[...] = jnp.zeros_like(carry_ref)
            db_ref[...] = jnp.zeros_like(db_ref)

        pick = (lax.broadcasted_iota(jnp.int32, (ATTN_W, 128), 0)
                == HEAD_DIM * lax.broadcasted_iota(jnp.int32, (ATTN_W, 128), 1)).astype(F32)
        d_heads = jnp.dot(dq_ref[...] - dk_ref[...], pick, preferred_element_type=F32,
                          precision=lax.Precision.HIGHEST)
        tri = (lax.broadcasted_iota(jnp.int32, (tm, tm), 0) <= lax.broadcasted_iota(jnp.int32, (tm, tm), 1))
        tail = jnp.dot(tri.astype(F32), d_heads, preferred_element_type=F32, precision=lax.Precision.HIGHEST)
        tail = tail + carry_ref[0:1, :]
        carry_ref[...] = jnp.broadcast_to(tail[0:1, :], carry_ref.shape)
        row = (nt - 1 - i) * tm + lax.broadcasted_iota(jnp.int32, (tm, 1), 0)
        dlogit = jnp.where(row >= ROW_PAD, tail * _sigmoid(-(z_ref[...] + b_ref[...])), 0.0)
        dz_ref[...] = jnp.zeros_like(dz_ref)
        dz_ref[:, 0:128] = dlogit.astype(BF16)
        db_ref[...] += jnp.sum(dlogit, axis=0, keepdims=True)

    rev = lambda i: (nt - 1 - i, 0)
    return pl.pallas_call(
        body, name="forget_gate_bwd",
        out_shape=[jax.ShapeDtypeStruct((t, F_PAD), BF16), jax.ShapeDtypeStruct((1, 128), F32)],
        grid=(nt,),
        in_specs=[pl.BlockSpec((tm, ATTN_W), rev), pl.BlockSpec((tm, ATTN_W), rev),
                  pl.BlockSpec((tm, 128), lambda i: (nt - 1 - i, f_col // 128)),
                  pl.BlockSpec((1, 128), lambda i: (0, 0))],
        out_specs=[pl.BlockSpec((tm, F_PAD), rev), pl.BlockSpec((1, 128), lambda i: (0, 0))],
        scratch_shapes=[pltpu.VMEM((8, 128), F32)],
        compiler_params=_params(("arbitrary",)),
    )(dfq, dfk, z, b_pad)


def _ffn_fwd(tag, n, w_in4, w_out, h, g_post, g_next):
    ab, s, s_t = _ffn_in(f"{tag}_in_fwd", n, w_in4)
    outs = _mm_resid_norm(f"{tag}_out_fwd", s, w_out, h, g_post, 0.5, g_next)
    return ab, s_t, outs


def _ffn_bwd_weights(tag, df, ab, s_t, n_t, w_in4, w_out):
    d, cw = w_in4.shape[1], w_in4.shape[2]
    t = df.shape[0]
    dw_out = _weight_grad(f"{tag}_dw_out", s_t, df, d, out_rows=cw // 2)
    dab = _ffn_bwd_mid(f"{tag}_mid_bwd", df, w_out, ab)
    bk = _k_tile(t)
    dw_in = _matmul(
        f"{tag}_dw_in", n_t, dab, jax.ShapeDtypeStruct((4, d, cw), F32), (1, 4, t // bk),
        pl.BlockSpec((d, bk), lambda a, b, k: (0, k)), pl.BlockSpec((bk, cw), lambda a, b, k: (k, b)),
        pl.BlockSpec((None, d, cw), lambda a, b, k: (_slot_of(b), 0, 0)), vmem=VMEM_BIG)
    return dab, dw_in, dw_out


LOSS_ROW = 12


def _pack_small(meta, conv, gains, b_forget, loss=None):
    d = gains[0].shape[1]
    rows = [meta.reshape(4, d), jnp.pad(conv.reshape(1, 3 * 128), ((0, 0), (0, d - 3 * 128)))]
    rows += list(gains) + [jnp.pad(b_forget, ((0, 0), (0, d - HEADS)))]
    last = jnp.zeros((4, d), F32)
    if loss is not None:
        last = jnp.pad(loss.reshape(1, 1), ((0, 3), (0, d - 1)))
    return jnp.concatenate(rows + [last], axis=0)


def _unpack_small(block):
    d = block.shape[1]
    meta = block[0:4].reshape(N_META, d // 4)
    conv = block[4, :3 * 128].reshape(1, 3, 128)
    gains = [block[5 + i:6 + i] for i in range(6)]
    return meta, conv, gains, block[11:12, :HEADS]


def kernel(x, meta_tokens, w_in, b_forget, conv_w, w_attn_branch, w_conv_branch, w_out, g_ffn1_pre, g_ffn1_post, w_ffn1_in, w_ffn1_out, g_mix_pre, g_mix_post, g_ffn2_pre, g_ffn2_post, w_ffn2_in, w_ffn2_out, loss_target, m_meta_tokens, m_w_in, m_b_forget, m_conv_w, m_w_attn_branch, m_w_conv_branch, m_w_out, m_g_ffn1_pre, m_g_ffn1_post, m_w_ffn1_in, m_w_ffn1_out, m_g_mix_pre, m_g_mix_post, m_g_ffn2_pre, m_g_ffn2_post, m_w_ffn2_in, m_w_ffn2_out, v_meta_tokens, v_w_in, v_b_forget, v_conv_w, v_w_attn_branch, v_w_conv_branch, v_w_out, v_g_ffn1_pre, v_g_ffn1_post, v_w_ffn1_in, v_w_ffn1_out, v_g_mix_pre, v_g_mix_post, v_g_ffn2_pre, v_g_ffn2_post, v_w_ffn2_in, v_w_ffn2_out):
    seq, d = x.shape[1], x.shape[2]
    t = seq + N_FRONT
    f_lo = 3 * ATTN_W
    c_arr = lax.axis_index("c").astype(jnp.int32).reshape(1)

    cs = w_in.shape[2]
    cs_pad = -(-cs // 64) * 64

    def w_in_rows(a):
        return jnp.pad(jnp.transpose(a[0]), ((0, cs_pad - cs), (0, 0)))

    big = [w_in_rows(w_in), w_attn_branch[0], w_conv_branch[0], w_out[0], w_ffn1_in[0], w_ffn1_out[0], w_ffn2_in[0],
           w_ffn2_out[0]]
    small_gather = jnp.concatenate(
        [meta_tokens.reshape(4, d), jnp.pad(conv_w.reshape(1, 3 * 128), ((0, 0), (0, d - 3 * 128))),
         jnp.zeros((11, d), F32)], axis=0)
    w_f1_in4, small4 = _all_gather([big[4].astype(BF16), small_gather])
    (second, rest), small4 = lax.optimization_barrier(
        (([big[5].astype(BF16)], [big[i].astype(BF16) for i in (0, 1, 2, 3, 6, 7)]), small4))
    second_gathered = _all_gather_async("all_gather_ffn1_out", second, 5)
    rest_gathered = _all_gather_async("all_gather_rest", rest, 1)
    meta_full = jnp.transpose(small4[:, 0:4].reshape(4, N_META, d // 4), (1, 0, 2)).reshape(N_META, d)
    conv_full = jnp.transpose(small4[:, 4, :3 * 128].reshape(4, 3, 128), (1, 0, 2)).reshape(3, CONV_W)
    conv_pad = jnp.pad(conv_full, ((0, 5), (0, 0)))
    b_pad = jnp.pad(b_forget, ((0, 0), (0, 128 - HEADS)))

    h0, n1, n1_t = _embed_norm(x[0], meta_full, g_ffn1_pre)
    ab1, s1, s1_t = _ffn_in("ffn1_in_fwd", n1, w_f1_in4)
    w_f1_out = second_gathered(s1, [0])[0].reshape(-1, d)
    f1, h1, u, u_t = _mm_resid_norm("ffn1_out_fwd", s1, w_f1_out, h0, g_ffn1_post, 0.5, g_mix_pre)

    w_in4, w_ab4, w_cb4, w_out4, w_f2_in4, w_f2_out4 = rest_gathered(u, range(6))
    w_in_t = w_in4[:, :cs].reshape(4 * cs, d)
    g_lo = f_lo + HEADS + 3 * CONV_W
    w_in_pad = jnp.concatenate(
        [w_in_t[:f_lo], w_in_t[g_lo:], w_in_t[f_lo + HEADS:g_lo], w_in_t[f_lo:f_lo + HEADS],
         jnp.zeros((F_PAD - HEADS, d), BF16)], axis=0)
    w_ab = jnp.transpose(w_ab4, (1, 0, 2)).reshape(ATTN_W, d)
    w_cb = jnp.transpose(w_cb4, (1, 0, 2)).reshape(CONV_W, d)
    w_out_full = w_out4.reshape(d, d)
    w_f2_out = w_f2_out4.reshape(-1, d)
    qkv, z = _in_proj(u, w_in_pad)
    f_col = z.shape[1] - F_PAD
    f_cum = _gate_prep(z, b_pad, f_col)
    f_heads = f_cum[:, :HEADS]
    o, lse = _attn_fwd(qkv, *_attn_bias_operands(f_heads))
    g, g_t = _conv_gate(z, conv_pad)
    mp, mp_t, o_t = _branch_mix(z, o, g, w_ab, w_cb, d)
    mixed, h2, n2, n2_t = _mm_resid_norm("mix_out_fwd", mp, w_out_full, h1, g_mix_post, 1.0, g_ffn2_pre)
    ab2, s2_t, (f2, h3) = _ffn_fwd("ffn2", n2, w_f2_in4, w_f2_out, h2, g_ffn2_post, None)
    dh3, df2, dg_f2_post, loss_part = _loss_norm_bwd(h3, loss_target[0], f2, g_ffn2_post, 0.5)

    reduced = {}

    def reduce_scatter(label, tags, slots, sequencer_id, hold=None, got=None, after=None):
        if got is None:
            got = _pair_send_halves(f"grad_pair_exchange_{label}", slots)
        else:
            got, _ = lax.optimization_barrier((got, after))
        sums = [_pair_add(tag, s, a, c_arr, F32 if tag == "small" else BF16) for tag, s, a in zip(tags, slots, got)]
        sums, hold = lax.optimization_barrier((sums, hold))
        if sequencer_id is None:
            arrived = _chip_scatter(f"grad_chip_scatter_{label}", sums)
        else:
            arrived = _chip_scatter_async(f"grad_chip_scatter_{label}", sums, sequencer_id)
        mine = [_chip_add(tag, a) for tag, a in zip(tags, arrived)]
        reduced.update(zip(tags, zip(mine, _pair_swap(f"grad_pair_swap_{label}", mine))))
        return hold

    dab2, dw_f2_in, dw_f2_out = _ffn_bwd_weights("ffn2", df2, ab2, s2_t, n2_t, w_f2_in4, w_f2_out)
    ffn2_slots = [dw_f2_in, dw_f2_out.reshape(4, -1, d)]
    ffn2_got = _pair_send_halves_async("grad_pair_exchange_ffn2", ffn2_slots, 6)
    dh2, dg_f2_pre, dmixed, dg_mix_post = _mm_nt_norm_bwd(
        "ffn2_in_bwd", dab2, w_f2_in4, h2, g_ffn2_pre, dh3, post=(mixed, g_mix_post, 1.0))
    reduce_scatter("ffn2", ["w_ffn2_in", "w_ffn2_out"], ffn2_slots, 2, got=ffn2_got, after=dh2)
    dw_out = _weight_grad("mix_dw_out", mp_t, dmixed, d)
    dya, dyc, dgates, do, dgconv = _branch_bwd(z, o, g, dmixed, w_out_full, w_ab, w_cb, d)
    dw_ab = _weight_grad("mix_dw_attn_branch", o_t, dya, d)
    dw_cb = _weight_grad("mix_dw_conv_branch", g_t, dyc, d)
    dz_conv, dconv_w = _conv_bwd(z, dgconv, conv_pad)
    front = lax.broadcasted_iota(jnp.int32, (t, 1), 0) < ROW_PAD
    lse_heads = jnp.where(front, 1e9, lse[:, ::HEAD_DIM])
    dq, dk, dv, dfk, dfq = _attn_bwd(qkv, *_attn_bias_operands(f_heads, lse_heads), o, do)
    dz_f, db_forget = _gate_bwd(dfq, dfk, z, b_pad, f_col)
    dz_pieces = {"q": dq, "k": dk, "v": dv, "gates": dgates, "conv": dz_conv, "f": dz_f}
    dh1, dg_mix_pre, df1, dg_f1_post = _mix_in_bwd(
        list(dz_pieces.values()), w_in_pad, h1, g_mix_pre, dh2, (f1, g_ffn1_post, 0.5))
    dw_t = {name: _weight_grad_t(f"mix_dw_in_{name}", u_t, piece) for name, piece in dz_pieces.items()}
    dw_in_t = jnp.concatenate(
        [dw_t["q"], dw_t["k"], dw_t["v"], dw_t["f"][:HEADS], dw_t["conv"], dw_t["gates"]], axis=0)
    mix_slots = [jnp.pad(dw_in_t.reshape(4, cs, d), ((0, 0), (0, cs_pad - cs), (0, 0))),
                 jnp.transpose(dw_ab.reshape(ATTN_W, 4, d // 4), (1, 0, 2)),
                 jnp.transpose(dw_cb.reshape(CONV_W, 4, d // 4), (1, 0, 2)),
                 dw_out.reshape(4, d // 4, d)]
    mix_got = _pair_send_halves_async("grad_pair_exchange_mix", mix_slots, 7)
    dab1, dw_f1_in, dw_f1_out = _ffn_bwd_weights("ffn1", df1, ab1, s1_t, n1_t, w_f1_in4, w_f1_out)
    reduce_scatter("mix", ["w_in", "w_attn_branch", "w_conv_branch", "w_out"], mix_slots, 3, got=mix_got,
                   after=dw_f1_out)
    dab1 = reduce_scatter("ffn1", ["w_ffn1_in", "w_ffn1_out"], [dw_f1_in, dw_f1_out.reshape(4, -1, d)], 4, dab1)
    dh0, dg_f1_pre = _mm_nt_norm_bwd("ffn1_in_bwd", dab1, w_f1_in4, h0, g_ffn1_pre, dh1)
    grad_x = dh0[N_FRONT:][None]
    dmeta = dh0[ROW_PAD:N_FRONT]
    small_grad = jnp.stack([
        _pack_small(dmeta[:, j * (d // 4):(j + 1) * (d // 4)], dconv_w[:3, j * 128:(j + 1) * 128],
                    [dg_f1_pre, dg_f1_post, dg_mix_pre, dg_mix_post, dg_f2_pre, dg_f2_post], db_forget[:, :HEADS],
                    loss_part[0, 0])
        for j in range(4)])
    reduce_scatter("small", ["small"], [small_grad], None)
    tags =["w_in", "w_attn_branch", "w_conv_branch", "w_out", "w_ffn1_in", "w_ffn1_out", "w_ffn2_in", "w_ffn2_out", "small"]
    halves = [reduced[tag][0] for tag in tags]
    others = [reduced[tag][1] for tag in tags]

    small = [g_ffn1_pre, g_ffn1_post, g_mix_pre, g_mix_post, g_ffn2_pre, g_ffn2_post]
    small_m = [m_g_ffn1_pre, m_g_ffn1_post, m_g_mix_pre, m_g_mix_post, m_g_ffn2_pre, m_g_ffn2_post]
    small_v = [v_g_ffn1_pre, v_g_ffn1_post, v_g_mix_pre, v_g_mix_post, v_g_ffn2_pre, v_g_ffn2_post]
    ws = big + [_pack_small(meta_tokens, conv_w[0], small, b_forget)]
    ms = [w_in_rows(m_w_in), m_w_attn_branch[0], m_w_conv_branch[0], m_w_out[0], m_w_ffn1_in[0], m_w_ffn1_out[0],
          m_w_ffn2_in[0], m_w_ffn2_out[0], _pack_small(m_meta_tokens, m_conv_w[0], small_m, m_b_forget)]
    vs = [w_in_rows(v_w_in), v_w_attn_branch[0], v_w_conv_branch[0], v_w_out[0], v_w_ffn1_in[0], v_w_ffn1_out[0],
          v_w_ffn2_in[0], v_w_ffn2_out[0], _pack_small(v_meta_tokens, v_conv_w[0], small_v, v_b_forget)]
    updates = [_adamw(tag, w, a, b, m, v, c_arr) for tag, w, a, b, m, v in zip(tags, ws, halves, others, ms, vs)]

    def leaves(big_vals, small_block):
        meta, conv, gains, bf = _unpack_small(small_block)
        w_in_t_, w_ab_, w_cb_, w_out_, f1_in, f1_out, f2_in, f2_out = [b[None] for b in big_vals]
        w_in_ = jnp.transpose(w_in_t_[:, :cs], (0, 2, 1))
        return [meta, w_in_, bf, conv, w_ab_, w_cb_, w_out_, gains[0], gains[1], f1_in, f1_out,
                gains[2], gains[3], gains[4], gains[5], f2_in, f2_out]

    out_g, out_d, out_m, out_v = [leaves([u_[k] for u_ in updates[:8]], updates[8][k]) for k in range(4)]
    loss = updates[8][0][LOSS_ROW, 0]
    return (loss, grad_x, *out_g, *out_d, *out_m, *out_v)
```

```python
import functools

import jax
import jax.numpy as jnp
from jax import lax
from jax.experimental import pallas as pl
from jax.experimental.pallas import tpu as pltpu
from jax.experimental.pallas import tpu_sc as plsc

N_META = 16
ROW_PAD = 112
N_FRONT = ROW_PAD + N_META
HEADS = 8
HEAD_DIM = 64
ATTN_W = HEADS * HEAD_DIM
CONV_W = 512
NORM_EPS = 1e-6
ROW_TILE = 640
F_PAD = 128
ATTN_Q_GROUP = 4
ATTN_KV_GROUP = 4
NEG = -1e30
ADAM_LR = 0.001
ADAM_B1 = 0.9
ADAM_B2 = 0.999
ADAM_EPS = 1e-08
ADAM_WD = 0.01
ADAM_STEP = 10
VMEM_BIG = 56 * 1024 * 1024
MESH = pl.DeviceIdType.MESH
ANY = pl.BlockSpec(memory_space=pl.ANY)
F32 = jnp.float32
BF16 = jnp.bfloat16


def _params(sem, vmem=None):
    return pltpu.CompilerParams(dimension_semantics=sem, vmem_limit_bytes=vmem)


def _sigmoid(x):
    return 1.0 / (1.0 + jnp.exp(-x))


def _rstd(x):
    return lax.rsqrt(jnp.mean(x * x, axis=-1, keepdims=True) + NORM_EPS)


def _rms_bwd(x, g, dy):
    r = _rstd(x)
    xr = x * r
    gdy = g * dy
    dx = r * (gdy - xr * jnp.mean(xr * gdy, axis=-1, keepdims=True))
    return dx, jnp.sum(dy * xr, axis=0, keepdims=True)


def _dot(a, b):
    return jnp.dot(a, b, preferred_element_type=F32)


def _dot_nt(a, b):
    return lax.dot_general(a, b, (((1,), (1,)), ((), ())), preferred_element_type=F32)


def _k_tile(t):
    return 1664 if t % 1664 == 0 else ROW_TILE


def _place():
    x, y, c = lax.axis_index("x"), lax.axis_index("y"), lax.axis_index("c")
    chips = [(1 - x, y), (x, 1 - y), (1 - x, 1 - y)]
    return x, y, c, chips


def _all_gather(shards):
    n = len(shards)
    split = [s.reshape(2, s.shape[0] // 2, s.shape[1]) for s in shards]

    def body(*refs):
        ins, outs = refs[:n], refs[n:2 * n]
        send_sems, recv_sems = refs[2 * n:]
        x, y, c, chips = _place()
        me = 2 * x + y
        sibling = (x, y, 1 - c)

        def remote(i, k, slot, part, to, src=None):
            dst = outs[i].at[slot, part]
            return pltpu.make_async_remote_copy(
                src_ref=dst if src is None else src, dst_ref=dst,
                send_sem=send_sems.at[i, k], recv_sem=recv_sems.at[i, k],
                device_id=to, device_id_type=MESH)

        started = []
        for i in range(n):
            for k, (cx, cy) in enumerate(chips):
                cp = remote(i, k, me, c, (cx, cy, c), src=ins[i].at[c])
                cp.start()
                started.append(cp)
        for i in range(n):
            for k, (cx, cy) in enumerate(chips):
                remote(i, k, 2 * cx + cy, c, (x, y, c)).wait_recv()
                cp = remote(i, 3 + k, 2 * cx + cy, c, sibling)
                cp.start()
                started.append(cp)
        for i in range(n):
            for k, (cx, cy) in enumerate(chips):
                remote(i, 3 + k, 2 * cx + cy, 1 - c, (x, y, c)).wait_recv()
        for cp in started:
            cp.wait_send()

    outs = pl.pallas_call(
        body, name="all_gather_weights",
        out_shape=[jax.ShapeDtypeStruct((4,) + s.shape, s.dtype) for s in split],
        in_specs=[ANY] * n, out_specs=[ANY] * n,
        scratch_shapes=[pltpu.SemaphoreType.DMA((n, 6)), pltpu.SemaphoreType.DMA((n, 6))],
    )(*split)
    me =2 * lax.axis_index("x") + lax.axis_index("y")
    outs = [lax.dynamic_update_slice(o, s[None], (me, 0, 0, 0)) for o, s in zip(outs, split)]
    return [o.reshape((4,) + s.shape) for o, s in zip(outs, shards)]


def _all_gather_async(name, shards, collective_id):
    n = len(shards)
    split = [s.reshape(2, s.shape[0] // 2, s.shape[1]) for s in shards]
    ins = [jax.new_ref(s, memory_space=pltpu.MemorySpace.HBM) for s in split]
    outs = [jax.empty_ref(jax.ShapeDtypeStruct((4,) + s.shape, s.dtype), memory_space=pltpu.MemorySpace.HBM)
            for s in split]

    @pl.kernel(mesh=plsc.ScalarSubcoreMesh(axis_name="sequencer", num_cores=1), name=name,
               scratch_types=(pltpu.SemaphoreType.DMA((n, 6)), pltpu.SemaphoreType.DMA((n, 6))),
               compiler_params=pltpu.CompilerParams(collective_id=collective_id))
    def launch(send_sems, recv_sems):
        x, y, c, chips = _place()
        me = 2 * x + y
        sibling = (x, y, 1 - c)
        barrier = pltpu.get_barrier_semaphore()
        for peer in [(cx, cy, c) for cx, cy in chips] + [sibling]:
            pl.semaphore_signal(barrier, inc=1, device_id=peer, device_id_type=MESH)
        pl.semaphore_wait(barrier, 4)

        def remote(i, k, slot, part, to, src=None):
            dst = outs[i].at[slot, part]
            return pltpu.make_async_remote_copy(
                src_ref=dst if src is None else src, dst_ref=dst,
                send_sem=send_sems.at[i, k], recv_sem=recv_sems.at[i, k],
                device_id=to, device_id_type=MESH)

        started = []
        for i in range(n):
            for k, (cx, cy) in enumerate(chips):
                cp = remote(i, k, me, c, (cx, cy, c), src=ins[i].at[c])
                cp.start()
                started.append(cp)
        for i in range(n):
            for k, (cx, cy) in enumerate(chips):
                remote(i, k, 2 * cx + cy, c, (x, y, c)).wait_recv()
                cp = remote(i, 3 + k, 2 * cx + cy, c, sibling)
                cp.start()
                started.append(cp)
        for i in range(n):
            for k, (cx, cy) in enumerate(chips):
                remote(i, 3 + k, 2 * cx + cy, 1 - c, (x, y, c)).wait_recv()
        for cp in started:
            cp.wait_send()

    launch()
    raw = [o[...] for o in outs]

    def finish(after, which):
        arrived, _ = lax.optimization_barrier(([raw[i] for i in which], after))
        me = 2 * lax.axis_index("x") + lax.axis_index("y")
        gathered = [lax.dynamic_update_slice(a, split[i][None], (me, 0, 0, 0)) for a, i in zip(arrived, which)]
        return [g.reshape((4,) + shards[i].shape) for g, i in zip(gathered, which)]

    return finish


def _pair_send_halves(name, grads):
    n = len(grads)

    def body(*refs):
        ins, outs = refs[:n], refs[n:2 * n]
        send_sems, recv_sems = refs[2 * n:]
        x, y, c, _ = _place()
        cps = []
        for i in range(n):
            half = ins[i].shape[1] // 2
            cp = pltpu.make_async_remote_copy(
                src_ref=ins[i].at[:, pl.ds((1 - c) * half, half)], dst_ref=outs[i],
                send_sem=send_sems.at[i], recv_sem=recv_sems.at[i],
                device_id=(x, y, 1 - c), device_id_type=MESH)
            cp.start()
            cps.append(cp)
        for cp in cps:
            cp.wait()

    return pl.pallas_call(
        body, name=name,
        out_shape=[jax.ShapeDtypeStruct((4, g.shape[1] // 2, g.shape[2]), g.dtype) for g in grads],
        in_specs=[ANY] * n, out_specs=[ANY] * n,
        scratch_shapes=[pltpu.SemaphoreType.DMA((n,)), pltpu.SemaphoreType.DMA((n,))],
    )(*grads)


def _pair_send_halves_async(name, grads, collective_id):
    n = len(grads)
    ins = [jax.new_ref(g, memory_space=pltpu.MemorySpace.HBM) for g in grads]
    outs = [jax.empty_ref(jax.ShapeDtypeStruct((4, g.shape[1] // 2, g.shape[2]), g.dtype),
                          memory_space=pltpu.MemorySpace.HBM) for g in grads]

    @pl.kernel(mesh=plsc.ScalarSubcoreMesh(axis_name="sequencer", num_cores=1), name=name,
               scratch_types=(pltpu.SemaphoreType.DMA((n,)), pltpu.SemaphoreType.DMA((n,))),
               compiler_params=pltpu.CompilerParams(collective_id=collective_id))
    def launch(send_sems, recv_sems):
        x, y, c, _ = _place()
        barrier = pltpu.get_barrier_semaphore()
        pl.semaphore_signal(barrier, inc=1, device_id=(x, y, 1 - c), device_id_type=MESH)
        pl.semaphore_wait(barrier, 1)
        cps = []
        for i in range(n):
            half = ins[i].shape[1] // 2
            cp = pltpu.make_async_remote_copy(
                src_ref=ins[i].at[:, pl.ds((1 - c) * half, half)], dst_ref=outs[i],
                send_sem=send_sems.at[i], recv_sem=recv_sems.at[i],
                device_id=(x, y, 1 - c), device_id_type=MESH)
            cp.start()
            cps.append(cp)
        for cp in cps:
            cp.wait()

    launch()
    return [o[...] for o in outs]


def _chip_scatter(name, parts):
    n = len(parts)

    def body(*refs):
        _scatter_copies(refs[:n], refs[n:2 * n], *refs[2 * n:])

    arrived = pl.pallas_call(
        body, name=name,
        out_shape=[jax.ShapeDtypeStruct(p.shape, p.dtype) for p in parts],
        in_specs=[ANY] * n, out_specs=[ANY] * n,
        scratch_shapes=[pltpu.SemaphoreType.DMA((n, 3)), pltpu.SemaphoreType.DMA((n, 3))],
    )(*parts)
    return _own_slots(parts, arrived)


def _scatter_copies(ins, outs, send_sems, recv_sems):
    x, y, c, chips = _place()
    me = 2 * x + y
    sends = []
    for i in range(len(ins)):
        for k, (cx, cy) in enumerate(chips):
            cp = pltpu.make_async_remote_copy(
                src_ref=ins[i].at[2 * cx + cy], dst_ref=outs[i].at[me],
                send_sem=send_sems.at[i, k], recv_sem=recv_sems.at[i, k],
                device_id=(cx, cy, c), device_id_type=MESH)
            cp.start()
            sends.append(cp)
    for i in range(len(ins)):
        for k, (cx, cy) in enumerate(chips):
            got = outs[i].at[2 * cx + cy]
            pltpu.make_async_remote_copy(
                src_ref=got, dst_ref=got, send_sem=send_sems.at[i, k], recv_sem=recv_sems.at[i, k],
                device_id=(x, y, c), device_id_type=MESH).wait_recv()
    for cp in sends:
        cp.wait_send()


def _own_slots(parts, arrived):
    me = 2 * lax.axis_index("x") + lax.axis_index("y")
    return [lax.dynamic_update_slice(a, lax.dynamic_slice_in_dim(p, me, 1, axis=0), (me, 0, 0))
            for p, a in zip(parts, arrived)]


def _chip_scatter_async(name, parts, collective_id):
    n = len(parts)
    ins = [jax.new_ref(p, memory_space=pltpu.MemorySpace.HBM) for p in parts]
    outs = [jax.empty_ref(jax.ShapeDtypeStruct(p.shape, p.dtype), memory_space=pltpu.MemorySpace.HBM) for p in parts]

    @pl.kernel(mesh=plsc.ScalarSubcoreMesh(axis_name="sequencer", num_cores=1), name=name,
               scratch_types=(pltpu.SemaphoreType.DMA((n, 3)), pltpu.SemaphoreType.DMA((n, 3))),
               compiler_params=pltpu.CompilerParams(collective_id=collective_id))
    def launch(send_sems, recv_sems):
        x, y, c, chips = _place()
        barrier = pltpu.get_barrier_semaphore()
        for cx, cy in chips:
            pl.semaphore_signal(barrier, inc=1, device_id=(cx, cy, c), device_id_type=MESH)
        pl.semaphore_wait(barrier, 3)
        _scatter_copies(ins, outs, send_sems, recv_sems)

    launch()
    return _own_slots(parts, [o[...] for o in outs])


def _pair_swap(name, halves):
    n = len(halves)

    def body(*refs):
        ins, outs = refs[:n], refs[n:2 * n]
        send_sems, recv_sems = refs[2 * n:]
        x, y, c, _ = _place()
        cps = []
        for i in range(n):
            cp = pltpu.make_async_remote_copy(
                src_ref=ins[i], dst_ref=outs[i], send_sem=send_sems.at[i], recv_sem=recv_sems.at[i],
                device_id=(x, y, 1 - c), device_id_type=MESH)
            cp.start()
            cps.append(cp)
        for cp in cps:
            cp.wait()

    return pl.pallas_call(
        body, name=name,
        out_shape=[jax.ShapeDtypeStruct(h.shape, h.dtype) for h in halves],
        in_specs=[ANY] * n, out_specs=[ANY] * n,
        scratch_shapes=[pltpu.SemaphoreType.DMA((n,)), pltpu.SemaphoreType.DMA((n,))],
    )(*halves)


def _row_block(rows, cols, n_bufs, budget=20 * 1024 * 1024):
    best = min(rows, 16)
    for b in range(16, rows + 1, 16):
        if rows % b == 0 and 2 * n_bufs * b * cols * 4 <= budget:
            best = b
    return best


def _pair_add(tag, grad, got, c_arr, out_dtype):
    _, rows, cols = grad.shape
    half = rows // 2
    bh = _row_block(half, cols, 3)
    nb = half // bh

    def body(c_ref, g_ref, a_ref, o_ref):
        o_ref[...] = (g_ref[...] + a_ref[...]).astype(out_dtype)

    return pl.pallas_call(
        body, name=f"pair_add_{tag}",
        out_shape=jax.ShapeDtypeStruct((4, half, cols), out_dtype),
        grid_spec=pltpu.PrefetchScalarGridSpec(
            num_scalar_prefetch=1, grid=(4, nb),
            in_specs=[pl.BlockSpec((None, bh, cols), lambda j, r, c: (j, c[0] * nb + r, 0)),
                      pl.BlockSpec((None, bh, cols), lambda j, r, c: (j, r, 0))],
            out_specs=pl.BlockSpec((None, bh, cols), lambda j, r, c: (j, r, 0))),
        compiler_params=_params(("parallel", "parallel")),
    )(c_arr, grad, got)


def _chip_add(tag, parts):
    _, half, cols = parts.shape
    bh = _row_block(half, cols, 5)

    def body(p_ref, o_ref):
        a, b, c, d = [p_ref[j].astype(F32) for j in range(4)]
        o_ref[...] = ((a + b) + c) + d

    return pl.pallas_call(
        body, name=f"chip_add_{tag}",
        out_shape=jax.ShapeDtypeStruct((half, cols), F32),
        grid=(half // bh,),
        in_specs=[pl.BlockSpec((4, bh, cols), lambda r: (0, r, 0))],
        out_specs=pl.BlockSpec((bh, cols), lambda r: (r, 0)),
        compiler_params=_params(("parallel",)),
    )(parts)


def _adamw(tag, w, mine, theirs, m, v, c_arr):
    rows, cols = w.shape
    half = rows // 2
    br = _row_block(half, cols, 9)
    nb = half // br

    def body(c_ref, w_ref, a_ref, b_ref, m_ref, v_ref, g_ref, d_ref, mo_ref, vo_ref):
        own = (pl.program_id(0) // nb) == c_ref[0]
        g = jnp.where(own, a_ref[...], b_ref[...])
        g_ref[...] = g
        m_new = ADAM_B1 * m_ref[...] + (1.0 - ADAM_B1) * g
        v_new = ADAM_B2 * v_ref[...] + (1.0 - ADAM_B2) * (g * g)
        m_hat = m_new / (1.0 - ADAM_B1 ** ADAM_STEP)
        v_hat = v_new / (1.0 - ADAM_B2 ** ADAM_STEP)
        d_ref[...] = -ADAM_LR * (m_hat / (jnp.sqrt(v_hat) + ADAM_EPS) + ADAM_WD * w_ref[...])
        mo_ref[...] = m_new
        vo_ref[...] = v_new

    spec = pl.BlockSpec((br, cols), lambda r, c: (r, 0))
    mine_spec = pl.BlockSpec((br, cols), lambda r, c: (jnp.clip(r - c[0] * nb, 0, nb - 1), 0))
    theirs_spec = pl.BlockSpec((br, cols), lambda r, c: (jnp.clip(r - (1 - c[0]) * nb, 0, nb - 1), 0))
    return pl.pallas_call(
        body, name=f"adamw_{tag}",
        out_shape=[jax.ShapeDtypeStruct((rows, cols), F32)] * 4,
        grid_spec=pltpu.PrefetchScalarGridSpec(
            num_scalar_prefetch=1, grid=(rows // br,),
            in_specs=[spec, mine_spec, theirs_spec, spec, spec], out_specs=[spec] * 4),
        compiler_params=_params(("arbitrary",)),
    )(c_arr, w, mine, theirs, m, v)


def _matmul(name, x, w, out_shape, grid, x_spec, w_spec, o_spec, *, nt=False, vmem=None):
    nk = grid[2]
    acc_shape = tuple(d for d in o_spec.block_shape if d is not None)

    def body(x_ref, w_ref, o_ref, acc_ref):
        k = pl.program_id(2)
        part = _dot_nt(x_ref[...], w_ref[...]) if nt else _dot(x_ref[...], w_ref[...])
        if nk == 1:
            o_ref[...] = part.astype(o_ref.dtype)
        else:
            @pl.when(k == 0)
            def _():
                acc_ref[...] = part

            @pl.when(k > 0)
            def _():
                acc_ref[...] += part

            @pl.when(k == nk - 1)
            def _():
                o_ref[...] = acc_ref[...].astype(o_ref.dtype)

    return pl.pallas_call(
        body, name=name, out_shape=out_shape, grid=grid,
        in_specs=[x_spec, w_spec], out_specs=o_spec,
        scratch_shapes=[pltpu.VMEM(acc_shape if nk > 1 else (8, 128), F32)],
        compiler_params=_params(("parallel", "parallel", "arbitrary"), vmem),
    )(x, w)


def _weight_grad(name, xt, dy, bn, out_rows=None):
    m, t = xt.shape
    n = dy.shape[1]
    bm = m if out_rows is None else out_rows
    bk = _k_tile(t)
    return _matmul(
        name, xt, dy, jax.ShapeDtypeStruct((m, n), F32), (m // bm, n // bn, t // bk),
        pl.BlockSpec((bm, bk), lambda a, b, k: (a, k)),
        pl.BlockSpec((bk, bn), lambda a, b, k: (k, b)),
        pl.BlockSpec((bm, bn), lambda a, b, k: (a, b)), vmem=VMEM_BIG)


def _weight_grad_t(name, xt, dy):
    m, t = xt.shape
    n = dy.shape[1]
    bn = min(n, 512)
    bk = _k_tile(t)
    nk = t // bk

    def body(x_ref, dy_ref, o_ref, acc_ref):
        k = pl.program_id(1)
        part = _dot(x_ref[...], dy_ref[...].astype(BF16))

        @pl.when(k == 0)
        def _():
            acc_ref[...] = part

        @pl.when(k > 0)
        def _():
            acc_ref[...] += part

        @pl.when(k == nk - 1)
        def _():
            o_ref[...] = acc_ref[...].T

    return pl.pallas_call(
        body, name=name, out_shape=jax.ShapeDtypeStruct((n, m), F32), grid=(n // bn, nk),
        in_specs=[pl.BlockSpec((m, bk), lambda b, k: (0, k)), pl.BlockSpec((bk, bn), lambda b, k: (k, b))],
        out_specs=pl.BlockSpec((bn, m), lambda b, k: (b, 0)),
        scratch_shapes=[pltpu.VMEM((m, bn), F32)],
        compiler_params=_params(("parallel", "arbitrary"), VMEM_BIG),
    )(xt, dy)


def _mix_in_bwd(pieces, wt, h, g, dh_in, post):
    t, d = h.shape
    tm = ROW_TILE // 2
    widths = [p.shape[1] for p in pieces]
    n = len(pieces)

    def body(*refs):
        dy_refs = refs[:n]
        w_ref, h_ref, g_ref, dhi_ref, xp_ref, gp_ref, dh_ref, dg_ref, dxp_ref, dgp_ref = refs[n:]
        first = pl.program_id(0) == 0

        @pl.when(first)
        def _():
            dg_ref[...] = jnp.zeros_like(dg_ref)

        dn, off = None, 0
        for dy_ref, wd in zip(dy_refs, widths):
            part = _dot(dy_ref[...].astype(BF16), w_ref[off:off + wd, :])
            dn = part if dn is None else dn + part
            off += wd
        dx, dg = _rms_bwd(h_ref[...], g_ref[...], dn)
        dh = dhi_ref[...] + dx
        dh_ref[...] = dh
        dg_ref[...] += dg
        _next_post_norm_bwd(dh, (xp_ref, gp_ref, dxp_ref, dgp_ref), post[2], first)

    row = pl.BlockSpec((tm, d), lambda i: (i, 0))
    vec = pl.BlockSpec((1, d), lambda i: (0, 0))
    return pl.pallas_call(
        body, name="mix_in_bwd",
        out_shape=[jax.ShapeDtypeStruct((t, d), F32), jax.ShapeDtypeStruct((1, d), F32),
                   jax.ShapeDtypeStruct((t, d), BF16), jax.ShapeDtypeStruct((1, d), F32)],
        grid=(t // tm,),
        in_specs=[pl.BlockSpec((tm, wd), lambda i: (i, 0)) for wd in widths]
        + [pl.BlockSpec(wt.shape, lambda i: (0, 0)), row, vec, row, row, vec],
        out_specs=[row, vec, row, vec],
        compiler_params=_params(("arbitrary",), VMEM_BIG),
    )(*pieces, wt, h, g, dh_in, post[0], post[1])


def _read_token_rows(src_hbm, buf, sems, i, n):
    tm = buf.shape[1]

    def first_tile():
        return pltpu.make_async_copy(src_hbm.at[pl.ds(0, tm - N_FRONT)], buf.at[0, pl.ds(N_FRONT, tm - N_FRONT)],
                                     sems.at[0])

    def tile(j):
        return pltpu.make_async_copy(src_hbm.at[pl.ds(pl.multiple_of(j * tm - N_FRONT, N_FRONT), tm)],
                                     buf.at[j % 2], sems.at[j % 2])

    @pl.when(i == 0)
    def _():
        buf[0, 0:N_FRONT, :] = jnp.zeros((N_FRONT, buf.shape[2]), buf.dtype)
        first_tile().start()

    @pl.when(i + 1 < n)
    def _():
        tile(i + 1).start()

    @pl.when(i == 0)
    def _():
        first_tile().wait()

    @pl.when(i > 0)
    def _():
        tile(i).wait()

    return buf.at[i % 2]


def _embed_norm(x, meta, g):
    seq, d = x.shape
    t = seq + N_FRONT
    tm = ROW_TILE

    def body(x_hbm, meta_ref, g_ref, h_ref, n_ref, nt_ref, buf, sems):
        i = pl.program_id(0)
        rows = _read_token_rows(x_hbm, buf, sems, i, t // tm)

        @pl.when(i == 0)
        def _():
            buf[0, ROW_PAD:N_FRONT, :] = meta_ref[...]

        h = rows[...]
        h_ref[...] = h
        y = h * _rstd(h) * g_ref[...]
        n_ref[...] = y.astype(BF16)
        nt_ref[...] = y.T.astype(BF16)

    row = pl.BlockSpec((tm, d), lambda i: (i, 0))
    return pl.pallas_call(
        body, name="embed_and_ffn1_pre_norm",
        out_shape=[jax.ShapeDtypeStruct((t, d), F32), jax.ShapeDtypeStruct((t, d), BF16),
                   jax.ShapeDtypeStruct((d, t), BF16)],
        grid=(t // tm,),
        in_specs=[ANY, pl.BlockSpec((N_META, d), lambda i: (0, 0)), pl.BlockSpec((1, d), lambda i: (0, 0))],
        out_specs=[row, row, pl.BlockSpec((d, tm), lambda i: (0, i))],
        scratch_shapes=[pltpu.VMEM((2, tm, d), F32), pltpu.SemaphoreType.DMA((2,))],
        compiler_params=_params(("arbitrary",)),
    )(x, meta, g)


def _slot_of(kk):
    return (kk % 2) * 2 + kk // 2


def _ffn_in(name, n, w4):
    t, d = n.shape
    cw = w4.shape[2]
    tm = ROW_TILE

    def body(x_ref, wg_ref, wu_ref, ab_ref, s_ref, st_ref):
        x = x_ref[...]
        a = _dot(x, wg_ref[...])
        b = _dot(x, wu_ref[...])
        ab_ref[:, :cw] = a.astype(BF16)
        ab_ref[:, cw:] = b.astype(BF16)
        s = a * _sigmoid(a) * b
        s_ref[...] = s.astype(BF16)
        st_ref[...] = s.T.astype(BF16)

    return pl.pallas_call(
        body, name=name,
        out_shape=[jax.ShapeDtypeStruct((t, 4 * cw), BF16), jax.ShapeDtypeStruct((t, 2 * cw), BF16),
                   jax.ShapeDtypeStruct((2 * cw, t), BF16)],
        grid=(2, t // tm),
        in_specs=[pl.BlockSpec((tm, d), lambda j, i: (i, 0)),
                  pl.BlockSpec((None, d, cw), lambda j, i: (j, 0, 0)),
                  pl.BlockSpec((None, d, cw), lambda j, i: (2 + j, 0, 0))],
        out_specs=[pl.BlockSpec((tm, 2 * cw), lambda j, i: (i, j)),
                   pl.BlockSpec((tm, cw), lambda j, i: (i, j)),
                   pl.BlockSpec((cw, tm), lambda j, i: (j, i))],
        compiler_params=_params(("parallel", "parallel"), VMEM_BIG),
    )(n, w4, w4)


def _mm_resid_norm(name, x, w, h, g_post, alpha, g_next):
    t, kdim = x.shape
    d = w.shape[1]
    tm = ROW_TILE
    with_next = g_next is not None

    def body(x_ref, w_ref, h_ref, gp_ref, gn_ref, f_ref, hn_ref, *rest):
        f = _dot(x_ref[...], w_ref[...])
        f_ref[...] = f
        hn = h_ref[...] + alpha * (f * _rstd(f) * gp_ref[...])
        hn_ref[...] = hn
        if with_next:
            y = hn * _rstd(hn) * gn_ref[...]
            rest[0][...] = y.astype(BF16)
            rest[1][...] = y.T.astype(BF16)

    row = lambda i: (i, 0)
    vec = pl.BlockSpec((1, d), lambda i: (0, 0))
    out_shape = [jax.ShapeDtypeStruct((t, d), F32), jax.ShapeDtypeStruct((t, d), F32)]
    out_specs = [pl.BlockSpec((tm, d), row), pl.BlockSpec((tm, d), row)]
    if with_next:
        out_shape += [jax.ShapeDtypeStruct((t, d), BF16), jax.ShapeDtypeStruct((d, t), BF16)]
        out_specs += [pl.BlockSpec((tm, d), row), pl.BlockSpec((d, tm), lambda i: (0, i))]
    return pl.pallas_call(
        body, name=name, out_shape=out_shape, grid=(t // tm,),
        in_specs=[pl.BlockSpec((tm, kdim), row), pl.BlockSpec((kdim, d), lambda i: (0, 0)),
                  pl.BlockSpec((tm, d), row), vec, vec],
        out_specs=out_specs,
        compiler_params=_params(("parallel",), VMEM_BIG),
    )(x, w, h, g_post, g_post if g_next is None else g_next)


def _in_proj(u, w):
    t, d = u.shape
    nz = w.shape[0]
    nq = 3 * ATTN_W
    tm = ROW_TILE // 2

    def body(u_ref, w_ref, qkv_ref, z_ref):
        qkv_ref[...] = _dot_nt(u_ref[...], w_ref[0:nq, :]).astype(BF16)
        z_ref[...] = _dot_nt(u_ref[...], w_ref[nq:, :])

    return pl.pallas_call(
        body, name="mix_in_proj",
        out_shape=[jax.ShapeDtypeStruct((t, nq), BF16), jax.ShapeDtypeStruct((t, nz - nq), F32)],
        grid=(t // tm,),
        in_specs=[pl.BlockSpec((tm, d), lambda i: (i, 0)), pl.BlockSpec((nz, d), lambda i: (0, 0))],
        out_specs=[pl.BlockSpec((tm, nq), lambda i: (i, 0)), pl.BlockSpec((tm, nz - nq), lambda i: (i, 0))],
        compiler_params=_params(("parallel",), VMEM_BIG),
    )(u, w)


def _gate_prep(z, b_pad, f_col):
    t = z.shape[0]
    tm = ROW_TILE

    def body(z_ref, b_ref, f_ref, carry_ref):
        i = pl.program_id(0)

        @pl.when(i == 0)
        def _():
            carry_ref[...] = jnp.zeros_like(carry_ref)

        xs = z_ref[...] + b_ref[...]
        logf = jnp.minimum(xs, 0.0) - jnp.log(1.0 + jnp.exp(-jnp.abs(xs)))
        row = i * tm + lax.broadcasted_iota(jnp.int32, (tm, 1), 0)
        logf = jnp.where(row >= ROW_PAD, logf, 0.0)
        tri = (lax.broadcasted_iota(jnp.int32, (tm, tm), 0) >= lax.broadcasted_iota(jnp.int32, (tm, tm), 1))
        f = jnp.dot(tri.astype(F32), logf, preferred_element_type=F32, precision=lax.Precision.HIGHEST)
        f = f + carry_ref[0:1, :]
        f_ref[...] = f
        carry_ref[...] = jnp.broadcast_to(f[tm - 1:tm, :], carry_ref.shape)

    return pl.pallas_call(
        body, name="forget_gate_cumsum", out_shape=jax.ShapeDtypeStruct((t, 128), F32),
        grid=(t // tm,),
        in_specs=[pl.BlockSpec((tm, 128), lambda i: (i, f_col // 128)), pl.BlockSpec((1, 128), lambda i: (0, 0))],
        out_specs=pl.BlockSpec((tm, 128), lambda i: (i, 0)),
        scratch_shapes=[pltpu.VMEM((8, 128), F32)],
        compiler_params=_params(("arbitrary",)),
    )(z, b_pad)


def _lane_halves():
    lane = lax.broadcasted_iota(jnp.int32, (1, 128), 1)
    return lane < HEAD_DIM


def _causal_mask(tq, tk, row0=0):
    row = row0 + lax.broadcasted_iota(jnp.int32, (tq, 1), 0)
    col = lax.broadcasted_iota(jnp.int32, (1, tk), 1)
    return col <= row


def _lane_one(lane):
    return (lax.broadcasted_iota(jnp.int32, (1, 128), 1) == lane).astype(BF16)


def _split3(x):
    hi = x.astype(BF16)
    rest = x - hi.astype(F32)
    mid = rest.astype(BF16)
    return hi, mid, (rest - mid.astype(F32)).astype(BF16)


def _split3_glue(x):
    hi = lax.reduce_precision(x, 8, 7)
    mid = lax.reduce_precision(x - hi, 8, 7)
    lo = lax.reduce_precision((x - hi) - mid, 8, 7)
    return hi.astype(BF16), mid.astype(BF16), lo.astype(BF16)


def _aug_pairs(cols):
    t = cols[0].shape[0]
    a = jnp.pad(jnp.stack(cols, axis=2), ((0, 0), (0, 0), (0, HEAD_DIM - len(cols))))
    a = a.reshape(t, 4, 2, HEAD_DIM)[:, :, ::-1, :]
    return jnp.transpose(a.reshape(t, 4, 128), (1, 0, 2))


def _attn_bias_operands(f_heads, lse_heads=None):
    t = f_heads.shape[0]
    one = jnp.ones((t, HEADS), BF16)
    row = lax.broadcasted_iota(jnp.int32, (t, 1), 0)
    fq = _split3_glue(f_heads)
    fk = _split3_glue(jnp.where(row < ROW_PAD, 1e9, f_heads))
    q_cols = list(fq) + [one] * 3
    k_cols = [one] * 3 + [-c for c in fk]
    if lse_heads is not None:
        q_cols += [-c for c in _split3_glue(lse_heads)]
        k_cols += [one] * 3
    return _aug_pairs(q_cols), _aug_pairs(k_cols)


def _attn_fwd(z, aug_q, aug_k):
    t = z.shape[0]
    tq = tk = ROW_TILE
    nq = t // tq
    grp = ATTN_KV_GROUP
    steps = [(qi, ka) for qi in range(nq) for ka in range(0, qi + 1, grp)]
    q_tab = jnp.array([qi for qi, _ in steps], jnp.int32)
    k_tab = jnp.array([ka for _, ka in steps], jnp.int32)

    def body(qt_ref, kt_ref, q_ref, *refs):
        k_refs, v_refs, aq_ref, ak_refs = refs[:grp], refs[grp:2 * grp], refs[2 * grp], refs[2 * grp + 1:3 * grp + 1]
        o_ref, lse_ref, m_ref, l_ref, acc_ref = refs[3 * grp + 1:]
        step = pl.program_id(1)
        qi, ka = qt_ref[step], kt_ref[step]

        @pl.when(ka == 0)
        def _():
            m_ref[...] = jnp.full_like(m_ref, NEG)
            l_ref[...] = jnp.zeros_like(l_ref)
            acc_ref[...] = jnp.zeros_like(acc_ref)

        def sweep(diagonal):
            first = _lane_halves()
            halves = (first, jnp.logical_not(first))
            q = (q_ref[...] * (HEAD_DIM ** -0.5)).astype(BF16)
            aq = aq_ref[...]
            qa = [jnp.where(lanes, q, aq) for lanes in halves]
            blocks = list(zip(k_refs, v_refs, ak_refs, diagonal))
            s = []
            for k_ref, _, ak_ref, diag in blocks:
                k, ak = k_ref[...].astype(BF16), ak_ref[...]
                for hh, lanes in enumerate(halves):
                    s_c = _dot_nt(qa[hh], jnp.where(lanes, k, ak))
                    s.append(jnp.where(_causal_mask(tq, tk), s_c, NEG) if diag else s_c)
            nb = len(blocks)
            m_prev = [m_ref[:, c0:c0 + 1] for c0 in (0, HEAD_DIM)]
            m_new = []
            for hh in range(2):
                m_h = m_prev[hh]
                for b in range(nb):
                    m_h = jnp.maximum(m_h, jnp.max(s[2 * b + hh], axis=1, keepdims=True))
                m_new.append(m_h)
            pv = [None, None]
            for b, (_, v_ref, _, _) in enumerate(blocks):
                v = v_ref[...].astype(BF16)
                for hh, (lanes, a0) in enumerate(zip(halves, (HEAD_DIM, 0))):
                    part = _dot(jnp.exp(s[2 * b + hh] - m_new[hh]).astype(BF16), jnp.where(lanes, v, _lane_one(a0)))
                    pv[hh] = part if pv[hh] is None else pv[hh] + part
            al0, al1 = [jnp.exp(mp - m_h) for mp, m_h in zip(m_prev, m_new)]
            l0 = al0 * l_ref[:, 0:1] + pv[0][:, HEAD_DIM:HEAD_DIM + 1]
            l1 = al1 * l_ref[:, HEAD_DIM:HEAD_DIM + 1] + pv[1][:, 0:1]
            acc_ref[...] = acc_ref[...] * jnp.where(first, al0, al1) + jnp.where(first, pv[0], pv[1])
            m_ref[...] = jnp.where(first, m_new[0], m_new[1])
            l_ref[...] = jnp.where(first, l0, l1)

        def finish():
            o_ref[...] = acc_ref[...] / l_ref[...]
            lse_ref[...] = m_ref[...] + jnp.log(l_ref[...])

        @pl.when(ka + grp - 1 < qi)
        def _():
            sweep((False,) * grp)

        for nb in range(1, grp + 1):
            @pl.when(ka + nb - 1 == qi)
            def _(nb=nb):
                sweep((False,) * (nb - 1) + (True,))
                finish()

    def kblock(j):
        return lambda s, qt, kt: jnp.minimum(kt[s] + j, qt[s])

    kbs = [kblock(j) for j in range(grp)]
    return pl.pallas_call(
        body, name="attention_fwd",
        out_shape=[jax.ShapeDtypeStruct((t, ATTN_W), F32), jax.ShapeDtypeStruct((t, ATTN_W), F32)],
        grid_spec=pltpu.PrefetchScalarGridSpec(
            num_scalar_prefetch=2, grid=(4, len(steps)),
            in_specs=[pl.BlockSpec((tq, 128), lambda p, s, qt, kt: (qt[s], p))]
            + [pl.BlockSpec((tk, 128), functools.partial(lambda p, s, qt, kt, kb: (kb(s, qt, kt), 4 + p), kb=kb))
               for kb in kbs]
            + [pl.BlockSpec((tk, 128), functools.partial(lambda p, s, qt, kt, kb: (kb(s, qt, kt), 8 + p), kb=kb))
               for kb in kbs]
            + [pl.BlockSpec((None, tq, 128), lambda p, s, qt, kt: (p, qt[s], 0))]
            + [pl.BlockSpec((None, tk, 128), functools.partial(lambda p, s, qt, kt, kb: (p, kb(s, qt, kt), 0), kb=kb))
               for kb in kbs],
            out_specs=[pl.BlockSpec((tq, 128), lambda p, s, qt, kt: (qt[s], p)),
                       pl.BlockSpec((tq, 128), lambda p, s, qt, kt: (qt[s], p))],
            scratch_shapes=[pltpu.VMEM((tq, 128), F32)] * 3),
        compiler_params=_params(("parallel", "arbitrary"), VMEM_BIG),
    )(q_tab, k_tab, z, *([z] * (2 * grp)), aug_q, *([aug_k] * grp))


def _attn_bwd(z, aug_q, aug_k, o, do):
    t = z.shape[0]
    tq = tk = ROW_TILE
    nq = t // tq
    grp = ATTN_Q_GROUP
    steps = [(qa, ki) for ki in range(nq) for qa in range(ki, nq, grp)]
    q_tab = jnp.array([qa for qa, _ in steps], jnp.int32)
    k_tab = jnp.array([ki for _, ki in steps], jnp.int32)
    tn = (((0,), (0,)), ((), ()))

    def body(qt_ref, kt_ref, *refs):
        q_refs, (k_ref, v_ref) = refs[:grp], refs[grp:grp + 2]
        aq_refs, ak_ref = refs[grp + 2:2 * grp + 2], refs[2 * grp + 2]
        o_refs, do_refs = refs[2 * grp + 3:3 * grp + 3], refs[3 * grp + 3:4 * grp + 3]
        dq_ref, dk_ref, dv_ref, dfk_ref, dfq_ref = refs[4 * grp + 3:]
        step = pl.program_id(1)
        qa, ki = qt_ref[step], kt_ref[step]

        def rows(j):
            return pl.ds(pl.multiple_of((qa + j) * tq, tq), tq)

        for j in range(grp):
            @pl.when((ki == 0) & (qa + j < nq))
            def _(j=j):
                dq_ref[rows(j), :] = jnp.zeros((tq, 128), F32)
                dfq_ref[rows(j), :] = jnp.zeros((tq, 128), F32)

        @pl.when(qa == ki)
        def _():
            dk_ref[...] = jnp.zeros_like(dk_ref)
            dv_ref[...] = jnp.zeros_like(dv_ref)
            dfk_ref[...] = jnp.zeros_like(dfk_ref)

        def sweep(nb, diagonal):
            first = _lane_halves()
            lane = lax.broadcasted_iota(jnp.int32, (1, 128), 1)
            scale = HEAD_DIM ** -0.5
            halves = (first, jnp.logical_not(first))
            spare = (HEAD_DIM, 0)
            k = k_ref[...].astype(BF16)
            v = v_ref[...].astype(BF16)
            ak = ak_ref[...]
            k_bias = [jnp.where(lanes, k, ak) for lanes in halves]
            k_ones = [jnp.where(lanes, k, _lane_one(a)) for lanes, a in zip(halves, spare)]
            v_ones = [jnp.where(lanes, v, ((lane >= a) & (lane < a + 3)).astype(BF16)) for lanes, a in zip(halves, spare)]
            chains = [(j, hh) for j in range(nb) for hh in range(2)]
            q16, do16, dos = [], [], []
            for j in range(nb):
                q16.append((q_refs[j][...] * scale).astype(BF16))
                do_ = do_refs[j][...]
                do16.append(do_.astype(BF16))
                od = o_refs[j][...] * do_
                for lanes, a in zip(halves, spare):
                    d_hi, d_mid, d_lo = _split3(jnp.sum(jnp.where(lanes, od, 0.0), axis=1, keepdims=True))
                    minus_delta = jnp.where(lane == a, -d_hi, jnp.where(lane == a + 1, -d_mid,
                                            jnp.where(lane == a + 2, -d_lo, jnp.zeros((), BF16))))
                    dos.append(jnp.where(lanes, do16[j], minus_delta))
            s = [_dot_nt(jnp.where(halves[hh], q16[j], aq_refs[j][...]), k_bias[hh]) for j, hh in chains]
            dp = [_dot_nt(dos[2 * j + hh], v_ones[hh]) for j, hh in chains]
            p = [jnp.exp(s_c) for s_c in s]
            if diagonal:
                p = [jnp.where(_causal_mask(tq, tk), p_c, 0.0) if j == 0 else p_c for p_c, (j, _) in zip(p, chains)]
            ds16 = [(p_c * dp_c).astype(BF16) for p_c, dp_c in zip(p, dp)]
            dv, dk = [None, None], [None, None]
            for c, (j, hh) in enumerate(chains):
                lanes = halves[hh]
                dv_c = lax.dot_general(jnp.where(lanes, do16[j], jnp.zeros((), BF16)), p[c].astype(BF16), tn,
                                       preferred_element_type=F32)
                dk_c = lax.dot_general(jnp.where(lanes, q16[j], _lane_one(spare[hh])), ds16[c], tn,
                                       preferred_element_type=F32)
                dv[hh] = dv_c if dv[hh] is None else dv[hh] + dv_c
                dk[hh] = dk_c if dk[hh] is None else dk[hh] + dk_c
            dv = [x.T for x in dv]
            dk = [x.T for x in dk]
            for j in range(nb):
                dq0, dq1 = [_dot(ds16[2 * j + hh], k_ones[hh]) for hh in range(2)]
                dq_ref[rows(j), :] += jnp.where(first, dq0, dq1) * scale
                dfq_ref[rows(j), :] += jnp.where(first, dq0[:, HEAD_DIM:HEAD_DIM + 1], dq1[:, 0:1])
            dk_ref[...] += jnp.where(first, dk[0], dk[1])
            dfk_ref[...] += jnp.where(first, dk[0][:, HEAD_DIM:HEAD_DIM + 1], dk[1][:, 0:1])
            dv_ref[...] += dv[0] + dv[1]

        for nb in range(1, grp + 1):
            exists = (qa + grp <= nq) if nb == grp else (qa + nb == nq)
            for diagonal in (False, True):
                @pl.when(exists & ((qa == ki) == diagonal))
                def _(nb=nb, diagonal=diagonal):
                    sweep(nb, diagonal)

    def qblock(j):
        return lambda s, qt: jnp.minimum(qt[s] + j, nq - 1)

    qbs = [qblock(j) for j in range(grp)]
    qcol = [functools.partial(lambda p, s, qt, kt, qb: (qb(s, qt), p), qb=qb) for qb in qbs]
    krow = lambda p, s, qt, kt: (kt[s], p)
    return pl.pallas_call(
        body, name="attention_bwd",
        out_shape=[jax.ShapeDtypeStruct((t, ATTN_W), F32)] * 5,
        grid_spec=pltpu.PrefetchScalarGridSpec(
            num_scalar_prefetch=2, grid=(4, len(steps)),
            in_specs=[pl.BlockSpec((tq, 128), m) for m in qcol]
            + [pl.BlockSpec((tk, 128), lambda p, s, qt, kt: (kt[s], 4 + p)),
               pl.BlockSpec((tk, 128), lambda p, s, qt, kt: (kt[s], 8 + p))]
            + [pl.BlockSpec((None, tq, 128), functools.partial(lambda p, s, qt, kt, qb: (p, qb(s, qt), 0), qb=qb))
               for qb in qbs]
            + [pl.BlockSpec((None, tk, 128), lambda p, s, qt, kt: (p, kt[s], 0))]
            + [pl.BlockSpec((tq, 128), m) for m in qcol] + [pl.BlockSpec((tq, 128), m) for m in qcol],
            out_specs=[pl.BlockSpec((t, 128), lambda p, s, qt, kt: (0, p)),
                       pl.BlockSpec((tk, 128), krow), pl.BlockSpec((tk, 128), krow), pl.BlockSpec((tk, 128), krow),
                       pl.BlockSpec((t, 128), lambda p, s, qt, kt: (0, p))]),
        compiler_params=_params(("parallel", "arbitrary"), VMEM_BIG),
    )(q_tab, k_tab, *([z] * grp), z, z, *([aug_q] * grp), aug_k, *([o] * grp), *([do] * grp))


def _shifted(prev_rows, x, shift):
    tm = x.shape[0]
    return pltpu.roll(jnp.concatenate([prev_rows, x], axis=0), shift, 0)[8:8 + tm]


def _ahead(x, next_rows, shift):
    tm = x.shape[0]
    return pltpu.roll(jnp.concatenate([x, next_rows], axis=0), tm + 8 - shift, 0)[0:tm]


def _conv_col0(z):
    return (z.shape[1] - F_PAD - 3 * CONV_W) // CONV_W


def _conv_specs(tm, c0):
    cols = (c0, c0 + 1, c0 + 2)
    tiles = [pl.BlockSpec((tm, CONV_W), functools.partial(lambda i, c: (i, c), c=c)) for c in cols]
    halos = [pl.BlockSpec((8, CONV_W), functools.partial(lambda i, c: (jnp.maximum(i * (tm // 8) - 1, 0), c), c=c))
             for c in cols]
    return tiles, halos


def _conv_gate(z, conv_w):
    t = z.shape[0]
    tm = ROW_TILE
    nt = t // tm

    def body(cb_ref, cc_ref, ci_ref, hc_ref, hi_ref, w_ref, g_ref, gt_ref):
        i = pl.program_id(0)
        cc = cc_ref[...] * ci_ref[...]
        prev = jnp.where(i > 0, hc_ref[...] * hi_ref[...], 0.0)
        conv = w_ref[0:1, :] * _shifted(prev, cc, 2) + w_ref[1:2, :] * _shifted(prev, cc, 1) + w_ref[2:3, :] * cc
        g = cb_ref[...] * conv
        g_ref[...] = g.astype(BF16)
        gt_ref[...] = g.T.astype(BF16)

    (cb, cc, ci), (_, hc, hi) = _conv_specs(tm, _conv_col0(z))
    return pl.pallas_call(
        body, name="conv_gate_fwd",
        out_shape=[jax.ShapeDtypeStruct((t, CONV_W), BF16), jax.ShapeDtypeStruct((CONV_W, t), BF16)],
        grid=(nt,),
        in_specs=[cb, cc, ci, hc, hi, pl.BlockSpec((8, CONV_W), lambda i: (0, 0))],
        out_specs=[pl.BlockSpec((tm, CONV_W), lambda i: (i, 0)), pl.BlockSpec((CONV_W, tm), lambda i: (0, i))],
        compiler_params=_params(("parallel",)),
    )(z, z, z, z, z, conv_w)


def _conv_bwd(z, dg, conv_w):
    t = z.shape[0]
    tm = ROW_TILE
    nt = t // tm

    def body(cb_ref, cc_ref, ci_ref, hc_ref, hi_ref, dg_ref, ncb_ref, ndg_ref, w_ref, dz_ref, dw_ref):
        i = pl.program_id(0)

        @pl.when(i == 0)
        def _():
            dw_ref[...] = jnp.zeros_like(dw_ref)

        cb, c_c, c_in = cb_ref[...], cc_ref[...], ci_ref[...]
        cc = c_c * c_in
        prev = jnp.where(i > 0, hc_ref[...] * hi_ref[...], 0.0)
        cc1, cc2 = _shifted(prev, cc, 1), _shifted(prev, cc, 2)
        w0, w1, w2 = w_ref[0:1, :], w_ref[1:2, :], w_ref[2:3, :]
        conv = w0 * cc2 + w1 * cc1 + w2 * cc
        dgv = dg_ref[...]
        dconv = dgv * cb
        nxt = jnp.where(i < nt - 1, ndg_ref[...] * ncb_ref[...], 0.0)
        dcc = w2 * dconv + w1 * _ahead(dconv, nxt, 1) + w0 * _ahead(dconv, nxt, 2)
        dz_ref[:, 0:CONV_W] = (dgv * conv).astype(BF16)
        dz_ref[:, CONV_W:2 * CONV_W] = (dcc * c_in).astype(BF16)
        dz_ref[:, 2 * CONV_W:] = (dcc * c_c).astype(BF16)
        dw_ref[0:1, :] += jnp.sum(dconv * cc2, axis=0, keepdims=True)
        dw_ref[1:2, :] += jnp.sum(dconv * cc1, axis=0, keepdims=True)
        dw_ref[2:3, :] += jnp.sum(dconv * cc, axis=0, keepdims=True)

    c0 = _conv_col0(z)
    (cb, cc, ci), (_, hc, hi) = _conv_specs(tm, c0)
    nxt = lambda i, c: (jnp.minimum((i + 1) * (tm // 8), t // 8 - 1), c)
    return pl.pallas_call(
        body, name="conv_gate_bwd",
        out_shape=[jax.ShapeDtypeStruct((t, 3 * CONV_W), BF16), jax.ShapeDtypeStruct((8, CONV_W), F32)],
        grid=(nt,),
        in_specs=[cb, cc, ci, hc, hi, pl.BlockSpec((tm, CONV_W), lambda i: (i, 0)),
                  pl.BlockSpec((8, CONV_W), lambda i: nxt(i, c0)), pl.BlockSpec((8, CONV_W), lambda i: nxt(i, 0)),
                  pl.BlockSpec((8, CONV_W), lambda i: (0, 0))],
        out_specs=[pl.BlockSpec((tm, 3 * CONV_W), lambda i: (i, 0)), pl.BlockSpec((8, CONV_W), lambda i: (0, 0))],
        compiler_params=_params(("arbitrary",)),
    )(z, z, z, z, z, dg, z, dg, conv_w)


def _branch_mix(z, o, g, w_ab, w_cb, d):
    t = z.shape[0]
    tm = ROW_TILE
    ga_col = 0

    def body(o_ref, g_ref, ga_ref, gc_ref, wa_ref, wc_ref, mp_ref, mpt_ref, ot_ref):
        o_ = o_ref[...]
        ya = _dot(o_.astype(BF16), wa_ref[...])
        yc = _dot(g_ref[...], wc_ref[...])
        mp = _sigmoid(ga_ref[...]) * ya + _sigmoid(gc_ref[...]) * yc
        mp_ref[...] = mp.astype(BF16)
        mpt_ref[...] = mp.T.astype(BF16)
        ot_ref[...] = o_.T.astype(BF16)

    return pl.pallas_call(
        body, name="branch_mix_fwd",
        out_shape=[jax.ShapeDtypeStruct((t, d), BF16), jax.ShapeDtypeStruct((d, t), BF16),
                   jax.ShapeDtypeStruct((ATTN_W, t), BF16)],
        grid=(t // tm,),
        in_specs=[pl.BlockSpec((tm, ATTN_W), lambda i: (i, 0)), pl.BlockSpec((tm, CONV_W), lambda i: (i, 0)),
                  pl.BlockSpec((tm, d), lambda i: (i, ga_col)), pl.BlockSpec((tm, d), lambda i: (i, ga_col + 1)),
                  pl.BlockSpec((ATTN_W, d), lambda i: (0, 0)), pl.BlockSpec((CONV_W, d), lambda i: (0, 0))],
        out_specs=[pl.BlockSpec((tm, d), lambda i: (i, 0)), pl.BlockSpec((d, tm), lambda i: (0, i)),
                   pl.BlockSpec((ATTN_W, tm), lambda i: (0, i))],
        compiler_params=_params(("parallel",), VMEM_BIG),
    )(o, g, z, z, w_ab, w_cb)


def _branch_bwd(z, o, g, dmixed, w_out, w_ab, w_cb, d):
    t = z.shape[0]
    tm = ROW_TILE // 2
    ga_col = 0

    def body(dm_ref, o_ref, g_ref, ga_ref, gc_ref, wo_ref, wa_ref, wc_ref, dya_ref, dyc_ref, dgt_ref, do_ref, dg_ref):
        dmp = _dot_nt(dm_ref[...], wo_ref[...])
        ya = _dot(o_ref[...].astype(BF16), wa_ref[...])
        yc = _dot(g_ref[...], wc_ref[...])
        sa, sc = _sigmoid(ga_ref[...]), _sigmoid(gc_ref[...])
        dya = (dmp * sa).astype(BF16)
        dyc = (dmp * sc).astype(BF16)
        dya_ref[...] = dya
        dyc_ref[...] = dyc
        dgt_ref[:, :d] = (dmp * ya * sa * (1.0 - sa)).astype(BF16)
        dgt_ref[:, d:] = (dmp * yc * sc * (1.0 - sc)).astype(BF16)
        do_ref[...] = _dot_nt(dya, wa_ref[...])
        dg_ref[...] = _dot_nt(dyc, wc_ref[...])

    row = lambda i: (i, 0)
    fixed = lambda i: (0, 0)
    return pl.pallas_call(
        body, name="branch_mix_bwd",
        out_shape=[jax.ShapeDtypeStruct((t, d), BF16), jax.ShapeDtypeStruct((t, d), BF16),
                   jax.ShapeDtypeStruct((t, 2 * d), BF16), jax.ShapeDtypeStruct((t, ATTN_W), F32),
                   jax.ShapeDtypeStruct((t, CONV_W), F32)],
        grid=(t // tm,),
        in_specs=[pl.BlockSpec((tm, d), row), pl.BlockSpec((tm, ATTN_W), row), pl.BlockSpec((tm, CONV_W), row),
                  pl.BlockSpec((tm, d), lambda i: (i, ga_col)), pl.BlockSpec((tm, d), lambda i: (i, ga_col + 1)),
                  pl.BlockSpec((d, d), fixed), pl.BlockSpec((ATTN_W, d), fixed), pl.BlockSpec((CONV_W, d), fixed)],
        out_specs=[pl.BlockSpec((tm, d), row), pl.BlockSpec((tm, d), row), pl.BlockSpec((tm, 2 * d), row),
                   pl.BlockSpec((tm, ATTN_W), row), pl.BlockSpec((tm, CONV_W), row)],
        compiler_params=_params(("parallel",), VMEM_BIG),
    )(dmixed, o, g, z, z, w_out, w_ab, w_cb)


def _loss_norm_bwd(h, target, f, g_post, alpha):
    t, d = h.shape
    tm = ROW_TILE

    def body(h_ref, t_hbm, f_ref, g_ref, dh_ref, df_ref, dg_ref, loss_ref, t_buf, sems):
        i = pl.program_id(0)

        @pl.when(i == 0)
        def _():
            loss_ref[...] = jnp.zeros_like(loss_ref)
            dg_ref[...] = jnp.zeros_like(dg_ref)

        target = _read_token_rows(t_hbm, t_buf, sems, i, t // tm)
        row = i * tm + lax.broadcasted_iota(jnp.int32, (tm, 1), 0)
        err = jnp.where(row >= N_FRONT, h_ref[...] - target[...], 0.0)
        dy = err * (1.0 / d)
        dh_ref[...] = dy
        per_row = jnp.sum(err * err, axis=1, keepdims=True) * (1.0 / d)
        loss_ref[...] += 0.5 * jnp.sum(per_row, axis=0, keepdims=True)
        dx, dg = _rms_bwd(f_ref[...], g_ref[...], dy)
        df_ref[...] = (alpha * dx).astype(BF16)
        dg_ref[...] += alpha * dg

    row = pl.BlockSpec((tm, d), lambda i: (i, 0))
    vec = pl.BlockSpec((1, d), lambda i: (0, 0))
    return pl.pallas_call(
        body, name="loss_and_post_norm_bwd",
        out_shape=[jax.ShapeDtypeStruct((t, d), F32), jax.ShapeDtypeStruct((t, d), BF16),
                   jax.ShapeDtypeStruct((1, d), F32), jax.ShapeDtypeStruct((1, 128), F32)],
        grid=(t // tm,),
        in_specs=[row, ANY, row, vec],
        out_specs=[row, row, vec, pl.BlockSpec((1, 128), lambda i: (0, 0))],
        scratch_shapes=[pltpu.VMEM((2, tm, d), F32), pltpu.SemaphoreType.DMA((2,))],
        compiler_params=_params(("arbitrary",)),
    )(h, target, f, g_post)


def _ffn_bwd_mid(name, df, w_out, ab):
    t, d = df.shape
    cw = ab.shape[1] // 4
    tm = ROW_TILE

    def body(df_ref, w_ref, ab_ref, o_ref):
        ds = _dot_nt(df_ref[...], w_ref[...])
        a = ab_ref[:, :cw].astype(F32)
        b = ab_ref[:, cw:].astype(F32)
        sg = _sigmoid(a)
        o_ref[:, :cw] = (ds * b * (sg * (1.0 + a * (1.0 - sg)))).astype(BF16)
        o_ref[:, cw:] = (ds * (a * sg)).astype(BF16)

    return pl.pallas_call(
        body, name=name, out_shape=jax.ShapeDtypeStruct((t, 4 * cw), BF16),
        grid=(2, t // tm),
        in_specs=[pl.BlockSpec((tm, d), lambda j, i: (i, 0)), pl.BlockSpec((cw, d), lambda j, i: (j, 0)),
                  pl.BlockSpec((tm, 2 * cw), lambda j, i: (i, j))],
        out_specs=pl.BlockSpec((tm, 2 * cw), lambda j, i: (i, j)),
        compiler_params=_params(("parallel", "parallel"), VMEM_BIG),
    )(df, w_out, ab)


def _next_post_norm_bwd(dh, post_refs, alpha, first):
    x_ref, g_ref, dx_ref, dg_ref = post_refs

    @pl.when(first)
    def _():
        dg_ref[...] = jnp.zeros_like(dg_ref)

    dx, dg = _rms_bwd(x_ref[...], g_ref[...], dh)
    dx_ref[...] = (alpha * dx).astype(BF16)
    dg_ref[...] += alpha * dg


def _mm_nt_norm_bwd(name, dy, w, h, g, dh_in, post=None):
    t, kdim = dy.shape
    d = h.shape[1]
    tm = ROW_TILE // 2

    def body(dy_ref, w_ref, h_ref, g_ref, dhi_ref, *rest):
        dh_ref, dg_ref = rest[-4:-2] if post else rest
        first = pl.program_id(0) == 0

        @pl.when(first)
        def _():
            dg_ref[...] = jnp.zeros_like(dg_ref)

        cw = w_ref.shape[2]
        dn = _dot_nt(dy_ref[:, 0:cw], w_ref[_slot_of(0)])
        for k in range(1, 4):
            dn += _dot_nt(dy_ref[:, k * cw:(k + 1) * cw], w_ref[_slot_of(k)])
        dx, dg = _rms_bwd(h_ref[...], g_ref[...], dn)
        dh = dhi_ref[...] + dx
        dh_ref[...] = dh
        dg_ref[...] += dg
        if post:
            _next_post_norm_bwd(dh, rest[0:2] + rest[-2:], post[2], first)

    row = pl.BlockSpec((tm, d), lambda i: (i, 0))
    vec = pl.BlockSpec((1, d), lambda i: (0, 0))
    out_shape = [jax.ShapeDtypeStruct((t, d), F32), jax.ShapeDtypeStruct((1, d), F32)]
    if post:
        out_shape += [jax.ShapeDtypeStruct((t, d), BF16), jax.ShapeDtypeStruct((1, d), F32)]
    return pl.pallas_call(
        body, name=name, out_shape=out_shape, grid=(t // tm,),
        in_specs=[pl.BlockSpec((tm, kdim), lambda i: (i, 0)), pl.BlockSpec(w.shape, lambda i: (0,) * w.ndim),
                  row, vec, row] + ([row, vec] if post else []),
        out_specs=[row, vec] + ([row, vec] if post else []),
        compiler_params=_params(("arbitrary",), VMEM_BIG),
    )(dy, w, h, g, dh_in, *(post[:2] if post else ()))


def _gate_bwd(dfq, dfk, z, b_pad, f_col):
    t = z.shape[0]
    tm = ROW_TILE
    nt = t // tm

    def body(dq_ref, dk_ref, z_ref, b_ref, dz_ref, db_ref, carry_ref):
        i = pl.program_id(0)

        @pl.when(i == 0)
        def _():
            carry_ref[...] = jnp.zeros_like(carry_ref)
            db_ref[...] = jnp.zeros_like(db_ref)

        pick = (lax.broadcasted_iota(jnp.int32, (ATTN_W, 128), 0)
                == HEAD_DIM * lax.broadcasted_iota(jnp.int32, (ATTN_W, 128), 1)).astype(F32)
        d_heads = jnp.dot(dq_ref[...] - dk_ref[...], pick, preferred_element_type=F32,
                          precision=lax.Precision.HIGHEST)
        tri = (lax.broadcasted_iota(jnp.int32, (tm, tm), 0) <= lax.broadcasted_iota(jnp.int32, (tm, tm), 1))
        tail = jnp.dot(tri.astype(F32), d_heads, preferred_element_type=F32, precision=lax.Precision.HIGHEST)
        tail = tail + carry_ref[0:1, :]
        carry_ref[...] = jnp.broadcast_to(tail[0:1, :], carry_ref.shape)
        row = (nt - 1 - i) * tm + lax.broadcasted_iota(jnp.int32, (tm, 1), 0)
        dlogit = jnp.where(row >= ROW_PAD, tail * _sigmoid(-(z_ref[...] + b_ref[...])), 0.0)
        dz_ref[...] = jnp.zeros_like(dz_ref)
        dz_ref[:, 0:128] = dlogit.astype(BF16)
        db_ref[...] += jnp.sum(dlogit, axis=0, keepdims=True)

    rev = lambda i: (nt - 1 - i, 0)
    return pl.pallas_call(
        body, name="forget_gate_bwd",
        out_shape=[jax.ShapeDtypeStruct((t, F_PAD), BF16), jax.ShapeDtypeStruct((1, 128), F32)],
        grid=(nt,),
        in_specs=[pl.BlockSpec((tm, ATTN_W), rev), pl.BlockSpec((tm, ATTN_W), rev),
                  pl.BlockSpec((tm, 128), lambda i: (nt - 1 - i, f_col // 128)),
                  pl.BlockSpec((1, 128), lambda i: (0, 0))],
        out_specs=[pl.BlockSpec((tm, F_PAD), rev), pl.BlockSpec((1, 128), lambda i: (0, 0))],
        scratch_shapes=[pltpu.VMEM((8, 128), F32)],
        compiler_params=_params(("arbitrary",)),
    )(dfq, dfk, z, b_pad)


def _ffn_fwd(tag, n, w_in4, w_out, h, g_post, g_next):
    ab, s, s_t = _ffn_in(f"{tag}_in_fwd", n, w_in4)
    outs = _mm_resid_norm(f"{tag}_out_fwd", s, w_out, h, g_post, 0.5, g_next)
    return ab, s_t, outs


def _ffn_bwd_weights(tag, df, ab, s_t, n_t, w_in4, w_out):
    d, cw = w_in4.shape[1], w_in4.shape[2]
    t = df.shape[0]
    dw_out = _weight_grad(f"{tag}_dw_out", s_t, df, d, out_rows=cw // 2)
    dab = _ffn_bwd_mid(f"{tag}_mid_bwd", df, w_out, ab)
    bk = _k_tile(t)
    dw_in = _matmul(
        f"{tag}_dw_in", n_t, dab, jax.ShapeDtypeStruct((4, d, cw), F32), (1, 4, t // bk),
        pl.BlockSpec((d, bk), lambda a, b, k: (0, k)), pl.BlockSpec((bk, cw), lambda a, b, k: (k, b)),
        pl.BlockSpec((None, d, cw), lambda a, b, k: (_slot_of(b), 0, 0)), vmem=VMEM_BIG)
    return dab, dw_in, dw_out


LOSS_ROW = 12


def _pack_small(meta, conv, gains, b_forget, loss=None):
    d = gains[0].shape[1]
    rows = [meta.reshape(4, d), jnp.pad(conv.reshape(1, 3 * 128), ((0, 0), (0, d - 3 * 128)))]
    rows += list(gains) + [jnp.pad(b_forget, ((0, 0), (0, d - HEADS)))]
    last = jnp.zeros((4, d), F32)
    if loss is not None:
        last = jnp.pad(loss.reshape(1, 1), ((0, 3), (0, d - 1)))
    return jnp.concatenate(rows + [last], axis=0)


def _unpack_small(block):
    d = block.shape[1]
    meta = block[0:4].reshape(N_META, d // 4)
    conv = block[4, :3 * 128].reshape(1, 3, 128)
    gains = [block[5 + i:6 + i] for i in range(6)]
    return meta, conv, gains, block[11:12, :HEADS]


def kernel(x, meta_tokens, w_in, b_forget, conv_w, w_attn_branch, w_conv_branch, w_out, g_ffn1_pre, g_ffn1_post, w_ffn1_in, w_ffn1_out, g_mix_pre, g_mix_post, g_ffn2_pre, g_ffn2_post, w_ffn2_in, w_ffn2_out, loss_target, m_meta_tokens, m_w_in, m_b_forget, m_conv_w, m_w_attn_branch, m_w_conv_branch, m_w_out, m_g_ffn1_pre, m_g_ffn1_post, m_w_ffn1_in, m_w_ffn1_out, m_g_mix_pre, m_g_mix_post, m_g_ffn2_pre, m_g_ffn2_post, m_w_ffn2_in, m_w_ffn2_out, v_meta_tokens, v_w_in, v_b_forget, v_conv_w, v_w_attn_branch, v_w_conv_branch, v_w_out, v_g_ffn1_pre, v_g_ffn1_post, v_w_ffn1_in, v_w_ffn1_out, v_g_mix_pre, v_g_mix_post, v_g_ffn2_pre, v_g_ffn2_post, v_w_ffn2_in, v_w_ffn2_out):
    seq, d = x.shape[1], x.shape[2]
    t = seq + N_FRONT
    f_lo = 3 * ATTN_W
    c_arr = lax.axis_index("c").astype(jnp.int32).reshape(1)

    cs = w_in.shape[2]
    cs_pad = -(-cs // 64) * 64

    def w_in_rows(a):
        return jnp.pad(jnp.transpose(a[0]), ((0, cs_pad - cs), (0, 0)))

    big = [w_in_rows(w_in), w_attn_branch[0], w_conv_branch[0], w_out[0], w_ffn1_in[0], w_ffn1_out[0], w_ffn2_in[0],
           w_ffn2_out[0]]
    small_gather = jnp.concatenate(
        [meta_tokens.reshape(4, d), jnp.pad(conv_w.reshape(1, 3 * 128), ((0, 0), (0, d - 3 * 128))),
         jnp.zeros((11, d), F32)], axis=0)
    w_f1_in4, small4 = _all_gather([big[4].astype(BF16), small_gather])
    (second, rest), small4 = lax.optimization_barrier(
        (([big[5].astype(BF16)], [big[i].astype(BF16) for i in (0, 1, 2, 3, 6, 7)]), small4))
    second_gathered = _all_gather_async("all_gather_ffn1_out", second, 5)
    rest_gathered = _all_gather_async("all_gather_rest", rest, 1)
    meta_full = jnp.transpose(small4[:, 0:4].reshape(4, N_META, d // 4), (1, 0, 2)).reshape(N_META, d)
    conv_full = jnp.transpose(small4[:, 4, :3 * 128].reshape(4, 3, 128), (1, 0, 2)).reshape(3, CONV_W)
    conv_pad = jnp.pad(conv_full, ((0, 5), (0, 0)))
    b_pad = jnp.pad(b_forget, ((0, 0), (0, 128 - HEADS)))

    h0, n1, n1_t = _embed_norm(x[0], meta_full, g_ffn1_pre)
    ab1, s1, s1_t = _ffn_in("ffn1_in_fwd", n1, w_f1_in4)
    w_f1_out = second_gathered(s1, [0])[0].reshape(-1, d)
    f1, h1, u, u_t = _mm_resid_norm("ffn1_out_fwd", s1, w_f1_out, h0, g_ffn1_post, 0.5, g_mix_pre)

    w_in4, w_ab4, w_cb4, w_out4, w_f2_in4, w_f2_out4 = rest_gathered(u, range(6))
    w_in_t = w_in4[:, :cs].reshape(4 * cs, d)
    g_lo = f_lo + HEADS + 3 * CONV_W
    w_in_pad = jnp.concatenate(
        [w_in_t[:f_lo], w_in_t[g_lo:], w_in_t[f_lo + HEADS:g_lo], w_in_t[f_lo:f_lo + HEADS],
         jnp.zeros((F_PAD - HEADS, d), BF16)], axis=0)
    w_ab = jnp.transpose(w_ab4, (1, 0, 2)).reshape(ATTN_W, d)
    w_cb = jnp.transpose(w_cb4, (1, 0, 2)).reshape(CONV_W, d)
    w_out_full = w_out4.reshape(d, d)
    w_f2_out = w_f2_out4.reshape(-1, d)
    qkv, z = _in_proj(u, w_in_pad)
    f_col = z.shape[1] - F_PAD
    f_cum = _gate_prep(z, b_pad, f_col)
    f_heads = f_cum[:, :HEADS]
    o, lse = _attn_fwd(qkv, *_attn_bias_operands(f_heads))
    g, g_t = _conv_gate(z, conv_pad)
    mp, mp_t, o_t = _branch_mix(z, o, g, w_ab, w_cb, d)
    mixed, h2, n2, n2_t = _mm_resid_norm("mix_out_fwd", mp, w_out_full, h1, g_mix_post, 1.0, g_ffn2_pre)
    ab2, s2_t, (f2, h3) = _ffn_fwd("ffn2", n2, w_f2_in4, w_f2_out, h2, g_ffn2_post, None)
    dh3, df2, dg_f2_post, loss_part = _loss_norm_bwd(h3, loss_target[0], f2, g_ffn2_post, 0.5)

    reduced = {}

    def reduce_scatter(label, tags, slots, sequencer_id, hold=None, got=None, after=None):
        if got is None:
            got = _pair_send_halves(f"grad_pair_exchange_{label}", slots)
        else:
            got, _ = lax.optimization_barrier((got, after))
        sums = [_pair_add(tag, s, a, c_arr, F32 if tag == "small" else BF16) for tag, s, a in zip(tags, slots, got)]
        sums, hold = lax.optimization_barrier((sums, hold))
        if sequencer_id is None:
            arrived = _chip_scatter(f"grad_chip_scatter_{label}", sums)
        else:
            arrived = _chip_scatter_async(f"grad_chip_scatter_{label}", sums, sequencer_id)
        mine = [_chip_add(tag, a) for tag, a in zip(tags, arrived)]
        reduced.update(zip(tags, zip(mine, _pair_swap(f"grad_pair_swap_{label}", mine))))
        return hold

    dab2, dw_f2_in, dw_f2_out = _ffn_bwd_weights("ffn2", df2, ab2, s2_t, n2_t, w_f2_in4, w_f2_out)
    ffn2_slots = [dw_f2_in, dw_f2_out.reshape(4, -1, d)]
    ffn2_got = _pair_send_halves_async("grad_pair_exchange_ffn2", ffn2_slots, 6)
    dh2, dg_f2_pre, dmixed, dg_mix_post = _mm_nt_norm_bwd(
        "ffn2_in_bwd", dab2, w_f2_in4, h2, g_ffn2_pre, dh3, post=(mixed, g_mix_post, 1.0))
    reduce_scatter("ffn2", ["w_ffn2_in", "w_ffn2_out"], ffn2_slots, 2, got=ffn2_got, after=dh2)
    dw_out = _weight_grad("mix_dw_out", mp_t, dmixed, d)
    dya, dyc, dgates, do, dgconv = _branch_bwd(z, o, g, dmixed, w_out_full, w_ab, w_cb, d)
    dw_ab = _weight_grad("mix_dw_attn_branch", o_t, dya, d)
    dw_cb = _weight_grad("mix_dw_conv_branch", g_t, dyc, d)
    dz_conv, dconv_w = _conv_bwd(z, dgconv, conv_pad)
    front = lax.broadcasted_iota(jnp.int32, (t, 1), 0) < ROW_PAD
    lse_heads = jnp.where(front, 1e9, lse[:, ::HEAD_DIM])
    dq, dk, dv, dfk, dfq = _attn_bwd(qkv, *_attn_bias_operands(f_heads, lse_heads), o, do)
    dz_f, db_forget = _gate_bwd(dfq, dfk, z, b_pad, f_col)
    dz_pieces = {"q": dq, "k": dk, "v": dv, "gates": dgates, "conv": dz_conv, "f": dz_f}
    dh1, dg_mix_pre, df1, dg_f1_post = _mix_in_bwd(
        list(dz_pieces.values()), w_in_pad, h1, g_mix_pre, dh2, (f1, g_ffn1_post, 0.5))
    dw_t = {name: _weight_grad_t(f"mix_dw_in_{name}", u_t, piece) for name, piece in dz_pieces.items()}
    dw_in_t = jnp.concatenate(
        [dw_t["q"], dw_t["k"], dw_t["v"], dw_t["f"][:HEADS], dw_t["conv"], dw_t["gates"]], axis=0)
    mix_slots = [jnp.pad(dw_in_t.reshape(4, cs, d), ((0, 0), (0, cs_pad - cs), (0, 0))),
                 jnp.transpose(dw_ab.reshape(ATTN_W, 4, d // 4), (1, 0, 2)),
                 jnp.transpose(dw_cb.reshape(CONV_W, 4, d // 4), (1, 0, 2)),
                 dw_out.reshape(4, d // 4, d)]
    mix_got = _pair_send_halves_async("grad_pair_exchange_mix", mix_slots, 7)
    dab1, dw_f1_in, dw_f1_out = _ffn_bwd_weights("ffn1", df1, ab1, s1_t, n1_t, w_f1_in4, w_f1_out)
    reduce_scatter("mix", ["w_in", "w_attn_branch", "w_conv_branch", "w_out"], mix_slots, 3, got=mix_got,
                   after=dw_f1_out)
    dab1 = reduce_scatter("ffn1", ["w_ffn1_in", "w_ffn1_out"], [dw_f1_in, dw_f1_out.reshape(4, -1, d)], 4, dab1)
    dh0, dg_f1_pre = _mm_nt_norm_bwd("ffn1_in_bwd", dab1, w_f1_in4, h0, g_ffn1_pre, dh1)
    grad_x = dh0[N_FRONT:][None]
    dmeta = dh0[ROW_PAD:N_FRONT]
    small_grad = jnp.stack([
        _pack_small(dmeta[:, j * (d // 4):(j + 1) * (d // 4)], dconv_w[:3, j * 128:(j + 1) * 128],
                    [dg_f1_pre, dg_f1_post, dg_mix_pre, dg_mix_post, dg_f2_pre, dg_f2_post], db_forget[:, :HEADS],
                    loss_part[0, 0])
        for j in range(4)])
    reduce_scatter("small", ["small"], [small_grad], None)
    tags =["w_in", "w_attn_branch", "w_conv_branch", "w_out", "w_ffn1_in", "w_ffn1_out", "w_ffn2_in", "w_ffn2_out", "small"]
    halves = [reduced[tag][0] for tag in tags]
    others = [reduced[tag][1] for tag in tags]

    small = [g_ffn1_pre, g_ffn1_post, g_mix_pre, g_mix_post, g_ffn2_pre, g_ffn2_post]
    small_m = [m_g_ffn1_pre, m_g_ffn1_post, m_g_mix_pre, m_g_mix_post, m_g_ffn2_pre, m_g_ffn2_post]
    small_v = [v_g_ffn1_pre, v_g_ffn1_post, v_g_mix_pre, v_g_mix_post, v_g_ffn2_pre, v_g_ffn2_post]
    ws = big + [_pack_small(meta_tokens, conv_w[0], small, b_forget)]
    ms = [w_in_rows(m_w_in), m_w_attn_branch[0], m_w_conv_branch[0], m_w_out[0], m_w_ffn1_in[0], m_w_ffn1_out[0],
          m_w_ffn2_in[0], m_w_ffn2_out[0], _pack_small(m_meta_tokens, m_conv_w[0], small_m, m_b_forget)]
    vs = [w_in_rows(v_w_in), v_w_attn_branch[0], v_w_conv_branch[0], v_w_out[0], v_w_ffn1_in[0], v_w_ffn1_out[0],
          v_w_ffn2_in[0], v_w_ffn2_out[0], _pack_small(v_meta_tokens, v_conv_w[0], small_v, v_b_forget)]
    updates = [_adamw(tag, w, a, b, m, v, c_arr) for tag, w, a, b, m, v in zip(tags, ws, halves, others, ms, vs)]

    def leaves(big_vals, small_block):
        meta, conv, gains, bf = _unpack_small(small_block)
        w_in_t_, w_ab_, w_cb_, w_out_, f1_in, f1_out, f2_in, f2_out = [b[None] for b in big_vals]
        w_in_ = jnp.transpose(w_in_t_[:, :cs], (0, 2, 1))
        return [meta, w_in_, bf, conv, w_ab_, w_cb_, w_out_, gains[0], gains[1], f1_in, f1_out,
                gains[2], gains[3], gains[4], gains[5], f2_in, f2_out]

    out_g, out_d, out_m, out_v = [leaves([u_[k] for u_ in updates[:8]], updates[8][k]) for k in range(4)]
    loss = updates[8][0][LOSS_ROW, 0]
    return (loss, grad_x, *out_g, *out_d, *out_m, *out_v)
```

```python
import functools

import jax
import jax.numpy as jnp
from jax import lax
from jax.experimental import pallas as pl
from jax.experimental.pallas import tpu as pltpu
from jax.experimental.pallas import tpu_sc as plsc

N_META = 16
ROW_PAD = 112
N_FRONT = ROW_PAD + N_META
HEADS = 8
HEAD_DIM = 64
ATTN_W = HEADS * HEAD_DIM
CONV_W = 512
NORM_EPS = 1e-6
ROW_TILE = 640
F_PAD = 128
ATTN_Q_GROUP = 2
ATTN_KV_GROUP = 4
NEG = -1e30
ADAM_LR = 0.001
ADAM_B1 = 0.9
ADAM_B2 = 0.999
ADAM_EPS = 1e-08
ADAM_WD = 0.01
ADAM_STEP = 10
VMEM_BIG = 56 * 1024 * 1024
MESH = pl.DeviceIdType.MESH
ANY = pl.BlockSpec(memory_space=pl.ANY)
F32 = jnp.float32
BF16 = jnp.bfloat16


def _params(sem, vmem=None):
    return pltpu.CompilerParams(dimension_semantics=sem, vmem_limit_bytes=vmem)


def _sigmoid(x):
    return 1.0 / (1.0 + jnp.exp(-x))


def _rstd(x):
    return lax.rsqrt(jnp.mean(x * x, axis=-1, keepdims=True) + NORM_EPS)


def _rms_bwd(x, g, dy):
    r = _rstd(x)
    xr = x * r
    gdy = g * dy
    dx = r * (gdy - xr * jnp.mean(xr * gdy, axis=-1, keepdims=True))
    return dx, jnp.sum(dy * xr, axis=0, keepdims=True)


def _dot(a, b):
    return jnp.dot(a, b, preferred_element_type=F32)


def _dot_nt(a, b):
    return lax.dot_general(a, b, (((1,), (1,)), ((), ())), preferred_element_type=F32)


def _k_tile(t):
    return 1664 if t % 1664 == 0 else ROW_TILE


def _place():
    x, y, c = lax.axis_index("x"), lax.axis_index("y"), lax.axis_index("c")
    chips = [(1 - x, y), (x, 1 - y), (1 - x, 1 - y)]
    return x, y, c, chips


def _all_gather(shards):
    n = len(shards)
    split = [s.reshape(2, s.shape[0] // 2, s.shape[1]) for s in shards]

    def body(*refs):
        ins, outs = refs[:n], refs[n:2 * n]
        send_sems, recv_sems = refs[2 * n:]
        x, y, c, chips = _place()
        me = 2 * x + y
        sibling = (x, y, 1 - c)

        def remote(i, k, slot, part, to, src=None):
            dst = outs[i].at[slot, part]
            return pltpu.make_async_remote_copy(
                src_ref=dst if src is None else src, dst_ref=dst,
                send_sem=send_sems.at[i, k], recv_sem=recv_sems.at[i, k],
                device_id=to, device_id_type=MESH)

        started = []
        for i in range(n):
            for k, (cx, cy) in enumerate(chips):
                cp = remote(i, k, me, c, (cx, cy, c), src=ins[i].at[c])
                cp.start()
                started.append(cp)
        for i in range(n):
            for k, (cx, cy) in enumerate(chips):
                remote(i, k, 2 * cx + cy, c, (x, y, c)).wait_recv()
                cp = remote(i, 3 + k, 2 * cx + cy, c, sibling)
                cp.start()
                started.append(cp)
        for i in range(n):
            for k, (cx, cy) in enumerate(chips):
                remote(i, 3 + k, 2 * cx + cy, 1 - c, (x, y, c)).wait_recv()
        for cp in started:
            cp.wait_send()

    outs = pl.pallas_call(
        body, name="all_gather_weights",
        out_shape=[jax.ShapeDtypeStruct((4,) + s.shape, s.dtype) for s in split],
        in_specs=[ANY] * n, out_specs=[ANY] * n,
        scratch_shapes=[pltpu.SemaphoreType.DMA((n, 6)), pltpu.SemaphoreType.DMA((n, 6))],
    )(*split)
    me =2 * lax.axis_index("x") + lax.axis_index("y")
    outs = [lax.dynamic_update_slice(o, s[None], (me, 0, 0, 0)) for o, s in zip(outs, split)]
    return [o.reshape((4,) + s.shape) for o, s in zip(outs, shards)]


def _all_gather_async(name, shards, collective_id):
    n = len(shards)
    split = [s.reshape(2, s.shape[0] // 2, s.shape[1]) for s in shards]
    ins = [jax.new_ref(s, memory_space=pltpu.MemorySpace.HBM) for s in split]
    outs = [jax.empty_ref(jax.ShapeDtypeStruct((4,) + s.shape, s.dtype), memory_space=pltpu.MemorySpace.HBM)
            for s in split]

    @pl.kernel(mesh=plsc.ScalarSubcoreMesh(axis_name="sequencer", num_cores=1), name=name,
               scratch_types=(pltpu.SemaphoreType.DMA((n, 6)), pltpu.SemaphoreType.DMA((n, 6))),
               compiler_params=pltpu.CompilerParams(collective_id=collective_id))
    def launch(send_sems, recv_sems):
        x, y, c, chips = _place()
        me = 2 * x + y
        sibling = (x, y, 1 - c)
        barrier = pltpu.get_barrier_semaphore()
        for peer in [(cx, cy, c) for cx, cy in chips] + [sibling]:
            pl.semaphore_signal(barrier, inc=1, device_id=peer, device_id_type=MESH)
        pl.semaphore_wait(barrier, 4)

        def remote(i, k, slot, part, to, src=None):
            dst = outs[i].at[slot, part]
            return pltpu.make_async_remote_copy(
                src_ref=dst if src is None else src, dst_ref=dst,
                send_sem=send_sems.at[i, k], recv_sem=recv_sems.at[i, k],
                device_id=to, device_id_type=MESH)

        started = []
        for i in range(n):
            for k, (cx, cy) in enumerate(chips):
                cp = remote(i, k, me, c, (cx, cy, c), src=ins[i].at[c])
                cp.start()
                started.append(cp)
        for i in range(n):
            for k, (cx, cy) in enumerate(chips):
                remote(i, k, 2 * cx + cy, c, (x, y, c)).wait_recv()
                cp = remote(i, 3 + k, 2 * cx + cy, c, sibling)
                cp.start()
                started.append(cp)
        for i in range(n):
            for k, (cx, cy) in enumerate(chips):
                remote(i, 3 + k, 2 * cx + cy, 1 - c, (x, y, c)).wait_recv()
        for cp in started:
            cp.wait_send()

    launch()
    raw = [o[...] for o in outs]

    def finish(after, which):
        arrived, _ = lax.optimization_barrier(([raw[i] for i in which], after))
        me = 2 * lax.axis_index("x") + lax.axis_index("y")
        gathered = [lax.dynamic_update_slice(a, split[i][None], (me, 0, 0, 0)) for a, i in zip(arrived, which)]
        return [g.reshape((4,) + shards[i].shape) for g, i in zip(gathered, which)]

    return finish


def _pair_send_halves(name, grads):
    n = len(grads)

    def body(*refs):
        ins, outs = refs[:n], refs[n:2 * n]
        send_sems, recv_sems = refs[2 * n:]
        x, y, c, _ = _place()
        cps = []
        for i in range(n):
            half = ins[i].shape[1] // 2
            cp = pltpu.make_async_remote_copy(
                src_ref=ins[i].at[:, pl.ds((1 - c) * half, half)], dst_ref=outs[i],
                send_sem=send_sems.at[i], recv_sem=recv_sems.at[i],
                device_id=(x, y, 1 - c), device_id_type=MESH)
            cp.start()
            cps.append(cp)
        for cp in cps:
            cp.wait()

    return pl.pallas_call(
        body, name=name,
        out_shape=[jax.ShapeDtypeStruct((4, g.shape[1] // 2, g.shape[2]), g.dtype) for g in grads],
        in_specs=[ANY] * n, out_specs=[ANY] * n,
        scratch_shapes=[pltpu.SemaphoreType.DMA((n,)), pltpu.SemaphoreType.DMA((n,))],
    )(*grads)


def _pair_send_halves_async(name, grads, collective_id):
    n = len(grads)
    ins = [jax.new_ref(g, memory_space=pltpu.MemorySpace.HBM) for g in grads]
    outs = [jax.empty_ref(jax.ShapeDtypeStruct((4, g.shape[1] // 2, g.shape[2]), g.dtype),
                          memory_space=pltpu.MemorySpace.HBM) for g in grads]

    @pl.kernel(mesh=plsc.ScalarSubcoreMesh(axis_name="sequencer", num_cores=1), name=name,
               scratch_types=(pltpu.SemaphoreType.DMA((n,)), pltpu.SemaphoreType.DMA((n,))),
               compiler_params=pltpu.CompilerParams(collective_id=collective_id))
    def launch(send_sems, recv_sems):
        x, y, c, _ = _place()
        barrier = pltpu.get_barrier_semaphore()
        pl.semaphore_signal(barrier, inc=1, device_id=(x, y, 1 - c), device_id_type=MESH)
        pl.semaphore_wait(barrier, 1)
        cps = []
        for i in range(n):
            half = ins[i].shape[1] // 2
            cp = pltpu.make_async_remote_copy(
                src_ref=ins[i].at[:, pl.ds((1 - c) * half, half)], dst_ref=outs[i],
                send_sem=send_sems.at[i], recv_sem=recv_sems.at[i],
                device_id=(x, y, 1 - c), device_id_type=MESH)
            cp.start()
            cps.append(cp)
        for cp in cps:
            cp.wait()

    launch()
    return [o[...] for o in outs]


def _chip_scatter(name, parts):
    n = len(parts)

    def body(*refs):
        _scatter_copies(refs[:n], refs[n:2 * n], *refs[2 * n:])

    arrived = pl.pallas_call(
        body, name=name,
        out_shape=[jax.ShapeDtypeStruct(p.shape, p.dtype) for p in parts],
        in_specs=[ANY] * n, out_specs=[ANY] * n,
        scratch_shapes=[pltpu.SemaphoreType.DMA((n, 3)), pltpu.SemaphoreType.DMA((n, 3))],
    )(*parts)
    return _own_slots(parts, arrived)


def _scatter_copies(ins, outs, send_sems, recv_sems):
    x, y, c, chips = _place()
    me = 2 * x + y
    sends = []
    for i in range(len(ins)):
        for k, (cx, cy) in enumerate(chips):
            cp = pltpu.make_async_remote_copy(
                src_ref=ins[i].at[2 * cx + cy], dst_ref=outs[i].at[me],
                send_sem=send_sems.at[i, k], recv_sem=recv_sems.at[i, k],
                device_id=(cx, cy, c), device_id_type=MESH)
            cp.start()
            sends.append(cp)
    for i in range(len(ins)):
        for k, (cx, cy) in enumerate(chips):
            got = outs[i].at[2 * cx + cy]
            pltpu.make_async_remote_copy(
                src_ref=got, dst_ref=got, send_sem=send_sems.at[i, k], recv_sem=recv_sems.at[i, k],
                device_id=(x, y, c), device_id_type=MESH).wait_recv()
    for cp in sends:
        cp.wait_send()


def _own_slots(parts, arrived):
    me = 2 * lax.axis_index("x") + lax.axis_index("y")
    return [lax.dynamic_update_slice(a, lax.dynamic_slice_in_dim(p, me, 1, axis=0), (me, 0, 0))
            for p, a in zip(parts, arrived)]


def _chip_scatter_async(name, parts, collective_id):
    n = len(parts)
    ins = [jax.new_ref(p, memory_space=pltpu.MemorySpace.HBM) for p in parts]
    outs = [jax.empty_ref(jax.ShapeDtypeStruct(p.shape, p.dtype), memory_space=pltpu.MemorySpace.HBM) for p in parts]

    @pl.kernel(mesh=plsc.ScalarSubcoreMesh(axis_name="sequencer", num_cores=1), name=name,
               scratch_types=(pltpu.SemaphoreType.DMA((n, 3)), pltpu.SemaphoreType.DMA((n, 3))),
               compiler_params=pltpu.CompilerParams(collective_id=collective_id))
    def launch(send_sems, recv_sems):
        x, y, c, chips = _place()
        barrier = pltpu.get_barrier_semaphore()
        for cx, cy in chips:
            pl.semaphore_signal(barrier, inc=1, device_id=(cx, cy, c), device_id_type=MESH)
        pl.semaphore_wait(barrier, 3)
        _scatter_copies(ins, outs, send_sems, recv_sems)

    launch()
    return _own_slots(parts, [o[...] for o in outs])


def _pair_swap_async(name, halves, collective_id):
    n = len(halves)
    ins = [jax.new_ref(h, memory_space=pltpu.MemorySpace.HBM) for h in halves]
    outs = [jax.empty_ref(jax.ShapeDtypeStruct(h.shape, h.dtype), memory_space=pltpu.MemorySpace.HBM) for h in halves]

    @pl.kernel(mesh=plsc.ScalarSubcoreMesh(axis_name="sequencer", num_cores=1), name=name,
               scratch_types=(pltpu.SemaphoreType.DMA((n,)), pltpu.SemaphoreType.DMA((n,))),
               compiler_params=pltpu.CompilerParams(collective_id=collective_id))
    def launch(send_sems, recv_sems):
        x, y, c, _ = _place()
        barrier = pltpu.get_barrier_semaphore()
        pl.semaphore_signal(barrier, inc=1, device_id=(x, y, 1 - c), device_id_type=MESH)
        pl.semaphore_wait(barrier, 1)
        cps = []
        for i in range(n):
            cp = pltpu.make_async_remote_copy(
                src_ref=ins[i], dst_ref=outs[i], send_sem=send_sems.at[i], recv_sem=recv_sems.at[i],
                device_id=(x, y, 1 - c), device_id_type=MESH)
            cp.start()
            cps.append(cp)
        for cp in cps:
            cp.wait()

    launch()
    return [o[...] for o in outs]


def _pair_swap(name, halves):
    n = len(halves)

    def body(*refs):
        ins, outs = refs[:n], refs[n:2 * n]
        send_sems, recv_sems = refs[2 * n:]
        x, y, c, _ = _place()
        cps = []
        for i in range(n):
            cp = pltpu.make_async_remote_copy(
                src_ref=ins[i], dst_ref=outs[i], send_sem=send_sems.at[i], recv_sem=recv_sems.at[i],
                device_id=(x, y, 1 - c), device_id_type=MESH)
            cp.start()
            cps.append(cp)
        for cp in cps:
            cp.wait()

    return pl.pallas_call(
        body, name=name,
        out_shape=[jax.ShapeDtypeStruct(h.shape, h.dtype) for h in halves],
        in_specs=[ANY] * n, out_specs=[ANY] * n,
        scratch_shapes=[pltpu.SemaphoreType.DMA((n,)), pltpu.SemaphoreType.DMA((n,))],
    )(*halves)


def _row_block(rows, cols, n_bufs, budget=20 * 1024 * 1024):
    best = min(rows, 16)
    for b in range(16, rows + 1, 16):
        if rows % b == 0 and 2 * n_bufs * b * cols * 4 <= budget:
            best = b
    return best


def _pair_add(tag, grad, got, c_arr, out_dtype):
    _, rows, cols = grad.shape
    half = rows // 2
    bh = _row_block(half, cols, 3)
    nb = half // bh

    def body(c_ref, g_ref, a_ref, o_ref):
        o_ref[...] = (g_ref[...] + a_ref[...]).astype(out_dtype)

    return pl.pallas_call(
        body, name=f"pair_add_{tag}",
        out_shape=jax.ShapeDtypeStruct((4, half, cols), out_dtype),
        grid_spec=pltpu.PrefetchScalarGridSpec(
            num_scalar_prefetch=1, grid=(4, nb),
            in_specs=[pl.BlockSpec((None, bh, cols), lambda j, r, c: (j, c[0] * nb + r, 0)),
                      pl.BlockSpec((None, bh, cols), lambda j, r, c: (j, r, 0))],
            out_specs=pl.BlockSpec((None, bh, cols), lambda j, r, c: (j, r, 0))),
        compiler_params=_params(("parallel", "parallel")),
    )(c_arr, grad, got)


def _chip_add(tag, parts):
    _, half, cols = parts.shape
    bh = _row_block(half, cols, 5)

    def body(p_ref, o_ref):
        a, b, c, d = [p_ref[j].astype(F32) for j in range(4)]
        o_ref[...] = ((a + b) + c) + d

    return pl.pallas_call(
        body, name=f"chip_add_{tag}",
        out_shape=jax.ShapeDtypeStruct((half, cols), F32),
        grid=(half // bh,),
        in_specs=[pl.BlockSpec((4, bh, cols), lambda r: (0, r, 0))],
        out_specs=pl.BlockSpec((bh, cols), lambda r: (r, 0)),
        compiler_params=_params(("parallel",)),
    )(parts)


def _adamw(tag, w, mine, theirs, m, v, c_arr):
    rows, cols = w.shape
    half = rows // 2
    br = _row_block(half, cols, 9)
    nb = half // br

    def body(c_ref, w_ref, a_ref, b_ref, m_ref, v_ref, g_ref, d_ref, mo_ref, vo_ref):
        own = (pl.program_id(0) // nb) == c_ref[0]
        g = jnp.where(own, a_ref[...], b_ref[...])
        g_ref[...] = g
        m_new = ADAM_B1 * m_ref[...] + (1.0 - ADAM_B1) * g
        v_new = ADAM_B2 * v_ref[...] + (1.0 - ADAM_B2) * (g * g)
        m_hat = m_new / (1.0 - ADAM_B1 ** ADAM_STEP)
        v_hat = v_new / (1.0 - ADAM_B2 ** ADAM_STEP)
        d_ref[...] = -ADAM_LR * (m_hat / (jnp.sqrt(v_hat) + ADAM_EPS) + ADAM_WD * w_ref[...])
        mo_ref[...] = m_new
        vo_ref[...] = v_new

    spec = pl.BlockSpec((br, cols), lambda r, c: (r, 0))
    mine_spec = pl.BlockSpec((br, cols), lambda r, c: (jnp.clip(r - c[0] * nb, 0, nb - 1), 0))
    theirs_spec = pl.BlockSpec((br, cols), lambda r, c: (jnp.clip(r - (1 - c[0]) * nb, 0, nb - 1), 0))
    return pl.pallas_call(
        body, name=f"adamw_{tag}",
        out_shape=[jax.ShapeDtypeStruct((rows, cols), F32)] * 4,
        grid_spec=pltpu.PrefetchScalarGridSpec(
            num_scalar_prefetch=1, grid=(rows // br,),
            in_specs=[spec, mine_spec, theirs_spec, spec, spec], out_specs=[spec] * 4),
        compiler_params=_params(("arbitrary",)),
    )(c_arr, w, mine, theirs, m, v)


def _matmul(name, x, w, out_shape, grid, x_spec, w_spec, o_spec, *, nt=False, vmem=None):
    nk = grid[2]
    acc_shape = tuple(d for d in o_spec.block_shape if d is not None)

    def body(x_ref, w_ref, o_ref, acc_ref):
        k = pl.program_id(2)
        part = _dot_nt(x_ref[...], w_ref[...]) if nt else _dot(x_ref[...], w_ref[...])
        if nk == 1:
            o_ref[...] = part.astype(o_ref.dtype)
        else:
            @pl.when(k == 0)
            def _():
                acc_ref[...] = part

            @pl.when(k > 0)
            def _():
                acc_ref[...] += part

            @pl.when(k == nk - 1)
            def _():
                o_ref[...] = acc_ref[...].astype(o_ref.dtype)

    return pl.pallas_call(
        body, name=name, out_shape=out_shape, grid=grid,
        in_specs=[x_spec, w_spec], out_specs=o_spec,
        scratch_shapes=[pltpu.VMEM(acc_shape if nk > 1 else (8, 128), F32)],
        compiler_params=_params(("parallel", "parallel", "arbitrary"), vmem),
    )(x, w)


def _weight_grad(name, xt, dy, bn, out_rows=None):
    m, t = xt.shape
    n = dy.shape[1]
    bm = m if out_rows is None else out_rows
    bk = _k_tile(t)
    return _matmul(
        name, xt, dy, jax.ShapeDtypeStruct((m, n), F32), (m // bm, n // bn, t // bk),
        pl.BlockSpec((bm, bk), lambda a, b, k: (a, k)),
        pl.BlockSpec((bk, bn), lambda a, b, k: (k, b)),
        pl.BlockSpec((bm, bn), lambda a, b, k: (a, b)), vmem=VMEM_BIG)


def _weight_grad_t(name, xt, dy):
    m, t = xt.shape
    n = dy.shape[1]
    bn = min(n, 512)
    bk = _k_tile(t)
    nk = t // bk

    def body(x_ref, dy_ref, o_ref, acc_ref):
        k = pl.program_id(1)
        part = _dot(x_ref[...], dy_ref[...].astype(BF16))

        @pl.when(k == 0)
        def _():
            acc_ref[...] = part

        @pl.when(k > 0)
        def _():
            acc_ref[...] += part

        @pl.when(k == nk - 1)
        def _():
            o_ref[...] = acc_ref[...].T

    return pl.pallas_call(
        body, name=name, out_shape=jax.ShapeDtypeStruct((n, m), F32), grid=(n // bn, nk),
        in_specs=[pl.BlockSpec((m, bk), lambda b, k: (0, k)), pl.BlockSpec((bk, bn), lambda b, k: (k, b))],
        out_specs=pl.BlockSpec((bn, m), lambda b, k: (b, 0)),
        scratch_shapes=[pltpu.VMEM((m, bn), F32)],
        compiler_params=_params(("parallel", "arbitrary"), VMEM_BIG),
    )(xt, dy)


def _mix_in_bwd(pieces, wt, h, g, dh_in, post):
    t, d = h.shape
    tm = ROW_TILE // 2
    widths = [p.shape[1] for p in pieces]
    n = len(pieces)

    def body(*refs):
        dy_refs = refs[:n]
        w_ref, h_ref, g_ref, dhi_ref, xp_ref, gp_ref, dh_ref, dg_ref, dxp_ref, dgp_ref = refs[n:]
        first = pl.program_id(0) == 0

        @pl.when(first)
        def _():
            dg_ref[...] = jnp.zeros_like(dg_ref)

        dn, off = None, 0
        for dy_ref, wd in zip(dy_refs, widths):
            part = _dot(dy_ref[...].astype(BF16), w_ref[off:off + wd, :])
            dn = part if dn is None else dn + part
            off += wd
        dx, dg = _rms_bwd(h_ref[...], g_ref[...], dn)
        dh = dhi_ref[...] + dx
        dh_ref[...] = dh
        dg_ref[...] += dg
        _next_post_norm_bwd(dh, (xp_ref, gp_ref, dxp_ref, dgp_ref), post[2], first)

    row = pl.BlockSpec((tm, d), lambda i: (i, 0))
    vec = pl.BlockSpec((1, d), lambda i: (0, 0))
    return pl.pallas_call(
        body, name="mix_in_bwd",
        out_shape=[jax.ShapeDtypeStruct((t, d), F32), jax.ShapeDtypeStruct((1, d), F32),
                   jax.ShapeDtypeStruct((t, d), BF16), jax.ShapeDtypeStruct((1, d), F32)],
        grid=(t // tm,),
        in_specs=[pl.BlockSpec((tm, wd), lambda i: (i, 0)) for wd in widths]
        + [pl.BlockSpec(wt.shape, lambda i: (0, 0)), row, vec, row, row, vec],
        out_specs=[row, vec, row, vec],
        compiler_params=_params(("arbitrary",), VMEM_BIG),
    )(*pieces, wt, h, g, dh_in, post[0], post[1])


def _read_token_rows(src_hbm, buf, sems, i, n):
    tm = buf.shape[1]

    def first_tile():
        return pltpu.make_async_copy(src_hbm.at[pl.ds(0, tm - N_FRONT)], buf.at[0, pl.ds(N_FRONT, tm - N_FRONT)],
                                     sems.at[0])

    def tile(j):
        return pltpu.make_async_copy(src_hbm.at[pl.ds(pl.multiple_of(j * tm - N_FRONT, N_FRONT), tm)],
                                     buf.at[j % 2], sems.at[j % 2])

    @pl.when(i == 0)
    def _():
        buf[0, 0:N_FRONT, :] = jnp.zeros((N_FRONT, buf.shape[2]), buf.dtype)
        first_tile().start()

    @pl.when(i + 1 < n)
    def _():
        tile(i + 1).start()

    @pl.when(i == 0)
    def _():
        first_tile().wait()

    @pl.when(i > 0)
    def _():
        tile(i).wait()

    return buf.at[i % 2]


def _embed_norm(x, meta, g):
    seq, d = x.shape
    t = seq + N_FRONT
    tm = ROW_TILE

    def body(x_hbm, meta_ref, g_ref, h_ref, n_ref, nt_ref, buf, sems):
        i = pl.program_id(0)
        rows = _read_token_rows(x_hbm, buf, sems, i, t // tm)

        @pl.when(i == 0)
        def _():
            buf[0, ROW_PAD:N_FRONT, :] = meta_ref[...]

        h = rows[...]
        h_ref[...] = h
        y = h * _rstd(h) * g_ref[...]
        n_ref[...] = y.astype(BF16)
        nt_ref[...] = y.T.astype(BF16)

    row = pl.BlockSpec((tm, d), lambda i: (i, 0))
    return pl.pallas_call(
        body, name="embed_and_ffn1_pre_norm",
        out_shape=[jax.ShapeDtypeStruct((t, d), F32), jax.ShapeDtypeStruct((t, d), BF16),
                   jax.ShapeDtypeStruct((d, t), BF16)],
        grid=(t // tm,),
        in_specs=[ANY, pl.BlockSpec((N_META, d), lambda i: (0, 0)), pl.BlockSpec((1, d), lambda i: (0, 0))],
        out_specs=[row, row, pl.BlockSpec((d, tm), lambda i: (0, i))],
        scratch_shapes=[pltpu.VMEM((2, tm, d), F32), pltpu.SemaphoreType.DMA((2,))],
        compiler_params=_params(("arbitrary",)),
    )(x, meta, g)


def _slot_of(kk):
    return (kk % 2) * 2 + kk // 2


def _ffn_in(name, n, w4):
    t, d = n.shape
    cw = w4.shape[2]
    tm = ROW_TILE

    def body(x_ref, wg_ref, wu_ref, ab_ref, s_ref, st_ref):
        x = x_ref[...]
        a = _dot(x, wg_ref[...])
        b = _dot(x, wu_ref[...])
        ab_ref[:, :cw] = a.astype(BF16)
        ab_ref[:, cw:] = b.astype(BF16)
        s = a * _sigmoid(a) * b
        s_ref[...] = s.astype(BF16)
        st_ref[...] = s.T.astype(BF16)

    return pl.pallas_call(
        body, name=name,
        out_shape=[jax.ShapeDtypeStruct((t, 4 * cw), BF16), jax.ShapeDtypeStruct((t, 2 * cw), BF16),
                   jax.ShapeDtypeStruct((2 * cw, t), BF16)],
        grid=(2, t // tm),
        in_specs=[pl.BlockSpec((tm, d), lambda j, i: (i, 0)),
                  pl.BlockSpec((None, d, cw), lambda j, i: (j, 0, 0)),
                  pl.BlockSpec((None, d, cw), lambda j, i: (2 + j, 0, 0))],
        out_specs=[pl.BlockSpec((tm, 2 * cw), lambda j, i: (i, j)),
                   pl.BlockSpec((tm, cw), lambda j, i: (i, j)),
                   pl.BlockSpec((cw, tm), lambda j, i: (j, i))],
        compiler_params=_params(("parallel", "parallel"), VMEM_BIG),
    )(n, w4, w4)


def _mm_resid_norm(name, x, w, h, g_post, alpha, g_next):
    t, kdim = x.shape
    d = w.shape[1]
    tm = ROW_TILE
    with_next = g_next is not None

    def body(x_ref, w_ref, h_ref, gp_ref, gn_ref, f_ref, hn_ref, *rest):
        f = _dot(x_ref[...], w_ref[...])
        f_ref[...] = f
        hn = h_ref[...] + alpha * (f * _rstd(f) * gp_ref[...])
        hn_ref[...] = hn
        if with_next:
            y = hn * _rstd(hn) * gn_ref[...]
            rest[0][...] = y.astype(BF16)
            rest[1][...] = y.T.astype(BF16)

    row = lambda i: (i, 0)
    vec = pl.BlockSpec((1, d), lambda i: (0, 0))
    out_shape = [jax.ShapeDtypeStruct((t, d), F32), jax.ShapeDtypeStruct((t, d), F32)]
    out_specs = [pl.BlockSpec((tm, d), row), pl.BlockSpec((tm, d), row)]
    if with_next:
        out_shape += [jax.ShapeDtypeStruct((t, d), BF16), jax.ShapeDtypeStruct((d, t), BF16)]
        out_specs += [pl.BlockSpec((tm, d), row), pl.BlockSpec((d, tm), lambda i: (0, i))]
    return pl.pallas_call(
        body, name=name, out_shape=out_shape, grid=(t // tm,),
        in_specs=[pl.BlockSpec((tm, kdim), row), pl.BlockSpec((kdim, d), lambda i: (0, 0)),
                  pl.BlockSpec((tm, d), row), vec, vec],
        out_specs=out_specs,
        compiler_params=_params(("parallel",), VMEM_BIG),
    )(x, w, h, g_post, g_post if g_next is None else g_next)


def _in_proj(u, w):
    t, d = u.shape
    nz = w.shape[0]
    nq = 3 * ATTN_W
    tm = ROW_TILE // 2

    def body(u_ref, w_ref, qkv_ref, z_ref):
        qkv_ref[...] = _dot_nt(u_ref[...], w_ref[0:nq, :]).astype(BF16)
        z_ref[...] = _dot_nt(u_ref[...], w_ref[nq:, :])

    return pl.pallas_call(
        body, name="mix_in_proj",
        out_shape=[jax.ShapeDtypeStruct((t, nq), BF16), jax.ShapeDtypeStruct((t, nz - nq), F32)],
        grid=(t // tm,),
        in_specs=[pl.BlockSpec((tm, d), lambda i: (i, 0)), pl.BlockSpec((nz, d), lambda i: (0, 0))],
        out_specs=[pl.BlockSpec((tm, nq), lambda i: (i, 0)), pl.BlockSpec((tm, nz - nq), lambda i: (i, 0))],
        compiler_params=_params(("parallel",), VMEM_BIG),
    )(u, w)


def _gate_prep(z, b_pad, f_col):
    t = z.shape[0]
    tm = ROW_TILE

    def body(z_ref, b_ref, f_ref, carry_ref):
        i = pl.program_id(0)

        @pl.when(i == 0)
        def _():
            carry_ref[...] = jnp.zeros_like(carry_ref)

        xs = z_ref[...] + b_ref[...]
        logf = jnp.minimum(xs, 0.0) - jnp.log(1.0 + jnp.exp(-jnp.abs(xs)))
        row = i * tm + lax.broadcasted_iota(jnp.int32, (tm, 1), 0)
        logf = jnp.where(row >= ROW_PAD, logf, 0.0)
        tri = (lax.broadcasted_iota(jnp.int32, (tm, tm), 0) >= lax.broadcasted_iota(jnp.int32, (tm, tm), 1))
        f = jnp.dot(tri.astype(F32), logf, preferred_element_type=F32, precision=lax.Precision.HIGHEST)
        f = f + carry_ref[0:1, :]
        f_ref[...] = f
        carry_ref[...] = jnp.broadcast_to(f[tm - 1:tm, :], carry_ref.shape)

    return pl.pallas_call(
        body, name="forget_gate_cumsum", out_shape=jax.ShapeDtypeStruct((t, 128), F32),
        grid=(t // tm,),
        in_specs=[pl.BlockSpec((tm, 128), lambda i: (i, f_col // 128)), pl.BlockSpec((1, 128), lambda i: (0, 0))],
        out_specs=pl.BlockSpec((tm, 128), lambda i: (i, 0)),
        scratch_shapes=[pltpu.VMEM((8, 128), F32)],
        compiler_params=_params(("arbitrary",)),
    )(z, b_pad)


def _lane_halves():
    lane = lax.broadcasted_iota(jnp.int32, (1, 128), 1)
    return lane < HEAD_DIM


def _causal_mask(tq, tk, row0=0):
    row = row0 + lax.broadcasted_iota(jnp.int32, (tq, 1), 0)
    col = lax.broadcasted_iota(jnp.int32, (1, tk), 1)
    return col <= row


def _lane_one(lane):
    return (lax.broadcasted_iota(jnp.int32, (1, 128), 1) == lane).astype(BF16)


def _split3(x):
    hi = x.astype(BF16)
    rest = x - hi.astype(F32)
    mid = rest.astype(BF16)
    return hi, mid, (rest - mid.astype(F32)).astype(BF16)


def _split3_glue(x):
    hi = lax.reduce_precision(x, 8, 7)
    mid = lax.reduce_precision(x - hi, 8, 7)
    lo = lax.reduce_precision((x - hi) - mid, 8, 7)
    return hi.astype(BF16), mid.astype(BF16), lo.astype(BF16)


def _aug_pairs(cols):
    t = cols[0].shape[0]
    a = jnp.pad(jnp.stack(cols, axis=2), ((0, 0), (0, 0), (0, HEAD_DIM - len(cols))))
    a = a.reshape(t, 4, 2, HEAD_DIM)[:, :, ::-1, :]
    return jnp.transpose(a.reshape(t, 4, 128), (1, 0, 2))


def _attn_bias_operands(f_heads, lse_heads=None):
    t = f_heads.shape[0]
    one = jnp.ones((t, HEADS), BF16)
    row = lax.broadcasted_iota(jnp.int32, (t, 1), 0)
    fq = _split3_glue(f_heads)
    fk = _split3_glue(jnp.where(row < ROW_PAD, 1e9, f_heads))
    q_cols = list(fq) + [one] * 3
    k_cols = [one] * 3 + [-c for c in fk]
    if lse_heads is not None:
        q_cols += [-c for c in _split3_glue(lse_heads)]
        k_cols += [one] * 3
    return _aug_pairs(q_cols), _aug_pairs(k_cols)


def _attn_fwd(z, aug_q, aug_k):
    t = z.shape[0]
    tq = tk = ROW_TILE
    nq = t // tq
    grp = ATTN_KV_GROUP
    steps = [(qi, ka) for qi in range(nq) for ka in range(0, qi + 1, grp)]
    q_tab = jnp.array([qi for qi, _ in steps], jnp.int32)
    k_tab = jnp.array([ka for _, ka in steps], jnp.int32)

    def body(qt_ref, kt_ref, q_ref, *refs):
        k_refs, v_refs, aq_ref, ak_refs = refs[:grp], refs[grp:2 * grp], refs[2 * grp], refs[2 * grp + 1:3 * grp + 1]
        o_ref, lse_ref, m_ref, l_ref, acc_ref = refs[3 * grp + 1:]
        step = pl.program_id(1)
        qi, ka = qt_ref[step], kt_ref[step]

        @pl.when(ka == 0)
        def _():
            m_ref[...] = jnp.full_like(m_ref, NEG)
            l_ref[...] = jnp.zeros_like(l_ref)
            acc_ref[...] = jnp.zeros_like(acc_ref)

        def sweep(diagonal):
            first = _lane_halves()
            halves = (first, jnp.logical_not(first))
            q = (q_ref[...] * (HEAD_DIM ** -0.5)).astype(BF16)
            aq = aq_ref[...]
            qa = [jnp.where(lanes, q, aq) for lanes in halves]
            blocks = list(zip(k_refs, v_refs, ak_refs, diagonal))
            s = []
            for k_ref, _, ak_ref, diag in blocks:
                k, ak = k_ref[...].astype(BF16), ak_ref[...]
                for hh, lanes in enumerate(halves):
                    s_c = _dot_nt(qa[hh], jnp.where(lanes, k, ak))
                    s.append(jnp.where(_causal_mask(tq, tk), s_c, NEG) if diag else s_c)
            nb = len(blocks)
            m_prev = [m_ref[:, c0:c0 + 1] for c0 in (0, HEAD_DIM)]
            m_new = []
            for hh in range(2):
                m_h = m_prev[hh]
                for b in range(nb):
                    m_h = jnp.maximum(m_h, jnp.max(s[2 * b + hh], axis=1, keepdims=True))
                m_new.append(m_h)
            pv = [None, None]
            for b, (_, v_ref, _, _) in enumerate(blocks):
                v = v_ref[...].astype(BF16)
                for hh, (lanes, a0) in enumerate(zip(halves, (HEAD_DIM, 0))):
                    part = _dot(jnp.exp(s[2 * b + hh] - m_new[hh]).astype(BF16), jnp.where(lanes, v, _lane_one(a0)))
                    pv[hh] = part if pv[hh] is None else pv[hh] + part
            al0, al1 = [jnp.exp(mp - m_h) for mp, m_h in zip(m_prev, m_new)]
            l0 = al0 * l_ref[:, 0:1] + pv[0][:, HEAD_DIM:HEAD_DIM + 1]
            l1 = al1 * l_ref[:, HEAD_DIM:HEAD_DIM + 1] + pv[1][:, 0:1]
            acc_ref[...] = acc_ref[...] * jnp.where(first, al0, al1) + jnp.where(first, pv[0], pv[1])
            m_ref[...] = jnp.where(first, m_new[0], m_new[1])
            l_ref[...] = jnp.where(first, l0, l1)

        def finish():
            o_ref[...] = acc_ref[...] / l_ref[...]
            lse_ref[...] = m_ref[...] + jnp.log(l_ref[...])

        @pl.when(ka + grp - 1 < qi)
        def _():
            sweep((False,) * grp)

        for nb in range(1, grp + 1):
            @pl.when(ka + nb - 1 == qi)
            def _(nb=nb):
                sweep((False,) * (nb - 1) + (True,))
                finish()

    def kblock(j):
        return lambda s, qt, kt: jnp.minimum(kt[s] + j, qt[s])

    kbs = [kblock(j) for j in range(grp)]
    return pl.pallas_call(
        body, name="attention_fwd",
        out_shape=[jax.ShapeDtypeStruct((t, ATTN_W), F32), jax.ShapeDtypeStruct((t, ATTN_W), F32)],
        grid_spec=pltpu.PrefetchScalarGridSpec(
            num_scalar_prefetch=2, grid=(4, len(steps)),
            in_specs=[pl.BlockSpec((tq, 128), lambda p, s, qt, kt: (qt[s], p))]
            + [pl.BlockSpec((tk, 128), functools.partial(lambda p, s, qt, kt, kb: (kb(s, qt, kt), 4 + p), kb=kb))
               for kb in kbs]
            + [pl.BlockSpec((tk, 128), functools.partial(lambda p, s, qt, kt, kb: (kb(s, qt, kt), 8 + p), kb=kb))
               for kb in kbs]
            + [pl.BlockSpec((None, tq, 128), lambda p, s, qt, kt: (p, qt[s], 0))]
            + [pl.BlockSpec((None, tk, 128), functools.partial(lambda p, s, qt, kt, kb: (p, kb(s, qt, kt), 0), kb=kb))
               for kb in kbs],
            out_specs=[pl.BlockSpec((tq, 128), lambda p, s, qt, kt: (qt[s], p)),
                       pl.BlockSpec((tq, 128), lambda p, s, qt, kt: (qt[s], p))],
            scratch_shapes=[pltpu.VMEM((tq, 128), F32)] * 3),
        compiler_params=_params(("parallel", "arbitrary"), VMEM_BIG),
    )(q_tab, k_tab, z, *([z] * (2 * grp)), aug_q, *([aug_k] * grp))


def _attn_bwd(z, aug_q, aug_k, o, do):
    t = z.shape[0]
    tq = tk = ROW_TILE
    nq = t // tq
    grp = ATTN_Q_GROUP
    steps = [(qa, ki) for ki in range(nq) for qa in range(ki, nq, grp)]
    q_tab = jnp.array([qa for qa, _ in steps], jnp.int32)
    k_tab = jnp.array([ki for _, ki in steps], jnp.int32)
    tn = (((0,), (0,)), ((), ()))

    def body(qt_ref, kt_ref, *refs):
        q_refs, (k_ref, v_ref) = refs[:grp], refs[grp:grp + 2]
        aq_refs, ak_ref = refs[grp + 2:2 * grp + 2], refs[2 * grp + 2]
        o_refs, do_refs = refs[2 * grp + 3:3 * grp + 3], refs[3 * grp + 3:4 * grp + 3]
        dq_ref, dk_ref, dv_ref, dfk_ref, dfq_ref = refs[4 * grp + 3:]
        step = pl.program_id(1)
        qa, ki = qt_ref[step], kt_ref[step]

        def rows(j):
            return pl.ds(pl.multiple_of((qa + j) * tq, tq), tq)

        for j in range(grp):
            @pl.when((ki == 0) & (qa + j < nq))
            def _(j=j):
                dq_ref[rows(j), :] = jnp.zeros((tq, 128), F32)
                dfq_ref[rows(j), :] = jnp.zeros((tq, 128), F32)

        @pl.when(qa == ki)
        def _():
            dk_ref[...] = jnp.zeros_like(dk_ref)
            dv_ref[...] = jnp.zeros_like(dv_ref)
            dfk_ref[...] = jnp.zeros_like(dfk_ref)

        def sweep(nb, diagonal):
            first = _lane_halves()
            lane = lax.broadcasted_iota(jnp.int32, (1, 128), 1)
            scale = HEAD_DIM ** -0.5
            halves = (first, jnp.logical_not(first))
            spare = (HEAD_DIM, 0)
            k = k_ref[...].astype(BF16)
            v = v_ref[...].astype(BF16)
            ak = ak_ref[...]
            k_bias = [jnp.where(lanes, k, ak) for lanes in halves]
            k_ones = [jnp.where(lanes, k, _lane_one(a)) for lanes, a in zip(halves, spare)]
            v_ones = [jnp.where(lanes, v, ((lane >= a) & (lane < a + 3)).astype(BF16)) for lanes, a in zip(halves, spare)]
            chains = [(j, hh) for j in range(nb) for hh in range(2)]
            q16, do16, dos = [], [], []
            for j in range(nb):
                q16.append((q_refs[j][...] * scale).astype(BF16))
                do_ = do_refs[j][...]
                do16.append(do_.astype(BF16))
                od = o_refs[j][...] * do_
                for lanes, a in zip(halves, spare):
                    d_hi, d_mid, d_lo = _split3(jnp.sum(jnp.where(lanes, od, 0.0), axis=1, keepdims=True))
                    minus_delta = jnp.where(lane == a, -d_hi, jnp.where(lane == a + 1, -d_mid,
                                            jnp.where(lane == a + 2, -d_lo, jnp.zeros((), BF16))))
                    dos.append(jnp.where(lanes, do16[j], minus_delta))
            s = [_dot_nt(jnp.where(halves[hh], q16[j], aq_refs[j][...]), k_bias[hh]) for j, hh in chains]
            dp = [_dot_nt(dos[2 * j + hh], v_ones[hh]) for j, hh in chains]
            p = [jnp.exp(s_c) for s_c in s]
            if diagonal:
                p = [jnp.where(_causal_mask(tq, tk), p_c, 0.0) if j == 0 else p_c for p_c, (j, _) in zip(p, chains)]
            ds16 = [(p_c * dp_c).astype(BF16) for p_c, dp_c in zip(p, dp)]
            dv, dk = [None, None], [None, None]
            for c, (j, hh) in enumerate(chains):
                lanes = halves[hh]
                dv_c = lax.dot_general(jnp.where(lanes, do16[j], jnp.zeros((), BF16)), p[c].astype(BF16), tn,
                                       preferred_element_type=F32)
                dk_c = lax.dot_general(jnp.where(lanes, q16[j], _lane_one(spare[hh])), ds16[c], tn,
                                       preferred_element_type=F32)
                dv[hh] = dv_c if dv[hh] is None else dv[hh] + dv_c
                dk[hh] = dk_c if dk[hh] is None else dk[hh] + dk_c
            dv = [x.T for x in dv]
            dk = [x.T for x in dk]
            for j in range(nb):
                dq0, dq1 = [_dot(ds16[2 * j + hh], k_ones[hh]) for hh in range(2)]
                dq_ref[rows(j), :] += jnp.where(first, dq0, dq1) * scale
                dfq_ref[rows(j), :] += jnp.where(first, dq0[:, HEAD_DIM:HEAD_DIM + 1], dq1[:, 0:1])
            dk_ref[...] += jnp.where(first, dk[0], dk[1])
            dfk_ref[...] += jnp.where(first, dk[0][:, HEAD_DIM:HEAD_DIM + 1], dk[1][:, 0:1])
            dv_ref[...] += dv[0] + dv[1]

        for nb in range(1, grp + 1):
            exists = (qa + grp <= nq) if nb == grp else (qa + nb == nq)
            for diagonal in (False, True):
                @pl.when(exists & ((qa == ki) == diagonal))
                def _(nb=nb, diagonal=diagonal):
                    sweep(nb, diagonal)

    def qblock(j):
        return lambda s, qt: jnp.minimum(qt[s] + j, nq - 1)

    qbs = [qblock(j) for j in range(grp)]
    qcol = [functools.partial(lambda p, s, qt, kt, qb: (qb(s, qt), p), qb=qb) for qb in qbs]
    krow = lambda p, s, qt, kt: (kt[s], p)
    return pl.pallas_call(
        body, name="attention_bwd",
        out_shape=[jax.ShapeDtypeStruct((t, ATTN_W), F32)] * 5,
        grid_spec=pltpu.PrefetchScalarGridSpec(
            num_scalar_prefetch=2, grid=(4, len(steps)),
            in_specs=[pl.BlockSpec((tq, 128), m) for m in qcol]
            + [pl.BlockSpec((tk, 128), lambda p, s, qt, kt: (kt[s], 4 + p)),
               pl.BlockSpec((tk, 128), lambda p, s, qt, kt: (kt[s], 8 + p))]
            + [pl.BlockSpec((None, tq, 128), functools.partial(lambda p, s, qt, kt, qb: (p, qb(s, qt), 0), qb=qb))
               for qb in qbs]
            + [pl.BlockSpec((None, tk, 128), lambda p, s, qt, kt: (p, kt[s], 0))]
            + [pl.BlockSpec((tq, 128), m) for m in qcol] + [pl.BlockSpec((tq, 128), m) for m in qcol],
            out_specs=[pl.BlockSpec((t, 128), lambda p, s, qt, kt: (0, p)),
                       pl.BlockSpec((tk, 128), krow), pl.BlockSpec((tk, 128), krow), pl.BlockSpec((tk, 128), krow),
                       pl.BlockSpec((t, 128), lambda p, s, qt, kt: (0, p))]),
        compiler_params=_params(("parallel", "arbitrary"), VMEM_BIG),
    )(q_tab, k_tab, *([z] * grp), z, z, *([aug_q] * grp), aug_k, *([o] * grp), *([do] * grp))


def _shifted(prev_rows, x, shift):
    tm = x.shape[0]
    return pltpu.roll(jnp.concatenate([prev_rows, x], axis=0), shift, 0)[8:8 + tm]


def _ahead(x, next_rows, shift):
    tm = x.shape[0]
    return pltpu.roll(jnp.concatenate([x, next_rows], axis=0), tm + 8 - shift, 0)[0:tm]


def _conv_col0(z):
    return (z.shape[1] - F_PAD - 3 * CONV_W) // CONV_W


def _conv_specs(tm, c0):
    cols = (c0, c0 + 1, c0 + 2)
    tiles = [pl.BlockSpec((tm, CONV_W), functools.partial(lambda i, c: (i, c), c=c)) for c in cols]
    halos = [pl.BlockSpec((8, CONV_W), functools.partial(lambda i, c: (jnp.maximum(i * (tm // 8) - 1, 0), c), c=c))
             for c in cols]
    return tiles, halos


def _conv_gate(z, conv_w):
    t = z.shape[0]
    tm = ROW_TILE
    nt = t // tm

    def body(cb_ref, cc_ref, ci_ref, hc_ref, hi_ref, w_ref, g_ref, gt_ref):
        i = pl.program_id(0)
        cc = cc_ref[...] * ci_ref[...]
        prev = jnp.where(i > 0, hc_ref[...] * hi_ref[...], 0.0)
        conv = w_ref[0:1, :] * _shifted(prev, cc, 2) + w_ref[1:2, :] * _shifted(prev, cc, 1) + w_ref[2:3, :] * cc
        g = cb_ref[...] * conv
        g_ref[...] = g.astype(BF16)
        gt_ref[...] = g.T.astype(BF16)

    (cb, cc, ci), (_, hc, hi) = _conv_specs(tm, _conv_col0(z))
    return pl.pallas_call(
        body, name="conv_gate_fwd",
        out_shape=[jax.ShapeDtypeStruct((t, CONV_W), BF16), jax.ShapeDtypeStruct((CONV_W, t), BF16)],
        grid=(nt,),
        in_specs=[cb, cc, ci, hc, hi, pl.BlockSpec((8, CONV_W), lambda i: (0, 0))],
        out_specs=[pl.BlockSpec((tm, CONV_W), lambda i: (i, 0)), pl.BlockSpec((CONV_W, tm), lambda i: (0, i))],
        compiler_params=_params(("parallel",)),
    )(z, z, z, z, z, conv_w)


def _conv_bwd(z, dg, conv_w):
    t = z.shape[0]
    tm = ROW_TILE
    nt = t // tm

    def body(cb_ref, cc_ref, ci_ref, hc_ref, hi_ref, dg_ref, ncb_ref, ndg_ref, w_ref, dz_ref, dw_ref):
        i = pl.program_id(0)

        @pl.when(i == 0)
        def _():
            dw_ref[...] = jnp.zeros_like(dw_ref)

        cb, c_c, c_in = cb_ref[...], cc_ref[...], ci_ref[...]
        cc = c_c * c_in
        prev = jnp.where(i > 0, hc_ref[...] * hi_ref[...], 0.0)
        cc1, cc2 = _shifted(prev, cc, 1), _shifted(prev, cc, 2)
        w0, w1, w2 = w_ref[0:1, :], w_ref[1:2, :], w_ref[2:3, :]
        conv = w0 * cc2 + w1 * cc1 + w2 * cc
        dgv = dg_ref[...]
        dconv = dgv * cb
        nxt = jnp.where(i < nt - 1, ndg_ref[...] * ncb_ref[...], 0.0)
        dcc = w2 * dconv + w1 * _ahead(dconv, nxt, 1) + w0 * _ahead(dconv, nxt, 2)
        dz_ref[:, 0:CONV_W] = (dgv * conv).astype(BF16)
        dz_ref[:, CONV_W:2 * CONV_W] = (dcc * c_in).astype(BF16)
        dz_ref[:, 2 * CONV_W:] = (dcc * c_c).astype(BF16)
        dw_ref[0:1, :] += jnp.sum(dconv * cc2, axis=0, keepdims=True)
        dw_ref[1:2, :] += jnp.sum(dconv * cc1, axis=0, keepdims=True)
        dw_ref[2:3, :] += jnp.sum(dconv * cc, axis=0, keepdims=True)

    c0 = _conv_col0(z)
    (cb, cc, ci), (_, hc, hi) = _conv_specs(tm, c0)
    nxt = lambda i, c: (jnp.minimum((i + 1) * (tm // 8), t // 8 - 1), c)
    return pl.pallas_call(
        body, name="conv_gate_bwd",
        out_shape=[jax.ShapeDtypeStruct((t, 3 * CONV_W), BF16), jax.ShapeDtypeStruct((8, CONV_W), F32)],
        grid=(nt,),
        in_specs=[cb, cc, ci, hc, hi, pl.BlockSpec((tm, CONV_W), lambda i: (i, 0)),
                  pl.BlockSpec((8, CONV_W), lambda i: nxt(i, c0)), pl.BlockSpec((8, CONV_W), lambda i: nxt(i, 0)),
                  pl.BlockSpec((8, CONV_W), lambda i: (0, 0))],
        out_specs=[pl.BlockSpec((tm, 3 * CONV_W), lambda i: (i, 0)), pl.BlockSpec((8, CONV_W), lambda i: (0, 0))],
        compiler_params=_params(("arbitrary",)),
    )(z, z, z, z, z, dg, z, dg, conv_w)


def _branch_mix(z, o, g, w_ab, w_cb, d):
    t = z.shape[0]
    tm = ROW_TILE
    ga_col = 0

    def body(o_ref, g_ref, ga_ref, gc_ref, wa_ref, wc_ref, mp_ref, mpt_ref, ot_ref):
        o_ = o_ref[...]
        ya = _dot(o_.astype(BF16), wa_ref[...])
        yc = _dot(g_ref[...], wc_ref[...])
        mp = _sigmoid(ga_ref[...]) * ya + _sigmoid(gc_ref[...]) * yc
        mp_ref[...] = mp.astype(BF16)
        mpt_ref[...] = mp.T.astype(BF16)
        ot_ref[...] = o_.T.astype(BF16)

    return pl.pallas_call(
        body, name="branch_mix_fwd",
        out_shape=[jax.ShapeDtypeStruct((t, d), BF16), jax.ShapeDtypeStruct((d, t), BF16),
                   jax.ShapeDtypeStruct((ATTN_W, t), BF16)],
        grid=(t // tm,),
        in_specs=[pl.BlockSpec((tm, ATTN_W), lambda i: (i, 0)), pl.BlockSpec((tm, CONV_W), lambda i: (i, 0)),
                  pl.BlockSpec((tm, d), lambda i: (i, ga_col)), pl.BlockSpec((tm, d), lambda i: (i, ga_col + 1)),
                  pl.BlockSpec((ATTN_W, d), lambda i: (0, 0)), pl.BlockSpec((CONV_W, d), lambda i: (0, 0))],
        out_specs=[pl.BlockSpec((tm, d), lambda i: (i, 0)), pl.BlockSpec((d, tm), lambda i: (0, i)),
                   pl.BlockSpec((ATTN_W, tm), lambda i: (0, i))],
        compiler_params=_params(("parallel",), VMEM_BIG),
    )(o, g, z, z, w_ab, w_cb)


def _branch_bwd(z, o, g, dmixed, w_out, w_ab, w_cb, d):
    t = z.shape[0]
    tm = ROW_TILE // 2
    ga_col = 0

    def body(dm_ref, o_ref, g_ref, ga_ref, gc_ref, wo_ref, wa_ref, wc_ref, dya_ref, dyc_ref, dgt_ref, do_ref, dg_ref):
        dmp = _dot_nt(dm_ref[...], wo_ref[...])
        ya = _dot(o_ref[...].astype(BF16), wa_ref[...])
        yc = _dot(g_ref[...], wc_ref[...])
        sa, sc = _sigmoid(ga_ref[...]), _sigmoid(gc_ref[...])
        dya = (dmp * sa).astype(BF16)
        dyc = (dmp * sc).astype(BF16)
        dya_ref[...] = dya
        dyc_ref[...] = dyc
        dgt_ref[:, :d] = (dmp * ya * sa * (1.0 - sa)).astype(BF16)
        dgt_ref[:, d:] = (dmp * yc * sc * (1.0 - sc)).astype(BF16)
        do_ref[...] = _dot_nt(dya, wa_ref[...])
        dg_ref[...] = _dot_nt(dyc, wc_ref[...])

    row = lambda i: (i, 0)
    fixed = lambda i: (0, 0)
    return pl.pallas_call(
        body, name="branch_mix_bwd",
        out_shape=[jax.ShapeDtypeStruct((t, d), BF16), jax.ShapeDtypeStruct((t, d), BF16),
                   jax.ShapeDtypeStruct((t, 2 * d), BF16), jax.ShapeDtypeStruct((t, ATTN_W), F32),
                   jax.ShapeDtypeStruct((t, CONV_W), F32)],
        grid=(t // tm,),
        in_specs=[pl.BlockSpec((tm, d), row), pl.BlockSpec((tm, ATTN_W), row), pl.BlockSpec((tm, CONV_W), row),
                  pl.BlockSpec((tm, d), lambda i: (i, ga_col)), pl.BlockSpec((tm, d), lambda i: (i, ga_col + 1)),
                  pl.BlockSpec((d, d), fixed), pl.BlockSpec((ATTN_W, d), fixed), pl.BlockSpec((CONV_W, d), fixed)],
        out_specs=[pl.BlockSpec((tm, d), row), pl.BlockSpec((tm, d), row), pl.BlockSpec((tm, 2 * d), row),
                   pl.BlockSpec((tm, ATTN_W), row), pl.BlockSpec((tm, CONV_W), row)],
        compiler_params=_params(("parallel",), VMEM_BIG),
    )(dmixed, o, g, z, z, w_out, w_ab, w_cb)


def _loss_norm_bwd(h, target, f, g_post, alpha):
    t, d = h.shape
    tm = ROW_TILE

    def body(h_ref, t_hbm, f_ref, g_ref, dh_ref, df_ref, dg_ref, loss_ref, t_buf, sems):
        i = pl.program_id(0)

        @pl.when(i == 0)
        def _():
            loss_ref[...] = jnp.zeros_like(loss_ref)
            dg_ref[...] = jnp.zeros_like(dg_ref)

        target = _read_token_rows(t_hbm, t_buf, sems, i, t // tm)
        row = i * tm + lax.broadcasted_iota(jnp.int32, (tm, 1), 0)
        err = jnp.where(row >= N_FRONT, h_ref[...] - target[...], 0.0)
        dy = err * (1.0 / d)
        dh_ref[...] = dy
        per_row = jnp.sum(err * err, axis=1, keepdims=True) * (1.0 / d)
        loss_ref[...] += 0.5 * jnp.sum(per_row, axis=0, keepdims=True)
        dx, dg = _rms_bwd(f_ref[...], g_ref[...], dy)
        df_ref[...] = (alpha * dx).astype(BF16)
        dg_ref[...] += alpha * dg

    row = pl.BlockSpec((tm, d), lambda i: (i, 0))
    vec = pl.BlockSpec((1, d), lambda i: (0, 0))
    return pl.pallas_call(
        body, name="loss_and_post_norm_bwd",
        out_shape=[jax.ShapeDtypeStruct((t, d), F32), jax.ShapeDtypeStruct((t, d), BF16),
                   jax.ShapeDtypeStruct((1, d), F32), jax.ShapeDtypeStruct((1, 128), F32)],
        grid=(t // tm,),
        in_specs=[row, ANY, row, vec],
        out_specs=[row, row, vec, pl.BlockSpec((1, 128), lambda i: (0, 0))],
        scratch_shapes=[pltpu.VMEM((2, tm, d), F32), pltpu.SemaphoreType.DMA((2,))],
        compiler_params=_params(("arbitrary",)),
    )(h, target, f, g_post)


def _ffn_bwd_mid(name, df, w_out, ab):
    t, d = df.shape
    cw = ab.shape[1] // 4
    tm = ROW_TILE

    def body(df_ref, w_ref, ab_ref, o_ref):
        ds = _dot_nt(df_ref[...], w_ref[...])
        a = ab_ref[:, :cw].astype(F32)
        b = ab_ref[:, cw:].astype(F32)
        sg = _sigmoid(a)
        o_ref[:, :cw] = (ds * b * (sg * (1.0 + a * (1.0 - sg)))).astype(BF16)
        o_ref[:, cw:] = (ds * (a * sg)).astype(BF16)

    return pl.pallas_call(
        body, name=name, out_shape=jax.ShapeDtypeStruct((t, 4 * cw), BF16),
        grid=(2, t // tm),
        in_specs=[pl.BlockSpec((tm, d), lambda j, i: (i, 0)), pl.BlockSpec((cw, d), lambda j, i: (j, 0)),
                  pl.BlockSpec((tm, 2 * cw), lambda j, i: (i, j))],
        out_specs=pl.BlockSpec((tm, 2 * cw), lambda j, i: (i, j)),
        compiler_params=_params(("parallel", "parallel"), VMEM_BIG),
    )(df, w_out, ab)


def _next_post_norm_bwd(dh, post_refs, alpha, first):
    x_ref, g_ref, dx_ref, dg_ref = post_refs

    @pl.when(first)
    def _():
        dg_ref[...] = jnp.zeros_like(dg_ref)

    dx, dg = _rms_bwd(x_ref[...], g_ref[...], dh)
    dx_ref[...] = (alpha * dx).astype(BF16)
    dg_ref[...] += alpha * dg


def _mm_nt_norm_bwd(name, dy, w, h, g, dh_in, post=None):
    t, kdim = dy.shape
    d = h.shape[1]
    tm = ROW_TILE // 2

    def body(dy_ref, w_ref, h_ref, g_ref, dhi_ref, *rest):
        dh_ref, dg_ref = rest[-4:-2] if post else rest
        first = pl.program_id(0) == 0

        @pl.when(first)
        def _():
            dg_ref[...] = jnp.zeros_like(dg_ref)

        cw = w_ref.shape[2]
        dn = _dot_nt(dy_ref[:, 0:cw], w_ref[_slot_of(0)])
        for k in range(1, 4):
            dn += _dot_nt(dy_ref[:, k * cw:(k + 1) * cw], w_ref[_slot_of(k)])
        dx, dg = _rms_bwd(h_ref[...], g_ref[...], dn)
        dh = dhi_ref[...] + dx
        dh_ref[...] = dh
        dg_ref[...] += dg
        if post:
            _next_post_norm_bwd(dh, rest[0:2] + rest[-2:], post[2], first)

    row = pl.BlockSpec((tm, d), lambda i: (i, 0))
    vec = pl.BlockSpec((1, d), lambda i: (0, 0))
    out_shape = [jax.ShapeDtypeStruct((t, d), F32), jax.ShapeDtypeStruct((1, d), F32)]
    if post:
        out_shape += [jax.ShapeDtypeStruct((t, d), BF16), jax.ShapeDtypeStruct((1, d), F32)]
    return pl.pallas_call(
        body, name=name, out_shape=out_shape, grid=(t // tm,),
        in_specs=[pl.BlockSpec((tm, kdim), lambda i: (i, 0)), pl.BlockSpec(w.shape, lambda i: (0,) * w.ndim),
                  row, vec, row] + ([row, vec] if post else []),
        out_specs=[row, vec] + ([row, vec] if post else []),
        compiler_params=_params(("arbitrary",), VMEM_BIG),
    )(dy, w, h, g, dh_in, *(post[:2] if post else ()))


def _gate_bwd(dfq, dfk, z, b_pad, f_col):
    t = z.shape[0]
    tm = ROW_TILE
    nt = t // tm

    def body(dq_ref, dk_ref, z_ref, b_ref, dz_ref, db_ref, carry_ref):
        i = pl.program_id(0)

        @pl.when(i == 0)
        def _():
            carry_ref[...] = jnp.zeros_like(carry_ref)
            db_ref[...] = jnp.zeros_like(db_ref)

        pick = (lax.broadcasted_iota(jnp.int32, (ATTN_W, 128), 0)
                == HEAD_DIM * lax.broadcasted_iota(jnp.int32, (ATTN_W, 128), 1)).astype(F32)
        d_heads = jnp.dot(dq_ref[...] - dk_ref[...], pick, preferred_element_type=F32,
                          precision=lax.Precision.HIGHEST)
        tri = (lax.broadcasted_iota(jnp.int32, (tm, tm), 0) <= lax.broadcasted_iota(jnp.int32, (tm, tm), 1))
        tail = jnp.dot(tri.astype(F32), d_heads, preferred_element_type=F32, precision=lax.Precision.HIGHEST)
        tail = tail + carry_ref[0:1, :]
        carry_ref[...] = jnp.broadcast_to(tail[0:1, :], carry_ref.shape)
        row = (nt - 1 - i) * tm + lax.broadcasted_iota(jnp.int32, (tm, 1), 0)
        dlogit = jnp.where(row >= ROW_PAD, tail * _sigmoid(-(z_ref[...] + b_ref[...])), 0.0)
        dz_ref[...] = jnp.zeros_like(dz_ref)
        dz_ref[:, 0:128] = dlogit.astype(BF16)
        db_ref[...] += jnp.sum(dlogit, axis=0, keepdims=True)

    rev = lambda i: (nt - 1 - i, 0)
    return pl.pallas_call(
        body, name="forget_gate_bwd",
        out_shape=[jax.ShapeDtypeStruct((t, F_PAD), BF16), jax.ShapeDtypeStruct((1, 128), F32)],
        grid=(nt,),
        in_specs=[pl.BlockSpec((tm, ATTN_W), rev), pl.BlockSpec((tm, ATTN_W), rev),
                  pl.BlockSpec((tm, 128), lambda i: (nt - 1 - i, f_col // 128)),
                  pl.BlockSpec((1, 128), lambda i: (0, 0))],
        out_specs=[pl.BlockSpec((tm, F_PAD), rev), pl.BlockSpec((1, 128), lambda i: (0, 0))],
        scratch_shapes=[pltpu.VMEM((8, 128), F32)],
        compiler_params=_params(("arbitrary",)),
    )(dfq, dfk, z, b_pad)


def _ffn_fwd(tag, n, w_in4, w_out, h, g_post, g_next):
    ab, s, s_t = _ffn_in(f"{tag}_in_fwd", n, w_in4)
    outs = _mm_resid_norm(f"{tag}_out_fwd", s, w_out, h, g_post, 0.5, g_next)
    return ab, s_t, outs


def _ffn_bwd_weights(tag, df, ab, s_t, n_t, w_in4, w_out):
    d, cw = w_in4.shape[1], w_in4.shape[2]
    t = df.shape[0]
    dw_out = _weight_grad(f"{tag}_dw_out", s_t, df, d, out_rows=cw // 2)
    dab = _ffn_bwd_mid(f"{tag}_mid_bwd", df, w_out, ab)
    bk = _k_tile(t)
    dw_in = _matmul(
        f"{tag}_dw_in", n_t, dab, jax.ShapeDtypeStruct((4, d, cw), F32), (1, 4, t // bk),
        pl.BlockSpec((d, bk), lambda a, b, k: (0, k)), pl.BlockSpec((bk, cw), lambda a, b, k: (k, b)),
        pl.BlockSpec((None, d, cw), lambda a, b, k: (_slot_of(b), 0, 0)), vmem=VMEM_BIG)
    return dab, dw_in, dw_out


LOSS_ROW = 12


def _pack_small(meta, conv, gains, b_forget, loss=None):
    d = gains[0].shape[1]
    rows = [meta.reshape(4, d), jnp.pad(conv.reshape(1, 3 * 128), ((0, 0), (0, d - 3 * 128)))]
    rows += list(gains) + [jnp.pad(b_forget, ((0, 0), (0, d - HEADS)))]
    last = jnp.zeros((4, d), F32)
    if loss is not None:
        last = jnp.pad(loss.reshape(1, 1), ((0, 3), (0, d - 1)))
    return jnp.concatenate(rows + [last], axis=0)


def _unpack_small(block):
    d = block.shape[1]
    meta = block[0:4].reshape(N_META, d // 4)
    conv = block[4, :3 * 128].reshape(1, 3, 128)
    gains = [block[5 + i:6 + i] for i in range(6)]
    return meta, conv, gains, block[11:12, :HEADS]


def kernel(x, meta_tokens, w_in, b_forget, conv_w, w_attn_branch, w_conv_branch, w_out, g_ffn1_pre, g_ffn1_post, w_ffn1_in, w_ffn1_out, g_mix_pre, g_mix_post, g_ffn2_pre, g_ffn2_post, w_ffn2_in, w_ffn2_out, loss_target, m_meta_tokens, m_w_in, m_b_forget, m_conv_w, m_w_attn_branch, m_w_conv_branch, m_w_out, m_g_ffn1_pre, m_g_ffn1_post, m_w_ffn1_in, m_w_ffn1_out, m_g_mix_pre, m_g_mix_post, m_g_ffn2_pre, m_g_ffn2_post, m_w_ffn2_in, m_w_ffn2_out, v_meta_tokens, v_w_in, v_b_forget, v_conv_w, v_w_attn_branch, v_w_conv_branch, v_w_out, v_g_ffn1_pre, v_g_ffn1_post, v_w_ffn1_in, v_w_ffn1_out, v_g_mix_pre, v_g_mix_post, v_g_ffn2_pre, v_g_ffn2_post, v_w_ffn2_in, v_w_ffn2_out):
    seq, d = x.shape[1], x.shape[2]
    t = seq + N_FRONT
    f_lo = 3 * ATTN_W
    c_arr = lax.axis_index("c").astype(jnp.int32).reshape(1)

    cs = w_in.shape[2]
    cs_pad = -(-cs // 64) * 64

    def w_in_rows(a):
        return jnp.pad(jnp.transpose(a[0]), ((0, cs_pad - cs), (0, 0)))

    big = [w_in_rows(w_in), w_attn_branch[0], w_conv_branch[0], w_out[0], w_ffn1_in[0], w_ffn1_out[0], w_ffn2_in[0],
           w_ffn2_out[0]]
    small_gather = jnp.concatenate(
        [meta_tokens.reshape(4, d), jnp.pad(conv_w.reshape(1, 3 * 128), ((0, 0), (0, d - 3 * 128))),
         jnp.zeros((11, d), F32)], axis=0)
    w_f1_in4, small4 = _all_gather([big[4].astype(BF16), small_gather])
    (second, rest), small4 = lax.optimization_barrier(
        (([big[5].astype(BF16)], [big[i].astype(BF16) for i in (0, 1, 2, 3, 6, 7)]), small4))
    second_gathered = _all_gather_async("all_gather_ffn1_out", second, 5)
    rest_gathered = _all_gather_async("all_gather_rest", rest, 1)
    meta_full = jnp.transpose(small4[:, 0:4].reshape(4, N_META, d // 4), (1, 0, 2)).reshape(N_META, d)
    conv_full = jnp.transpose(small4[:, 4, :3 * 128].reshape(4, 3, 128), (1, 0, 2)).reshape(3, CONV_W)
    conv_pad = jnp.pad(conv_full, ((0, 5), (0, 0)))
    b_pad = jnp.pad(b_forget, ((0, 0), (0, 128 - HEADS)))

    h0, n1, n1_t = _embed_norm(x[0], meta_full, g_ffn1_pre)
    ab1, s1, s1_t = _ffn_in("ffn1_in_fwd", n1, w_f1_in4)
    w_f1_out = second_gathered(s1, [0])[0].reshape(-1, d)
    f1, h1, u, u_t = _mm_resid_norm("ffn1_out_fwd", s1, w_f1_out, h0, g_ffn1_post, 0.5, g_mix_pre)

    w_in4, w_ab4, w_cb4, w_out4, w_f2_in4, w_f2_out4 = rest_gathered(u, range(6))
    w_in_t = w_in4[:, :cs].reshape(4 * cs, d)
    g_lo = f_lo + HEADS + 3 * CONV_W
    w_in_pad = jnp.concatenate(
        [w_in_t[:f_lo], w_in_t[g_lo:], w_in_t[f_lo + HEADS:g_lo], w_in_t[f_lo:f_lo + HEADS],
         jnp.zeros((F_PAD - HEADS, d), BF16)], axis=0)
    w_ab = jnp.transpose(w_ab4, (1, 0, 2)).reshape(ATTN_W, d)
    w_cb = jnp.transpose(w_cb4, (1, 0, 2)).reshape(CONV_W, d)
    w_out_full = w_out4.reshape(d, d)
    w_f2_out = w_f2_out4.reshape(-1, d)
    qkv, z = _in_proj(u, w_in_pad)
    f_col = z.shape[1] - F_PAD
    f_cum = _gate_prep(z, b_pad, f_col)
    f_heads = f_cum[:, :HEADS]
    o, lse = _attn_fwd(qkv, *_attn_bias_operands(f_heads))
    g, g_t = _conv_gate(z, conv_pad)
    mp, mp_t, o_t = _branch_mix(z, o, g, w_ab, w_cb, d)
    mixed, h2, n2, n2_t = _mm_resid_norm("mix_out_fwd", mp, w_out_full, h1, g_mix_post, 1.0, g_ffn2_pre)
    ab2, s2_t, (f2, h3) = _ffn_fwd("ffn2", n2, w_f2_in4, w_f2_out, h2, g_ffn2_post, None)
    dh3, df2, dg_f2_post, loss_part = _loss_norm_bwd(h3, loss_target[0], f2, g_ffn2_post, 0.5)

    reduced = {}

    late_swaps = []

    def reduce_scatter(label, tags, slots, sequencer_id, hold=None, got=None, after=None, swap_id=None):
        if got is None:
            got = _pair_send_halves(f"grad_pair_exchange_{label}", slots)
        else:
            got, _ = lax.optimization_barrier((got, after))
        sums = [_pair_add(tag, s, a, c_arr, F32 if tag == "small" else BF16) for tag, s, a in zip(tags, slots, got)]
        sums, hold = lax.optimization_barrier((sums, hold))
        if sequencer_id is None:
            arrived = _chip_scatter(f"grad_chip_scatter_{label}", sums)
        else:
            arrived = _chip_scatter_async(f"grad_chip_scatter_{label}", sums, sequencer_id)
        mine = [_chip_add(tag, a) for tag, a in zip(tags, arrived)]
        if swap_id is None:
            reduced.update(zip(tags, zip(mine, _pair_swap(f"grad_pair_swap_{label}", mine))))
        else:
            late_swaps.append((tags, mine, _pair_swap_async(f"grad_pair_swap_{label}", mine, swap_id)))
        return hold

    dab2, dw_f2_in, dw_f2_out = _ffn_bwd_weights("ffn2", df2, ab2, s2_t, n2_t, w_f2_in4, w_f2_out)
    ffn2_slots = [dw_f2_in, dw_f2_out.reshape(4, -1, d)]
    ffn2_got = _pair_send_halves_async("grad_pair_exchange_ffn2", ffn2_slots, 6)
    dh2, dg_f2_pre, dmixed, dg_mix_post = _mm_nt_norm_bwd(
        "ffn2_in_bwd", dab2, w_f2_in4, h2, g_ffn2_pre, dh3, post=(mixed, g_mix_post, 1.0))
    reduce_scatter("ffn2", ["w_ffn2_in", "w_ffn2_out"], ffn2_slots, 2, got=ffn2_got, after=dh2, swap_id=8)
    dw_out = _weight_grad("mix_dw_out", mp_t, dmixed, d)
    dya, dyc, dgates, do, dgconv = _branch_bwd(z, o, g, dmixed, w_out_full, w_ab, w_cb, d)
    dw_ab = _weight_grad("mix_dw_attn_branch", o_t, dya, d)
    dw_cb = _weight_grad("mix_dw_conv_branch", g_t, dyc, d)
    dz_conv, dconv_w = _conv_bwd(z, dgconv, conv_pad)
    front = lax.broadcasted_iota(jnp.int32, (t, 1), 0) < ROW_PAD
    lse_heads = jnp.where(front, 1e9, lse[:, ::HEAD_DIM])
    dq, dk, dv, dfk, dfq = _attn_bwd(qkv, *_attn_bias_operands(f_heads, lse_heads), o, do)
    dz_f, db_forget = _gate_bwd(dfq, dfk, z, b_pad, f_col)
    dz_pieces = {"q": dq, "k": dk, "v": dv, "gates": dgates, "conv": dz_conv, "f": dz_f}
    dh1, dg_mix_pre, df1, dg_f1_post = _mix_in_bwd(
        list(dz_pieces.values()), w_in_pad, h1, g_mix_pre, dh2, (f1, g_ffn1_post, 0.5))
    dw_t = {name: _weight_grad_t(f"mix_dw_in_{name}", u_t, piece) for name, piece in dz_pieces.items()}
    dw_in_t = jnp.concatenate(
        [dw_t["q"], dw_t["k"], dw_t["v"], dw_t["f"][:HEADS], dw_t["conv"], dw_t["gates"]], axis=0)
    mix_slots = [jnp.pad(dw_in_t.reshape(4, cs, d), ((0, 0), (0, cs_pad - cs), (0, 0))),
                 jnp.transpose(dw_ab.reshape(ATTN_W, 4, d // 4), (1, 0, 2)),
                 jnp.transpose(dw_cb.reshape(CONV_W, 4, d // 4), (1, 0, 2)),
                 dw_out.reshape(4, d // 4, d)]
    mix_got = _pair_send_halves_async("grad_pair_exchange_mix", mix_slots, 7)
    dab1, dw_f1_in, dw_f1_out = _ffn_bwd_weights("ffn1", df1, ab1, s1_t, n1_t, w_f1_in4, w_f1_out)
    reduce_scatter("mix", ["w_in", "w_attn_branch", "w_conv_branch", "w_out"], mix_slots, 3, got=mix_got,
                   after=dw_f1_out, swap_id=9)
    dab1 = reduce_scatter("ffn1", ["w_ffn1_in", "w_ffn1_out"], [dw_f1_in, dw_f1_out.reshape(4, -1, d)], 4, dab1)
    dh0, dg_f1_pre = _mm_nt_norm_bwd("ffn1_in_bwd", dab1, w_f1_in4, h0, g_ffn1_pre, dh1)
    for tags_, mine_, theirs_ in late_swaps:
        theirs_, _ = lax.optimization_barrier((theirs_, dh0))
        reduced.update(zip(tags_, zip(mine_, theirs_)))
    grad_x = dh0[N_FRONT:][None]
    dmeta = dh0[ROW_PAD:N_FRONT]
    small_grad = jnp.stack([
        _pack_small(dmeta[:, j * (d // 4):(j + 1) * (d // 4)], dconv_w[:3, j * 128:(j + 1) * 128],
                    [dg_f1_pre, dg_f1_post, dg_mix_pre, dg_mix_post, dg_f2_pre, dg_f2_post], db_forget[:, :HEADS],
                    loss_part[0, 0])
        for j in range(4)])
    reduce_scatter("small", ["small"], [small_grad], None)
    tags =["w_in", "w_attn_branch", "w_conv_branch", "w_out", "w_ffn1_in", "w_ffn1_out", "w_ffn2_in", "w_ffn2_out", "small"]
    halves = [reduced[tag][0] for tag in tags]
    others = [reduced[tag][1] for tag in tags]

    small = [g_ffn1_pre, g_ffn1_post, g_mix_pre, g_mix_post, g_ffn2_pre, g_ffn2_post]
    small_m = [m_g_ffn1_pre, m_g_ffn1_post, m_g_mix_pre, m_g_mix_post, m_g_ffn2_pre, m_g_ffn2_post]
    small_v = [v_g_ffn1_pre, v_g_ffn1_post, v_g_mix_pre, v_g_mix_post, v_g_ffn2_pre, v_g_ffn2_post]
    ws = big + [_pack_small(meta_tokens, conv_w[0], small, b_forget)]
    ms = [w_in_rows(m_w_in), m_w_attn_branch[0], m_w_conv_branch[0], m_w_out[0], m_w_ffn1_in[0], m_w_ffn1_out[0],
          m_w_ffn2_in[0], m_w_ffn2_out[0], _pack_small(m_meta_tokens, m_conv_w[0], small_m, m_b_forget)]
    vs = [w_in_rows(v_w_in), v_w_attn_branch[0], v_w_conv_branch[0], v_w_out[0], v_w_ffn1_in[0], v_w_ffn1_out[0],
          v_w_ffn2_in[0], v_w_ffn2_out[0], _pack_small(v_meta_tokens, v_conv_w[0], small_v, v_b_forget)]
    updates = [_adamw(tag, w, a, b, m, v, c_arr) for tag, w, a, b, m, v in zip(tags, ws, halves, others, ms, vs)]

    def leaves(big_vals, small_block):
        meta, conv, gains, bf = _unpack_small(small_block)
        w_in_t_, w_ab_, w_cb_, w_out_, f1_in, f1_out, f2_in, f2_out = [b[None] for b in big_vals]
        w_in_ = jnp.transpose(w_in_t_[:, :cs], (0, 2, 1))
        return [meta, w_in_, bf, conv, w_ab_, w_cb_, w_out_, gains[0], gains[1], f1_in, f1_out,
                gains[2], gains[3], gains[4], gains[5], f2_in, f2_out]

    out_g, out_d, out_m, out_v = [leaves([u_[k] for u_ in updates[:8]], updates[8][k]) for k in range(4)]
    loss = updates[8][0][LOSS_ROW, 0]
    return (loss, grad_x, *out_g, *out_d, *out_m, *out_v)
```

```python
import functools

import jax
import jax.numpy as jnp
from jax import lax
from jax.experimental import pallas as pl
from jax.experimental.pallas import tpu as pltpu
from jax.experimental.pallas import tpu_sc as plsc

N_META = 16
ROW_PAD = 112
N_FRONT = ROW_PAD + N_META
HEADS = 8
HEAD_DIM = 64
ATTN_W = HEADS * HEAD_DIM
CONV_W = 512
NORM_EPS = 1e-6
ROW_TILE = 640
F_PAD = 128
ATTN_Q_GROUP = 2
ATTN_KV_GROUP = 4
NEG = -1e30
ADAM_LR = 0.001
ADAM_B1 = 0.9
ADAM_B2 = 0.999
ADAM_EPS = 1e-08
ADAM_WD = 0.01
ADAM_STEP = 10
VMEM_BIG = 56 * 1024 * 1024
MESH = pl.DeviceIdType.MESH
ANY = pl.BlockSpec(memory_space=pl.ANY)
F32 = jnp.float32
BF16 = jnp.bfloat16


def _params(sem, vmem=None):
    return pltpu.CompilerParams(dimension_semantics=sem, vmem_limit_bytes=vmem)


def _sigmoid(x):
    return 1.0 / (1.0 + jnp.exp(-x))


def _rstd(x):
    return lax.rsqrt(jnp.mean(x * x, axis=-1, keepdims=True) + NORM_EPS)


def _rms_bwd(x, g, dy):
    r = _rstd(x)
    xr = x * r
    gdy = g * dy
    dx = r * (gdy - xr * jnp.mean(xr * gdy, axis=-1, keepdims=True))
    return dx, jnp.sum(dy * xr, axis=0, keepdims=True)


def _dot(a, b):
    return jnp.dot(a, b, preferred_element_type=F32)


def _dot_nt(a, b):
    return lax.dot_general(a, b, (((1,), (1,)), ((), ())), preferred_element_type=F32)


def _k_tile(t):
    return 1664 if t % 1664 == 0 else ROW_TILE


def _place():
    x, y, c = lax.axis_index("x"), lax.axis_index("y"), lax.axis_index("c")
    chips = [(1 - x, y), (x, 1 - y), (1 - x, 1 - y)]
    return x, y, c, chips


def _all_gather(shards):
    n = len(shards)
    split = [s.reshape(2, s.shape[0] // 2, s.shape[1]) for s in shards]

    def body(*refs):
        ins, outs = refs[:n], refs[n:2 * n]
        send_sems, recv_sems = refs[2 * n:]
        x, y, c, chips = _place()
        me = 2 * x + y
        sibling = (x, y, 1 - c)

        def remote(i, k, slot, part, to, src=None):
            dst = outs[i].at[slot, part]
            return pltpu.make_async_remote_copy(
                src_ref=dst if src is None else src, dst_ref=dst,
                send_sem=send_sems.at[i, k], recv_sem=recv_sems.at[i, k],
                device_id=to, device_id_type=MESH)

        started = []
        for i in range(n):
            for k, (cx, cy) in enumerate(chips):
                cp = remote(i, k, me, c, (cx, cy, c), src=ins[i].at[c])
                cp.start()
                started.append(cp)
        for i in range(n):
            for k, (cx, cy) in enumerate(chips):
                remote(i, k, 2 * cx + cy, c, (x, y, c)).wait_recv()
                cp = remote(i, 3 + k, 2 * cx + cy, c, sibling)
                cp.start()
                started.append(cp)
        for i in range(n):
            for k, (cx, cy) in enumerate(chips):
                remote(i, 3 + k, 2 * cx + cy, 1 - c, (x, y, c)).wait_recv()
        for cp in started:
            cp.wait_send()

    outs = pl.pallas_call(
        body, name="all_gather_weights",
        out_shape=[jax.ShapeDtypeStruct((4,) + s.shape, s.dtype) for s in split],
        in_specs=[ANY] * n, out_specs=[ANY] * n,
        scratch_shapes=[pltpu.SemaphoreType.DMA((n, 6)), pltpu.SemaphoreType.DMA((n, 6))],
    )(*split)
    me =2 * lax.axis_index("x") + lax.axis_index("y")
    outs = [lax.dynamic_update_slice(o, s[None], (me, 0, 0, 0)) for o, s in zip(outs, split)]
    return [o.reshape((4,) + s.shape) for o, s in zip(outs, shards)]


def _all_gather_async(name, shards, collective_id):
    n = len(shards)
    split = [s.reshape(2, s.shape[0] // 2, s.shape[1]) for s in shards]
    ins = [jax.new_ref(s, memory_space=pltpu.MemorySpace.HBM) for s in split]
    outs = [jax.empty_ref(jax.ShapeDtypeStruct((4,) + s.shape, s.dtype), memory_space=pltpu.MemorySpace.HBM)
            for s in split]

    @pl.kernel(mesh=plsc.ScalarSubcoreMesh(axis_name="sequencer", num_cores=1), name=name,
               scratch_types=(pltpu.SemaphoreType.DMA((n, 6)), pltpu.SemaphoreType.DMA((n, 6))),
               compiler_params=pltpu.CompilerParams(collective_id=collective_id))
    def launch(send_sems, recv_sems):
        x, y, c, chips = _place()
        me = 2 * x + y
        sibling = (x, y, 1 - c)
        barrier = pltpu.get_barrier_semaphore()
        for peer in [(cx, cy, c) for cx, cy in chips] + [sibling]:
            pl.semaphore_signal(barrier, inc=1, device_id=peer, device_id_type=MESH)
        pl.semaphore_wait(barrier, 4)

        def remote(i, k, slot, part, to, src=None):
            dst = outs[i].at[slot, part]
            return pltpu.make_async_remote_copy(
                src_ref=dst if src is None else src, dst_ref=dst,
                send_sem=send_sems.at[i, k], recv_sem=recv_sems.at[i, k],
                device_id=to, device_id_type=MESH)

        started = []
        for i in range(n):
            for k, (cx, cy) in enumerate(chips):
                cp = remote(i, k, me, c, (cx, cy, c), src=ins[i].at[c])
                cp.start()
                started.append(cp)
        for i in range(n):
            for k, (cx, cy) in enumerate(chips):
                remote(i, k, 2 * cx + cy, c, (x, y, c)).wait_recv()
                cp = remote(i, 3 + k, 2 * cx + cy, c, sibling)
                cp.start()
                started.append(cp)
        for i in range(n):
            for k, (cx, cy) in enumerate(chips):
                remote(i, 3 + k, 2 * cx + cy, 1 - c, (x, y, c)).wait_recv()
        for cp in started:
            cp.wait_send()

    launch()
    raw = [o[...] for o in outs]

    def finish(after, which):
        arrived, _ = lax.optimization_barrier(([raw[i] for i in which], after))
        me = 2 * lax.axis_index("x") + lax.axis_index("y")
        gathered = [lax.dynamic_update_slice(a, split[i][None], (me, 0, 0, 0)) for a, i in zip(arrived, which)]
        return [g.reshape((4,) + shards[i].shape) for g, i in zip(gathered, which)]

    return finish


def _pair_send_halves(name, grads):
    n = len(grads)

    def body(*refs):
        ins, outs = refs[:n], refs[n:2 * n]
        send_sems, recv_sems = refs[2 * n:]
        x, y, c, _ = _place()
        cps = []
        for i in range(n):
            half = ins[i].shape[1] // 2
            cp = pltpu.make_async_remote_copy(
                src_ref=ins[i].at[:, pl.ds((1 - c) * half, half)], dst_ref=outs[i],
                send_sem=send_sems.at[i], recv_sem=recv_sems.at[i],
                device_id=(x, y, 1 - c), device_id_type=MESH)
            cp.start()
            cps.append(cp)
        for cp in cps:
            cp.wait()

    return pl.pallas_call(
        body, name=name,
        out_shape=[jax.ShapeDtypeStruct((4, g.shape[1] // 2, g.shape[2]), g.dtype) for g in grads],
        in_specs=[ANY] * n, out_specs=[ANY] * n,
        scratch_shapes=[pltpu.SemaphoreType.DMA((n,)), pltpu.SemaphoreType.DMA((n,))],
    )(*grads)


def _pair_send_halves_async(name, grads, collective_id):
    n = len(grads)
    ins = [jax.new_ref(g, memory_space=pltpu.MemorySpace.HBM) for g in grads]
    outs = [jax.empty_ref(jax.ShapeDtypeStruct((4, g.shape[1] // 2, g.shape[2]), g.dtype),
                          memory_space=pltpu.MemorySpace.HBM) for g in grads]

    @pl.kernel(mesh=plsc.ScalarSubcoreMesh(axis_name="sequencer", num_cores=1), name=name,
               scratch_types=(pltpu.SemaphoreType.DMA((n,)), pltpu.SemaphoreType.DMA((n,))),
               compiler_params=pltpu.CompilerParams(collective_id=collective_id))
    def launch(send_sems, recv_sems):
        x, y, c, _ = _place()
        barrier = pltpu.get_barrier_semaphore()
        pl.semaphore_signal(barrier, inc=1, device_id=(x, y, 1 - c), device_id_type=MESH)
        pl.semaphore_wait(barrier, 1)
        cps = []
        for i in range(n):
            half = ins[i].shape[1] // 2
            cp = pltpu.make_async_remote_copy(
                src_ref=ins[i].at[:, pl.ds((1 - c) * half, half)], dst_ref=outs[i],
                send_sem=send_sems.at[i], recv_sem=recv_sems.at[i],
                device_id=(x, y, 1 - c), device_id_type=MESH)
            cp.start()
            cps.append(cp)
        for cp in cps:
            cp.wait()

    launch()
    return [o[...] for o in outs]


def _chip_scatter(name, parts):
    n = len(parts)

    def body(*refs):
        _scatter_copies(refs[:n], refs[n:2 * n], *refs[2 * n:])

    arrived = pl.pallas_call(
        body, name=name,
        out_shape=[jax.ShapeDtypeStruct(p.shape, p.dtype) for p in parts],
        in_specs=[ANY] * n, out_specs=[ANY] * n,
        scratch_shapes=[pltpu.SemaphoreType.DMA((n, 3)), pltpu.SemaphoreType.DMA((n, 3))],
    )(*parts)
    return _own_slots(parts, arrived)


def _scatter_copies(ins, outs, send_sems, recv_sems):
    x, y, c, chips = _place()
    me = 2 * x + y
    sends = []
    for i in range(len(ins)):
        for k, (cx, cy) in enumerate(chips):
            cp = pltpu.make_async_remote_copy(
                src_ref=ins[i].at[2 * cx + cy], dst_ref=outs[i].at[me],
                send_sem=send_sems.at[i, k], recv_sem=recv_sems.at[i, k],
                device_id=(cx, cy, c), device_id_type=MESH)
            cp.start()
            sends.append(cp)
    for i in range(len(ins)):
        for k, (cx, cy) in enumerate(chips):
            got = outs[i].at[2 * cx + cy]
            pltpu.make_async_remote_copy(
                src_ref=got, dst_ref=got, send_sem=send_sems.at[i, k], recv_sem=recv_sems.at[i, k],
                device_id=(x, y, c), device_id_type=MESH).wait_recv()
    for cp in sends:
        cp.wait_send()


def _own_slots(parts, arrived):
    me = 2 * lax.axis_index("x") + lax.axis_index("y")
    return [lax.dynamic_update_slice(a, lax.dynamic_slice_in_dim(p, me, 1, axis=0), (me, 0, 0))
            for p, a in zip(parts, arrived)]


def _chip_scatter_async(name, parts, collective_id):
    n = len(parts)
    ins = [jax.new_ref(p, memory_space=pltpu.MemorySpace.HBM) for p in parts]
    outs = [jax.empty_ref(jax.ShapeDtypeStruct(p.shape, p.dtype), memory_space=pltpu.MemorySpace.HBM) for p in parts]

    @pl.kernel(mesh=plsc.ScalarSubcoreMesh(axis_name="sequencer", num_cores=1), name=name,
               scratch_types=(pltpu.SemaphoreType.DMA((n, 3)), pltpu.SemaphoreType.DMA((n, 3))),
               compiler_params=pltpu.CompilerParams(collective_id=collective_id))
    def launch(send_sems, recv_sems):
        x, y, c, chips = _place()
        barrier = pltpu.get_barrier_semaphore()
        for cx, cy in chips:
            pl.semaphore_signal(barrier, inc=1, device_id=(cx, cy, c), device_id_type=MESH)
        pl.semaphore_wait(barrier, 3)
        _scatter_copies(ins, outs, send_sems, recv_sems)

    launch()
    return _own_slots(parts, [o[...] for o in outs])


def _pair_swap(name, halves):
    n = len(halves)

    def body(*refs):
        ins, outs = refs[:n], refs[n:2 * n]
        send_sems, recv_sems = refs[2 * n:]
        x, y, c, _ = _place()
        cps = []
        for i in range(n):
            cp = pltpu.make_async_remote_copy(
                src_ref=ins[i], dst_ref=outs[i], send_sem=send_sems.at[i], recv_sem=recv_sems.at[i],
                device_id=(x, y, 1 - c), device_id_type=MESH)
            cp.start()
            cps.append(cp)
        for cp in cps:
            cp.wait()

    return pl.pallas_call(
        body, name=name,
        out_shape=[jax.ShapeDtypeStruct(h.shape, h.dtype) for h in halves],
        in_specs=[ANY] * n, out_specs=[ANY] * n,
        scratch_shapes=[pltpu.SemaphoreType.DMA((n,)), pltpu.SemaphoreType.DMA((n,))],
    )(*halves)


def _row_block(rows, cols, n_bufs, budget=20 * 1024 * 1024):
    best = min(rows, 16)
    for b in range(16, rows + 1, 16):
        if rows % b == 0 and 2 * n_bufs * b * cols * 4 <= budget:
            best = b
    return best


def _pair_add(tag, grad, got, c_arr, out_dtype):
    _, rows, cols = grad.shape
    half = rows // 2
    bh = _row_block(half, cols, 3)
    nb = half // bh

    def body(c_ref, g_ref, a_ref, o_ref):
        o_ref[...] = (g_ref[...] + a_ref[...]).astype(out_dtype)

    return pl.pallas_call(
        body, name=f"pair_add_{tag}",
        out_shape=jax.ShapeDtypeStruct((4, half, cols), out_dtype),
        grid_spec=pltpu.PrefetchScalarGridSpec(
            num_scalar_prefetch=1, grid=(4, nb),
            in_specs=[pl.BlockSpec((None, bh, cols), lambda j, r, c: (j, c[0] * nb + r, 0)),
                      pl.BlockSpec((None, bh, cols), lambda j, r, c: (j, r, 0))],
            out_specs=pl.BlockSpec((None, bh, cols), lambda j, r, c: (j, r, 0))),
        compiler_params=_params(("parallel", "parallel")),
    )(c_arr, grad, got)


def _chip_add(tag, parts):
    _, half, cols = parts.shape
    bh = _row_block(half, cols, 5)

    def body(p_ref, o_ref):
        a, b, c, d = [p_ref[j].astype(F32) for j in range(4)]
        o_ref[...] = ((a + b) + c) + d

    return pl.pallas_call(
        body, name=f"chip_add_{tag}",
        out_shape=jax.ShapeDtypeStruct((half, cols), F32),
        grid=(half // bh,),
        in_specs=[pl.BlockSpec((4, bh, cols), lambda r: (0, r, 0))],
        out_specs=pl.BlockSpec((bh, cols), lambda r: (r, 0)),
        compiler_params=_params(("parallel",)),
    )(parts)


def _adamw(tag, w, mine, theirs, m, v, c_arr):
    rows, cols = w.shape
    half = rows // 2
    br = _row_block(half, cols, 9)
    nb = half // br

    def body(c_ref, w_ref, a_ref, b_ref, m_ref, v_ref, g_ref, d_ref, mo_ref, vo_ref):
        own = (pl.program_id(0) // nb) == c_ref[0]
        g = jnp.where(own, a_ref[...], b_ref[...])
        g_ref[...] = g
        m_new = ADAM_B1 * m_ref[...] + (1.0 - ADAM_B1) * g
        v_new = ADAM_B2 * v_ref[...] + (1.0 - ADAM_B2) * (g * g)
        m_hat = m_new / (1.0 - ADAM_B1 ** ADAM_STEP)
        v_hat = v_new / (1.0 - ADAM_B2 ** ADAM_STEP)
        d_ref[...] = -ADAM_LR * (m_hat / (jnp.sqrt(v_hat) + ADAM_EPS) + ADAM_WD * w_ref[...])
        mo_ref[...] = m_new
        vo_ref[...] = v_new

    spec = pl.BlockSpec((br, cols), lambda r, c: (r, 0))
    mine_spec = pl.BlockSpec((br, cols), lambda r, c: (jnp.clip(r - c[0] * nb, 0, nb - 1), 0))
    theirs_spec = pl.BlockSpec((br, cols), lambda r, c: (jnp.clip(r - (1 - c[0]) * nb, 0, nb - 1), 0))
    return pl.pallas_call(
        body, name=f"adamw_{tag}",
        out_shape=[jax.ShapeDtypeStruct((rows, cols), F32)] * 4,
        grid_spec=pltpu.PrefetchScalarGridSpec(
            num_scalar_prefetch=1, grid=(rows // br,),
            in_specs=[spec, mine_spec, theirs_spec, spec, spec], out_specs=[spec] * 4),
        compiler_params=_params(("arbitrary",)),
    )(c_arr, w, mine, theirs, m, v)


def _matmul(name, x, w, out_shape, grid, x_spec, w_spec, o_spec, *, nt=False, vmem=None):
    nk = grid[2]
    acc_shape = tuple(d for d in o_spec.block_shape if d is not None)

    def body(x_ref, w_ref, o_ref, acc_ref):
        k = pl.program_id(2)
        part = _dot_nt(x_ref[...], w_ref[...]) if nt else _dot(x_ref[...], w_ref[...])
        if nk == 1:
            o_ref[...] = part.astype(o_ref.dtype)
        else:
            @pl.when(k == 0)
            def _():
                acc_ref[...] = part

            @pl.when(k > 0)
            def _():
                acc_ref[...] += part

            @pl.when(k == nk - 1)
            def _():
                o_ref[...] = acc_ref[...].astype(o_ref.dtype)

    return pl.pallas_call(
        body, name=name, out_shape=out_shape, grid=grid,
        in_specs=[x_spec, w_spec], out_specs=o_spec,
        scratch_shapes=[pltpu.VMEM(acc_shape if nk > 1 else (8, 128), F32)],
        compiler_params=_params(("parallel", "parallel", "arbitrary"), vmem),
    )(x, w)


def _weight_grad(name, xt, dy, bn, out_rows=None):
    m, t = xt.shape
    n = dy.shape[1]
    bm = m if out_rows is None else out_rows
    bk = _k_tile(t)
    return _matmul(
        name, xt, dy, jax.ShapeDtypeStruct((m, n), F32), (m // bm, n // bn, t // bk),
        pl.BlockSpec((bm, bk), lambda a, b, k: (a, k)),
        pl.BlockSpec((bk, bn), lambda a, b, k: (k, b)),
        pl.BlockSpec((bm, bn), lambda a, b, k: (a, b)), vmem=VMEM_BIG)


def _weight_grad_t(name, xt, dy):
    m, t = xt.shape
    n = dy.shape[1]
    bn = min(n, 512)
    bk = _k_tile(t)
    nk = t // bk

    def body(x_ref, dy_ref, o_ref, acc_ref):
        k = pl.program_id(1)
        part = _dot(x_ref[...], dy_ref[...].astype(BF16))

        @pl.when(k == 0)
        def _():
            acc_ref[...] = part

        @pl.when(k > 0)
        def _():
            acc_ref[...] += part

        @pl.when(k == nk - 1)
        def _():
            o_ref[...] = acc_ref[...].T

    return pl.pallas_call(
        body, name=name, out_shape=jax.ShapeDtypeStruct((n, m), F32), grid=(n // bn, nk),
        in_specs=[pl.BlockSpec((m, bk), lambda b, k: (0, k)), pl.BlockSpec((bk, bn), lambda b, k: (k, b))],
        out_specs=pl.BlockSpec((bn, m), lambda b, k: (b, 0)),
        scratch_shapes=[pltpu.VMEM((m, bn), F32)],
        compiler_params=_params(("parallel", "arbitrary"), VMEM_BIG),
    )(xt, dy)


def _mix_in_bwd(pieces, wt, h, g, dh_in, post):
    t, d = h.shape
    tm = ROW_TILE // 2
    widths = [p.shape[1] for p in pieces]
    n = len(pieces)

    def body(*refs):
        dy_refs = refs[:n]
        w_ref, h_ref, g_ref, dhi_ref, xp_ref, gp_ref, dh_ref, dg_ref, dxp_ref, dgp_ref = refs[n:]
        first = pl.program_id(0) == 0

        @pl.when(first)
        def _():
            dg_ref[...] = jnp.zeros_like(dg_ref)

        dn, off = None, 0
        for dy_ref, wd in zip(dy_refs, widths):
            part = _dot(dy_ref[...].astype(BF16), w_ref[off:off + wd, :])
            dn = part if dn is None else dn + part
            off += wd
        dx, dg = _rms_bwd(h_ref[...], g_ref[...], dn)
        dh = dhi_ref[...] + dx
        dh_ref[...] = dh
        dg_ref[...] += dg
        _next_post_norm_bwd(dh, (xp_ref, gp_ref, dxp_ref, dgp_ref), post[2], first)

    row = pl.BlockSpec((tm, d), lambda i: (i, 0))
    vec = pl.BlockSpec((1, d), lambda i: (0, 0))
    return pl.pallas_call(
        body, name="mix_in_bwd",
        out_shape=[jax.ShapeDtypeStruct((t, d), F32), jax.ShapeDtypeStruct((1, d), F32),
                   jax.ShapeDtypeStruct((t, d), BF16), jax.ShapeDtypeStruct((1, d), F32)],
        grid=(t // tm,),
        in_specs=[pl.BlockSpec((tm, wd), lambda i: (i, 0)) for wd in widths]
        + [pl.BlockSpec(wt.shape, lambda i: (0, 0)), row, vec, row, row, vec],
        out_specs=[row, vec, row, vec],
        compiler_params=_params(("arbitrary",), VMEM_BIG),
    )(*pieces, wt, h, g, dh_in, post[0], post[1])


def _read_token_rows(src_hbm, buf, sems, i, n):
    tm = buf.shape[1]

    def first_tile():
        return pltpu.make_async_copy(src_hbm.at[pl.ds(0, tm - N_FRONT)], buf.at[0, pl.ds(N_FRONT, tm - N_FRONT)],
                                     sems.at[0])

    def tile(j):
        return pltpu.make_async_copy(src_hbm.at[pl.ds(pl.multiple_of(j * tm - N_FRONT, N_FRONT), tm)],
                                     buf.at[j % 2], sems.at[j % 2])

    @pl.when(i == 0)
    def _():
        buf[0, 0:N_FRONT, :] = jnp.zeros((N_FRONT, buf.shape[2]), buf.dtype)
        first_tile().start()

    @pl.when(i + 1 < n)
    def _():
        tile(i + 1).start()

    @pl.when(i == 0)
    def _():
        first_tile().wait()

    @pl.when(i > 0)
    def _():
        tile(i).wait()

    return buf.at[i % 2]


def _embed_norm(x, meta, g):
    seq, d = x.shape
    t = seq + N_FRONT
    tm = ROW_TILE

    def body(x_hbm, meta_ref, g_ref, h_ref, n_ref, nt_ref, buf, sems):
        i = pl.program_id(0)
        rows = _read_token_rows(x_hbm, buf, sems, i, t // tm)

        @pl.when(i == 0)
        def _():
            buf[0, ROW_PAD:N_FRONT, :] = meta_ref[...]

        h = rows[...]
        h_ref[...] = h
        y = h * _rstd(h) * g_ref[...]
        n_ref[...] = y.astype(BF16)
        nt_ref[...] = y.T.astype(BF16)

    row = pl.BlockSpec((tm, d), lambda i: (i, 0))
    return pl.pallas_call(
        body, name="embed_and_ffn1_pre_norm",
        out_shape=[jax.ShapeDtypeStruct((t, d), F32), jax.ShapeDtypeStruct((t, d), BF16),
                   jax.ShapeDtypeStruct((d, t), BF16)],
        grid=(t // tm,),
        in_specs=[ANY, pl.BlockSpec((N_META, d), lambda i: (0, 0)), pl.BlockSpec((1, d), lambda i: (0, 0))],
        out_specs=[row, row, pl.BlockSpec((d, tm), lambda i: (0, i))],
        scratch_shapes=[pltpu.VMEM((2, tm, d), F32), pltpu.SemaphoreType.DMA((2,))],
        compiler_params=_params(("arbitrary",)),
    )(x, meta, g)


def _slot_of(kk):
    return (kk % 2) * 2 + kk // 2


def _ffn_in(name, n, w4):
    t, d = n.shape
    cw = w4.shape[2]
    tm = ROW_TILE

    def body(x_ref, wg_ref, wu_ref, ab_ref, s_ref, st_ref):
        x = x_ref[...]
        a = _dot(x, wg_ref[...])
        b = _dot(x, wu_ref[...])
        ab_ref[:, :cw] = a.astype(BF16)
        ab_ref[:, cw:] = b.astype(BF16)
        s = a * _sigmoid(a) * b
        s_ref[...] = s.astype(BF16)
        st_ref[...] = s.T.astype(BF16)

    return pl.pallas_call(
        body, name=name,
        out_shape=[jax.ShapeDtypeStruct((t, 4 * cw), BF16), jax.ShapeDtypeStruct((t, 2 * cw), BF16),
                   jax.ShapeDtypeStruct((2 * cw, t), BF16)],
        grid=(2, t // tm),
        in_specs=[pl.BlockSpec((tm, d), lambda j, i: (i, 0)),
                  pl.BlockSpec((None, d, cw), lambda j, i: (j, 0, 0)),
                  pl.BlockSpec((None, d, cw), lambda j, i: (2 + j, 0, 0))],
        out_specs=[pl.BlockSpec((tm, 2 * cw), lambda j, i: (i, j)),
                   pl.BlockSpec((tm, cw), lambda j, i: (i, j)),
                   pl.BlockSpec((cw, tm), lambda j, i: (j, i))],
        compiler_params=_params(("parallel", "parallel"), VMEM_BIG),
    )(n, w4, w4)


def _mm_resid_norm(name, x, w, h, g_post, alpha, g_next):
    t, kdim = x.shape
    d = w.shape[1]
    tm = ROW_TILE
    with_next = g_next is not None

    def body(x_ref, w_ref, h_ref, gp_ref, gn_ref, f_ref, hn_ref, *rest):
        f = _dot(x_ref[...], w_ref[...])
        f_ref[...] = f
        hn = h_ref[...] + alpha * (f * _rstd(f) * gp_ref[...])
        hn_ref[...] = hn
        if with_next:
            y = hn * _rstd(hn) * gn_ref[...]
            rest[0][...] = y.astype(BF16)
            rest[1][...] = y.T.astype(BF16)

    row = lambda i: (i, 0)
    vec = pl.BlockSpec((1, d), lambda i: (0, 0))
    out_shape = [jax.ShapeDtypeStruct((t, d), F32), jax.ShapeDtypeStruct((t, d), F32)]
    out_specs = [pl.BlockSpec((tm, d), row), pl.BlockSpec((tm, d), row)]
    if with_next:
        out_shape += [jax.ShapeDtypeStruct((t, d), BF16), jax.ShapeDtypeStruct((d, t), BF16)]
        out_specs += [pl.BlockSpec((tm, d), row), pl.BlockSpec((d, tm), lambda i: (0, i))]
    return pl.pallas_call(
        body, name=name, out_shape=out_shape, grid=(t // tm,),
        in_specs=[pl.BlockSpec((tm, kdim), row), pl.BlockSpec((kdim, d), lambda i: (0, 0)),
                  pl.BlockSpec((tm, d), row), vec, vec],
        out_specs=out_specs,
        compiler_params=_params(("parallel",), VMEM_BIG),
    )(x, w, h, g_post, g_post if g_next is None else g_next)


def _in_proj(u, w):
    t, d = u.shape
    nz = w.shape[0]
    nq = 3 * ATTN_W
    tm = ROW_TILE

    def body(u_ref, w_ref, qkv_ref, z_ref):
        qkv_ref[...] = _dot_nt(u_ref[...], w_ref[0:nq, :]).astype(BF16)
        z_ref[...] = _dot_nt(u_ref[...], w_ref[nq:, :])

    return pl.pallas_call(
        body, name="mix_in_proj",
        out_shape=[jax.ShapeDtypeStruct((t, nq), BF16), jax.ShapeDtypeStruct((t, nz - nq), F32)],
        grid=(t // tm,),
        in_specs=[pl.BlockSpec((tm, d), lambda i: (i, 0)),
                  pl.BlockSpec((nz, d), lambda i: (0, 0), pipeline_mode=pl.Buffered(1))],
        out_specs=[pl.BlockSpec((tm, nq), lambda i: (i, 0)), pl.BlockSpec((tm, nz - nq), lambda i: (i, 0))],
        compiler_params=_params(("parallel",), VMEM_BIG),
    )(u, w)


def _gate_prep(z, b_pad, f_col):
    t = z.shape[0]
    tm = ROW_TILE

    def body(z_ref, b_ref, f_ref, carry_ref):
        i = pl.program_id(0)

        @pl.when(i == 0)
        def _():
            carry_ref[...] = jnp.zeros_like(carry_ref)

        xs = z_ref[...] + b_ref[...]
        logf = jnp.minimum(xs, 0.0) - jnp.log(1.0 + jnp.exp(-jnp.abs(xs)))
        row = i * tm + lax.broadcasted_iota(jnp.int32, (tm, 1), 0)
        logf = jnp.where(row >= ROW_PAD, logf, 0.0)
        tri = (lax.broadcasted_iota(jnp.int32, (tm, tm), 0) >= lax.broadcasted_iota(jnp.int32, (tm, tm), 1))
        f = jnp.dot(tri.astype(F32), logf, preferred_element_type=F32, precision=lax.Precision.HIGHEST)
        f = f + carry_ref[0:1, :]
        f_ref[...] = f
        carry_ref[...] = jnp.broadcast_to(f[tm - 1:tm, :], carry_ref.shape)

    return pl.pallas_call(
        body, name="forget_gate_cumsum", out_shape=jax.ShapeDtypeStruct((t, 128), F32),
        grid=(t // tm,),
        in_specs=[pl.BlockSpec((tm, 128), lambda i: (i, f_col // 128)), pl.BlockSpec((1, 128), lambda i: (0, 0))],
        out_specs=pl.BlockSpec((tm, 128), lambda i: (i, 0)),
        scratch_shapes=[pltpu.VMEM((8, 128), F32)],
        compiler_params=_params(("arbitrary",)),
    )(z, b_pad)


def _lane_halves():
    lane = lax.broadcasted_iota(jnp.int32, (1, 128), 1)
    return lane < HEAD_DIM


def _causal_mask(tq, tk, row0=0):
    row = row0 + lax.broadcasted_iota(jnp.int32, (tq, 1), 0)
    col = lax.broadcasted_iota(jnp.int32, (1, tk), 1)
    return col <= row


def _lane_one(lane):
    return (lax.broadcasted_iota(jnp.int32, (1, 128), 1) == lane).astype(BF16)


def _split3(x):
    hi = x.astype(BF16)
    rest = x - hi.astype(F32)
    mid = rest.astype(BF16)
    return hi, mid, (rest - mid.astype(F32)).astype(BF16)


def _split3_glue(x):
    hi = lax.reduce_precision(x, 8, 7)
    mid = lax.reduce_precision(x - hi, 8, 7)
    lo = lax.reduce_precision((x - hi) - mid, 8, 7)
    return hi.astype(BF16), mid.astype(BF16), lo.astype(BF16)


def _aug_pairs(cols):
    t = cols[0].shape[0]
    a = jnp.pad(jnp.stack(cols, axis=2), ((0, 0), (0, 0), (0, HEAD_DIM - len(cols))))
    a = a.reshape(t, 4, 2, HEAD_DIM)[:, :, ::-1, :]
    return jnp.transpose(a.reshape(t, 4, 128), (1, 0, 2))


def _attn_bias_operands(f_heads, lse_heads=None):
    t = f_heads.shape[0]
    one = jnp.ones((t, HEADS), BF16)
    row = lax.broadcasted_iota(jnp.int32, (t, 1), 0)
    fq = _split3_glue(f_heads)
    fk = _split3_glue(jnp.where(row < ROW_PAD, 1e9, f_heads))
    q_cols = list(fq) + [one] * 3
    k_cols = [one] * 3 + [-c for c in fk]
    if lse_heads is not None:
        q_cols += [-c for c in _split3_glue(lse_heads)]
        k_cols += [one] * 3
    return _aug_pairs(q_cols), _aug_pairs(k_cols)


def _attn_fwd(z, aug_q, aug_k):
    t = z.shape[0]
    tq = tk = ROW_TILE
    nq = t // tq
    grp = ATTN_KV_GROUP
    steps = [(qi, ka) for qi in range(nq) for ka in range(0, qi + 1, grp)]
    q_tab = jnp.array([qi for qi, _ in steps], jnp.int32)
    k_tab = jnp.array([ka for _, ka in steps], jnp.int32)

    def body(qt_ref, kt_ref, q_ref, *refs):
        k_refs, v_refs, aq_ref, ak_refs = refs[:grp], refs[grp:2 * grp], refs[2 * grp], refs[2 * grp + 1:3 * grp + 1]
        o_ref, lse_ref, m_ref, l_ref, acc_ref = refs[3 * grp + 1:]
        step = pl.program_id(1)
        qi, ka = qt_ref[step], kt_ref[step]

        @pl.when(ka == 0)
        def _():
            m_ref[...] = jnp.full_like(m_ref, NEG)
            l_ref[...] = jnp.zeros_like(l_ref)
            acc_ref[...] = jnp.zeros_like(acc_ref)

        def sweep(diagonal):
            first = _lane_halves()
            halves = (first, jnp.logical_not(first))
            q = (q_ref[...] * (HEAD_DIM ** -0.5)).astype(BF16)
            aq = aq_ref[...]
            qa = [jnp.where(lanes, q, aq) for lanes in halves]
            blocks = list(zip(k_refs, v_refs, ak_refs, diagonal))
            s = []
            for k_ref, _, ak_ref, diag in blocks:
                k, ak = k_ref[...].astype(BF16), ak_ref[...]
                for hh, lanes in enumerate(halves):
                    s_c = _dot_nt(qa[hh], jnp.where(lanes, k, ak))
                    s.append(jnp.where(_causal_mask(tq, tk), s_c, NEG) if diag else s_c)
            nb = len(blocks)
            m_prev = [m_ref[:, c0:c0 + 1] for c0 in (0, HEAD_DIM)]
            m_new = []
            for hh in range(2):
                m_h = m_prev[hh]
                for b in range(nb):
                    m_h = jnp.maximum(m_h, jnp.max(s[2 * b + hh], axis=1, keepdims=True))
                m_new.append(m_h)
            pv = [None, None]
            for b, (_, v_ref, _, _) in enumerate(blocks):
                v = v_ref[...].astype(BF16)
                for hh, (lanes, a0) in enumerate(zip(halves, (HEAD_DIM, 0))):
                    part = _dot(jnp.exp(s[2 * b + hh] - m_new[hh]).astype(BF16), jnp.where(lanes, v, _lane_one(a0)))
                    pv[hh] = part if pv[hh] is None else pv[hh] + part
            al0, al1 = [jnp.exp(mp - m_h) for mp, m_h in zip(m_prev, m_new)]
            l0 = al0 * l_ref[:, 0:1] + pv[0][:, HEAD_DIM:HEAD_DIM + 1]
            l1 = al1 * l_ref[:, HEAD_DIM:HEAD_DIM + 1] + pv[1][:, 0:1]
            acc_ref[...] = acc_ref[...] * jnp.where(first, al0, al1) + jnp.where(first, pv[0], pv[1])
            m_ref[...] = jnp.where(first, m_new[0], m_new[1])
            l_ref[...] = jnp.where(first, l0, l1)

        def finish():
            o_ref[...] = acc_ref[...] / l_ref[...]
            lse_ref[...] = m_ref[...] + jnp.log(l_ref[...])

        @pl.when(ka + grp - 1 < qi)
        def _():
            sweep((False,) * grp)

        for nb in range(1, grp + 1):
            @pl.when(ka + nb - 1 == qi)
            def _(nb=nb):
                sweep((False,) * (nb - 1) + (True,))
                finish()

    def kblock(j):
        return lambda s, qt, kt: jnp.minimum(kt[s] + j, qt[s])

    kbs = [kblock(j) for j in range(grp)]
    return pl.pallas_call(
        body, name="attention_fwd",
        out_shape=[jax.ShapeDtypeStruct((t, ATTN_W), F32), jax.ShapeDtypeStruct((t, ATTN_W), F32)],
        grid_spec=pltpu.PrefetchScalarGridSpec(
            num_scalar_prefetch=2, grid=(4, len(steps)),
            in_specs=[pl.BlockSpec((tq, 128), lambda p, s, qt, kt: (qt[s], p))]
            + [pl.BlockSpec((tk, 128), functools.partial(lambda p, s, qt, kt, kb: (kb(s, qt, kt), 4 + p), kb=kb))
               for kb in kbs]
            + [pl.BlockSpec((tk, 128), functools.partial(lambda p, s, qt, kt, kb: (kb(s, qt, kt), 8 + p), kb=kb))
               for kb in kbs]
            + [pl.BlockSpec((None, tq, 128), lambda p, s, qt, kt: (p, qt[s], 0))]
            + [pl.BlockSpec((None, tk, 128), functools.partial(lambda p, s, qt, kt, kb: (p, kb(s, qt, kt), 0), kb=kb))
               for kb in kbs],
            out_specs=[pl.BlockSpec((tq, 128), lambda p, s, qt, kt: (qt[s], p)),
                       pl.BlockSpec((tq, 128), lambda p, s, qt, kt: (qt[s], p))],
            scratch_shapes=[pltpu.VMEM((tq, 128), F32)] * 3),
        compiler_params=_params(("parallel", "arbitrary"), VMEM_BIG),
    )(q_tab, k_tab, z, *([z] * (2 * grp)), aug_q, *([aug_k] * grp))


def _attn_bwd(z, aug_q, aug_k, o, do):
    t = z.shape[0]
    tq = tk = ROW_TILE
    nq = t // tq
    grp = ATTN_Q_GROUP
    steps = [(qa, ki) for ki in range(nq) for qa in range(ki, nq, grp)]
    q_tab = jnp.array([qa for qa, _ in steps], jnp.int32)
    k_tab = jnp.array([ki for _, ki in steps], jnp.int32)
    tn = (((0,), (0,)), ((), ()))

    def body(qt_ref, kt_ref, *refs):
        q_refs, (k_ref, v_ref) = refs[:grp], refs[grp:grp + 2]
        aq_refs, ak_ref = refs[grp + 2:2 * grp + 2], refs[2 * grp + 2]
        o_refs, do_refs = refs[2 * grp + 3:3 * grp + 3], refs[3 * grp + 3:4 * grp + 3]
        dq_ref, dk_ref, dv_ref, dfk_ref, dfq_ref = refs[4 * grp + 3:]
        step = pl.program_id(1)
        qa, ki = qt_ref[step], kt_ref[step]

        def rows(j):
            return pl.ds(pl.multiple_of((qa + j) * tq, tq), tq)

        for j in range(grp):
            @pl.when((ki == 0) & (qa + j < nq))
            def _(j=j):
                dq_ref[rows(j), :] = jnp.zeros((tq, 128), F32)
                dfq_ref[rows(j), :] = jnp.zeros((tq, 128), F32)

        @pl.when(qa == ki)
        def _():
            dk_ref[...] = jnp.zeros_like(dk_ref)
            dv_ref[...] = jnp.zeros_like(dv_ref)
            dfk_ref[...] = jnp.zeros_like(dfk_ref)

        def sweep(nb, diagonal):
            first = _lane_halves()
            lane = lax.broadcasted_iota(jnp.int32, (1, 128), 1)
            scale = HEAD_DIM ** -0.5
            halves = (first, jnp.logical_not(first))
            spare = (HEAD_DIM, 0)
            k = k_ref[...].astype(BF16)
            v = v_ref[...].astype(BF16)
            ak = ak_ref[...]
            k_bias = [jnp.where(lanes, k, ak) for lanes in halves]
            k_ones = [jnp.where(lanes, k, _lane_one(a)) for lanes, a in zip(halves, spare)]
            v_ones = [jnp.where(lanes, v, ((lane >= a) & (lane < a + 3)).astype(BF16)) for lanes, a in zip(halves, spare)]
            chains = [(j, hh) for j in range(nb) for hh in range(2)]
            q16, do16, dos = [], [], []
            for j in range(nb):
                q16.append((q_refs[j][...] * scale).astype(BF16))
                do_ = do_refs[j][...]
                do16.append(do_.astype(BF16))
                od = o_refs[j][...] * do_
                for lanes, a in zip(halves, spare):
                    d_hi, d_mid, d_lo = _split3(jnp.sum(jnp.where(lanes, od, 0.0), axis=1, keepdims=True))
                    minus_delta = jnp.where(lane == a, -d_hi, jnp.where(lane == a + 1, -d_mid,
                                            jnp.where(lane == a + 2, -d_lo, jnp.zeros((), BF16))))
                    dos.append(jnp.where(lanes, do16[j], minus_delta))
            s = [_dot_nt(jnp.where(halves[hh], q16[j], aq_refs[j][...]), k_bias[hh]) for j, hh in chains]
            dp = [_dot_nt(dos[2 * j + hh], v_ones[hh]) for j, hh in chains]
            p = [jnp.exp(s_c) for s_c in s]
            if diagonal:
                p = [jnp.where(_causal_mask(tq, tk), p_c, 0.0) if j == 0 else p_c for p_c, (j, _) in zip(p, chains)]
            ds16 = [(p_c * dp_c).astype(BF16) for p_c, dp_c in zip(p, dp)]
            dv, dk = [None, None], [None, None]
            for c, (j, hh) in enumerate(chains):
                lanes = halves[hh]
                dv_c = lax.dot_general(jnp.where(lanes, do16[j], jnp.zeros((), BF16)), p[c].astype(BF16), tn,
                                       preferred_element_type=F32)
                dk_c = lax.dot_general(jnp.where(lanes, q16[j], _lane_one(spare[hh])), ds16[c], tn,
                                       preferred_element_type=F32)
                dv[hh] = dv_c if dv[hh] is None else dv[hh] + dv_c
                dk[hh] = dk_c if dk[hh] is None else dk[hh] + dk_c
            dv = [x.T for x in dv]
            dk = [x.T for x in dk]
            for j in range(nb):
                dq0, dq1 = [_dot(ds16[2 * j + hh], k_ones[hh]) for hh in range(2)]
                dq_ref[rows(j), :] += jnp.where(first, dq0, dq1) * scale
                dfq_ref[rows(j), :] += jnp.where(first, dq0[:, HEAD_DIM:HEAD_DIM + 1], dq1[:, 0:1])
            dk_ref[...] += jnp.where(first, dk[0], dk[1])
            dfk_ref[...] += jnp.where(first, dk[0][:, HEAD_DIM:HEAD_DIM + 1], dk[1][:, 0:1])
            dv_ref[...] += dv[0] + dv[1]

        for nb in range(1, grp + 1):
            exists = (qa + grp <= nq) if nb == grp else (qa + nb == nq)
            for diagonal in (False, True):
                @pl.when(exists & ((qa == ki) == diagonal))
                def _(nb=nb, diagonal=diagonal):
                    sweep(nb, diagonal)

    def qblock(j):
        return lambda s, qt: jnp.minimum(qt[s] + j, nq - 1)

    qbs = [qblock(j) for j in range(grp)]
    qcol = [functools.partial(lambda p, s, qt, kt, qb: (qb(s, qt), p), qb=qb) for qb in qbs]
    krow = lambda p, s, qt, kt: (kt[s], p)
    return pl.pallas_call(
        body, name="attention_bwd",
        out_shape=[jax.ShapeDtypeStruct((t, ATTN_W), F32)] * 5,
        grid_spec=pltpu.PrefetchScalarGridSpec(
            num_scalar_prefetch=2, grid=(4, len(steps)),
            in_specs=[pl.BlockSpec((tq, 128), m) for m in qcol]
            + [pl.BlockSpec((tk, 128), lambda p, s, qt, kt: (kt[s], 4 + p)),
               pl.BlockSpec((tk, 128), lambda p, s, qt, kt: (kt[s], 8 + p))]
            + [pl.BlockSpec((None, tq, 128), functools.partial(lambda p, s, qt, kt, qb: (p, qb(s, qt), 0), qb=qb))
               for qb in qbs]
            + [pl.BlockSpec((None, tk, 128), lambda p, s, qt, kt: (p, kt[s], 0))]
            + [pl.BlockSpec((tq, 128), m) for m in qcol] + [pl.BlockSpec((tq, 128), m) for m in qcol],
            out_specs=[pl.BlockSpec((t, 128), lambda p, s, qt, kt: (0, p)),
                       pl.BlockSpec((tk, 128), krow), pl.BlockSpec((tk, 128), krow), pl.BlockSpec((tk, 128), krow),
                       pl.BlockSpec((t, 128), lambda p, s, qt, kt: (0, p))]),
        compiler_params=_params(("parallel", "arbitrary"), VMEM_BIG),
    )(q_tab, k_tab, *([z] * grp), z, z, *([aug_q] * grp), aug_k, *([o] * grp), *([do] * grp))


def _shifted(prev_rows, x, shift):
    tm = x.shape[0]
    return pltpu.roll(jnp.concatenate([prev_rows, x], axis=0), shift, 0)[8:8 + tm]


def _ahead(x, next_rows, shift):
    tm = x.shape[0]
    return pltpu.roll(jnp.concatenate([x, next_rows], axis=0), tm + 8 - shift, 0)[0:tm]


def _conv_col0(z):
    return (z.shape[1] - F_PAD - 3 * CONV_W) // CONV_W


def _conv_specs(tm, c0):
    cols = (c0, c0 + 1, c0 + 2)
    tiles = [pl.BlockSpec((tm, CONV_W), functools.partial(lambda i, c: (i, c), c=c)) for c in cols]
    halos = [pl.BlockSpec((8, CONV_W), functools.partial(lambda i, c: (jnp.maximum(i * (tm // 8) - 1, 0), c), c=c))
             for c in cols]
    return tiles, halos


def _conv_gate(z, conv_w):
    t = z.shape[0]
    tm = ROW_TILE
    nt = t // tm

    def body(cb_ref, cc_ref, ci_ref, hc_ref, hi_ref, w_ref, g_ref, gt_ref):
        i = pl.program_id(0)
        cc = cc_ref[...] * ci_ref[...]
        prev = jnp.where(i > 0, hc_ref[...] * hi_ref[...], 0.0)
        conv = w_ref[0:1, :] * _shifted(prev, cc, 2) + w_ref[1:2, :] * _shifted(prev, cc, 1) + w_ref[2:3, :] * cc
        g = cb_ref[...] * conv
        g_ref[...] = g.astype(BF16)
        gt_ref[...] = g.T.astype(BF16)

    (cb, cc, ci), (_, hc, hi) = _conv_specs(tm, _conv_col0(z))
    return pl.pallas_call(
        body, name="conv_gate_fwd",
        out_shape=[jax.ShapeDtypeStruct((t, CONV_W), BF16), jax.ShapeDtypeStruct((CONV_W, t), BF16)],
        grid=(nt,),
        in_specs=[cb, cc, ci, hc, hi, pl.BlockSpec((8, CONV_W), lambda i: (0, 0))],
        out_specs=[pl.BlockSpec((tm, CONV_W), lambda i: (i, 0)), pl.BlockSpec((CONV_W, tm), lambda i: (0, i))],
        compiler_params=_params(("parallel",)),
    )(z, z, z, z, z, conv_w)


def _conv_bwd(z, dg, conv_w):
    t = z.shape[0]
    tm = ROW_TILE
    nt = t // tm

    def body(cb_ref, cc_ref, ci_ref, hc_ref, hi_ref, dg_ref, ncb_ref, ndg_ref, w_ref, dz_ref, dw_ref):
        i = pl.program_id(0)

        @pl.when(i == 0)
        def _():
            dw_ref[...] = jnp.zeros_like(dw_ref)

        cb, c_c, c_in = cb_ref[...], cc_ref[...], ci_ref[...]
        cc = c_c * c_in
        prev = jnp.where(i > 0, hc_ref[...] * hi_ref[...], 0.0)
        cc1, cc2 = _shifted(prev, cc, 1), _shifted(prev, cc, 2)
        w0, w1, w2 = w_ref[0:1, :], w_ref[1:2, :], w_ref[2:3, :]
        conv = w0 * cc2 + w1 * cc1 + w2 * cc
        dgv = dg_ref[...]
        dconv = dgv * cb
        nxt = jnp.where(i < nt - 1, ndg_ref[...] * ncb_ref[...], 0.0)
        dcc = w2 * dconv + w1 * _ahead(dconv, nxt, 1) + w0 * _ahead(dconv, nxt, 2)
        dz_ref[:, 0:CONV_W] = (dgv * conv).astype(BF16)
        dz_ref[:, CONV_W:2 * CONV_W] = (dcc * c_in).astype(BF16)
        dz_ref[:, 2 * CONV_W:] = (dcc * c_c).astype(BF16)
        dw_ref[0:1, :] += jnp.sum(dconv * cc2, axis=0, keepdims=True)
        dw_ref[1:2, :] += jnp.sum(dconv * cc1, axis=0, keepdims=True)
        dw_ref[2:3, :] += jnp.sum(dconv * cc, axis=0, keepdims=True)

    c0 = _conv_col0(z)
    (cb, cc, ci), (_, hc, hi) = _conv_specs(tm, c0)
    nxt = lambda i, c: (jnp.minimum((i + 1) * (tm // 8), t // 8 - 1), c)
    return pl.pallas_call(
        body, name="conv_gate_bwd",
        out_shape=[jax.ShapeDtypeStruct((t, 3 * CONV_W), BF16), jax.ShapeDtypeStruct((8, CONV_W), F32)],
        grid=(nt,),
        in_specs=[cb, cc, ci, hc, hi, pl.BlockSpec((tm, CONV_W), lambda i: (i, 0)),
                  pl.BlockSpec((8, CONV_W), lambda i: nxt(i, c0)), pl.BlockSpec((8, CONV_W), lambda i: nxt(i, 0)),
                  pl.BlockSpec((8, CONV_W), lambda i: (0, 0))],
        out_specs=[pl.BlockSpec((tm, 3 * CONV_W), lambda i: (i, 0)), pl.BlockSpec((8, CONV_W), lambda i: (0, 0))],
        compiler_params=_params(("arbitrary",)),
    )(z, z, z, z, z, dg, z, dg, conv_w)


def _branch_mix(z, o, g, w_ab, w_cb, d):
    t = z.shape[0]
    tm = ROW_TILE
    ga_col = 0

    def body(o_ref, g_ref, ga_ref, gc_ref, wa_ref, wc_ref, mp_ref, mpt_ref, ot_ref):
        o_ = o_ref[...]
        ya = _dot(o_.astype(BF16), wa_ref[...])
        yc = _dot(g_ref[...], wc_ref[...])
        mp = _sigmoid(ga_ref[...]) * ya + _sigmoid(gc_ref[...]) * yc
        mp_ref[...] = mp.astype(BF16)
        mpt_ref[...] = mp.T.astype(BF16)
        ot_ref[...] = o_.T.astype(BF16)

    return pl.pallas_call(
        body, name="branch_mix_fwd",
        out_shape=[jax.ShapeDtypeStruct((t, d), BF16), jax.ShapeDtypeStruct((d, t), BF16),
                   jax.ShapeDtypeStruct((ATTN_W, t), BF16)],
        grid=(t // tm,),
        in_specs=[pl.BlockSpec((tm, ATTN_W), lambda i: (i, 0)), pl.BlockSpec((tm, CONV_W), lambda i: (i, 0)),
                  pl.BlockSpec((tm, d), lambda i: (i, ga_col)), pl.BlockSpec((tm, d), lambda i: (i, ga_col + 1)),
                  pl.BlockSpec((ATTN_W, d), lambda i: (0, 0)), pl.BlockSpec((CONV_W, d), lambda i: (0, 0))],
        out_specs=[pl.BlockSpec((tm, d), lambda i: (i, 0)), pl.BlockSpec((d, tm), lambda i: (0, i)),
                   pl.BlockSpec((ATTN_W, tm), lambda i: (0, i))],
        compiler_params=_params(("parallel",), VMEM_BIG),
    )(o, g, z, z, w_ab, w_cb)


def _branch_bwd(z, o, g, dmixed, w_out, w_ab, w_cb, d):
    t = z.shape[0]
    tm = ROW_TILE // 2
    ga_col = 0

    def body(dm_ref, o_ref, g_ref, ga_ref, gc_ref, wo_ref, wa_ref, wc_ref, dya_ref, dyc_ref, dgt_ref, do_ref, dg_ref):
        dmp = _dot_nt(dm_ref[...], wo_ref[...])
        ya = _dot(o_ref[...].astype(BF16), wa_ref[...])
        yc = _dot(g_ref[...], wc_ref[...])
        sa, sc = _sigmoid(ga_ref[...]), _sigmoid(gc_ref[...])
        dya = (dmp * sa).astype(BF16)
        dyc = (dmp * sc).astype(BF16)
        dya_ref[...] = dya
        dyc_ref[...] = dyc
        dgt_ref[:, :d] = (dmp * ya * sa * (1.0 - sa)).astype(BF16)
        dgt_ref[:, d:] = (dmp * yc * sc * (1.0 - sc)).astype(BF16)
        do_ref[...] = _dot_nt(dya, wa_ref[...])
        dg_ref[...] = _dot_nt(dyc, wc_ref[...])

    row = lambda i: (i, 0)
    fixed = lambda i: (0, 0)
    return pl.pallas_call(
        body, name="branch_mix_bwd",
        out_shape=[jax.ShapeDtypeStruct((t, d), BF16), jax.ShapeDtypeStruct((t, d), BF16),
                   jax.ShapeDtypeStruct((t, 2 * d), BF16), jax.ShapeDtypeStruct((t, ATTN_W), F32),
                   jax.ShapeDtypeStruct((t, CONV_W), F32)],
        grid=(t // tm,),
        in_specs=[pl.BlockSpec((tm, d), row), pl.BlockSpec((tm, ATTN_W), row), pl.BlockSpec((tm, CONV_W), row),
                  pl.BlockSpec((tm, d), lambda i: (i, ga_col)), pl.BlockSpec((tm, d), lambda i: (i, ga_col + 1)),
                  pl.BlockSpec((d, d), fixed), pl.BlockSpec((ATTN_W, d), fixed), pl.BlockSpec((CONV_W, d), fixed)],
        out_specs=[pl.BlockSpec((tm, d), row), pl.BlockSpec((tm, d), row), pl.BlockSpec((tm, 2 * d), row),
                   pl.BlockSpec((tm, ATTN_W), row), pl.BlockSpec((tm, CONV_W), row)],
        compiler_params=_params(("parallel",), VMEM_BIG),
    )(dmixed, o, g, z, z, w_out, w_ab, w_cb)


def _loss_norm_bwd(h, target, f, g_post, alpha):
    t, d = h.shape
    tm = ROW_TILE

    def body(h_ref, t_hbm, f_ref, g_ref, dh_ref, df_ref, dg_ref, loss_ref, t_buf, sems):
        i = pl.program_id(0)

        @pl.when(i == 0)
        def _():
            loss_ref[...] = jnp.zeros_like(loss_ref)
            dg_ref[...] = jnp.zeros_like(dg_ref)

        target = _read_token_rows(t_hbm, t_buf, sems, i, t // tm)
        row = i * tm + lax.broadcasted_iota(jnp.int32, (tm, 1), 0)
        err = jnp.where(row >= N_FRONT, h_ref[...] - target[...], 0.0)
        dy = err * (1.0 / d)
        dh_ref[...] = dy
        per_row = jnp.sum(err * err, axis=1, keepdims=True) * (1.0 / d)
        loss_ref[...] += 0.5 * jnp.sum(per_row, axis=0, keepdims=True)
        dx, dg = _rms_bwd(f_ref[...], g_ref[...], dy)
        df_ref[...] = (alpha * dx).astype(BF16)
        dg_ref[...] += alpha * dg

    row = pl.BlockSpec((tm, d), lambda i: (i, 0))
    vec = pl.BlockSpec((1, d), lambda i: (0, 0))
    return pl.pallas_call(
        body, name="loss_and_post_norm_bwd",
        out_shape=[jax.ShapeDtypeStruct((t, d), F32), jax.ShapeDtypeStruct((t, d), BF16),
                   jax.ShapeDtypeStruct((1, d), F32), jax.ShapeDtypeStruct((1, 128), F32)],
        grid=(t // tm,),
        in_specs=[row, ANY, row, vec],
        out_specs=[row, row, vec, pl.BlockSpec((1, 128), lambda i: (0, 0))],
        scratch_shapes=[pltpu.VMEM((2, tm, d), F32), pltpu.SemaphoreType.DMA((2,))],
        compiler_params=_params(("arbitrary",)),
    )(h, target, f, g_post)


def _ffn_bwd_mid(name, df, w_out, ab):
    t, d = df.shape
    cw = ab.shape[1] // 4
    tm = ROW_TILE

    def body(df_ref, w_ref, ab_ref, o_ref):
        ds = _dot_nt(df_ref[...], w_ref[...])
        a = ab_ref[:, :cw].astype(F32)
        b = ab_ref[:, cw:].astype(F32)
        sg = _sigmoid(a)
        o_ref[:, :cw] = (ds * b * (sg * (1.0 + a * (1.0 - sg)))).astype(BF16)
        o_ref[:, cw:] = (ds * (a * sg)).astype(BF16)

    return pl.pallas_call(
        body, name=name, out_shape=jax.ShapeDtypeStruct((t, 4 * cw), BF16),
        grid=(2, t // tm),
        in_specs=[pl.BlockSpec((tm, d), lambda j, i: (i, 0)), pl.BlockSpec((cw, d), lambda j, i: (j, 0)),
                  pl.BlockSpec((tm, 2 * cw), lambda j, i: (i, j))],
        out_specs=pl.BlockSpec((tm, 2 * cw), lambda j, i: (i, j)),
        compiler_params=_params(("parallel", "parallel"), VMEM_BIG),
    )(df, w_out, ab)


def _next_post_norm_bwd(dh, post_refs, alpha, first):
    x_ref, g_ref, dx_ref, dg_ref = post_refs

    @pl.when(first)
    def _():
        dg_ref[...] = jnp.zeros_like(dg_ref)

    dx, dg = _rms_bwd(x_ref[...], g_ref[...], dh)
    dx_ref[...] = (alpha * dx).astype(BF16)
    dg_ref[...] += alpha * dg


def _mm_nt_norm_bwd(name, dy, w, h, g, dh_in, post=None):
    t, kdim = dy.shape
    d = h.shape[1]
    tm = ROW_TILE // 2

    def body(dy_ref, w_ref, h_ref, g_ref, dhi_ref, *rest):
        dh_ref, dg_ref = rest[-4:-2] if post else rest
        first = pl.program_id(0) == 0

        @pl.when(first)
        def _():
            dg_ref[...] = jnp.zeros_like(dg_ref)

        cw = w_ref.shape[2]
        dn = _dot_nt(dy_ref[:, 0:cw], w_ref[_slot_of(0)])
        for k in range(1, 4):
            dn += _dot_nt(dy_ref[:, k * cw:(k + 1) * cw], w_ref[_slot_of(k)])
        dx, dg = _rms_bwd(h_ref[...], g_ref[...], dn)
        dh = dhi_ref[...] + dx
        dh_ref[...] = dh
        dg_ref[...] += dg
        if post:
            _next_post_norm_bwd(dh, rest[0:2] + rest[-2:], post[2], first)

    row = pl.BlockSpec((tm, d), lambda i: (i, 0))
    vec = pl.BlockSpec((1, d), lambda i: (0, 0))
    out_shape = [jax.ShapeDtypeStruct((t, d), F32), jax.ShapeDtypeStruct((1, d), F32)]
    if post:
        out_shape += [jax.ShapeDtypeStruct((t, d), BF16), jax.ShapeDtypeStruct((1, d), F32)]
    return pl.pallas_call(
        body, name=name, out_shape=out_shape, grid=(t // tm,),
        in_specs=[pl.BlockSpec((tm, kdim), lambda i: (i, 0)), pl.BlockSpec(w.shape, lambda i: (0,) * w.ndim),
                  row, vec, row] + ([row, vec] if post else []),
        out_specs=[row, vec] + ([row, vec] if post else []),
        compiler_params=_params(("arbitrary",), VMEM_BIG),
    )(dy, w, h, g, dh_in, *(post[:2] if post else ()))


def _gate_bwd(dfq, dfk, z, b_pad, f_col):
    t = z.shape[0]
    tm = ROW_TILE
    nt = t // tm

    def body(dq_ref, dk_ref, z_ref, b_ref, dz_ref, db_ref, carry_ref):
        i = pl.program_id(0)

        @pl.when(i == 0)
        def _():
            carry_ref[...] = jnp.zeros_like(carry_ref)
            db_ref[...] = jnp.zeros_like(db_ref)

        pick = (lax.broadcasted_iota(jnp.int32, (ATTN_W, 128), 0)
                == HEAD_DIM * lax.broadcasted_iota(jnp.int32, (ATTN_W, 128), 1)).astype(F32)
        d_heads = jnp.dot(dq_ref[...] - dk_ref[...], pick, preferred_element_type=F32,
                          precision=lax.Precision.HIGHEST)
        tri = (lax.broadcasted_iota(jnp.int32, (tm, tm), 0) <= lax.broadcasted_iota(jnp.int32, (tm, tm), 1))
        tail = jnp.dot(tri.astype(F32), d_heads, preferred_element_type=F32, precision=lax.Precision.HIGHEST)
        tail = tail + carry_ref[0:1, :]
        carry_ref[...] = jnp.broadcast_to(tail[0:1, :], carry_ref.shape)
        row = (nt - 1 - i) * tm + lax.broadcasted_iota(jnp.int32, (tm, 1), 0)
        dlogit = jnp.where(row >= ROW_PAD, tail * _sigmoid(-(z_ref[...] + b_ref[...])), 0.0)
        dz_ref[...] = jnp.zeros_like(dz_ref)
        dz_ref[:, 0:128] = dlogit.astype(BF16)
        db_ref[...] += jnp.sum(dlogit, axis=0, keepdims=True)

    rev = lambda i: (nt - 1 - i, 0)
    return pl.pallas_call(
        body, name="forget_gate_bwd",
        out_shape=[jax.ShapeDtypeStruct((t, F_PAD), BF16), jax.ShapeDtypeStruct((1, 128), F32)],
        grid=(nt,),
        in_specs=[pl.BlockSpec((tm, ATTN_W), rev), pl.BlockSpec((tm, ATTN_W), rev),
                  pl.BlockSpec((tm, 128), lambda i: (nt - 1 - i, f_col // 128)),
                  pl.BlockSpec((1, 128), lambda i: (0, 0))],
        out_specs=[pl.BlockSpec((tm, F_PAD), rev), pl.BlockSpec((1, 128), lambda i: (0, 0))],
        scratch_shapes=[pltpu.VMEM((8, 128), F32)],
        compiler_params=_params(("arbitrary",)),
    )(dfq, dfk, z, b_pad)


def _ffn_fwd(tag, n, w_in4, w_out, h, g_post, g_next):
    ab, s, s_t = _ffn_in(f"{tag}_in_fwd", n, w_in4)
    outs = _mm_resid_norm(f"{tag}_out_fwd", s, w_out, h, g_post, 0.5, g_next)
    return ab, s_t, outs


def _ffn_bwd_weights(tag, df, ab, s_t, n_t, w_in4, w_out):
    d, cw = w_in4.shape[1], w_in4.shape[2]
    t = df.shape[0]
    dw_out = _weight_grad(f"{tag}_dw_out", s_t, df, d, out_rows=cw // 2)
    dab = _ffn_bwd_mid(f"{tag}_mid_bwd", df, w_out, ab)
    bk = _k_tile(t)
    dw_in = _matmul(
        f"{tag}_dw_in", n_t, dab, jax.ShapeDtypeStruct((4, d, cw), F32), (1, 4, t // bk),
        pl.BlockSpec((d, bk), lambda a, b, k: (0, k)), pl.BlockSpec((bk, cw), lambda a, b, k: (k, b)),
        pl.BlockSpec((None, d, cw), lambda a, b, k: (_slot_of(b), 0, 0)), vmem=VMEM_BIG)
    return dab, dw_in, dw_out


LOSS_ROW = 12


def _pack_small(meta, conv, gains, b_forget, loss=None):
    d = gains[0].shape[1]
    rows = [meta.reshape(4, d), jnp.pad(conv.reshape(1, 3 * 128), ((0, 0), (0, d - 3 * 128)))]
    rows += list(gains) + [jnp.pad(b_forget, ((0, 0), (0, d - HEADS)))]
    last = jnp.zeros((4, d), F32)
    if loss is not None:
        last = jnp.pad(loss.reshape(1, 1), ((0, 3), (0, d - 1)))
    return jnp.concatenate(rows + [last], axis=0)


def _unpack_small(block):
    d = block.shape[1]
    meta = block[0:4].reshape(N_META, d // 4)
    conv = block[4, :3 * 128].reshape(1, 3, 128)
    gains = [block[5 + i:6 + i] for i in range(6)]
    return meta, conv, gains, block[11:12, :HEADS]


def kernel(x, meta_tokens, w_in, b_forget, conv_w, w_attn_branch, w_conv_branch, w_out, g_ffn1_pre, g_ffn1_post, w_ffn1_in, w_ffn1_out, g_mix_pre, g_mix_post, g_ffn2_pre, g_ffn2_post, w_ffn2_in, w_ffn2_out, loss_target, m_meta_tokens, m_w_in, m_b_forget, m_conv_w, m_w_attn_branch, m_w_conv_branch, m_w_out, m_g_ffn1_pre, m_g_ffn1_post, m_w_ffn1_in, m_w_ffn1_out, m_g_mix_pre, m_g_mix_post, m_g_ffn2_pre, m_g_ffn2_post, m_w_ffn2_in, m_w_ffn2_out, v_meta_tokens, v_w_in, v_b_forget, v_conv_w, v_w_attn_branch, v_w_conv_branch, v_w_out, v_g_ffn1_pre, v_g_ffn1_post, v_w_ffn1_in, v_w_ffn1_out, v_g_mix_pre, v_g_mix_post, v_g_ffn2_pre, v_g_ffn2_post, v_w_ffn2_in, v_w_ffn2_out):
    seq, d = x.shape[1], x.shape[2]
    t = seq + N_FRONT
    f_lo = 3 * ATTN_W
    c_arr = lax.axis_index("c").astype(jnp.int32).reshape(1)

    cs = w_in.shape[2]
    cs_pad = -(-cs // 64) * 64

    def w_in_rows(a):
        return jnp.pad(jnp.transpose(a[0]), ((0, cs_pad - cs), (0, 0)))

    big = [w_in_rows(w_in), w_attn_branch[0], w_conv_branch[0], w_out[0], w_ffn1_in[0], w_ffn1_out[0], w_ffn2_in[0],
           w_ffn2_out[0]]
    small_gather = jnp.concatenate(
        [meta_tokens.reshape(4, d), jnp.pad(conv_w.reshape(1, 3 * 128), ((0, 0), (0, d - 3 * 128))),
         jnp.zeros((11, d), F32)], axis=0)
    w_f1_in4, small4 = _all_gather([big[4].astype(BF16), small_gather])
    (second, rest), small4 = lax.optimization_barrier(
        (([big[5].astype(BF16)], [big[i].astype(BF16) for i in (0, 1, 2, 3, 6, 7)]), small4))
    second_gathered = _all_gather_async("all_gather_ffn1_out", second, 5)
    rest_gathered = _all_gather_async("all_gather_rest", rest, 1)
    meta_full = jnp.transpose(small4[:, 0:4].reshape(4, N_META, d // 4), (1, 0, 2)).reshape(N_META, d)
    conv_full = jnp.transpose(small4[:, 4, :3 * 128].reshape(4, 3, 128), (1, 0, 2)).reshape(3, CONV_W)
    conv_pad = jnp.pad(conv_full, ((0, 5), (0, 0)))
    b_pad = jnp.pad(b_forget, ((0, 0), (0, 128 - HEADS)))

    h0, n1, n1_t = _embed_norm(x[0], meta_full, g_ffn1_pre)
    ab1, s1, s1_t = _ffn_in("ffn1_in_fwd", n1, w_f1_in4)
    w_f1_out = second_gathered(s1, [0])[0].reshape(-1, d)
    f1, h1, u, u_t = _mm_resid_norm("ffn1_out_fwd", s1, w_f1_out, h0, g_ffn1_post, 0.5, g_mix_pre)

    w_in4, w_ab4, w_cb4, w_out4, w_f2_in4, w_f2_out4 = rest_gathered(u, range(6))
    w_in_t = w_in4[:, :cs].reshape(4 * cs, d)
    g_lo = f_lo + HEADS + 3 * CONV_W
    w_in_pad = jnp.concatenate(
        [w_in_t[:f_lo], w_in_t[g_lo:], w_in_t[f_lo + HEADS:g_lo], w_in_t[f_lo:f_lo + HEADS],
         jnp.zeros((F_PAD - HEADS, d), BF16)], axis=0)
    w_ab = jnp.transpose(w_ab4, (1, 0, 2)).reshape(ATTN_W, d)
    w_cb = jnp.transpose(w_cb4, (1, 0, 2)).reshape(CONV_W, d)
    w_out_full = w_out4.reshape(d, d)
    w_f2_out = w_f2_out4.reshape(-1, d)
    qkv, z = _in_proj(u, w_in_pad)
    f_col = z.shape[1] - F_PAD
    f_cum = _gate_prep(z, b_pad, f_col)
    f_heads = f_cum[:, :HEADS]
    o, lse = _attn_fwd(qkv, *_attn_bias_operands(f_heads))
    g, g_t = _conv_gate(z, conv_pad)
    mp, mp_t, o_t = _branch_mix(z, o, g, w_ab, w_cb, d)
    mixed, h2, n2, n2_t = _mm_resid_norm("mix_out_fwd", mp, w_out_full, h1, g_mix_post, 1.0, g_ffn2_pre)
    ab2, s2_t, (f2, h3) = _ffn_fwd("ffn2", n2, w_f2_in4, w_f2_out, h2, g_ffn2_post, None)
    dh3, df2, dg_f2_post, loss_part = _loss_norm_bwd(h3, loss_target[0], f2, g_ffn2_post, 0.5)

    reduced = {}

    def reduce_scatter(label, tags, slots, sequencer_id, hold=None, got=None, after=None):
        if got is None:
            got = _pair_send_halves(f"grad_pair_exchange_{label}", slots)
        else:
            got, _ = lax.optimization_barrier((got, after))
        sums = [_pair_add(tag, s, a, c_arr, F32 if tag == "small" else BF16) for tag, s, a in zip(tags, slots, got)]
        sums, hold = lax.optimization_barrier((sums, hold))
        if sequencer_id is None:
            arrived = _chip_scatter(f"grad_chip_scatter_{label}", sums)
        else:
            arrived = _chip_scatter_async(f"grad_chip_scatter_{label}", sums, sequencer_id)
        mine = [_chip_add(tag, a) for tag, a in zip(tags, arrived)]
        reduced.update(zip(tags, zip(mine, _pair_swap(f"grad_pair_swap_{label}", mine))))
        return hold

    dab2, dw_f2_in, dw_f2_out = _ffn_bwd_weights("ffn2", df2, ab2, s2_t, n2_t, w_f2_in4, w_f2_out)
    ffn2_slots = [dw_f2_in, dw_f2_out.reshape(4, -1, d)]
    ffn2_got = _pair_send_halves_async("grad_pair_exchange_ffn2", ffn2_slots, 6)
    dh2, dg_f2_pre, dmixed, dg_mix_post = _mm_nt_norm_bwd(
        "ffn2_in_bwd", dab2, w_f2_in4, h2, g_ffn2_pre, dh3, post=(mixed, g_mix_post, 1.0))
    reduce_scatter("ffn2", ["w_ffn2_in", "w_ffn2_out"], ffn2_slots, 2, got=ffn2_got, after=dh2)
    dw_out = _weight_grad("mix_dw_out", mp_t, dmixed, d)
    dya, dyc, dgates, do, dgconv = _branch_bwd(z, o, g, dmixed, w_out_full, w_ab, w_cb, d)
    dw_ab = _weight_grad("mix_dw_attn_branch", o_t, dya, d)
    dw_cb = _weight_grad("mix_dw_conv_branch", g_t, dyc, d)
    dz_conv, dconv_w = _conv_bwd(z, dgconv, conv_pad)
    front = lax.broadcasted_iota(jnp.int32, (t, 1), 0) < ROW_PAD
    lse_heads = jnp.where(front, 1e9, lse[:, ::HEAD_DIM])
    dq, dk, dv, dfk, dfq = _attn_bwd(qkv, *_attn_bias_operands(f_heads, lse_heads), o, do)
    dz_f, db_forget = _gate_bwd(dfq, dfk, z, b_pad, f_col)
    dz_pieces = {"q": dq, "k": dk, "v": dv, "gates": dgates, "conv": dz_conv, "f": dz_f}
    dh1, dg_mix_pre, df1, dg_f1_post = _mix_in_bwd(
        list(dz_pieces.values()), w_in_pad, h1, g_mix_pre, dh2, (f1, g_ffn1_post, 0.5))
    dw_t = {name: _weight_grad_t(f"mix_dw_in_{name}", u_t, piece) for name, piece in dz_pieces.items()}
    dw_in_t = jnp.concatenate(
        [dw_t["q"], dw_t["k"], dw_t["v"], dw_t["f"][:HEADS], dw_t["conv"], dw_t["gates"]], axis=0)
    mix_slots = [jnp.pad(dw_in_t.reshape(4, cs, d), ((0, 0), (0, cs_pad - cs), (0, 0))),
                 jnp.transpose(dw_ab.reshape(ATTN_W, 4, d // 4), (1, 0, 2)),
                 jnp.transpose(dw_cb.reshape(CONV_W, 4, d // 4), (1, 0, 2)),
                 dw_out.reshape(4, d // 4, d)]
    mix_got = _pair_send_halves_async("grad_pair_exchange_mix", mix_slots, 7)
    dab1, dw_f1_in, dw_f1_out = _ffn_bwd_weights("ffn1", df1, ab1, s1_t, n1_t, w_f1_in4, w_f1_out)
    reduce_scatter("mix", ["w_in", "w_attn_branch", "w_conv_branch", "w_out"], mix_slots, 3, got=mix_got,
                   after=dw_f1_out)
    dab1 = reduce_scatter("ffn1", ["w_ffn1_in", "w_ffn1_out"], [dw_f1_in, dw_f1_out.reshape(4, -1, d)], 4, dab1)
    dh0, dg_f1_pre = _mm_nt_norm_bwd("ffn1_in_bwd", dab1, w_f1_in4, h0, g_ffn1_pre, dh1)
    grad_x = dh0[N_FRONT:][None]
    dmeta = dh0[ROW_PAD:N_FRONT]
    small_grad = jnp.stack([
        _pack_small(dmeta[:, j * (d // 4):(j + 1) * (d // 4)], dconv_w[:3, j * 128:(j + 1) * 128],
                    [dg_f1_pre, dg_f1_post, dg_mix_pre, dg_mix_post, dg_f2_pre, dg_f2_post], db_forget[:, :HEADS],
                    loss_part[0, 0])
        for j in range(4)])
    reduce_scatter("small", ["small"], [small_grad], None)
    tags =["w_in", "w_attn_branch", "w_conv_branch", "w_out", "w_ffn1_in", "w_ffn1_out", "w_ffn2_in", "w_ffn2_out", "small"]
    halves = [reduced[tag][0] for tag in tags]
    others = [reduced[tag][1] for tag in tags]

    small = [g_ffn1_pre, g_ffn1_post, g_mix_pre, g_mix_post, g_ffn2_pre, g_ffn2_post]
    small_m = [m_g_ffn1_pre, m_g_ffn1_post, m_g_mix_pre, m_g_mix_post, m_g_ffn2_pre, m_g_ffn2_post]
    small_v = [v_g_ffn1_pre, v_g_ffn1_post, v_g_mix_pre, v_g_mix_post, v_g_ffn2_pre, v_g_ffn2_post]
    ws = big + [_pack_small(meta_tokens, conv_w[0], small, b_forget)]
    ms = [w_in_rows(m_w_in), m_w_attn_branch[0], m_w_conv_branch[0], m_w_out[0], m_w_ffn1_in[0], m_w_ffn1_out[0],
          m_w_ffn2_in[0], m_w_ffn2_out[0], _pack_small(m_meta_tokens, m_conv_w[0], small_m, m_b_forget)]
    vs = [w_in_rows(v_w_in), v_w_attn_branch[0], v_w_conv_branch[0], v_w_out[0], v_w_ffn1_in[0], v_w_ffn1_out[0],
          v_w_ffn2_in[0], v_w_ffn2_out[0], _pack_small(v_meta_tokens, v_conv_w[0], small_v, v_b_forget)]
    updates = [_adamw(tag, w, a, b, m, v, c_arr) for tag, w, a, b, m, v in zip(tags, ws, halves, others, ms, vs)]

    def leaves(big_vals, small_block):
        meta, conv, gains, bf = _unpack_small(small_block)
        w_in_t_, w_ab_, w_cb_, w_out_, f1_in, f1_out, f2_in, f2_out = [b[None] for b in big_vals]
        w_in_ = jnp.transpose(w_in_t_[:, :cs], (0, 2, 1))
        return [meta, w_in_, bf, conv, w_ab_, w_cb_, w_out_, gains[0], gains[1], f1_in, f1_out,
                gains[2], gains[3], gains[4], gains[5], f2_in, f2_out]

    out_g, out_d, out_m, out_v = [leaves([u_[k] for u_ in updates[:8]], updates[8][k]) for k in range(4)]
    loss = updates[8][0][LOSS_ROW, 0]
    return (loss, grad_x, *out_g, *out_d, *out_m, *out_v)
```
